```python
import jax, jax.numpy as jnp
from jax import lax
import numpy as np

D_MODEL = 1024
BATCH = 4
SEQ = 4096
DEPTH = 4
DEC_BATCH = 16
DEC_SEQ = 64
PAST_LEN = 4096

CHUNK = 64
N_MIXERS = 2
N_RET = (DEPTH + 1) // 2
N_GM = DEPTH // 2

RET_HEADS = 4
RET_DK = 256
RET_DV = 512
RET_QK = RET_HEADS * RET_DK
RET_V = RET_HEADS * RET_DV
RET_IN = 2 * RET_QK + 2 * RET_V
RET_CHUNK = CHUNK
ROPE_BASE = 10000.0

GM_FFN = 6 * D_MODEL
GM_HALF = GM_FFN // 2
GM_GROUPS = 4
GM_GDIM = GM_HALF // GM_GROUPS
GM_CHUNK = 128

MOE_GROUPS = 4
MOE_PER_GROUP = 8
MOE_EXPERTS = MOE_GROUPS * MOE_PER_GROUP
MOE_TOPK = 2
MOE_HIDDEN = 512
MOE_BLOCK = 128

EPS = 1e-6

kernel_name = 'hybrid_retention_gmlp_hmoe_stream_step'


def rms_norm(x, gain=None):
    xf = x.astype(jnp.float32)
    y = xf * lax.rsqrt(jnp.mean(xf * xf, axis=-1, keepdims=True) + EPS)
    if gain is not None:
        y = y * gain.astype(jnp.float32)
    return y.astype(x.dtype)


def layer_norm(x, g, b):
    xf = x.astype(jnp.float32)
    mu = jnp.mean(xf, axis=-1, keepdims=True)
    xc = xf - mu
    var = jnp.mean(xc * xc, axis=-1, keepdims=True)
    y = xc * lax.rsqrt(var + EPS) * g.astype(jnp.float32) + b.astype(jnp.float32)
    return y.astype(x.dtype)


def ada_modulation(c, w, b):
    m = jax.nn.silu(c) @ w + b
    return jnp.split(m[:, None, :], 6, axis=-1)


def rotary(x, pos):
    half = x.shape[-1] // 2
    inv = ROPE_BASE ** (-jnp.arange(half, dtype=jnp.float32) / half)
    ang = pos.astype(jnp.float32)[:, None] * inv[None, :]
    cos = jnp.cos(ang)[None, :, None, :]
    sin = jnp.sin(ang)[None, :, None, :]
    x1, x2 = x[..., :half], x[..., half:]
    return jnp.concatenate([x1 * cos - x2 * sin, x1 * sin + x2 * cos], axis=-1)


def retention_core(q, k, v, s0):
    B, L = q.shape[0], q.shape[1]
    cl = min(RET_CHUNK, L)
    n = L // cl
    log_g = jnp.log1p(-jnp.exp2(-5.0 - jnp.arange(RET_HEADS, dtype=jnp.float32)))
    idx = jnp.arange(cl, dtype=jnp.float32)
    diff = idx[:, None] - idx[None, :]
    intra = jnp.where(diff >= 0, jnp.exp(log_g[:, None, None] * jnp.maximum(diff, 0.0)), 0.0)
    q_decay = jnp.exp(log_g[None, :] * (idx[:, None] + 1.0))[None, :, :, None]
    k_decay = jnp.exp(log_g[None, :] * (cl - 1.0 - idx[:, None]))[None, :, :, None]
    chunk_decay = jnp.exp(log_g * cl)[None, :, None, None]

    def to_chunks(a):
        return jnp.moveaxis(a.reshape((B, n, cl) + a.shape[2:]), 1, 0)

    def step(s, qkv):
        qn, kn, vn = qkv
        scores = jnp.einsum('bchd,bshd->bhcs', qn, kn) * intra
        o = jnp.einsum('bhcs,bshe->bche', scores, vn)
        o = o + jnp.einsum('bchd,bhde->bche', qn, s) * q_decay
        s = s * chunk_decay + jnp.einsum('bshd,bshe->bhde', kn * k_decay, vn)
        return s, o

    s_fin, oc = lax.scan(step, s0, (to_chunks(q), to_chunks(k), to_chunks(v)))
    o = jnp.moveaxis(oc, 0, 1).reshape(B, L, RET_HEADS, RET_DV)
    return o, s_fin


def retention_mixer(h, s0, pos0, w_in, w_out):
    B, L, _ = h.shape
    p = h @ w_in
    q, k, v, g = jnp.split(p, [RET_QK, 2 * RET_QK, 2 * RET_QK + RET_V], axis=-1)
    pos = pos0 + jnp.arange(L)
    q = rotary(q.reshape(B, L, RET_HEADS, RET_DK).astype(jnp.float32), pos)
    k = rotary(k.reshape(B, L, RET_HEADS, RET_DK).astype(jnp.float32), pos) * (RET_DK ** -0.5)
    v = v.reshape(B, L, RET_HEADS, RET_DV).astype(jnp.float32)
    o, s = retention_core(q, k, v, s0.astype(jnp.float32))
    o = rms_norm(o).reshape(B, L, RET_V)
    y = (jax.nn.silu(g.astype(jnp.float32)) * o).astype(h.dtype) @ w_out
    return y, s.astype(h.dtype)


def gmlp_mixer(h, w_in, b_in, ln_g, ln_b, w_s, b_s, w_out, b_out):
    B, L, _ = h.shape
    z = jax.nn.gelu(h @ w_in + b_in)
    u, v = jnp.split(z, 2, axis=-1)
    v = layer_norm(v, ln_g, ln_b)
    cl = min(GM_CHUNK, L)
    n = L // cl
    mask = jnp.tril(jnp.ones((cl, cl), dtype=bool))
    ws = jnp.where(mask[None], w_s[:, :cl, :cl], 0.0)
    vg = v.reshape(B, n, cl, GM_GROUPS, GM_GDIM)
    sp = jnp.einsum('gts,bnsgc->bntgc', ws, vg) + b_s[:, :cl].T[None, None, :, :, None]
    y = (u * sp.reshape(B, L, GM_HALF)) @ w_out + b_out
    return y, v


def hier_moe(h, w_rg, b_rg, w_re, b_re, w1, w3, w2):
    B, L, D = h.shape
    x = h.reshape(-1, D)
    T = x.shape[0]
    rows = jnp.arange(T)
    gl = (x @ w_rg).astype(jnp.float32) + b_rg.astype(jnp.float32)
    gp = jax.nn.softmax(gl, axis=-1)
    grp = jnp.argmax(gl, axis=-1)
    g_w = gp[rows, grp]
    el = (x @ w_re).astype(jnp.float32).reshape(T, MOE_GROUPS, MOE_PER_GROUP) + b_re.astype(jnp.float32)
    el = el[rows, grp]
    ev, ei = lax.top_k(el, MOE_TOPK)
    gate = (g_w[:, None] * jax.nn.softmax(ev, axis=-1)).reshape(-1)
    eid = (grp[:, None] * MOE_PER_GROUP + ei).reshape(-1).astype(jnp.int32)
    tok = jnp.repeat(jnp.arange(T, dtype=jnp.int32), MOE_TOPK)
    order = jnp.argsort(eid)
    se = eid[order]
    counts = jnp.zeros((MOE_EXPERTS,), jnp.int32).at[eid].add(1)
    pcounts = (counts + MOE_BLOCK - 1) // MOE_BLOCK * MOE_BLOCK
    starts = jnp.cumsum(counts) - counts
    pends = jnp.cumsum(pcounts)
    pstarts = pends - pcounts
    dest = pstarts[se] + jnp.arange(T * MOE_TOPK, dtype=jnp.int32) - starts[se]
    nb = -(-(T * MOE_TOPK + MOE_EXPERTS * (MOE_BLOCK - 1)) // MOE_BLOCK)
    P = nb * MOE_BLOCK
    buf_tok = jnp.full((P,), T, jnp.int32).at[dest].set(tok[order])
    buf_gate = jnp.zeros((P,), jnp.float32).at[dest].set(gate[order])
    blk_e = jnp.minimum(jnp.searchsorted(pends, jnp.arange(nb, dtype=jnp.int32) * MOE_BLOCK, side='right'), MOE_EXPERTS - 1)
    xp = jnp.concatenate([x, jnp.zeros((1, D), x.dtype)], axis=0)
    xb = xp[buf_tok].reshape(nb, MOE_BLOCK, D)

    def expert_block(args):
        xblk, e = args
        return (jax.nn.silu(xblk @ w1[e]) * (xblk @ w3[e])) @ w2[e]

    yb = lax.map(expert_block, (xb, blk_e)).reshape(P, D)
    y = jnp.zeros((T + 1, D), yb.dtype).at[buf_tok].add(yb * buf_gate[:, None].astype(yb.dtype))[:T]
    return y.reshape(B, L, D).astype(h.dtype)


def setup_inputs(seed: int = 0) -> dict:
    key = jax.random.key(seed)
    ks = iter(jax.random.split(key, 32))
    D = D_MODEL

    def nrm(shape, scale):
        return jax.random.normal(next(ks), shape, jnp.float32) * scale

    return {
        'x_prompt': nrm((BATCH, SEQ, D), 1.0),
        'x_sample': nrm((DEC_BATCH, DEC_SEQ, D), 1.0),
        'c_prompt': nrm((BATCH, D), 1.0),
        'c_sample': nrm((DEC_BATCH, D), 1.0),
        'state_ret': nrm((N_RET, DEC_BATCH, RET_HEADS, RET_DK, RET_DV), 0.5),
        'ada_w': nrm((DEPTH, D, 6 * D), 0.5 * D ** -0.5),
        'ada_b': nrm((DEPTH, 6 * D), 0.02),
        'norm1_g': 1.0 + nrm((DEPTH, D), 0.02),
        'norm2_g': 1.0 + nrm((DEPTH, D), 0.02),
        'ret_w_in': nrm((N_RET, D, RET_IN), D ** -0.5),
        'ret_w_out': nrm((N_RET, RET_V, D), RET_V ** -0.5),
        'gm_w_in': nrm((N_GM, D, GM_FFN), D ** -0.5),
        'gm_b_in': nrm((N_GM, GM_FFN), 0.02),
        'gm_ln_g': 1.0 + nrm((N_GM, GM_HALF), 0.02),
        'gm_ln_b': nrm((N_GM, GM_HALF), 0.02),
        'gm_w_s': nrm((N_GM, GM_GROUPS, GM_CHUNK, GM_CHUNK), GM_CHUNK ** -0.5),
        'gm_b_s': 1.0 + nrm((N_GM, GM_GROUPS, GM_CHUNK), 0.02),
        'gm_w_out': nrm((N_GM, GM_HALF, D), GM_HALF ** -0.5),
        'gm_b_out': nrm((N_GM, D), 0.02),
        'moe_w_rg': nrm((DEPTH, D, MOE_GROUPS), D ** -0.5),
        'moe_b_rg': nrm((DEPTH, MOE_GROUPS), 0.01),
        'moe_w_re': nrm((DEPTH, D, MOE_EXPERTS), D ** -0.5),
        'moe_b_re': nrm((DEPTH, MOE_GROUPS, MOE_PER_GROUP), 0.01),
        'moe_w1': nrm((DEPTH, MOE_EXPERTS, D, MOE_HIDDEN), D ** -0.5),
        'moe_w3': nrm((DEPTH, MOE_EXPERTS, D, MOE_HIDDEN), D ** -0.5),
        'moe_w2': nrm((DEPTH, MOE_EXPERTS, MOE_HIDDEN, D), MOE_HIDDEN ** -0.5),
        'final_g': 1.0 + nrm((D,), 0.02),
    }


def reference(x_prompt, x_sample, c_prompt, c_sample, state_ret, ada_w, ada_b, norm1_g, norm2_g, ret_w_in, ret_w_out, gm_w_in, gm_b_in, gm_ln_g, gm_ln_b, gm_w_s, gm_b_s, gm_w_out, gm_b_out, moe_w_rg, moe_b_rg, moe_w_re, moe_b_re, moe_w1, moe_w3, moe_w2, final_g):
    def trunk(x, c, ret_init, pos0):
        ret_out = []
        gm_out = []
        for i in range(DEPTH):
            sh1, sc1, g1, sh2, sc2, g2 = ada_modulation(c, ada_w[i], ada_b[i])
            h = rms_norm(x, norm1_g[i]) * (1 + sc1) + sh1
            j = i // N_MIXERS
            if i % N_MIXERS == 0:
                y, s = retention_mixer(h, ret_init[j], pos0, ret_w_in[j], ret_w_out[j])
                ret_out.append(s)
            else:
                y, v = gmlp_mixer(h, gm_w_in[j], gm_b_in[j], gm_ln_g[j], gm_ln_b[j], gm_w_s[j], gm_b_s[j], gm_w_out[j], gm_b_out[j])
                gm_out.append(v)
            x = x + g1 * y
            h = rms_norm(x, norm2_g[i]) * (1 + sc2) + sh2
            x = x + g2 * hier_moe(h, moe_w_rg[i], moe_b_rg[i], moe_w_re[i], moe_b_re[i], moe_w1[i], moe_w3[i], moe_w2[i])
        return rms_norm(x, final_g), jnp.stack(ret_out), jnp.stack(gm_out)

    zero_state = jnp.zeros((N_RET, x_prompt.shape[0], RET_HEADS, RET_DK, RET_DV), x_prompt.dtype)
    y_prompt, ret_prompt, _ = trunk(x_prompt, c_prompt, zero_state, 0)
    y_sample, ret_sample, gm_sample = trunk(x_sample, c_sample, state_ret, PAST_LEN)
    return (y_prompt, y_sample, ret_prompt, ret_sample, gm_sample)
```

```python
import functools

import numpy as np
import jax
import jax.numpy as jnp
from jax import lax
from jax.experimental import pallas as pl
from jax.experimental.pallas import tpu as pltpu

F32 = jnp.float32
BF16 = jnp.bfloat16

D = 1024
BATCH, SEQ = 4, 4096
DEC_BATCH, DEC_SEQ = 16, 64
PAST_LEN = 4096
DEPTH = 4
N_SEQ = BATCH + DEC_BATCH

RET_HEADS, RET_DK, RET_DV = 4, 256, 512
RET_QK = RET_HEADS * RET_DK
RET_V = RET_HEADS * RET_DV
RET_IN = 2 * RET_QK + 2 * RET_V
ROPE_BASE = 10000.0
ROPE_HALF = RET_DK // 2

GM_FFN = 6 * D
GM_HALF = GM_FFN // 2
GM_GROUPS = 4
GM_GDIM = GM_HALF // GM_GROUPS
GM_CHUNK = 128

MOE_GROUPS, MOE_PER_GROUP = 4, 8
MOE_EXPERTS = MOE_GROUPS * MOE_PER_GROUP
MOE_TOPK = 2
MOE_HIDDEN = 512
EPS = 1e-6

GROUP = 64
T_PROMPT = BATCH * SEQ
T_SAMPLE = DEC_BATCH * DEC_SEQ
T_ALL = T_PROMPT + T_SAMPLE
N_GROUPS = T_ALL // GROUP
TILE_GROUPS = 4
ROWS = TILE_GROUPS * GROUP
N_TILES = T_ALL // ROWS
N_PROMPT_TILES = T_PROMPT // ROWS
N_SAMPLE_TILES = T_SAMPLE // ROWS

RET_CHUNK_PROMPT = 256
RET_CHUNK_SAMPLE = DEC_SEQ

EXP_BLOCK = 256
N_ASSIGN = T_ALL * MOE_TOPK
N_EXP_BLOCKS = -(-(N_ASSIGN + MOE_EXPERTS * (EXP_BLOCK - 1)) // EXP_BLOCK)
P_ROWS = N_EXP_BLOCKS * EXP_BLOCK
ROUTER_LANES = 128

V7X_VMEM_LIMIT_BYTES = 56 * 1024 * 1024


def _params(*sem):
    return pltpu.CompilerParams(dimension_semantics=sem, vmem_limit_bytes=V7X_VMEM_LIMIT_BYTES)


def _resident(shape):
    nd = len(shape)
    return pl.BlockSpec(shape, lambda *_: (0,) * nd, pipeline_mode=pl.Buffered(1))


def _rms(x):
    return x * lax.rsqrt(jnp.mean(x * x, axis=-1, keepdims=True) + EPS)


def _silu(x):
    return x * jax.nn.sigmoid(x)


def _per_group(x2d, fn):
    return fn(x2d.reshape(TILE_GROUPS, GROUP, D)).reshape(ROWS, D)


def _norm_mod(x, gain_ref, mod_ref, shift_idx):
    y = _rms(x) * gain_ref[...]
    scale = mod_ref[:, shift_idx + 1:shift_idx + 2, :]
    shift = mod_ref[:, shift_idx:shift_idx + 1, :]
    return _per_group(y, lambda y3: y3 * (1.0 + scale) + shift)


def _add_moe(x, yg0_ref, yg1_ref, gates_ref, modp_ref):
    g = gates_ref[...]
    y = g[:, 0:1] * yg0_ref[...].astype(F32) + g[:, 1:2] * yg1_ref[...].astype(F32)
    gate2 = modp_ref[:, 5:6, :]
    return x + _per_group(y, lambda y3: y3 * gate2)


ADA_TN = 1536


def _ada_kernel(c_ref, w_ref, b_ref, o_ref):
    c = c_ref[...]
    s = _silu(c).astype(BF16)
    o_ref[0] = jnp.dot(s, w_ref[0].astype(BF16), preferred_element_type=F32) + b_ref[0]


def _ada_call(c_all, ada_w, ada_b):
    nt = 6 * D // ADA_TN
    return pl.pallas_call(
        _ada_kernel,
        grid=(DEPTH, nt),
        in_specs=[
            pl.BlockSpec((N_SEQ, D), lambda i, j: (0, 0)),
            pl.BlockSpec((1, D, ADA_TN), lambda i, j: (i, 0, j)),
            pl.BlockSpec((1, 1, ADA_TN), lambda i, j: (i, 0, j)),
        ],
        out_specs=pl.BlockSpec((1, N_SEQ, ADA_TN), lambda i, j: (i, 0, j)),
        out_shape=jax.ShapeDtypeStruct((DEPTH, N_SEQ, 6 * D), F32),
        compiler_params=_params("parallel", "parallel"),
        name="ada_modulation",
    )(c_all, ada_w, ada_b.reshape(DEPTH, 1, 6 * D))


def _row_spec(width):
    return pl.BlockSpec((ROWS, width), lambda i: (i, 0))


_MOD_SPEC = pl.BlockSpec((TILE_GROUPS, 6, D), lambda i: (i, 0, 0))


def _prev_specs():
    return [_row_spec(D), _row_spec(D), _row_spec(ROUTER_LANES), _MOD_SPEC]


def _ret_proj_kernel(has_prev, *refs):
    if has_prev:
        (x_ref, yg0_ref, yg1_ref, gates_ref, modp_ref, mod_ref, gain_ref, w_ref, cos_ref, sin_ref,
         xo_ref, p_ref) = refs
    else:
        x_ref, mod_ref, gain_ref, w_ref, cos_ref, sin_ref, p_ref = refs
    x = x_ref[...]
    if has_prev:
        x = _add_moe(x, yg0_ref, yg1_ref, gates_ref, modp_ref)
        xo_ref[...] = x
    hb = _norm_mod(x, gain_ref, mod_ref, 0).astype(BF16)
    cos = cos_ref[...]
    sin = sin_ref[...]
    for j in range(2 * RET_HEADS):
        lo = j * RET_DK
        acc = jnp.dot(hb, w_ref[:, lo:lo + RET_DK], preferred_element_type=F32)
        x1 = acc[:, :ROPE_HALF]
        x2 = acc[:, ROPE_HALF:]
        r1 = x1 * cos - x2 * sin
        r2 = x1 * sin + x2 * cos
        if j >= RET_HEADS:
            r1 = r1 * (RET_DK ** -0.5)
            r2 = r2 * (RET_DK ** -0.5)
        p_ref[:, lo:lo + ROPE_HALF] = r1.astype(BF16)
        p_ref[:, lo + ROPE_HALF:lo + RET_DK] = r2.astype(BF16)
    for j in range(2 * RET_HEADS):
        lo = 2 * RET_QK + j * RET_DV
        p_ref[:, lo:lo + RET_DV] = jnp.dot(
            hb, w_ref[:, lo:lo + RET_DV], preferred_element_type=F32).astype(BF16)


def _ret_proj_call(x, prev, mod, gain, w_in, cos, sin):
    has_prev = prev is not None
    in_specs = [_row_spec(D)]
    args = [x]
    if has_prev:
        in_specs += _prev_specs()
        args += list(prev)
    in_specs += [_MOD_SPEC, _resident((1, D)), _resident((D, RET_IN)),
                 _row_spec(ROPE_HALF), _row_spec(ROPE_HALF)]
    args += [mod, gain, w_in, cos, sin]
    out_specs = [_row_spec(RET_IN)]
    out_shape = [jax.ShapeDtypeStruct((T_ALL, RET_IN), BF16)]
    if has_prev:
        out_specs = [_row_spec(D)] + out_specs
        out_shape = [jax.ShapeDtypeStruct((T_ALL, D), F32)] + out_shape
    outs = pl.pallas_call(
        functools.partial(_ret_proj_kernel, has_prev),
        grid=(N_TILES,),
        in_specs=in_specs,
        out_specs=out_specs,
        out_shape=out_shape,
        compiler_params=_params("parallel"),
        name="ret_proj",
    )(*args)
    if has_prev:
        return outs[0], outs[1]
    return x, outs[0]


def _ret_core_kernel(has_s0, n_chunks, *refs):
    if has_s0:
        (q_ref, k_ref, v_ref, g_ref, s0_ref, intra_ref, qd_ref, kd_ref, cd_ref, _y_in,
         y_ref, so_ref, s_ref) = refs
    else:
        (q_ref, k_ref, v_ref, g_ref, intra_ref, qd_ref, kd_ref, cd_ref,
         y_ref, so_ref, s_ref) = refs
    c = pl.program_id(2)

    @pl.when(c == 0)
    def _():
        if has_s0:
            s_ref[...] = s0_ref[0, 0]
        else:
            s_ref[...] = jnp.zeros_like(s_ref)

    qb = q_ref[...]
    kb = k_ref[...]
    vb = v_ref[...]
    scores = lax.dot_general(qb, kb, (((1,), (1,)), ((), ())), preferred_element_type=F32)
    scores = scores * intra_ref[0]
    o = jnp.dot(scores.astype(BF16), vb, preferred_element_type=F32)
    s_old = s_ref[...]
    o = o + jnp.dot(qb, s_old.astype(BF16), preferred_element_type=F32) * qd_ref[0]
    kdec = (kb.astype(F32) * kd_ref[0]).astype(BF16)
    s_new = s_old * cd_ref[0][:, 0:1] + lax.dot_general(
        kdec, vb, (((0,), (0,)), ((), ())), preferred_element_type=F32)
    s_ref[...] = s_new
    gf = g_ref[...].astype(F32)
    y_ref[...] = (_silu(gf) * _rms(o)).astype(BF16)

    @pl.when(c == n_chunks - 1)
    def _():
        so_ref[0, 0] = s_new


def _ret_decay_tables(cl):
    log_g = np.log1p(-np.exp2(-5.0 - np.arange(RET_HEADS, dtype=np.float64)))
    idx = np.arange(cl, dtype=np.float64)
    diff = idx[:, None] - idx[None, :]
    intra = np.where(diff >= 0, np.exp(log_g[:, None, None] * np.maximum(diff, 0.0)), 0.0)
    qd = np.exp(log_g[:, None] * (idx[None, :] + 1.0))[:, :, None]
    kd = np.exp(log_g[:, None] * (cl - 1.0 - idx[None, :]))[:, :, None]
    cd = np.broadcast_to(np.exp(log_g * cl)[:, None, None], (RET_HEADS, 1, 128))
    return tuple(jnp.asarray(a, F32) for a in (intra, qd, kd, cd))


def _ret_core_call(p, s0, y_prev, n_seq, seq_len, cl, row0):
    has_s0 = s0 is not None
    n_chunks = seq_len // cl
    rb0 = row0 // cl
    intra, qd, kd, cd = _ret_decay_tables(cl)

    def rows(b, h, c):
        return rb0 + b * n_chunks + c

    in_specs = [
        pl.BlockSpec((cl, RET_DK), lambda b, h, c: (rows(b, h, c), h)),
        pl.BlockSpec((cl, RET_DK), lambda b, h, c: (rows(b, h, c), RET_HEADS + h)),
        pl.BlockSpec((cl, RET_DV), lambda b, h, c: (rows(b, h, c), RET_HEADS + h)),
        pl.BlockSpec((cl, RET_DV), lambda b, h, c: (rows(b, h, c), 2 * RET_HEADS + h)),
    ]
    args = [p, p, p, p]
    if has_s0:
        in_specs.append(pl.BlockSpec((1, 1, RET_DK, RET_DV), lambda b, h, c: (b, h, 0, 0)))
        args.append(s0)
    in_specs += [
        pl.BlockSpec((1, cl, cl), lambda b, h, c: (h, 0, 0)),
        pl.BlockSpec((1, cl, 1), lambda b, h, c: (h, 0, 0)),
        pl.BlockSpec((1, cl, 1), lambda b, h, c: (h, 0, 0)),
        pl.BlockSpec((1, 1, 128), lambda b, h, c: (h, 0, 0)),
    ]
    args += [intra, qd, kd, cd]
    aliases = {}
    if y_prev is not None:
        in_specs.append(pl.BlockSpec(memory_space=pl.ANY))
        args.append(y_prev)
        aliases = {len(args) - 1: 0}
    y, s_out = pl.pallas_call(
        functools.partial(_ret_core_kernel, has_s0, n_chunks),
        grid=(n_seq, RET_HEADS, n_chunks),
        in_specs=in_specs,
        out_specs=[
            pl.BlockSpec((cl, RET_DV), lambda b, h, c: (rows(b, h, c), h)),
            pl.BlockSpec((1, 1, RET_DK, RET_DV), lambda b, h, c: (b, h, 0, 0)),
        ],
        out_shape=[
            jax.ShapeDtypeStruct((T_ALL, RET_V), BF16),
            jax.ShapeDtypeStruct((n_seq, RET_HEADS, RET_DK, RET_DV), F32),
        ],
        scratch_shapes=[pltpu.VMEM((RET_DK, RET_DV), F32)],
        input_output_aliases=aliases,
        compiler_params=_params("parallel", "parallel", "arbitrary"),
        name="ret_core",
    )(*args)
    return y, s_out


def _residual_router(acc, x_ref, mod_ref, gain_ref, wr_ref, br_ref, xo_ref, h2_ref, lg_ref):
    gate1 = mod_ref[:, 2:3, :]
    xn = x_ref[...] + _per_group(acc, lambda a3: a3 * gate1)
    xo_ref[...] = xn
    h2 = _norm_mod(xn, gain_ref, mod_ref, 3)
    h2_ref[...] = h2.astype(BF16)
    lg_ref[...] = jnp.dot(h2, wr_ref[...], preferred_element_type=F32,
                          precision=lax.Precision.HIGHEST) + br_ref[...]


def _ret_out_kernel(yin_ref, w_ref, x_ref, mod_ref, gain_ref, wr_ref, br_ref, xo_ref, h2_ref, lg_ref):
    acc = jnp.dot(yin_ref[...], w_ref[...], preferred_element_type=F32)
    _residual_router(acc, x_ref, mod_ref, gain_ref, wr_ref, br_ref, xo_ref, h2_ref, lg_ref)


_MIX_OUT_SPECS = [_row_spec(D), _row_spec(D), _row_spec(ROUTER_LANES)]
_MIX_OUT_SHAPE = [
    jax.ShapeDtypeStruct((T_ALL, D), F32),
    jax.ShapeDtypeStruct((T_ALL, D), BF16),
    jax.ShapeDtypeStruct((T_ALL, ROUTER_LANES), F32),
]


def _ret_out_call(yin, w_out, x, mod, gain, w_r, b_r):
    return pl.pallas_call(
        _ret_out_kernel,
        grid=(N_TILES,),
        in_specs=[_row_spec(RET_V), _resident((RET_V, D)), _row_spec(D), _MOD_SPEC,
                  _resident((1, D)), _resident((D, ROUTER_LANES)), _resident((1, ROUTER_LANES))],
        out_specs=_MIX_OUT_SPECS,
        out_shape=_MIX_OUT_SHAPE,
        compiler_params=_params("parallel"),
        name="ret_out",
    )(yin, w_out, x, mod, gain, w_r, b_r)


GM_TN = 512


def _gelu_tanh(x):
    return 0.5 * x * (1.0 + jnp.tanh(np.sqrt(2.0 / np.pi) * (x + 0.044715 * (x * x * x))))


def _gm_proj_kernel(has_prev, *refs):
    if has_prev:
        (x_ref, yg0_ref, yg1_ref, gates_ref, modp_ref, mod_ref, gain_ref, w_ref, b_ref, lg_ref, lb_ref,
         xo_ref, u_ref, v_ref, vs_ref, vacc_ref) = refs
    else:
        (x_ref, mod_ref, gain_ref, w_ref, b_ref, lg_ref, lb_ref,
         u_ref, v_ref, vs_ref, vacc_ref) = refs
    x = x_ref[...]
    if has_prev:
        x = _add_moe(x, yg0_ref, yg1_ref, gates_ref, modp_ref)
        xo_ref[...] = x
    hb = _norm_mod(x, gain_ref, mod_ref, 0).astype(BF16)
    for j in range(GM_FFN // GM_TN):
        lo = j * GM_TN
        z = jnp.dot(hb, w_ref[:, lo:lo + GM_TN], preferred_element_type=F32) + b_ref[:, lo:lo + GM_TN]
        z = _gelu_tanh(z)
        if lo < GM_HALF:
            u_ref[:, lo:lo + GM_TN] = z.astype(BF16)
        else:
            vacc_ref[:, lo - GM_HALF:lo - GM_HALF + GM_TN] = z
    v = vacc_ref[...]
    mu = jnp.mean(v, axis=-1, keepdims=True)
    vc = v - mu
    var = jnp.mean(vc * vc, axis=-1, keepdims=True)
    vn = vc * lax.rsqrt(var + EPS) * lg_ref[...] + lb_ref[...]
    v_ref[...] = vn.astype(BF16)

    @pl.when(pl.program_id(0) >= N_PROMPT_TILES)
    def _():
        vs_ref[...] = vn


def _gm_proj_call(x, prev, mod, gain, w_in, b_in, ln_g, ln_b):
    has_prev = prev is not None
    in_specs = [_row_spec(D)]
    args = [x]
    if has_prev:
        in_specs += _prev_specs()
        args += list(prev)
    in_specs += [_MOD_SPEC, _resident((1, D)), _resident((D, GM_FFN)), _resident((1, GM_FFN)),
                 _resident((1, GM_HALF)), _resident((1, GM_HALF))]
    args += [mod, gain, w_in, b_in, ln_g, ln_b]
    vs_spec = pl.BlockSpec((ROWS, GM_HALF), lambda i: (jnp.maximum(i - N_PROMPT_TILES, 0), 0))
    out_specs = [_row_spec(GM_HALF), _row_spec(GM_HALF), vs_spec]
    out_shape = [jax.ShapeDtypeStruct((T_ALL, GM_HALF), BF16),
                 jax.ShapeDtypeStruct((T_ALL, GM_HALF), BF16),
                 jax.ShapeDtypeStruct((T_SAMPLE, GM_HALF), F32)]
    if has_prev:
        out_specs = [_row_spec(D)] + out_specs
        out_shape = [jax.ShapeDtypeStruct((T_ALL, D), F32)] + out_shape
    outs = pl.pallas_call(
        functools.partial(_gm_proj_kernel, has_prev),
        grid=(N_TILES,),
        in_specs=in_specs,
        out_specs=out_specs,
        out_shape=out_shape,
        scratch_shapes=[pltpu.VMEM((ROWS, GM_HALF), F32)],
        compiler_params=_params("arbitrary"),
        name="gm_proj",
    )(*args)
    if has_prev:
        return outs
    return (x,) + tuple(outs)


def _gm_out_kernel(u_ref, v_ref, ws_ref, bs_ref, w_ref, bo_ref, x_ref, mod_ref, gain_ref, wr_ref, br_ref,
                   xo_ref, h2_ref, lg_ref):
    acc = jnp.zeros((ROWS, D), F32)
    for g in range(GM_GROUPS):
        lo = g * GM_GDIM
        sp = jnp.dot(ws_ref[0, g], v_ref[:, lo:lo + GM_GDIM], preferred_element_type=F32) + bs_ref[0, g]
        gated = (u_ref[:, lo:lo + GM_GDIM].astype(F32) * sp).astype(BF16)
        acc = acc + jnp.dot(gated, w_ref[lo:lo + GM_GDIM, :], preferred_element_type=F32)
    acc = acc + bo_ref[...]
    _residual_router(acc, x_ref, mod_ref, gain_ref, wr_ref, br_ref, xo_ref, h2_ref, lg_ref)


def _gm_block_diag(w_s, b_s):
    mats, biases = [], []
    for cl in (GM_CHUNK, DEC_SEQ):
        tri = jnp.tril(jnp.ones((cl, cl), bool))
        blk = jnp.where(tri[None], w_s[:, :cl, :cl], 0.0)
        reps = ROWS // cl
        eye = jnp.eye(reps, dtype=w_s.dtype)
        bd = jnp.einsum("ab,gts->gatbs", eye, blk).reshape(GM_GROUPS, ROWS, ROWS)
        mats.append(bd)
        biases.append(jnp.tile(b_s[:, :cl], (1, reps))[:, :, None])
    return jnp.stack(mats).astype(BF16), jnp.stack(biases).astype(F32)


def _gm_out_call(u, v, ws_bd, bs_bd, w_out, b_out, x, mod, gain, w_r, b_r):
    def variant(i):
        return jnp.where(i >= N_PROMPT_TILES, 1, 0)

    return pl.pallas_call(
        _gm_out_kernel,
        grid=(N_TILES,),
        in_specs=[_row_spec(GM_HALF), _row_spec(GM_HALF),
                  pl.BlockSpec((1, GM_GROUPS, ROWS, ROWS), lambda i: (variant(i), 0, 0, 0)),
                  pl.BlockSpec((1, GM_GROUPS, ROWS, 1), lambda i: (variant(i), 0, 0, 0)),
                  _resident((GM_HALF, D)), _resident((1, D)), _row_spec(D), _MOD_SPEC,
                  _resident((1, D)), _resident((D, ROUTER_LANES)), _resident((1, ROUTER_LANES))],
        out_specs=_MIX_OUT_SPECS,
        out_shape=_MIX_OUT_SHAPE,
        compiler_params=_params("arbitrary"),
        name="gm_out",
    )(u, v, ws_bd, bs_bd, w_out, b_out, x, mod, gain, w_r, b_r)


def _expert_kernel(be_ref, bv_ref, nb_ref, xb_ref, w1_ref, w3_ref, w2_ref, yb_ref, w1s, w3s, w2s):
    b = pl.program_id(0)

    @pl.when(b < nb_ref[0])
    def _():
        prev_e = be_ref[jnp.maximum(b - 1, 0)]

        @pl.when((b == 0) | (be_ref[b] != prev_e))
        def _():
            w1s[...] = w1_ref[0].astype(BF16)
            w3s[...] = w3_ref[0].astype(BF16)
            w2s[...] = w2_ref[0].astype(BF16)

        row = lax.broadcasted_iota(jnp.int32, (EXP_BLOCK, 1), 0)
        x = jnp.where(row < bv_ref[b], xb_ref[...], jnp.zeros((), BF16))
        a = jnp.dot(x, w1s[...], preferred_element_type=F32)
        c = jnp.dot(x, w3s[...], preferred_element_type=F32)
        h = (_silu(a) * c).astype(BF16)
        yb_ref[...] = jnp.dot(h, w2s[...], preferred_element_type=F32).astype(BF16)


def _expert_call(blk_e, blk_valid, n_blk, xb, w1, w3, w2):
    def blk(b, be, bv, nb):
        return jnp.minimum(b, nb[0] - 1)

    grid_spec = pltpu.PrefetchScalarGridSpec(
        num_scalar_prefetch=3,
        grid=(N_EXP_BLOCKS,),
        in_specs=[
            pl.BlockSpec((EXP_BLOCK, D), lambda b, be, bv, nb: (blk(b, be, bv, nb), 0)),
            pl.BlockSpec((1, D, MOE_HIDDEN), lambda b, be, bv, nb: (be[blk(b, be, bv, nb)], 0, 0)),
            pl.BlockSpec((1, D, MOE_HIDDEN), lambda b, be, bv, nb: (be[blk(b, be, bv, nb)], 0, 0)),
            pl.BlockSpec((1, MOE_HIDDEN, D), lambda b, be, bv, nb: (be[blk(b, be, bv, nb)], 0, 0)),
        ],
        out_specs=pl.BlockSpec((EXP_BLOCK, D), lambda b, be, bv, nb: (blk(b, be, bv, nb), 0)),
        scratch_shapes=[pltpu.VMEM((D, MOE_HIDDEN), BF16), pltpu.VMEM((D, MOE_HIDDEN), BF16),
                        pltpu.VMEM((MOE_HIDDEN, D), BF16)],
    )
    return pl.pallas_call(
        _expert_kernel,
        grid_spec=grid_spec,
        out_shape=jax.ShapeDtypeStruct((P_ROWS, D), BF16),
        compiler_params=_params("arbitrary"),
        name="experts",
    )(blk_e, blk_valid, n_blk, xb, w1, w3, w2)


def _route(logits):
    gl = logits[:, :MOE_GROUPS]
    el = logits[:, MOE_GROUPS:MOE_GROUPS + MOE_EXPERTS].reshape(T_ALL, MOE_GROUPS, MOE_PER_GROUP)
    gp = jax.nn.softmax(gl, axis=-1)
    grp = jnp.argmax(gl, axis=-1)
    g_w = jnp.take_along_axis(gp, grp[:, None], axis=1)[:, 0]
    els = jnp.take_along_axis(el, grp[:, None, None], axis=1)[:, 0]
    ev, ei = lax.top_k(els, MOE_TOPK)
    gate = g_w[:, None] * jax.nn.softmax(ev, axis=-1)
    eid = (grp[:, None] * MOE_PER_GROUP + ei).astype(jnp.int32)
    return eid, gate


def _dispatch_plan(eid):
    flat = eid.reshape(-1)
    oh = jax.nn.one_hot(flat, MOE_EXPERTS, dtype=jnp.int32)
    csum = jnp.cumsum(oh, axis=0)
    rank = jnp.take_along_axis(csum - oh, flat[:, None], axis=1)[:, 0]
    counts = csum[-1]
    nblk = (counts + EXP_BLOCK - 1) // EXP_BLOCK
    bend = jnp.cumsum(nblk)
    bstart = bend - nblk
    dest = (bstart[flat] * EXP_BLOCK + rank).astype(jnp.int32)
    bidx = jnp.arange(N_EXP_BLOCKS, dtype=jnp.int32)
    blk_e = jnp.minimum(jnp.sum(bidx[:, None] >= bend[None, :], axis=1), MOE_EXPERTS - 1).astype(jnp.int32)
    blk_valid = jnp.clip(counts[blk_e] - (bidx - bstart[blk_e]) * EXP_BLOCK, 0, EXP_BLOCK).astype(jnp.int32)
    n_blk = bend[-1:].astype(jnp.int32)
    return dest.reshape(T_ALL, MOE_TOPK), blk_e, blk_valid, n_blk


def _moe(h2, logits, w1, w3, w2):
    eid, gate = _route(logits)
    dest, blk_e, blk_valid, n_blk = _dispatch_plan(eid)
    tok = jnp.repeat(jnp.arange(T_ALL, dtype=jnp.int32), MOE_TOPK)
    src = jnp.zeros((P_ROWS,), jnp.int32).at[dest.reshape(-1)].set(tok)
    xb = jnp.take(h2, src, axis=0)
    yb = _expert_call(blk_e, blk_valid, n_blk, xb, w1, w3, w2)
    yg0 = jnp.take(yb, dest[:, 0], axis=0)
    yg1 = jnp.take(yb, dest[:, 1], axis=0)
    gates = jnp.pad(gate.astype(F32), ((0, 0), (0, ROUTER_LANES - MOE_TOPK)))
    return yg0, yg1, gates


def _final_kernel(x_ref, yg0_ref, yg1_ref, gates_ref, modp_ref, gain_ref, o_ref):
    x = _add_moe(x_ref[...], yg0_ref, yg1_ref, gates_ref, modp_ref)
    o_ref[...] = _rms(x) * gain_ref[...]


def _final_call(x, prev, gain, tile0, n_tiles):
    def rows(width):
        return pl.BlockSpec((ROWS, width), lambda i: (tile0 + i, 0))

    return pl.pallas_call(
        _final_kernel,
        grid=(n_tiles,),
        in_specs=[rows(D), rows(D), rows(D), rows(ROUTER_LANES),
                  pl.BlockSpec((TILE_GROUPS, 6, D), lambda i: (tile0 + i, 0, 0)), _resident((1, D))],
        out_specs=pl.BlockSpec((ROWS, D), lambda i: (i, 0)),
        out_shape=jax.ShapeDtypeStruct((n_tiles * ROWS, D), F32),
        compiler_params=_params("parallel"),
        name="final_norm",
    )(x, *prev, gain)


def _rope_tables():
    pos = np.concatenate([np.tile(np.arange(SEQ), BATCH),
                          np.tile(PAST_LEN + np.arange(DEC_SEQ), DEC_BATCH)]).astype(np.float32)
    inv = (ROPE_BASE ** (-np.arange(ROPE_HALF, dtype=np.float32) / ROPE_HALF)).astype(np.float32)
    ang = (pos[:, None] * inv[None, :]).astype(np.float32).astype(np.float64)
    return jnp.asarray(np.cos(ang), F32), jnp.asarray(np.sin(ang), F32)


def kernel(x_prompt, x_sample, c_prompt, c_sample, state_ret, ada_w, ada_b, norm1_g, norm2_g, ret_w_in,
           ret_w_out, gm_w_in, gm_b_in, gm_ln_g, gm_ln_b, gm_w_s, gm_b_s, gm_w_out, gm_b_out, moe_w_rg,
           moe_b_rg, moe_w_re, moe_b_re, moe_w1, moe_w3, moe_w2, final_g):
    x = jnp.concatenate([x_prompt.reshape(T_PROMPT, D), x_sample.reshape(T_SAMPLE, D)], axis=0)
    c_all = jnp.concatenate([c_prompt, c_sample], axis=0)
    mod_all = _ada_call(c_all, ada_w, ada_b)
    seq_of_group = np.concatenate([np.repeat(np.arange(BATCH), SEQ // GROUP),
                                   BATCH + np.repeat(np.arange(DEC_BATCH), DEC_SEQ // GROUP)])
    cos, sin = _rope_tables()

    ret_prompt, ret_sample, gm_sample = [], [], []
    prev = None
    for i in range(DEPTH):
        j = i // 2
        mod = mod_all[i][seq_of_group].reshape(N_GROUPS, 6, D)
        g1 = norm1_g[i].reshape(1, D)
        g2 = norm2_g[i].reshape(1, D)
        w_r = jnp.pad(jnp.concatenate([moe_w_rg[i], moe_w_re[i]], axis=1),
                      ((0, 0), (0, ROUTER_LANES - MOE_GROUPS - MOE_EXPERTS)))
        b_r = jnp.pad(jnp.concatenate([moe_b_rg[i], moe_b_re[i].reshape(-1)]),
                      (0, ROUTER_LANES - MOE_GROUPS - MOE_EXPERTS)).reshape(1, ROUTER_LANES)
        if i % 2 == 0:
            x, p = _ret_proj_call(x, prev, mod, g1, ret_w_in[j].astype(BF16), cos, sin)
            y, s_p = _ret_core_call(p, None, None, BATCH, SEQ, RET_CHUNK_PROMPT, 0)
            y, s_s = _ret_core_call(p, state_ret[j], y, DEC_BATCH, DEC_SEQ, RET_CHUNK_SAMPLE, T_PROMPT)
            ret_prompt.append(s_p)
            ret_sample.append(s_s)
            x, h2, logits = _ret_out_call(y, ret_w_out[j].astype(BF16), x, mod, g2, w_r, b_r)
        else:
            x, u, v, vs = _gm_proj_call(x, prev, mod, g1, gm_w_in[j].astype(BF16),
                                        gm_b_in[j].reshape(1, GM_FFN), gm_ln_g[j].reshape(1, GM_HALF),
                                        gm_ln_b[j].reshape(1, GM_HALF))
            gm_sample.append(vs.reshape(DEC_BATCH, DEC_SEQ, GM_HALF))
            ws_bd, bs_bd = _gm_block_diag(gm_w_s[j], gm_b_s[j])
            x, h2, logits = _gm_out_call(u, v, ws_bd, bs_bd, gm_w_out[j].astype(BF16),
                                         gm_b_out[j].reshape(1, D), x, mod, g2, w_r, b_r)
        yg0, yg1, gates = _moe(h2, logits, moe_w1[i], moe_w3[i], moe_w2[i])
        prev = (yg0, yg1, gates, mod)

    fg = final_g.reshape(1, D)
    y_prompt = _final_call(x, prev, fg, 0, N_PROMPT_TILES).reshape(BATCH, SEQ, D)
    y_sample = _final_call(x, prev, fg, N_PROMPT_TILES, N_SAMPLE_TILES).reshape(DEC_BATCH, DEC_SEQ, D)
    return (y_prompt, y_sample, jnp.stack(ret_prompt), jnp.stack(ret_sample), jnp.stack(gm_sample))
```

```python
import functools

import numpy as np
import jax
import jax.numpy as jnp
from jax import lax
from jax.experimental import pallas as pl
from jax.experimental.pallas import tpu as pltpu
from jax.experimental.pallas import tpu_sc as plsc

F32 = jnp.float32
BF16 = jnp.bfloat16

D = 1024
BATCH, SEQ = 4, 4096
DEC_BATCH, DEC_SEQ = 16, 64
PAST_LEN = 4096
DEPTH = 4
N_SEQ = BATCH + DEC_BATCH

RET_HEADS, RET_DK, RET_DV = 4, 256, 512
RET_QK = RET_HEADS * RET_DK
RET_V = RET_HEADS * RET_DV
RET_IN = 2 * RET_QK + 2 * RET_V
ROPE_BASE = 10000.0
ROPE_HALF = RET_DK // 2

GM_FFN = 6 * D
GM_HALF = GM_FFN // 2
GM_GROUPS = 4
GM_GDIM = GM_HALF // GM_GROUPS
GM_CHUNK = 128

MOE_GROUPS, MOE_PER_GROUP = 4, 8
MOE_EXPERTS = MOE_GROUPS * MOE_PER_GROUP
MOE_TOPK = 2
MOE_HIDDEN = 512
EPS = 1e-6

GROUP = 64
T_PROMPT = BATCH * SEQ
T_SAMPLE = DEC_BATCH * DEC_SEQ
T_ALL = T_PROMPT + T_SAMPLE
N_GROUPS = T_ALL // GROUP
TILE_GROUPS = 4
ROWS = TILE_GROUPS * GROUP
N_TILES = T_ALL // ROWS
N_PROMPT_TILES = T_PROMPT // ROWS
N_SAMPLE_TILES = T_SAMPLE // ROWS

RET_CHUNK_PROMPT = 256
RET_CHUNK_SAMPLE = DEC_SEQ

EXP_BLOCK = 256
N_ASSIGN = T_ALL * MOE_TOPK
N_EXP_BLOCKS = -(-(N_ASSIGN + MOE_EXPERTS * (EXP_BLOCK - 1)) // EXP_BLOCK)
P_ROWS = N_EXP_BLOCKS * EXP_BLOCK
ROUTER_LANES = 128
ROUTE_TM = 512
N_ROUTE_TILES = T_ALL // ROUTE_TM
META_LANES = 256
assert META_LANES >= N_EXP_BLOCKS

DP = D // 2
U32 = jnp.uint32
SC_CORES, SC_SUBCORES = 2, 16
SC_WORKERS = SC_CORES * SC_SUBCORES
SC_ROWS_PER_WORKER = T_ALL // SC_WORKERS
SC_CHUNK = 32
assert SC_ROWS_PER_WORKER % SC_CHUNK == 0 and SC_CHUNK % 8 == 0

V7X_VMEM_LIMIT_BYTES = 56 * 1024 * 1024


def _params(*sem):
    return pltpu.CompilerParams(dimension_semantics=sem, vmem_limit_bytes=V7X_VMEM_LIMIT_BYTES)


def _resident(shape):
    nd = len(shape)
    return pl.BlockSpec(shape, lambda *_: (0,) * nd, pipeline_mode=pl.Buffered(1))


def _rms(x):
    return x * lax.rsqrt(jnp.mean(x * x, axis=-1, keepdims=True) + EPS)


def _silu(x):
    return x * jax.nn.sigmoid(x)


def _per_group(x2d, fn):
    return fn(x2d.reshape(TILE_GROUPS, GROUP, D)).reshape(ROWS, D)


def _norm_mod(x, gain_ref, mod_ref, shift_idx):
    y = _rms(x) * gain_ref[...]
    scale = mod_ref[:, shift_idx + 1:shift_idx + 2, :]
    shift = mod_ref[:, shift_idx:shift_idx + 1, :]
    return _per_group(y, lambda y3: y3 * (1.0 + scale) + shift)


def _pack_bf16_pairs(x):
    lo = lax.bitcast_convert_type(x[:, :DP].astype(BF16).astype(F32), U32)
    hi = lax.bitcast_convert_type(x[:, DP:].astype(BF16).astype(F32), U32)
    return (lo >> 16) | (hi & U32(0xFFFF0000))


def _unpack_bf16_pairs(w):
    lo = lax.bitcast_convert_type(w << 16, F32)
    hi = lax.bitcast_convert_type(w & U32(0xFFFF0000), F32)
    return jnp.concatenate([lo, hi], axis=1)


def _add_moe(x, yg0_ref, yg1_ref, gates_ref, modp_ref):
    g = gates_ref[...]
    y = g[:, 0:1] * _unpack_bf16_pairs(yg0_ref[...]) + g[:, 1:2] * _unpack_bf16_pairs(yg1_ref[...])
    gate2 = modp_ref[:, 5:6, :]
    return x + _per_group(y, lambda y3: y3 * gate2)


ADA_TN = 1536


def _ada_kernel(c_ref, w_ref, b_ref, o_ref):
    c = c_ref[...]
    s = _silu(c).astype(BF16)
    o_ref[0] = jnp.dot(s, w_ref[0].astype(BF16), preferred_element_type=F32) + b_ref[0]


def _ada_call(c_all, ada_w, ada_b):
    nt = 6 * D // ADA_TN
    return pl.pallas_call(
        _ada_kernel,
        grid=(DEPTH, nt),
        in_specs=[
            pl.BlockSpec((N_SEQ, D), lambda i, j: (0, 0)),
            pl.BlockSpec((1, D, ADA_TN), lambda i, j: (i, 0, j)),
            pl.BlockSpec((1, 1, ADA_TN), lambda i, j: (i, 0, j)),
        ],
        out_specs=pl.BlockSpec((1, N_SEQ, ADA_TN), lambda i, j: (i, 0, j)),
        out_shape=jax.ShapeDtypeStruct((DEPTH, N_SEQ, 6 * D), F32),
        compiler_params=_params("parallel", "parallel"),
        name="ada_modulation",
    )(c_all, ada_w, ada_b.reshape(DEPTH, 1, 6 * D))


def _row_spec(width):
    return pl.BlockSpec((ROWS, width), lambda i: (i, 0))


_MOD_SPEC = pl.BlockSpec((TILE_GROUPS, 6, D), lambda i: (i, 0, 0))


def _prev_specs():
    return [_row_spec(DP), _row_spec(DP), _row_spec(ROUTER_LANES), _MOD_SPEC]


def _ret_proj_kernel(has_prev, *refs):
    if has_prev:
        (x_ref, yg0_ref, yg1_ref, gates_ref, modp_ref, mod_ref, gain_ref, w_ref, cos_ref, sin_ref,
         xo_ref, p_ref) = refs
    else:
        x_ref, mod_ref, gain_ref, w_ref, cos_ref, sin_ref, p_ref = refs
    x = x_ref[...]
    if has_prev:
        x = _add_moe(x, yg0_ref, yg1_ref, gates_ref, modp_ref)
        xo_ref[...] = x
    hb = _norm_mod(x, gain_ref, mod_ref, 0).astype(BF16)
    cos = cos_ref[...]
    sin = sin_ref[...]
    for j in range(2 * RET_HEADS):
        lo = j * RET_DK
        acc = jnp.dot(hb, w_ref[:, lo:lo + RET_DK], preferred_element_type=F32)
        x1 = acc[:, :ROPE_HALF]
        x2 = acc[:, ROPE_HALF:]
        r1 = x1 * cos - x2 * sin
        r2 = x1 * sin + x2 * cos
        if j >= RET_HEADS:
            r1 = r1 * (RET_DK ** -0.5)
            r2 = r2 * (RET_DK ** -0.5)
        p_ref[:, lo:lo + ROPE_HALF] = r1.astype(BF16)
        p_ref[:, lo + ROPE_HALF:lo + RET_DK] = r2.astype(BF16)
    for j in range(2 * RET_HEADS):
        lo = 2 * RET_QK + j * RET_DV
        p_ref[:, lo:lo + RET_DV] = jnp.dot(
            hb, w_ref[:, lo:lo + RET_DV], preferred_element_type=F32).astype(BF16)


def _ret_proj_call(x, prev, mod, gain, w_in, cos, sin):
    has_prev = prev is not None
    in_specs = [_row_spec(D)]
    args = [x]
    if has_prev:
        in_specs += _prev_specs()
        args += list(prev)
    in_specs += [_MOD_SPEC, _resident((1, D)), _resident((D, RET_IN)),
                 _row_spec(ROPE_HALF), _row_spec(ROPE_HALF)]
    args += [mod, gain, w_in, cos, sin]
    out_specs = [_row_spec(RET_IN)]
    out_shape = [jax.ShapeDtypeStruct((T_ALL, RET_IN), BF16)]
    if has_prev:
        out_specs = [_row_spec(D)] + out_specs
        out_shape = [jax.ShapeDtypeStruct((T_ALL, D), F32)] + out_shape
    outs = pl.pallas_call(
        functools.partial(_ret_proj_kernel, has_prev),
        grid=(N_TILES,),
        in_specs=in_specs,
        out_specs=out_specs,
        out_shape=out_shape,
        compiler_params=_params("parallel"),
        name="ret_proj",
    )(*args)
    if has_prev:
        return outs[0], outs[1]
    return x, outs[0]


def _ret_core_kernel(has_s0, n_chunks, *refs):
    if has_s0:
        (q_ref, k_ref, v_ref, g_ref, s0_ref, intra_ref, qd_ref, kd_ref, cd_ref,
         y_ref, so_ref, s_ref) = refs
    else:
        (q_ref, k_ref, v_ref, g_ref, intra_ref, qd_ref, kd_ref, cd_ref,
         y_ref, so_ref, s_ref) = refs
    c = pl.program_id(2)

    @pl.when(c == 0)
    def _():
        if has_s0:
            s_ref[...] = s0_ref[0, 0]
        else:
            s_ref[...] = jnp.zeros_like(s_ref)

    qb = q_ref[...]
    kb = k_ref[...]
    vb = v_ref[...]
    scores = lax.dot_general(qb, kb, (((1,), (1,)), ((), ())), preferred_element_type=F32)
    scores = scores * intra_ref[0]
    o = jnp.dot(scores.astype(BF16), vb, preferred_element_type=F32)
    s_old = s_ref[...]
    o = o + jnp.dot(qb, s_old.astype(BF16), preferred_element_type=F32) * qd_ref[0]
    kdec = (kb.astype(F32) * kd_ref[0]).astype(BF16)
    s_new = s_old * cd_ref[0][:, 0:1] + lax.dot_general(
        kdec, vb, (((0,), (0,)), ((), ())), preferred_element_type=F32)
    s_ref[...] = s_new
    gf = g_ref[...].astype(F32)
    y_ref[...] = (_silu(gf) * _rms(o)).astype(BF16)

    @pl.when(c == n_chunks - 1)
    def _():
        so_ref[0, 0] = s_new


def _ret_decay_tables(cl):
    log_g = np.log1p(-np.exp2(-5.0 - np.arange(RET_HEADS, dtype=np.float64)))
    idx = np.arange(cl, dtype=np.float64)
    diff = idx[:, None] - idx[None, :]
    intra = np.where(diff >= 0, np.exp(log_g[:, None, None] * np.maximum(diff, 0.0)), 0.0)
    qd = np.exp(log_g[:, None] * (idx[None, :] + 1.0))[:, :, None]
    kd = np.exp(log_g[:, None] * (cl - 1.0 - idx[None, :]))[:, :, None]
    cd = np.broadcast_to(np.exp(log_g * cl)[:, None, None], (RET_HEADS, 1, 128))
    return tuple(jnp.asarray(a, F32) for a in (intra, qd, kd, cd))


def _ret_core_call(p, s0, n_seq, seq_len, cl, row0):
    has_s0 = s0 is not None
    n_chunks = seq_len // cl
    rb0 = row0 // cl
    intra, qd, kd, cd = _ret_decay_tables(cl)

    def rows(b, h, c):
        return rb0 + b * n_chunks + c

    in_specs = [
        pl.BlockSpec((cl, RET_DK), lambda b, h, c: (rows(b, h, c), h)),
        pl.BlockSpec((cl, RET_DK), lambda b, h, c: (rows(b, h, c), RET_HEADS + h)),
        pl.BlockSpec((cl, RET_DV), lambda b, h, c: (rows(b, h, c), RET_HEADS + h)),
        pl.BlockSpec((cl, RET_DV), lambda b, h, c: (rows(b, h, c), 2 * RET_HEADS + h)),
    ]
    args = [p, p, p, p]
    if has_s0:
        in_specs.append(pl.BlockSpec((1, 1, RET_DK, RET_DV), lambda b, h, c: (b, h, 0, 0)))
        args.append(s0)
    in_specs += [
        pl.BlockSpec((1, cl, cl), lambda b, h, c: (h, 0, 0)),
        pl.BlockSpec((1, cl, 1), lambda b, h, c: (h, 0, 0)),
        pl.BlockSpec((1, cl, 1), lambda b, h, c: (h, 0, 0)),
        pl.BlockSpec((1, 1, 128), lambda b, h, c: (h, 0, 0)),
    ]
    args += [intra, qd, kd, cd]
    y, s_out = pl.pallas_call(
        functools.partial(_ret_core_kernel, has_s0, n_chunks),
        grid=(n_seq, RET_HEADS, n_chunks),
        in_specs=in_specs,
        out_specs=[
            pl.BlockSpec((cl, RET_DV), lambda b, h, c: (b * n_chunks + c, h)),
            pl.BlockSpec((1, 1, RET_DK, RET_DV), lambda b, h, c: (b, h, 0, 0)),
        ],
        out_shape=[
            jax.ShapeDtypeStruct((n_seq * seq_len, RET_V), BF16),
            jax.ShapeDtypeStruct((n_seq, RET_HEADS, RET_DK, RET_DV), F32),
        ],
        scratch_shapes=[pltpu.VMEM((RET_DK, RET_DV), F32)],
        compiler_params=_params("parallel", "parallel", "arbitrary"),
        name="ret_core",
    )(*args)
    return y, s_out


def _residual_router(acc, x_ref, mod_ref, gain_ref, wr_ref, br_ref, xo_ref, h2_ref, lg_ref):
    gate1 = mod_ref[:, 2:3, :]
    xn = x_ref[...] + _per_group(acc, lambda a3: a3 * gate1)
    xo_ref[...] = xn
    h2 = _norm_mod(xn, gain_ref, mod_ref, 3)
    h2_ref[...] = _pack_bf16_pairs(h2)
    lg_ref[...] = jnp.dot(h2, wr_ref[...], preferred_element_type=F32,
                          precision=lax.Precision.HIGHEST) + br_ref[...]


def _ret_out_kernel(yp_ref, ys_ref, w_ref, x_ref, mod_ref, gain_ref, wr_ref, br_ref, xo_ref, h2_ref, lg_ref):
    yin = jnp.where(pl.program_id(0) < N_PROMPT_TILES, yp_ref[...], ys_ref[...])
    acc = jnp.dot(yin, w_ref[...], preferred_element_type=F32)
    _residual_router(acc, x_ref, mod_ref, gain_ref, wr_ref, br_ref, xo_ref, h2_ref, lg_ref)


_MIX_OUT_SPECS = [_row_spec(D), _row_spec(DP), _row_spec(ROUTER_LANES)]
_MIX_OUT_SHAPE = [
    jax.ShapeDtypeStruct((T_ALL, D), F32),
    jax.ShapeDtypeStruct((T_ALL, DP), U32),
    jax.ShapeDtypeStruct((T_ALL, ROUTER_LANES), F32),
]


def _ret_out_call(y_prompt, y_sample, w_out, x, mod, gain, w_r, b_r):
    yp_spec = pl.BlockSpec((ROWS, RET_V), lambda i: (jnp.minimum(i, N_PROMPT_TILES - 1), 0))
    ys_spec = pl.BlockSpec((ROWS, RET_V), lambda i: (jnp.maximum(i - N_PROMPT_TILES, 0), 0))
    return pl.pallas_call(
        _ret_out_kernel,
        grid=(N_TILES,),
        in_specs=[yp_spec, ys_spec, _resident((RET_V, D)), _row_spec(D), _MOD_SPEC,
                  _resident((1, D)), _resident((D, ROUTER_LANES)), _resident((1, ROUTER_LANES))],
        out_specs=_MIX_OUT_SPECS,
        out_shape=_MIX_OUT_SHAPE,
        compiler_params=_params("parallel"),
        name="ret_out",
    )(y_prompt, y_sample, w_out, x, mod, gain, w_r, b_r)


GM_TN = 512


def _gelu_tanh(x):
    return 0.5 * x * (1.0 + jnp.tanh(np.sqrt(2.0 / np.pi) * (x + 0.044715 * (x * x * x))))


def _gm_proj_kernel(has_prev, *refs):
    if has_prev:
        (x_ref, yg0_ref, yg1_ref, gates_ref, modp_ref, mod_ref, gain_ref, w_ref, b_ref, lg_ref, lb_ref,
         xo_ref, u_ref, v_ref, vs_ref, vacc_ref) = refs
    else:
        (x_ref, mod_ref, gain_ref, w_ref, b_ref, lg_ref, lb_ref,
         u_ref, v_ref, vs_ref, vacc_ref) = refs
    x = x_ref[...]
    if has_prev:
        x = _add_moe(x, yg0_ref, yg1_ref, gates_ref, modp_ref)
        xo_ref[...] = x
    hb = _norm_mod(x, gain_ref, mod_ref, 0).astype(BF16)
    for j in range(GM_FFN // GM_TN):
        lo = j * GM_TN
        z = jnp.dot(hb, w_ref[:, lo:lo + GM_TN], preferred_element_type=F32) + b_ref[:, lo:lo + GM_TN]
        z = _gelu_tanh(z)
        if lo < GM_HALF:
            u_ref[:, lo:lo + GM_TN] = z.astype(BF16)
        else:
            vacc_ref[:, lo - GM_HALF:lo - GM_HALF + GM_TN] = z
    v = vacc_ref[...]
    mu = jnp.mean(v, axis=-1, keepdims=True)
    vc = v - mu
    var = jnp.mean(vc * vc, axis=-1, keepdims=True)
    vn = vc * lax.rsqrt(var + EPS) * lg_ref[...] + lb_ref[...]
    v_ref[...] = vn.astype(BF16)

    @pl.when(pl.program_id(0) >= N_PROMPT_TILES)
    def _():
        vs_ref[...] = vn


def _gm_proj_call(x, prev, mod, gain, w_in, b_in, ln_g, ln_b):
    has_prev = prev is not None
    in_specs = [_row_spec(D)]
    args = [x]
    if has_prev:
        in_specs += _prev_specs()
        args += list(prev)
    in_specs += [_MOD_SPEC, _resident((1, D)), _resident((D, GM_FFN)), _resident((1, GM_FFN)),
                 _resident((1, GM_HALF)), _resident((1, GM_HALF))]
    args += [mod, gain, w_in, b_in, ln_g, ln_b]
    vs_spec = pl.BlockSpec((ROWS, GM_HALF), lambda i: (jnp.maximum(i - N_PROMPT_TILES, 0), 0))
    out_specs = [_row_spec(GM_HALF), _row_spec(GM_HALF), vs_spec]
    out_shape = [jax.ShapeDtypeStruct((T_ALL, GM_HALF), BF16),
                 jax.ShapeDtypeStruct((T_ALL, GM_HALF), BF16),
                 jax.ShapeDtypeStruct((T_SAMPLE, GM_HALF), F32)]
    if has_prev:
        out_specs = [_row_spec(D)] + out_specs
        out_shape = [jax.ShapeDtypeStruct((T_ALL, D), F32)] + out_shape
    outs = pl.pallas_call(
        functools.partial(_gm_proj_kernel, has_prev),
        grid=(N_TILES,),
        in_specs=in_specs,
        out_specs=out_specs,
        out_shape=out_shape,
        scratch_shapes=[pltpu.VMEM((ROWS, GM_HALF), F32)],
        compiler_params=_params("arbitrary"),
        name="gm_proj",
    )(*args)
    if has_prev:
        return outs
    return (x,) + tuple(outs)


def _gm_out_kernel(u_ref, v_ref, ws_ref, bs_ref, w_ref, bo_ref, x_ref, mod_ref, gain_ref, wr_ref, br_ref,
                   xo_ref, h2_ref, lg_ref):
    acc = jnp.zeros((ROWS, D), F32)
    for g in range(GM_GROUPS):
        lo = g * GM_GDIM
        sp = jnp.dot(ws_ref[0, g], v_ref[:, lo:lo + GM_GDIM], preferred_element_type=F32) + bs_ref[0, g]
        gated = (u_ref[:, lo:lo + GM_GDIM].astype(F32) * sp).astype(BF16)
        acc = acc + jnp.dot(gated, w_ref[lo:lo + GM_GDIM, :], preferred_element_type=F32)
    acc = acc + bo_ref[...]
    _residual_router(acc, x_ref, mod_ref, gain_ref, wr_ref, br_ref, xo_ref, h2_ref, lg_ref)


def _gm_block_diag(w_s, b_s):
    mats, biases = [], []
    for cl in (GM_CHUNK, DEC_SEQ):
        tri = jnp.tril(jnp.ones((cl, cl), bool))
        blk = jnp.where(tri[None], w_s[:, :cl, :cl], 0.0)
        reps = ROWS // cl
        eye = jnp.eye(reps, dtype=w_s.dtype)
        bd = jnp.einsum("ab,gts->gatbs", eye, blk).reshape(GM_GROUPS, ROWS, ROWS)
        mats.append(bd)
        biases.append(jnp.tile(b_s[:, :cl], (1, reps))[:, :, None])
    return jnp.stack(mats).astype(BF16), jnp.stack(biases).astype(F32)


def _gm_out_call(u, v, ws_bd, bs_bd, w_out, b_out, x, mod, gain, w_r, b_r):
    def variant(i):
        return jnp.where(i >= N_PROMPT_TILES, 1, 0)

    return pl.pallas_call(
        _gm_out_kernel,
        grid=(N_TILES,),
        in_specs=[_row_spec(GM_HALF), _row_spec(GM_HALF),
                  pl.BlockSpec((1, GM_GROUPS, ROWS, ROWS), lambda i: (variant(i), 0, 0, 0)),
                  pl.BlockSpec((1, GM_GROUPS, ROWS, 1), lambda i: (variant(i), 0, 0, 0)),
                  _resident((GM_HALF, D)), _resident((1, D)), _row_spec(D), _MOD_SPEC,
                  _resident((1, D)), _resident((D, ROUTER_LANES)), _resident((1, ROUTER_LANES))],
        out_specs=_MIX_OUT_SPECS,
        out_shape=_MIX_OUT_SHAPE,
        compiler_params=_params("arbitrary"),
        name="gm_out",
    )(u, v, ws_bd, bs_bd, w_out, b_out, x, mod, gain, w_r, b_r)


def _expert_kernel(be_ref, bv_ref, nb_ref, xb_ref, w1_ref, w3_ref, w2_ref, yb_ref, w1s, w3s, w2s):
    b = pl.program_id(0)

    @pl.when(b < nb_ref[0])
    def _():
        prev_e = be_ref[jnp.maximum(b - 1, 0)]

        @pl.when((b == 0) | (be_ref[b] != prev_e))
        def _():
            w1s[...] = w1_ref[0].astype(BF16)
            w3s[...] = w3_ref[0].astype(BF16)
            w2s[...] = w2_ref[0].astype(BF16)

        row = lax.broadcasted_iota(jnp.int32, (EXP_BLOCK, 1), 0)
        xw = jnp.where(row < bv_ref[b], xb_ref[...], U32(0))
        x = _unpack_bf16_pairs(xw).astype(BF16)
        a = jnp.dot(x, w1s[...], preferred_element_type=F32)
        c = jnp.dot(x, w3s[...], preferred_element_type=F32)
        h = (_silu(a) * c).astype(BF16)
        yb_ref[...] = _pack_bf16_pairs(jnp.dot(h, w2s[...], preferred_element_type=F32))


def _expert_call(blk_e, blk_valid, n_blk, xb, w1, w3, w2):
    def blk(b, be, bv, nb):
        return jnp.minimum(b, nb[0] - 1)

    grid_spec = pltpu.PrefetchScalarGridSpec(
        num_scalar_prefetch=3,
        grid=(N_EXP_BLOCKS,),
        in_specs=[
            pl.BlockSpec((EXP_BLOCK, DP), lambda b, be, bv, nb: (blk(b, be, bv, nb), 0)),
            pl.BlockSpec((1, D, MOE_HIDDEN), lambda b, be, bv, nb: (be[blk(b, be, bv, nb)], 0, 0)),
            pl.BlockSpec((1, D, MOE_HIDDEN), lambda b, be, bv, nb: (be[blk(b, be, bv, nb)], 0, 0)),
            pl.BlockSpec((1, MOE_HIDDEN, D), lambda b, be, bv, nb: (be[blk(b, be, bv, nb)], 0, 0)),
        ],
        out_specs=pl.BlockSpec((EXP_BLOCK, DP), lambda b, be, bv, nb: (blk(b, be, bv, nb), 0)),
        scratch_shapes=[pltpu.VMEM((D, MOE_HIDDEN), BF16), pltpu.VMEM((D, MOE_HIDDEN), BF16),
                        pltpu.VMEM((MOE_HIDDEN, D), BF16)],
    )
    return pl.pallas_call(
        _expert_kernel,
        grid_spec=grid_spec,
        out_shape=jax.ShapeDtypeStruct((P_ROWS, DP), U32),
        compiler_params=_params("arbitrary"),
        name="experts",
    )(blk_e, blk_valid, n_blk, xb, w1, w3, w2)


def _route_kernel(lg_ref, dest_ref, gates_ref, meta_ref, cnt_ref, base_ref):
    ph = pl.program_id(0)
    t = pl.program_id(1)
    tm = ROUTE_TM
    lt = lg_ref[...].T
    el = lt[0:MOE_EXPERTS]
    gl = lt[MOE_EXPERTS:MOE_EXPERTS + 8]
    gidx = lax.broadcasted_iota(jnp.int32, (8, tm), 0)
    neg = jnp.float32(-jnp.inf)
    gl = jnp.where(gidx < MOE_GROUPS, gl, neg)
    gmax = jnp.max(gl, axis=0, keepdims=True)
    grp = jnp.min(jnp.where(gl == gmax, gidx, MOE_GROUPS), axis=0, keepdims=True)
    eidx = lax.broadcasted_iota(jnp.int32, (MOE_EXPERTS, tm), 0)
    els = jnp.where((eidx >> 3) == grp, el, neg)
    m1 = jnp.max(els, axis=0, keepdims=True)
    i1 = jnp.min(jnp.where(els == m1, eidx, MOE_EXPERTS), axis=0, keepdims=True)
    els2 = jnp.where(eidx == i1, neg, els)
    m2 = jnp.max(els2, axis=0, keepdims=True)
    i2 = jnp.min(jnp.where(els2 == m2, eidx, MOE_EXPERTS), axis=0, keepdims=True)
    sel1 = eidx == i1
    sel2 = eidx == i2
    cnt = jnp.where(sel1 | sel2, 1.0, 0.0)
    tile_counts = jnp.sum(cnt, axis=1, keepdims=True)

    @pl.when(ph == 0)
    def _():
        @pl.when(t == 0)
        def _():
            cnt_ref[...] = jnp.zeros_like(cnt_ref)

        cnt_ref[...] += tile_counts

    @pl.when(ph == 1)
    def _():
        @pl.when(t == 0)
        def _():
            counts = cnt_ref[...]
            nblk = jnp.floor((counts + (EXP_BLOCK - 1.0)) * (1.0 / EXP_BLOCK))
            r = lax.broadcasted_iota(jnp.int32, (MOE_EXPERTS, MOE_EXPERTS), 0)
            c = lax.broadcasted_iota(jnp.int32, (MOE_EXPERTS, MOE_EXPERTS), 1)
            nblk_row = jnp.sum(jnp.where(r == c, nblk, 0.0), axis=0, keepdims=True)
            bstart = jnp.sum(jnp.where(c < r, nblk_row, 0.0), axis=1, keepdims=True)
            base_ref[...] = bstart * EXP_BLOCK
            bend = bstart + nblk
            bidx = lax.broadcasted_iota(jnp.int32, (1, META_LANES), 1).astype(F32)
            blk_e = jnp.minimum(jnp.sum(jnp.where(bidx >= bend, 1.0, 0.0), axis=0, keepdims=True),
                                MOE_EXPERTS - 1.0)
            erow = lax.broadcasted_iota(jnp.int32, (MOE_EXPERTS, META_LANES), 0).astype(F32)
            mine = erow == blk_e
            cnt_b = jnp.sum(jnp.where(mine, counts, 0.0), axis=0, keepdims=True)
            start_b = jnp.sum(jnp.where(mine, bstart, 0.0), axis=0, keepdims=True)
            valid = jnp.clip(cnt_b - (bidx - start_b) * EXP_BLOCK, 0.0, float(EXP_BLOCK))
            n_blk = jnp.sum(nblk, axis=0, keepdims=True)
            mrow = lax.broadcasted_iota(jnp.int32, (8, META_LANES), 0)
            meta = jnp.where(mrow == 0, blk_e, jnp.where(mrow == 1, valid, jnp.where(mrow == 2, n_blk, 0.0)))
            meta_ref[...] = meta.astype(jnp.int32)

        base = base_ref[...]
        before = (lax.broadcasted_iota(jnp.int32, (tm, tm), 0) < lax.broadcasted_iota(jnp.int32, (tm, tm), 1))
        prefix = jnp.dot(cnt.astype(BF16), jnp.where(before, 1.0, 0.0).astype(BF16),
                         preferred_element_type=F32)
        pos = base + prefix
        d1 = jnp.sum(jnp.where(sel1, pos, 0.0), axis=0, keepdims=True)
        d2 = jnp.sum(jnp.where(sel2, pos, 0.0), axis=0, keepdims=True)
        dest_ref[...] = jnp.concatenate([d1, d2], axis=0).astype(jnp.int32)
        base_ref[...] = base + tile_counts
        g_w = 1.0 / jnp.sum(jnp.exp(gl - gmax), axis=0, keepdims=True)
        e21 = jnp.exp(m2 - m1)
        p1 = 1.0 / (1.0 + e21)
        rid = lax.broadcasted_iota(jnp.int32, (ROUTER_LANES, tm), 0)
        gt = jnp.where(rid == 0, g_w * p1, jnp.where(rid == 1, g_w * (e21 * p1), 0.0))
        gates_ref[...] = gt.T


def _route_call(logits):
    return pl.pallas_call(
        _route_kernel,
        grid=(2, N_ROUTE_TILES),
        in_specs=[pl.BlockSpec((ROUTE_TM, ROUTER_LANES), lambda ph, t: (t, 0))],
        out_specs=[pl.BlockSpec((MOE_TOPK, ROUTE_TM), lambda ph, t: (0, t * ph)),
                   pl.BlockSpec((ROUTE_TM, ROUTER_LANES), lambda ph, t: (t * ph, 0)),
                   pl.BlockSpec((8, META_LANES), lambda ph, t: (0, 0))],
        out_shape=[jax.ShapeDtypeStruct((MOE_TOPK, T_ALL), jnp.int32),
                   jax.ShapeDtypeStruct((T_ALL, ROUTER_LANES), F32),
                   jax.ShapeDtypeStruct((8, META_LANES), jnp.int32)],
        scratch_shapes=[pltpu.VMEM((MOE_EXPERTS, 1), F32), pltpu.VMEM((MOE_EXPERTS, 1), F32)],
        compiler_params=_params("arbitrary", "arbitrary"),
        name="route",
    )(logits)


def _sc_mesh():
    return plsc.VectorSubcoreMesh(core_axis_name="c", subcore_axis_name="s")


def _sc_token_offset(j):
    wid = lax.axis_index("s") * SC_CORES + lax.axis_index("c")
    return pl.multiple_of(wid * SC_ROWS_PER_WORKER + j * SC_CHUNK, 8)


def _dispatch_body(h_hbm, d0_hbm, d1_hbm, out_hbm, i0_v, i1_v, rows_v, sem0, sem1, sem2):
    @pl.loop(0, SC_ROWS_PER_WORKER // SC_CHUNK)
    def _(j):
        off = _sc_token_offset(j)
        c0 = pltpu.async_copy(d0_hbm.at[pl.ds(off, SC_CHUNK)], i0_v, sem0)
        c1 = pltpu.async_copy(d1_hbm.at[pl.ds(off, SC_CHUNK)], i1_v, sem1)
        c2 = pltpu.async_copy(h_hbm.at[pl.ds(off, SC_CHUNK)], rows_v, sem2)
        c0.wait()
        c1.wait()
        c2.wait()
        s0 = pltpu.async_copy(rows_v, out_hbm.at[i0_v], sem0)
        s1 = pltpu.async_copy(rows_v, out_hbm.at[i1_v], sem1)
        s0.wait()
        s1.wait()


def _dispatch_call(h2, dest0, dest1):
    return pl.kernel(
        _dispatch_body,
        out_type=jax.ShapeDtypeStruct((P_ROWS, DP), U32),
        mesh=_sc_mesh(),
        scratch_types=[pltpu.VMEM((SC_CHUNK,), jnp.int32), pltpu.VMEM((SC_CHUNK,), jnp.int32),
                       pltpu.VMEM((SC_CHUNK, DP), U32),
                       pltpu.SemaphoreType.DMA, pltpu.SemaphoreType.DMA, pltpu.SemaphoreType.DMA],
        name="moe_dispatch",
    )(h2, dest0, dest1)


def _combine_body(yb_hbm, d0_hbm, d1_hbm, o0_hbm, o1_hbm, i0_v, i1_v, r0_v, r1_v, sem0, sem1):
    @pl.loop(0, SC_ROWS_PER_WORKER // SC_CHUNK)
    def _(j):
        off = _sc_token_offset(j)
        c0 = pltpu.async_copy(d0_hbm.at[pl.ds(off, SC_CHUNK)], i0_v, sem0)
        c1 = pltpu.async_copy(d1_hbm.at[pl.ds(off, SC_CHUNK)], i1_v, sem1)
        c0.wait()
        c1.wait()
        g0 = pltpu.async_copy(yb_hbm.at[i0_v], r0_v, sem0)
        g1 = pltpu.async_copy(yb_hbm.at[i1_v], r1_v, sem1)
        g0.wait()
        g1.wait()
        w0 = pltpu.async_copy(r0_v, o0_hbm.at[pl.ds(off, SC_CHUNK)], sem0)
        w1 = pltpu.async_copy(r1_v, o1_hbm.at[pl.ds(off, SC_CHUNK)], sem1)
        w0.wait()
        w1.wait()


def _combine_call(yb, dest0, dest1):
    out = jax.ShapeDtypeStruct((T_ALL, DP), U32)
    return pl.kernel(
        _combine_body,
        out_type=(out, out),
        mesh=_sc_mesh(),
        scratch_types=[pltpu.VMEM((SC_CHUNK,), jnp.int32), pltpu.VMEM((SC_CHUNK,), jnp.int32),
                       pltpu.VMEM((SC_CHUNK, DP), U32), pltpu.VMEM((SC_CHUNK, DP), U32),
                       pltpu.SemaphoreType.DMA, pltpu.SemaphoreType.DMA],
        name="moe_combine",
    )(yb, dest0, dest1)


def _moe(h2, logits, w1, w3, w2):
    dest, gates, meta = _route_call(logits)
    dest0, dest1 = dest[0], dest[1]
    xb = _dispatch_call(h2, dest0, dest1)
    yb = _expert_call(meta[0, :N_EXP_BLOCKS], meta[1, :N_EXP_BLOCKS], meta[2, :1], xb, w1, w3, w2)
    yg0, yg1 = _combine_call(yb, dest0, dest1)
    return yg0, yg1, gates


def _final_kernel(x_ref, yg0_ref, yg1_ref, gates_ref, modp_ref, gain_ref, o_ref):
    x = _add_moe(x_ref[...], yg0_ref, yg1_ref, gates_ref, modp_ref)
    o_ref[...] = _rms(x) * gain_ref[...]


def _final_call(x, prev, gain, tile0, n_tiles):
    def rows(width):
        return pl.BlockSpec((ROWS, width), lambda i: (tile0 + i, 0))

    return pl.pallas_call(
        _final_kernel,
        grid=(n_tiles,),
        in_specs=[rows(D), rows(DP), rows(DP), rows(ROUTER_LANES),
                  pl.BlockSpec((TILE_GROUPS, 6, D), lambda i: (tile0 + i, 0, 0)), _resident((1, D))],
        out_specs=pl.BlockSpec((ROWS, D), lambda i: (i, 0)),
        out_shape=jax.ShapeDtypeStruct((n_tiles * ROWS, D), F32),
        compiler_params=_params("parallel"),
        name="final_norm",
    )(x, *prev, gain)


def _rope_tables():
    pos = np.concatenate([np.tile(np.arange(SEQ), BATCH),
                          np.tile(PAST_LEN + np.arange(DEC_SEQ), DEC_BATCH)]).astype(np.float32)
    inv = (ROPE_BASE ** (-np.arange(ROPE_HALF, dtype=np.float32) / ROPE_HALF)).astype(np.float32)
    ang = (pos[:, None] * inv[None, :]).astype(np.float32).astype(np.float64)
    return jnp.asarray(np.cos(ang), F32), jnp.asarray(np.sin(ang), F32)


def kernel(x_prompt, x_sample, c_prompt, c_sample, state_ret, ada_w, ada_b, norm1_g, norm2_g, ret_w_in,
           ret_w_out, gm_w_in, gm_b_in, gm_ln_g, gm_ln_b, gm_w_s, gm_b_s, gm_w_out, gm_b_out, moe_w_rg,
           moe_b_rg, moe_w_re, moe_b_re, moe_w1, moe_w3, moe_w2, final_g):
    x = jnp.concatenate([x_prompt.reshape(T_PROMPT, D), x_sample.reshape(T_SAMPLE, D)], axis=0)
    c_all = jnp.concatenate([c_prompt, c_sample], axis=0)
    mod_all = _ada_call(c_all, ada_w, ada_b)
    seq_of_group = np.concatenate([np.repeat(np.arange(BATCH), SEQ // GROUP),
                                   BATCH + np.repeat(np.arange(DEC_BATCH), DEC_SEQ // GROUP)])
    cos, sin = _rope_tables()

    ret_prompt, ret_sample, gm_sample = [], [], []
    prev = None
    for i in range(DEPTH):
        j = i // 2
        mod = mod_all[i][seq_of_group].reshape(N_GROUPS, 6, D)
        g1 = norm1_g[i].reshape(1, D)
        g2 = norm2_g[i].reshape(1, D)
        w_r = jnp.pad(jnp.concatenate([moe_w_re[i], moe_w_rg[i]], axis=1),
                      ((0, 0), (0, ROUTER_LANES - MOE_GROUPS - MOE_EXPERTS)))
        b_r = jnp.pad(jnp.concatenate([moe_b_re[i].reshape(-1), moe_b_rg[i]]),
                      (0, ROUTER_LANES - MOE_GROUPS - MOE_EXPERTS)).reshape(1, ROUTER_LANES)
        if i % 2 == 0:
            x, p = _ret_proj_call(x, prev, mod, g1, ret_w_in[j].astype(BF16), cos, sin)
            y_p, s_p = _ret_core_call(p, None, BATCH, SEQ, RET_CHUNK_PROMPT, 0)
            y_s, s_s = _ret_core_call(p, state_ret[j], DEC_BATCH, DEC_SEQ, RET_CHUNK_SAMPLE, T_PROMPT)
            ret_prompt.append(s_p)
            ret_sample.append(s_s)
            x, h2, logits = _ret_out_call(y_p, y_s, ret_w_out[j].astype(BF16), x, mod, g2, w_r, b_r)
        else:
            x, u, v, vs = _gm_proj_call(x, prev, mod, g1, gm_w_in[j].astype(BF16),
                                        gm_b_in[j].reshape(1, GM_FFN), gm_ln_g[j].reshape(1, GM_HALF),
                                        gm_ln_b[j].reshape(1, GM_HALF))
            gm_sample.append(vs.reshape(DEC_BATCH, DEC_SEQ, GM_HALF))
            ws_bd, bs_bd = _gm_block_diag(gm_w_s[j], gm_b_s[j])
            x, h2, logits = _gm_out_call(u, v, ws_bd, bs_bd, gm_w_out[j].astype(BF16),
                                         gm_b_out[j].reshape(1, D), x, mod, g2, w_r, b_r)
        yg0, yg1, gates = _moe(h2, logits, moe_w1[i], moe_w3[i], moe_w2[i])
        prev = (yg0, yg1, gates, mod)

    fg = final_g.reshape(1, D)
    y_prompt = _final_call(x, prev, fg, 0, N_PROMPT_TILES).reshape(BATCH, SEQ, D)
    y_sample = _final_call(x, prev, fg, N_PROMPT_TILES, N_SAMPLE_TILES).reshape(DEC_BATCH, DEC_SEQ, D)
    return (y_prompt, y_sample, jnp.stack(ret_prompt), jnp.stack(ret_sample), jnp.stack(gm_sample))
```

```python
import functools

import numpy as np
import jax
import jax.numpy as jnp
from jax import lax
from jax.experimental import pallas as pl
from jax.experimental.pallas import tpu as pltpu
from jax.experimental.pallas import tpu_sc as plsc

F32 = jnp.float32
BF16 = jnp.bfloat16
U32 = jnp.uint32

D = 1024
BATCH, SEQ = 4, 4096
DEC_BATCH, DEC_SEQ = 16, 64
PAST_LEN = 4096
DEPTH = 4
N_SEQ = BATCH + DEC_BATCH

RET_HEADS, RET_DK, RET_DV = 4, 256, 512
RET_QK = RET_HEADS * RET_DK
RET_V = RET_HEADS * RET_DV
RET_IN = 2 * RET_QK + 2 * RET_V
ROPE_BASE = 10000.0
ROPE_HALF = RET_DK // 2

GM_FFN = 6 * D
GM_HALF = GM_FFN // 2
GM_GROUPS = 4
GM_GDIM = GM_HALF // GM_GROUPS
GM_CHUNK = 128

MOE_GROUPS, MOE_PER_GROUP = 4, 8
MOE_EXPERTS = MOE_GROUPS * MOE_PER_GROUP
MOE_TOPK = 2
MOE_HIDDEN = 512
EPS = 1e-6

GROUP = 64
T_PROMPT = BATCH * SEQ
T_SAMPLE = DEC_BATCH * DEC_SEQ
T_ALL = T_PROMPT + T_SAMPLE
N_GROUPS = T_ALL // GROUP
ROWS_WIDE = 512
ROWS_GM_PROJ = 256

RET_CHUNK_PROMPT = 256
RET_CHUNK_SAMPLE = DEC_SEQ

GM_MIX = 256

EXP_BLOCK = 256
N_ASSIGN = T_ALL * MOE_TOPK
N_EXP_BLOCKS = -(-(N_ASSIGN + MOE_EXPERTS * (EXP_BLOCK - 1)) // EXP_BLOCK)
P_ROWS = N_EXP_BLOCKS * EXP_BLOCK
ROUTER_LANES = 128
ROUTE_TM = 512
N_ROUTE_TILES = T_ALL // ROUTE_TM
META_LANES = 256
assert META_LANES >= N_EXP_BLOCKS

DP = D // 2
SC_CORES, SC_SUBCORES = 2, 16
SC_WORKERS = SC_CORES * SC_SUBCORES
SC_ROWS_PER_WORKER = T_ALL // SC_WORKERS
SC_CHUNK = 32
assert SC_ROWS_PER_WORKER % SC_CHUNK == 0 and SC_CHUNK % 8 == 0

V7X_VMEM_LIMIT_BYTES = 56 * 1024 * 1024


def _params(*sem):
    return pltpu.CompilerParams(dimension_semantics=sem, vmem_limit_bytes=V7X_VMEM_LIMIT_BYTES)


def _resident(shape):
    nd = len(shape)
    return pl.BlockSpec(shape, lambda *_: (0,) * nd, pipeline_mode=pl.Buffered(1))


def _resident_layer(shape, layer):
    nd = len(shape)
    return pl.BlockSpec((1,) + shape, lambda *_: (layer,) + (0,) * nd, pipeline_mode=pl.Buffered(1))


def _rms(x):
    return x * lax.rsqrt(jnp.mean(x * x, axis=-1, keepdims=True) + EPS)


def _silu(x):
    return x * jax.nn.sigmoid(x)


def _per_group(x2d, fn):
    rows = x2d.shape[0]
    return fn(x2d.reshape(rows // GROUP, GROUP, D)).reshape(rows, D)


def _norm_mod(x, gain_ref, mod_ref, shift_idx):
    y = _rms(x) * gain_ref[...]
    scale = mod_ref[:, shift_idx + 1:shift_idx + 2, :]
    shift = mod_ref[:, shift_idx:shift_idx + 1, :]
    return _per_group(y, lambda y3: y3 * (1.0 + scale) + shift)


def _pack_bf16_pairs(x):
    lo = lax.bitcast_convert_type(x[:, :DP].astype(BF16).astype(F32), U32)
    hi = lax.bitcast_convert_type(x[:, DP:].astype(BF16).astype(F32), U32)
    return (lo >> 16) | (hi & U32(0xFFFF0000))


def _unpack_bf16_pairs(w):
    lo = lax.bitcast_convert_type(w << 16, F32)
    hi = lax.bitcast_convert_type(w & U32(0xFFFF0000), F32)
    return jnp.concatenate([lo, hi], axis=1)


def _add_moe(x, yg0_ref, yg1_ref, gates_ref, modp_ref):
    g = gates_ref[...]
    y = g[:, 0:1] * _unpack_bf16_pairs(yg0_ref[...]) + g[:, 1:2] * _unpack_bf16_pairs(yg1_ref[...])
    gate2 = modp_ref[:, 5:6, :]
    return x + _per_group(y, lambda y3: y3 * gate2)


ADA_TN = 1536


def _ada_kernel(c_ref, w_ref, b_ref, o_ref):
    c = c_ref[...]
    s = _silu(c).astype(BF16)
    o_ref[0] = jnp.dot(s, w_ref[0].astype(BF16), preferred_element_type=F32) + b_ref[0]


def _ada_call(c_all, ada_w, ada_b):
    nt = 6 * D // ADA_TN
    return pl.pallas_call(
        _ada_kernel,
        grid=(DEPTH, nt),
        in_specs=[
            pl.BlockSpec((N_SEQ, D), lambda i, j: (0, 0)),
            pl.BlockSpec((1, D, ADA_TN), lambda i, j: (i, 0, j)),
            pl.BlockSpec((1, 1, ADA_TN), lambda i, j: (i, 0, j)),
        ],
        out_specs=pl.BlockSpec((1, N_SEQ, ADA_TN), lambda i, j: (i, 0, j)),
        out_shape=jax.ShapeDtypeStruct((DEPTH, N_SEQ, 6 * D), F32),
        compiler_params=_params("parallel", "parallel"),
        name="ada_modulation",
    )(c_all, ada_w, ada_b.reshape(DEPTH, 1, 6 * D))


def _n_tiles(rows):
    return T_ALL // rows


def _n_prompt_tiles(rows):
    return T_PROMPT // rows


def _row_spec(rows, width):
    return pl.BlockSpec((rows, width), lambda i: (i, 0))


def _mod_spec(rows):
    return pl.BlockSpec((rows // GROUP, 6, D), lambda i: (i, 0, 0))


def _prev_specs(rows):
    return [_row_spec(rows, DP), _row_spec(rows, DP), _row_spec(rows, ROUTER_LANES), _mod_spec(rows)]


def _prompt_rows_spec(rows, width):
    last = _n_prompt_tiles(rows) - 1
    return pl.BlockSpec((rows, width), lambda i: (jnp.minimum(i, last), 0))


def _sample_rows_spec(rows, width):
    npt = _n_prompt_tiles(rows)
    return pl.BlockSpec((rows, width), lambda i: (jnp.maximum(i - npt, 0), 0))


def _ret_proj_kernel(has_prev, n_prompt_tiles, *refs):
    if has_prev:
        (x_ref, yg0_ref, yg1_ref, gates_ref, modp_ref, mod_ref, gain_ref, w_ref, cos_ref, sin_ref,
         xo_ref, p_ref) = refs
        x = _add_moe(x_ref[...], yg0_ref, yg1_ref, gates_ref, modp_ref)
    else:
        xp_ref, xs_ref, mod_ref, gain_ref, w_ref, cos_ref, sin_ref, xo_ref, p_ref = refs
        x = jnp.where(pl.program_id(0) < n_prompt_tiles, xp_ref[...], xs_ref[...])
    xo_ref[...] = x
    hb = _norm_mod(x, gain_ref, mod_ref, 0).astype(BF16)
    cos = cos_ref[...]
    sin = sin_ref[...]
    for j in range(2 * RET_HEADS):
        lo = j * RET_DK
        acc = jnp.dot(hb, w_ref[0, :, lo:lo + RET_DK], preferred_element_type=F32)
        x1 = acc[:, :ROPE_HALF]
        x2 = acc[:, ROPE_HALF:]
        r1 = x1 * cos - x2 * sin
        r2 = x1 * sin + x2 * cos
        if j >= RET_HEADS:
            r1 = r1 * (RET_DK ** -0.5)
            r2 = r2 * (RET_DK ** -0.5)
        p_ref[:, lo:lo + ROPE_HALF] = r1.astype(BF16)
        p_ref[:, lo + ROPE_HALF:lo + RET_DK] = r2.astype(BF16)
    for j in range(2 * RET_HEADS):
        lo = 2 * RET_QK + j * RET_DV
        p_ref[:, lo:lo + RET_DV] = jnp.dot(
            hb, w_ref[0, :, lo:lo + RET_DV], preferred_element_type=F32).astype(BF16)


def _ret_proj_call(x, prev, mod, gain, w_in, layer, cos, sin):
    rows = ROWS_WIDE
    has_prev = prev is not None
    if has_prev:
        in_specs = [_row_spec(rows, D)] + _prev_specs(rows)
        args = [x] + list(prev)
    else:
        in_specs = [_prompt_rows_spec(rows, D), _sample_rows_spec(rows, D)]
        args = list(x)
    in_specs += [_mod_spec(rows), _resident((1, D)), _resident_layer((D, RET_IN), layer),
                 _row_spec(rows, ROPE_HALF), _row_spec(rows, ROPE_HALF)]
    args += [mod, gain, w_in, cos, sin]
    return pl.pallas_call(
        functools.partial(_ret_proj_kernel, has_prev, _n_prompt_tiles(rows)),
        grid=(_n_tiles(rows),),
        in_specs=in_specs,
        out_specs=[_row_spec(rows, D), _row_spec(rows, RET_IN)],
        out_shape=[jax.ShapeDtypeStruct((T_ALL, D), F32), jax.ShapeDtypeStruct((T_ALL, RET_IN), BF16)],
        compiler_params=_params("parallel"),
        name="ret_proj",
    )(*args)


def _ret_core_kernel(has_s0, n_chunks, *refs):
    if has_s0:
        p_ref, s0_ref, intra_ref, qd_ref, kd_ref, cd_ref, y_ref, so_ref, s_ref = refs
    else:
        p_ref, intra_ref, qd_ref, kd_ref, cd_ref, y_ref, so_ref, s_ref = refs
    c = pl.program_id(1)

    @pl.when(c == 0)
    def _():
        if has_s0:
            s_ref[...] = s0_ref[0, 0]
        else:
            s_ref[...] = jnp.zeros_like(s_ref)

    for h in range(RET_HEADS):
        qb = p_ref[:, h * RET_DK:(h + 1) * RET_DK]
        kb = p_ref[:, RET_QK + h * RET_DK:RET_QK + (h + 1) * RET_DK]
        vb = p_ref[:, 2 * RET_QK + h * RET_DV:2 * RET_QK + (h + 1) * RET_DV]
        gb = p_ref[:, 2 * RET_QK + RET_V + h * RET_DV:2 * RET_QK + RET_V + (h + 1) * RET_DV]
        scores = lax.dot_general(qb, kb, (((1,), (1,)), ((), ())), preferred_element_type=F32)
        scores = scores * intra_ref[h]
        o = jnp.dot(scores.astype(BF16), vb, preferred_element_type=F32)
        s_old = s_ref[h]
        o = o + jnp.dot(qb, s_old.astype(BF16), preferred_element_type=F32) * qd_ref[h]
        kdec = (kb.astype(F32) * kd_ref[h]).astype(BF16)
        s_ref[h] = s_old * cd_ref[h][:, 0:1] + lax.dot_general(
            kdec, vb, (((0,), (0,)), ((), ())), preferred_element_type=F32)
        gf = gb.astype(F32)
        y_ref[:, h * RET_DV:(h + 1) * RET_DV] = (_silu(gf) * _rms(o)).astype(BF16)

    @pl.when(c == n_chunks - 1)
    def _():
        so_ref[0] = s_ref[...]


def _ret_decay_tables(cl):
    log_g = np.log1p(-np.exp2(-5.0 - np.arange(RET_HEADS, dtype=np.float64)))
    idx = np.arange(cl, dtype=np.float64)
    diff = idx[:, None] - idx[None, :]
    intra = np.where(diff >= 0, np.exp(log_g[:, None, None] * np.maximum(diff, 0.0)), 0.0)
    qd = np.exp(log_g[:, None] * (idx[None, :] + 1.0))[:, :, None]
    kd = np.exp(log_g[:, None] * (cl - 1.0 - idx[None, :]))[:, :, None]
    cd = np.broadcast_to(np.exp(log_g * cl)[:, None, None], (RET_HEADS, 1, 128))
    return tuple(jnp.asarray(a, F32) for a in (intra, qd, kd, cd))


def _ret_core_call(p, s0, layer, n_seq, seq_len, cl, row0):
    has_s0 = s0 is not None
    n_chunks = seq_len // cl
    rb0 = row0 // cl
    state_block = (1, RET_HEADS, RET_DK, RET_DV)
    in_specs = [pl.BlockSpec((cl, RET_IN), lambda b, c: (rb0 + b * n_chunks + c, 0))]
    args = [p]
    if has_s0:
        in_specs.append(pl.BlockSpec((1,) + state_block, lambda b, c: (layer, b, 0, 0, 0)))
        args.append(s0)
    in_specs += [_resident((RET_HEADS, cl, cl)), _resident((RET_HEADS, cl, 1)),
                 _resident((RET_HEADS, cl, 1)), _resident((RET_HEADS, 1, 128))]
    args += list(_ret_decay_tables(cl))
    return pl.pallas_call(
        functools.partial(_ret_core_kernel, has_s0, n_chunks),
        grid=(n_seq, n_chunks),
        in_specs=in_specs,
        out_specs=[pl.BlockSpec((cl, RET_V), lambda b, c: (b * n_chunks + c, 0)),
                   pl.BlockSpec(state_block, lambda b, c: (b, 0, 0, 0))],
        out_shape=[jax.ShapeDtypeStruct((n_seq * seq_len, RET_V), BF16),
                   jax.ShapeDtypeStruct((n_seq,) + state_block[1:], F32)],
        scratch_shapes=[pltpu.VMEM(state_block[1:], F32)],
        compiler_params=_params("parallel", "arbitrary"),
        name="ret_core",
    )(*args)


def _residual_router(acc, x_ref, mod_ref, gain_ref, wr_ref, br_ref, xo_ref, h2_ref, lg_ref):
    gate1 = mod_ref[:, 2:3, :]
    xn = x_ref[...] + _per_group(acc, lambda a3: a3 * gate1)
    xo_ref[...] = xn
    h2 = _norm_mod(xn, gain_ref, mod_ref, 3)
    h2_ref[...] = _pack_bf16_pairs(h2)
    h_hi = h2.astype(BF16)
    h_lo = (h2 - h_hi.astype(F32)).astype(BF16)
    hh = jnp.dot(h_hi, wr_ref[...], preferred_element_type=F32)
    lh = jnp.dot(h_lo, wr_ref[:, :ROUTER_LANES], preferred_element_type=F32)
    lg_ref[...] = hh[:, :ROUTER_LANES] + hh[:, ROUTER_LANES:] + lh + br_ref[...]


def _mix_out_specs(rows):
    return [_row_spec(rows, D), _row_spec(rows, DP), _row_spec(rows, ROUTER_LANES)]


_MIX_OUT_SHAPE = [
    jax.ShapeDtypeStruct((T_ALL, D), F32),
    jax.ShapeDtypeStruct((T_ALL, DP), U32),
    jax.ShapeDtypeStruct((T_ALL, ROUTER_LANES), F32),
]


def _router_specs():
    return [_resident((D, 2 * ROUTER_LANES)), _resident((1, ROUTER_LANES))]


def _ret_out_kernel(n_prompt_tiles, yp_ref, ys_ref, w_ref, x_ref, mod_ref, gain_ref, wr_ref, br_ref,
                    xo_ref, h2_ref, lg_ref):
    yin = jnp.where(pl.program_id(0) < n_prompt_tiles, yp_ref[...], ys_ref[...])
    acc = jnp.dot(yin, w_ref[0], preferred_element_type=F32)
    _residual_router(acc, x_ref, mod_ref, gain_ref, wr_ref, br_ref, xo_ref, h2_ref, lg_ref)


def _ret_out_call(y_prompt, y_sample, w_out, layer, x, mod, gain, w_r, b_r):
    rows = ROWS_WIDE
    return pl.pallas_call(
        functools.partial(_ret_out_kernel, _n_prompt_tiles(rows)),
        grid=(_n_tiles(rows),),
        in_specs=[_prompt_rows_spec(rows, RET_V), _sample_rows_spec(rows, RET_V),
                  _resident_layer((RET_V, D), layer), _row_spec(rows, D), _mod_spec(rows),
                  _resident((1, D))] + _router_specs(),
        out_specs=_mix_out_specs(rows),
        out_shape=_MIX_OUT_SHAPE,
        compiler_params=_params("parallel"),
        name="ret_out",
    )(y_prompt, y_sample, w_out, x, mod, gain, w_r, b_r)


GM_TN = 512


def _gelu_tanh(x):
    return 0.5 * x * (1.0 + jnp.tanh(np.sqrt(2.0 / np.pi) * (x + 0.044715 * (x * x * x))))


def _gm_proj_kernel(n_prompt_tiles, x_ref, yg0_ref, yg1_ref, gates_ref, modp_ref, mod_ref, gain_ref, w_ref,
                    b_ref, lg_ref, lb_ref, xo_ref, u_ref, v_ref, vs_ref, vacc_ref):
    x = _add_moe(x_ref[...], yg0_ref, yg1_ref, gates_ref, modp_ref)
    xo_ref[...] = x
    hb = _norm_mod(x, gain_ref, mod_ref, 0).astype(BF16)
    for j in range(GM_FFN // GM_TN):
        lo = j * GM_TN
        z = jnp.dot(hb, w_ref[0, :, lo:lo + GM_TN], preferred_element_type=F32) + b_ref[:, lo:lo + GM_TN]
        z = _gelu_tanh(z)
        if lo < GM_HALF:
            u_ref[:, lo:lo + GM_TN] = z.astype(BF16)
        else:
            vacc_ref[:, lo - GM_HALF:lo - GM_HALF + GM_TN] = z
    v = vacc_ref[...]
    mu = jnp.mean(v, axis=-1, keepdims=True)
    vc = v - mu
    var = jnp.mean(vc * vc, axis=-1, keepdims=True)
    vn = vc * lax.rsqrt(var + EPS) * lg_ref[...] + lb_ref[...]
    v_ref[...] = vn.astype(BF16)

    @pl.when(pl.program_id(0) >= n_prompt_tiles)
    def _():
        vs_ref[...] = vn


def _gm_proj_call(x, prev, mod, gain, w_in, layer, b_in, ln_g, ln_b):
    rows = ROWS_GM_PROJ
    in_specs = [_row_spec(rows, D)] + _prev_specs(rows) + [
        _mod_spec(rows), _resident((1, D)), _resident_layer((D, GM_FFN), layer), _resident((1, GM_FFN)),
        _resident((1, GM_HALF)), _resident((1, GM_HALF))]
    return pl.pallas_call(
        functools.partial(_gm_proj_kernel, _n_prompt_tiles(rows)),
        grid=(_n_tiles(rows),),
        in_specs=in_specs,
        out_specs=[_row_spec(rows, D), _row_spec(rows, GM_HALF), _row_spec(rows, GM_HALF),
                   _sample_rows_spec(rows, GM_HALF)],
        out_shape=[jax.ShapeDtypeStruct((T_ALL, D), F32),
                   jax.ShapeDtypeStruct((T_ALL, GM_HALF), BF16),
                   jax.ShapeDtypeStruct((T_ALL, GM_HALF), BF16),
                   jax.ShapeDtypeStruct((T_SAMPLE, GM_HALF), F32)],
        scratch_shapes=[pltpu.VMEM((rows, GM_HALF), F32)],
        compiler_params=_params("arbitrary"),
        name="gm_proj",
    )(x, *prev, mod, gain, w_in, b_in, ln_g, ln_b)


def _gm_out_kernel(u_ref, v_ref, ws_ref, bs_ref, w_ref, bo_ref, x_ref, mod_ref, gain_ref, wr_ref, br_ref,
                   xo_ref, h2_ref, lg_ref):
    rows = u_ref.shape[0]
    pieces = []
    for r0 in range(0, rows, GM_MIX):
        acc = jnp.zeros((GM_MIX, D), F32)
        for g in range(GM_GROUPS):
            lo = g * GM_GDIM
            sp = jnp.dot(ws_ref[0, g], v_ref[r0:r0 + GM_MIX, lo:lo + GM_GDIM],
                         preferred_element_type=F32) + bs_ref[0, g]
            gated = (u_ref[r0:r0 + GM_MIX, lo:lo + GM_GDIM].astype(F32) * sp).astype(BF16)
            acc = acc + jnp.dot(gated, w_ref[0, lo:lo + GM_GDIM, :], preferred_element_type=F32)
        pieces.append(acc)
    acc = jnp.concatenate(pieces, axis=0) + bo_ref[...]
    _residual_router(acc, x_ref, mod_ref, gain_ref, wr_ref, br_ref, xo_ref, h2_ref, lg_ref)


def _gm_block_diag(w_s, b_s):
    mats, biases = [], []
    for cl in (GM_CHUNK, DEC_SEQ):
        tri = jnp.tril(jnp.ones((cl, cl), bool))
        blk = jnp.where(tri[None], w_s[:, :cl, :cl], 0.0)
        reps = GM_MIX // cl
        eye = jnp.eye(reps, dtype=w_s.dtype)
        bd = jnp.einsum("ab,gts->gatbs", eye, blk).reshape(GM_GROUPS, GM_MIX, GM_MIX)
        mats.append(bd)
        biases.append(jnp.tile(b_s[:, :cl], (1, reps))[:, :, None])
    return jnp.stack(mats).astype(BF16), jnp.stack(biases).astype(F32)


def _gm_out_call(u, v, ws_bd, bs_bd, w_out, layer, b_out, x, mod, gain, w_r, b_r):
    rows = ROWS_WIDE
    npt = _n_prompt_tiles(rows)

    def variant(i):
        return jnp.where(i >= npt, 1, 0)

    return pl.pallas_call(
        _gm_out_kernel,
        grid=(_n_tiles(rows),),
        in_specs=[_row_spec(rows, GM_HALF), _row_spec(rows, GM_HALF),
                  pl.BlockSpec((1, GM_GROUPS, GM_MIX, GM_MIX), lambda i: (variant(i), 0, 0, 0)),
                  pl.BlockSpec((1, GM_GROUPS, GM_MIX, 1), lambda i: (variant(i), 0, 0, 0)),
                  _resident_layer((GM_HALF, D), layer), _resident((1, D)), _row_spec(rows, D),
                  _mod_spec(rows), _resident((1, D))] + _router_specs(),
        out_specs=_mix_out_specs(rows),
        out_shape=_MIX_OUT_SHAPE,
        compiler_params=_params("arbitrary"),
        name="gm_out",
    )(u, v, ws_bd, bs_bd, w_out, b_out, x, mod, gain, w_r, b_r)


def _expert_kernel(be_ref, bv_ref, nb_ref, xb_ref, w1_ref, w3_ref, w2_ref, yb_ref, w1s, w3s, w2s):
    b = pl.program_id(0)

    @pl.when(b < nb_ref[0])
    def _():
        prev_e = be_ref[jnp.maximum(b - 1, 0)]

        @pl.when((b == 0) | (be_ref[b] != prev_e))
        def _():
            w1s[...] = w1_ref[0, 0].astype(BF16)
            w3s[...] = w3_ref[0, 0].astype(BF16)
            w2s[...] = w2_ref[0, 0].astype(BF16)

        row = lax.broadcasted_iota(jnp.int32, (EXP_BLOCK, 1), 0)
        xw = jnp.where(row < bv_ref[b], xb_ref[...], U32(0))
        x = _unpack_bf16_pairs(xw).astype(BF16)
        a = jnp.dot(x, w1s[...], preferred_element_type=F32)
        c = jnp.dot(x, w3s[...], preferred_element_type=F32)
        h = (_silu(a) * c).astype(BF16)
        yb_ref[...] = _pack_bf16_pairs(jnp.dot(h, w2s[...], preferred_element_type=F32))


def _expert_call(blk_e, blk_valid, n_blk, xb, w1, w3, w2, layer):
    def blk(b, be, bv, nb):
        return jnp.minimum(b, nb[0] - 1)

    def w_idx(b, be, bv, nb):
        return (layer, be[blk(b, be, bv, nb)], 0, 0)

    grid_spec = pltpu.PrefetchScalarGridSpec(
        num_scalar_prefetch=3,
        grid=(N_EXP_BLOCKS,),
        in_specs=[
            pl.BlockSpec((EXP_BLOCK, DP), lambda b, be, bv, nb: (blk(b, be, bv, nb), 0)),
            pl.BlockSpec((1, 1, D, MOE_HIDDEN), w_idx),
            pl.BlockSpec((1, 1, D, MOE_HIDDEN), w_idx),
            pl.BlockSpec((1, 1, MOE_HIDDEN, D), w_idx),
        ],
        out_specs=pl.BlockSpec((EXP_BLOCK, DP), lambda b, be, bv, nb: (blk(b, be, bv, nb), 0)),
        scratch_shapes=[pltpu.VMEM((D, MOE_HIDDEN), BF16), pltpu.VMEM((D, MOE_HIDDEN), BF16),
                        pltpu.VMEM((MOE_HIDDEN, D), BF16)],
    )
    return pl.pallas_call(
        _expert_kernel,
        grid_spec=grid_spec,
        out_shape=jax.ShapeDtypeStruct((P_ROWS, DP), U32),
        compiler_params=_params("arbitrary"),
        name="experts",
    )(blk_e, blk_valid, n_blk, xb, w1, w3, w2)


def _route_kernel(lg_ref, dest_ref, gates_ref, meta_ref, cnt_ref, base_ref):
    ph = pl.program_id(0)
    t = pl.program_id(1)
    tm = ROUTE_TM
    lt = lg_ref[...].T
    el = lt[0:MOE_EXPERTS]
    gl = lt[MOE_EXPERTS:MOE_EXPERTS + 8]
    gidx = lax.broadcasted_iota(jnp.int32, (8, tm), 0)
    neg = jnp.float32(-jnp.inf)
    gl = jnp.where(gidx < MOE_GROUPS, gl, neg)
    gmax = jnp.max(gl, axis=0, keepdims=True)
    grp = jnp.min(jnp.where(gl == gmax, gidx, MOE_GROUPS), axis=0, keepdims=True)
    eidx = lax.broadcasted_iota(jnp.int32, (MOE_EXPERTS, tm), 0)
    els = jnp.where((eidx >> 3) == grp, el, neg)
    m1 = jnp.max(els, axis=0, keepdims=True)
    i1 = jnp.min(jnp.where(els == m1, eidx, MOE_EXPERTS), axis=0, keepdims=True)
    els2 = jnp.where(eidx == i1, neg, els)
    m2 = jnp.max(els2, axis=0, keepdims=True)
    i2 = jnp.min(jnp.where(els2 == m2, eidx, MOE_EXPERTS), axis=0, keepdims=True)
    sel1 = eidx == i1
    sel2 = eidx == i2
    cnt = jnp.where(sel1 | sel2, 1.0, 0.0)
    tile_counts = jnp.sum(cnt, axis=1, keepdims=True)

    @pl.when(ph == 0)
    def _():
        @pl.when(t == 0)
        def _():
            cnt_ref[...] = jnp.zeros_like(cnt_ref)

        cnt_ref[...] += tile_counts

    @pl.when(ph == 1)
    def _():
        @pl.when(t == 0)
        def _():
            counts = cnt_ref[...]
            nblk = jnp.floor((counts + (EXP_BLOCK - 1.0)) * (1.0 / EXP_BLOCK))
            r = lax.broadcasted_iota(jnp.int32, (MOE_EXPERTS, MOE_EXPERTS), 0)
            c = lax.broadcasted_iota(jnp.int32, (MOE_EXPERTS, MOE_EXPERTS), 1)
            nblk_row = jnp.sum(jnp.where(r == c, nblk, 0.0), axis=0, keepdims=True)
            bstart = jnp.sum(jnp.where(c < r, nblk_row, 0.0), axis=1, keepdims=True)
            base_ref[...] = bstart * EXP_BLOCK
            bend = bstart + nblk
            bidx = lax.broadcasted_iota(jnp.int32, (1, META_LANES), 1).astype(F32)
            blk_e = jnp.minimum(jnp.sum(jnp.where(bidx >= bend, 1.0, 0.0), axis=0, keepdims=True),
                                MOE_EXPERTS - 1.0)
            erow = lax.broadcasted_iota(jnp.int32, (MOE_EXPERTS, META_LANES), 0).astype(F32)
            mine = erow == blk_e
            cnt_b = jnp.sum(jnp.where(mine, counts, 0.0), axis=0, keepdims=True)
            start_b = jnp.sum(jnp.where(mine, bstart, 0.0), axis=0, keepdims=True)
            valid = jnp.clip(cnt_b - (bidx - start_b) * EXP_BLOCK, 0.0, float(EXP_BLOCK))
            n_blk = jnp.sum(nblk, axis=0, keepdims=True)
            mrow = lax.broadcasted_iota(jnp.int32, (8, META_LANES), 0)
            meta = jnp.where(mrow == 0, blk_e, jnp.where(mrow == 1, valid, jnp.where(mrow == 2, n_blk, 0.0)))
            meta_ref[...] = meta.astype(jnp.int32)

        base = base_ref[...]
        before = (lax.broadcasted_iota(jnp.int32, (tm, tm), 0) < lax.broadcasted_iota(jnp.int32, (tm, tm), 1))
        prefix = jnp.dot(cnt.astype(BF16), jnp.where(before, 1.0, 0.0).astype(BF16),
                         preferred_element_type=F32)
        pos = base + prefix
        d1 = jnp.sum(jnp.where(sel1, pos, 0.0), axis=0, keepdims=True)
        d2 = jnp.sum(jnp.where(sel2, pos, 0.0), axis=0, keepdims=True)
        dest_ref[...] = jnp.concatenate([d1, d2], axis=0).astype(jnp.int32)
        base_ref[...] = base + tile_counts
        g_w = 1.0 / jnp.sum(jnp.exp(gl - gmax), axis=0, keepdims=True)
        e21 = jnp.exp(m2 - m1)
        p1 = 1.0 / (1.0 + e21)
        rid = lax.broadcasted_iota(jnp.int32, (ROUTER_LANES, tm), 0)
        gt = jnp.where(rid == 0, g_w * p1, jnp.where(rid == 1, g_w * (e21 * p1), 0.0))
        gates_ref[...] = gt.T


def _route_call(logits):
    return pl.pallas_call(
        _route_kernel,
        grid=(2, N_ROUTE_TILES),
        in_specs=[pl.BlockSpec((ROUTE_TM, ROUTER_LANES), lambda ph, t: (t, 0))],
        out_specs=[pl.BlockSpec((MOE_TOPK, ROUTE_TM), lambda ph, t: (0, t * ph)),
                   pl.BlockSpec((ROUTE_TM, ROUTER_LANES), lambda ph, t: (t * ph, 0)),
                   pl.BlockSpec((8, META_LANES), lambda ph, t: (0, 0))],
        out_shape=[jax.ShapeDtypeStruct((MOE_TOPK, T_ALL), jnp.int32),
                   jax.ShapeDtypeStruct((T_ALL, ROUTER_LANES), F32),
                   jax.ShapeDtypeStruct((8, META_LANES), jnp.int32)],
        scratch_shapes=[pltpu.VMEM((MOE_EXPERTS, 1), F32), pltpu.VMEM((MOE_EXPERTS, 1), F32)],
        compiler_params=_params("arbitrary", "arbitrary"),
        name="route",
    )(logits)


def _sc_mesh():
    return plsc.VectorSubcoreMesh(core_axis_name="c", subcore_axis_name="s")


def _sc_token_offset(j):
    wid = lax.axis_index("s") * SC_CORES + lax.axis_index("c")
    return pl.multiple_of(wid * SC_ROWS_PER_WORKER + j * SC_CHUNK, 8)


def _dispatch_body(h_hbm, d0_hbm, d1_hbm, out_hbm, i0_v, i1_v, rows_v, sem0, sem1, sem2):
    @pl.loop(0, SC_ROWS_PER_WORKER // SC_CHUNK)
    def _(j):
        off = _sc_token_offset(j)
        c0 = pltpu.async_copy(d0_hbm.at[pl.ds(off, SC_CHUNK)], i0_v, sem0)
        c1 = pltpu.async_copy(d1_hbm.at[pl.ds(off, SC_CHUNK)], i1_v, sem1)
        c2 = pltpu.async_copy(h_hbm.at[pl.ds(off, SC_CHUNK)], rows_v, sem2)
        c0.wait()
        c1.wait()
        c2.wait()
        s0 = pltpu.async_copy(rows_v, out_hbm.at[i0_v], sem0)
        s1 = pltpu.async_copy(rows_v, out_hbm.at[i1_v], sem1)
        s0.wait()
        s1.wait()


def _dispatch_call(h2, dest0, dest1):
    return pl.kernel(
        _dispatch_body,
        out_type=jax.ShapeDtypeStruct((P_ROWS, DP), U32),
        mesh=_sc_mesh(),
        scratch_types=[pltpu.VMEM((SC_CHUNK,), jnp.int32), pltpu.VMEM((SC_CHUNK,), jnp.int32),
                       pltpu.VMEM((SC_CHUNK, DP), U32),
                       pltpu.SemaphoreType.DMA, pltpu.SemaphoreType.DMA, pltpu.SemaphoreType.DMA],
        name="moe_dispatch",
    )(h2, dest0, dest1)


def _combine_body(yb_hbm, d0_hbm, d1_hbm, o0_hbm, o1_hbm, i0_v, i1_v, r0_v, r1_v, sem0, sem1):
    @pl.loop(0, SC_ROWS_PER_WORKER // SC_CHUNK)
    def _(j):
        off = _sc_token_offset(j)
        c0 = pltpu.async_copy(d0_hbm.at[pl.ds(off, SC_CHUNK)], i0_v, sem0)
        c1 = pltpu.async_copy(d1_hbm.at[pl.ds(off, SC_CHUNK)], i1_v, sem1)
        c0.wait()
        c1.wait()
        g0 = pltpu.async_copy(yb_hbm.at[i0_v], r0_v, sem0)
        g1 = pltpu.async_copy(yb_hbm.at[i1_v], r1_v, sem1)
        g0.wait()
        g1.wait()
        w0 = pltpu.async_copy(r0_v, o0_hbm.at[pl.ds(off, SC_CHUNK)], sem0)
        w1 = pltpu.async_copy(r1_v, o1_hbm.at[pl.ds(off, SC_CHUNK)], sem1)
        w0.wait()
        w1.wait()


def _combine_call(yb, dest0, dest1):
    out = jax.ShapeDtypeStruct((T_ALL, DP), U32)
    return pl.kernel(
        _combine_body,
        out_type=(out, out),
        mesh=_sc_mesh(),
        scratch_types=[pltpu.VMEM((SC_CHUNK,), jnp.int32), pltpu.VMEM((SC_CHUNK,), jnp.int32),
                       pltpu.VMEM((SC_CHUNK, DP), U32), pltpu.VMEM((SC_CHUNK, DP), U32),
                       pltpu.SemaphoreType.DMA, pltpu.SemaphoreType.DMA],
        name="moe_combine",
    )(yb, dest0, dest1)


def _moe(h2, logits, w1, w3, w2, layer):
    dest, gates, meta = _route_call(logits)
    dest0, dest1 = dest[0], dest[1]
    xb = _dispatch_call(h2, dest0, dest1)
    yb = _expert_call(meta[0, :N_EXP_BLOCKS], meta[1, :N_EXP_BLOCKS], meta[2, :1], xb, w1, w3, w2, layer)
    yg0, yg1 = _combine_call(yb, dest0, dest1)
    return yg0, yg1, gates


def _final_kernel(x_ref, yg0_ref, yg1_ref, gates_ref, modp_ref, gain_ref, o_ref):
    x = _add_moe(x_ref[...], yg0_ref, yg1_ref, gates_ref, modp_ref)
    o_ref[...] = _rms(x) * gain_ref[...]


def _final_call(x, prev, gain, row0, n_rows):
    rows = ROWS_WIDE
    tile0 = row0 // rows

    def tile(width):
        return pl.BlockSpec((rows, width), lambda i: (tile0 + i, 0))

    return pl.pallas_call(
        _final_kernel,
        grid=(n_rows // rows,),
        in_specs=[tile(D), tile(DP), tile(DP), tile(ROUTER_LANES),
                  pl.BlockSpec((rows // GROUP, 6, D), lambda i: (tile0 + i, 0, 0)), _resident((1, D))],
        out_specs=pl.BlockSpec((rows, D), lambda i: (i, 0)),
        out_shape=jax.ShapeDtypeStruct((n_rows, D), F32),
        compiler_params=_params("parallel"),
        name="final_norm",
    )(x, *prev, gain)


def _rope_tables():
    pos = np.concatenate([np.tile(np.arange(SEQ), BATCH),
                          np.tile(PAST_LEN + np.arange(DEC_SEQ), DEC_BATCH)]).astype(np.float32)
    inv = (ROPE_BASE ** (-np.arange(ROPE_HALF, dtype=np.float32) / ROPE_HALF)).astype(np.float32)
    ang = (pos[:, None] * inv[None, :]).astype(np.float32).astype(np.float64)
    return jnp.asarray(np.cos(ang), F32), jnp.asarray(np.sin(ang), F32)


def kernel(x_prompt, x_sample, c_prompt, c_sample, state_ret, ada_w, ada_b, norm1_g, norm2_g, ret_w_in,
           ret_w_out, gm_w_in, gm_b_in, gm_ln_g, gm_ln_b, gm_w_s, gm_b_s, gm_w_out, gm_b_out, moe_w_rg,
           moe_b_rg, moe_w_re, moe_b_re, moe_w1, moe_w3, moe_w2, final_g):
    x = (x_prompt.reshape(T_PROMPT, D), x_sample.reshape(T_SAMPLE, D))
    c_all = jnp.concatenate([c_prompt, c_sample], axis=0)
    mod_all = _ada_call(c_all, ada_w, ada_b).reshape(DEPTH, N_SEQ, 6, D)
    seq_of_group = np.concatenate([np.repeat(np.arange(BATCH), SEQ // GROUP),
                                   BATCH + np.repeat(np.arange(DEC_BATCH), DEC_SEQ // GROUP)])
    cos, sin = _rope_tables()
    ret_w_in_b, ret_w_out_b = ret_w_in.astype(BF16), ret_w_out.astype(BF16)
    gm_w_in_b, gm_w_out_b = gm_w_in.astype(BF16), gm_w_out.astype(BF16)

    ret_prompt, ret_sample, gm_sample = [], [], []
    prev = None
    for i in range(DEPTH):
        j = i // 2
        mod = mod_all[i][seq_of_group]
        g1 = norm1_g[i].reshape(1, D)
        g2 = norm2_g[i].reshape(1, D)
        w_r = jnp.pad(jnp.concatenate([moe_w_re[i], moe_w_rg[i]], axis=1),
                      ((0, 0), (0, ROUTER_LANES - MOE_GROUPS - MOE_EXPERTS)))
        w_r_hi = w_r.astype(BF16)
        w_r_lo = (w_r - w_r_hi.astype(F32)).astype(BF16)
        w_r = jnp.concatenate([w_r_hi, w_r_lo], axis=1)
        b_r = jnp.pad(jnp.concatenate([moe_b_re[i].reshape(-1), moe_b_rg[i]]),
                      (0, ROUTER_LANES - MOE_GROUPS - MOE_EXPERTS)).reshape(1, ROUTER_LANES)
        if i % 2 == 0:
            x, p = _ret_proj_call(x, prev, mod, g1, ret_w_in_b, j, cos, sin)
            y_p, s_p = _ret_core_call(p, None, j, BATCH, SEQ, RET_CHUNK_PROMPT, 0)
            y_s, s_s = _ret_core_call(p, state_ret, j, DEC_BATCH, DEC_SEQ, RET_CHUNK_SAMPLE, T_PROMPT)
            ret_prompt.append(s_p)
            ret_sample.append(s_s)
            x, h2, logits = _ret_out_call(y_p, y_s, ret_w_out_b, j, x, mod, g2, w_r, b_r)
        else:
            x, u, v, vs = _gm_proj_call(x, prev, mod, g1, gm_w_in_b, j,
                                        gm_b_in[j].reshape(1, GM_FFN), gm_ln_g[j].reshape(1, GM_HALF),
                                        gm_ln_b[j].reshape(1, GM_HALF))
            gm_sample.append(vs.reshape(DEC_BATCH, DEC_SEQ, GM_HALF))
            ws_bd, bs_bd = _gm_block_diag(gm_w_s[j], gm_b_s[j])
            x, h2, logits = _gm_out_call(u, v, ws_bd, bs_bd, gm_w_out_b, j,
                                         gm_b_out[j].reshape(1, D), x, mod, g2, w_r, b_r)
        yg0, yg1, gates = _moe(h2, logits, moe_w1, moe_w3, moe_w2, i)
        prev = (yg0, yg1, gates, mod)

    fg = final_g.reshape(1, D)
    y_prompt = _final_call(x, prev, fg, 0, T_PROMPT).reshape(BATCH, SEQ, D)
    y_sample = _final_call(x, prev, fg, T_PROMPT, T_SAMPLE).reshape(DEC_BATCH, DEC_SEQ, D)
    return (y_prompt, y_sample, jnp.stack(ret_prompt), jnp.stack(ret_sample), jnp.stack(gm_sample))
```

```python
import functools

import numpy as np
import jax
import jax.numpy as jnp
from jax import lax
from jax.experimental import pallas as pl
from jax.experimental.pallas import tpu as pltpu
from jax.experimental.pallas import tpu_sc as plsc

F32 = jnp.float32
BF16 = jnp.bfloat16
U32 = jnp.uint32

D = 1024
BATCH, SEQ = 4, 4096
DEC_BATCH, DEC_SEQ = 16, 64
PAST_LEN = 4096
DEPTH = 4
N_RET = (DEPTH + 1) // 2
N_GM = DEPTH // 2
N_SEQ = BATCH + DEC_BATCH

RET_HEADS, RET_DK, RET_DV = 4, 256, 512
RET_QK = RET_HEADS * RET_DK
RET_V = RET_HEADS * RET_DV
RET_IN = 2 * RET_QK + 2 * RET_V
ROPE_BASE = 10000.0
ROPE_HALF = RET_DK // 2

GM_FFN = 6 * D
GM_HALF = GM_FFN // 2
GM_GROUPS = 4
GM_GDIM = GM_HALF // GM_GROUPS
GM_CHUNK = 128

MOE_GROUPS, MOE_PER_GROUP = 4, 8
MOE_EXPERTS = MOE_GROUPS * MOE_PER_GROUP
MOE_TOPK = 2
MOE_HIDDEN = 512
EPS = 1e-6

GROUP = 64
T_PROMPT = BATCH * SEQ
T_SAMPLE = DEC_BATCH * DEC_SEQ
T_ALL = T_PROMPT + T_SAMPLE
N_GROUPS = T_ALL // GROUP
ROWS_WIDE = 512
ROWS_GM_PROJ = 256

RET_CHUNK_PROMPT = 256
RET_CHUNK_SAMPLE = DEC_SEQ

GM_MIX = 256

EXP_BLOCK = 256
N_ASSIGN = T_ALL * MOE_TOPK
N_EXP_BLOCKS = -(-(N_ASSIGN + MOE_EXPERTS * (EXP_BLOCK - 1)) // EXP_BLOCK)
P_ROWS = N_EXP_BLOCKS * EXP_BLOCK
ROUTER_LANES = 128
ROUTE_TM = 1024
N_ROUTE_TILES = T_ALL // ROUTE_TM
META_LANES = 256
assert META_LANES >= N_EXP_BLOCKS

DP = D // 2
SC_CORES, SC_SUBCORES = 2, 16
SC_WORKERS = SC_CORES * SC_SUBCORES
SC_ROWS_PER_WORKER = T_ALL // SC_WORKERS
SC_CHUNK = 32
assert SC_ROWS_PER_WORKER % SC_CHUNK == 0 and SC_CHUNK % 8 == 0

V7X_VMEM_LIMIT_BYTES = 56 * 1024 * 1024


def _params(*sem):
    return pltpu.CompilerParams(dimension_semantics=sem, vmem_limit_bytes=V7X_VMEM_LIMIT_BYTES)


def _resident(shape):
    nd = len(shape)
    return pl.BlockSpec(shape, lambda *_: (0,) * nd, pipeline_mode=pl.Buffered(1))


def _resident_layer(shape, layer):
    nd = len(shape)
    return pl.BlockSpec((1,) + shape, lambda *_: (layer,) + (0,) * nd, pipeline_mode=pl.Buffered(1))


def _rms(x):
    return x * lax.rsqrt(jnp.mean(x * x, axis=-1, keepdims=True) + EPS)


def _silu(x):
    return x * jax.nn.sigmoid(x)


def _per_group(x2d, fn):
    rows = x2d.shape[0]
    return fn(x2d.reshape(rows // GROUP, GROUP, D)).reshape(rows, D)


def _norm_mod(x, gain_ref, mod_ref, shift_idx):
    y = _rms(x) * gain_ref[...]
    scale = mod_ref[:, shift_idx + 1:shift_idx + 2, :]
    shift = mod_ref[:, shift_idx:shift_idx + 1, :]
    return _per_group(y, lambda y3: y3 * (1.0 + scale) + shift)


def _pack_bf16_pairs(x):
    lo = lax.bitcast_convert_type(x[:, :DP].astype(BF16).astype(F32), U32)
    hi = lax.bitcast_convert_type(x[:, DP:].astype(BF16).astype(F32), U32)
    return (lo >> 16) | (hi & U32(0xFFFF0000))


def _unpack_bf16_pairs(w):
    lo = lax.bitcast_convert_type(w << 16, F32)
    hi = lax.bitcast_convert_type(w & U32(0xFFFF0000), F32)
    return jnp.concatenate([lo, hi], axis=1)


def _add_moe(x, yg0_ref, yg1_ref, gates_ref, modp_ref):
    g = gates_ref[...]
    y = g[:, 0:1] * _unpack_bf16_pairs(yg0_ref[...]) + g[:, 1:2] * _unpack_bf16_pairs(yg1_ref[...])
    gate2 = modp_ref[:, 5:6, :]
    return x + _per_group(y, lambda y3: y3 * gate2)


ADA_TN = 1536


def _ada_kernel(c_ref, w_ref, b_ref, o_ref):
    c = c_ref[...]
    s = _silu(c).astype(BF16)
    o_ref[0] = jnp.dot(s, w_ref[0].astype(BF16), preferred_element_type=F32) + b_ref[0]


def _ada_call(c_all, ada_w, ada_b):
    nt = 6 * D // ADA_TN
    return pl.pallas_call(
        _ada_kernel,
        grid=(DEPTH, nt),
        in_specs=[
            pl.BlockSpec((N_SEQ, D), lambda i, j: (0, 0)),
            pl.BlockSpec((1, D, ADA_TN), lambda i, j: (i, 0, j)),
            pl.BlockSpec((1, 1, ADA_TN), lambda i, j: (i, 0, j)),
        ],
        out_specs=pl.BlockSpec((1, N_SEQ, ADA_TN), lambda i, j: (i, 0, j)),
        out_shape=jax.ShapeDtypeStruct((DEPTH, N_SEQ, 6 * D), F32),
        compiler_params=_params("parallel", "parallel"),
        name="ada_modulation",
    )(c_all, ada_w, ada_b.reshape(DEPTH, 1, 6 * D))


def _n_tiles(rows):
    return T_ALL // rows


def _n_prompt_tiles(rows):
    return T_PROMPT // rows


def _row_spec(rows, width):
    return pl.BlockSpec((rows, width), lambda i: (i, 0))


def _mod_spec(rows):
    return pl.BlockSpec((rows // GROUP, 6, D), lambda i: (i, 0, 0))


def _prev_specs(rows):
    return [_row_spec(rows, DP), _row_spec(rows, DP), _row_spec(rows, ROUTER_LANES), _mod_spec(rows)]


def _prompt_rows_spec(rows, width):
    last = _n_prompt_tiles(rows) - 1
    return pl.BlockSpec((rows, width), lambda i: (jnp.minimum(i, last), 0))


def _sample_rows_spec(rows, width):
    npt = _n_prompt_tiles(rows)
    return pl.BlockSpec((rows, width), lambda i: (jnp.maximum(i - npt, 0), 0))


def _ret_proj_kernel(has_prev, n_prompt_tiles, *refs):
    if has_prev:
        (x_ref, yg0_ref, yg1_ref, gates_ref, modp_ref, mod_ref, gain_ref, w_ref, cos_ref, sin_ref,
         xo_ref, p_ref) = refs
        x = _add_moe(x_ref[...], yg0_ref, yg1_ref, gates_ref, modp_ref)
    else:
        xp_ref, xs_ref, mod_ref, gain_ref, w_ref, cos_ref, sin_ref, xo_ref, p_ref = refs
        x = jnp.where(pl.program_id(0) < n_prompt_tiles, xp_ref[...], xs_ref[...])
    xo_ref[...] = x
    hb = _norm_mod(x, gain_ref, mod_ref, 0).astype(BF16)
    cos = cos_ref[...]
    sin = sin_ref[...]
    for j in range(2 * RET_HEADS):
        lo = j * RET_DK
        acc = jnp.dot(hb, w_ref[0, :, lo:lo + RET_DK], preferred_element_type=F32)
        x1 = acc[:, :ROPE_HALF]
        x2 = acc[:, ROPE_HALF:]
        r1 = x1 * cos - x2 * sin
        r2 = x1 * sin + x2 * cos
        if j >= RET_HEADS:
            r1 = r1 * (RET_DK ** -0.5)
            r2 = r2 * (RET_DK ** -0.5)
        p_ref[:, lo:lo + ROPE_HALF] = r1.astype(BF16)
        p_ref[:, lo + ROPE_HALF:lo + RET_DK] = r2.astype(BF16)
    for j in range(2 * RET_HEADS):
        lo = 2 * RET_QK + j * RET_DV
        p_ref[:, lo:lo + RET_DV] = jnp.dot(
            hb, w_ref[0, :, lo:lo + RET_DV], preferred_element_type=F32).astype(BF16)


def _ret_proj_call(x, prev, mod, gain, w_in, layer, cos, sin):
    rows = ROWS_WIDE
    has_prev = prev is not None
    if has_prev:
        in_specs = [_row_spec(rows, D)] + _prev_specs(rows)
        args = [x] + list(prev)
    else:
        in_specs = [_prompt_rows_spec(rows, D), _sample_rows_spec(rows, D)]
        args = list(x)
    in_specs += [_mod_spec(rows), _resident((1, D)), _resident_layer((D, RET_IN), layer),
                 _row_spec(rows, ROPE_HALF), _row_spec(rows, ROPE_HALF)]
    args += [mod, gain, w_in, cos, sin]
    return pl.pallas_call(
        functools.partial(_ret_proj_kernel, has_prev, _n_prompt_tiles(rows)),
        grid=(_n_tiles(rows),),
        in_specs=in_specs,
        out_specs=[_row_spec(rows, D), _row_spec(rows, RET_IN)],
        out_shape=[jax.ShapeDtypeStruct((T_ALL, D), F32), jax.ShapeDtypeStruct((T_ALL, RET_IN), BF16)],
        compiler_params=_params("parallel"),
        name="ret_proj",
    )(*args)


def _ret_core_kernel(has_s0, n_chunks, layer, *refs):
    refs = list(refs)
    p_ref = refs.pop(0)
    s0_ref = refs.pop(0) if has_s0 else None
    intra_ref, qd_ref, kd_ref, cd_ref = refs[:4]
    y_ref, so_ref, s_ref = refs[-3:]
    c = pl.program_id(1)

    @pl.when(c == 0)
    def _():
        if has_s0:
            s_ref[...] = s0_ref[0, 0]
        else:
            s_ref[...] = jnp.zeros_like(s_ref)

    for h in range(RET_HEADS):
        qb = p_ref[:, h * RET_DK:(h + 1) * RET_DK]
        kb = p_ref[:, RET_QK + h * RET_DK:RET_QK + (h + 1) * RET_DK]
        vb = p_ref[:, 2 * RET_QK + h * RET_DV:2 * RET_QK + (h + 1) * RET_DV]
        gb = p_ref[:, 2 * RET_QK + RET_V + h * RET_DV:2 * RET_QK + RET_V + (h + 1) * RET_DV]
        scores = lax.dot_general(qb, kb, (((1,), (1,)), ((), ())), preferred_element_type=F32)
        scores = scores * intra_ref[h]
        o = jnp.dot(scores.astype(BF16), vb, preferred_element_type=F32)
        s_old = s_ref[h]
        o = o + jnp.dot(qb, s_old.astype(BF16), preferred_element_type=F32) * qd_ref[h]
        kdec = (kb.astype(F32) * kd_ref[h]).astype(BF16)
        s_ref[h] = s_old * cd_ref[h][:, 0:1] + lax.dot_general(
            kdec, vb, (((0,), (0,)), ((), ())), preferred_element_type=F32)
        gf = gb.astype(F32)
        y_ref[:, h * RET_DV:(h + 1) * RET_DV] = (_silu(gf) * _rms(o)).astype(BF16)

    @pl.when(c == n_chunks - 1)
    def _():
        so_ref[0, 0] = s_ref[...]
        if layer == 0:
            for later in range(1, N_RET):
                so_ref[later, 0] = jnp.zeros_like(s_ref)


def _ret_decay_tables(cl):
    log_g = np.log1p(-np.exp2(-5.0 - np.arange(RET_HEADS, dtype=np.float64)))
    idx = np.arange(cl, dtype=np.float64)
    diff = idx[:, None] - idx[None, :]
    intra = np.where(diff >= 0, np.exp(log_g[:, None, None] * np.maximum(diff, 0.0)), 0.0)
    qd = np.exp(log_g[:, None] * (idx[None, :] + 1.0))[:, :, None]
    kd = np.exp(log_g[:, None] * (cl - 1.0 - idx[None, :]))[:, :, None]
    cd = np.broadcast_to(np.exp(log_g * cl)[:, None, None], (RET_HEADS, 1, 128))
    return tuple(jnp.asarray(a, F32) for a in (intra, qd, kd, cd))


def _ret_core_call(p, s0, states, layer, n_seq, seq_len, cl, row0):
    has_s0 = s0 is not None
    n_chunks = seq_len // cl
    rb0 = row0 // cl
    state = (RET_HEADS, RET_DK, RET_DV)
    in_specs = [pl.BlockSpec((cl, RET_IN), lambda b, c: (rb0 + b * n_chunks + c, 0))]
    args = [p]
    if has_s0:
        in_specs.append(pl.BlockSpec((1, 1) + state, lambda b, c: (layer, b, 0, 0, 0)))
        args.append(s0)
    in_specs += [_resident((RET_HEADS, cl, cl)), _resident((RET_HEADS, cl, 1)),
                 _resident((RET_HEADS, cl, 1)), _resident((RET_HEADS, 1, 128))]
    args += list(_ret_decay_tables(cl))
    if layer == 0:
        assert states is None
        state_spec = pl.BlockSpec((N_RET, 1) + state, lambda b, c: (0, b, 0, 0, 0))
        aliases = {}
    else:
        in_specs.append(pl.BlockSpec(memory_space=pl.ANY))
        args.append(states)
        state_spec = pl.BlockSpec((1, 1) + state, lambda b, c: (layer, b, 0, 0, 0))
        aliases = {len(args) - 1: 1}
    return pl.pallas_call(
        functools.partial(_ret_core_kernel, has_s0, n_chunks, layer),
        grid=(n_seq, n_chunks),
        in_specs=in_specs,
        out_specs=[pl.BlockSpec((cl, RET_V), lambda b, c: (b * n_chunks + c, 0)), state_spec],
        out_shape=[jax.ShapeDtypeStruct((n_seq * seq_len, RET_V), BF16),
                   jax.ShapeDtypeStruct((N_RET, n_seq) + state, F32)],
        scratch_shapes=[pltpu.VMEM(state, F32)],
        input_output_aliases=aliases,
        compiler_params=_params("parallel", "arbitrary"),
        name="ret_core",
    )(*args)


def _residual_router(acc, x_ref, mod_ref, gain_ref, wr_ref, br_ref, xo_ref, h2_ref, lg_ref):
    gate1 = mod_ref[:, 2:3, :]
    xn = x_ref[...] + _per_group(acc, lambda a3: a3 * gate1)
    xo_ref[...] = xn
    h2 = _norm_mod(xn, gain_ref, mod_ref, 3)
    h2_ref[...] = _pack_bf16_pairs(h2)
    h_hi = h2.astype(BF16)
    h_lo = (h2 - h_hi.astype(F32)).astype(BF16)
    hh = jnp.dot(h_hi, wr_ref[...], preferred_element_type=F32)
    lh = jnp.dot(h_lo, wr_ref[:, :ROUTER_LANES], preferred_element_type=F32)
    lg_ref[...] = hh[:, :ROUTER_LANES] + hh[:, ROUTER_LANES:] + lh + br_ref[...]


def _mix_out_specs(rows):
    return [_row_spec(rows, D), _row_spec(rows, DP), _row_spec(rows, ROUTER_LANES)]


_MIX_OUT_SHAPE = [
    jax.ShapeDtypeStruct((T_ALL, D), F32),
    jax.ShapeDtypeStruct((T_ALL, DP), U32),
    jax.ShapeDtypeStruct((T_ALL, ROUTER_LANES), F32),
]


def _router_specs():
    return [_resident((D, 2 * ROUTER_LANES)), _resident((1, ROUTER_LANES))]


def _ret_out_kernel(n_prompt_tiles, yp_ref, ys_ref, w_ref, x_ref, mod_ref, gain_ref, wr_ref, br_ref,
                    xo_ref, h2_ref, lg_ref):
    yin = jnp.where(pl.program_id(0) < n_prompt_tiles, yp_ref[...], ys_ref[...])
    acc = jnp.dot(yin, w_ref[0], preferred_element_type=F32)
    _residual_router(acc, x_ref, mod_ref, gain_ref, wr_ref, br_ref, xo_ref, h2_ref, lg_ref)


def _ret_out_call(y_prompt, y_sample, w_out, layer, x, mod, gain, w_r, b_r):
    rows = ROWS_WIDE
    return pl.pallas_call(
        functools.partial(_ret_out_kernel, _n_prompt_tiles(rows)),
        grid=(_n_tiles(rows),),
        in_specs=[_prompt_rows_spec(rows, RET_V), _sample_rows_spec(rows, RET_V),
                  _resident_layer((RET_V, D), layer), _row_spec(rows, D), _mod_spec(rows),
                  _resident((1, D))] + _router_specs(),
        out_specs=_mix_out_specs(rows),
        out_shape=_MIX_OUT_SHAPE,
        compiler_params=_params("parallel"),
        name="ret_out",
    )(y_prompt, y_sample, w_out, x, mod, gain, w_r, b_r)


GM_TN = 512


_GELU_C = float(np.sqrt(2.0 / np.pi))


def _gelu_tanh(x):
    hx = 0.5 * x
    return hx * jnp.tanh(x * (_GELU_C + (_GELU_C * 0.044715) * (x * x))) + hx


def _gm_proj_kernel(n_prompt_tiles, layer, *refs):
    x_ref, yg0_ref, yg1_ref, gates_ref, modp_ref, mod_ref, gain_ref, w_ref, b_ref, lg_ref, lb_ref = refs[:11]
    xo_ref, u_ref, v_ref, vs_ref, vraw_ref = refs[-5:]
    x = _add_moe(x_ref[...], yg0_ref, yg1_ref, gates_ref, modp_ref)
    xo_ref[...] = x
    hb = _norm_mod(x, gain_ref, mod_ref, 0).astype(BF16)
    rows = hb.shape[0]
    is_sample = pl.program_id(0) >= n_prompt_tiles

    def gelu_chunk(lo):
        z = jnp.dot(hb, w_ref[0, :, lo:lo + GM_TN], preferred_element_type=F32) + b_ref[:, lo:lo + GM_TN]
        return _gelu_tanh(z.astype(BF16))

    s1 = jnp.zeros((rows, 128), F32)
    s2 = jnp.zeros((rows, 128), F32)
    for lo in range(0, GM_HALF, GM_TN):
        gz = gelu_chunk(GM_HALF + lo)
        vraw_ref[:, lo:lo + GM_TN] = gz
        gf = gz.astype(F32)
        for k in range(0, GM_TN, 128):
            piece = gf[:, k:k + 128]
            s1 = s1 + piece
            s2 = s2 + piece * piece
    mu = jnp.sum(s1, axis=-1, keepdims=True) * (1.0 / GM_HALF)
    var = jnp.sum(s2, axis=-1, keepdims=True) * (1.0 / GM_HALF) - mu * mu
    rstd = lax.rsqrt(var + EPS)
    shift = -mu * rstd

    for lo in range(0, GM_HALF, GM_TN):
        u_ref[:, lo:lo + GM_TN] = gelu_chunk(lo)
        vn = ((vraw_ref[:, lo:lo + GM_TN].astype(F32) * rstd + shift) * lg_ref[:, lo:lo + GM_TN]
              + lb_ref[:, lo:lo + GM_TN])
        v_ref[:, lo:lo + GM_TN] = vn.astype(BF16)
        vs_ref[0, :, lo:lo + GM_TN] = vn

    if layer == 0:
        @pl.when(is_sample)
        def _():
            for later in range(1, N_GM):
                vs_ref[later] = jnp.zeros((rows, GM_HALF), F32)


def _gm_proj_call(x, prev, mod, gain, w_in, layer, b_in, ln_g, ln_b, vs_all):
    rows = ROWS_GM_PROJ
    npt = _n_prompt_tiles(rows)
    in_specs = [_row_spec(rows, D)] + _prev_specs(rows) + [
        _mod_spec(rows), _resident((1, D)), _resident_layer((D, GM_FFN), layer), _resident((1, GM_FFN)),
        _resident((1, GM_HALF)), _resident((1, GM_HALF))]
    args = [x, *prev, mod, gain, w_in, b_in, ln_g, ln_b]
    if layer == 0:
        assert vs_all is None
        vs_spec = pl.BlockSpec((N_GM, rows, GM_HALF), lambda i: (0, jnp.maximum(i - npt, 0), 0))
        aliases = {}
    else:
        in_specs.append(pl.BlockSpec(memory_space=pl.ANY))
        args.append(vs_all)
        vs_spec = pl.BlockSpec((1, rows, GM_HALF), lambda i: (layer, jnp.maximum(i - npt, 0), 0))
        aliases = {len(args) - 1: 3}
    return pl.pallas_call(
        functools.partial(_gm_proj_kernel, npt, layer),
        grid=(_n_tiles(rows),),
        in_specs=in_specs,
        out_specs=[_row_spec(rows, D), _row_spec(rows, GM_HALF), _row_spec(rows, GM_HALF), vs_spec],
        out_shape=[jax.ShapeDtypeStruct((T_ALL, D), F32),
                   jax.ShapeDtypeStruct((T_ALL, GM_HALF), BF16),
                   jax.ShapeDtypeStruct((T_ALL, GM_HALF), BF16),
                   jax.ShapeDtypeStruct((N_GM, T_SAMPLE, GM_HALF), F32)],
        scratch_shapes=[pltpu.VMEM((rows, GM_HALF), BF16)],
        input_output_aliases=aliases,
        compiler_params=_params("arbitrary"),
        name="gm_proj",
    )(*args)


def _gm_out_kernel(u_ref, v_ref, ws_ref, bs_ref, w_ref, bo_ref, x_ref, mod_ref, gain_ref, wr_ref, br_ref,
                   xo_ref, h2_ref, lg_ref):
    rows = u_ref.shape[0]
    pieces = []
    for r0 in range(0, rows, GM_MIX):
        acc = jnp.zeros((GM_MIX, D), F32)
        for g in range(GM_GROUPS):
            lo = g * GM_GDIM
            sp = jnp.dot(ws_ref[0, g], v_ref[r0:r0 + GM_MIX, lo:lo + GM_GDIM],
                         preferred_element_type=F32) + bs_ref[0, g]
            gated = (u_ref[r0:r0 + GM_MIX, lo:lo + GM_GDIM].astype(F32) * sp).astype(BF16)
            acc = acc + jnp.dot(gated, w_ref[0, lo:lo + GM_GDIM, :], preferred_element_type=F32)
        pieces.append(acc)
    acc = jnp.concatenate(pieces, axis=0) + bo_ref[...]
    _residual_router(acc, x_ref, mod_ref, gain_ref, wr_ref, br_ref, xo_ref, h2_ref, lg_ref)


def _gm_block_diag(w_s, b_s):
    mats, biases = [], []
    for cl in (GM_CHUNK, DEC_SEQ):
        tri = jnp.tril(jnp.ones((cl, cl), bool))
        blk = jnp.where(tri[None], w_s[:, :cl, :cl], 0.0)
        reps = GM_MIX // cl
        eye = jnp.eye(reps, dtype=w_s.dtype)
        bd = jnp.einsum("ab,gts->gatbs", eye, blk).reshape(GM_GROUPS, GM_MIX, GM_MIX)
        mats.append(bd)
        biases.append(jnp.tile(b_s[:, :cl], (1, reps))[:, :, None])
    return jnp.stack(mats).astype(BF16), jnp.stack(biases).astype(F32)


def _gm_out_call(u, v, ws_bd, bs_bd, w_out, layer, b_out, x, mod, gain, w_r, b_r):
    rows = ROWS_WIDE
    npt = _n_prompt_tiles(rows)

    def variant(i):
        return jnp.where(i >= npt, 1, 0)

    return pl.pallas_call(
        _gm_out_kernel,
        grid=(_n_tiles(rows),),
        in_specs=[_row_spec(rows, GM_HALF), _row_spec(rows, GM_HALF),
                  pl.BlockSpec((1, GM_GROUPS, GM_MIX, GM_MIX), lambda i: (variant(i), 0, 0, 0)),
                  pl.BlockSpec((1, GM_GROUPS, GM_MIX, 1), lambda i: (variant(i), 0, 0, 0)),
                  _resident_layer((GM_HALF, D), layer), _resident((1, D)), _row_spec(rows, D),
                  _mod_spec(rows), _resident((1, D))] + _router_specs(),
        out_specs=_mix_out_specs(rows),
        out_shape=_MIX_OUT_SHAPE,
        compiler_params=_params("arbitrary"),
        name="gm_out",
    )(u, v, ws_bd, bs_bd, w_out, b_out, x, mod, gain, w_r, b_r)


def _expert_kernel(be_ref, bv_ref, nb_ref, xb_ref, w1_ref, w3_ref, w2_ref, yb_ref, w1s, w3s, w2s):
    b = pl.program_id(0)

    @pl.when(b < nb_ref[0])
    def _():
        prev_e = be_ref[jnp.maximum(b - 1, 0)]

        @pl.when((b == 0) | (be_ref[b] != prev_e))
        def _():
            w1s[...] = w1_ref[0, 0].astype(BF16)
            w3s[...] = w3_ref[0, 0].astype(BF16)
            w2s[...] = w2_ref[0, 0].astype(BF16)

        row = lax.broadcasted_iota(jnp.int32, (EXP_BLOCK, 1), 0)
        xw = jnp.where(row < bv_ref[b], xb_ref[...], U32(0))
        x = _unpack_bf16_pairs(xw).astype(BF16)
        a = jnp.dot(x, w1s[...], preferred_element_type=F32)
        c = jnp.dot(x, w3s[...], preferred_element_type=F32)
        h = (_silu(a) * c).astype(BF16)
        yb_ref[...] = _pack_bf16_pairs(jnp.dot(h, w2s[...], preferred_element_type=F32))


def _expert_call(blk_e, blk_valid, n_blk, xb, w1, w3, w2, layer):
    def blk(b, be, bv, nb):
        return jnp.minimum(b, nb[0] - 1)

    def w_idx(b, be, bv, nb):
        return (layer, be[blk(b, be, bv, nb)], 0, 0)

    grid_spec = pltpu.PrefetchScalarGridSpec(
        num_scalar_prefetch=3,
        grid=(N_EXP_BLOCKS,),
        in_specs=[
            pl.BlockSpec((EXP_BLOCK, DP), lambda b, be, bv, nb: (blk(b, be, bv, nb), 0)),
            pl.BlockSpec((1, 1, D, MOE_HIDDEN), w_idx),
            pl.BlockSpec((1, 1, D, MOE_HIDDEN), w_idx),
            pl.BlockSpec((1, 1, MOE_HIDDEN, D), w_idx),
        ],
        out_specs=pl.BlockSpec((EXP_BLOCK, DP), lambda b, be, bv, nb: (blk(b, be, bv, nb), 0)),
        scratch_shapes=[pltpu.VMEM((D, MOE_HIDDEN), BF16), pltpu.VMEM((D, MOE_HIDDEN), BF16),
                        pltpu.VMEM((MOE_HIDDEN, D), BF16)],
    )
    return pl.pallas_call(
        _expert_kernel,
        grid_spec=grid_spec,
        out_shape=jax.ShapeDtypeStruct((P_ROWS, DP), U32),
        compiler_params=_params("arbitrary"),
        name="experts",
    )(blk_e, blk_valid, n_blk, xb, w1, w3, w2)


def _route_kernel(lg_ref, dest_ref, gates_ref, meta_ref, cnt_ref, base_ref):
    ph = pl.program_id(0)
    t = pl.program_id(1)
    tm = ROUTE_TM
    lt = lg_ref[...].T
    el = lt[0:MOE_EXPERTS]
    gl = lt[MOE_EXPERTS:MOE_EXPERTS + 8]
    gidx = lax.broadcasted_iota(jnp.int32, (8, tm), 0)
    neg = jnp.float32(-jnp.inf)
    gl = jnp.where(gidx < MOE_GROUPS, gl, neg)
    gmax = jnp.max(gl, axis=0, keepdims=True)
    grp = jnp.min(jnp.where(gl == gmax, gidx, MOE_GROUPS), axis=0, keepdims=True)
    eidx = lax.broadcasted_iota(jnp.int32, (MOE_EXPERTS, tm), 0)
    els = jnp.where((eidx >> 3) == grp, el, neg)
    m1 = jnp.max(els, axis=0, keepdims=True)
    i1 = jnp.min(jnp.where(els == m1, eidx, MOE_EXPERTS), axis=0, keepdims=True)
    els2 = jnp.where(eidx == i1, neg, els)
    m2 = jnp.max(els2, axis=0, keepdims=True)
    i2 = jnp.min(jnp.where(els2 == m2, eidx, MOE_EXPERTS), axis=0, keepdims=True)
    sel1 = eidx == i1
    sel2 = eidx == i2
    cnt = jnp.where(sel1 | sel2, 1.0, 0.0)
    tile_counts = jnp.sum(cnt, axis=1, keepdims=True)

    @pl.when(ph == 0)
    def _():
        @pl.when(t == 0)
        def _():
            cnt_ref[...] = jnp.zeros_like(cnt_ref)

        cnt_ref[...] += tile_counts

    @pl.when(ph == 1)
    def _():
        @pl.when(t == 0)
        def _():
            counts = cnt_ref[...]
            nblk = jnp.floor((counts + (EXP_BLOCK - 1.0)) * (1.0 / EXP_BLOCK))
            r = lax.broadcasted_iota(jnp.int32, (MOE_EXPERTS, MOE_EXPERTS), 0)
            c = lax.broadcasted_iota(jnp.int32, (MOE_EXPERTS, MOE_EXPERTS), 1)
            nblk_row = jnp.sum(jnp.where(r == c, nblk, 0.0), axis=0, keepdims=True)
            bstart = jnp.sum(jnp.where(c < r, nblk_row, 0.0), axis=1, keepdims=True)
            base_ref[...] = bstart * EXP_BLOCK
            bend = bstart + nblk
            bidx = lax.broadcasted_iota(jnp.int32, (1, META_LANES), 1).astype(F32)
            blk_e = jnp.minimum(jnp.sum(jnp.where(bidx >= bend, 1.0, 0.0), axis=0, keepdims=True),
                                MOE_EXPERTS - 1.0)
            erow = lax.broadcasted_iota(jnp.int32, (MOE_EXPERTS, META_LANES), 0).astype(F32)
            mine = erow == blk_e
            cnt_b = jnp.sum(jnp.where(mine, counts, 0.0), axis=0, keepdims=True)
            start_b = jnp.sum(jnp.where(mine, bstart, 0.0), axis=0, keepdims=True)
            valid = jnp.clip(cnt_b - (bidx - start_b) * EXP_BLOCK, 0.0, float(EXP_BLOCK))
            n_blk = jnp.sum(nblk, axis=0, keepdims=True)
            mrow = lax.broadcasted_iota(jnp.int32, (8, META_LANES), 0)
            meta = jnp.where(mrow == 0, blk_e, jnp.where(mrow == 1, valid, jnp.where(mrow == 2, n_blk, 0.0)))
            meta_ref[...] = meta.astype(jnp.int32)

        lane = ROUTER_LANES
        before = (lax.broadcasted_iota(jnp.int32, (lane, lane), 0)
                  < lax.broadcasted_iota(jnp.int32, (lane, lane), 1))
        tri = jnp.where(before, 1.0, 0.0).astype(BF16)
        run = base_ref[...]
        d1, d2 = [], []
        for k in range(tm // lane):
            piece = slice(k * lane, (k + 1) * lane)
            ck = cnt[:, piece]
            pos = run + jnp.dot(ck.astype(BF16), tri, preferred_element_type=F32)
            d1.append(jnp.sum(jnp.where(sel1[:, piece], pos, 0.0), axis=0, keepdims=True))
            d2.append(jnp.sum(jnp.where(sel2[:, piece], pos, 0.0), axis=0, keepdims=True))
            run = run + jnp.sum(ck, axis=1, keepdims=True)
        dest_ref[...] = jnp.concatenate(
            [jnp.concatenate(d1, axis=1), jnp.concatenate(d2, axis=1)], axis=0).astype(jnp.int32)
        base_ref[...] = run
        g_w = 1.0 / jnp.sum(jnp.exp(gl - gmax), axis=0, keepdims=True)
        e21 = jnp.exp(m2 - m1)
        p1 = 1.0 / (1.0 + e21)
        rid = lax.broadcasted_iota(jnp.int32, (ROUTER_LANES, tm), 0)
        gt = jnp.where(rid == 0, g_w * p1, jnp.where(rid == 1, g_w * (e21 * p1), 0.0))
        gates_ref[...] = gt.T


def _route_call(logits):
    return pl.pallas_call(
        _route_kernel,
        grid=(2, N_ROUTE_TILES),
        in_specs=[pl.BlockSpec((ROUTE_TM, ROUTER_LANES), lambda ph, t: (t, 0))],
        out_specs=[pl.BlockSpec((MOE_TOPK, ROUTE_TM), lambda ph, t: (0, t * ph)),
                   pl.BlockSpec((ROUTE_TM, ROUTER_LANES), lambda ph, t: (t * ph, 0)),
                   pl.BlockSpec((8, META_LANES), lambda ph, t: (0, 0))],
        out_shape=[jax.ShapeDtypeStruct((MOE_TOPK, T_ALL), jnp.int32),
                   jax.ShapeDtypeStruct((T_ALL, ROUTER_LANES), F32),
                   jax.ShapeDtypeStruct((8, META_LANES), jnp.int32)],
        scratch_shapes=[pltpu.VMEM((MOE_EXPERTS, 1), F32), pltpu.VMEM((MOE_EXPERTS, 1), F32)],
        compiler_params=_params("arbitrary", "arbitrary"),
        name="route",
    )(logits)


def _sc_mesh():
    return plsc.VectorSubcoreMesh(core_axis_name="c", subcore_axis_name="s")


def _sc_token_offset(j):
    wid = lax.axis_index("s") * SC_CORES + lax.axis_index("c")
    return pl.multiple_of(wid * SC_ROWS_PER_WORKER + j * SC_CHUNK, 8)


def _dispatch_body(h_hbm, d0_hbm, d1_hbm, out_hbm, i0_v, i1_v, rows_v, sem0, sem1, sem2):
    @pl.loop(0, SC_ROWS_PER_WORKER // SC_CHUNK)
    def _(j):
        off = _sc_token_offset(j)
        c0 = pltpu.async_copy(d0_hbm.at[pl.ds(off, SC_CHUNK)], i0_v, sem0)
        c1 = pltpu.async_copy(d1_hbm.at[pl.ds(off, SC_CHUNK)], i1_v, sem1)
        c2 = pltpu.async_copy(h_hbm.at[pl.ds(off, SC_CHUNK)], rows_v, sem2)
        c0.wait()
        c1.wait()
        c2.wait()
        s0 = pltpu.async_copy(rows_v, out_hbm.at[i0_v], sem0)
        s1 = pltpu.async_copy(rows_v, out_hbm.at[i1_v], sem1)
        s0.wait()
        s1.wait()


def _dispatch_call(h2, dest0, dest1):
    return pl.kernel(
        _dispatch_body,
        out_type=jax.ShapeDtypeStruct((P_ROWS, DP), U32),
        mesh=_sc_mesh(),
        scratch_types=[pltpu.VMEM((SC_CHUNK,), jnp.int32), pltpu.VMEM((SC_CHUNK,), jnp.int32),
                       pltpu.VMEM((SC_CHUNK, DP), U32),
                       pltpu.SemaphoreType.DMA, pltpu.SemaphoreType.DMA, pltpu.SemaphoreType.DMA],
        name="moe_dispatch",
    )(h2, dest0, dest1)


def _combine_body(yb_hbm, d0_hbm, d1_hbm, o0_hbm, o1_hbm, i0_v, i1_v, r0_v, r1_v, sem0, sem1):
    @pl.loop(0, SC_ROWS_PER_WORKER // SC_CHUNK)
    def _(j):
        off = _sc_token_offset(j)
        c0 = pltpu.async_copy(d0_hbm.at[pl.ds(off, SC_CHUNK)], i0_v, sem0)
        c1 = pltpu.async_copy(d1_hbm.at[pl.ds(off, SC_CHUNK)], i1_v, sem1)
        c0.wait()
        c1.wait()
        g0 = pltpu.async_copy(yb_hbm.at[i0_v], r0_v, sem0)
        g1 = pltpu.async_copy(yb_hbm.at[i1_v], r1_v, sem1)
        g0.wait()
        g1.wait()
        w0 = pltpu.async_copy(r0_v, o0_hbm.at[pl.ds(off, SC_CHUNK)], sem0)
        w1 = pltpu.async_copy(r1_v, o1_hbm.at[pl.ds(off, SC_CHUNK)], sem1)
        w0.wait()
        w1.wait()


def _combine_call(yb, dest0, dest1):
    out = jax.ShapeDtypeStruct((T_ALL, DP), U32)
    return pl.kernel(
        _combine_body,
        out_type=(out, out),
        mesh=_sc_mesh(),
        scratch_types=[pltpu.VMEM((SC_CHUNK,), jnp.int32), pltpu.VMEM((SC_CHUNK,), jnp.int32),
                       pltpu.VMEM((SC_CHUNK, DP), U32), pltpu.VMEM((SC_CHUNK, DP), U32),
                       pltpu.SemaphoreType.DMA, pltpu.SemaphoreType.DMA],
        name="moe_combine",
    )(yb, dest0, dest1)


def _moe(h2, logits, w1, w3, w2, layer):
    dest, gates, meta = _route_call(logits)
    dest0, dest1 = dest[0], dest[1]
    xb = _dispatch_call(h2, dest0, dest1)
    yb = _expert_call(meta[0, :N_EXP_BLOCKS], meta[1, :N_EXP_BLOCKS], meta[2, :1], xb, w1, w3, w2, layer)
    yg0, yg1 = _combine_call(yb, dest0, dest1)
    return yg0, yg1, gates


def _final_kernel(x_ref, yg0_ref, yg1_ref, gates_ref, modp_ref, gain_ref, o_ref):
    x = _add_moe(x_ref[...], yg0_ref, yg1_ref, gates_ref, modp_ref)
    o_ref[...] = _rms(x) * gain_ref[...]


def _final_call(x, prev, gain, row0, n_rows):
    rows = ROWS_WIDE
    tile0 = row0 // rows

    def tile(width):
        return pl.BlockSpec((rows, width), lambda i: (tile0 + i, 0))

    return pl.pallas_call(
        _final_kernel,
        grid=(n_rows // rows,),
        in_specs=[tile(D), tile(DP), tile(DP), tile(ROUTER_LANES),
                  pl.BlockSpec((rows // GROUP, 6, D), lambda i: (tile0 + i, 0, 0)), _resident((1, D))],
        out_specs=pl.BlockSpec((rows, D), lambda i: (i, 0)),
        out_shape=jax.ShapeDtypeStruct((n_rows, D), F32),
        compiler_params=_params("parallel"),
        name="final_norm",
    )(x, *prev, gain)


def _rope_tables():
    pos = np.concatenate([np.tile(np.arange(SEQ), BATCH),
                          np.tile(PAST_LEN + np.arange(DEC_SEQ), DEC_BATCH)]).astype(np.float32)
    inv = (ROPE_BASE ** (-np.arange(ROPE_HALF, dtype=np.float32) / ROPE_HALF)).astype(np.float32)
    ang = (pos[:, None] * inv[None, :]).astype(np.float32).astype(np.float64)
    return jnp.asarray(np.cos(ang), F32), jnp.asarray(np.sin(ang), F32)


def kernel(x_prompt, x_sample, c_prompt, c_sample, state_ret, ada_w, ada_b, norm1_g, norm2_g, ret_w_in,
           ret_w_out, gm_w_in, gm_b_in, gm_ln_g, gm_ln_b, gm_w_s, gm_b_s, gm_w_out, gm_b_out, moe_w_rg,
           moe_b_rg, moe_w_re, moe_b_re, moe_w1, moe_w3, moe_w2, final_g):
    x = (x_prompt.reshape(T_PROMPT, D), x_sample.reshape(T_SAMPLE, D))
    c_all = jnp.concatenate([c_prompt, c_sample], axis=0)
    mod_all = _ada_call(c_all, ada_w, ada_b).reshape(DEPTH, N_SEQ, 6, D)
    cos, sin = _rope_tables()
    ret_w_in_b, ret_w_out_b = ret_w_in.astype(BF16), ret_w_out.astype(BF16)
    gm_w_in_b, gm_w_out_b = gm_w_in.astype(BF16), gm_w_out.astype(BF16)

    ret_prompt = ret_sample = gm_sample = None
    prev = None
    for i in range(DEPTH):
        j = i // 2
        mod = jnp.concatenate([
            jnp.broadcast_to(mod_all[i, :BATCH, None], (BATCH, SEQ // GROUP, 6, D)).reshape(-1, 6, D),
            jnp.broadcast_to(mod_all[i, BATCH:, None], (DEC_BATCH, DEC_SEQ // GROUP, 6, D)).reshape(-1, 6, D),
        ], axis=0)
        g1 = norm1_g[i].reshape(1, D)
        g2 = norm2_g[i].reshape(1, D)
        w_r = jnp.pad(jnp.concatenate([moe_w_re[i], moe_w_rg[i]], axis=1),
                      ((0, 0), (0, ROUTER_LANES - MOE_GROUPS - MOE_EXPERTS)))
        w_r_hi = w_r.astype(BF16)
        w_r_lo = (w_r - w_r_hi.astype(F32)).astype(BF16)
        w_r = jnp.concatenate([w_r_hi, w_r_lo], axis=1)
        b_r = jnp.pad(jnp.concatenate([moe_b_re[i].reshape(-1), moe_b_rg[i]]),
                      (0, ROUTER_LANES - MOE_GROUPS - MOE_EXPERTS)).reshape(1, ROUTER_LANES)
        if i % 2 == 0:
            x, p = _ret_proj_call(x, prev, mod, g1, ret_w_in_b, j, cos, sin)
            y_p, ret_prompt = _ret_core_call(p, None, ret_prompt, j, BATCH, SEQ, RET_CHUNK_PROMPT, 0)
            y_s, ret_sample = _ret_core_call(p, state_ret, ret_sample, j, DEC_BATCH, DEC_SEQ,
                                             RET_CHUNK_SAMPLE, T_PROMPT)
            x, h2, logits = _ret_out_call(y_p, y_s, ret_w_out_b, j, x, mod, g2, w_r, b_r)
        else:
            x, u, v, gm_sample = _gm_proj_call(x, prev, mod, g1, gm_w_in_b, j,
                                               gm_b_in[j].reshape(1, GM_FFN), gm_ln_g[j].reshape(1, GM_HALF),
                                               gm_ln_b[j].reshape(1, GM_HALF), gm_sample)
            ws_bd, bs_bd = _gm_block_diag(gm_w_s[j], gm_b_s[j])
            x, h2, logits = _gm_out_call(u, v, ws_bd, bs_bd, gm_w_out_b, j,
                                         gm_b_out[j].reshape(1, D), x, mod, g2, w_r, b_r)
        yg0, yg1, gates = _moe(h2, logits, moe_w1, moe_w3, moe_w2, i)
        prev = (yg0, yg1, gates, mod)

    fg = final_g.reshape(1, D)
    y_prompt = _final_call(x, prev, fg, 0, T_PROMPT).reshape(BATCH, SEQ, D)
    y_sample = _final_call(x, prev, fg, T_PROMPT, T_SAMPLE).reshape(DEC_BATCH, DEC_SEQ, D)
    return (y_prompt, y_sample, ret_prompt, ret_sample,
            gm_sample.reshape(N_GM, DEC_BATCH, DEC_SEQ, GM_HALF))
```

```python
import functools

import numpy as np
import jax
import jax.numpy as jnp
from jax import lax
from jax.experimental import pallas as pl
from jax.experimental.pallas import tpu as pltpu
from jax.experimental.pallas import tpu_sc as plsc

F32 = jnp.float32
BF16 = jnp.bfloat16
U32 = jnp.uint32

D = 1024
BATCH, SEQ = 4, 4096
DEC_BATCH, DEC_SEQ = 16, 64
PAST_LEN = 4096
DEPTH = 4
N_RET = (DEPTH + 1) // 2
N_GM = DEPTH // 2
N_SEQ = BATCH + DEC_BATCH

RET_HEADS, RET_DK, RET_DV = 4, 256, 512
RET_QK = RET_HEADS * RET_DK
RET_V = RET_HEADS * RET_DV
RET_IN = 2 * RET_QK + 2 * RET_V
ROPE_BASE = 10000.0
ROPE_HALF = RET_DK // 2

GM_FFN = 6 * D
GM_HALF = GM_FFN // 2
GM_GROUPS = 4
GM_GDIM = GM_HALF // GM_GROUPS
GM_CHUNK = 128

MOE_GROUPS, MOE_PER_GROUP = 4, 8
MOE_EXPERTS = MOE_GROUPS * MOE_PER_GROUP
MOE_TOPK = 2
MOE_HIDDEN = 512
EPS = 1e-6

GROUP = 64
T_PROMPT = BATCH * SEQ
T_SAMPLE = DEC_BATCH * DEC_SEQ
T_ALL = T_PROMPT + T_SAMPLE
N_GROUPS = T_ALL // GROUP
ROWS_WIDE = 512
ROWS_GM_PROJ = 256

RET_CHUNK_PROMPT = 256
RET_CHUNK_SAMPLE = DEC_SEQ

GM_MIX = 256

EXP_BLOCK = 256
N_ASSIGN = T_ALL * MOE_TOPK
N_EXP_BLOCKS = -(-(N_ASSIGN + MOE_EXPERTS * (EXP_BLOCK - 1)) // EXP_BLOCK)
P_ROWS = N_EXP_BLOCKS * EXP_BLOCK
ROUTER_LANES = 128
ROUTE_TM = 1024
N_ROUTE_TILES = T_ALL // ROUTE_TM
META_LANES = 256
assert META_LANES >= N_EXP_BLOCKS

DP = D // 2
SC_CORES, SC_SUBCORES = 2, 16
SC_WORKERS = SC_CORES * SC_SUBCORES
SC_ROWS_PER_WORKER = T_ALL // SC_WORKERS
SC_CHUNK = 32
assert SC_ROWS_PER_WORKER % SC_CHUNK == 0 and SC_CHUNK % 8 == 0

V7X_VMEM_LIMIT_BYTES = 56 * 1024 * 1024


def _params(*sem):
    return pltpu.CompilerParams(dimension_semantics=sem, vmem_limit_bytes=V7X_VMEM_LIMIT_BYTES)


def _resident(shape):
    nd = len(shape)
    return pl.BlockSpec(shape, lambda *_: (0,) * nd, pipeline_mode=pl.Buffered(1))


def _resident_layer(shape, layer):
    nd = len(shape)
    return pl.BlockSpec((1,) + shape, lambda *_: (layer,) + (0,) * nd, pipeline_mode=pl.Buffered(1))


def _rms(x):
    return x * lax.rsqrt(jnp.mean(x * x, axis=-1, keepdims=True) + EPS)


def _silu(x):
    return x * jax.nn.sigmoid(x)


def _per_group(x2d, fn):
    rows = x2d.shape[0]
    return fn(x2d.reshape(rows // GROUP, GROUP, D)).reshape(rows, D)


def _norm_mod(x, gain_ref, mod_ref, shift_idx):
    y = _rms(x) * gain_ref[...]
    scale = mod_ref[:, shift_idx + 1:shift_idx + 2, :]
    shift = mod_ref[:, shift_idx:shift_idx + 1, :]
    return _per_group(y, lambda y3: y3 * (1.0 + scale) + shift)


def _pack_bf16_pairs(x):
    lo = lax.bitcast_convert_type(x[:, :DP].astype(BF16).astype(F32), U32)
    hi = lax.bitcast_convert_type(x[:, DP:].astype(BF16).astype(F32), U32)
    return (lo >> 16) | (hi & U32(0xFFFF0000))


def _unpack_bf16_pairs(w):
    lo = lax.bitcast_convert_type(w << 16, F32)
    hi = lax.bitcast_convert_type(w & U32(0xFFFF0000), F32)
    return jnp.concatenate([lo, hi], axis=1)


def _add_moe(x, yg0_ref, yg1_ref, gates_ref, modp_ref):
    g = gates_ref[...]
    y = g[:, 0:1] * _unpack_bf16_pairs(yg0_ref[...]) + g[:, 1:2] * _unpack_bf16_pairs(yg1_ref[...])
    gate2 = modp_ref[:, 5:6, :]
    return x + _per_group(y, lambda y3: y3 * gate2)


ADA_TN = 1536


def _ada_kernel(c_ref, w_ref, b_ref, o_ref):
    c = c_ref[...]
    s = _silu(c).astype(BF16)
    o_ref[0] = jnp.dot(s, w_ref[0].astype(BF16), preferred_element_type=F32) + b_ref[0]


def _ada_call(c_all, ada_w, ada_b):
    nt = 6 * D // ADA_TN
    return pl.pallas_call(
        _ada_kernel,
        grid=(DEPTH, nt),
        in_specs=[
            pl.BlockSpec((N_SEQ, D), lambda i, j: (0, 0)),
            pl.BlockSpec((1, D, ADA_TN), lambda i, j: (i, 0, j)),
            pl.BlockSpec((1, 1, ADA_TN), lambda i, j: (i, 0, j)),
        ],
        out_specs=pl.BlockSpec((1, N_SEQ, ADA_TN), lambda i, j: (i, 0, j)),
        out_shape=jax.ShapeDtypeStruct((DEPTH, N_SEQ, 6 * D), F32),
        compiler_params=_params("parallel", "parallel"),
        name="ada_modulation",
    )(c_all, ada_w, ada_b.reshape(DEPTH, 1, 6 * D))


def _n_tiles(rows):
    return T_ALL // rows


def _n_prompt_tiles(rows):
    return T_PROMPT // rows


def _row_spec(rows, width):
    return pl.BlockSpec((rows, width), lambda i: (i, 0))


def _mod_spec(rows):
    return pl.BlockSpec((rows // GROUP, 6, D), lambda i: (i, 0, 0))


def _prev_specs(rows):
    return [_row_spec(rows, DP), _row_spec(rows, DP), _row_spec(rows, ROUTER_LANES), _mod_spec(rows)]


def _prompt_rows_spec(rows, width):
    last = _n_prompt_tiles(rows) - 1
    return pl.BlockSpec((rows, width), lambda i: (jnp.minimum(i, last), 0))


def _sample_rows_spec(rows, width):
    npt = _n_prompt_tiles(rows)
    return pl.BlockSpec((rows, width), lambda i: (jnp.maximum(i - npt, 0), 0))


def _ret_proj_kernel(has_prev, n_prompt_tiles, *refs):
    if has_prev:
        (x_ref, yg0_ref, yg1_ref, gates_ref, modp_ref, mod_ref, gain_ref, w_ref, cos_ref, sin_ref, dec_ref,
         xo_ref, p_ref) = refs
        x = _add_moe(x_ref[...], yg0_ref, yg1_ref, gates_ref, modp_ref)
    else:
        xp_ref, xs_ref, mod_ref, gain_ref, w_ref, cos_ref, sin_ref, dec_ref, xo_ref, p_ref = refs
        x = jnp.where(pl.program_id(0) < n_prompt_tiles, xp_ref[...], xs_ref[...])
    xo_ref[...] = x
    hb = _norm_mod(x, gain_ref, mod_ref, 0).astype(BF16)
    cos = cos_ref[...]
    sin = sin_ref[...]
    dec = dec_ref[...]
    for j in range(2 * RET_HEADS):
        lo = j * RET_DK
        acc = jnp.dot(hb, w_ref[0, :, lo:lo + RET_DK], preferred_element_type=F32)
        x1 = acc[:, :ROPE_HALF]
        x2 = acc[:, ROPE_HALF:]
        scale = dec[:, j:j + 1]
        p_ref[:, lo:lo + ROPE_HALF] = ((x1 * cos - x2 * sin) * scale).astype(BF16)
        p_ref[:, lo + ROPE_HALF:lo + RET_DK] = ((x1 * sin + x2 * cos) * scale).astype(BF16)
    for j in range(2 * RET_HEADS):
        lo = 2 * RET_QK + j * RET_DV
        acc = jnp.dot(hb, w_ref[0, :, lo:lo + RET_DV], preferred_element_type=F32)
        if j >= RET_HEADS:
            acc = _silu(acc)
        p_ref[:, lo:lo + RET_DV] = acc.astype(BF16)


def _ret_proj_call(x, prev, mod, gain, w_in, layer, cos, sin, dec):
    rows = ROWS_WIDE
    has_prev = prev is not None
    if has_prev:
        in_specs = [_row_spec(rows, D)] + _prev_specs(rows)
        args = [x] + list(prev)
    else:
        in_specs = [_prompt_rows_spec(rows, D), _sample_rows_spec(rows, D)]
        args = list(x)
    in_specs += [_mod_spec(rows), _resident((1, D)), _resident_layer((D, RET_IN), layer),
                 _row_spec(rows, ROPE_HALF), _row_spec(rows, ROPE_HALF), _row_spec(rows, 2 * RET_HEADS)]
    args += [mod, gain, w_in, cos, sin, dec]
    return pl.pallas_call(
        functools.partial(_ret_proj_kernel, has_prev, _n_prompt_tiles(rows)),
        grid=(_n_tiles(rows),),
        in_specs=in_specs,
        out_specs=[_row_spec(rows, D), _row_spec(rows, RET_IN)],
        out_shape=[jax.ShapeDtypeStruct((T_ALL, D), F32), jax.ShapeDtypeStruct((T_ALL, RET_IN), BF16)],
        compiler_params=_params("parallel"),
        name="ret_proj",
    )(*args)


def _ret_core_kernel(has_s0, n_chunks, layer, *refs):
    refs = list(refs)
    p_ref = refs.pop(0)
    s0_ref = refs.pop(0) if has_s0 else None
    causal_ref, cd_ref = refs[:2]
    y_ref, so_ref, s_ref = refs[-3:]
    c = pl.program_id(1)

    @pl.when(c == 0)
    def _():
        if has_s0:
            s_ref[...] = s0_ref[0, 0]
        else:
            s_ref[...] = jnp.zeros_like(s_ref)

    for h in range(RET_HEADS):
        qb = p_ref[:, h * RET_DK:(h + 1) * RET_DK]
        kb = p_ref[:, RET_QK + h * RET_DK:RET_QK + (h + 1) * RET_DK]
        vb = p_ref[:, 2 * RET_QK + h * RET_DV:2 * RET_QK + (h + 1) * RET_DV]
        gb = p_ref[:, 2 * RET_QK + RET_V + h * RET_DV:2 * RET_QK + RET_V + (h + 1) * RET_DV]
        scores = lax.dot_general(qb, kb, (((1,), (1,)), ((), ())), preferred_element_type=F32)
        scores = scores * causal_ref[...]
        s_old = s_ref[h]
        o = (jnp.dot(scores.astype(BF16), vb, preferred_element_type=F32)
             + jnp.dot(qb, s_old.astype(BF16), preferred_element_type=F32))
        s_ref[h] = cd_ref[h][:, 0:1] * (s_old + lax.dot_general(
            kb, vb, (((0,), (0,)), ((), ())), preferred_element_type=F32))
        y_ref[:, h * RET_DV:(h + 1) * RET_DV] = (gb.astype(F32) * _rms(o)).astype(BF16)

    @pl.when(c == n_chunks - 1)
    def _():
        so_ref[0, 0] = s_ref[...]
        if layer == 0:
            for later in range(1, N_RET):
                so_ref[later, 0] = jnp.zeros_like(s_ref)


def _ret_log_gamma():
    return np.log1p(-np.exp2(-5.0 - np.arange(RET_HEADS, dtype=np.float64)))


def _ret_chunk_tables(cl):
    idx = np.arange(cl)
    causal = (idx[:, None] >= idx[None, :]).astype(np.float32)
    cd = np.broadcast_to(np.exp(_ret_log_gamma() * cl)[:, None, None], (RET_HEADS, 1, 128))
    return jnp.asarray(causal, F32), jnp.asarray(cd, F32)


def _ret_row_scales():
    c = np.concatenate([np.arange(T_PROMPT) % RET_CHUNK_PROMPT,
                        np.arange(T_SAMPLE) % RET_CHUNK_SAMPLE]).astype(np.float64)
    e = (c[:, None] + 1.0) * _ret_log_gamma()[None, :]
    return jnp.asarray(np.concatenate([np.exp(e), np.exp(-e) * RET_DK ** -0.5], axis=1), F32)


def _ret_core_call(p, s0, states, layer, n_seq, seq_len, cl, row0):
    has_s0 = s0 is not None
    n_chunks = seq_len // cl
    rb0 = row0 // cl
    state = (RET_HEADS, RET_DK, RET_DV)
    in_specs = [pl.BlockSpec((cl, RET_IN), lambda b, c: (rb0 + b * n_chunks + c, 0))]
    args = [p]
    if has_s0:
        in_specs.append(pl.BlockSpec((1, 1) + state, lambda b, c: (layer, b, 0, 0, 0)))
        args.append(s0)
    in_specs += [_resident((cl, cl)), _resident((RET_HEADS, 1, 128))]
    args += list(_ret_chunk_tables(cl))
    if layer == 0:
        assert states is None
        state_spec = pl.BlockSpec((N_RET, 1) + state, lambda b, c: (0, b, 0, 0, 0))
        aliases = {}
    else:
        in_specs.append(pl.BlockSpec(memory_space=pl.ANY))
        args.append(states)
        state_spec = pl.BlockSpec((1, 1) + state, lambda b, c: (layer, b, 0, 0, 0))
        aliases = {len(args) - 1: 1}
    return pl.pallas_call(
        functools.partial(_ret_core_kernel, has_s0, n_chunks, layer),
        grid=(n_seq, n_chunks),
        in_specs=in_specs,
        out_specs=[pl.BlockSpec((cl, RET_V), lambda b, c: (b * n_chunks + c, 0)), state_spec],
        out_shape=[jax.ShapeDtypeStruct((n_seq * seq_len, RET_V), BF16),
                   jax.ShapeDtypeStruct((N_RET, n_seq) + state, F32)],
        scratch_shapes=[pltpu.VMEM(state, F32)],
        input_output_aliases=aliases,
        compiler_params=_params("parallel", "arbitrary"),
        name="ret_core",
    )(*args)


def _residual_router(acc, x_ref, mod_ref, gain_ref, wr_ref, br_ref, xo_ref, h2_ref, lg_ref):
    gate1 = mod_ref[:, 2:3, :]
    xn = x_ref[...] + _per_group(acc, lambda a3: a3 * gate1)
    xo_ref[...] = xn
    h2 = _norm_mod(xn, gain_ref, mod_ref, 3)
    h2_ref[...] = _pack_bf16_pairs(h2)
    hh = jnp.dot(h2.astype(BF16), wr_ref[...], preferred_element_type=F32)
    lg_ref[...] = hh[:, :ROUTER_LANES] + hh[:, ROUTER_LANES:] + br_ref[...]


def _mix_out_specs(rows):
    return [_row_spec(rows, D), _row_spec(rows, DP), _row_spec(rows, ROUTER_LANES)]


_MIX_OUT_SHAPE = [
    jax.ShapeDtypeStruct((T_ALL, D), F32),
    jax.ShapeDtypeStruct((T_ALL, DP), U32),
    jax.ShapeDtypeStruct((T_ALL, ROUTER_LANES), F32),
]


def _router_specs():
    return [_resident((D, 2 * ROUTER_LANES)), _resident((1, ROUTER_LANES))]


def _ret_out_kernel(n_prompt_tiles, yp_ref, ys_ref, w_ref, x_ref, mod_ref, gain_ref, wr_ref, br_ref,
                    xo_ref, h2_ref, lg_ref):
    yin = jnp.where(pl.program_id(0) < n_prompt_tiles, yp_ref[...], ys_ref[...])
    acc = jnp.dot(yin, w_ref[0], preferred_element_type=F32)
    _residual_router(acc, x_ref, mod_ref, gain_ref, wr_ref, br_ref, xo_ref, h2_ref, lg_ref)


def _ret_out_call(y_prompt, y_sample, w_out, layer, x, mod, gain, w_r, b_r):
    rows = ROWS_WIDE
    return pl.pallas_call(
        functools.partial(_ret_out_kernel, _n_prompt_tiles(rows)),
        grid=(_n_tiles(rows),),
        in_specs=[_prompt_rows_spec(rows, RET_V), _sample_rows_spec(rows, RET_V),
                  _resident_layer((RET_V, D), layer), _row_spec(rows, D), _mod_spec(rows),
                  _resident((1, D))] + _router_specs(),
        out_specs=_mix_out_specs(rows),
        out_shape=_MIX_OUT_SHAPE,
        compiler_params=_params("parallel"),
        name="ret_out",
    )(y_prompt, y_sample, w_out, x, mod, gain, w_r, b_r)


GM_TN = 512


_GELU_C = float(np.sqrt(2.0 / np.pi))


def _gelu_tanh(x):
    hx = 0.5 * x
    return hx * jnp.tanh(x * (_GELU_C + (_GELU_C * 0.044715) * (x * x))) + hx


def _gm_proj_kernel(n_prompt_tiles, layer, *refs):
    x_ref, yg0_ref, yg1_ref, gates_ref, modp_ref, mod_ref, gain_ref, w_ref, b_ref, lg_ref, lb_ref = refs[:11]
    xo_ref, u_ref, v_ref, vs_ref, vraw_ref = refs[-5:]
    x = _add_moe(x_ref[...], yg0_ref, yg1_ref, gates_ref, modp_ref)
    xo_ref[...] = x
    hb = _norm_mod(x, gain_ref, mod_ref, 0).astype(BF16)
    rows = hb.shape[0]
    is_sample = pl.program_id(0) >= n_prompt_tiles

    def gelu_chunk(lo):
        z = jnp.dot(hb, w_ref[0, :, lo:lo + GM_TN], preferred_element_type=F32) + b_ref[:, lo:lo + GM_TN]
        return _gelu_tanh(z.astype(BF16))

    s1 = jnp.zeros((rows, 128), F32)
    s2 = jnp.zeros((rows, 128), F32)
    for lo in range(0, GM_HALF, GM_TN):
        gz = gelu_chunk(GM_HALF + lo)
        vraw_ref[:, lo:lo + GM_TN] = gz
        gf = gz.astype(F32)
        for k in range(0, GM_TN, 128):
            piece = gf[:, k:k + 128]
            s1 = s1 + piece
            s2 = s2 + piece * piece
    mu = jnp.sum(s1, axis=-1, keepdims=True) * (1.0 / GM_HALF)
    var = jnp.sum(s2, axis=-1, keepdims=True) * (1.0 / GM_HALF) - mu * mu
    rstd = lax.rsqrt(var + EPS)
    shift = -mu * rstd

    for lo in range(0, GM_HALF, GM_TN):
        u_ref[:, lo:lo + GM_TN] = gelu_chunk(lo)
        vn = ((vraw_ref[:, lo:lo + GM_TN].astype(F32) * rstd + shift) * lg_ref[:, lo:lo + GM_TN]
              + lb_ref[:, lo:lo + GM_TN])
        v_ref[:, lo:lo + GM_TN] = vn.astype(BF16)
        vs_ref[0, :, lo:lo + GM_TN] = vn

    if layer == 0:
        @pl.when(is_sample)
        def _():
            for later in range(1, N_GM):
                vs_ref[later] = jnp.zeros((rows, GM_HALF), F32)


def _gm_proj_call(x, prev, mod, gain, w_in, layer, b_in, ln_g, ln_b, vs_all):
    rows = ROWS_GM_PROJ
    npt = _n_prompt_tiles(rows)
    in_specs = [_row_spec(rows, D)] + _prev_specs(rows) + [
        _mod_spec(rows), _resident((1, D)), _resident_layer((D, GM_FFN), layer), _resident((1, GM_FFN)),
        _resident((1, GM_HALF)), _resident((1, GM_HALF))]
    args = [x, *prev, mod, gain, w_in, b_in, ln_g, ln_b]
    if layer == 0:
        assert vs_all is None
        vs_spec = pl.BlockSpec((N_GM, rows, GM_HALF), lambda i: (0, jnp.maximum(i - npt, 0), 0))
        aliases = {}
    else:
        in_specs.append(pl.BlockSpec(memory_space=pl.ANY))
        args.append(vs_all)
        vs_spec = pl.BlockSpec((1, rows, GM_HALF), lambda i: (layer, jnp.maximum(i - npt, 0), 0))
        aliases = {len(args) - 1: 3}
    return pl.pallas_call(
        functools.partial(_gm_proj_kernel, npt, layer),
        grid=(_n_tiles(rows),),
        in_specs=in_specs,
        out_specs=[_row_spec(rows, D), _row_spec(rows, GM_HALF), _row_spec(rows, GM_HALF), vs_spec],
        out_shape=[jax.ShapeDtypeStruct((T_ALL, D), F32),
                   jax.ShapeDtypeStruct((T_ALL, GM_HALF), BF16),
                   jax.ShapeDtypeStruct((T_ALL, GM_HALF), BF16),
                   jax.ShapeDtypeStruct((N_GM, T_SAMPLE, GM_HALF), F32)],
        scratch_shapes=[pltpu.VMEM((rows, GM_HALF), BF16)],
        input_output_aliases=aliases,
        compiler_params=_params("arbitrary"),
        name="gm_proj",
    )(*args)


def _gm_out_kernel(u_ref, v_ref, ws_ref, bs_ref, w_ref, bo_ref, x_ref, mod_ref, gain_ref, wr_ref, br_ref,
                   xo_ref, h2_ref, lg_ref):
    rows = u_ref.shape[0]
    pieces = []
    for r0 in range(0, rows, GM_MIX):
        acc = jnp.zeros((GM_MIX, D), F32)
        for g in range(GM_GROUPS):
            lo = g * GM_GDIM
            sp = jnp.dot(ws_ref[0, g], v_ref[r0:r0 + GM_MIX, lo:lo + GM_GDIM],
                         preferred_element_type=F32) + bs_ref[0, g]
            gated = (u_ref[r0:r0 + GM_MIX, lo:lo + GM_GDIM].astype(F32) * sp).astype(BF16)
            acc = acc + jnp.dot(gated, w_ref[0, lo:lo + GM_GDIM, :], preferred_element_type=F32)
        pieces.append(acc)
    acc = jnp.concatenate(pieces, axis=0) + bo_ref[...]
    _residual_router(acc, x_ref, mod_ref, gain_ref, wr_ref, br_ref, xo_ref, h2_ref, lg_ref)


def _gm_block_diag(w_s, b_s):
    mats, biases = [], []
    for cl in (GM_CHUNK, DEC_SEQ):
        tri = jnp.tril(jnp.ones((cl, cl), bool))
        blk = jnp.where(tri[None], w_s[:, :cl, :cl], 0.0)
        reps = GM_MIX // cl
        eye = jnp.eye(reps, dtype=w_s.dtype)
        bd = jnp.einsum("ab,gts->gatbs", eye, blk).reshape(GM_GROUPS, GM_MIX, GM_MIX)
        mats.append(bd)
        biases.append(jnp.tile(b_s[:, :cl], (1, reps))[:, :, None])
    return jnp.stack(mats).astype(BF16), jnp.stack(biases).astype(F32)


def _gm_out_call(u, v, ws_bd, bs_bd, w_out, layer, b_out, x, mod, gain, w_r, b_r):
    rows = ROWS_WIDE
    npt = _n_prompt_tiles(rows)

    def variant(i):
        return jnp.where(i >= npt, 1, 0)

    return pl.pallas_call(
        _gm_out_kernel,
        grid=(_n_tiles(rows),),
        in_specs=[_row_spec(rows, GM_HALF), _row_spec(rows, GM_HALF),
                  pl.BlockSpec((1, GM_GROUPS, GM_MIX, GM_MIX), lambda i: (variant(i), 0, 0, 0)),
                  pl.BlockSpec((1, GM_GROUPS, GM_MIX, 1), lambda i: (variant(i), 0, 0, 0)),
                  _resident_layer((GM_HALF, D), layer), _resident((1, D)), _row_spec(rows, D),
                  _mod_spec(rows), _resident((1, D))] + _router_specs(),
        out_specs=_mix_out_specs(rows),
        out_shape=_MIX_OUT_SHAPE,
        compiler_params=_params("arbitrary"),
        name="gm_out",
    )(u, v, ws_bd, bs_bd, w_out, b_out, x, mod, gain, w_r, b_r)


def _expert_kernel(be_ref, bv_ref, nb_ref, xb_ref, w1_ref, w3_ref, w2_ref, yb_ref, w1s, w3s, w2s):
    b = pl.program_id(0)

    @pl.when(b < nb_ref[0])
    def _():
        prev_e = be_ref[jnp.maximum(b - 1, 0)]

        @pl.when((b == 0) | (be_ref[b] != prev_e))
        def _():
            w1s[...] = w1_ref[0, 0].astype(BF16)
            w3s[...] = w3_ref[0, 0].astype(BF16)
            w2s[...] = w2_ref[0, 0].astype(BF16)

        row = lax.broadcasted_iota(jnp.int32, (EXP_BLOCK, 1), 0)
        xw = jnp.where(row < bv_ref[b], xb_ref[...], U32(0))
        x = _unpack_bf16_pairs(xw).astype(BF16)
        a = jnp.dot(x, w1s[...], preferred_element_type=F32)
        c = jnp.dot(x, w3s[...], preferred_element_type=F32)
        h = (_silu(a) * c).astype(BF16)
        yb_ref[...] = _pack_bf16_pairs(jnp.dot(h, w2s[...], preferred_element_type=F32))


def _expert_call(blk_e, blk_valid, n_blk, xb, w1, w3, w2, layer):
    def blk(b, be, bv, nb):
        return jnp.minimum(b, nb[0] - 1)

    def w_idx(b, be, bv, nb):
        return (layer, be[blk(b, be, bv, nb)], 0, 0)

    grid_spec = pltpu.PrefetchScalarGridSpec(
        num_scalar_prefetch=3,
        grid=(N_EXP_BLOCKS,),
        in_specs=[
            pl.BlockSpec((EXP_BLOCK, DP), lambda b, be, bv, nb: (blk(b, be, bv, nb), 0)),
            pl.BlockSpec((1, 1, D, MOE_HIDDEN), w_idx),
            pl.BlockSpec((1, 1, D, MOE_HIDDEN), w_idx),
            pl.BlockSpec((1, 1, MOE_HIDDEN, D), w_idx),
        ],
        out_specs=pl.BlockSpec((EXP_BLOCK, DP), lambda b, be, bv, nb: (blk(b, be, bv, nb), 0)),
        scratch_shapes=[pltpu.VMEM((D, MOE_HIDDEN), BF16), pltpu.VMEM((D, MOE_HIDDEN), BF16),
                        pltpu.VMEM((MOE_HIDDEN, D), BF16)],
    )
    return pl.pallas_call(
        _expert_kernel,
        grid_spec=grid_spec,
        out_shape=jax.ShapeDtypeStruct((P_ROWS, DP), U32),
        compiler_params=_params("arbitrary"),
        name="experts",
    )(blk_e, blk_valid, n_blk, xb, w1, w3, w2)


def _route_kernel(lg_ref, dest_ref, gates_ref, meta_ref, cnt_ref, base_ref):
    ph = pl.program_id(0)
    t = pl.program_id(1)
    tm = ROUTE_TM
    lt = lg_ref[...].T
    el = lt[0:MOE_EXPERTS]
    gl = lt[MOE_EXPERTS:MOE_EXPERTS + 8]
    gidx = lax.broadcasted_iota(jnp.int32, (8, tm), 0)
    neg = jnp.float32(-jnp.inf)
    gl = jnp.where(gidx < MOE_GROUPS, gl, neg)
    gmax = jnp.max(gl, axis=0, keepdims=True)
    grp = jnp.min(jnp.where(gl == gmax, gidx, MOE_GROUPS), axis=0, keepdims=True)
    eidx = lax.broadcasted_iota(jnp.int32, (MOE_EXPERTS, tm), 0)
    els = jnp.where((eidx >> 3) == grp, el, neg)
    m1 = jnp.max(els, axis=0, keepdims=True)
    i1 = jnp.min(jnp.where(els == m1, eidx, MOE_EXPERTS), axis=0, keepdims=True)
    els2 = jnp.where(eidx == i1, neg, els)
    m2 = jnp.max(els2, axis=0, keepdims=True)
    i2 = jnp.min(jnp.where(els2 == m2, eidx, MOE_EXPERTS), axis=0, keepdims=True)
    sel1 = eidx == i1
    sel2 = eidx == i2
    cnt = jnp.where(sel1 | sel2, 1.0, 0.0)
    tile_counts = jnp.sum(cnt, axis=1, keepdims=True)

    @pl.when(ph == 0)
    def _():
        @pl.when(t == 0)
        def _():
            cnt_ref[...] = jnp.zeros_like(cnt_ref)

        cnt_ref[...] += tile_counts

    @pl.when(ph == 1)
    def _():
        @pl.when(t == 0)
        def _():
            counts = cnt_ref[...]
            nblk = jnp.floor((counts + (EXP_BLOCK - 1.0)) * (1.0 / EXP_BLOCK))
            r = lax.broadcasted_iota(jnp.int32, (MOE_EXPERTS, MOE_EXPERTS), 0)
            c = lax.broadcasted_iota(jnp.int32, (MOE_EXPERTS, MOE_EXPERTS), 1)
            nblk_row = jnp.sum(jnp.where(r == c, nblk, 0.0), axis=0, keepdims=True)
            bstart = jnp.sum(jnp.where(c < r, nblk_row, 0.0), axis=1, keepdims=True)
            base_ref[...] = bstart * EXP_BLOCK
            bend = bstart + nblk
            bidx = lax.broadcasted_iota(jnp.int32, (1, META_LANES), 1).astype(F32)
            blk_e = jnp.minimum(jnp.sum(jnp.where(bidx >= bend, 1.0, 0.0), axis=0, keepdims=True),
                                MOE_EXPERTS - 1.0)
            erow = lax.broadcasted_iota(jnp.int32, (MOE_EXPERTS, META_LANES), 0).astype(F32)
            mine = erow == blk_e
            cnt_b = jnp.sum(jnp.where(mine, counts, 0.0), axis=0, keepdims=True)
            start_b = jnp.sum(jnp.where(mine, bstart, 0.0), axis=0, keepdims=True)
            valid = jnp.clip(cnt_b - (bidx - start_b) * EXP_BLOCK, 0.0, float(EXP_BLOCK))
            n_blk = jnp.sum(nblk, axis=0, keepdims=True)
            mrow = lax.broadcasted_iota(jnp.int32, (8, META_LANES), 0)
            meta = jnp.where(mrow == 0, blk_e, jnp.where(mrow == 1, valid, jnp.where(mrow == 2, n_blk, 0.0)))
            meta_ref[...] = meta.astype(jnp.int32)

        lane = ROUTER_LANES
        before = (lax.broadcasted_iota(jnp.int32, (lane, lane), 0)
                  < lax.broadcasted_iota(jnp.int32, (lane, lane), 1))
        tri = jnp.where(before, 1.0, 0.0).astype(BF16)
        run = base_ref[...]
        d1, d2 = [], []
        for k in range(tm // lane):
            piece = slice(k * lane, (k + 1) * lane)
            ck = cnt[:, piece]
            pos = run + jnp.dot(ck.astype(BF16), tri, preferred_element_type=F32)
            d1.append(jnp.sum(jnp.where(sel1[:, piece], pos, 0.0), axis=0, keepdims=True))
            d2.append(jnp.sum(jnp.where(sel2[:, piece], pos, 0.0), axis=0, keepdims=True))
            run = run + jnp.sum(ck, axis=1, keepdims=True)
        dest_ref[...] = jnp.concatenate(
            [jnp.concatenate(d1, axis=1), jnp.concatenate(d2, axis=1)], axis=0).astype(jnp.int32)
        base_ref[...] = run
        g_w = 1.0 / jnp.sum(jnp.exp(gl - gmax), axis=0, keepdims=True)
        e21 = jnp.exp(m2 - m1)
        p1 = 1.0 / (1.0 + e21)
        rid = lax.broadcasted_iota(jnp.int32, (ROUTER_LANES, tm), 0)
        gt = jnp.where(rid == 0, g_w * p1, jnp.where(rid == 1, g_w * (e21 * p1), 0.0))
        gates_ref[...] = gt.T


def _route_call(logits):
    return pl.pallas_call(
        _route_kernel,
        grid=(2, N_ROUTE_TILES),
        in_specs=[pl.BlockSpec((ROUTE_TM, ROUTER_LANES), lambda ph, t: (t, 0))],
        out_specs=[pl.BlockSpec((MOE_TOPK, ROUTE_TM), lambda ph, t: (0, t * ph)),
                   pl.BlockSpec((ROUTE_TM, ROUTER_LANES), lambda ph, t: (t * ph, 0)),
                   pl.BlockSpec((8, META_LANES), lambda ph, t: (0, 0))],
        out_shape=[jax.ShapeDtypeStruct((MOE_TOPK, T_ALL), jnp.int32),
                   jax.ShapeDtypeStruct((T_ALL, ROUTER_LANES), F32),
                   jax.ShapeDtypeStruct((8, META_LANES), jnp.int32)],
        scratch_shapes=[pltpu.VMEM((MOE_EXPERTS, 1), F32), pltpu.VMEM((MOE_EXPERTS, 1), F32)],
        compiler_params=_params("arbitrary", "arbitrary"),
        name="route",
    )(logits)


def _sc_mesh():
    return plsc.VectorSubcoreMesh(core_axis_name="c", subcore_axis_name="s")


def _sc_token_offset(j):
    wid = lax.axis_index("s") * SC_CORES + lax.axis_index("c")
    return pl.multiple_of(wid * SC_ROWS_PER_WORKER + j * SC_CHUNK, 8)


def _dispatch_body(h_hbm, d0_hbm, d1_hbm, out_hbm, i0_v, i1_v, rows_v, sem0, sem1, sem2):
    @pl.loop(0, SC_ROWS_PER_WORKER // SC_CHUNK)
    def _(j):
        off = _sc_token_offset(j)
        c0 = pltpu.async_copy(d0_hbm.at[pl.ds(off, SC_CHUNK)], i0_v, sem0)
        c1 = pltpu.async_copy(d1_hbm.at[pl.ds(off, SC_CHUNK)], i1_v, sem1)
        c2 = pltpu.async_copy(h_hbm.at[pl.ds(off, SC_CHUNK)], rows_v, sem2)
        c0.wait()
        c1.wait()
        c2.wait()
        s0 = pltpu.async_copy(rows_v, out_hbm.at[i0_v], sem0)
        s1 = pltpu.async_copy(rows_v, out_hbm.at[i1_v], sem1)
        s0.wait()
        s1.wait()


def _dispatch_call(h2, dest0, dest1):
    return pl.kernel(
        _dispatch_body,
        out_type=jax.ShapeDtypeStruct((P_ROWS, DP), U32),
        mesh=_sc_mesh(),
        scratch_types=[pltpu.VMEM((SC_CHUNK,), jnp.int32), pltpu.VMEM((SC_CHUNK,), jnp.int32),
                       pltpu.VMEM((SC_CHUNK, DP), U32),
                       pltpu.SemaphoreType.DMA, pltpu.SemaphoreType.DMA, pltpu.SemaphoreType.DMA],
        name="moe_dispatch",
    )(h2, dest0, dest1)


def _combine_body(yb_hbm, d0_hbm, d1_hbm, o0_hbm, o1_hbm, i0_v, i1_v, r0_v, r1_v, sem0, sem1):
    @pl.loop(0, SC_ROWS_PER_WORKER // SC_CHUNK)
    def _(j):
        off = _sc_token_offset(j)
        c0 = pltpu.async_copy(d0_hbm.at[pl.ds(off, SC_CHUNK)], i0_v, sem0)
        c1 = pltpu.async_copy(d1_hbm.at[pl.ds(off, SC_CHUNK)], i1_v, sem1)
        c0.wait()
        c1.wait()
        g0 = pltpu.async_copy(yb_hbm.at[i0_v], r0_v, sem0)
        g1 = pltpu.async_copy(yb_hbm.at[i1_v], r1_v, sem1)
        g0.wait()
        g1.wait()
        w0 = pltpu.async_copy(r0_v, o0_hbm.at[pl.ds(off, SC_CHUNK)], sem0)
        w1 = pltpu.async_copy(r1_v, o1_hbm.at[pl.ds(off, SC_CHUNK)], sem1)
        w0.wait()
        w1.wait()


def _combine_call(yb, dest0, dest1):
    out = jax.ShapeDtypeStruct((T_ALL, DP), U32)
    return pl.kernel(
        _combine_body,
        out_type=(out, out),
        mesh=_sc_mesh(),
        scratch_types=[pltpu.VMEM((SC_CHUNK,), jnp.int32), pltpu.VMEM((SC_CHUNK,), jnp.int32),
                       pltpu.VMEM((SC_CHUNK, DP), U32), pltpu.VMEM((SC_CHUNK, DP), U32),
                       pltpu.SemaphoreType.DMA, pltpu.SemaphoreType.DMA],
        name="moe_combine",
    )(yb, dest0, dest1)


def _moe(h2, logits, w1, w3, w2, layer):
    dest, gates, meta = _route_call(logits)
    dest0, dest1 = dest[0], dest[1]
    xb = _dispatch_call(h2, dest0, dest1)
    yb = _expert_call(meta[0, :N_EXP_BLOCKS], meta[1, :N_EXP_BLOCKS], meta[2, :1], xb, w1, w3, w2, layer)
    yg0, yg1 = _combine_call(yb, dest0, dest1)
    return yg0, yg1, gates


def _final_kernel(x_ref, yg0_ref, yg1_ref, gates_ref, modp_ref, gain_ref, o_ref):
    x = _add_moe(x_ref[...], yg0_ref, yg1_ref, gates_ref, modp_ref)
    o_ref[...] = _rms(x) * gain_ref[...]


def _final_call(x, prev, gain, row0, n_rows):
    rows = ROWS_WIDE
    tile0 = row0 // rows

    def tile(width):
        return pl.BlockSpec((rows, width), lambda i: (tile0 + i, 0))

    return pl.pallas_call(
        _final_kernel,
        grid=(n_rows // rows,),
        in_specs=[tile(D), tile(DP), tile(DP), tile(ROUTER_LANES),
                  pl.BlockSpec((rows // GROUP, 6, D), lambda i: (tile0 + i, 0, 0)), _resident((1, D))],
        out_specs=pl.BlockSpec((rows, D), lambda i: (i, 0)),
        out_shape=jax.ShapeDtypeStruct((n_rows, D), F32),
        compiler_params=_params("parallel"),
        name="final_norm",
    )(x, *prev, gain)


def _rope_tables():
    pos = np.concatenate([np.tile(np.arange(SEQ), BATCH),
                          np.tile(PAST_LEN + np.arange(DEC_SEQ), DEC_BATCH)]).astype(np.float32)
    inv = (ROPE_BASE ** (-np.arange(ROPE_HALF, dtype=np.float32) / ROPE_HALF)).astype(np.float32)
    ang = (pos[:, None] * inv[None, :]).astype(np.float32).astype(np.float64)
    return jnp.asarray(np.cos(ang), F32), jnp.asarray(np.sin(ang), F32)


def kernel(x_prompt, x_sample, c_prompt, c_sample, state_ret, ada_w, ada_b, norm1_g, norm2_g, ret_w_in,
           ret_w_out, gm_w_in, gm_b_in, gm_ln_g, gm_ln_b, gm_w_s, gm_b_s, gm_w_out, gm_b_out, moe_w_rg,
           moe_b_rg, moe_w_re, moe_b_re, moe_w1, moe_w3, moe_w2, final_g):
    x = (x_prompt.reshape(T_PROMPT, D), x_sample.reshape(T_SAMPLE, D))
    c_all = jnp.concatenate([c_prompt, c_sample], axis=0)
    mod_all = _ada_call(c_all, ada_w, ada_b).reshape(DEPTH, N_SEQ, 6, D)
    cos, sin = _rope_tables()
    dec = _ret_row_scales()
    ret_w_in_b, ret_w_out_b = ret_w_in.astype(BF16), ret_w_out.astype(BF16)
    gm_w_in_b, gm_w_out_b = gm_w_in.astype(BF16), gm_w_out.astype(BF16)

    ret_prompt = ret_sample = gm_sample = None
    prev = None
    for i in range(DEPTH):
        j = i // 2
        mod = jnp.concatenate([
            jnp.broadcast_to(mod_all[i, :BATCH, None], (BATCH, SEQ // GROUP, 6, D)).reshape(-1, 6, D),
            jnp.broadcast_to(mod_all[i, BATCH:, None], (DEC_BATCH, DEC_SEQ // GROUP, 6, D)).reshape(-1, 6, D),
        ], axis=0)
        g1 = norm1_g[i].reshape(1, D)
        g2 = norm2_g[i].reshape(1, D)
        w_r = jnp.pad(jnp.concatenate([moe_w_re[i], moe_w_rg[i]], axis=1),
                      ((0, 0), (0, ROUTER_LANES - MOE_GROUPS - MOE_EXPERTS)))
        w_r_hi = w_r.astype(BF16)
        w_r_lo = (w_r - w_r_hi.astype(F32)).astype(BF16)
        w_r = jnp.concatenate([w_r_hi, w_r_lo], axis=1)
        b_r = jnp.pad(jnp.concatenate([moe_b_re[i].reshape(-1), moe_b_rg[i]]),
                      (0, ROUTER_LANES - MOE_GROUPS - MOE_EXPERTS)).reshape(1, ROUTER_LANES)
        if i % 2 == 0:
            x, p = _ret_proj_call(x, prev, mod, g1, ret_w_in_b, j, cos, sin, dec)
            y_p, ret_prompt = _ret_core_call(p, None, ret_prompt, j, BATCH, SEQ, RET_CHUNK_PROMPT, 0)
            y_s, ret_sample = _ret_core_call(p, state_ret, ret_sample, j, DEC_BATCH, DEC_SEQ,
                                             RET_CHUNK_SAMPLE, T_PROMPT)
            x, h2, logits = _ret_out_call(y_p, y_s, ret_w_out_b, j, x, mod, g2, w_r, b_r)
        else:
            x, u, v, gm_sample = _gm_proj_call(x, prev, mod, g1, gm_w_in_b, j,
                                               gm_b_in[j].reshape(1, GM_FFN), gm_ln_g[j].reshape(1, GM_HALF),
                                               gm_ln_b[j].reshape(1, GM_HALF), gm_sample)
            ws_bd, bs_bd = _gm_block_diag(gm_w_s[j], gm_b_s[j])
            x, h2, logits = _gm_out_call(u, v, ws_bd, bs_bd, gm_w_out_b, j,
                                         gm_b_out[j].reshape(1, D), x, mod, g2, w_r, b_r)
        yg0, yg1, gates = _moe(h2, logits, moe_w1, moe_w3, moe_w2, i)
        prev = (yg0, yg1, gates, mod)

    fg = final_g.reshape(1, D)
    y_prompt = _final_call(x, prev, fg, 0, T_PROMPT).reshape(BATCH, SEQ, D)
    y_sample = _final_call(x, prev, fg, T_PROMPT, T_SAMPLE).reshape(DEC_BATCH, DEC_SEQ, D)
    return (y_prompt, y_sample, ret_prompt, ret_sample,
            gm_sample.reshape(N_GM, DEC_BATCH, DEC_SEQ, GM_HALF))
```

```python
import functools

import numpy as np
import jax
import jax.numpy as jnp
from jax import lax
from jax.experimental import pallas as pl
from jax.experimental.pallas import tpu as pltpu
from jax.experimental.pallas import tpu_sc as plsc

F32 = jnp.float32
BF16 = jnp.bfloat16
U32 = jnp.uint32

D = 1024
BATCH, SEQ = 4, 4096
DEC_BATCH, DEC_SEQ = 16, 64
PAST_LEN = 4096
DEPTH = 4
N_RET = (DEPTH + 1) // 2
N_GM = DEPTH // 2
N_SEQ = BATCH + DEC_BATCH

RET_HEADS, RET_DK, RET_DV = 4, 256, 512
RET_QK = RET_HEADS * RET_DK
RET_V = RET_HEADS * RET_DV
RET_IN = 2 * RET_QK + 2 * RET_V
ROPE_BASE = 10000.0
ROPE_HALF = RET_DK // 2

GM_FFN = 6 * D
GM_HALF = GM_FFN // 2
GM_GROUPS = 4
GM_GDIM = GM_HALF // GM_GROUPS
GM_CHUNK = 128

MOE_GROUPS, MOE_PER_GROUP = 4, 8
MOE_EXPERTS = MOE_GROUPS * MOE_PER_GROUP
MOE_TOPK = 2
MOE_HIDDEN = 512
EPS = 1e-6

GROUP = 64
T_PROMPT = BATCH * SEQ
T_SAMPLE = DEC_BATCH * DEC_SEQ
T_ALL = T_PROMPT + T_SAMPLE
N_GROUPS = T_ALL // GROUP
ROWS_WIDE = 512
ROWS_GM_PROJ = 256

RET_CHUNK_PROMPT = 256
RET_CHUNK_SAMPLE = DEC_SEQ

GM_MIX = 256

EXP_BLOCK = 512
N_ASSIGN = T_ALL * MOE_TOPK
N_EXP_BLOCKS = -(-(N_ASSIGN + MOE_EXPERTS * (EXP_BLOCK - 1)) // EXP_BLOCK)
P_ROWS = N_EXP_BLOCKS * EXP_BLOCK
ROUTER_LANES = 128
ROUTE_TM = 1024
N_ROUTE_TILES = T_ALL // ROUTE_TM
META_LANES = 256
assert META_LANES >= N_EXP_BLOCKS

DP = D // 2
SC_CORES, SC_SUBCORES = 2, 16
SC_WORKERS = SC_CORES * SC_SUBCORES
SC_ROWS_PER_WORKER = T_ALL // SC_WORKERS
SC_CHUNK = 32
assert SC_ROWS_PER_WORKER % SC_CHUNK == 0 and SC_CHUNK % 8 == 0

V7X_VMEM_LIMIT_BYTES = 56 * 1024 * 1024


def _params(*sem):
    return pltpu.CompilerParams(dimension_semantics=sem, vmem_limit_bytes=V7X_VMEM_LIMIT_BYTES)


def _resident(shape):
    nd = len(shape)
    return pl.BlockSpec(shape, lambda *_: (0,) * nd, pipeline_mode=pl.Buffered(1))


def _resident_layer(shape, layer):
    nd = len(shape)
    return pl.BlockSpec((1,) + shape, lambda *_: (layer,) + (0,) * nd, pipeline_mode=pl.Buffered(1))


def _rms(x):
    return x * lax.rsqrt(jnp.mean(x * x, axis=-1, keepdims=True) + EPS)


def _silu(x):
    return x * jax.nn.sigmoid(x)


def _per_group(x2d, fn):
    rows = x2d.shape[0]
    return fn(x2d.reshape(rows // GROUP, GROUP, D)).reshape(rows, D)


def _norm_mod(x, gain_ref, mod_ref, shift_idx):
    y = _rms(x) * gain_ref[...]
    scale = mod_ref[:, shift_idx + 1:shift_idx + 2, :]
    shift = mod_ref[:, shift_idx:shift_idx + 1, :]
    return _per_group(y, lambda y3: y3 * (1.0 + scale) + shift)


def _pack_bf16_pairs(x):
    lo = lax.bitcast_convert_type(x[:, :DP].astype(BF16).astype(F32), U32)
    hi = lax.bitcast_convert_type(x[:, DP:].astype(BF16).astype(F32), U32)
    return (lo >> 16) | (hi & U32(0xFFFF0000))


def _unpack_bf16_pairs(w):
    lo = lax.bitcast_convert_type(w << 16, F32)
    hi = lax.bitcast_convert_type(w & U32(0xFFFF0000), F32)
    return jnp.concatenate([lo, hi], axis=1)


def _add_moe(x, yg0_ref, yg1_ref, gates_ref, modp_ref):
    g = gates_ref[...]
    y = g[:, 0:1] * _unpack_bf16_pairs(yg0_ref[...]) + g[:, 1:2] * _unpack_bf16_pairs(yg1_ref[...])
    gate2 = modp_ref[:, 5:6, :]
    return x + _per_group(y, lambda y3: y3 * gate2)


ADA_TN = 1536


def _ada_kernel(c_ref, w_ref, b_ref, o_ref):
    c = c_ref[...]
    s = _silu(c).astype(BF16)
    o_ref[0] = jnp.dot(s, w_ref[0].astype(BF16), preferred_element_type=F32) + b_ref[0]


def _ada_call(c_all, ada_w, ada_b):
    nt = 6 * D // ADA_TN
    return pl.pallas_call(
        _ada_kernel,
        grid=(DEPTH, nt),
        in_specs=[
            pl.BlockSpec((N_SEQ, D), lambda i, j: (0, 0)),
            pl.BlockSpec((1, D, ADA_TN), lambda i, j: (i, 0, j)),
            pl.BlockSpec((1, 1, ADA_TN), lambda i, j: (i, 0, j)),
        ],
        out_specs=pl.BlockSpec((1, N_SEQ, ADA_TN), lambda i, j: (i, 0, j)),
        out_shape=jax.ShapeDtypeStruct((DEPTH, N_SEQ, 6 * D), F32),
        compiler_params=_params("parallel", "parallel"),
        name="ada_modulation",
    )(c_all, ada_w, ada_b.reshape(DEPTH, 1, 6 * D))


def _n_tiles(rows):
    return T_ALL // rows


def _n_prompt_tiles(rows):
    return T_PROMPT // rows


def _row_spec(rows, width):
    return pl.BlockSpec((rows, width), lambda i: (i, 0))


def _mod_spec(rows):
    return pl.BlockSpec((rows // GROUP, 6, D), lambda i: (i, 0, 0))


def _prev_specs(rows):
    return [_row_spec(rows, DP), _row_spec(rows, DP), _row_spec(rows, ROUTER_LANES), _mod_spec(rows)]


def _prompt_rows_spec(rows, width):
    last = _n_prompt_tiles(rows) - 1
    return pl.BlockSpec((rows, width), lambda i: (jnp.minimum(i, last), 0))


def _sample_rows_spec(rows, width):
    npt = _n_prompt_tiles(rows)
    return pl.BlockSpec((rows, width), lambda i: (jnp.maximum(i - npt, 0), 0))


def _ret_proj_kernel(has_prev, n_prompt_tiles, *refs):
    if has_prev:
        (x_ref, yg0_ref, yg1_ref, gates_ref, modp_ref, mod_ref, gain_ref, w_ref, cos_ref, sin_ref, dec_ref,
         xo_ref, p_ref) = refs
        x = _add_moe(x_ref[...], yg0_ref, yg1_ref, gates_ref, modp_ref)
    else:
        xp_ref, xs_ref, mod_ref, gain_ref, w_ref, cos_ref, sin_ref, dec_ref, xo_ref, p_ref = refs
        x = jnp.where(pl.program_id(0) < n_prompt_tiles, xp_ref[...], xs_ref[...])
    xo_ref[...] = x
    hb = _norm_mod(x, gain_ref, mod_ref, 0).astype(BF16)
    cos = cos_ref[...]
    sin = sin_ref[...]
    dec = dec_ref[...]
    for j in range(2 * RET_HEADS):
        lo = j * RET_DK
        acc = jnp.dot(hb, w_ref[0, :, lo:lo + RET_DK], preferred_element_type=F32)
        x1 = acc[:, :ROPE_HALF]
        x2 = acc[:, ROPE_HALF:]
        scale = dec[:, j:j + 1]
        p_ref[:, lo:lo + ROPE_HALF] = ((x1 * cos - x2 * sin) * scale).astype(BF16)
        p_ref[:, lo + ROPE_HALF:lo + RET_DK] = ((x1 * sin + x2 * cos) * scale).astype(BF16)
    for j in range(2 * RET_HEADS):
        lo = 2 * RET_QK + j * RET_DV
        acc = jnp.dot(hb, w_ref[0, :, lo:lo + RET_DV], preferred_element_type=F32)
        if j >= RET_HEADS:
            acc = _silu(acc)
        p_ref[:, lo:lo + RET_DV] = acc.astype(BF16)


def _ret_proj_call(x, prev, mod, gain, w_in, layer, cos, sin, dec):
    rows = ROWS_WIDE
    has_prev = prev is not None
    if has_prev:
        in_specs = [_row_spec(rows, D)] + _prev_specs(rows)
        args = [x] + list(prev)
    else:
        in_specs = [_prompt_rows_spec(rows, D), _sample_rows_spec(rows, D)]
        args = list(x)
    in_specs += [_mod_spec(rows), _resident((1, D)), _resident_layer((D, RET_IN), layer),
                 _row_spec(rows, ROPE_HALF), _row_spec(rows, ROPE_HALF), _row_spec(rows, 2 * RET_HEADS)]
    args += [mod, gain, w_in, cos, sin, dec]
    return pl.pallas_call(
        functools.partial(_ret_proj_kernel, has_prev, _n_prompt_tiles(rows)),
        grid=(_n_tiles(rows),),
        in_specs=in_specs,
        out_specs=[_row_spec(rows, D), _row_spec(rows, RET_IN)],
        out_shape=[jax.ShapeDtypeStruct((T_ALL, D), F32), jax.ShapeDtypeStruct((T_ALL, RET_IN), BF16)],
        compiler_params=_params("parallel"),
        name="ret_proj",
    )(*args)


def _ret_core_kernel(has_s0, n_chunks, layer, *refs):
    refs = list(refs)
    p_ref = refs.pop(0)
    s0_ref = refs.pop(0) if has_s0 else None
    causal_ref, cd_ref = refs[:2]
    y_ref, so_ref, s_ref = refs[-3:]
    c = pl.program_id(1)

    @pl.when(c == 0)
    def _():
        if has_s0:
            s_ref[...] = s0_ref[0, 0]
        else:
            s_ref[...] = jnp.zeros_like(s_ref)

    for h in range(RET_HEADS):
        qb = p_ref[:, h * RET_DK:(h + 1) * RET_DK]
        kb = p_ref[:, RET_QK + h * RET_DK:RET_QK + (h + 1) * RET_DK]
        vb = p_ref[:, 2 * RET_QK + h * RET_DV:2 * RET_QK + (h + 1) * RET_DV]
        gb = p_ref[:, 2 * RET_QK + RET_V + h * RET_DV:2 * RET_QK + RET_V + (h + 1) * RET_DV]
        scores = lax.dot_general(qb, kb, (((1,), (1,)), ((), ())), preferred_element_type=F32)
        scores = scores * causal_ref[...]
        s_old = s_ref[h]
        o = (jnp.dot(scores.astype(BF16), vb, preferred_element_type=F32)
             + jnp.dot(qb, s_old.astype(BF16), preferred_element_type=F32))
        s_ref[h] = cd_ref[h][:, 0:1] * (s_old + lax.dot_general(
            kb, vb, (((0,), (0,)), ((), ())), preferred_element_type=F32))
        y_ref[:, h * RET_DV:(h + 1) * RET_DV] = (gb.astype(F32) * _rms(o)).astype(BF16)

    @pl.when(c == n_chunks - 1)
    def _():
        so_ref[0, 0] = s_ref[...]
        if layer == 0:
            for later in range(1, N_RET):
                so_ref[later, 0] = jnp.zeros_like(s_ref)


def _ret_log_gamma():
    return np.log1p(-np.exp2(-5.0 - np.arange(RET_HEADS, dtype=np.float64)))


def _ret_chunk_tables(cl):
    idx = np.arange(cl)
    causal = (idx[:, None] >= idx[None, :]).astype(np.float32)
    cd = np.broadcast_to(np.exp(_ret_log_gamma() * cl)[:, None, None], (RET_HEADS, 1, 128))
    return jnp.asarray(causal, F32), jnp.asarray(cd, F32)


def _ret_row_scales():
    c = np.concatenate([np.arange(T_PROMPT) % RET_CHUNK_PROMPT,
                        np.arange(T_SAMPLE) % RET_CHUNK_SAMPLE]).astype(np.float64)
    e = (c[:, None] + 1.0) * _ret_log_gamma()[None, :]
    return jnp.asarray(np.concatenate([np.exp(e), np.exp(-e) * RET_DK ** -0.5], axis=1), F32)


def _ret_core_call(p, s0, states, layer, n_seq, seq_len, cl, row0):
    has_s0 = s0 is not None
    n_chunks = seq_len // cl
    rb0 = row0 // cl
    state = (RET_HEADS, RET_DK, RET_DV)
    in_specs = [pl.BlockSpec((cl, RET_IN), lambda b, c: (rb0 + b * n_chunks + c, 0))]
    args = [p]
    if has_s0:
        in_specs.append(pl.BlockSpec((1, 1) + state, lambda b, c: (layer, b, 0, 0, 0)))
        args.append(s0)
    in_specs += [_resident((cl, cl)), _resident((RET_HEADS, 1, 128))]
    args += list(_ret_chunk_tables(cl))
    if layer == 0:
        assert states is None
        state_spec = pl.BlockSpec((N_RET, 1) + state, lambda b, c: (0, b, 0, 0, 0))
        aliases = {}
    else:
        in_specs.append(pl.BlockSpec(memory_space=pl.ANY))
        args.append(states)
        state_spec = pl.BlockSpec((1, 1) + state, lambda b, c: (layer, b, 0, 0, 0))
        aliases = {len(args) - 1: 1}
    return pl.pallas_call(
        functools.partial(_ret_core_kernel, has_s0, n_chunks, layer),
        grid=(n_seq, n_chunks),
        in_specs=in_specs,
        out_specs=[pl.BlockSpec((cl, RET_V), lambda b, c: (b * n_chunks + c, 0)), state_spec],
        out_shape=[jax.ShapeDtypeStruct((n_seq * seq_len, RET_V), BF16),
                   jax.ShapeDtypeStruct((N_RET, n_seq) + state, F32)],
        scratch_shapes=[pltpu.VMEM(state, F32)],
        input_output_aliases=aliases,
        compiler_params=_params("parallel", "arbitrary"),
        name="ret_core",
    )(*args)


def _residual_router(acc, x_ref, mod_ref, gain_ref, wr_ref, br_ref, xo_ref, h2_ref, lg_ref):
    gate1 = mod_ref[:, 2:3, :]
    xn = x_ref[...] + _per_group(acc, lambda a3: a3 * gate1)
    xo_ref[...] = xn
    h2 = _norm_mod(xn, gain_ref, mod_ref, 3)
    h2_ref[...] = _pack_bf16_pairs(h2)
    hh = jnp.dot(h2.astype(BF16), wr_ref[...], preferred_element_type=F32)
    lg_ref[...] = hh[:, :ROUTER_LANES] + hh[:, ROUTER_LANES:] + br_ref[...]


def _mix_out_specs(rows):
    return [_row_spec(rows, D), _row_spec(rows, DP), _row_spec(rows, ROUTER_LANES)]


_MIX_OUT_SHAPE = [
    jax.ShapeDtypeStruct((T_ALL, D), F32),
    jax.ShapeDtypeStruct((T_ALL, DP), U32),
    jax.ShapeDtypeStruct((T_ALL, ROUTER_LANES), F32),
]


def _router_specs():
    return [_resident((D, 2 * ROUTER_LANES)), _resident((1, ROUTER_LANES))]


def _ret_out_kernel(n_prompt_tiles, yp_ref, ys_ref, w_ref, x_ref, mod_ref, gain_ref, wr_ref, br_ref,
                    xo_ref, h2_ref, lg_ref):
    yin = jnp.where(pl.program_id(0) < n_prompt_tiles, yp_ref[...], ys_ref[...])
    acc = jnp.dot(yin, w_ref[0], preferred_element_type=F32)
    _residual_router(acc, x_ref, mod_ref, gain_ref, wr_ref, br_ref, xo_ref, h2_ref, lg_ref)


def _ret_out_call(y_prompt, y_sample, w_out, layer, x, mod, gain, w_r, b_r):
    rows = ROWS_WIDE
    return pl.pallas_call(
        functools.partial(_ret_out_kernel, _n_prompt_tiles(rows)),
        grid=(_n_tiles(rows),),
        in_specs=[_prompt_rows_spec(rows, RET_V), _sample_rows_spec(rows, RET_V),
                  _resident_layer((RET_V, D), layer), _row_spec(rows, D), _mod_spec(rows),
                  _resident((1, D))] + _router_specs(),
        out_specs=_mix_out_specs(rows),
        out_shape=_MIX_OUT_SHAPE,
        compiler_params=_params("parallel"),
        name="ret_out",
    )(y_prompt, y_sample, w_out, x, mod, gain, w_r, b_r)


GM_TN = 512


_GELU_C = float(np.sqrt(2.0 / np.pi))


def _gelu_tanh(x):
    hx = 0.5 * x
    return hx * jnp.tanh(x * (_GELU_C + (_GELU_C * 0.044715) * (x * x))) + hx


def _gm_proj_kernel(n_prompt_tiles, layer, *refs):
    x_ref, yg0_ref, yg1_ref, gates_ref, modp_ref, mod_ref, gain_ref, w_ref, b_ref, lg_ref, lb_ref = refs[:11]
    xo_ref, u_ref, v_ref, vs_ref, vraw_ref = refs[-5:]
    x = _add_moe(x_ref[...], yg0_ref, yg1_ref, gates_ref, modp_ref)
    xo_ref[...] = x
    hb = _norm_mod(x, gain_ref, mod_ref, 0).astype(BF16)
    rows = hb.shape[0]
    is_sample = pl.program_id(0) >= n_prompt_tiles

    def gelu_chunk(lo):
        z = jnp.dot(hb, w_ref[0, :, lo:lo + GM_TN], preferred_element_type=F32) + b_ref[:, lo:lo + GM_TN]
        return _gelu_tanh(z.astype(BF16))

    s1 = jnp.zeros((rows, 128), F32)
    s2 = jnp.zeros((rows, 128), F32)
    for lo in range(0, GM_HALF, GM_TN):
        gz = gelu_chunk(GM_HALF + lo)
        vraw_ref[:, lo:lo + GM_TN] = gz
        gf = gz.astype(F32)
        for k in range(0, GM_TN, 128):
            piece = gf[:, k:k + 128]
            s1 = s1 + piece
            s2 = s2 + piece * piece
    mu = jnp.sum(s1, axis=-1, keepdims=True) * (1.0 / GM_HALF)
    var = jnp.sum(s2, axis=-1, keepdims=True) * (1.0 / GM_HALF) - mu * mu
    rstd = lax.rsqrt(var + EPS)
    shift = -mu * rstd

    for lo in range(0, GM_HALF, GM_TN):
        u_ref[:, lo:lo + GM_TN] = gelu_chunk(lo)
        vn = ((vraw_ref[:, lo:lo + GM_TN].astype(F32) * rstd + shift) * lg_ref[:, lo:lo + GM_TN]
              + lb_ref[:, lo:lo + GM_TN])
        v_ref[:, lo:lo + GM_TN] = vn.astype(BF16)
        vs_ref[0, :, lo:lo + GM_TN] = vn

    if layer == 0:
        @pl.when(is_sample)
        def _():
            for later in range(1, N_GM):
                vs_ref[later] = jnp.zeros((rows, GM_HALF), F32)


def _gm_proj_call(x, prev, mod, gain, w_in, layer, b_in, ln_g, ln_b, vs_all):
    rows = ROWS_GM_PROJ
    npt = _n_prompt_tiles(rows)
    in_specs = [_row_spec(rows, D)] + _prev_specs(rows) + [
        _mod_spec(rows), _resident((1, D)), _resident_layer((D, GM_FFN), layer), _resident((1, GM_FFN)),
        _resident((1, GM_HALF)), _resident((1, GM_HALF))]
    args = [x, *prev, mod, gain, w_in, b_in, ln_g, ln_b]
    if layer == 0:
        assert vs_all is None
        vs_spec = pl.BlockSpec((N_GM, rows, GM_HALF), lambda i: (0, jnp.maximum(i - npt, 0), 0))
        aliases = {}
    else:
        in_specs.append(pl.BlockSpec(memory_space=pl.ANY))
        args.append(vs_all)
        vs_spec = pl.BlockSpec((1, rows, GM_HALF), lambda i: (layer, jnp.maximum(i - npt, 0), 0))
        aliases = {len(args) - 1: 3}
    return pl.pallas_call(
        functools.partial(_gm_proj_kernel, npt, layer),
        grid=(_n_tiles(rows),),
        in_specs=in_specs,
        out_specs=[_row_spec(rows, D), _row_spec(rows, GM_HALF), _row_spec(rows, GM_HALF), vs_spec],
        out_shape=[jax.ShapeDtypeStruct((T_ALL, D), F32),
                   jax.ShapeDtypeStruct((T_ALL, GM_HALF), BF16),
                   jax.ShapeDtypeStruct((T_ALL, GM_HALF), BF16),
                   jax.ShapeDtypeStruct((N_GM, T_SAMPLE, GM_HALF), F32)],
        scratch_shapes=[pltpu.VMEM((rows, GM_HALF), BF16)],
        input_output_aliases=aliases,
        compiler_params=_params("arbitrary"),
        name="gm_proj",
    )(*args)


def _gm_out_kernel(u_ref, v_ref, ws_ref, bs_ref, w_ref, bo_ref, x_ref, mod_ref, gain_ref, wr_ref, br_ref,
                   xo_ref, h2_ref, lg_ref):
    rows = u_ref.shape[0]
    pieces = []
    for r0 in range(0, rows, GM_MIX):
        acc = jnp.zeros((GM_MIX, D), F32)
        for g in range(GM_GROUPS):
            lo = g * GM_GDIM
            sp = jnp.dot(ws_ref[0, g], v_ref[r0:r0 + GM_MIX, lo:lo + GM_GDIM],
                         preferred_element_type=F32) + bs_ref[0, g]
            gated = (u_ref[r0:r0 + GM_MIX, lo:lo + GM_GDIM].astype(F32) * sp).astype(BF16)
            acc = acc + jnp.dot(gated, w_ref[0, lo:lo + GM_GDIM, :], preferred_element_type=F32)
        pieces.append(acc)
    acc = jnp.concatenate(pieces, axis=0) + bo_ref[...]
    _residual_router(acc, x_ref, mod_ref, gain_ref, wr_ref, br_ref, xo_ref, h2_ref, lg_ref)


def _gm_block_diag(w_s, b_s):
    mats, biases = [], []
    for cl in (GM_CHUNK, DEC_SEQ):
        tri = jnp.tril(jnp.ones((cl, cl), bool))
        blk = jnp.where(tri[None], w_s[:, :cl, :cl], 0.0)
        reps = GM_MIX // cl
        eye = jnp.eye(reps, dtype=w_s.dtype)
        bd = jnp.einsum("ab,gts->gatbs", eye, blk).reshape(GM_GROUPS, GM_MIX, GM_MIX)
        mats.append(bd)
        biases.append(jnp.tile(b_s[:, :cl], (1, reps))[:, :, None])
    return jnp.stack(mats).astype(BF16), jnp.stack(biases).astype(F32)


def _gm_out_call(u, v, ws_bd, bs_bd, w_out, layer, b_out, x, mod, gain, w_r, b_r):
    rows = ROWS_WIDE
    npt = _n_prompt_tiles(rows)

    def variant(i):
        return jnp.where(i >= npt, 1, 0)

    return pl.pallas_call(
        _gm_out_kernel,
        grid=(_n_tiles(rows),),
        in_specs=[_row_spec(rows, GM_HALF), _row_spec(rows, GM_HALF),
                  pl.BlockSpec((1, GM_GROUPS, GM_MIX, GM_MIX), lambda i: (variant(i), 0, 0, 0)),
                  pl.BlockSpec((1, GM_GROUPS, GM_MIX, 1), lambda i: (variant(i), 0, 0, 0)),
                  _resident_layer((GM_HALF, D), layer), _resident((1, D)), _row_spec(rows, D),
                  _mod_spec(rows), _resident((1, D))] + _router_specs(),
        out_specs=_mix_out_specs(rows),
        out_shape=_MIX_OUT_SHAPE,
        compiler_params=_params("arbitrary"),
        name="gm_out",
    )(u, v, ws_bd, bs_bd, w_out, b_out, x, mod, gain, w_r, b_r)


def _expert_kernel(layer, be_ref, bv_ref, nb_ref, nx_ref, xb_ref, w1_hbm, w3_hbm, w2_hbm, yb_ref,
                   st1, st3, st2, w1s, w3s, w2s, sems, slot_ref):
    b = pl.program_id(0)

    def weight_copies(e, slot):
        return (pltpu.make_async_copy(w1_hbm.at[layer, e], st1.at[slot], sems.at[slot, 0]),
                pltpu.make_async_copy(w3_hbm.at[layer, e], st3.at[slot], sems.at[slot, 1]),
                pltpu.make_async_copy(w2_hbm.at[layer, e], st2.at[slot], sems.at[slot, 2]))

    @pl.when(b == 0)
    def _():
        slot_ref[0] = 0
        for cp in weight_copies(be_ref[0], 0):
            cp.start()

    @pl.when(b < nb_ref[0])
    def _():
        prev_e = be_ref[jnp.maximum(b - 1, 0)]

        @pl.when((b == 0) | (be_ref[b] != prev_e))
        def _():
            slot = slot_ref[0]
            for cp in weight_copies(be_ref[b], slot):
                cp.wait()
            w1s[...] = st1[slot].astype(BF16)
            w3s[...] = st3[slot].astype(BF16)
            w2s[...] = st2[slot].astype(BF16)

            @pl.when(nx_ref[b] >= 0)
            def _():
                for cp in weight_copies(nx_ref[b], 1 - slot):
                    cp.start()

            slot_ref[0] = 1 - slot

        row = lax.broadcasted_iota(jnp.int32, (EXP_BLOCK, 1), 0)
        xw = jnp.where(row < bv_ref[b], xb_ref[...], U32(0))
        x = _unpack_bf16_pairs(xw).astype(BF16)
        a = jnp.dot(x, w1s[...], preferred_element_type=F32)
        c = jnp.dot(x, w3s[...], preferred_element_type=F32)
        h = (_silu(a) * c).astype(BF16)
        yb_ref[...] = _pack_bf16_pairs(jnp.dot(h, w2s[...], preferred_element_type=F32))


def _expert_call(blk_e, blk_valid, n_blk, blk_next, xb, w1, w3, w2, layer):
    def blk(b, be, bv, nb, nx):
        return (jnp.minimum(b, nb[0] - 1), 0)

    up, down = (D, MOE_HIDDEN), (MOE_HIDDEN, D)
    grid_spec = pltpu.PrefetchScalarGridSpec(
        num_scalar_prefetch=4,
        grid=(N_EXP_BLOCKS,),
        in_specs=[pl.BlockSpec((EXP_BLOCK, DP), blk)] + [pl.BlockSpec(memory_space=pl.ANY)] * 3,
        out_specs=pl.BlockSpec((EXP_BLOCK, DP), blk),
        scratch_shapes=[pltpu.VMEM((2,) + up, F32), pltpu.VMEM((2,) + up, F32), pltpu.VMEM((2,) + down, F32),
                        pltpu.VMEM(up, BF16), pltpu.VMEM(up, BF16), pltpu.VMEM(down, BF16),
                        pltpu.SemaphoreType.DMA((2, 3)), pltpu.SMEM((1,), jnp.int32)],
    )
    return pl.pallas_call(
        functools.partial(_expert_kernel, layer),
        grid_spec=grid_spec,
        out_shape=jax.ShapeDtypeStruct((P_ROWS, DP), U32),
        compiler_params=_params("arbitrary"),
        name="experts",
    )(blk_e, blk_valid, n_blk, blk_next, xb, w1, w3, w2)


def _route_kernel(lg_ref, dest_ref, gates_ref, meta_ref, cnt_ref, base_ref):
    ph = pl.program_id(0)
    t = pl.program_id(1)
    tm = ROUTE_TM
    lt = lg_ref[...].T
    el = lt[0:MOE_EXPERTS]
    gl = lt[MOE_EXPERTS:MOE_EXPERTS + 8]
    gidx = lax.broadcasted_iota(jnp.int32, (8, tm), 0)
    neg = jnp.float32(-jnp.inf)
    gl = jnp.where(gidx < MOE_GROUPS, gl, neg)
    gmax = jnp.max(gl, axis=0, keepdims=True)
    grp = jnp.min(jnp.where(gl == gmax, gidx, MOE_GROUPS), axis=0, keepdims=True)
    eidx = lax.broadcasted_iota(jnp.int32, (MOE_EXPERTS, tm), 0)
    els = jnp.where((eidx >> 3) == grp, el, neg)
    m1 = jnp.max(els, axis=0, keepdims=True)
    i1 = jnp.min(jnp.where(els == m1, eidx, MOE_EXPERTS), axis=0, keepdims=True)
    els2 = jnp.where(eidx == i1, neg, els)
    m2 = jnp.max(els2, axis=0, keepdims=True)
    i2 = jnp.min(jnp.where(els2 == m2, eidx, MOE_EXPERTS), axis=0, keepdims=True)
    sel1 = eidx == i1
    sel2 = eidx == i2
    cnt = jnp.where(sel1 | sel2, 1.0, 0.0)
    tile_counts = jnp.sum(cnt, axis=1, keepdims=True)

    @pl.when(ph == 0)
    def _():
        @pl.when(t == 0)
        def _():
            cnt_ref[...] = jnp.zeros_like(cnt_ref)

        cnt_ref[...] += tile_counts

    @pl.when(ph == 1)
    def _():
        @pl.when(t == 0)
        def _():
            counts = cnt_ref[...]
            nblk = jnp.floor((counts + (EXP_BLOCK - 1.0)) * (1.0 / EXP_BLOCK))
            r = lax.broadcasted_iota(jnp.int32, (MOE_EXPERTS, MOE_EXPERTS), 0)
            c = lax.broadcasted_iota(jnp.int32, (MOE_EXPERTS, MOE_EXPERTS), 1)
            nblk_row = jnp.sum(jnp.where(r == c, nblk, 0.0), axis=0, keepdims=True)
            bstart = jnp.sum(jnp.where(c < r, nblk_row, 0.0), axis=1, keepdims=True)
            base_ref[...] = bstart * EXP_BLOCK
            bend = bstart + nblk
            bidx = lax.broadcasted_iota(jnp.int32, (1, META_LANES), 1).astype(F32)
            blk_e = jnp.minimum(jnp.sum(jnp.where(bidx >= bend, 1.0, 0.0), axis=0, keepdims=True),
                                MOE_EXPERTS - 1.0)
            erow = lax.broadcasted_iota(jnp.int32, (MOE_EXPERTS, META_LANES), 0).astype(F32)
            mine = erow == blk_e
            cnt_b = jnp.sum(jnp.where(mine, counts, 0.0), axis=0, keepdims=True)
            start_b = jnp.sum(jnp.where(mine, bstart, 0.0), axis=0, keepdims=True)
            valid = jnp.clip(cnt_b - (bidx - start_b) * EXP_BLOCK, 0.0, float(EXP_BLOCK))
            n_blk = jnp.sum(nblk, axis=0, keepdims=True)
            end_b = jnp.sum(jnp.where(mine, bend, 0.0), axis=0, keepdims=True)
            nxt = jnp.minimum(jnp.sum(jnp.where(end_b >= bend, 1.0, 0.0), axis=0, keepdims=True),
                              MOE_EXPERTS - 1.0)
            nxt = jnp.where(end_b < n_blk, nxt, -1.0)
            mrow = lax.broadcasted_iota(jnp.int32, (8, META_LANES), 0)
            meta = jnp.where(mrow == 0, blk_e, jnp.where(mrow == 1, valid, jnp.where(
                mrow == 2, n_blk, jnp.where(mrow == 3, nxt, 0.0))))
            meta_ref[...] = meta.astype(jnp.int32)

        lane = ROUTER_LANES
        before = (lax.broadcasted_iota(jnp.int32, (lane, lane), 0)
                  < lax.broadcasted_iota(jnp.int32, (lane, lane), 1))
        tri = jnp.where(before, 1.0, 0.0).astype(BF16)
        run = base_ref[...]
        d1, d2 = [], []
        for k in range(tm // lane):
            piece = slice(k * lane, (k + 1) * lane)
            ck = cnt[:, piece]
            pos = run + jnp.dot(ck.astype(BF16), tri, preferred_element_type=F32)
            d1.append(jnp.sum(jnp.where(sel1[:, piece], pos, 0.0), axis=0, keepdims=True))
            d2.append(jnp.sum(jnp.where(sel2[:, piece], pos, 0.0), axis=0, keepdims=True))
            run = run + jnp.sum(ck, axis=1, keepdims=True)
        dest_ref[...] = jnp.concatenate(
            [jnp.concatenate(d1, axis=1), jnp.concatenate(d2, axis=1)], axis=0).astype(jnp.int32)
        base_ref[...] = run
        g_w = 1.0 / jnp.sum(jnp.exp(gl - gmax), axis=0, keepdims=True)
        e21 = jnp.exp(m2 - m1)
        p1 = 1.0 / (1.0 + e21)
        rid = lax.broadcasted_iota(jnp.int32, (ROUTER_LANES, tm), 0)
        gt = jnp.where(rid == 0, g_w * p1, jnp.where(rid == 1, g_w * (e21 * p1), 0.0))
        gates_ref[...] = gt.T


def _route_call(logits):
    return pl.pallas_call(
        _route_kernel,
        grid=(2, N_ROUTE_TILES),
        in_specs=[pl.BlockSpec((ROUTE_TM, ROUTER_LANES), lambda ph, t: (t, 0))],
        out_specs=[pl.BlockSpec((MOE_TOPK, ROUTE_TM), lambda ph, t: (0, t * ph)),
                   pl.BlockSpec((ROUTE_TM, ROUTER_LANES), lambda ph, t: (t * ph, 0)),
                   pl.BlockSpec((8, META_LANES), lambda ph, t: (0, 0))],
        out_shape=[jax.ShapeDtypeStruct((MOE_TOPK, T_ALL), jnp.int32),
                   jax.ShapeDtypeStruct((T_ALL, ROUTER_LANES), F32),
                   jax.ShapeDtypeStruct((8, META_LANES), jnp.int32)],
        scratch_shapes=[pltpu.VMEM((MOE_EXPERTS, 1), F32), pltpu.VMEM((MOE_EXPERTS, 1), F32)],
        compiler_params=_params("arbitrary", "arbitrary"),
        name="route",
    )(logits)


def _sc_mesh():
    return plsc.VectorSubcoreMesh(core_axis_name="c", subcore_axis_name="s")


def _sc_token_offset(j):
    wid = lax.axis_index("s") * SC_CORES + lax.axis_index("c")
    return pl.multiple_of(wid * SC_ROWS_PER_WORKER + j * SC_CHUNK, 8)


def _dispatch_body(h_hbm, d0_hbm, d1_hbm, out_hbm, i0_v, i1_v, rows_v, sem0, sem1, sem2):
    @pl.loop(0, SC_ROWS_PER_WORKER // SC_CHUNK)
    def _(j):
        off = _sc_token_offset(j)
        c0 = pltpu.async_copy(d0_hbm.at[pl.ds(off, SC_CHUNK)], i0_v, sem0)
        c1 = pltpu.async_copy(d1_hbm.at[pl.ds(off, SC_CHUNK)], i1_v, sem1)
        c2 = pltpu.async_copy(h_hbm.at[pl.ds(off, SC_CHUNK)], rows_v, sem2)
        c0.wait()
        c1.wait()
        c2.wait()
        s0 = pltpu.async_copy(rows_v, out_hbm.at[i0_v], sem0)
        s1 = pltpu.async_copy(rows_v, out_hbm.at[i1_v], sem1)
        s0.wait()
        s1.wait()


def _dispatch_call(h2, dest0, dest1):
    return pl.kernel(
        _dispatch_body,
        out_type=jax.ShapeDtypeStruct((P_ROWS, DP), U32),
        mesh=_sc_mesh(),
        scratch_types=[pltpu.VMEM((SC_CHUNK,), jnp.int32), pltpu.VMEM((SC_CHUNK,), jnp.int32),
                       pltpu.VMEM((SC_CHUNK, DP), U32),
                       pltpu.SemaphoreType.DMA, pltpu.SemaphoreType.DMA, pltpu.SemaphoreType.DMA],
        name="moe_dispatch",
    )(h2, dest0, dest1)


def _combine_body(yb_hbm, d0_hbm, d1_hbm, o0_hbm, o1_hbm, i0_v, i1_v, r0_v, r1_v, sem0, sem1):
    @pl.loop(0, SC_ROWS_PER_WORKER // SC_CHUNK)
    def _(j):
        off = _sc_token_offset(j)
        c0 = pltpu.async_copy(d0_hbm.at[pl.ds(off, SC_CHUNK)], i0_v, sem0)
        c1 = pltpu.async_copy(d1_hbm.at[pl.ds(off, SC_CHUNK)], i1_v, sem1)
        c0.wait()
        c1.wait()
        g0 = pltpu.async_copy(yb_hbm.at[i0_v], r0_v, sem0)
        g1 = pltpu.async_copy(yb_hbm.at[i1_v], r1_v, sem1)
        g0.wait()
        g1.wait()
        w0 = pltpu.async_copy(r0_v, o0_hbm.at[pl.ds(off, SC_CHUNK)], sem0)
        w1 = pltpu.async_copy(r1_v, o1_hbm.at[pl.ds(off, SC_CHUNK)], sem1)
        w0.wait()
        w1.wait()


def _combine_call(yb, dest0, dest1):
    out = jax.ShapeDtypeStruct((T_ALL, DP), U32)
    return pl.kernel(
        _combine_body,
        out_type=(out, out),
        mesh=_sc_mesh(),
        scratch_types=[pltpu.VMEM((SC_CHUNK,), jnp.int32), pltpu.VMEM((SC_CHUNK,), jnp.int32),
                       pltpu.VMEM((SC_CHUNK, DP), U32), pltpu.VMEM((SC_CHUNK, DP), U32),
                       pltpu.SemaphoreType.DMA, pltpu.SemaphoreType.DMA],
        name="moe_combine",
    )(yb, dest0, dest1)


def _moe(h2, logits, w1, w3, w2, layer):
    dest, gates, meta = _route_call(logits)
    dest0, dest1 = dest[0], dest[1]
    xb = _dispatch_call(h2, dest0, dest1)
    yb = _expert_call(meta[0, :N_EXP_BLOCKS], meta[1, :N_EXP_BLOCKS], meta[2, :1], meta[3, :N_EXP_BLOCKS],
                      xb, w1, w3, w2, layer)
    yg0, yg1 = _combine_call(yb, dest0, dest1)
    return yg0, yg1, gates


def _final_kernel(x_ref, yg0_ref, yg1_ref, gates_ref, modp_ref, gain_ref, o_ref):
    x = _add_moe(x_ref[...], yg0_ref, yg1_ref, gates_ref, modp_ref)
    o_ref[...] = _rms(x) * gain_ref[...]


def _final_call(x, prev, gain, row0, n_rows):
    rows = ROWS_WIDE
    tile0 = row0 // rows

    def tile(width):
        return pl.BlockSpec((rows, width), lambda i: (tile0 + i, 0))

    return pl.pallas_call(
        _final_kernel,
        grid=(n_rows // rows,),
        in_specs=[tile(D), tile(DP), tile(DP), tile(ROUTER_LANES),
                  pl.BlockSpec((rows // GROUP, 6, D), lambda i: (tile0 + i, 0, 0)), _resident((1, D))],
        out_specs=pl.BlockSpec((rows, D), lambda i: (i, 0)),
        out_shape=jax.ShapeDtypeStruct((n_rows, D), F32),
        compiler_params=_params("parallel"),
        name="final_norm",
    )(x, *prev, gain)


def _rope_tables():
    pos = np.concatenate([np.tile(np.arange(SEQ), BATCH),
                          np.tile(PAST_LEN + np.arange(DEC_SEQ), DEC_BATCH)]).astype(np.float32)
    inv = (ROPE_BASE ** (-np.arange(ROPE_HALF, dtype=np.float32) / ROPE_HALF)).astype(np.float32)
    ang = (pos[:, None] * inv[None, :]).astype(np.float32).astype(np.float64)
    return jnp.asarray(np.cos(ang), F32), jnp.asarray(np.sin(ang), F32)


def kernel(x_prompt, x_sample, c_prompt, c_sample, state_ret, ada_w, ada_b, norm1_g, norm2_g, ret_w_in,
           ret_w_out, gm_w_in, gm_b_in, gm_ln_g, gm_ln_b, gm_w_s, gm_b_s, gm_w_out, gm_b_out, moe_w_rg,
           moe_b_rg, moe_w_re, moe_b_re, moe_w1, moe_w3, moe_w2, final_g):
    x = (x_prompt.reshape(T_PROMPT, D), x_sample.reshape(T_SAMPLE, D))
    c_all = jnp.concatenate([c_prompt, c_sample], axis=0)
    mod_all = _ada_call(c_all, ada_w, ada_b).reshape(DEPTH, N_SEQ, 6, D)
    cos, sin = _rope_tables()
    dec = _ret_row_scales()
    ret_w_in_b, ret_w_out_b = ret_w_in.astype(BF16), ret_w_out.astype(BF16)
    gm_w_in_b, gm_w_out_b = gm_w_in.astype(BF16), gm_w_out.astype(BF16)

    ret_prompt = ret_sample = gm_sample = None
    prev = None
    for i in range(DEPTH):
        j = i // 2
        mod = jnp.concatenate([
            jnp.broadcast_to(mod_all[i, :BATCH, None], (BATCH, SEQ // GROUP, 6, D)).reshape(-1, 6, D),
            jnp.broadcast_to(mod_all[i, BATCH:, None], (DEC_BATCH, DEC_SEQ // GROUP, 6, D)).reshape(-1, 6, D),
        ], axis=0)
        g1 = norm1_g[i].reshape(1, D)
        g2 = norm2_g[i].reshape(1, D)
        w_r = jnp.pad(jnp.concatenate([moe_w_re[i], moe_w_rg[i]], axis=1),
                      ((0, 0), (0, ROUTER_LANES - MOE_GROUPS - MOE_EXPERTS)))
        w_r_hi = w_r.astype(BF16)
        w_r_lo = (w_r - w_r_hi.astype(F32)).astype(BF16)
        w_r = jnp.concatenate([w_r_hi, w_r_lo], axis=1)
        b_r = jnp.pad(jnp.concatenate([moe_b_re[i].reshape(-1), moe_b_rg[i]]),
                      (0, ROUTER_LANES - MOE_GROUPS - MOE_EXPERTS)).reshape(1, ROUTER_LANES)
        if i % 2 == 0:
            x, p = _ret_proj_call(x, prev, mod, g1, ret_w_in_b, j, cos, sin, dec)
            y_p, ret_prompt = _ret_core_call(p, None, ret_prompt, j, BATCH, SEQ, RET_CHUNK_PROMPT, 0)
            y_s, ret_sample = _ret_core_call(p, state_ret, ret_sample, j, DEC_BATCH, DEC_SEQ,
                                             RET_CHUNK_SAMPLE, T_PROMPT)
            x, h2, logits = _ret_out_call(y_p, y_s, ret_w_out_b, j, x, mod, g2, w_r, b_r)
        else:
            x, u, v, gm_sample = _gm_proj_call(x, prev, mod, g1, gm_w_in_b, j,
                                               gm_b_in[j].reshape(1, GM_FFN), gm_ln_g[j].reshape(1, GM_HALF),
                                               gm_ln_b[j].reshape(1, GM_HALF), gm_sample)
            ws_bd, bs_bd = _gm_block_diag(gm_w_s[j], gm_b_s[j])
            x, h2, logits = _gm_out_call(u, v, ws_bd, bs_bd, gm_w_out_b, j,
                                         gm_b_out[j].reshape(1, D), x, mod, g2, w_r, b_r)
        yg0, yg1, gates = _moe(h2, logits, moe_w1, moe_w3, moe_w2, i)
        prev = (yg0, yg1, gates, mod)

    fg = final_g.reshape(1, D)
    y_prompt = _final_call(x, prev, fg, 0, T_PROMPT).reshape(BATCH, SEQ, D)
    y_sample = _final_call(x, prev, fg, T_PROMPT, T_SAMPLE).reshape(DEC_BATCH, DEC_SEQ, D)
    return (y_prompt, y_sample, ret_prompt, ret_sample,
            gm_sample.reshape(N_GM, DEC_BATCH, DEC_SEQ, GM_HALF))
```

```python
import functools

import numpy as np
import jax
import jax.numpy as jnp
from jax import lax
from jax.experimental import pallas as pl
from jax.experimental.pallas import tpu as pltpu
from jax.experimental.pallas import tpu_sc as plsc

F32 = jnp.float32
BF16 = jnp.bfloat16
U32 = jnp.uint32

D = 1024
BATCH, SEQ = 4, 4096
DEC_BATCH, DEC_SEQ = 16, 64
PAST_LEN = 4096
DEPTH = 4
N_RET = (DEPTH + 1) // 2
N_GM = DEPTH // 2
N_SEQ = BATCH + DEC_BATCH

RET_HEADS, RET_DK, RET_DV = 4, 256, 512
RET_QK = RET_HEADS * RET_DK
RET_V = RET_HEADS * RET_DV
RET_IN = 2 * RET_QK + 2 * RET_V
ROPE_BASE = 10000.0
ROPE_HALF = RET_DK // 2

GM_FFN = 6 * D
GM_HALF = GM_FFN // 2
GM_GROUPS = 4
GM_GDIM = GM_HALF // GM_GROUPS
GM_CHUNK = 128

MOE_GROUPS, MOE_PER_GROUP = 4, 8
MOE_EXPERTS = MOE_GROUPS * MOE_PER_GROUP
MOE_TOPK = 2
MOE_HIDDEN = 512
EPS = 1e-6

GROUP = 64
T_PROMPT = BATCH * SEQ
T_SAMPLE = DEC_BATCH * DEC_SEQ
T_ALL = T_PROMPT + T_SAMPLE
N_GROUPS = T_ALL // GROUP
ROWS_WIDE = 512
ROWS_GM_PROJ = 256

RET_CHUNK_PROMPT = 256
RET_CHUNK_SAMPLE = DEC_SEQ

GM_MIX = 256

EXP_BLOCK = 512
N_ASSIGN = T_ALL * MOE_TOPK
N_EXP_BLOCKS = -(-(N_ASSIGN + MOE_EXPERTS * (EXP_BLOCK - 1)) // EXP_BLOCK)
P_ROWS = N_EXP_BLOCKS * EXP_BLOCK
ROUTER_LANES = 128
ROUTE_TM = 1024
N_ROUTE_TILES = T_ALL // ROUTE_TM
META_LANES = 256
assert META_LANES >= N_EXP_BLOCKS

DP = D // 2
SC_CORES, SC_SUBCORES = 2, 16
SC_WORKERS = SC_CORES * SC_SUBCORES
SC_ROWS_PER_WORKER = T_ALL // SC_WORKERS
SC_CHUNK = 32
assert SC_ROWS_PER_WORKER % SC_CHUNK == 0 and SC_CHUNK % 8 == 0

V7X_VMEM_LIMIT_BYTES = 56 * 1024 * 1024


def _params(*sem):
    return pltpu.CompilerParams(dimension_semantics=sem, vmem_limit_bytes=V7X_VMEM_LIMIT_BYTES)


def _resident(shape):
    nd = len(shape)
    return pl.BlockSpec(shape, lambda *_: (0,) * nd, pipeline_mode=pl.Buffered(1))


def _rms(x):
    return x * lax.rsqrt(jnp.mean(x * x, axis=-1, keepdims=True) + EPS)


def _silu(x):
    return x * jax.nn.sigmoid(x)


def _per_group(x2d, fn):
    rows = x2d.shape[0]
    return fn(x2d.reshape(rows // GROUP, GROUP, D)).reshape(rows, D)


def _norm_mod(x, gain_ref, mod_ref, shift_idx):
    y = _rms(x) * gain_ref[...]
    scale = mod_ref[:, shift_idx + 1:shift_idx + 2, :]
    shift = mod_ref[:, shift_idx:shift_idx + 1, :]
    return _per_group(y, lambda y3: y3 * (1.0 + scale) + shift)


def _pack_bf16_pairs(x):
    lo = lax.bitcast_convert_type(x[:, :DP].astype(BF16).astype(F32), U32)
    hi = lax.bitcast_convert_type(x[:, DP:].astype(BF16).astype(F32), U32)
    return (lo >> 16) | (hi & U32(0xFFFF0000))


def _unpack_bf16_pairs(w):
    lo = lax.bitcast_convert_type(w << 16, F32)
    hi = lax.bitcast_convert_type(w & U32(0xFFFF0000), F32)
    return jnp.concatenate([lo, hi], axis=1)


def _add_moe(x, yg0_ref, yg1_ref, gates_ref, modp_ref):
    g = gates_ref[...]
    y = g[:, 0:1] * _unpack_bf16_pairs(yg0_ref[...]) + g[:, 1:2] * _unpack_bf16_pairs(yg1_ref[...])
    gate2 = modp_ref[:, 5:6, :]
    return x + _per_group(y, lambda y3: y3 * gate2)


ADA_TN = 1536


def _ada_kernel(c_ref, w_ref, b_ref, o_ref):
    c = c_ref[...]
    s = _silu(c).astype(BF16)
    o_ref[0] = jnp.dot(s, w_ref[0].astype(BF16), preferred_element_type=F32) + b_ref[0]


def _ada_call(c_all, ada_w, ada_b, layer):
    return pl.pallas_call(
        _ada_kernel,
        grid=(6 * D // ADA_TN,),
        in_specs=[
            pl.BlockSpec((N_SEQ, D), lambda j: (0, 0)),
            pl.BlockSpec((1, D, ADA_TN), lambda j: (layer, 0, j)),
            pl.BlockSpec((1, 1, ADA_TN), lambda j: (layer, 0, j)),
        ],
        out_specs=pl.BlockSpec((1, N_SEQ, ADA_TN), lambda j: (0, 0, j)),
        out_shape=jax.ShapeDtypeStruct((1, N_SEQ, 6 * D), F32),
        compiler_params=_params("parallel"),
        name="ada_modulation",
    )(c_all, ada_w, ada_b.reshape(DEPTH, 1, 6 * D))[0]


def _n_tiles(rows):
    return T_ALL // rows


def _n_prompt_tiles(rows):
    return T_PROMPT // rows


def _row_spec(rows, width):
    return pl.BlockSpec((rows, width), lambda i: (i, 0))


def _mod_spec(rows):
    return pl.BlockSpec((rows // GROUP, 6, D), lambda i: (i, 0, 0))


def _prev_specs(rows):
    return [_row_spec(rows, DP), _row_spec(rows, DP), _row_spec(rows, ROUTER_LANES), _mod_spec(rows)]


def _prompt_rows_spec(rows, width):
    last = _n_prompt_tiles(rows) - 1
    return pl.BlockSpec((rows, width), lambda i: (jnp.minimum(i, last), 0))


def _sample_rows_spec(rows, width):
    npt = _n_prompt_tiles(rows)
    return pl.BlockSpec((rows, width), lambda i: (jnp.maximum(i - npt, 0), 0))


def _ret_proj_kernel(has_prev, n_prompt_tiles, *refs):
    if has_prev:
        (x_ref, yg0_ref, yg1_ref, gates_ref, modp_ref, mod_ref, gain_ref, w_ref, cos_ref, sin_ref, dec_ref,
         xo_ref, p_ref) = refs
        x = _add_moe(x_ref[...], yg0_ref, yg1_ref, gates_ref, modp_ref)
    else:
        xp_ref, xs_ref, mod_ref, gain_ref, w_ref, cos_ref, sin_ref, dec_ref, xo_ref, p_ref = refs
        x = jnp.where(pl.program_id(0) < n_prompt_tiles, xp_ref[...], xs_ref[...])
    xo_ref[...] = x
    hb = _norm_mod(x, gain_ref, mod_ref, 0).astype(BF16)
    cos = cos_ref[...]
    sin = sin_ref[...]
    dec = dec_ref[...]
    for j in range(2 * RET_HEADS):
        lo = j * RET_DK
        acc = jnp.dot(hb, w_ref[0, :, lo:lo + RET_DK], preferred_element_type=F32)
        x1 = acc[:, :ROPE_HALF]
        x2 = acc[:, ROPE_HALF:]
        scale = dec[:, j:j + 1]
        p_ref[:, lo:lo + ROPE_HALF] = ((x1 * cos - x2 * sin) * scale).astype(BF16)
        p_ref[:, lo + ROPE_HALF:lo + RET_DK] = ((x1 * sin + x2 * cos) * scale).astype(BF16)
    for j in range(2 * RET_HEADS):
        lo = 2 * RET_QK + j * RET_DV
        acc = jnp.dot(hb, w_ref[0, :, lo:lo + RET_DV], preferred_element_type=F32)
        if j >= RET_HEADS:
            acc = _silu(acc)
        p_ref[:, lo:lo + RET_DV] = acc.astype(BF16)


def _ret_proj_call(x, prev, mod, gain, w_in, cos, sin, dec):
    rows = ROWS_WIDE
    has_prev = prev is not None
    if has_prev:
        in_specs = [_row_spec(rows, D)] + _prev_specs(rows)
        args = [x] + list(prev)
    else:
        in_specs = [_prompt_rows_spec(rows, D), _sample_rows_spec(rows, D)]
        args = list(x)
    in_specs += [_mod_spec(rows), _resident((1, D)), _resident((1, D, RET_IN)),
                 _row_spec(rows, ROPE_HALF), _row_spec(rows, ROPE_HALF), _row_spec(rows, 2 * RET_HEADS)]
    args += [mod, gain, w_in, cos, sin, dec]
    return pl.pallas_call(
        functools.partial(_ret_proj_kernel, has_prev, _n_prompt_tiles(rows)),
        grid=(_n_tiles(rows),),
        in_specs=in_specs,
        out_specs=[_row_spec(rows, D), _row_spec(rows, RET_IN)],
        out_shape=[jax.ShapeDtypeStruct((T_ALL, D), F32), jax.ShapeDtypeStruct((T_ALL, RET_IN), BF16)],
        compiler_params=_params("parallel"),
        name="ret_proj",
    )(*args)


def _ret_core_kernel(has_s0, n_chunks, layer, *refs):
    refs = list(refs)
    p_ref = refs.pop(0)
    s0_ref = refs.pop(0) if has_s0 else None
    causal_ref, cd_ref = refs[:2]
    y_ref, so_ref, s_ref = refs[-3:]
    c = pl.program_id(1)

    @pl.when(c == 0)
    def _():
        if has_s0:
            s_ref[...] = s0_ref[0, 0]
        else:
            s_ref[...] = jnp.zeros_like(s_ref)

    for h in range(RET_HEADS):
        qb = p_ref[:, h * RET_DK:(h + 1) * RET_DK]
        kb = p_ref[:, RET_QK + h * RET_DK:RET_QK + (h + 1) * RET_DK]
        vb = p_ref[:, 2 * RET_QK + h * RET_DV:2 * RET_QK + (h + 1) * RET_DV]
        gb = p_ref[:, 2 * RET_QK + RET_V + h * RET_DV:2 * RET_QK + RET_V + (h + 1) * RET_DV]
        scores = lax.dot_general(qb, kb, (((1,), (1,)), ((), ())), preferred_element_type=F32)
        scores = scores * causal_ref[...]
        s_old = s_ref[h]
        o = (jnp.dot(scores.astype(BF16), vb, preferred_element_type=F32)
             + jnp.dot(qb, s_old.astype(BF16), preferred_element_type=F32))
        s_ref[h] = cd_ref[h][:, 0:1] * (s_old + lax.dot_general(
            kb, vb, (((0,), (0,)), ((), ())), preferred_element_type=F32))
        y_ref[:, h * RET_DV:(h + 1) * RET_DV] = (gb.astype(F32) * _rms(o)).astype(BF16)

    @pl.when(c == n_chunks - 1)
    def _():
        so_ref[0, 0] = s_ref[...]
        if layer == 0:
            for later in range(1, N_RET):
                so_ref[later, 0] = jnp.zeros_like(s_ref)


def _ret_log_gamma():
    return np.log1p(-np.exp2(-5.0 - np.arange(RET_HEADS, dtype=np.float64)))


def _ret_chunk_tables(cl):
    idx = np.arange(cl)
    causal = (idx[:, None] >= idx[None, :]).astype(np.float32)
    cd = np.broadcast_to(np.exp(_ret_log_gamma() * cl)[:, None, None], (RET_HEADS, 1, 128))
    return jnp.asarray(causal, F32), jnp.asarray(cd, F32)


def _ret_row_scales():
    c = np.concatenate([np.arange(T_PROMPT) % RET_CHUNK_PROMPT,
                        np.arange(T_SAMPLE) % RET_CHUNK_SAMPLE]).astype(np.float64)
    e = (c[:, None] + 1.0) * _ret_log_gamma()[None, :]
    return jnp.asarray(np.concatenate([np.exp(e), np.exp(-e) * RET_DK ** -0.5], axis=1), F32)


def _ret_core_call(p, s0, states, layer, n_seq, seq_len, cl, row0):
    has_s0 = s0 is not None
    n_chunks = seq_len // cl
    rb0 = row0 // cl
    state = (RET_HEADS, RET_DK, RET_DV)
    in_specs = [pl.BlockSpec((cl, RET_IN), lambda b, c: (rb0 + b * n_chunks + c, 0))]
    args = [p]
    if has_s0:
        in_specs.append(pl.BlockSpec((1, 1) + state, lambda b, c: (layer, b, 0, 0, 0)))
        args.append(s0)
    in_specs += [_resident((cl, cl)), _resident((RET_HEADS, 1, 128))]
    args += list(_ret_chunk_tables(cl))
    if layer == 0:
        assert states is None
        state_spec = pl.BlockSpec((N_RET, 1) + state, lambda b, c: (0, b, 0, 0, 0))
        aliases = {}
    else:
        in_specs.append(pl.BlockSpec(memory_space=pl.ANY))
        args.append(states)
        state_spec = pl.BlockSpec((1, 1) + state, lambda b, c: (layer, b, 0, 0, 0))
        aliases = {len(args) - 1: 1}
    return pl.pallas_call(
        functools.partial(_ret_core_kernel, has_s0, n_chunks, layer),
        grid=(n_seq, n_chunks),
        in_specs=in_specs,
        out_specs=[pl.BlockSpec((cl, RET_V), lambda b, c: (b * n_chunks + c, 0)), state_spec],
        out_shape=[jax.ShapeDtypeStruct((n_seq * seq_len, RET_V), BF16),
                   jax.ShapeDtypeStruct((N_RET, n_seq) + state, F32)],
        scratch_shapes=[pltpu.VMEM(state, F32)],
        input_output_aliases=aliases,
        compiler_params=_params("parallel", "arbitrary"),
        name="ret_core",
    )(*args)


def _residual_router(acc, x_ref, mod_ref, gain_ref, wr_ref, br_ref, xo_ref, h2_ref, lg_ref):
    gate1 = mod_ref[:, 2:3, :]
    xn = x_ref[...] + _per_group(acc, lambda a3: a3 * gate1)
    xo_ref[...] = xn
    h2 = _norm_mod(xn, gain_ref, mod_ref, 3)
    h2_ref[...] = _pack_bf16_pairs(h2)
    hh = jnp.dot(h2.astype(BF16), wr_ref[...], preferred_element_type=F32)
    lg_ref[...] = hh[:, :ROUTER_LANES] + hh[:, ROUTER_LANES:] + br_ref[...]


def _mix_out_specs(rows):
    return [_row_spec(rows, D), _row_spec(rows, DP), _row_spec(rows, ROUTER_LANES)]


_MIX_OUT_SHAPE = [
    jax.ShapeDtypeStruct((T_ALL, D), F32),
    jax.ShapeDtypeStruct((T_ALL, DP), U32),
    jax.ShapeDtypeStruct((T_ALL, ROUTER_LANES), F32),
]


def _router_specs():
    return [_resident((D, 2 * ROUTER_LANES)), _resident((1, ROUTER_LANES))]


def _ret_out_kernel(n_prompt_tiles, yp_ref, ys_ref, w_ref, x_ref, mod_ref, gain_ref, wr_ref, br_ref,
                    xo_ref, h2_ref, lg_ref):
    yin = jnp.where(pl.program_id(0) < n_prompt_tiles, yp_ref[...], ys_ref[...])
    acc = jnp.dot(yin, w_ref[0], preferred_element_type=F32)
    _residual_router(acc, x_ref, mod_ref, gain_ref, wr_ref, br_ref, xo_ref, h2_ref, lg_ref)


def _ret_out_call(y_prompt, y_sample, w_out, x, mod, gain, w_r, b_r):
    rows = ROWS_WIDE
    return pl.pallas_call(
        functools.partial(_ret_out_kernel, _n_prompt_tiles(rows)),
        grid=(_n_tiles(rows),),
        in_specs=[_prompt_rows_spec(rows, RET_V), _sample_rows_spec(rows, RET_V),
                  _resident((1, RET_V, D)), _row_spec(rows, D), _mod_spec(rows),
                  _resident((1, D))] + _router_specs(),
        out_specs=_mix_out_specs(rows),
        out_shape=_MIX_OUT_SHAPE,
        compiler_params=_params("parallel"),
        name="ret_out",
    )(y_prompt, y_sample, w_out, x, mod, gain, w_r, b_r)


GM_TN = 512


_GELU_C = float(np.sqrt(2.0 / np.pi))


def _gelu_tanh(x):
    hx = 0.5 * x
    return hx * jnp.tanh(x * (_GELU_C + (_GELU_C * 0.044715) * (x * x))) + hx


def _gm_proj_kernel(n_prompt_tiles, layer, *refs):
    x_ref, yg0_ref, yg1_ref, gates_ref, modp_ref, mod_ref, gain_ref, w_ref, b_ref, lg_ref, lb_ref = refs[:11]
    xo_ref, u_ref, v_ref, vs_ref, vraw_ref = refs[-5:]
    x = _add_moe(x_ref[...], yg0_ref, yg1_ref, gates_ref, modp_ref)
    xo_ref[...] = x
    hb = _norm_mod(x, gain_ref, mod_ref, 0).astype(BF16)
    rows = hb.shape[0]
    is_sample = pl.program_id(0) >= n_prompt_tiles

    def gelu_chunk(lo):
        z = jnp.dot(hb, w_ref[0, :, lo:lo + GM_TN], preferred_element_type=F32) + b_ref[:, lo:lo + GM_TN]
        return _gelu_tanh(z.astype(BF16))

    s1 = jnp.zeros((rows, 128), F32)
    s2 = jnp.zeros((rows, 128), F32)
    for lo in range(0, GM_HALF, GM_TN):
        gz = gelu_chunk(GM_HALF + lo)
        vraw_ref[:, lo:lo + GM_TN] = gz
        gf = gz.astype(F32)
        for k in range(0, GM_TN, 128):
            piece = gf[:, k:k + 128]
            s1 = s1 + piece
            s2 = s2 + piece * piece
    mu = jnp.sum(s1, axis=-1, keepdims=True) * (1.0 / GM_HALF)
    var = jnp.sum(s2, axis=-1, keepdims=True) * (1.0 / GM_HALF) - mu * mu
    rstd = lax.rsqrt(var + EPS)
    shift = -mu * rstd

    for lo in range(0, GM_HALF, GM_TN):
        u_ref[:, lo:lo + GM_TN] = gelu_chunk(lo)
        vn = ((vraw_ref[:, lo:lo + GM_TN].astype(F32) * rstd + shift) * lg_ref[:, lo:lo + GM_TN]
              + lb_ref[:, lo:lo + GM_TN])
        v_ref[:, lo:lo + GM_TN] = vn.astype(BF16)
        vs_ref[0, :, lo:lo + GM_TN] = vn

    if layer == 0:
        @pl.when(is_sample)
        def _():
            for later in range(1, N_GM):
                vs_ref[later] = jnp.zeros((rows, GM_HALF), F32)


def _gm_proj_call(x, prev, mod, gain, w_in, layer, b_in, ln_g, ln_b, vs_all):
    rows = ROWS_GM_PROJ
    npt = _n_prompt_tiles(rows)
    in_specs = [_row_spec(rows, D)] + _prev_specs(rows) + [
        _mod_spec(rows), _resident((1, D)), _resident((1, D, GM_FFN)), _resident((1, GM_FFN)),
        _resident((1, GM_HALF)), _resident((1, GM_HALF))]
    args = [x, *prev, mod, gain, w_in, b_in, ln_g, ln_b]
    if layer == 0:
        assert vs_all is None
        vs_spec = pl.BlockSpec((N_GM, rows, GM_HALF), lambda i: (0, jnp.maximum(i - npt, 0), 0))
        aliases = {}
    else:
        in_specs.append(pl.BlockSpec(memory_space=pl.ANY))
        args.append(vs_all)
        vs_spec = pl.BlockSpec((1, rows, GM_HALF), lambda i: (layer, jnp.maximum(i - npt, 0), 0))
        aliases = {len(args) - 1: 3}
    return pl.pallas_call(
        functools.partial(_gm_proj_kernel, npt, layer),
        grid=(_n_tiles(rows),),
        in_specs=in_specs,
        out_specs=[_row_spec(rows, D), _row_spec(rows, GM_HALF), _row_spec(rows, GM_HALF), vs_spec],
        out_shape=[jax.ShapeDtypeStruct((T_ALL, D), F32),
                   jax.ShapeDtypeStruct((T_ALL, GM_HALF), BF16),
                   jax.ShapeDtypeStruct((T_ALL, GM_HALF), BF16),
                   jax.ShapeDtypeStruct((N_GM, T_SAMPLE, GM_HALF), F32)],
        scratch_shapes=[pltpu.VMEM((rows, GM_HALF), BF16)],
        input_output_aliases=aliases,
        compiler_params=_params("arbitrary"),
        name="gm_proj",
    )(*args)


def _gm_out_kernel(u_ref, v_ref, ws_ref, bs_ref, w_ref, bo_ref, x_ref, mod_ref, gain_ref, wr_ref, br_ref,
                   xo_ref, h2_ref, lg_ref):
    rows = u_ref.shape[0]
    pieces = []
    for r0 in range(0, rows, GM_MIX):
        acc = jnp.zeros((GM_MIX, D), F32)
        for g in range(GM_GROUPS):
            lo = g * GM_GDIM
            sp = jnp.dot(ws_ref[0, g], v_ref[r0:r0 + GM_MIX, lo:lo + GM_GDIM],
                         preferred_element_type=F32) + bs_ref[0, g]
            gated = (u_ref[r0:r0 + GM_MIX, lo:lo + GM_GDIM].astype(F32) * sp).astype(BF16)
            acc = acc + jnp.dot(gated, w_ref[0, lo:lo + GM_GDIM, :], preferred_element_type=F32)
        pieces.append(acc)
    acc = jnp.concatenate(pieces, axis=0) + bo_ref[...]
    _residual_router(acc, x_ref, mod_ref, gain_ref, wr_ref, br_ref, xo_ref, h2_ref, lg_ref)


def _gm_block_diag(w_s, b_s):
    mats, biases = [], []
    for cl in (GM_CHUNK, DEC_SEQ):
        tri = jnp.tril(jnp.ones((cl, cl), bool))
        blk = jnp.where(tri[None], w_s[:, :cl, :cl], 0.0)
        reps = GM_MIX // cl
        eye = jnp.eye(reps, dtype=w_s.dtype)
        bd = jnp.einsum("ab,gts->gatbs", eye, blk).reshape(GM_GROUPS, GM_MIX, GM_MIX)
        mats.append(bd)
        biases.append(jnp.tile(b_s[:, :cl], (1, reps))[:, :, None])
    return jnp.stack(mats).astype(BF16), jnp.stack(biases).astype(F32)


def _gm_out_call(u, v, ws_bd, bs_bd, w_out, b_out, x, mod, gain, w_r, b_r):
    rows = ROWS_WIDE
    npt = _n_prompt_tiles(rows)

    def variant(i):
        return jnp.where(i >= npt, 1, 0)

    return pl.pallas_call(
        _gm_out_kernel,
        grid=(_n_tiles(rows),),
        in_specs=[_row_spec(rows, GM_HALF), _row_spec(rows, GM_HALF),
                  pl.BlockSpec((1, GM_GROUPS, GM_MIX, GM_MIX), lambda i: (variant(i), 0, 0, 0)),
                  pl.BlockSpec((1, GM_GROUPS, GM_MIX, 1), lambda i: (variant(i), 0, 0, 0)),
                  _resident((1, GM_HALF, D)), _resident((1, D)), _row_spec(rows, D),
                  _mod_spec(rows), _resident((1, D))] + _router_specs(),
        out_specs=_mix_out_specs(rows),
        out_shape=_MIX_OUT_SHAPE,
        compiler_params=_params("arbitrary"),
        name="gm_out",
    )(u, v, ws_bd, bs_bd, w_out, b_out, x, mod, gain, w_r, b_r)


def _expert_kernel(layer, be_ref, bv_ref, nb_ref, nx_ref, xb_ref, w1_hbm, w3_hbm, w2_hbm, yb_ref,
                   st1, st3, st2, w1s, w3s, w2s, sems, slot_ref):
    b = pl.program_id(0)

    def weight_copies(e, slot):
        return (pltpu.make_async_copy(w1_hbm.at[layer, e], st1.at[slot], sems.at[slot, 0]),
                pltpu.make_async_copy(w3_hbm.at[layer, e], st3.at[slot], sems.at[slot, 1]),
                pltpu.make_async_copy(w2_hbm.at[layer, e], st2.at[slot], sems.at[slot, 2]))

    @pl.when(b == 0)
    def _():
        slot_ref[0] = 0
        for cp in weight_copies(be_ref[0], 0):
            cp.start()

    @pl.when(b < nb_ref[0])
    def _():
        prev_e = be_ref[jnp.maximum(b - 1, 0)]

        @pl.when((b == 0) | (be_ref[b] != prev_e))
        def _():
            slot = slot_ref[0]
            for cp in weight_copies(be_ref[b], slot):
                cp.wait()
            w1s[...] = st1[slot].astype(BF16)
            w3s[...] = st3[slot].astype(BF16)
            w2s[...] = st2[slot].astype(BF16)

            @pl.when(nx_ref[b] >= 0)
            def _():
                for cp in weight_copies(nx_ref[b], 1 - slot):
                    cp.start()

            slot_ref[0] = 1 - slot

        row = lax.broadcasted_iota(jnp.int32, (EXP_BLOCK, 1), 0)
        xw = jnp.where(row < bv_ref[b], xb_ref[...], U32(0))
        x = _unpack_bf16_pairs(xw).astype(BF16)
        a = jnp.dot(x, w1s[...], preferred_element_type=F32)
        c = jnp.dot(x, w3s[...], preferred_element_type=F32)
        h = (_silu(a) * c).astype(BF16)
        yb_ref[...] = _pack_bf16_pairs(jnp.dot(h, w2s[...], preferred_element_type=F32))


def _expert_call(blk_e, blk_valid, n_blk, blk_next, xb, w1, w3, w2, layer):
    def blk(b, be, bv, nb, nx):
        return (jnp.minimum(b, nb[0] - 1), 0)

    up, down = (D, MOE_HIDDEN), (MOE_HIDDEN, D)
    grid_spec = pltpu.PrefetchScalarGridSpec(
        num_scalar_prefetch=4,
        grid=(N_EXP_BLOCKS,),
        in_specs=[pl.BlockSpec((EXP_BLOCK, DP), blk)] + [pl.BlockSpec(memory_space=pl.ANY)] * 3,
        out_specs=pl.BlockSpec((EXP_BLOCK, DP), blk),
        scratch_shapes=[pltpu.VMEM((2,) + up, F32), pltpu.VMEM((2,) + up, F32), pltpu.VMEM((2,) + down, F32),
                        pltpu.VMEM(up, BF16), pltpu.VMEM(up, BF16), pltpu.VMEM(down, BF16),
                        pltpu.SemaphoreType.DMA((2, 3)), pltpu.SMEM((1,), jnp.int32)],
    )
    return pl.pallas_call(
        functools.partial(_expert_kernel, layer),
        grid_spec=grid_spec,
        out_shape=jax.ShapeDtypeStruct((P_ROWS, DP), U32),
        compiler_params=_params("arbitrary"),
        name="experts",
    )(blk_e, blk_valid, n_blk, blk_next, xb, w1, w3, w2)


def _route_kernel(lg_ref, dest_ref, gates_ref, meta_ref, cnt_ref, base_ref):
    ph = pl.program_id(0)
    t = pl.program_id(1)
    tm = ROUTE_TM
    lt = lg_ref[...].T
    el = lt[0:MOE_EXPERTS]
    gl = lt[MOE_EXPERTS:MOE_EXPERTS + 8]
    gidx = lax.broadcasted_iota(jnp.int32, (8, tm), 0)
    neg = jnp.float32(-jnp.inf)
    gl = jnp.where(gidx < MOE_GROUPS, gl, neg)
    gmax = jnp.max(gl, axis=0, keepdims=True)
    grp = jnp.min(jnp.where(gl == gmax, gidx, MOE_GROUPS), axis=0, keepdims=True)
    eidx = lax.broadcasted_iota(jnp.int32, (MOE_EXPERTS, tm), 0)
    els = jnp.where((eidx >> 3) == grp, el, neg)
    m1 = jnp.max(els, axis=0, keepdims=True)
    i1 = jnp.min(jnp.where(els == m1, eidx, MOE_EXPERTS), axis=0, keepdims=True)
    els2 = jnp.where(eidx == i1, neg, els)
    m2 = jnp.max(els2, axis=0, keepdims=True)
    i2 = jnp.min(jnp.where(els2 == m2, eidx, MOE_EXPERTS), axis=0, keepdims=True)
    sel1 = eidx == i1
    sel2 = eidx == i2
    cnt = jnp.where(sel1 | sel2, 1.0, 0.0)
    tile_counts = jnp.sum(cnt, axis=1, keepdims=True)

    @pl.when(ph == 0)
    def _():
        @pl.when(t == 0)
        def _():
            cnt_ref[...] = jnp.zeros_like(cnt_ref)

        cnt_ref[...] += tile_counts

    @pl.when(ph == 1)
    def _():
        @pl.when(t == 0)
        def _():
            counts = cnt_ref[...]
            nblk = jnp.floor((counts + (EXP_BLOCK - 1.0)) * (1.0 / EXP_BLOCK))
            r = lax.broadcasted_iota(jnp.int32, (MOE_EXPERTS, MOE_EXPERTS), 0)
            c = lax.broadcasted_iota(jnp.int32, (MOE_EXPERTS, MOE_EXPERTS), 1)
            nblk_row = jnp.sum(jnp.where(r == c, nblk, 0.0), axis=0, keepdims=True)
            bstart = jnp.sum(jnp.where(c < r, nblk_row, 0.0), axis=1, keepdims=True)
            base_ref[...] = bstart * EXP_BLOCK
            bend = bstart + nblk
            bidx = lax.broadcasted_iota(jnp.int32, (1, META_LANES), 1).astype(F32)
            blk_e = jnp.minimum(jnp.sum(jnp.where(bidx >= bend, 1.0, 0.0), axis=0, keepdims=True),
                                MOE_EXPERTS - 1.0)
            erow = lax.broadcasted_iota(jnp.int32, (MOE_EXPERTS, META_LANES), 0).astype(F32)
            mine = erow == blk_e
            cnt_b = jnp.sum(jnp.where(mine, counts, 0.0), axis=0, keepdims=True)
            start_b = jnp.sum(jnp.where(mine, bstart, 0.0), axis=0, keepdims=True)
            valid = jnp.clip(cnt_b - (bidx - start_b) * EXP_BLOCK, 0.0, float(EXP_BLOCK))
            n_blk = jnp.sum(nblk, axis=0, keepdims=True)
            end_b = jnp.sum(jnp.where(mine, bend, 0.0), axis=0, keepdims=True)
            nxt = jnp.minimum(jnp.sum(jnp.where(end_b >= bend, 1.0, 0.0), axis=0, keepdims=True),
                              MOE_EXPERTS - 1.0)
            nxt = jnp.where(end_b < n_blk, nxt, -1.0)
            mrow = lax.broadcasted_iota(jnp.int32, (8, META_LANES), 0)
            meta = jnp.where(mrow == 0, blk_e, jnp.where(mrow == 1, valid, jnp.where(
                mrow == 2, n_blk, jnp.where(mrow == 3, nxt, 0.0))))
            meta_ref[...] = meta.astype(jnp.int32)

        lane = ROUTER_LANES
        before = (lax.broadcasted_iota(jnp.int32, (lane, lane), 0)
                  < lax.broadcasted_iota(jnp.int32, (lane, lane), 1))
        tri = jnp.where(before, 1.0, 0.0).astype(BF16)
        run = base_ref[...]
        d1, d2 = [], []
        for k in range(tm // lane):
            piece = slice(k * lane, (k + 1) * lane)
            ck = cnt[:, piece]
            pos = run + jnp.dot(ck.astype(BF16), tri, preferred_element_type=F32)
            d1.append(jnp.sum(jnp.where(sel1[:, piece], pos, 0.0), axis=0, keepdims=True))
            d2.append(jnp.sum(jnp.where(sel2[:, piece], pos, 0.0), axis=0, keepdims=True))
            run = run + jnp.sum(ck, axis=1, keepdims=True)
        dest_ref[...] = jnp.concatenate(
            [jnp.concatenate(d1, axis=1), jnp.concatenate(d2, axis=1)], axis=0).astype(jnp.int32)
        base_ref[...] = run
        g_w = 1.0 / jnp.sum(jnp.exp(gl - gmax), axis=0, keepdims=True)
        e21 = jnp.exp(m2 - m1)
        p1 = 1.0 / (1.0 + e21)
        rid = lax.broadcasted_iota(jnp.int32, (ROUTER_LANES, tm), 0)
        gt = jnp.where(rid == 0, g_w * p1, jnp.where(rid == 1, g_w * (e21 * p1), 0.0))
        gates_ref[...] = gt.T


def _route_call(logits):
    return pl.pallas_call(
        _route_kernel,
        grid=(2, N_ROUTE_TILES),
        in_specs=[pl.BlockSpec((ROUTE_TM, ROUTER_LANES), lambda ph, t: (t, 0))],
        out_specs=[pl.BlockSpec((MOE_TOPK, ROUTE_TM), lambda ph, t: (0, t * ph)),
                   pl.BlockSpec((ROUTE_TM, ROUTER_LANES), lambda ph, t: (t * ph, 0)),
                   pl.BlockSpec((8, META_LANES), lambda ph, t: (0, 0))],
        out_shape=[jax.ShapeDtypeStruct((MOE_TOPK, T_ALL), jnp.int32),
                   jax.ShapeDtypeStruct((T_ALL, ROUTER_LANES), F32),
                   jax.ShapeDtypeStruct((8, META_LANES), jnp.int32)],
        scratch_shapes=[pltpu.VMEM((MOE_EXPERTS, 1), F32), pltpu.VMEM((MOE_EXPERTS, 1), F32)],
        compiler_params=_params("arbitrary", "arbitrary"),
        name="route",
    )(logits)


def _sc_mesh():
    return plsc.VectorSubcoreMesh(core_axis_name="c", subcore_axis_name="s")


def _sc_token_offset(j):
    wid = lax.axis_index("s") * SC_CORES + lax.axis_index("c")
    return pl.multiple_of(wid * SC_ROWS_PER_WORKER + j * SC_CHUNK, 8)


def _dispatch_body(h_hbm, d0_hbm, d1_hbm, out_hbm, i0_v, i1_v, rows_v, sem0, sem1, sem2):
    @pl.loop(0, SC_ROWS_PER_WORKER // SC_CHUNK)
    def _(j):
        off = _sc_token_offset(j)
        c0 = pltpu.async_copy(d0_hbm.at[pl.ds(off, SC_CHUNK)], i0_v, sem0)
        c1 = pltpu.async_copy(d1_hbm.at[pl.ds(off, SC_CHUNK)], i1_v, sem1)
        c2 = pltpu.async_copy(h_hbm.at[pl.ds(off, SC_CHUNK)], rows_v, sem2)
        c0.wait()
        c1.wait()
        c2.wait()
        s0 = pltpu.async_copy(rows_v, out_hbm.at[i0_v], sem0)
        s1 = pltpu.async_copy(rows_v, out_hbm.at[i1_v], sem1)
        s0.wait()
        s1.wait()


def _dispatch_call(h2, dest0, dest1):
    return pl.kernel(
        _dispatch_body,
        out_type=jax.ShapeDtypeStruct((P_ROWS, DP), U32),
        mesh=_sc_mesh(),
        scratch_types=[pltpu.VMEM((SC_CHUNK,), jnp.int32), pltpu.VMEM((SC_CHUNK,), jnp.int32),
                       pltpu.VMEM((SC_CHUNK, DP), U32),
                       pltpu.SemaphoreType.DMA, pltpu.SemaphoreType.DMA, pltpu.SemaphoreType.DMA],
        name="moe_dispatch",
    )(h2, dest0, dest1)


def _combine_body(yb_hbm, d0_hbm, d1_hbm, o0_hbm, o1_hbm, i0_v, i1_v, r0_v, r1_v, sem0, sem1):
    @pl.loop(0, SC_ROWS_PER_WORKER // SC_CHUNK)
    def _(j):
        off = _sc_token_offset(j)
        c0 = pltpu.async_copy(d0_hbm.at[pl.ds(off, SC_CHUNK)], i0_v, sem0)
        c1 = pltpu.async_copy(d1_hbm.at[pl.ds(off, SC_CHUNK)], i1_v, sem1)
        c0.wait()
        c1.wait()
        g0 = pltpu.async_copy(yb_hbm.at[i0_v], r0_v, sem0)
        g1 = pltpu.async_copy(yb_hbm.at[i1_v], r1_v, sem1)
        g0.wait()
        g1.wait()
        w0 = pltpu.async_copy(r0_v, o0_hbm.at[pl.ds(off, SC_CHUNK)], sem0)
        w1 = pltpu.async_copy(r1_v, o1_hbm.at[pl.ds(off, SC_CHUNK)], sem1)
        w0.wait()
        w1.wait()


def _combine_call(yb, dest0, dest1):
    out = jax.ShapeDtypeStruct((T_ALL, DP), U32)
    return pl.kernel(
        _combine_body,
        out_type=(out, out),
        mesh=_sc_mesh(),
        scratch_types=[pltpu.VMEM((SC_CHUNK,), jnp.int32), pltpu.VMEM((SC_CHUNK,), jnp.int32),
                       pltpu.VMEM((SC_CHUNK, DP), U32), pltpu.VMEM((SC_CHUNK, DP), U32),
                       pltpu.SemaphoreType.DMA, pltpu.SemaphoreType.DMA],
        name="moe_combine",
    )(yb, dest0, dest1)


def _moe_rows(h2, dest, meta, w1, w3, w2, layer):
    dest0, dest1 = dest[0], dest[1]
    xb = _dispatch_call(h2, dest0, dest1)
    yb = _expert_call(meta[0, :N_EXP_BLOCKS], meta[1, :N_EXP_BLOCKS], meta[2, :1], meta[3, :N_EXP_BLOCKS],
                      xb, w1, w3, w2, layer)
    return _combine_call(yb, dest0, dest1)


def _final_kernel(x_ref, yg0_ref, yg1_ref, gates_ref, modp_ref, gain_ref, o_ref):
    x = _add_moe(x_ref[...], yg0_ref, yg1_ref, gates_ref, modp_ref)
    o_ref[...] = _rms(x) * gain_ref[...]


def _final_call(x, prev, gain, row0, n_rows):
    rows = ROWS_WIDE
    tile0 = row0 // rows

    def tile(width):
        return pl.BlockSpec((rows, width), lambda i: (tile0 + i, 0))

    return pl.pallas_call(
        _final_kernel,
        grid=(n_rows // rows,),
        in_specs=[tile(D), tile(DP), tile(DP), tile(ROUTER_LANES),
                  pl.BlockSpec((rows // GROUP, 6, D), lambda i: (tile0 + i, 0, 0)), _resident((1, D))],
        out_specs=pl.BlockSpec((rows, D), lambda i: (i, 0)),
        out_shape=jax.ShapeDtypeStruct((n_rows, D), F32),
        compiler_params=_params("parallel"),
        name="final_norm",
    )(x, *prev, gain)


def _rope_tables():
    pos = np.concatenate([np.tile(np.arange(SEQ), BATCH),
                          np.tile(PAST_LEN + np.arange(DEC_SEQ), DEC_BATCH)]).astype(np.float32)
    inv = (ROPE_BASE ** (-np.arange(ROPE_HALF, dtype=np.float32) / ROPE_HALF)).astype(np.float32)
    ang = (pos[:, None] * inv[None, :]).astype(np.float32).astype(np.float64)
    return jnp.asarray(np.cos(ang), F32), jnp.asarray(np.sin(ang), F32)


def kernel(x_prompt, x_sample, c_prompt, c_sample, state_ret, ada_w, ada_b, norm1_g, norm2_g, ret_w_in,
           ret_w_out, gm_w_in, gm_b_in, gm_ln_g, gm_ln_b, gm_w_s, gm_b_s, gm_w_out, gm_b_out, moe_w_rg,
           moe_b_rg, moe_w_re, moe_b_re, moe_w1, moe_w3, moe_w2, final_g):
    x = (x_prompt.reshape(T_PROMPT, D), x_sample.reshape(T_SAMPLE, D))
    c_all = jnp.concatenate([c_prompt, c_sample], axis=0)
    cos, sin = _rope_tables()
    dec = _ret_row_scales()

    def layer_params(i, dep):
        j = i // 2
        mixer = (ret_w_in, ret_w_out) if i % 2 == 0 else (gm_w_in, gm_w_out, gm_w_s, gm_b_s)
        tied = (c_all, ada_w, ada_b, moe_w_re, moe_w_rg, moe_b_re, moe_b_rg) + mixer
        if dep is not None:
            tied = lax.optimization_barrier((dep,) + tied)[1:]
        c_t, ada_w_t, ada_b_t, w_re, w_rg, b_re, b_rg = tied[:7]
        mod_seq = _ada_call(c_t, ada_w_t, ada_b_t, i).reshape(N_SEQ, 6, D)
        mod = jnp.concatenate([
            jnp.broadcast_to(mod_seq[:BATCH, None], (BATCH, SEQ // GROUP, 6, D)).reshape(-1, 6, D),
            jnp.broadcast_to(mod_seq[BATCH:, None], (DEC_BATCH, DEC_SEQ // GROUP, 6, D)).reshape(-1, 6, D),
        ], axis=0)
        w_r = jnp.pad(jnp.concatenate([w_re[i], w_rg[i]], axis=1),
                      ((0, 0), (0, ROUTER_LANES - MOE_GROUPS - MOE_EXPERTS)))
        w_r_hi = w_r.astype(BF16)
        w_r_lo = (w_r - w_r_hi.astype(F32)).astype(BF16)
        w_r = jnp.concatenate([w_r_hi, w_r_lo], axis=1)
        b_r = jnp.pad(jnp.concatenate([b_re[i].reshape(-1), b_rg[i]]),
                      (0, ROUTER_LANES - MOE_GROUPS - MOE_EXPERTS)).reshape(1, ROUTER_LANES)
        weights = [w[j:j + 1].astype(BF16) for w in tied[7:9]]
        if i % 2 == 1:
            weights += list(_gm_block_diag(tied[9][j], tied[10][j]))
        return mod, w_r, b_r, weights

    ret_prompt = ret_sample = gm_sample = None
    prev = None
    params = layer_params(0, None)
    for i in range(DEPTH):
        j = i // 2
        mod, w_r, b_r, weights = params
        g1 = norm1_g[i].reshape(1, D)
        g2 = norm2_g[i].reshape(1, D)
        if i % 2 == 0:
            w_in_b, w_out_b = weights
            x, p = _ret_proj_call(x, prev, mod, g1, w_in_b, cos, sin, dec)
            y_p, ret_prompt = _ret_core_call(p, None, ret_prompt, j, BATCH, SEQ, RET_CHUNK_PROMPT, 0)
            y_s, ret_sample = _ret_core_call(p, state_ret, ret_sample, j, DEC_BATCH, DEC_SEQ,
                                             RET_CHUNK_SAMPLE, T_PROMPT)
            x, h2, logits = _ret_out_call(y_p, y_s, w_out_b, x, mod, g2, w_r, b_r)
        else:
            w_in_b, w_out_b, ws_bd, bs_bd = weights
            x, u, v, gm_sample = _gm_proj_call(x, prev, mod, g1, w_in_b, j,
                                               gm_b_in[j].reshape(1, GM_FFN), gm_ln_g[j].reshape(1, GM_HALF),
                                               gm_ln_b[j].reshape(1, GM_HALF), gm_sample)
            x, h2, logits = _gm_out_call(u, v, ws_bd, bs_bd, w_out_b,
                                         gm_b_out[j].reshape(1, D), x, mod, g2, w_r, b_r)
        dest, gates, meta = _route_call(logits)
        if i + 1 < DEPTH:
            params = layer_params(i + 1, dest)
        yg0, yg1 = _moe_rows(h2, dest, meta, moe_w1, moe_w3, moe_w2, i)
        prev = (yg0, yg1, gates, mod)

    fg = final_g.reshape(1, D)
    y_prompt = _final_call(x, prev, fg, 0, T_PROMPT).reshape(BATCH, SEQ, D)
    y_sample = _final_call(x, prev, fg, T_PROMPT, T_SAMPLE).reshape(DEC_BATCH, DEC_SEQ, D)
    return (y_prompt, y_sample, ret_prompt, ret_sample,
            gm_sample.reshape(N_GM, DEC_BATCH, DEC_SEQ, GM_HALF))
```

```python
import functools

import numpy as np
import jax
import jax.numpy as jnp
from jax import lax
from jax.experimental import pallas as pl
from jax.experimental.pallas import tpu as pltpu
from jax.experimental.pallas import tpu_sc as plsc

F32 = jnp.float32
BF16 = jnp.bfloat16
U32 = jnp.uint32

D = 1024
BATCH, SEQ = 4, 4096
DEC_BATCH, DEC_SEQ = 16, 64
PAST_LEN = 4096
DEPTH = 4
N_RET = (DEPTH + 1) // 2
N_GM = DEPTH // 2
N_SEQ = BATCH + DEC_BATCH

RET_HEADS, RET_DK, RET_DV = 4, 256, 512
RET_QK = RET_HEADS * RET_DK
RET_V = RET_HEADS * RET_DV
RET_IN = 2 * RET_QK + 2 * RET_V
ROPE_BASE = 10000.0
ROPE_HALF = RET_DK // 2

GM_FFN = 6 * D
GM_HALF = GM_FFN // 2
GM_GROUPS = 4
GM_GDIM = GM_HALF // GM_GROUPS
GM_CHUNK = 128

MOE_GROUPS, MOE_PER_GROUP = 4, 8
MOE_EXPERTS = MOE_GROUPS * MOE_PER_GROUP
MOE_TOPK = 2
MOE_HIDDEN = 512
EPS = 1e-6

GROUP = 64
T_PROMPT = BATCH * SEQ
T_SAMPLE = DEC_BATCH * DEC_SEQ
T_ALL = T_PROMPT + T_SAMPLE
N_GROUPS = T_ALL // GROUP
ROWS_WIDE = 512
ROWS_GM_PROJ = 256

RET_CHUNK_PROMPT = 256
RET_CHUNK_SAMPLE = DEC_SEQ

GM_MIX = 256

EXP_BLOCK = 512
N_ASSIGN = T_ALL * MOE_TOPK
N_EXP_BLOCKS = -(-(N_ASSIGN + MOE_EXPERTS * (EXP_BLOCK - 1)) // EXP_BLOCK)
P_ROWS = N_EXP_BLOCKS * EXP_BLOCK
ROUTER_LANES = 128
ROUTE_TM = 1024
N_ROUTE_TILES = T_ALL // ROUTE_TM
META_LANES = 256
assert META_LANES >= N_EXP_BLOCKS

DP = D // 2
SC_CORES, SC_SUBCORES = 2, 16
SC_WORKERS = SC_CORES * SC_SUBCORES
SC_ROWS_PER_WORKER = T_ALL // SC_WORKERS
SC_CHUNK = 32
assert SC_ROWS_PER_WORKER % SC_CHUNK == 0 and SC_CHUNK % 8 == 0

V7X_VMEM_LIMIT_BYTES = 56 * 1024 * 1024


def _params(*sem):
    return pltpu.CompilerParams(dimension_semantics=sem, vmem_limit_bytes=V7X_VMEM_LIMIT_BYTES)


def _resident(shape):
    nd = len(shape)
    return pl.BlockSpec(shape, lambda *_: (0,) * nd, pipeline_mode=pl.Buffered(1))


def _rms(x):
    return x * lax.rsqrt(jnp.mean(x * x, axis=-1, keepdims=True) + EPS)


def _silu(x):
    return x * jax.nn.sigmoid(x)


def _per_group(x2d, fn):
    rows = x2d.shape[0]
    return fn(x2d.reshape(rows // GROUP, GROUP, D)).reshape(rows, D)


def _norm_mod(x, gain_ref, mod_ref, shift_idx):
    y = _rms(x) * gain_ref[...]
    scale = mod_ref[:, shift_idx + 1:shift_idx + 2, :]
    shift = mod_ref[:, shift_idx:shift_idx + 1, :]
    return _per_group(y, lambda y3: y3 * (1.0 + scale) + shift)


def _pack_bf16_pairs(x):
    lo = lax.bitcast_convert_type(x[:, :DP].astype(BF16).astype(F32), U32)
    hi = lax.bitcast_convert_type(x[:, DP:].astype(BF16).astype(F32), U32)
    return (lo >> 16) | (hi & U32(0xFFFF0000))


def _unpack_bf16_pairs(w):
    lo = lax.bitcast_convert_type(w << 16, F32)
    hi = lax.bitcast_convert_type(w & U32(0xFFFF0000), F32)
    return jnp.concatenate([lo, hi], axis=1)


def _add_moe(x, yg0_ref, yg1_ref, gates_ref, modp_ref):
    g = gates_ref[...]
    y = g[:, 0:1] * _unpack_bf16_pairs(yg0_ref[...]) + g[:, 1:2] * _unpack_bf16_pairs(yg1_ref[...])
    gate2 = modp_ref[:, 5:6, :]
    return x + _per_group(y, lambda y3: y3 * gate2)


ADA_TN = 1536


def _ada_kernel(c_ref, w_ref, b_ref, o_ref):
    c = c_ref[...]
    s = _silu(c).astype(BF16)
    o_ref[0] = jnp.dot(s, w_ref[0].astype(BF16), preferred_element_type=F32) + b_ref[0]


def _ada_call(c_all, ada_w, ada_b, layer):
    return pl.pallas_call(
        _ada_kernel,
        grid=(6 * D // ADA_TN,),
        in_specs=[
            pl.BlockSpec((N_SEQ, D), lambda j: (0, 0)),
            pl.BlockSpec((1, D, ADA_TN), lambda j: (layer, 0, j)),
            pl.BlockSpec((1, 1, ADA_TN), lambda j: (layer, 0, j)),
        ],
        out_specs=pl.BlockSpec((1, N_SEQ, ADA_TN), lambda j: (0, 0, j)),
        out_shape=jax.ShapeDtypeStruct((1, N_SEQ, 6 * D), F32),
        compiler_params=_params("parallel"),
        name="ada_modulation",
    )(c_all, ada_w, ada_b.reshape(DEPTH, 1, 6 * D))[0]


def _n_tiles(rows):
    return T_ALL // rows


def _n_prompt_tiles(rows):
    return T_PROMPT // rows


def _row_spec(rows, width):
    return pl.BlockSpec((rows, width), lambda i: (i, 0))


def _mod_spec(rows):
    return pl.BlockSpec((rows // GROUP, 6, D), lambda i: (i, 0, 0))


def _prev_specs(rows):
    return [_row_spec(rows, DP), _row_spec(rows, DP), _row_spec(rows, ROUTER_LANES), _mod_spec(rows)]


def _prompt_rows_spec(rows, width):
    last = _n_prompt_tiles(rows) - 1
    return pl.BlockSpec((rows, width), lambda i: (jnp.minimum(i, last), 0))


def _sample_rows_spec(rows, width):
    npt = _n_prompt_tiles(rows)
    return pl.BlockSpec((rows, width), lambda i: (jnp.maximum(i - npt, 0), 0))


def _ret_proj_kernel(has_prev, n_prompt_tiles, *refs):
    if has_prev:
        (x_ref, yg0_ref, yg1_ref, gates_ref, modp_ref, mod_ref, gain_ref, w_ref, cos_ref, sin_ref, dec_ref,
         xo_ref, p_ref) = refs
        x = _add_moe(x_ref[...], yg0_ref, yg1_ref, gates_ref, modp_ref)
    else:
        xp_ref, xs_ref, mod_ref, gain_ref, w_ref, cos_ref, sin_ref, dec_ref, xo_ref, p_ref = refs
        x = jnp.where(pl.program_id(0) < n_prompt_tiles, xp_ref[...], xs_ref[...])
    xo_ref[...] = x
    hb = _norm_mod(x, gain_ref, mod_ref, 0).astype(BF16)
    cos = cos_ref[...]
    sin = sin_ref[...]
    dec = dec_ref[...]
    for j in range(2 * RET_HEADS):
        lo = j * RET_DK
        acc = jnp.dot(hb, w_ref[0, :, lo:lo + RET_DK], preferred_element_type=F32)
        x1 = acc[:, :ROPE_HALF]
        x2 = acc[:, ROPE_HALF:]
        scale = dec[:, j:j + 1]
        p_ref[:, lo:lo + ROPE_HALF] = ((x1 * cos - x2 * sin) * scale).astype(BF16)
        p_ref[:, lo + ROPE_HALF:lo + RET_DK] = ((x1 * sin + x2 * cos) * scale).astype(BF16)
    for j in range(2 * RET_HEADS):
        lo = 2 * RET_QK + j * RET_DV
        acc = jnp.dot(hb, w_ref[0, :, lo:lo + RET_DV], preferred_element_type=F32)
        if j >= RET_HEADS:
            acc = _silu(acc)
        p_ref[:, lo:lo + RET_DV] = acc.astype(BF16)


def _ret_proj_call(x, prev, mod, gain, w_in, cos, sin, dec):
    rows = ROWS_WIDE
    has_prev = prev is not None
    if has_prev:
        in_specs = [_row_spec(rows, D)] + _prev_specs(rows)
        args = [x] + list(prev)
    else:
        in_specs = [_prompt_rows_spec(rows, D), _sample_rows_spec(rows, D)]
        args = list(x)
    in_specs += [_mod_spec(rows), _resident((1, D)), _resident((1, D, RET_IN)),
                 _row_spec(rows, ROPE_HALF), _row_spec(rows, ROPE_HALF), _row_spec(rows, 2 * RET_HEADS)]
    args += [mod, gain, w_in, cos, sin, dec]
    return pl.pallas_call(
        functools.partial(_ret_proj_kernel, has_prev, _n_prompt_tiles(rows)),
        grid=(_n_tiles(rows),),
        in_specs=in_specs,
        out_specs=[_row_spec(rows, D), _row_spec(rows, RET_IN)],
        out_shape=[jax.ShapeDtypeStruct((T_ALL, D), F32), jax.ShapeDtypeStruct((T_ALL, RET_IN), BF16)],
        compiler_params=_params("parallel"),
        name="ret_proj",
    )(*args)


def _ret_core_kernel(has_s0, n_chunks, layer, *refs):
    refs = list(refs)
    p_ref = refs.pop(0)
    s0_ref = refs.pop(0) if has_s0 else None
    causal_ref, cd_ref = refs[:2]
    y_ref, so_ref, s_ref = refs[-3:]
    c = pl.program_id(1)

    @pl.when(c == 0)
    def _():
        if has_s0:
            s_ref[...] = s0_ref[0, 0]
        else:
            s_ref[...] = jnp.zeros_like(s_ref)

    for h in range(RET_HEADS):
        qb = p_ref[:, h * RET_DK:(h + 1) * RET_DK]
        kb = p_ref[:, RET_QK + h * RET_DK:RET_QK + (h + 1) * RET_DK]
        vb = p_ref[:, 2 * RET_QK + h * RET_DV:2 * RET_QK + (h + 1) * RET_DV]
        gb = p_ref[:, 2 * RET_QK + RET_V + h * RET_DV:2 * RET_QK + RET_V + (h + 1) * RET_DV]
        scores = lax.dot_general(qb, kb, (((1,), (1,)), ((), ())), preferred_element_type=F32)
        scores = scores * causal_ref[...]
        s_old = s_ref[h]
        o = (jnp.dot(scores.astype(BF16), vb, preferred_element_type=F32)
             + jnp.dot(qb, s_old.astype(BF16), preferred_element_type=F32))
        s_ref[h] = cd_ref[h][:, 0:1] * (s_old + lax.dot_general(
            kb, vb, (((0,), (0,)), ((), ())), preferred_element_type=F32))
        y_ref[:, h * RET_DV:(h + 1) * RET_DV] = (gb.astype(F32) * _rms(o)).astype(BF16)

    @pl.when(c == n_chunks - 1)
    def _():
        so_ref[0, 0] = s_ref[...]
        if layer == 0:
            for later in range(1, N_RET):
                so_ref[later, 0] = jnp.zeros_like(s_ref)


def _ret_log_gamma():
    return np.log1p(-np.exp2(-5.0 - np.arange(RET_HEADS, dtype=np.float64)))


def _ret_chunk_tables(cl):
    idx = np.arange(cl)
    causal = (idx[:, None] >= idx[None, :]).astype(np.float32)
    cd = np.broadcast_to(np.exp(_ret_log_gamma() * cl)[:, None, None], (RET_HEADS, 1, 128))
    return jnp.asarray(causal, F32), jnp.asarray(cd, F32)


def _ret_row_scales():
    c = np.concatenate([np.arange(T_PROMPT) % RET_CHUNK_PROMPT,
                        np.arange(T_SAMPLE) % RET_CHUNK_SAMPLE]).astype(np.float64)
    e = (c[:, None] + 1.0) * _ret_log_gamma()[None, :]
    return jnp.asarray(np.concatenate([np.exp(e), np.exp(-e) * RET_DK ** -0.5], axis=1), F32)


def _ret_core_call(p, s0, states, layer, n_seq, seq_len, cl, row0):
    has_s0 = s0 is not None
    n_chunks = seq_len // cl
    rb0 = row0 // cl
    state = (RET_HEADS, RET_DK, RET_DV)
    in_specs = [pl.BlockSpec((cl, RET_IN), lambda b, c: (rb0 + b * n_chunks + c, 0))]
    args = [p]
    if has_s0:
        in_specs.append(pl.BlockSpec((1, 1) + state, lambda b, c: (layer, b, 0, 0, 0)))
        args.append(s0)
    in_specs += [_resident((cl, cl)), _resident((RET_HEADS, 1, 128))]
    args += list(_ret_chunk_tables(cl))
    if layer == 0:
        assert states is None
        state_spec = pl.BlockSpec((N_RET, 1) + state, lambda b, c: (0, b, 0, 0, 0))
        aliases = {}
    else:
        in_specs.append(pl.BlockSpec(memory_space=pl.ANY))
        args.append(states)
        state_spec = pl.BlockSpec((1, 1) + state, lambda b, c: (layer, b, 0, 0, 0))
        aliases = {len(args) - 1: 1}
    return pl.pallas_call(
        functools.partial(_ret_core_kernel, has_s0, n_chunks, layer),
        grid=(n_seq, n_chunks),
        in_specs=in_specs,
        out_specs=[pl.BlockSpec((cl, RET_V), lambda b, c: (b * n_chunks + c, 0)), state_spec],
        out_shape=[jax.ShapeDtypeStruct((n_seq * seq_len, RET_V), BF16),
                   jax.ShapeDtypeStruct((N_RET, n_seq) + state, F32)],
        scratch_shapes=[pltpu.VMEM(state, F32)],
        input_output_aliases=aliases,
        compiler_params=_params("parallel", "arbitrary"),
        name="ret_core",
    )(*args)


def _residual_router(acc, x_ref, mod_ref, gain_ref, wr_ref, br_ref, xo_ref, h2_ref, lg_ref):
    gate1 = mod_ref[:, 2:3, :]
    xn = x_ref[...] + _per_group(acc, lambda a3: a3 * gate1)
    xo_ref[...] = xn
    h2 = _norm_mod(xn, gain_ref, mod_ref, 3)
    h2_ref[...] = _pack_bf16_pairs(h2)
    hh = jnp.dot(h2.astype(BF16), wr_ref[...], preferred_element_type=F32)
    lg_ref[...] = hh[:, :ROUTER_LANES] + hh[:, ROUTER_LANES:] + br_ref[...]


def _mix_out_specs(rows):
    return [_row_spec(rows, D), _row_spec(rows, DP), _row_spec(rows, ROUTER_LANES)]


_MIX_OUT_SHAPE = [
    jax.ShapeDtypeStruct((T_ALL, D), F32),
    jax.ShapeDtypeStruct((T_ALL, DP), U32),
    jax.ShapeDtypeStruct((T_ALL, ROUTER_LANES), F32),
]


def _router_specs():
    return [_resident((D, 2 * ROUTER_LANES)), _resident((1, ROUTER_LANES))]


def _ret_out_kernel(n_prompt_tiles, yp_ref, ys_ref, w_ref, x_ref, mod_ref, gain_ref, wr_ref, br_ref,
                    xo_ref, h2_ref, lg_ref):
    yin = jnp.where(pl.program_id(0) < n_prompt_tiles, yp_ref[...], ys_ref[...])
    acc = jnp.dot(yin, w_ref[0], preferred_element_type=F32)
    _residual_router(acc, x_ref, mod_ref, gain_ref, wr_ref, br_ref, xo_ref, h2_ref, lg_ref)


def _ret_out_call(y_prompt, y_sample, w_out, x, mod, gain, w_r, b_r):
    rows = ROWS_WIDE
    return pl.pallas_call(
        functools.partial(_ret_out_kernel, _n_prompt_tiles(rows)),
        grid=(_n_tiles(rows),),
        in_specs=[_prompt_rows_spec(rows, RET_V), _sample_rows_spec(rows, RET_V),
                  _resident((1, RET_V, D)), _row_spec(rows, D), _mod_spec(rows),
                  _resident((1, D))] + _router_specs(),
        out_specs=_mix_out_specs(rows),
        out_shape=_MIX_OUT_SHAPE,
        compiler_params=_params("parallel"),
        name="ret_out",
    )(y_prompt, y_sample, w_out, x, mod, gain, w_r, b_r)


GM_TN = 512


_GELU_C = float(np.sqrt(2.0 / np.pi))


def _gelu_tanh(x):
    hx = 0.5 * x
    return hx * jnp.tanh(x * (_GELU_C + (_GELU_C * 0.044715) * (x * x))) + hx


def _gm_proj_kernel(n_prompt_tiles, layer, *refs):
    x_ref, yg0_ref, yg1_ref, gates_ref, modp_ref, mod_ref, gain_ref, w_ref, b_ref, lg_ref, lb_ref = refs[:11]
    xo_ref, u_ref, v_ref, vs_ref, vraw_ref = refs[-5:]
    x = _add_moe(x_ref[...], yg0_ref, yg1_ref, gates_ref, modp_ref)
    xo_ref[...] = x
    hb = _norm_mod(x, gain_ref, mod_ref, 0).astype(BF16)
    rows = hb.shape[0]
    is_sample = pl.program_id(0) >= n_prompt_tiles

    def gelu_chunk(lo):
        z = jnp.dot(hb, w_ref[0, :, lo:lo + GM_TN], preferred_element_type=F32) + b_ref[:, lo:lo + GM_TN]
        return _gelu_tanh(z.astype(BF16))

    s1 = jnp.zeros((rows, 128), F32)
    s2 = jnp.zeros((rows, 128), F32)
    for lo in range(0, GM_HALF, GM_TN):
        gz = gelu_chunk(GM_HALF + lo)
        vraw_ref[:, lo:lo + GM_TN] = gz
        gf = gz.astype(F32)
        for k in range(0, GM_TN, 128):
            piece = gf[:, k:k + 128]
            s1 = s1 + piece
            s2 = s2 + piece * piece
    mu = jnp.sum(s1, axis=-1, keepdims=True) * (1.0 / GM_HALF)
    var = jnp.sum(s2, axis=-1, keepdims=True) * (1.0 / GM_HALF) - mu * mu
    rstd = lax.rsqrt(var + EPS)
    shift = -mu * rstd

    for lo in range(0, GM_HALF, GM_TN):
        u_ref[:, lo:lo + GM_TN] = gelu_chunk(lo)
        vn = ((vraw_ref[:, lo:lo + GM_TN].astype(F32) * rstd + shift) * lg_ref[:, lo:lo + GM_TN]
              + lb_ref[:, lo:lo + GM_TN])
        v_ref[:, lo:lo + GM_TN] = vn.astype(BF16)
        vs_ref[0, :, lo:lo + GM_TN] = vn

    if layer == 0:
        @pl.when(is_sample)
        def _():
            for later in range(1, N_GM):
                vs_ref[later] = jnp.zeros((rows, GM_HALF), F32)


def _gm_proj_call(x, prev, mod, gain, w_in, layer, b_in, ln_g, ln_b, vs_all):
    rows = ROWS_GM_PROJ
    npt = _n_prompt_tiles(rows)
    in_specs = [_row_spec(rows, D)] + _prev_specs(rows) + [
        _mod_spec(rows), _resident((1, D)), _resident((1, D, GM_FFN)), _resident((1, GM_FFN)),
        _resident((1, GM_HALF)), _resident((1, GM_HALF))]
    args = [x, *prev, mod, gain, w_in, b_in, ln_g, ln_b]
    if layer == 0:
        assert vs_all is None
        vs_spec = pl.BlockSpec((N_GM, rows, GM_HALF), lambda i: (0, jnp.maximum(i - npt, 0), 0))
        aliases = {}
    else:
        in_specs.append(pl.BlockSpec(memory_space=pl.ANY))
        args.append(vs_all)
        vs_spec = pl.BlockSpec((1, rows, GM_HALF), lambda i: (layer, jnp.maximum(i - npt, 0), 0))
        aliases = {len(args) - 1: 3}
    return pl.pallas_call(
        functools.partial(_gm_proj_kernel, npt, layer),
        grid=(_n_tiles(rows),),
        in_specs=in_specs,
        out_specs=[_row_spec(rows, D), _row_spec(rows, GM_HALF), _row_spec(rows, GM_HALF), vs_spec],
        out_shape=[jax.ShapeDtypeStruct((T_ALL, D), F32),
                   jax.ShapeDtypeStruct((T_ALL, GM_HALF), BF16),
                   jax.ShapeDtypeStruct((T_ALL, GM_HALF), BF16),
                   jax.ShapeDtypeStruct((N_GM, T_SAMPLE, GM_HALF), F32)],
        scratch_shapes=[pltpu.VMEM((rows, GM_HALF), BF16)],
        input_output_aliases=aliases,
        compiler_params=_params("arbitrary"),
        name="gm_proj",
    )(*args)


def _gm_out_kernel(u_ref, v_ref, ws_ref, bs_ref, w_ref, bo_ref, x_ref, mod_ref, gain_ref, wr_ref, br_ref,
                   xo_ref, h2_ref, lg_ref):
    rows = u_ref.shape[0]
    pieces = []
    for r0 in range(0, rows, GM_MIX):
        acc = jnp.zeros((GM_MIX, D), F32)
        for g in range(GM_GROUPS):
            lo = g * GM_GDIM
            sp = jnp.dot(ws_ref[0, g], v_ref[r0:r0 + GM_MIX, lo:lo + GM_GDIM],
                         preferred_element_type=F32) + bs_ref[0, g]
            gated = (u_ref[r0:r0 + GM_MIX, lo:lo + GM_GDIM].astype(F32) * sp).astype(BF16)
            acc = acc + jnp.dot(gated, w_ref[0, lo:lo + GM_GDIM, :], preferred_element_type=F32)
        pieces.append(acc)
    acc = jnp.concatenate(pieces, axis=0) + bo_ref[...]
    _residual_router(acc, x_ref, mod_ref, gain_ref, wr_ref, br_ref, xo_ref, h2_ref, lg_ref)


def _gm_block_diag(w_s, b_s):
    mats, biases = [], []
    for cl in (GM_CHUNK, DEC_SEQ):
        tri = jnp.tril(jnp.ones((cl, cl), bool))
        blk = jnp.where(tri[None], w_s[:, :cl, :cl], 0.0)
        reps = GM_MIX // cl
        eye = jnp.eye(reps, dtype=w_s.dtype)
        bd = jnp.einsum("ab,gts->gatbs", eye, blk).reshape(GM_GROUPS, GM_MIX, GM_MIX)
        mats.append(bd)
        biases.append(jnp.tile(b_s[:, :cl], (1, reps))[:, :, None])
    return jnp.stack(mats).astype(BF16), jnp.stack(biases).astype(F32)


def _gm_out_call(u, v, ws_bd, bs_bd, w_out, b_out, x, mod, gain, w_r, b_r):
    rows = ROWS_WIDE
    npt = _n_prompt_tiles(rows)

    def variant(i):
        return jnp.where(i >= npt, 1, 0)

    return pl.pallas_call(
        _gm_out_kernel,
        grid=(_n_tiles(rows),),
        in_specs=[_row_spec(rows, GM_HALF), _row_spec(rows, GM_HALF),
                  pl.BlockSpec((1, GM_GROUPS, GM_MIX, GM_MIX), lambda i: (variant(i), 0, 0, 0)),
                  pl.BlockSpec((1, GM_GROUPS, GM_MIX, 1), lambda i: (variant(i), 0, 0, 0)),
                  _resident((1, GM_HALF, D)), _resident((1, D)), _row_spec(rows, D),
                  _mod_spec(rows), _resident((1, D))] + _router_specs(),
        out_specs=_mix_out_specs(rows),
        out_shape=_MIX_OUT_SHAPE,
        compiler_params=_params("arbitrary"),
        name="gm_out",
    )(u, v, ws_bd, bs_bd, w_out, b_out, x, mod, gain, w_r, b_r)


def _expert_kernel(layer, be_ref, bv_ref, nb_ref, nx_ref, xb_ref, w1_hbm, w3_hbm, w2_hbm, yb_ref,
                   st1, st3, st2, w1s, w3s, w2s, sems, slot_ref):
    b = pl.program_id(0)

    def weight_copies(e, slot):
        return (pltpu.make_async_copy(w1_hbm.at[layer, e], st1.at[slot], sems.at[slot, 0]),
                pltpu.make_async_copy(w3_hbm.at[layer, e], st3.at[slot], sems.at[slot, 1]),
                pltpu.make_async_copy(w2_hbm.at[layer, e], st2.at[slot], sems.at[slot, 2]))

    @pl.when(b == 0)
    def _():
        slot_ref[0] = 0
        for cp in weight_copies(be_ref[0], 0):
            cp.start()

    @pl.when(b < nb_ref[0])
    def _():
        prev_e = be_ref[jnp.maximum(b - 1, 0)]

        @pl.when((b == 0) | (be_ref[b] != prev_e))
        def _():
            slot = slot_ref[0]
            for cp in weight_copies(be_ref[b], slot):
                cp.wait()
            w1s[...] = st1[slot].astype(BF16)
            w3s[...] = st3[slot].astype(BF16)
            w2s[...] = st2[slot].astype(BF16)

            @pl.when(nx_ref[b] >= 0)
            def _():
                for cp in weight_copies(nx_ref[b], 1 - slot):
                    cp.start()

            slot_ref[0] = 1 - slot

        row = lax.broadcasted_iota(jnp.int32, (EXP_BLOCK, 1), 0)
        xw = jnp.where(row < bv_ref[b], xb_ref[...], U32(0))
        x = _unpack_bf16_pairs(xw).astype(BF16)
        a = jnp.dot(x, w1s[...], preferred_element_type=F32)
        c = jnp.dot(x, w3s[...], preferred_element_type=F32)
        h = (_silu(a) * c).astype(BF16)
        yb_ref[...] = _pack_bf16_pairs(jnp.dot(h, w2s[...], preferred_element_type=F32))


def _expert_call(blk_e, blk_valid, n_blk, blk_next, xb, w1, w3, w2, layer):
    def blk(b, be, bv, nb, nx):
        return (jnp.minimum(b, nb[0] - 1), 0)

    up, down = (D, MOE_HIDDEN), (MOE_HIDDEN, D)
    grid_spec = pltpu.PrefetchScalarGridSpec(
        num_scalar_prefetch=4,
        grid=(N_EXP_BLOCKS,),
        in_specs=[pl.BlockSpec((EXP_BLOCK, DP), blk)] + [pl.BlockSpec(memory_space=pl.ANY)] * 3,
        out_specs=pl.BlockSpec((EXP_BLOCK, DP), blk),
        scratch_shapes=[pltpu.VMEM((2,) + up, F32), pltpu.VMEM((2,) + up, F32), pltpu.VMEM((2,) + down, F32),
                        pltpu.VMEM(up, BF16), pltpu.VMEM(up, BF16), pltpu.VMEM(down, BF16),
                        pltpu.SemaphoreType.DMA((2, 3)), pltpu.SMEM((1,), jnp.int32)],
    )
    return pl.pallas_call(
        functools.partial(_expert_kernel, layer),
        grid_spec=grid_spec,
        out_shape=jax.ShapeDtypeStruct((P_ROWS, DP), U32),
        compiler_params=_params("arbitrary"),
        name="experts",
    )(blk_e, blk_valid, n_blk, blk_next, xb, w1, w3, w2)


def _route_kernel(lg_ref, dest_ref, gates_ref, meta_ref, cnt_ref, base_ref):
    ph = pl.program_id(0)
    t = pl.program_id(1)
    tm = ROUTE_TM
    lt = lg_ref[...].T
    el = lt[0:MOE_EXPERTS]
    gl = lt[MOE_EXPERTS:MOE_EXPERTS + 8]
    gidx = lax.broadcasted_iota(jnp.int32, (8, tm), 0)
    neg = jnp.float32(-jnp.inf)
    gl = jnp.where(gidx < MOE_GROUPS, gl, neg)
    gmax = jnp.max(gl, axis=0, keepdims=True)
    grp = jnp.min(jnp.where(gl == gmax, gidx, MOE_GROUPS), axis=0, keepdims=True)
    eidx = lax.broadcasted_iota(jnp.int32, (MOE_EXPERTS, tm), 0)
    els = jnp.where((eidx >> 3) == grp, el, neg)
    m1 = jnp.max(els, axis=0, keepdims=True)
    i1 = jnp.min(jnp.where(els == m1, eidx, MOE_EXPERTS), axis=0, keepdims=True)
    els2 = jnp.where(eidx == i1, neg, els)
    m2 = jnp.max(els2, axis=0, keepdims=True)
    i2 = jnp.min(jnp.where(els2 == m2, eidx, MOE_EXPERTS), axis=0, keepdims=True)
    sel1 = eidx == i1
    sel2 = eidx == i2
    cnt = jnp.where(sel1 | sel2, 1.0, 0.0)
    tile_counts = jnp.sum(cnt, axis=1, keepdims=True)

    @pl.when(ph == 0)
    def _():
        @pl.when(t == 0)
        def _():
            cnt_ref[...] = jnp.zeros_like(cnt_ref)

        cnt_ref[...] += tile_counts

    @pl.when(ph == 1)
    def _():
        @pl.when(t == 0)
        def _():
            counts = cnt_ref[...]
            nblk = jnp.floor((counts + (EXP_BLOCK - 1.0)) * (1.0 / EXP_BLOCK))
            r = lax.broadcasted_iota(jnp.int32, (MOE_EXPERTS, MOE_EXPERTS), 0)
            c = lax.broadcasted_iota(jnp.int32, (MOE_EXPERTS, MOE_EXPERTS), 1)
            nblk_row = jnp.sum(jnp.where(r == c, nblk, 0.0), axis=0, keepdims=True)
            bstart = jnp.sum(jnp.where(c < r, nblk_row, 0.0), axis=1, keepdims=True)
            base_ref[...] = bstart * EXP_BLOCK
            bend = bstart + nblk
            bidx = lax.broadcasted_iota(jnp.int32, (1, META_LANES), 1).astype(F32)
            blk_e = jnp.minimum(jnp.sum(jnp.where(bidx >= bend, 1.0, 0.0), axis=0, keepdims=True),
                                MOE_EXPERTS - 1.0)
            erow = lax.broadcasted_iota(jnp.int32, (MOE_EXPERTS, META_LANES), 0).astype(F32)
            mine = erow == blk_e
            cnt_b = jnp.sum(jnp.where(mine, counts, 0.0), axis=0, keepdims=True)
            start_b = jnp.sum(jnp.where(mine, bstart, 0.0), axis=0, keepdims=True)
            valid = jnp.clip(cnt_b - (bidx - start_b) * EXP_BLOCK, 0.0, float(EXP_BLOCK))
            n_blk = jnp.sum(nblk, axis=0, keepdims=True)
            end_b = jnp.sum(jnp.where(mine, bend, 0.0), axis=0, keepdims=True)
            nxt = jnp.minimum(jnp.sum(jnp.where(end_b >= bend, 1.0, 0.0), axis=0, keepdims=True),
                              MOE_EXPERTS - 1.0)
            nxt = jnp.where(end_b < n_blk, nxt, -1.0)
            mrow = lax.broadcasted_iota(jnp.int32, (8, META_LANES), 0)
            meta = jnp.where(mrow == 0, blk_e, jnp.where(mrow == 1, valid, jnp.where(
                mrow == 2, n_blk, jnp.where(mrow == 3, nxt, 0.0))))
            meta_ref[...] = meta.astype(jnp.int32)

        lane = ROUTER_LANES
        before = (lax.broadcasted_iota(jnp.int32, (lane, lane), 0)
                  < lax.broadcasted_iota(jnp.int32, (lane, lane), 1))
        tri = jnp.where(before, 1.0, 0.0).astype(BF16)
        run = base_ref[...]
        d1, d2 = [], []
        for k in range(tm // lane):
            piece = slice(k * lane, (k + 1) * lane)
            ck = cnt[:, piece]
            pos = run + jnp.dot(ck.astype(BF16), tri, preferred_element_type=F32)
            d1.append(jnp.sum(jnp.where(sel1[:, piece], pos, 0.0), axis=0, keepdims=True))
            d2.append(jnp.sum(jnp.where(sel2[:, piece], pos, 0.0), axis=0, keepdims=True))
            run = run + jnp.sum(ck, axis=1, keepdims=True)
        dest_ref[...] = jnp.concatenate(
            [jnp.concatenate(d1, axis=1), jnp.concatenate(d2, axis=1)], axis=0).astype(jnp.int32)
        base_ref[...] = run
        g_w = 1.0 / jnp.sum(jnp.exp(gl - gmax), axis=0, keepdims=True)
        e21 = jnp.exp(m2 - m1)
        p1 = 1.0 / (1.0 + e21)
        rid = lax.broadcasted_iota(jnp.int32, (ROUTER_LANES, tm), 0)
        gt = jnp.where(rid == 0, g_w * p1, jnp.where(rid == 1, g_w * (e21 * p1), 0.0))
        gates_ref[...] = gt.T


def _route_call(logits):
    return pl.pallas_call(
        _route_kernel,
        grid=(2, N_ROUTE_TILES),
        in_specs=[pl.BlockSpec((ROUTE_TM, ROUTER_LANES), lambda ph, t: (t, 0))],
        out_specs=[pl.BlockSpec((MOE_TOPK, ROUTE_TM), lambda ph, t: (0, t * ph)),
                   pl.BlockSpec((ROUTE_TM, ROUTER_LANES), lambda ph, t: (t * ph, 0)),
                   pl.BlockSpec((8, META_LANES), lambda ph, t: (0, 0))],
        out_shape=[jax.ShapeDtypeStruct((MOE_TOPK, T_ALL), jnp.int32),
                   jax.ShapeDtypeStruct((T_ALL, ROUTER_LANES), F32),
                   jax.ShapeDtypeStruct((8, META_LANES), jnp.int32)],
        scratch_shapes=[pltpu.VMEM((MOE_EXPERTS, 1), F32), pltpu.VMEM((MOE_EXPERTS, 1), F32)],
        compiler_params=_params("arbitrary", "arbitrary"),
        name="route",
    )(logits)


def _sc_mesh():
    return plsc.VectorSubcoreMesh(core_axis_name="c", subcore_axis_name="s")


def _sc_token_offset(j):
    wid = lax.axis_index("s") * SC_CORES + lax.axis_index("c")
    return pl.multiple_of(wid * SC_ROWS_PER_WORKER + j * SC_CHUNK, 8)


def _dispatch_body(h_hbm, d0_hbm, d1_hbm, out_hbm, i0_v, i1_v, rows_v, sem0, sem1, sem2):
    @pl.loop(0, SC_ROWS_PER_WORKER // SC_CHUNK)
    def _(j):
        off = _sc_token_offset(j)
        c0 = pltpu.async_copy(d0_hbm.at[pl.ds(off, SC_CHUNK)], i0_v, sem0)
        c1 = pltpu.async_copy(d1_hbm.at[pl.ds(off, SC_CHUNK)], i1_v, sem1)
        c2 = pltpu.async_copy(h_hbm.at[pl.ds(off, SC_CHUNK)], rows_v, sem2)
        c0.wait()
        c1.wait()
        c2.wait()
        s0 = pltpu.async_copy(rows_v, out_hbm.at[i0_v], sem0)
        s1 = pltpu.async_copy(rows_v, out_hbm.at[i1_v], sem1)
        s0.wait()
        s1.wait()


def _dispatch_call(h2, dest0, dest1):
    return pl.kernel(
        _dispatch_body,
        out_type=jax.ShapeDtypeStruct((P_ROWS, DP), U32),
        mesh=_sc_mesh(),
        scratch_types=[pltpu.VMEM((SC_CHUNK,), jnp.int32), pltpu.VMEM((SC_CHUNK,), jnp.int32),
                       pltpu.VMEM((SC_CHUNK, DP), U32),
                       pltpu.SemaphoreType.DMA, pltpu.SemaphoreType.DMA, pltpu.SemaphoreType.DMA],
        name="moe_dispatch",
    )(h2, dest0, dest1)


def _combine_body(yb_hbm, d0_hbm, d1_hbm, o0_hbm, o1_hbm, i0_v, i1_v, r0_v, r1_v, sem0, sem1):
    @pl.loop(0, SC_ROWS_PER_WORKER // SC_CHUNK)
    def _(j):
        off = _sc_token_offset(j)
        c0 = pltpu.async_copy(d0_hbm.at[pl.ds(off, SC_CHUNK)], i0_v, sem0)
        c1 = pltpu.async_copy(d1_hbm.at[pl.ds(off, SC_CHUNK)], i1_v, sem1)
        c0.wait()
        c1.wait()
        g0 = pltpu.async_copy(yb_hbm.at[i0_v], r0_v, sem0)
        g1 = pltpu.async_copy(yb_hbm.at[i1_v], r1_v, sem1)
        g0.wait()
        g1.wait()
        w0 = pltpu.async_copy(r0_v, o0_hbm.at[pl.ds(off, SC_CHUNK)], sem0)
        w1 = pltpu.async_copy(r1_v, o1_hbm.at[pl.ds(off, SC_CHUNK)], sem1)
        w0.wait()
        w1.wait()


def _combine_call(yb, dest0, dest1):
    out = jax.ShapeDtypeStruct((T_ALL, DP), U32)
    return pl.kernel(
        _combine_body,
        out_type=(out, out),
        mesh=_sc_mesh(),
        scratch_types=[pltpu.VMEM((SC_CHUNK,), jnp.int32), pltpu.VMEM((SC_CHUNK,), jnp.int32),
                       pltpu.VMEM((SC_CHUNK, DP), U32), pltpu.VMEM((SC_CHUNK, DP), U32),
                       pltpu.SemaphoreType.DMA, pltpu.SemaphoreType.DMA],
        name="moe_combine",
    )(yb, dest0, dest1)


def _moe_rows(h2, dest, meta, w1, w3, w2, layer, before_experts):
    dest0, dest1 = dest[0], dest[1]
    xb = _dispatch_call(h2, dest0, dest1)
    xb, *before_experts = lax.optimization_barrier((xb,) + tuple(before_experts))
    yb = _expert_call(meta[0, :N_EXP_BLOCKS], meta[1, :N_EXP_BLOCKS], meta[2, :1], meta[3, :N_EXP_BLOCKS],
                      xb, w1, w3, w2, layer)
    return _combine_call(yb, dest0, dest1), before_experts


def _final_kernel(x_ref, yg0_ref, yg1_ref, gates_ref, modp_ref, gain_ref, o_ref):
    x = _add_moe(x_ref[...], yg0_ref, yg1_ref, gates_ref, modp_ref)
    o_ref[...] = _rms(x) * gain_ref[...]


def _final_call(x, prev, gain, row0, n_rows):
    rows = ROWS_WIDE
    tile0 = row0 // rows

    def tile(width):
        return pl.BlockSpec((rows, width), lambda i: (tile0 + i, 0))

    return pl.pallas_call(
        _final_kernel,
        grid=(n_rows // rows,),
        in_specs=[tile(D), tile(DP), tile(DP), tile(ROUTER_LANES),
                  pl.BlockSpec((rows // GROUP, 6, D), lambda i: (tile0 + i, 0, 0)), _resident((1, D))],
        out_specs=pl.BlockSpec((rows, D), lambda i: (i, 0)),
        out_shape=jax.ShapeDtypeStruct((n_rows, D), F32),
        compiler_params=_params("parallel"),
        name="final_norm",
    )(x, *prev, gain)


def _rope_tables():
    pos = np.concatenate([np.tile(np.arange(SEQ), BATCH),
                          np.tile(PAST_LEN + np.arange(DEC_SEQ), DEC_BATCH)]).astype(np.float32)
    inv = (ROPE_BASE ** (-np.arange(ROPE_HALF, dtype=np.float32) / ROPE_HALF)).astype(np.float32)
    ang = (pos[:, None] * inv[None, :]).astype(np.float32).astype(np.float64)
    return jnp.asarray(np.cos(ang), F32), jnp.asarray(np.sin(ang), F32)


def kernel(x_prompt, x_sample, c_prompt, c_sample, state_ret, ada_w, ada_b, norm1_g, norm2_g, ret_w_in,
           ret_w_out, gm_w_in, gm_b_in, gm_ln_g, gm_ln_b, gm_w_s, gm_b_s, gm_w_out, gm_b_out, moe_w_rg,
           moe_b_rg, moe_w_re, moe_b_re, moe_w1, moe_w3, moe_w2, final_g):
    x = (x_prompt.reshape(T_PROMPT, D), x_sample.reshape(T_SAMPLE, D))
    c_all = jnp.concatenate([c_prompt, c_sample], axis=0)
    cos, sin = _rope_tables()
    dec = _ret_row_scales()

    def layer_params(i, dep):
        j = i // 2
        mixer = (ret_w_in, ret_w_out) if i % 2 == 0 else (gm_w_in, gm_w_out, gm_w_s, gm_b_s)
        tied = (c_all, ada_w, ada_b, moe_w_re, moe_w_rg, moe_b_re, moe_b_rg) + mixer
        if dep is not None:
            tied = lax.optimization_barrier((dep,) + tied)[1:]
        c_t, ada_w_t, ada_b_t, w_re, w_rg, b_re, b_rg = tied[:7]
        mod_seq = _ada_call(c_t, ada_w_t, ada_b_t, i).reshape(N_SEQ, 6, D)
        mod = jnp.concatenate([
            jnp.broadcast_to(mod_seq[:BATCH, None], (BATCH, SEQ // GROUP, 6, D)).reshape(-1, 6, D),
            jnp.broadcast_to(mod_seq[BATCH:, None], (DEC_BATCH, DEC_SEQ // GROUP, 6, D)).reshape(-1, 6, D),
        ], axis=0)
        w_r = jnp.pad(jnp.concatenate([w_re[i], w_rg[i]], axis=1),
                      ((0, 0), (0, ROUTER_LANES - MOE_GROUPS - MOE_EXPERTS)))
        w_r_hi = w_r.astype(BF16)
        w_r_lo = (w_r - w_r_hi.astype(F32)).astype(BF16)
        w_r = jnp.concatenate([w_r_hi, w_r_lo], axis=1)
        b_r = jnp.pad(jnp.concatenate([b_re[i].reshape(-1), b_rg[i]]),
                      (0, ROUTER_LANES - MOE_GROUPS - MOE_EXPERTS)).reshape(1, ROUTER_LANES)
        weights = [w[j:j + 1].astype(BF16) for w in tied[7:9]]
        if i % 2 == 1:
            weights += list(_gm_block_diag(tied[9][j], tied[10][j]))
        return mod, w_r, b_r, weights

    ret_prompt = ret_sample = gm_sample = None
    prev = None
    params = layer_params(0, None)
    for i in range(DEPTH):
        j = i // 2
        mod, w_r, b_r, weights = params
        g1 = norm1_g[i].reshape(1, D)
        g2 = norm2_g[i].reshape(1, D)
        if i % 2 == 0:
            w_in_b, w_out_b = weights
            x, p = _ret_proj_call(x, prev, mod, g1, w_in_b, cos, sin, dec)
            y_p, ret_prompt = _ret_core_call(p, None, ret_prompt, j, BATCH, SEQ, RET_CHUNK_PROMPT, 0)
            y_s, ret_sample = _ret_core_call(p, state_ret, ret_sample, j, DEC_BATCH, DEC_SEQ,
                                             RET_CHUNK_SAMPLE, T_PROMPT)
            x, h2, logits = _ret_out_call(y_p, y_s, w_out_b, x, mod, g2, w_r, b_r)
        else:
            w_in_b, w_out_b, ws_bd, bs_bd = weights
            x, u, v, gm_sample = _gm_proj_call(x, prev, mod, g1, w_in_b, j,
                                               gm_b_in[j].reshape(1, GM_FFN), gm_ln_g[j].reshape(1, GM_HALF),
                                               gm_ln_b[j].reshape(1, GM_HALF), gm_sample)
            x, h2, logits = _gm_out_call(u, v, ws_bd, bs_bd, w_out_b,
                                         gm_b_out[j].reshape(1, D), x, mod, g2, w_r, b_r)
        dest, gates, meta = _route_call(logits)
        next_weights = []
        if i + 1 < DEPTH:
            next_mod, next_w_r, next_b_r, next_weights = layer_params(i + 1, dest)
        (yg0, yg1), next_weights = _moe_rows(h2, dest, meta, moe_w1, moe_w3, moe_w2, i, next_weights)
        if i + 1 < DEPTH:
            params = (next_mod, next_w_r, next_b_r, next_weights)
        prev = (yg0, yg1, gates, mod)

    fg = final_g.reshape(1, D)
    y_prompt = _final_call(x, prev, fg, 0, T_PROMPT).reshape(BATCH, SEQ, D)
    y_sample = _final_call(x, prev, fg, T_PROMPT, T_SAMPLE).reshape(DEC_BATCH, DEC_SEQ, D)
    return (y_prompt, y_sample, ret_prompt, ret_sample,
            gm_sample.reshape(N_GM, DEC_BATCH, DEC_SEQ, GM_HALF))
```

```python
import functools

import numpy as np
import jax
import jax.numpy as jnp
from jax import lax
from jax.experimental import pallas as pl
from jax.experimental.pallas import tpu as pltpu
from jax.experimental.pallas import tpu_sc as plsc

F32 = jnp.float32
BF16 = jnp.bfloat16
U32 = jnp.uint32

D = 1024
BATCH, SEQ = 4, 4096
DEC_BATCH, DEC_SEQ = 16, 64
PAST_LEN = 4096
DEPTH = 4
N_RET = (DEPTH + 1) // 2
N_GM = DEPTH // 2
N_SEQ = BATCH + DEC_BATCH

RET_HEADS, RET_DK, RET_DV = 4, 256, 512
RET_QK = RET_HEADS * RET_DK
RET_V = RET_HEADS * RET_DV
RET_IN = 2 * RET_QK + 2 * RET_V
ROPE_BASE = 10000.0
ROPE_HALF = RET_DK // 2

GM_FFN = 6 * D
GM_HALF = GM_FFN // 2
GM_GROUPS = 4
GM_GDIM = GM_HALF // GM_GROUPS
GM_CHUNK = 128

MOE_GROUPS, MOE_PER_GROUP = 4, 8
MOE_EXPERTS = MOE_GROUPS * MOE_PER_GROUP
MOE_TOPK = 2
MOE_HIDDEN = 512
EPS = 1e-6

GROUP = DEC_SEQ
T_PROMPT = BATCH * SEQ
T_SAMPLE = DEC_BATCH * DEC_SEQ
T_ALL = T_PROMPT + T_SAMPLE
N_GROUPS = T_ALL // GROUP
ROWS_WIDE = 512
ROWS_GM_PROJ = 256

RET_CHUNK_PROMPT = 256
RET_CHUNK_SAMPLE = DEC_SEQ

GM_MIX = 256

EXP_BLOCK = 512
EXP_ROW_STEP = 128
N_ASSIGN = T_ALL * MOE_TOPK
N_EXP_BLOCKS = -(-(N_ASSIGN + MOE_EXPERTS * (EXP_BLOCK - 1)) // EXP_BLOCK)
P_ROWS = N_EXP_BLOCKS * EXP_BLOCK
ROUTER_LANES = 128
ROUTE_TM = 1024
N_ROUTE_TILES = T_ALL // ROUTE_TM
META_LANES = 256
assert META_LANES >= N_EXP_BLOCKS

DP = D // 2
SC_CORES, SC_SUBCORES = 2, 16
SC_WORKERS = SC_CORES * SC_SUBCORES
SC_ROWS_PER_WORKER = T_ALL // SC_WORKERS
SC_CHUNK = 32
assert SC_ROWS_PER_WORKER % SC_CHUNK == 0 and SC_CHUNK % 8 == 0

V7X_VMEM_LIMIT_BYTES = 56 * 1024 * 1024


def _params(*sem):
    return pltpu.CompilerParams(dimension_semantics=sem, vmem_limit_bytes=V7X_VMEM_LIMIT_BYTES)


def _resident(shape):
    nd = len(shape)
    return pl.BlockSpec(shape, lambda *_: (0,) * nd, pipeline_mode=pl.Buffered(1))


def _rms(x):
    return x * lax.rsqrt(jnp.mean(x * x, axis=-1, keepdims=True) + EPS)


def _silu(x):
    return x * jax.nn.sigmoid(x)


def _per_group(x2d, fn):
    rows = x2d.shape[0]
    return fn(x2d.reshape(rows // GROUP, GROUP, D)).reshape(rows, D)


def _tile_mod(mod_refs, is_prompt):
    modp_ref, mods_ref = mod_refs
    return jnp.where(is_prompt, jnp.broadcast_to(modp_ref[...], mods_ref.shape), mods_ref[...])


def _norm_mod(x, gain_ref, mod, shift_idx):
    y = _rms(x) * gain_ref[...]
    scale = mod[:, shift_idx + 1:shift_idx + 2, :]
    shift = mod[:, shift_idx:shift_idx + 1, :]
    return _per_group(y, lambda y3: y3 * (1.0 + scale) + shift)


def _pack_bf16_pairs(x):
    lo = lax.bitcast_convert_type(x[:, :DP].astype(BF16).astype(F32), U32)
    hi = lax.bitcast_convert_type(x[:, DP:].astype(BF16).astype(F32), U32)
    return (lo >> 16) | (hi & U32(0xFFFF0000))


def _unpack_bf16_pairs(w):
    lo = lax.bitcast_convert_type(w << 16, F32)
    hi = lax.bitcast_convert_type(w & U32(0xFFFF0000), F32)
    return jnp.concatenate([lo, hi], axis=1)


def _add_moe(x, yg0_ref, yg1_ref, gates_ref, mod_prev):
    g = gates_ref[...]
    y = g[:, 0:1] * _unpack_bf16_pairs(yg0_ref[...]) + g[:, 1:2] * _unpack_bf16_pairs(yg1_ref[...])
    gate2 = mod_prev[:, 5:6, :]
    return x + _per_group(y, lambda y3: y3 * gate2)


ADA_TN = 1536


def _ada_kernel(c_ref, w_ref, b_ref, o_ref):
    c = c_ref[...]
    s = _silu(c).astype(BF16)
    o_ref[0] = jnp.dot(s, w_ref[0].astype(BF16), preferred_element_type=F32) + b_ref[0]


def _ada_call(c_all, ada_w, ada_b):
    return pl.pallas_call(
        _ada_kernel,
        grid=(DEPTH, 6 * D // ADA_TN),
        in_specs=[
            pl.BlockSpec((N_SEQ, D), lambda i, j: (0, 0)),
            pl.BlockSpec((1, D, ADA_TN), lambda i, j: (i, 0, j)),
            pl.BlockSpec((1, 1, ADA_TN), lambda i, j: (i, 0, j)),
        ],
        out_specs=pl.BlockSpec((1, N_SEQ, ADA_TN), lambda i, j: (i, 0, j)),
        out_shape=jax.ShapeDtypeStruct((DEPTH, N_SEQ, 6 * D), F32),
        compiler_params=_params("parallel", "parallel"),
        name="ada_modulation",
    )(c_all, ada_w, ada_b.reshape(DEPTH, 1, 6 * D))


def _n_tiles(rows):
    return T_ALL // rows


def _n_prompt_tiles(rows):
    return T_PROMPT // rows


def _row_spec(rows, width):
    return pl.BlockSpec((rows, width), lambda i: (i, 0))


def _mod_specs(rows):
    npt = _n_prompt_tiles(rows)
    return [pl.BlockSpec((1, 6, D), lambda i: (jnp.minimum(i * rows // SEQ, BATCH - 1), 0, 0)),
            pl.BlockSpec((rows // DEC_SEQ, 6, D), lambda i: (jnp.maximum(i - npt, 0), 0, 0))]


def _prev_specs(rows):
    return [_row_spec(rows, DP), _row_spec(rows, DP), _row_spec(rows, ROUTER_LANES)] + _mod_specs(rows)


def _prompt_rows_spec(rows, width):
    last = _n_prompt_tiles(rows) - 1
    return pl.BlockSpec((rows, width), lambda i: (jnp.minimum(i, last), 0))


def _sample_rows_spec(rows, width):
    npt = _n_prompt_tiles(rows)
    return pl.BlockSpec((rows, width), lambda i: (jnp.maximum(i - npt, 0), 0))


def _ret_proj_kernel(has_prev, n_prompt_tiles, *refs):
    is_prompt = pl.program_id(0) < n_prompt_tiles
    if has_prev:
        x_ref, yg0_ref, yg1_ref, gates_ref = refs[:4]
        x = _add_moe(x_ref[...], yg0_ref, yg1_ref, gates_ref, _tile_mod(refs[4:6], is_prompt))
        refs = refs[6:]
    else:
        x = jnp.where(is_prompt, refs[0][...], refs[1][...])
        refs = refs[2:]
    mod = _tile_mod(refs[:2], is_prompt)
    gain_ref, w_ref, cos_ref, sin_ref, dec_ref, xo_ref, p_ref = refs[2:]
    xo_ref[...] = x
    hb = _norm_mod(x, gain_ref, mod, 0).astype(BF16)
    cos = cos_ref[...]
    sin = sin_ref[...]
    dec = dec_ref[...]
    for j in range(2 * RET_HEADS):
        lo = j * RET_DK
        acc = jnp.dot(hb, w_ref[0, :, lo:lo + RET_DK], preferred_element_type=F32)
        x1 = acc[:, :ROPE_HALF]
        x2 = acc[:, ROPE_HALF:]
        scale = dec[:, j:j + 1]
        p_ref[:, lo:lo + ROPE_HALF] = ((x1 * cos - x2 * sin) * scale).astype(BF16)
        p_ref[:, lo + ROPE_HALF:lo + RET_DK] = ((x1 * sin + x2 * cos) * scale).astype(BF16)
    for j in range(2 * RET_HEADS):
        lo = 2 * RET_QK + j * RET_DV
        acc = jnp.dot(hb, w_ref[0, :, lo:lo + RET_DV], preferred_element_type=F32)
        if j >= RET_HEADS:
            acc = _silu(acc)
        p_ref[:, lo:lo + RET_DV] = acc.astype(BF16)


def _ret_proj_call(x, prev, mod, gain, w_in, cos, sin, dec):
    rows = ROWS_WIDE
    has_prev = prev is not None
    if has_prev:
        in_specs = [_row_spec(rows, D)] + _prev_specs(rows)
        args = [x] + list(prev)
    else:
        in_specs = [_prompt_rows_spec(rows, D), _sample_rows_spec(rows, D)]
        args = list(x)
    in_specs += _mod_specs(rows) + [
        _resident((1, D)), _resident((1, D, RET_IN)),
        _row_spec(rows, ROPE_HALF), _row_spec(rows, ROPE_HALF), _row_spec(rows, 2 * RET_HEADS)]
    args += [*mod, gain, w_in, cos, sin, dec]
    return pl.pallas_call(
        functools.partial(_ret_proj_kernel, has_prev, _n_prompt_tiles(rows)),
        grid=(_n_tiles(rows),),
        in_specs=in_specs,
        out_specs=[_row_spec(rows, D), _row_spec(rows, RET_IN)],
        out_shape=[jax.ShapeDtypeStruct((T_ALL, D), F32), jax.ShapeDtypeStruct((T_ALL, RET_IN), BF16)],
        compiler_params=_params("parallel"),
        name="ret_proj",
    )(*args)


def _ret_core_kernel(has_s0, n_chunks, layer, *refs):
    refs = list(refs)
    p_ref = refs.pop(0)
    s0_ref = refs.pop(0) if has_s0 else None
    causal_ref, cd_ref = refs[:2]
    y_ref, so_ref, s_ref = refs[-3:]
    c = pl.program_id(1)

    @pl.when(c == 0)
    def _():
        if has_s0:
            s_ref[...] = s0_ref[0, 0]
        else:
            s_ref[...] = jnp.zeros_like(s_ref)

    for h in range(RET_HEADS):
        qb = p_ref[:, h * RET_DK:(h + 1) * RET_DK]
        kb = p_ref[:, RET_QK + h * RET_DK:RET_QK + (h + 1) * RET_DK]
        vb = p_ref[:, 2 * RET_QK + h * RET_DV:2 * RET_QK + (h + 1) * RET_DV]
        gb = p_ref[:, 2 * RET_QK + RET_V + h * RET_DV:2 * RET_QK + RET_V + (h + 1) * RET_DV]
        scores = lax.dot_general(qb, kb, (((1,), (1,)), ((), ())), preferred_element_type=F32)
        scores = scores * causal_ref[...]
        s_old = s_ref[h]
        o = (jnp.dot(scores.astype(BF16), vb, preferred_element_type=F32)
             + jnp.dot(qb, s_old.astype(BF16), preferred_element_type=F32))
        s_ref[h] = cd_ref[h][:, 0:1] * (s_old + lax.dot_general(
            kb, vb, (((0,), (0,)), ((), ())), preferred_element_type=F32))
        y_ref[:, h * RET_DV:(h + 1) * RET_DV] = (gb.astype(F32) * _rms(o)).astype(BF16)

    @pl.when(c == n_chunks - 1)
    def _():
        so_ref[0, 0] = s_ref[...]
        if layer == 0:
            for later in range(1, N_RET):
                so_ref[later, 0] = jnp.zeros_like(s_ref)


def _ret_log_gamma():
    return np.log1p(-np.exp2(-5.0 - np.arange(RET_HEADS, dtype=np.float64)))


def _ret_chunk_tables(cl):
    idx = np.arange(cl)
    causal = (idx[:, None] >= idx[None, :]).astype(np.float32)
    cd = np.broadcast_to(np.exp(_ret_log_gamma() * cl)[:, None, None], (RET_HEADS, 1, 128))
    return jnp.asarray(causal, F32), jnp.asarray(cd, F32)


def _ret_row_scales():
    c = np.concatenate([np.arange(T_PROMPT) % RET_CHUNK_PROMPT,
                        np.arange(T_SAMPLE) % RET_CHUNK_SAMPLE]).astype(np.float64)
    e = (c[:, None] + 1.0) * _ret_log_gamma()[None, :]
    return jnp.asarray(np.concatenate([np.exp(e), np.exp(-e) * RET_DK ** -0.5], axis=1), F32)


def _ret_core_call(p, s0, states, layer, n_seq, seq_len, cl, row0):
    has_s0 = s0 is not None
    n_chunks = seq_len // cl
    rb0 = row0 // cl
    state = (RET_HEADS, RET_DK, RET_DV)
    in_specs = [pl.BlockSpec((cl, RET_IN), lambda b, c: (rb0 + b * n_chunks + c, 0))]
    args = [p]
    if has_s0:
        in_specs.append(pl.BlockSpec((1, 1) + state, lambda b, c: (layer, b, 0, 0, 0)))
        args.append(s0)
    in_specs += [_resident((cl, cl)), _resident((RET_HEADS, 1, 128))]
    args += list(_ret_chunk_tables(cl))
    if layer == 0:
        assert states is None
        state_spec = pl.BlockSpec((N_RET, 1) + state, lambda b, c: (0, b, 0, 0, 0))
        aliases = {}
    else:
        in_specs.append(pl.BlockSpec(memory_space=pl.ANY))
        args.append(states)
        state_spec = pl.BlockSpec((1, 1) + state, lambda b, c: (layer, b, 0, 0, 0))
        aliases = {len(args) - 1: 1}
    return pl.pallas_call(
        functools.partial(_ret_core_kernel, has_s0, n_chunks, layer),
        grid=(n_seq, n_chunks),
        in_specs=in_specs,
        out_specs=[pl.BlockSpec((cl, RET_V), lambda b, c: (b * n_chunks + c, 0)), state_spec],
        out_shape=[jax.ShapeDtypeStruct((n_seq * seq_len, RET_V), BF16),
                   jax.ShapeDtypeStruct((N_RET, n_seq) + state, F32)],
        scratch_shapes=[pltpu.VMEM(state, F32)],
        input_output_aliases=aliases,
        compiler_params=_params("parallel", "arbitrary"),
        name="ret_core",
    )(*args)


def _residual_router(acc, x_ref, mod, gain_ref, wr_ref, br_ref, xo_ref, h2_ref, lg_ref):
    gate1 = mod[:, 2:3, :]
    xn = x_ref[...] + _per_group(acc, lambda a3: a3 * gate1)
    xo_ref[...] = xn
    h2 = _norm_mod(xn, gain_ref, mod, 3)
    h2_ref[...] = _pack_bf16_pairs(h2)
    hh = jnp.dot(h2.astype(BF16), wr_ref[...], preferred_element_type=F32)
    lg_ref[...] = hh[:, :ROUTER_LANES] + hh[:, ROUTER_LANES:] + br_ref[...]


def _mix_out_specs(rows):
    return [_row_spec(rows, D), _row_spec(rows, DP), _row_spec(rows, ROUTER_LANES)]


_MIX_OUT_SHAPE = [
    jax.ShapeDtypeStruct((T_ALL, D), F32),
    jax.ShapeDtypeStruct((T_ALL, DP), U32),
    jax.ShapeDtypeStruct((T_ALL, ROUTER_LANES), F32),
]


def _router_specs():
    return [_resident((D, 2 * ROUTER_LANES)), _resident((1, ROUTER_LANES))]


def _ret_out_kernel(n_prompt_tiles, yp_ref, ys_ref, w_ref, x_ref, modp_ref, mods_ref, gain_ref, wr_ref,
                    br_ref, xo_ref, h2_ref, lg_ref):
    is_prompt = pl.program_id(0) < n_prompt_tiles
    yin = jnp.where(is_prompt, yp_ref[...], ys_ref[...])
    acc = jnp.dot(yin, w_ref[0], preferred_element_type=F32)
    mod = _tile_mod((modp_ref, mods_ref), is_prompt)
    _residual_router(acc, x_ref, mod, gain_ref, wr_ref, br_ref, xo_ref, h2_ref, lg_ref)


def _ret_out_call(y_prompt, y_sample, w_out, x, mod, gain, w_r, b_r):
    rows = ROWS_WIDE
    return pl.pallas_call(
        functools.partial(_ret_out_kernel, _n_prompt_tiles(rows)),
        grid=(_n_tiles(rows),),
        in_specs=[_prompt_rows_spec(rows, RET_V), _sample_rows_spec(rows, RET_V),
                  _resident((1, RET_V, D)), _row_spec(rows, D), *_mod_specs(rows),
                  _resident((1, D))] + _router_specs(),
        out_specs=_mix_out_specs(rows),
        out_shape=_MIX_OUT_SHAPE,
        compiler_params=_params("parallel"),
        name="ret_out",
    )(y_prompt, y_sample, w_out, x, *mod, gain, w_r, b_r)


GM_TN = 512


_GELU_C = float(np.sqrt(2.0 / np.pi))


def _gelu_tanh(x):
    hx = 0.5 * x
    return hx * jnp.tanh(x * (_GELU_C + (_GELU_C * 0.044715) * (x * x))) + hx


def _gm_proj_kernel(n_prompt_tiles, layer, *refs):
    x_ref, yg0_ref, yg1_ref, gates_ref = refs[:4]
    gain_ref, w_ref, b_ref, lg_ref, lb_ref = refs[8:13]
    xo_ref, u_ref, v_ref, vs_ref, vraw_ref = refs[-5:]
    is_prompt = pl.program_id(0) < n_prompt_tiles
    is_sample = jnp.logical_not(is_prompt)
    x = _add_moe(x_ref[...], yg0_ref, yg1_ref, gates_ref, _tile_mod(refs[4:6], is_prompt))
    xo_ref[...] = x
    hb = _norm_mod(x, gain_ref, _tile_mod(refs[6:8], is_prompt), 0).astype(BF16)
    rows = hb.shape[0]

    def gelu_chunk(lo):
        z = jnp.dot(hb, w_ref[0, :, lo:lo + GM_TN], preferred_element_type=F32) + b_ref[:, lo:lo + GM_TN]
        return _gelu_tanh(z.astype(BF16))

    s1 = jnp.zeros((rows, 128), F32)
    s2 = jnp.zeros((rows, 128), F32)
    for lo in range(0, GM_HALF, GM_TN):
        gz = gelu_chunk(GM_HALF + lo)
        vraw_ref[:, lo:lo + GM_TN] = gz
        gf = gz.astype(F32)
        for k in range(0, GM_TN, 128):
            piece = gf[:, k:k + 128]
            s1 = s1 + piece
            s2 = s2 + piece * piece
    mu = jnp.sum(s1, axis=-1, keepdims=True) * (1.0 / GM_HALF)
    var = jnp.sum(s2, axis=-1, keepdims=True) * (1.0 / GM_HALF) - mu * mu
    rstd = lax.rsqrt(var + EPS)
    shift = -mu * rstd

    for lo in range(0, GM_HALF, GM_TN):
        u_ref[:, lo:lo + GM_TN] = gelu_chunk(lo)
        vn = ((vraw_ref[:, lo:lo + GM_TN].astype(F32) * rstd + shift) * lg_ref[:, lo:lo + GM_TN]
              + lb_ref[:, lo:lo + GM_TN])
        v_ref[:, lo:lo + GM_TN] = vn.astype(BF16)
        vs_ref[0, :, lo:lo + GM_TN] = vn

    if layer == 0:
        @pl.when(is_sample)
        def _():
            for later in range(1, N_GM):
                vs_ref[later] = jnp.zeros((rows, GM_HALF), F32)


def _gm_proj_call(x, prev, mod, gain, w_in, layer, b_in, ln_g, ln_b, vs_all):
    rows = ROWS_GM_PROJ
    npt = _n_prompt_tiles(rows)
    in_specs = [_row_spec(rows, D)] + _prev_specs(rows) + _mod_specs(rows) + [
        _resident((1, D)), _resident((1, D, GM_FFN)), _resident((1, GM_FFN)),
        _resident((1, GM_HALF)), _resident((1, GM_HALF))]
    args = [x, *prev, *mod, gain, w_in, b_in, ln_g, ln_b]
    if layer == 0:
        assert vs_all is None
        vs_spec = pl.BlockSpec((N_GM, rows, GM_HALF), lambda i: (0, jnp.maximum(i - npt, 0), 0))
        aliases = {}
    else:
        in_specs.append(pl.BlockSpec(memory_space=pl.ANY))
        args.append(vs_all)
        vs_spec = pl.BlockSpec((1, rows, GM_HALF), lambda i: (layer, jnp.maximum(i - npt, 0), 0))
        aliases = {len(args) - 1: 3}
    return pl.pallas_call(
        functools.partial(_gm_proj_kernel, npt, layer),
        grid=(_n_tiles(rows),),
        in_specs=in_specs,
        out_specs=[_row_spec(rows, D), _row_spec(rows, GM_HALF), _row_spec(rows, GM_HALF), vs_spec],
        out_shape=[jax.ShapeDtypeStruct((T_ALL, D), F32),
                   jax.ShapeDtypeStruct((T_ALL, GM_HALF), BF16),
                   jax.ShapeDtypeStruct((T_ALL, GM_HALF), BF16),
                   jax.ShapeDtypeStruct((N_GM, T_SAMPLE, GM_HALF), F32)],
        scratch_shapes=[pltpu.VMEM((rows, GM_HALF), BF16)],
        input_output_aliases=aliases,
        compiler_params=_params("arbitrary"),
        name="gm_proj",
    )(*args)


def _gm_out_kernel(n_prompt_tiles, u_ref, v_ref, ws_ref, bs_ref, w_ref, bo_ref, x_ref, modp_ref, mods_ref,
                   gain_ref, wr_ref, br_ref, xo_ref, h2_ref, lg_ref):
    rows = u_ref.shape[0]
    mod = _tile_mod((modp_ref, mods_ref), pl.program_id(0) < n_prompt_tiles)
    pieces = []
    for r0 in range(0, rows, GM_MIX):
        acc = jnp.zeros((GM_MIX, D), F32)
        for g in range(GM_GROUPS):
            lo = g * GM_GDIM
            sp = jnp.dot(ws_ref[0, g], v_ref[r0:r0 + GM_MIX, lo:lo + GM_GDIM],
                         preferred_element_type=F32) + bs_ref[0, g]
            gated = (u_ref[r0:r0 + GM_MIX, lo:lo + GM_GDIM].astype(F32) * sp).astype(BF16)
            acc = acc + jnp.dot(gated, w_ref[0, lo:lo + GM_GDIM, :], preferred_element_type=F32)
        pieces.append(acc)
    acc = jnp.concatenate(pieces, axis=0) + bo_ref[...]
    _residual_router(acc, x_ref, mod, gain_ref, wr_ref, br_ref, xo_ref, h2_ref, lg_ref)


def _gm_block_diag(w_s, b_s):
    mats, biases = [], []
    for cl in (GM_CHUNK, DEC_SEQ):
        tri = jnp.tril(jnp.ones((cl, cl), bool))
        blk = jnp.where(tri[None], w_s[:, :cl, :cl], 0.0)
        reps = GM_MIX // cl
        eye = jnp.eye(reps, dtype=w_s.dtype)
        bd = jnp.einsum("ab,gts->gatbs", eye, blk).reshape(GM_GROUPS, GM_MIX, GM_MIX)
        mats.append(bd)
        biases.append(jnp.tile(b_s[:, :cl], (1, reps))[:, :, None])
    return jnp.stack(mats).astype(BF16), jnp.stack(biases).astype(F32)


def _gm_out_call(u, v, ws_bd, bs_bd, w_out, b_out, x, mod, gain, w_r, b_r):
    rows = ROWS_WIDE
    npt = _n_prompt_tiles(rows)

    def variant(i):
        return jnp.where(i >= npt, 1, 0)

    return pl.pallas_call(
        functools.partial(_gm_out_kernel, npt),
        grid=(_n_tiles(rows),),
        in_specs=[_row_spec(rows, GM_HALF), _row_spec(rows, GM_HALF),
                  pl.BlockSpec((1, GM_GROUPS, GM_MIX, GM_MIX), lambda i: (variant(i), 0, 0, 0)),
                  pl.BlockSpec((1, GM_GROUPS, GM_MIX, 1), lambda i: (variant(i), 0, 0, 0)),
                  _resident((1, GM_HALF, D)), _resident((1, D)), _row_spec(rows, D),
                  *_mod_specs(rows), _resident((1, D))] + _router_specs(),
        out_specs=_mix_out_specs(rows),
        out_shape=_MIX_OUT_SHAPE,
        compiler_params=_params("arbitrary"),
        name="gm_out",
    )(u, v, ws_bd, bs_bd, w_out, b_out, x, *mod, gain, w_r, b_r)


def _expert_kernel(layer, be_ref, bv_ref, nb_ref, nx_ref, xb_ref, w1_hbm, w3_hbm, w2_hbm, yb_ref,
                   st1, st3, st2, w1s, w3s, w2s, sems, slot_ref):
    b = pl.program_id(0)

    def weight_copies(e, slot):
        return (pltpu.make_async_copy(w1_hbm.at[layer, e], st1.at[slot], sems.at[slot, 0]),
                pltpu.make_async_copy(w3_hbm.at[layer, e], st3.at[slot], sems.at[slot, 1]),
                pltpu.make_async_copy(w2_hbm.at[layer, e], st2.at[slot], sems.at[slot, 2]))

    @pl.when(b == 0)
    def _():
        slot_ref[0] = 0
        for cp in weight_copies(be_ref[0], 0):
            cp.start()

    @pl.when(b < nb_ref[0])
    def _():
        prev_e = be_ref[jnp.maximum(b - 1, 0)]

        @pl.when((b == 0) | (be_ref[b] != prev_e))
        def _():
            slot = slot_ref[0]
            for cp in weight_copies(be_ref[b], slot):
                cp.wait()
            w1s[...] = st1[slot].astype(BF16)
            w3s[...] = st3[slot].astype(BF16)
            w2s[...] = st2[slot].astype(BF16)

            @pl.when(nx_ref[b] >= 0)
            def _():
                for cp in weight_copies(nx_ref[b], 1 - slot):
                    cp.start()

            slot_ref[0] = 1 - slot

        valid = bv_ref[b]

        def run_rows(n):
            row = lax.broadcasted_iota(jnp.int32, (n, 1), 0)
            xw = jnp.where(row < valid, xb_ref[0:n], U32(0))
            x = _unpack_bf16_pairs(xw).astype(BF16)
            a = jnp.dot(x, w1s[...], preferred_element_type=F32)
            c = jnp.dot(x, w3s[...], preferred_element_type=F32)
            h = (_silu(a) * c).astype(BF16)
            yb_ref[0:n] = _pack_bf16_pairs(jnp.dot(h, w2s[...], preferred_element_type=F32))
            if n < EXP_BLOCK:
                yb_ref[n:EXP_BLOCK] = jnp.zeros((EXP_BLOCK - n, DP), U32)

        for n in range(EXP_ROW_STEP, EXP_BLOCK + 1, EXP_ROW_STEP):
            pl.when((valid > n - EXP_ROW_STEP) & (valid <= n))(functools.partial(run_rows, n))


def _expert_call(blk_e, blk_valid, n_blk, blk_next, xb, w1, w3, w2, layer):
    def blk(b, be, bv, nb, nx):
        return (jnp.minimum(b, nb[0] - 1), 0)

    up, down = (D, MOE_HIDDEN), (MOE_HIDDEN, D)
    grid_spec = pltpu.PrefetchScalarGridSpec(
        num_scalar_prefetch=4,
        grid=(N_EXP_BLOCKS,),
        in_specs=[pl.BlockSpec((EXP_BLOCK, DP), blk)] + [pl.BlockSpec(memory_space=pl.ANY)] * 3,
        out_specs=pl.BlockSpec((EXP_BLOCK, DP), blk),
        scratch_shapes=[pltpu.VMEM((2,) + up, F32), pltpu.VMEM((2,) + up, F32), pltpu.VMEM((2,) + down, F32),
                        pltpu.VMEM(up, BF16), pltpu.VMEM(up, BF16), pltpu.VMEM(down, BF16),
                        pltpu.SemaphoreType.DMA((2, 3)), pltpu.SMEM((1,), jnp.int32)],
    )
    return pl.pallas_call(
        functools.partial(_expert_kernel, layer),
        grid_spec=grid_spec,
        out_shape=jax.ShapeDtypeStruct((P_ROWS, DP), U32),
        compiler_params=_params("arbitrary"),
        name="experts",
    )(blk_e, blk_valid, n_blk, blk_next, xb, w1, w3, w2)


def _route_kernel(lg_ref, dest_ref, gates_ref, meta_ref, cnt_ref, base_ref):
    ph = pl.program_id(0)
    t = pl.program_id(1)
    tm = ROUTE_TM
    lt = lg_ref[...].T
    el = lt[0:MOE_EXPERTS]
    gl = lt[MOE_EXPERTS:MOE_EXPERTS + 8]
    gidx = lax.broadcasted_iota(jnp.int32, (8, tm), 0)
    neg = jnp.float32(-jnp.inf)
    gl = jnp.where(gidx < MOE_GROUPS, gl, neg)
    gmax = jnp.max(gl, axis=0, keepdims=True)
    grp = jnp.min(jnp.where(gl == gmax, gidx, MOE_GROUPS), axis=0, keepdims=True)
    eidx = lax.broadcasted_iota(jnp.int32, (MOE_EXPERTS, tm), 0)
    els = jnp.where((eidx >> 3) == grp, el, neg)
    m1 = jnp.max(els, axis=0, keepdims=True)
    i1 = jnp.min(jnp.where(els == m1, eidx, MOE_EXPERTS), axis=0, keepdims=True)
    els2 = jnp.where(eidx == i1, neg, els)
    m2 = jnp.max(els2, axis=0, keepdims=True)
    i2 = jnp.min(jnp.where(els2 == m2, eidx, MOE_EXPERTS), axis=0, keepdims=True)
    sel1 = eidx == i1
    sel2 = eidx == i2
    cnt = jnp.where(sel1 | sel2, 1.0, 0.0)
    tile_counts = jnp.sum(cnt, axis=1, keepdims=True)

    @pl.when(ph == 0)
    def _():
        @pl.when(t == 0)
        def _():
            cnt_ref[...] = jnp.zeros_like(cnt_ref)

        cnt_ref[...] += tile_counts

    @pl.when(ph == 1)
    def _():
        @pl.when(t == 0)
        def _():
            counts = cnt_ref[...]
            nblk = jnp.floor((counts + (EXP_BLOCK - 1.0)) * (1.0 / EXP_BLOCK))
            r = lax.broadcasted_iota(jnp.int32, (MOE_EXPERTS, MOE_EXPERTS), 0)
            c = lax.broadcasted_iota(jnp.int32, (MOE_EXPERTS, MOE_EXPERTS), 1)
            nblk_row = jnp.sum(jnp.where(r == c, nblk, 0.0), axis=0, keepdims=True)
            bstart = jnp.sum(jnp.where(c < r, nblk_row, 0.0), axis=1, keepdims=True)
            base_ref[...] = bstart * EXP_BLOCK
            bend = bstart + nblk
            bidx = lax.broadcasted_iota(jnp.int32, (1, META_LANES), 1).astype(F32)
            blk_e = jnp.minimum(jnp.sum(jnp.where(bidx >= bend, 1.0, 0.0), axis=0, keepdims=True),
                                MOE_EXPERTS - 1.0)
            erow = lax.broadcasted_iota(jnp.int32, (MOE_EXPERTS, META_LANES), 0).astype(F32)
            mine = erow == blk_e
            cnt_b = jnp.sum(jnp.where(mine, counts, 0.0), axis=0, keepdims=True)
            start_b = jnp.sum(jnp.where(mine, bstart, 0.0), axis=0, keepdims=True)
            valid = jnp.clip(cnt_b - (bidx - start_b) * EXP_BLOCK, 0.0, float(EXP_BLOCK))
            n_blk = jnp.sum(nblk, axis=0, keepdims=True)
            end_b = jnp.sum(jnp.where(mine, bend, 0.0), axis=0, keepdims=True)
            nxt = jnp.minimum(jnp.sum(jnp.where(end_b >= bend, 1.0, 0.0), axis=0, keepdims=True),
                              MOE_EXPERTS - 1.0)
            nxt = jnp.where(end_b < n_blk, nxt, -1.0)
            mrow = lax.broadcasted_iota(jnp.int32, (8, META_LANES), 0)
            meta = jnp.where(mrow == 0, blk_e, jnp.where(mrow == 1, valid, jnp.where(
                mrow == 2, n_blk, jnp.where(mrow == 3, nxt, 0.0))))
            meta_ref[...] = meta.astype(jnp.int32)

        lane = ROUTER_LANES
        before = (lax.broadcasted_iota(jnp.int32, (lane, lane), 0)
                  < lax.broadcasted_iota(jnp.int32, (lane, lane), 1))
        tri = jnp.where(before, 1.0, 0.0).astype(BF16)
        run = base_ref[...]
        d1, d2 = [], []
        for k in range(tm // lane):
            piece = slice(k * lane, (k + 1) * lane)
            ck = cnt[:, piece]
            pos = run + jnp.dot(ck.astype(BF16), tri, preferred_element_type=F32)
            d1.append(jnp.sum(jnp.where(sel1[:, piece], pos, 0.0), axis=0, keepdims=True))
            d2.append(jnp.sum(jnp.where(sel2[:, piece], pos, 0.0), axis=0, keepdims=True))
            run = run + jnp.sum(ck, axis=1, keepdims=True)
        dest_ref[...] = jnp.concatenate(
            [jnp.concatenate(d1, axis=1), jnp.concatenate(d2, axis=1)], axis=0).astype(jnp.int32)
        base_ref[...] = run
        g_w = 1.0 / jnp.sum(jnp.exp(gl - gmax), axis=0, keepdims=True)
        e21 = jnp.exp(m2 - m1)
        p1 = 1.0 / (1.0 + e21)
        rid = lax.broadcasted_iota(jnp.int32, (ROUTER_LANES, tm), 0)
        gt = jnp.where(rid == 0, g_w * p1, jnp.where(rid == 1, g_w * (e21 * p1), 0.0))
        gates_ref[...] = gt.T


def _route_call(logits):
    return pl.pallas_call(
        _route_kernel,
        grid=(2, N_ROUTE_TILES),
        in_specs=[pl.BlockSpec((ROUTE_TM, ROUTER_LANES), lambda ph, t: (t, 0))],
        out_specs=[pl.BlockSpec((MOE_TOPK, ROUTE_TM), lambda ph, t: (0, t * ph)),
                   pl.BlockSpec((ROUTE_TM, ROUTER_LANES), lambda ph, t: (t * ph, 0)),
                   pl.BlockSpec((8, META_LANES), lambda ph, t: (0, 0))],
        out_shape=[jax.ShapeDtypeStruct((MOE_TOPK, T_ALL), jnp.int32),
                   jax.ShapeDtypeStruct((T_ALL, ROUTER_LANES), F32),
                   jax.ShapeDtypeStruct((8, META_LANES), jnp.int32)],
        scratch_shapes=[pltpu.VMEM((MOE_EXPERTS, 1), F32), pltpu.VMEM((MOE_EXPERTS, 1), F32)],
        compiler_params=_params("arbitrary", "arbitrary"),
        name="route",
    )(logits)


def _sc_mesh():
    return plsc.VectorSubcoreMesh(core_axis_name="c", subcore_axis_name="s")


def _sc_token_offset(j):
    wid = lax.axis_index("s") * SC_CORES + lax.axis_index("c")
    return pl.multiple_of(wid * SC_ROWS_PER_WORKER + j * SC_CHUNK, 8)


def _dispatch_body(h_hbm, d0_hbm, d1_hbm, out_hbm, i0_v, i1_v, rows_v, sem0, sem1, sem2):
    @pl.loop(0, SC_ROWS_PER_WORKER // SC_CHUNK)
    def _(j):
        off = _sc_token_offset(j)
        c0 = pltpu.async_copy(d0_hbm.at[pl.ds(off, SC_CHUNK)], i0_v, sem0)
        c1 = pltpu.async_copy(d1_hbm.at[pl.ds(off, SC_CHUNK)], i1_v, sem1)
        c2 = pltpu.async_copy(h_hbm.at[pl.ds(off, SC_CHUNK)], rows_v, sem2)
        c0.wait()
        c1.wait()
        c2.wait()
        s0 = pltpu.async_copy(rows_v, out_hbm.at[i0_v], sem0)
        s1 = pltpu.async_copy(rows_v, out_hbm.at[i1_v], sem1)
        s0.wait()
        s1.wait()


def _dispatch_call(h2, dest0, dest1):
    return pl.kernel(
        _dispatch_body,
        out_type=jax.ShapeDtypeStruct((P_ROWS, DP), U32),
        mesh=_sc_mesh(),
        scratch_types=[pltpu.VMEM((SC_CHUNK,), jnp.int32), pltpu.VMEM((SC_CHUNK,), jnp.int32),
                       pltpu.VMEM((SC_CHUNK, DP), U32),
                       pltpu.SemaphoreType.DMA, pltpu.SemaphoreType.DMA, pltpu.SemaphoreType.DMA],
        name="moe_dispatch",
    )(h2, dest0, dest1)


def _combine_body(yb_hbm, d0_hbm, d1_hbm, o0_hbm, o1_hbm, i0_v, i1_v, r0_v, r1_v, sem0, sem1):
    @pl.loop(0, SC_ROWS_PER_WORKER // SC_CHUNK)
    def _(j):
        off = _sc_token_offset(j)
        c0 = pltpu.async_copy(d0_hbm.at[pl.ds(off, SC_CHUNK)], i0_v, sem0)
        c1 = pltpu.async_copy(d1_hbm.at[pl.ds(off, SC_CHUNK)], i1_v, sem1)
        c0.wait()
        c1.wait()
        g0 = pltpu.async_copy(yb_hbm.at[i0_v], r0_v, sem0)
        g1 = pltpu.async_copy(yb_hbm.at[i1_v], r1_v, sem1)
        g0.wait()
        g1.wait()
        w0 = pltpu.async_copy(r0_v, o0_hbm.at[pl.ds(off, SC_CHUNK)], sem0)
        w1 = pltpu.async_copy(r1_v, o1_hbm.at[pl.ds(off, SC_CHUNK)], sem1)
        w0.wait()
        w1.wait()


def _combine_call(yb, dest0, dest1):
    out = jax.ShapeDtypeStruct((T_ALL, DP), U32)
    return pl.kernel(
        _combine_body,
        out_type=(out, out),
        mesh=_sc_mesh(),
        scratch_types=[pltpu.VMEM((SC_CHUNK,), jnp.int32), pltpu.VMEM((SC_CHUNK,), jnp.int32),
                       pltpu.VMEM((SC_CHUNK, DP), U32), pltpu.VMEM((SC_CHUNK, DP), U32),
                       pltpu.SemaphoreType.DMA, pltpu.SemaphoreType.DMA],
        name="moe_combine",
    )(yb, dest0, dest1)


def _moe_rows(h2, dest, meta, w1, w3, w2, layer):
    dest0, dest1 = dest[0], dest[1]
    xb = _dispatch_call(h2, dest0, dest1)
    yb = _expert_call(meta[0, :N_EXP_BLOCKS], meta[1, :N_EXP_BLOCKS], meta[2, :1], meta[3, :N_EXP_BLOCKS],
                      xb, w1, w3, w2, layer)
    return _combine_call(yb, dest0, dest1)


def _final_kernel(is_prompt, x_ref, yg0_ref, yg1_ref, gates_ref, modp_ref, mods_ref, gain_ref, o_ref):
    x = _add_moe(x_ref[...], yg0_ref, yg1_ref, gates_ref, _tile_mod((modp_ref, mods_ref), is_prompt))
    o_ref[...] = _rms(x) * gain_ref[...]


def _final_call(x, prev, gain, row0, n_rows):
    rows = ROWS_WIDE
    tile0 = row0 // rows
    is_prompt = row0 < T_PROMPT
    assert row0 + n_rows <= T_PROMPT or not is_prompt

    def tile(width):
        return pl.BlockSpec((rows, width), lambda i: (tile0 + i, 0))

    mod_specs = [pl.BlockSpec(s.block_shape, lambda i, m=s.index_map: m(tile0 + i)) for s in _mod_specs(rows)]
    return pl.pallas_call(
        functools.partial(_final_kernel, is_prompt),
        grid=(n_rows // rows,),
        in_specs=[tile(D), tile(DP), tile(DP), tile(ROUTER_LANES), *mod_specs, _resident((1, D))],
        out_specs=pl.BlockSpec((rows, D), lambda i: (i, 0)),
        out_shape=jax.ShapeDtypeStruct((n_rows, D), F32),
        compiler_params=_params("parallel"),
        name="final_norm",
    )(x, *prev, gain)


def _rope_tables():
    pos = np.concatenate([np.tile(np.arange(SEQ), BATCH),
                          np.tile(PAST_LEN + np.arange(DEC_SEQ), DEC_BATCH)]).astype(np.float32)
    inv = (ROPE_BASE ** (-np.arange(ROPE_HALF, dtype=np.float32) / ROPE_HALF)).astype(np.float32)
    ang = (pos[:, None] * inv[None, :]).astype(np.float32).astype(np.float64)
    return jnp.asarray(np.cos(ang), F32), jnp.asarray(np.sin(ang), F32)


def kernel(x_prompt, x_sample, c_prompt, c_sample, state_ret, ada_w, ada_b, norm1_g, norm2_g, ret_w_in,
           ret_w_out, gm_w_in, gm_b_in, gm_ln_g, gm_ln_b, gm_w_s, gm_b_s, gm_w_out, gm_b_out, moe_w_rg,
           moe_b_rg, moe_w_re, moe_b_re, moe_w1, moe_w3, moe_w2, final_g):
    x = (x_prompt.reshape(T_PROMPT, D), x_sample.reshape(T_SAMPLE, D))
    c_all = jnp.concatenate([c_prompt, c_sample], axis=0)
    cos, sin = _rope_tables()
    dec = _ret_row_scales()

    mod_all = _ada_call(c_all, ada_w, ada_b).reshape(DEPTH, N_SEQ, 6, D)

    def layer_params(i):
        j = i // 2
        mod = (mod_all[i, :BATCH], mod_all[i, BATCH:])
        w_r = jnp.pad(jnp.concatenate([moe_w_re[i], moe_w_rg[i]], axis=1),
                      ((0, 0), (0, ROUTER_LANES - MOE_GROUPS - MOE_EXPERTS)))
        w_r_hi = w_r.astype(BF16)
        w_r_lo = (w_r - w_r_hi.astype(F32)).astype(BF16)
        w_r = jnp.concatenate([w_r_hi, w_r_lo], axis=1)
        b_r = jnp.pad(jnp.concatenate([moe_b_re[i].reshape(-1), moe_b_rg[i]]),
                      (0, ROUTER_LANES - MOE_GROUPS - MOE_EXPERTS)).reshape(1, ROUTER_LANES)
        if i % 2 == 0:
            weights = [ret_w_in[j:j + 1].astype(BF16), ret_w_out[j:j + 1].astype(BF16)]
        else:
            weights = [gm_w_in[j:j + 1].astype(BF16), gm_w_out[j:j + 1].astype(BF16),
                       *_gm_block_diag(gm_w_s[j], gm_b_s[j])]
        return mod, w_r, b_r, weights

    ret_prompt = ret_sample = gm_sample = None
    prev = None
    for i in range(DEPTH):
        j = i // 2
        mod, w_r, b_r, weights = layer_params(i)
        g1 = norm1_g[i].reshape(1, D)
        g2 = norm2_g[i].reshape(1, D)
        if i % 2 == 0:
            w_in_b, w_out_b = weights
            x, p = _ret_proj_call(x, prev, mod, g1, w_in_b, cos, sin, dec)
            y_p, ret_prompt = _ret_core_call(p, None, ret_prompt, j, BATCH, SEQ, RET_CHUNK_PROMPT, 0)
            y_s, ret_sample = _ret_core_call(p, state_ret, ret_sample, j, DEC_BATCH, DEC_SEQ,
                                             RET_CHUNK_SAMPLE, T_PROMPT)
            x, h2, logits = _ret_out_call(y_p, y_s, w_out_b, x, mod, g2, w_r, b_r)
        else:
            w_in_b, w_out_b, ws_bd, bs_bd = weights
            x, u, v, gm_sample = _gm_proj_call(x, prev, mod, g1, w_in_b, j,
                                               gm_b_in[j].reshape(1, GM_FFN), gm_ln_g[j].reshape(1, GM_HALF),
                                               gm_ln_b[j].reshape(1, GM_HALF), gm_sample)
            x, h2, logits = _gm_out_call(u, v, ws_bd, bs_bd, w_out_b,
                                         gm_b_out[j].reshape(1, D), x, mod, g2, w_r, b_r)
        dest, gates, meta = _route_call(logits)
        yg0, yg1 = _moe_rows(h2, dest, meta, moe_w1, moe_w3, moe_w2, i)
        prev = (yg0, yg1, gates, *mod)

    fg = final_g.reshape(1, D)
    y_prompt = _final_call(x, prev, fg, 0, T_PROMPT).reshape(BATCH, SEQ, D)
    y_sample = _final_call(x, prev, fg, T_PROMPT, T_SAMPLE).reshape(DEC_BATCH, DEC_SEQ, D)
    return (y_prompt, y_sample, ret_prompt, ret_sample,
            gm_sample.reshape(N_GM, DEC_BATCH, DEC_SEQ, GM_HALF))
```

```python
import functools

import numpy as np
import jax
import jax.numpy as jnp
from jax import lax
from jax.experimental import pallas as pl
from jax.experimental.pallas import tpu as pltpu
from jax.experimental.pallas import tpu_sc as plsc

F32 = jnp.float32
BF16 = jnp.bfloat16
U32 = jnp.uint32

D = 1024
BATCH, SEQ = 4, 4096
DEC_BATCH, DEC_SEQ = 16, 64
PAST_LEN = 4096
DEPTH = 4
N_RET = (DEPTH + 1) // 2
N_GM = DEPTH // 2
N_SEQ = BATCH + DEC_BATCH

RET_HEADS, RET_DK, RET_DV = 4, 256, 512
RET_QK = RET_HEADS * RET_DK
RET_V = RET_HEADS * RET_DV
RET_IN = 2 * RET_QK + 2 * RET_V
ROPE_BASE = 10000.0
ROPE_HALF = RET_DK // 2

GM_FFN = 6 * D
GM_HALF = GM_FFN // 2
GM_GROUPS = 4
GM_GDIM = GM_HALF // GM_GROUPS
GM_CHUNK = 128

MOE_GROUPS, MOE_PER_GROUP = 4, 8
MOE_EXPERTS = MOE_GROUPS * MOE_PER_GROUP
MOE_TOPK = 2
MOE_HIDDEN = 512
EPS = 1e-6

GROUP = DEC_SEQ
T_PROMPT = BATCH * SEQ
T_SAMPLE = DEC_BATCH * DEC_SEQ
T_ALL = T_PROMPT + T_SAMPLE
N_GROUPS = T_ALL // GROUP
ROWS_WIDE = 512
ROWS_GM_PROJ = 256

RET_CHUNK_PROMPT = 256
RET_CHUNK_SAMPLE = DEC_SEQ

GM_MIX = 256

EXP_BLOCK = 512
EXP_ROW_STEP = 128
N_ASSIGN = T_ALL * MOE_TOPK
N_EXP_BLOCKS = -(-(N_ASSIGN + MOE_EXPERTS * (EXP_BLOCK - 1)) // EXP_BLOCK)
P_ROWS = N_EXP_BLOCKS * EXP_BLOCK
ROUTER_LANES = 128
ROUTE_TM = 1024
N_ROUTE_TILES = T_ALL // ROUTE_TM
META_LANES = 256
assert META_LANES >= N_EXP_BLOCKS

DP = D // 2
SC_CORES, SC_SUBCORES = 2, 16
SC_WORKERS = SC_CORES * SC_SUBCORES
SC_ROWS_PER_WORKER = T_ALL // SC_WORKERS
SC_CHUNK = 32
assert SC_ROWS_PER_WORKER % SC_CHUNK == 0 and SC_CHUNK % 8 == 0

V7X_VMEM_LIMIT_BYTES = 56 * 1024 * 1024


def _params(*sem):
    return pltpu.CompilerParams(dimension_semantics=sem, vmem_limit_bytes=V7X_VMEM_LIMIT_BYTES)


def _resident(shape):
    nd = len(shape)
    return pl.BlockSpec(shape, lambda *_: (0,) * nd, pipeline_mode=pl.Buffered(1))


def _rms(x):
    return x * lax.rsqrt(jnp.mean(x * x, axis=-1, keepdims=True) + EPS)


def _silu(x):
    return x * jax.nn.sigmoid(x)


def _per_group(x2d, fn):
    rows = x2d.shape[0]
    return fn(x2d.reshape(rows // GROUP, GROUP, D)).reshape(rows, D)


def _tile_mod(mod_refs, is_prompt):
    modp_ref, mods_ref = mod_refs
    return jnp.where(is_prompt, jnp.broadcast_to(modp_ref[...], mods_ref.shape), mods_ref[...])


def _norm_mod(x, gain_ref, mod, shift_idx):
    y = _rms(x) * gain_ref[...]
    scale = mod[:, shift_idx + 1:shift_idx + 2, :]
    shift = mod[:, shift_idx:shift_idx + 1, :]
    return _per_group(y, lambda y3: y3 * (1.0 + scale) + shift)


def _pack_bf16_pairs(x):
    lo = lax.bitcast_convert_type(x[:, :DP].astype(BF16).astype(F32), U32)
    hi = lax.bitcast_convert_type(x[:, DP:].astype(BF16).astype(F32), U32)
    return (lo >> 16) | (hi & U32(0xFFFF0000))


def _unpack_bf16_pairs(w):
    lo = lax.bitcast_convert_type(w << 16, F32)
    hi = lax.bitcast_convert_type(w & U32(0xFFFF0000), F32)
    return jnp.concatenate([lo, hi], axis=1)


def _add_moe(x, yg0_ref, yg1_ref, gates_ref, mod_prev):
    g = gates_ref[...]
    y = g[:, 0:1] * _unpack_bf16_pairs(yg0_ref[...]) + g[:, 1:2] * _unpack_bf16_pairs(yg1_ref[...])
    gate2 = mod_prev[:, 5:6, :]
    return x + _per_group(y, lambda y3: y3 * gate2)


ADA_TN = 1536


def _ada_kernel(c_ref, w_ref, b_ref, o_ref):
    c = c_ref[...]
    s = _silu(c).astype(BF16)
    o_ref[0] = jnp.dot(s, w_ref[0].astype(BF16), preferred_element_type=F32) + b_ref[0]


def _ada_call(c_all, ada_w, ada_b):
    return pl.pallas_call(
        _ada_kernel,
        grid=(DEPTH, 6 * D // ADA_TN),
        in_specs=[
            pl.BlockSpec((N_SEQ, D), lambda i, j: (0, 0)),
            pl.BlockSpec((1, D, ADA_TN), lambda i, j: (i, 0, j)),
            pl.BlockSpec((1, 1, ADA_TN), lambda i, j: (i, 0, j)),
        ],
        out_specs=pl.BlockSpec((1, N_SEQ, ADA_TN), lambda i, j: (i, 0, j)),
        out_shape=jax.ShapeDtypeStruct((DEPTH, N_SEQ, 6 * D), F32),
        compiler_params=_params("parallel", "parallel"),
        name="ada_modulation",
    )(c_all, ada_w, ada_b.reshape(DEPTH, 1, 6 * D))


def _n_tiles(rows):
    return T_ALL // rows


def _n_prompt_tiles(rows):
    return T_PROMPT // rows


def _row_spec(rows, width):
    return pl.BlockSpec((rows, width), lambda i: (i, 0))


def _mod_specs(rows):
    npt = _n_prompt_tiles(rows)
    return [pl.BlockSpec((1, 6, D), lambda i: (jnp.minimum(i * rows // SEQ, BATCH - 1), 0, 0)),
            pl.BlockSpec((rows // DEC_SEQ, 6, D), lambda i: (jnp.maximum(i - npt, 0), 0, 0))]


def _prev_specs(rows):
    return [_row_spec(rows, DP), _row_spec(rows, DP), _row_spec(rows, ROUTER_LANES)] + _mod_specs(rows)


def _prompt_rows_spec(rows, width):
    last = _n_prompt_tiles(rows) - 1
    return pl.BlockSpec((rows, width), lambda i: (jnp.minimum(i, last), 0))


def _sample_rows_spec(rows, width):
    npt = _n_prompt_tiles(rows)
    return pl.BlockSpec((rows, width), lambda i: (jnp.maximum(i - npt, 0), 0))


def _ret_proj_kernel(has_prev, n_prompt_tiles, *refs):
    is_prompt = pl.program_id(0) < n_prompt_tiles
    if has_prev:
        x_ref, yg0_ref, yg1_ref, gates_ref = refs[:4]
        x = _add_moe(x_ref[...], yg0_ref, yg1_ref, gates_ref, _tile_mod(refs[4:6], is_prompt))
        refs = refs[6:]
    else:
        x = jnp.where(is_prompt, refs[0][...], refs[1][...])
        refs = refs[2:]
    mod = _tile_mod(refs[:2], is_prompt)
    gain_ref, w_ref, cos_ref, sin_ref, dec_ref, xo_ref, p_ref = refs[2:]
    xo_ref[...] = x
    hb = _norm_mod(x, gain_ref, mod, 0).astype(BF16)
    cos = cos_ref[...]
    sin = sin_ref[...]
    dec = dec_ref[...]
    for j in range(2 * RET_HEADS):
        lo = j * RET_DK
        acc = jnp.dot(hb, w_ref[0, :, lo:lo + RET_DK], preferred_element_type=F32)
        x1 = acc[:, :ROPE_HALF]
        x2 = acc[:, ROPE_HALF:]
        scale = dec[:, j:j + 1]
        p_ref[:, lo:lo + ROPE_HALF] = ((x1 * cos - x2 * sin) * scale).astype(BF16)
        p_ref[:, lo + ROPE_HALF:lo + RET_DK] = ((x1 * sin + x2 * cos) * scale).astype(BF16)
    for j in range(2 * RET_HEADS):
        lo = 2 * RET_QK + j * RET_DV
        acc = jnp.dot(hb, w_ref[0, :, lo:lo + RET_DV], preferred_element_type=F32)
        if j >= RET_HEADS:
            acc = _silu(acc)
        p_ref[:, lo:lo + RET_DV] = acc.astype(BF16)


def _ret_proj_call(x, prev, mod, gain, w_in, cos, sin, dec):
    rows = ROWS_WIDE
    has_prev = prev is not None
    if has_prev:
        in_specs = [_row_spec(rows, D)] + _prev_specs(rows)
        args = [x] + list(prev)
    else:
        in_specs = [_prompt_rows_spec(rows, D), _sample_rows_spec(rows, D)]
        args = list(x)
    in_specs += _mod_specs(rows) + [
        _resident((1, D)), _resident((1, D, RET_IN)),
        _row_spec(rows, ROPE_HALF), _row_spec(rows, ROPE_HALF), _row_spec(rows, 2 * RET_HEADS)]
    args += [*mod, gain, w_in, cos, sin, dec]
    return pl.pallas_call(
        functools.partial(_ret_proj_kernel, has_prev, _n_prompt_tiles(rows)),
        grid=(_n_tiles(rows),),
        in_specs=in_specs,
        out_specs=[_row_spec(rows, D), _row_spec(rows, RET_IN)],
        out_shape=[jax.ShapeDtypeStruct((T_ALL, D), F32), jax.ShapeDtypeStruct((T_ALL, RET_IN), BF16)],
        compiler_params=_params("parallel"),
        name="ret_proj",
    )(*args)


def _ret_core_kernel(has_s0, n_chunks, layer, *refs):
    refs = list(refs)
    p_ref = refs.pop(0)
    s0_ref = refs.pop(0) if has_s0 else None
    causal_ref, cd_ref = refs[:2]
    y_ref, so_ref, s_ref = refs[-3:]
    c = pl.program_id(1)

    @pl.when(c == 0)
    def _():
        if has_s0:
            s_ref[...] = s0_ref[0, 0]
        else:
            s_ref[...] = jnp.zeros_like(s_ref)

    for h in range(RET_HEADS):
        qb = p_ref[:, h * RET_DK:(h + 1) * RET_DK]
        kb = p_ref[:, RET_QK + h * RET_DK:RET_QK + (h + 1) * RET_DK]
        vb = p_ref[:, 2 * RET_QK + h * RET_DV:2 * RET_QK + (h + 1) * RET_DV]
        gb = p_ref[:, 2 * RET_QK + RET_V + h * RET_DV:2 * RET_QK + RET_V + (h + 1) * RET_DV]
        scores = lax.dot_general(qb, kb, (((1,), (1,)), ((), ())), preferred_element_type=F32)
        scores = scores * causal_ref[...]
        s_old = s_ref[h]
        o = (jnp.dot(scores.astype(BF16), vb, preferred_element_type=F32)
             + jnp.dot(qb, s_old.astype(BF16), preferred_element_type=F32))
        s_ref[h] = cd_ref[h][:, 0:1] * (s_old + lax.dot_general(
            kb, vb, (((0,), (0,)), ((), ())), preferred_element_type=F32))
        y_ref[:, h * RET_DV:(h + 1) * RET_DV] = (gb.astype(F32) * _rms(o)).astype(BF16)

    @pl.when(c == n_chunks - 1)
    def _():
        so_ref[0, 0] = s_ref[...]
        if layer == 0:
            for later in range(1, N_RET):
                so_ref[later, 0] = jnp.zeros_like(s_ref)


def _ret_log_gamma():
    return np.log1p(-np.exp2(-5.0 - np.arange(RET_HEADS, dtype=np.float64)))


def _ret_chunk_tables(cl):
    idx = np.arange(cl)
    causal = (idx[:, None] >= idx[None, :]).astype(np.float32)
    cd = np.broadcast_to(np.exp(_ret_log_gamma() * cl)[:, None, None], (RET_HEADS, 1, 128))
    return jnp.asarray(causal, F32), jnp.asarray(cd, F32)


def _ret_row_scales():
    c = np.concatenate([np.arange(T_PROMPT) % RET_CHUNK_PROMPT,
                        np.arange(T_SAMPLE) % RET_CHUNK_SAMPLE]).astype(np.float64)
    e = (c[:, None] + 1.0) * _ret_log_gamma()[None, :]
    return jnp.asarray(np.concatenate([np.exp(e), np.exp(-e) * RET_DK ** -0.5], axis=1), F32)


def _ret_core_call(p, s0, states, layer, n_seq, seq_len, cl, row0):
    has_s0 = s0 is not None
    n_chunks = seq_len // cl
    rb0 = row0 // cl
    state = (RET_HEADS, RET_DK, RET_DV)
    in_specs = [pl.BlockSpec((cl, RET_IN), lambda b, c: (rb0 + b * n_chunks + c, 0))]
    args = [p]
    if has_s0:
        in_specs.append(pl.BlockSpec((1, 1) + state, lambda b, c: (layer, b, 0, 0, 0)))
        args.append(s0)
    in_specs += [_resident((cl, cl)), _resident((RET_HEADS, 1, 128))]
    args += list(_ret_chunk_tables(cl))
    if layer == 0:
        assert states is None
        state_spec = pl.BlockSpec((N_RET, 1) + state, lambda b, c: (0, b, 0, 0, 0))
        aliases = {}
    else:
        in_specs.append(pl.BlockSpec(memory_space=pl.ANY))
        args.append(states)
        state_spec = pl.BlockSpec((1, 1) + state, lambda b, c: (layer, b, 0, 0, 0))
        aliases = {len(args) - 1: 1}
    return pl.pallas_call(
        functools.partial(_ret_core_kernel, has_s0, n_chunks, layer),
        grid=(n_seq, n_chunks),
        in_specs=in_specs,
        out_specs=[pl.BlockSpec((cl, RET_V), lambda b, c: (b * n_chunks + c, 0)), state_spec],
        out_shape=[jax.ShapeDtypeStruct((n_seq * seq_len, RET_V), BF16),
                   jax.ShapeDtypeStruct((N_RET, n_seq) + state, F32)],
        scratch_shapes=[pltpu.VMEM(state, F32)],
        input_output_aliases=aliases,
        compiler_params=_params("parallel", "arbitrary"),
        name="ret_core",
    )(*args)


def _residual_router(acc, x_ref, mod, gain_ref, wr_ref, br_ref, xo_ref, h2_ref, lg_ref):
    gate1 = mod[:, 2:3, :]
    xn = x_ref[...] + _per_group(acc, lambda a3: a3 * gate1)
    xo_ref[...] = xn
    h2 = _norm_mod(xn, gain_ref, mod, 3)
    h2_ref[...] = _pack_bf16_pairs(h2)
    hh = jnp.dot(h2.astype(BF16), wr_ref[...], preferred_element_type=F32)
    lg_ref[...] = hh[:, :ROUTER_LANES] + hh[:, ROUTER_LANES:] + br_ref[...]


def _mix_out_specs(rows):
    return [_row_spec(rows, D), _row_spec(rows, DP), _row_spec(rows, ROUTER_LANES)]


_MIX_OUT_SHAPE = [
    jax.ShapeDtypeStruct((T_ALL, D), F32),
    jax.ShapeDtypeStruct((T_ALL, DP), U32),
    jax.ShapeDtypeStruct((T_ALL, ROUTER_LANES), F32),
]


def _router_specs():
    return [_resident((D, 2 * ROUTER_LANES)), _resident((1, ROUTER_LANES))]


def _ret_out_kernel(n_prompt_tiles, yp_ref, ys_ref, w_ref, x_ref, modp_ref, mods_ref, gain_ref, wr_ref,
                    br_ref, xo_ref, h2_ref, lg_ref):
    is_prompt = pl.program_id(0) < n_prompt_tiles
    yin = jnp.where(is_prompt, yp_ref[...], ys_ref[...])
    acc = jnp.dot(yin, w_ref[0], preferred_element_type=F32)
    mod = _tile_mod((modp_ref, mods_ref), is_prompt)
    _residual_router(acc, x_ref, mod, gain_ref, wr_ref, br_ref, xo_ref, h2_ref, lg_ref)


def _ret_out_call(y_prompt, y_sample, w_out, x, mod, gain, w_r, b_r):
    rows = ROWS_WIDE
    return pl.pallas_call(
        functools.partial(_ret_out_kernel, _n_prompt_tiles(rows)),
        grid=(_n_tiles(rows),),
        in_specs=[_prompt_rows_spec(rows, RET_V), _sample_rows_spec(rows, RET_V),
                  _resident((1, RET_V, D)), _row_spec(rows, D), *_mod_specs(rows),
                  _resident((1, D))] + _router_specs(),
        out_specs=_mix_out_specs(rows),
        out_shape=_MIX_OUT_SHAPE,
        compiler_params=_params("parallel"),
        name="ret_out",
    )(y_prompt, y_sample, w_out, x, *mod, gain, w_r, b_r)


GM_TN = 512


_GELU_C = float(np.sqrt(2.0 / np.pi))


def _gelu_tanh(x):
    hx = 0.5 * x
    return hx * jnp.tanh(x * (_GELU_C + (_GELU_C * 0.044715) * (x * x))) + hx


def _gm_proj_kernel(n_tiles, n_prompt_tiles, layer, *refs):
    x_ref, yg0_ref, yg1_ref, gates_ref = refs[:4]
    gain_ref, w_ref, b_ref, lg_ref, lb_ref = refs[8:13]
    xo_ref, u_ref, v_ref, vs_ref, hb_s, vraw_s, stat_s, sum_s = refs[-8:]
    i = pl.program_id(0)
    slot_a = i % 2
    slot_b = 1 - slot_a
    rows = x_ref.shape[0]

    @pl.when(i == 0)
    def _():
        hb_s[1] = jnp.zeros((rows, D), BF16)
        vraw_s[1] = jnp.zeros((rows, GM_HALF), BF16)
        stat_s[1] = jnp.zeros((rows, 256), F32)

    def add_row(x, row_ref, lo, width):
        x3 = x.reshape(rows // 8, 8, width) + row_ref[:, lo:lo + width]
        return x3.reshape(rows, width)

    def mul_row(x, row_ref, lo, width):
        x3 = x.reshape(rows // 8, 8, width) * row_ref[:, lo:lo + width]
        return x3.reshape(rows, width)

    def gelu_chunk(slot, lo):
        z = jnp.dot(hb_s[slot], w_ref[0, :, lo:lo + GM_TN], preferred_element_type=F32)
        return _gelu_tanh(add_row(z, b_ref, lo, GM_TN).astype(BF16))

    def stage_b_chunk(lo):
        u_ref[:, lo:lo + GM_TN] = gelu_chunk(slot_b, lo)
        for k in range(lo, lo + GM_TN, 128):
            vk = vraw_s[slot_b, :, k:k + 128].astype(F32) * stat_s[slot_b, :, 0:128] + stat_s[slot_b, :, 128:256]
            vn = add_row(mul_row(vk, lg_ref, k, 128), lb_ref, k, 128)
            v_ref[:, k:k + 128] = vn.astype(BF16)
            vs_ref[0, :, k:k + 128] = vn

    chunks = list(range(0, GM_HALF, GM_TN))
    lead = 2
    for lo in chunks[:lead]:
        stage_b_chunk(lo)

    a_is_prompt = jnp.minimum(i, n_tiles - 1) < n_prompt_tiles
    x = _add_moe(x_ref[...], yg0_ref, yg1_ref, gates_ref, _tile_mod(refs[4:6], a_is_prompt))
    xo_ref[...] = x
    hb = _norm_mod(x, gain_ref, _tile_mod(refs[6:8], a_is_prompt), 0).astype(BF16)
    hb_s[slot_a] = hb

    for n, lo in enumerate(chunks):
        gz = gelu_chunk(slot_a, GM_HALF + lo)
        vraw_s[slot_a, :, lo:lo + GM_TN] = gz
        gf = gz.astype(F32)
        pieces = [gf[:, k:k + 128] for k in range(0, GM_TN, 128)]
        t1 = functools.reduce(lambda p, q: p + q, pieces)
        t2 = functools.reduce(lambda p, q: p + q, [p * p for p in pieces])
        if n == 0:
            sum_s[:, 0:128] = t1
            sum_s[:, 128:256] = t2
        else:
            sum_s[:, 0:128] += t1
            sum_s[:, 128:256] += t2
        if n + lead < len(chunks):
            stage_b_chunk(chunks[n + lead])
    mu = jnp.sum(sum_s[:, 0:128], axis=-1, keepdims=True) * (1.0 / GM_HALF)
    var = jnp.sum(sum_s[:, 128:256], axis=-1, keepdims=True) * (1.0 / GM_HALF) - mu * mu
    rstd = lax.rsqrt(var + EPS)
    stat_s[slot_a, :, 0:128] = jnp.broadcast_to(rstd, (rows, 128))
    stat_s[slot_a, :, 128:256] = jnp.broadcast_to(-mu * rstd, (rows, 128))

    if layer == 0:
        @pl.when(i - 1 >= n_prompt_tiles)
        def _():
            for later in range(1, N_GM):
                vs_ref[later] = jnp.zeros((rows, GM_HALF), F32)


def _gm_proj_call(x, prev, mod, gain, w_in, layer, b_in, ln_g, ln_b, vs_all):
    rows = ROWS_GM_PROJ
    n, npt = _n_tiles(rows), _n_prompt_tiles(rows)

    def stage_a(spec):
        return pl.BlockSpec(spec.block_shape, lambda i, m=spec.index_map: m(jnp.minimum(i, n - 1)))

    def stage_b(spec):
        return pl.BlockSpec(spec.block_shape, lambda i, m=spec.index_map: m(jnp.maximum(i - 1, 0)))

    in_specs = [stage_a(s) for s in [_row_spec(rows, D)] + _prev_specs(rows) + _mod_specs(rows)] + [
        _resident((1, D)), _resident((1, D, GM_FFN)), _resident((8, GM_FFN)),
        _resident((8, GM_HALF)), _resident((8, GM_HALF))]
    rows8 = [jnp.broadcast_to(r.reshape(1, -1), (8, r.size)) for r in (b_in, ln_g, ln_b)]
    args = [x, *prev, *mod, gain, w_in, *rows8]
    if layer == 0:
        assert vs_all is None
        vs_spec = pl.BlockSpec((N_GM, rows, GM_HALF), lambda i: (0, jnp.maximum(i - 1 - npt, 0), 0))
        aliases = {}
    else:
        in_specs.append(pl.BlockSpec(memory_space=pl.ANY))
        args.append(vs_all)
        vs_spec = pl.BlockSpec((1, rows, GM_HALF), lambda i: (layer, jnp.maximum(i - 1 - npt, 0), 0))
        aliases = {len(args) - 1: 3}
    return pl.pallas_call(
        functools.partial(_gm_proj_kernel, n, npt, layer),
        grid=(n + 1,),
        in_specs=in_specs,
        out_specs=[stage_a(_row_spec(rows, D)), stage_b(_row_spec(rows, GM_HALF)),
                   stage_b(_row_spec(rows, GM_HALF)), vs_spec],
        out_shape=[jax.ShapeDtypeStruct((T_ALL, D), F32),
                   jax.ShapeDtypeStruct((T_ALL, GM_HALF), BF16),
                   jax.ShapeDtypeStruct((T_ALL, GM_HALF), BF16),
                   jax.ShapeDtypeStruct((N_GM, T_SAMPLE, GM_HALF), F32)],
        scratch_shapes=[pltpu.VMEM((2, rows, D), BF16), pltpu.VMEM((2, rows, GM_HALF), BF16),
                        pltpu.VMEM((2, rows, 256), F32), pltpu.VMEM((rows, 256), F32)],
        input_output_aliases=aliases,
        compiler_params=_params("arbitrary"),
        name="gm_proj",
    )(*args)


def _gm_out_kernel(n_prompt_tiles, u_ref, v_ref, ws_ref, bs_ref, w_ref, bo_ref, x_ref, modp_ref, mods_ref,
                   gain_ref, wr_ref, br_ref, xo_ref, h2_ref, lg_ref):
    rows = u_ref.shape[0]
    mod = _tile_mod((modp_ref, mods_ref), pl.program_id(0) < n_prompt_tiles)
    pieces = []
    for r0 in range(0, rows, GM_MIX):
        acc = jnp.zeros((GM_MIX, D), F32)
        for g in range(GM_GROUPS):
            lo = g * GM_GDIM
            sp = jnp.dot(ws_ref[0, g], v_ref[r0:r0 + GM_MIX, lo:lo + GM_GDIM],
                         preferred_element_type=F32) + bs_ref[0, g]
            gated = (u_ref[r0:r0 + GM_MIX, lo:lo + GM_GDIM].astype(F32) * sp).astype(BF16)
            acc = acc + jnp.dot(gated, w_ref[0, lo:lo + GM_GDIM, :], preferred_element_type=F32)
        pieces.append(acc)
    acc = jnp.concatenate(pieces, axis=0) + bo_ref[...]
    _residual_router(acc, x_ref, mod, gain_ref, wr_ref, br_ref, xo_ref, h2_ref, lg_ref)


def _gm_block_diag(w_s, b_s):
    mats, biases = [], []
    for cl in (GM_CHUNK, DEC_SEQ):
        tri = jnp.tril(jnp.ones((cl, cl), bool))
        blk = jnp.where(tri[None], w_s[:, :cl, :cl], 0.0)
        reps = GM_MIX // cl
        eye = jnp.eye(reps, dtype=w_s.dtype)
        bd = jnp.einsum("ab,gts->gatbs", eye, blk).reshape(GM_GROUPS, GM_MIX, GM_MIX)
        mats.append(bd)
        biases.append(jnp.tile(b_s[:, :cl], (1, reps))[:, :, None])
    return jnp.stack(mats).astype(BF16), jnp.stack(biases).astype(F32)


def _gm_out_call(u, v, ws_bd, bs_bd, w_out, b_out, x, mod, gain, w_r, b_r):
    rows = ROWS_WIDE
    npt = _n_prompt_tiles(rows)

    def variant(i):
        return jnp.where(i >= npt, 1, 0)

    return pl.pallas_call(
        functools.partial(_gm_out_kernel, npt),
        grid=(_n_tiles(rows),),
        in_specs=[_row_spec(rows, GM_HALF), _row_spec(rows, GM_HALF),
                  pl.BlockSpec((1, GM_GROUPS, GM_MIX, GM_MIX), lambda i: (variant(i), 0, 0, 0)),
                  pl.BlockSpec((1, GM_GROUPS, GM_MIX, 1), lambda i: (variant(i), 0, 0, 0)),
                  _resident((1, GM_HALF, D)), _resident((1, D)), _row_spec(rows, D),
                  *_mod_specs(rows), _resident((1, D))] + _router_specs(),
        out_specs=_mix_out_specs(rows),
        out_shape=_MIX_OUT_SHAPE,
        compiler_params=_params("arbitrary"),
        name="gm_out",
    )(u, v, ws_bd, bs_bd, w_out, b_out, x, *mod, gain, w_r, b_r)


def _expert_kernel(layer, be_ref, bv_ref, nb_ref, nx_ref, xb_ref, w1_hbm, w3_hbm, w2_hbm, yb_ref,
                   st1, st3, st2, w1s, w3s, w2s, sems, slot_ref):
    b = pl.program_id(0)

    def weight_copies(e, slot):
        return (pltpu.make_async_copy(w1_hbm.at[layer, e], st1.at[slot], sems.at[slot, 0]),
                pltpu.make_async_copy(w3_hbm.at[layer, e], st3.at[slot], sems.at[slot, 1]),
                pltpu.make_async_copy(w2_hbm.at[layer, e], st2.at[slot], sems.at[slot, 2]))

    @pl.when(b == 0)
    def _():
        slot_ref[0] = 0
        for cp in weight_copies(be_ref[0], 0):
            cp.start()

    @pl.when(b < nb_ref[0])
    def _():
        prev_e = be_ref[jnp.maximum(b - 1, 0)]

        @pl.when((b == 0) | (be_ref[b] != prev_e))
        def _():
            slot = slot_ref[0]
            for cp in weight_copies(be_ref[b], slot):
                cp.wait()
            w1s[...] = st1[slot].astype(BF16)
            w3s[...] = st3[slot].astype(BF16)
            w2s[...] = st2[slot].astype(BF16)

            @pl.when(nx_ref[b] >= 0)
            def _():
                for cp in weight_copies(nx_ref[b], 1 - slot):
                    cp.start()

            slot_ref[0] = 1 - slot

        valid = bv_ref[b]

        def run_rows(n):
            row = lax.broadcasted_iota(jnp.int32, (n, 1), 0)
            xw = jnp.where(row < valid, xb_ref[0:n], U32(0))
            x = _unpack_bf16_pairs(xw).astype(BF16)
            a = jnp.dot(x, w1s[...], preferred_element_type=F32)
            c = jnp.dot(x, w3s[...], preferred_element_type=F32)
            h = (_silu(a) * c).astype(BF16)
            yb_ref[0:n] = _pack_bf16_pairs(jnp.dot(h, w2s[...], preferred_element_type=F32))
            if n < EXP_BLOCK:
                yb_ref[n:EXP_BLOCK] = jnp.zeros((EXP_BLOCK - n, DP), U32)

        for n in range(EXP_ROW_STEP, EXP_BLOCK + 1, EXP_ROW_STEP):
            pl.when((valid > n - EXP_ROW_STEP) & (valid <= n))(functools.partial(run_rows, n))


def _expert_call(blk_e, blk_valid, n_blk, blk_next, xb, w1, w3, w2, layer):
    def blk(b, be, bv, nb, nx):
        return (jnp.minimum(b, nb[0] - 1), 0)

    up, down = (D, MOE_HIDDEN), (MOE_HIDDEN, D)
    grid_spec = pltpu.PrefetchScalarGridSpec(
        num_scalar_prefetch=4,
        grid=(N_EXP_BLOCKS,),
        in_specs=[pl.BlockSpec((EXP_BLOCK, DP), blk)] + [pl.BlockSpec(memory_space=pl.ANY)] * 3,
        out_specs=pl.BlockSpec((EXP_BLOCK, DP), blk),
        scratch_shapes=[pltpu.VMEM((2,) + up, F32), pltpu.VMEM((2,) + up, F32), pltpu.VMEM((2,) + down, F32),
                        pltpu.VMEM(up, BF16), pltpu.VMEM(up, BF16), pltpu.VMEM(down, BF16),
                        pltpu.SemaphoreType.DMA((2, 3)), pltpu.SMEM((1,), jnp.int32)],
    )
    return pl.pallas_call(
        functools.partial(_expert_kernel, layer),
        grid_spec=grid_spec,
        out_shape=jax.ShapeDtypeStruct((P_ROWS, DP), U32),
        compiler_params=_params("arbitrary"),
        name="experts",
    )(blk_e, blk_valid, n_blk, blk_next, xb, w1, w3, w2)


def _route_kernel(lg_ref, dest_ref, gates_ref, meta_ref, cnt_ref, base_ref):
    ph = pl.program_id(0)
    t = pl.program_id(1)
    tm = ROUTE_TM
    lt = lg_ref[...].T
    el = lt[0:MOE_EXPERTS]
    gl = lt[MOE_EXPERTS:MOE_EXPERTS + 8]
    gidx = lax.broadcasted_iota(jnp.int32, (8, tm), 0)
    neg = jnp.float32(-jnp.inf)
    gl = jnp.where(gidx < MOE_GROUPS, gl, neg)
    gmax = jnp.max(gl, axis=0, keepdims=True)
    grp = jnp.min(jnp.where(gl == gmax, gidx, MOE_GROUPS), axis=0, keepdims=True)
    eidx = lax.broadcasted_iota(jnp.int32, (MOE_EXPERTS, tm), 0)
    els = jnp.where((eidx >> 3) == grp, el, neg)
    m1 = jnp.max(els, axis=0, keepdims=True)
    i1 = jnp.min(jnp.where(els == m1, eidx, MOE_EXPERTS), axis=0, keepdims=True)
    els2 = jnp.where(eidx == i1, neg, els)
    m2 = jnp.max(els2, axis=0, keepdims=True)
    i2 = jnp.min(jnp.where(els2 == m2, eidx, MOE_EXPERTS), axis=0, keepdims=True)
    sel1 = eidx == i1
    sel2 = eidx == i2
    cnt = jnp.where(sel1 | sel2, 1.0, 0.0)
    tile_counts = jnp.sum(cnt, axis=1, keepdims=True)

    @pl.when(ph == 0)
    def _():
        @pl.when(t == 0)
        def _():
            cnt_ref[...] = jnp.zeros_like(cnt_ref)

        cnt_ref[...] += tile_counts

    @pl.when(ph == 1)
    def _():
        @pl.when(t == 0)
        def _():
            counts = cnt_ref[...]
            nblk = jnp.floor((counts + (EXP_BLOCK - 1.0)) * (1.0 / EXP_BLOCK))
            r = lax.broadcasted_iota(jnp.int32, (MOE_EXPERTS, MOE_EXPERTS), 0)
            c = lax.broadcasted_iota(jnp.int32, (MOE_EXPERTS, MOE_EXPERTS), 1)
            nblk_row = jnp.sum(jnp.where(r == c, nblk, 0.0), axis=0, keepdims=True)
            bstart = jnp.sum(jnp.where(c < r, nblk_row, 0.0), axis=1, keepdims=True)
            base_ref[...] = bstart * EXP_BLOCK
            bend = bstart + nblk
            bidx = lax.broadcasted_iota(jnp.int32, (1, META_LANES), 1).astype(F32)
            blk_e = jnp.minimum(jnp.sum(jnp.where(bidx >= bend, 1.0, 0.0), axis=0, keepdims=True),
                                MOE_EXPERTS - 1.0)
            erow = lax.broadcasted_iota(jnp.int32, (MOE_EXPERTS, META_LANES), 0).astype(F32)
            mine = erow == blk_e
            cnt_b = jnp.sum(jnp.where(mine, counts, 0.0), axis=0, keepdims=True)
            start_b = jnp.sum(jnp.where(mine, bstart, 0.0), axis=0, keepdims=True)
            valid = jnp.clip(cnt_b - (bidx - start_b) * EXP_BLOCK, 0.0, float(EXP_BLOCK))
            n_blk = jnp.sum(nblk, axis=0, keepdims=True)
            end_b = jnp.sum(jnp.where(mine, bend, 0.0), axis=0, keepdims=True)
            nxt = jnp.minimum(jnp.sum(jnp.where(end_b >= bend, 1.0, 0.0), axis=0, keepdims=True),
                              MOE_EXPERTS - 1.0)
            nxt = jnp.where(end_b < n_blk, nxt, -1.0)
            mrow = lax.broadcasted_iota(jnp.int32, (8, META_LANES), 0)
            meta = jnp.where(mrow == 0, blk_e, jnp.where(mrow == 1, valid, jnp.where(
                mrow == 2, n_blk, jnp.where(mrow == 3, nxt, 0.0))))
            meta_ref[...] = meta.astype(jnp.int32)

        lane = ROUTER_LANES
        before = (lax.broadcasted_iota(jnp.int32, (lane, lane), 0)
                  < lax.broadcasted_iota(jnp.int32, (lane, lane), 1))
        tri = jnp.where(before, 1.0, 0.0).astype(BF16)
        run = base_ref[...]
        d1, d2 = [], []
        for k in range(tm // lane):
            piece = slice(k * lane, (k + 1) * lane)
            ck = cnt[:, piece]
            pos = run + jnp.dot(ck.astype(BF16), tri, preferred_element_type=F32)
            d1.append(jnp.sum(jnp.where(sel1[:, piece], pos, 0.0), axis=0, keepdims=True))
            d2.append(jnp.sum(jnp.where(sel2[:, piece], pos, 0.0), axis=0, keepdims=True))
            run = run + jnp.sum(ck, axis=1, keepdims=True)
        dest_ref[...] = jnp.concatenate(
            [jnp.concatenate(d1, axis=1), jnp.concatenate(d2, axis=1)], axis=0).astype(jnp.int32)
        base_ref[...] = run
        g_w = 1.0 / jnp.sum(jnp.exp(gl - gmax), axis=0, keepdims=True)
        e21 = jnp.exp(m2 - m1)
        p1 = 1.0 / (1.0 + e21)
        rid = lax.broadcasted_iota(jnp.int32, (ROUTER_LANES, tm), 0)
        gt = jnp.where(rid == 0, g_w * p1, jnp.where(rid == 1, g_w * (e21 * p1), 0.0))
        gates_ref[...] = gt.T


def _route_call(logits):
    return pl.pallas_call(
        _route_kernel,
        grid=(2, N_ROUTE_TILES),
        in_specs=[pl.BlockSpec((ROUTE_TM, ROUTER_LANES), lambda ph, t: (t, 0))],
        out_specs=[pl.BlockSpec((MOE_TOPK, ROUTE_TM), lambda ph, t: (0, t * ph)),
                   pl.BlockSpec((ROUTE_TM, ROUTER_LANES), lambda ph, t: (t * ph, 0)),
                   pl.BlockSpec((8, META_LANES), lambda ph, t: (0, 0))],
        out_shape=[jax.ShapeDtypeStruct((MOE_TOPK, T_ALL), jnp.int32),
                   jax.ShapeDtypeStruct((T_ALL, ROUTER_LANES), F32),
                   jax.ShapeDtypeStruct((8, META_LANES), jnp.int32)],
        scratch_shapes=[pltpu.VMEM((MOE_EXPERTS, 1), F32), pltpu.VMEM((MOE_EXPERTS, 1), F32)],
        compiler_params=_params("arbitrary", "arbitrary"),
        name="route",
    )(logits)


def _sc_mesh():
    return plsc.VectorSubcoreMesh(core_axis_name="c", subcore_axis_name="s")


def _sc_token_offset(j):
    wid = lax.axis_index("s") * SC_CORES + lax.axis_index("c")
    return pl.multiple_of(wid * SC_ROWS_PER_WORKER + j * SC_CHUNK, 8)


def _dispatch_body(h_hbm, d0_hbm, d1_hbm, out_hbm, i0_v, i1_v, rows_v, sem0, sem1, sem2):
    @pl.loop(0, SC_ROWS_PER_WORKER // SC_CHUNK)
    def _(j):
        off = _sc_token_offset(j)
        c0 = pltpu.async_copy(d0_hbm.at[pl.ds(off, SC_CHUNK)], i0_v, sem0)
        c1 = pltpu.async_copy(d1_hbm.at[pl.ds(off, SC_CHUNK)], i1_v, sem1)
        c2 = pltpu.async_copy(h_hbm.at[pl.ds(off, SC_CHUNK)], rows_v, sem2)
        c0.wait()
        c1.wait()
        c2.wait()
        s0 = pltpu.async_copy(rows_v, out_hbm.at[i0_v], sem0)
        s1 = pltpu.async_copy(rows_v, out_hbm.at[i1_v], sem1)
        s0.wait()
        s1.wait()


def _dispatch_call(h2, dest0, dest1):
    return pl.kernel(
        _dispatch_body,
        out_type=jax.ShapeDtypeStruct((P_ROWS, DP), U32),
        mesh=_sc_mesh(),
        scratch_types=[pltpu.VMEM((SC_CHUNK,), jnp.int32), pltpu.VMEM((SC_CHUNK,), jnp.int32),
                       pltpu.VMEM((SC_CHUNK, DP), U32),
                       pltpu.SemaphoreType.DMA, pltpu.SemaphoreType.DMA, pltpu.SemaphoreType.DMA],
        name="moe_dispatch",
    )(h2, dest0, dest1)


def _combine_body(yb_hbm, d0_hbm, d1_hbm, o0_hbm, o1_hbm, i0_v, i1_v, r0_v, r1_v, sem0, sem1):
    @pl.loop(0, SC_ROWS_PER_WORKER // SC_CHUNK)
    def _(j):
        off = _sc_token_offset(j)
        c0 = pltpu.async_copy(d0_hbm.at[pl.ds(off, SC_CHUNK)], i0_v, sem0)
        c1 = pltpu.async_copy(d1_hbm.at[pl.ds(off, SC_CHUNK)], i1_v, sem1)
        c0.wait()
        c1.wait()
        g0 = pltpu.async_copy(yb_hbm.at[i0_v], r0_v, sem0)
        g1 = pltpu.async_copy(yb_hbm.at[i1_v], r1_v, sem1)
        g0.wait()
        g1.wait()
        w0 = pltpu.async_copy(r0_v, o0_hbm.at[pl.ds(off, SC_CHUNK)], sem0)
        w1 = pltpu.async_copy(r1_v, o1_hbm.at[pl.ds(off, SC_CHUNK)], sem1)
        w0.wait()
        w1.wait()


def _combine_call(yb, dest0, dest1):
    out = jax.ShapeDtypeStruct((T_ALL, DP), U32)
    return pl.kernel(
        _combine_body,
        out_type=(out, out),
        mesh=_sc_mesh(),
        scratch_types=[pltpu.VMEM((SC_CHUNK,), jnp.int32), pltpu.VMEM((SC_CHUNK,), jnp.int32),
                       pltpu.VMEM((SC_CHUNK, DP), U32), pltpu.VMEM((SC_CHUNK, DP), U32),
                       pltpu.SemaphoreType.DMA, pltpu.SemaphoreType.DMA],
        name="moe_combine",
    )(yb, dest0, dest1)


def _moe_rows(h2, dest, meta, w1, w3, w2, layer):
    dest0, dest1 = dest[0], dest[1]
    xb = _dispatch_call(h2, dest0, dest1)
    yb = _expert_call(meta[0, :N_EXP_BLOCKS], meta[1, :N_EXP_BLOCKS], meta[2, :1], meta[3, :N_EXP_BLOCKS],
                      xb, w1, w3, w2, layer)
    return _combine_call(yb, dest0, dest1)


def _final_kernel(is_prompt, x_ref, yg0_ref, yg1_ref, gates_ref, modp_ref, mods_ref, gain_ref, o_ref):
    x = _add_moe(x_ref[...], yg0_ref, yg1_ref, gates_ref, _tile_mod((modp_ref, mods_ref), is_prompt))
    o_ref[...] = _rms(x) * gain_ref[...]


def _final_call(x, prev, gain, row0, n_rows):
    rows = ROWS_WIDE
    tile0 = row0 // rows
    is_prompt = row0 < T_PROMPT
    assert row0 + n_rows <= T_PROMPT or not is_prompt

    def tile(width):
        return pl.BlockSpec((rows, width), lambda i: (tile0 + i, 0))

    mod_specs = [pl.BlockSpec(s.block_shape, lambda i, m=s.index_map: m(tile0 + i)) for s in _mod_specs(rows)]
    return pl.pallas_call(
        functools.partial(_final_kernel, is_prompt),
        grid=(n_rows // rows,),
        in_specs=[tile(D), tile(DP), tile(DP), tile(ROUTER_LANES), *mod_specs, _resident((1, D))],
        out_specs=pl.BlockSpec((rows, D), lambda i: (i, 0)),
        out_shape=jax.ShapeDtypeStruct((n_rows, D), F32),
        compiler_params=_params("parallel"),
        name="final_norm",
    )(x, *prev, gain)


def _rope_tables():
    pos = np.concatenate([np.tile(np.arange(SEQ), BATCH),
                          np.tile(PAST_LEN + np.arange(DEC_SEQ), DEC_BATCH)]).astype(np.float32)
    inv = (ROPE_BASE ** (-np.arange(ROPE_HALF, dtype=np.float32) / ROPE_HALF)).astype(np.float32)
    ang = (pos[:, None] * inv[None, :]).astype(np.float32).astype(np.float64)
    return jnp.asarray(np.cos(ang), F32), jnp.asarray(np.sin(ang), F32)


def kernel(x_prompt, x_sample, c_prompt, c_sample, state_ret, ada_w, ada_b, norm1_g, norm2_g, ret_w_in,
           ret_w_out, gm_w_in, gm_b_in, gm_ln_g, gm_ln_b, gm_w_s, gm_b_s, gm_w_out, gm_b_out, moe_w_rg,
           moe_b_rg, moe_w_re, moe_b_re, moe_w1, moe_w3, moe_w2, final_g):
    x = (x_prompt.reshape(T_PROMPT, D), x_sample.reshape(T_SAMPLE, D))
    c_all = jnp.concatenate([c_prompt, c_sample], axis=0)
    cos, sin = _rope_tables()
    dec = _ret_row_scales()

    mod_all = _ada_call(c_all, ada_w, ada_b).reshape(DEPTH, N_SEQ, 6, D)

    def layer_params(i):
        j = i // 2
        mod = (mod_all[i, :BATCH], mod_all[i, BATCH:])
        w_r = jnp.pad(jnp.concatenate([moe_w_re[i], moe_w_rg[i]], axis=1),
                      ((0, 0), (0, ROUTER_LANES - MOE_GROUPS - MOE_EXPERTS)))
        w_r_hi = w_r.astype(BF16)
        w_r_lo = (w_r - w_r_hi.astype(F32)).astype(BF16)
        w_r = jnp.concatenate([w_r_hi, w_r_lo], axis=1)
        b_r = jnp.pad(jnp.concatenate([moe_b_re[i].reshape(-1), moe_b_rg[i]]),
                      (0, ROUTER_LANES - MOE_GROUPS - MOE_EXPERTS)).reshape(1, ROUTER_LANES)
        if i % 2 == 0:
            weights = [ret_w_in[j:j + 1].astype(BF16), ret_w_out[j:j + 1].astype(BF16)]
        else:
            weights = [gm_w_in[j:j + 1].astype(BF16), gm_w_out[j:j + 1].astype(BF16),
                       *_gm_block_diag(gm_w_s[j], gm_b_s[j])]
        return mod, w_r, b_r, weights

    ret_prompt = ret_sample = gm_sample = None
    prev = None
    for i in range(DEPTH):
        j = i // 2
        mod, w_r, b_r, weights = layer_params(i)
        g1 = norm1_g[i].reshape(1, D)
        g2 = norm2_g[i].reshape(1, D)
        if i % 2 == 0:
            w_in_b, w_out_b = weights
            x, p = _ret_proj_call(x, prev, mod, g1, w_in_b, cos, sin, dec)
            y_p, ret_prompt = _ret_core_call(p, None, ret_prompt, j, BATCH, SEQ, RET_CHUNK_PROMPT, 0)
            y_s, ret_sample = _ret_core_call(p, state_ret, ret_sample, j, DEC_BATCH, DEC_SEQ,
                                             RET_CHUNK_SAMPLE, T_PROMPT)
            x, h2, logits = _ret_out_call(y_p, y_s, w_out_b, x, mod, g2, w_r, b_r)
        else:
            w_in_b, w_out_b, ws_bd, bs_bd = weights
            x, u, v, gm_sample = _gm_proj_call(x, prev, mod, g1, w_in_b, j,
                                               gm_b_in[j].reshape(1, GM_FFN), gm_ln_g[j].reshape(1, GM_HALF),
                                               gm_ln_b[j].reshape(1, GM_HALF), gm_sample)
            x, h2, logits = _gm_out_call(u, v, ws_bd, bs_bd, w_out_b,
                                         gm_b_out[j].reshape(1, D), x, mod, g2, w_r, b_r)
        dest, gates, meta = _route_call(logits)
        yg0, yg1 = _moe_rows(h2, dest, meta, moe_w1, moe_w3, moe_w2, i)
        prev = (yg0, yg1, gates, *mod)

    fg = final_g.reshape(1, D)
    y_prompt = _final_call(x, prev, fg, 0, T_PROMPT).reshape(BATCH, SEQ, D)
    y_sample = _final_call(x, prev, fg, T_PROMPT, T_SAMPLE).reshape(DEC_BATCH, DEC_SEQ, D)
    return (y_prompt, y_sample, ret_prompt, ret_sample,
            gm_sample.reshape(N_GM, DEC_BATCH, DEC_SEQ, GM_HALF))
```

```python
import functools

import numpy as np
import jax
import jax.numpy as jnp
from jax import lax
from jax.experimental import pallas as pl
from jax.experimental.pallas import tpu as pltpu
from jax.experimental.pallas import tpu_sc as plsc

F32 = jnp.float32
BF16 = jnp.bfloat16
U32 = jnp.uint32

D = 1024
BATCH, SEQ = 4, 4096
DEC_BATCH, DEC_SEQ = 16, 64
PAST_LEN = 4096
DEPTH = 4
N_RET = (DEPTH + 1) // 2
N_GM = DEPTH // 2
N_SEQ = BATCH + DEC_BATCH

RET_HEADS, RET_DK, RET_DV = 4, 256, 512
RET_QK = RET_HEADS * RET_DK
RET_V = RET_HEADS * RET_DV
RET_IN = 2 * RET_QK + 2 * RET_V
ROPE_BASE = 10000.0
ROPE_HALF = RET_DK // 2

GM_FFN = 6 * D
GM_HALF = GM_FFN // 2
GM_GROUPS = 4
GM_GDIM = GM_HALF // GM_GROUPS
GM_CHUNK = 128

MOE_GROUPS, MOE_PER_GROUP = 4, 8
MOE_EXPERTS = MOE_GROUPS * MOE_PER_GROUP
MOE_TOPK = 2
MOE_HIDDEN = 512
EPS = 1e-6

GROUP = DEC_SEQ
T_PROMPT = BATCH * SEQ
T_SAMPLE = DEC_BATCH * DEC_SEQ
T_ALL = T_PROMPT + T_SAMPLE
N_GROUPS = T_ALL // GROUP
ROWS_WIDE = 512
ROWS_GM_PROJ = 256

RET_CHUNK_PROMPT = 256
RET_CHUNK_SAMPLE = DEC_SEQ

GM_MIX = 256

EXP_BLOCK = 512
EXP_ROW_STEP = 128
N_ASSIGN = T_ALL * MOE_TOPK
N_EXP_BLOCKS = -(-(N_ASSIGN + MOE_EXPERTS * (EXP_BLOCK - 1)) // EXP_BLOCK)
P_ROWS = N_EXP_BLOCKS * EXP_BLOCK
ROUTER_LANES = 128
ROUTE_TM = 1024
N_ROUTE_TILES = T_ALL // ROUTE_TM
META_LANES = 256
assert META_LANES >= N_EXP_BLOCKS

DP = D // 2
SC_CORES, SC_SUBCORES = 2, 16
SC_WORKERS = SC_CORES * SC_SUBCORES
SC_ROWS_PER_WORKER = T_ALL // SC_WORKERS
SC_CHUNK = 32
assert SC_ROWS_PER_WORKER % SC_CHUNK == 0 and SC_CHUNK % 8 == 0

V7X_VMEM_LIMIT_BYTES = 56 * 1024 * 1024


def _params(*sem):
    return pltpu.CompilerParams(dimension_semantics=sem, vmem_limit_bytes=V7X_VMEM_LIMIT_BYTES)


def _resident(shape):
    nd = len(shape)
    return pl.BlockSpec(shape, lambda *_: (0,) * nd, pipeline_mode=pl.Buffered(1))


WEIGHT_STAGE_BYTES = 3 * 1024 * 1024


def _weight_scratch(k, n):
    chunk = k
    while chunk * n * 4 > WEIGHT_STAGE_BYTES:
        assert chunk % 16 == 0
        chunk //= 2
    return [pltpu.VMEM((k, n), BF16), pltpu.VMEM((2, chunk, n), F32), pltpu.SemaphoreType.DMA((2,))]


def _load_weight_bf16(w_hbm, layer, w_s, stage, sems):
    k = w_s.shape[0]
    chunk = stage.shape[1]

    def copy(c):
        return pltpu.make_async_copy(w_hbm.at[layer, pl.ds(c * chunk, chunk)], stage.at[c % 2], sems.at[c % 2])

    @pl.when(pl.program_id(0) == 0)
    def _():
        copy(0).start()
        for c in range(k // chunk):
            if c + 1 < k // chunk:
                copy(c + 1).start()
            copy(c).wait()
            w_s[c * chunk:(c + 1) * chunk, :] = stage[c % 2].astype(BF16)


def _rms(x):
    return x * lax.rsqrt(jnp.mean(x * x, axis=-1, keepdims=True) + EPS)


def _silu(x):
    return x * jax.nn.sigmoid(x)


def _per_group(x2d, fn):
    rows = x2d.shape[0]
    return fn(x2d.reshape(rows // GROUP, GROUP, D)).reshape(rows, D)


def _tile_mod(mod_refs, is_prompt):
    modp_ref, mods_ref = mod_refs
    return jnp.where(is_prompt, jnp.broadcast_to(modp_ref[...], mods_ref.shape), mods_ref[...])


def _norm_mod(x, gain_ref, mod, shift_idx):
    y = _rms(x) * gain_ref[...]
    scale = mod[:, shift_idx + 1:shift_idx + 2, :]
    shift = mod[:, shift_idx:shift_idx + 1, :]
    return _per_group(y, lambda y3: y3 * (1.0 + scale) + shift)


def _pack_bf16_pairs(x):
    lo = lax.bitcast_convert_type(x[:, :DP].astype(BF16).astype(F32), U32)
    hi = lax.bitcast_convert_type(x[:, DP:].astype(BF16).astype(F32), U32)
    return (lo >> 16) | (hi & U32(0xFFFF0000))


def _unpack_bf16_pairs(w):
    lo = lax.bitcast_convert_type(w << 16, F32)
    hi = lax.bitcast_convert_type(w & U32(0xFFFF0000), F32)
    return jnp.concatenate([lo, hi], axis=1)


def _add_moe(x, yg0_ref, yg1_ref, gates_ref, mod_prev):
    g = gates_ref[...]
    y = g[:, 0:1] * _unpack_bf16_pairs(yg0_ref[...]) + g[:, 1:2] * _unpack_bf16_pairs(yg1_ref[...])
    gate2 = mod_prev[:, 5:6, :]
    return x + _per_group(y, lambda y3: y3 * gate2)


ADA_TN = 1536


def _ada_kernel(c_ref, w_ref, b_ref, o_ref):
    c = c_ref[...]
    s = _silu(c).astype(BF16)
    o_ref[0] = jnp.dot(s, w_ref[0].astype(BF16), preferred_element_type=F32) + b_ref[0]


def _ada_call(c_all, ada_w, ada_b):
    return pl.pallas_call(
        _ada_kernel,
        grid=(DEPTH, 6 * D // ADA_TN),
        in_specs=[
            pl.BlockSpec((N_SEQ, D), lambda i, j: (0, 0)),
            pl.BlockSpec((1, D, ADA_TN), lambda i, j: (i, 0, j)),
            pl.BlockSpec((1, 1, ADA_TN), lambda i, j: (i, 0, j)),
        ],
        out_specs=pl.BlockSpec((1, N_SEQ, ADA_TN), lambda i, j: (i, 0, j)),
        out_shape=jax.ShapeDtypeStruct((DEPTH, N_SEQ, 6 * D), F32),
        compiler_params=_params("parallel", "parallel"),
        name="ada_modulation",
    )(c_all, ada_w, ada_b.reshape(DEPTH, 1, 6 * D))


def _n_tiles(rows):
    return T_ALL // rows


def _n_prompt_tiles(rows):
    return T_PROMPT // rows


def _row_spec(rows, width):
    return pl.BlockSpec((rows, width), lambda i: (i, 0))


def _mod_specs(rows):
    npt = _n_prompt_tiles(rows)
    return [pl.BlockSpec((1, 6, D), lambda i: (jnp.minimum(i * rows // SEQ, BATCH - 1), 0, 0)),
            pl.BlockSpec((rows // DEC_SEQ, 6, D), lambda i: (jnp.maximum(i - npt, 0), 0, 0))]


def _prev_specs(rows):
    return [_row_spec(rows, DP), _row_spec(rows, DP), _row_spec(rows, ROUTER_LANES)] + _mod_specs(rows)


def _prompt_rows_spec(rows, width):
    last = _n_prompt_tiles(rows) - 1
    return pl.BlockSpec((rows, width), lambda i: (jnp.minimum(i, last), 0))


def _sample_rows_spec(rows, width):
    npt = _n_prompt_tiles(rows)
    return pl.BlockSpec((rows, width), lambda i: (jnp.maximum(i - npt, 0), 0))


def _ret_proj_kernel(has_prev, n_prompt_tiles, layer, *refs):
    is_prompt = pl.program_id(0) < n_prompt_tiles
    if has_prev:
        x_ref, yg0_ref, yg1_ref, gates_ref = refs[:4]
        x = _add_moe(x_ref[...], yg0_ref, yg1_ref, gates_ref, _tile_mod(refs[4:6], is_prompt))
        refs = refs[6:]
    else:
        x = jnp.where(is_prompt, refs[0][...], refs[1][...])
        refs = refs[2:]
    mod = _tile_mod(refs[:2], is_prompt)
    gain_ref, w_hbm, cos_ref, sin_ref, dec_ref, xo_ref, p_ref, w_s, w_stage, w_sems = refs[2:]
    _load_weight_bf16(w_hbm, layer, w_s, w_stage, w_sems)
    xo_ref[...] = x
    hb = _norm_mod(x, gain_ref, mod, 0).astype(BF16)
    cos = cos_ref[...]
    sin = sin_ref[...]
    dec = dec_ref[...]
    for j in range(2 * RET_HEADS):
        lo = j * RET_DK
        acc = jnp.dot(hb, w_s[:, lo:lo + RET_DK], preferred_element_type=F32)
        x1 = acc[:, :ROPE_HALF]
        x2 = acc[:, ROPE_HALF:]
        scale = dec[:, j:j + 1]
        p_ref[:, lo:lo + ROPE_HALF] = ((x1 * cos - x2 * sin) * scale).astype(BF16)
        p_ref[:, lo + ROPE_HALF:lo + RET_DK] = ((x1 * sin + x2 * cos) * scale).astype(BF16)
    for j in range(2 * RET_HEADS):
        lo = 2 * RET_QK + j * RET_DV
        acc = jnp.dot(hb, w_s[:, lo:lo + RET_DV], preferred_element_type=F32)
        if j >= RET_HEADS:
            acc = _silu(acc)
        p_ref[:, lo:lo + RET_DV] = acc.astype(BF16)


def _ret_proj_call(x, prev, mod, gain, w_in, layer, cos, sin, dec):
    rows = ROWS_WIDE
    has_prev = prev is not None
    if has_prev:
        in_specs = [_row_spec(rows, D)] + _prev_specs(rows)
        args = [x] + list(prev)
    else:
        in_specs = [_prompt_rows_spec(rows, D), _sample_rows_spec(rows, D)]
        args = list(x)
    in_specs += _mod_specs(rows) + [
        _resident((1, D)), pl.BlockSpec(memory_space=pl.ANY),
        _row_spec(rows, ROPE_HALF), _row_spec(rows, ROPE_HALF), _row_spec(rows, 2 * RET_HEADS)]
    args += [*mod, gain, w_in, cos, sin, dec]
    return pl.pallas_call(
        functools.partial(_ret_proj_kernel, has_prev, _n_prompt_tiles(rows), layer),
        grid=(_n_tiles(rows),),
        in_specs=in_specs,
        out_specs=[_row_spec(rows, D), _row_spec(rows, RET_IN)],
        out_shape=[jax.ShapeDtypeStruct((T_ALL, D), F32), jax.ShapeDtypeStruct((T_ALL, RET_IN), BF16)],
        scratch_shapes=_weight_scratch(D, RET_IN),
        compiler_params=_params("arbitrary"),
        name="ret_proj",
    )(*args)


def _ret_core_kernel(has_s0, n_chunks, layer, *refs):
    refs = list(refs)
    p_ref = refs.pop(0)
    s0_ref = refs.pop(0) if has_s0 else None
    causal_ref, cd_ref = refs[:2]
    y_ref, so_ref, s_ref = refs[-3:]
    c = pl.program_id(1)

    @pl.when(c == 0)
    def _():
        if has_s0:
            s_ref[...] = s0_ref[0, 0]
        else:
            s_ref[...] = jnp.zeros_like(s_ref)

    for h in range(RET_HEADS):
        qb = p_ref[:, h * RET_DK:(h + 1) * RET_DK]
        kb = p_ref[:, RET_QK + h * RET_DK:RET_QK + (h + 1) * RET_DK]
        vb = p_ref[:, 2 * RET_QK + h * RET_DV:2 * RET_QK + (h + 1) * RET_DV]
        gb = p_ref[:, 2 * RET_QK + RET_V + h * RET_DV:2 * RET_QK + RET_V + (h + 1) * RET_DV]
        scores = lax.dot_general(qb, kb, (((1,), (1,)), ((), ())), preferred_element_type=F32)
        scores = scores * causal_ref[...]
        s_old = s_ref[h]
        o = (jnp.dot(scores.astype(BF16), vb, preferred_element_type=F32)
             + jnp.dot(qb, s_old.astype(BF16), preferred_element_type=F32))
        s_ref[h] = cd_ref[h][:, 0:1] * (s_old + lax.dot_general(
            kb, vb, (((0,), (0,)), ((), ())), preferred_element_type=F32))
        y_ref[:, h * RET_DV:(h + 1) * RET_DV] = (gb.astype(F32) * _rms(o)).astype(BF16)

    @pl.when(c == n_chunks - 1)
    def _():
        so_ref[0, 0] = s_ref[...]
        if layer == 0:
            for later in range(1, N_RET):
                so_ref[later, 0] = jnp.zeros_like(s_ref)


def _ret_log_gamma():
    return np.log1p(-np.exp2(-5.0 - np.arange(RET_HEADS, dtype=np.float64)))


def _ret_chunk_tables(cl):
    idx = np.arange(cl)
    causal = (idx[:, None] >= idx[None, :]).astype(np.float32)
    cd = np.broadcast_to(np.exp(_ret_log_gamma() * cl)[:, None, None], (RET_HEADS, 1, 128))
    return jnp.asarray(causal, F32), jnp.asarray(cd, F32)


def _ret_row_scales():
    c = np.concatenate([np.arange(T_PROMPT) % RET_CHUNK_PROMPT,
                        np.arange(T_SAMPLE) % RET_CHUNK_SAMPLE]).astype(np.float64)
    e = (c[:, None] + 1.0) * _ret_log_gamma()[None, :]
    return jnp.asarray(np.concatenate([np.exp(e), np.exp(-e) * RET_DK ** -0.5], axis=1), F32)


def _ret_core_call(p, s0, states, layer, n_seq, seq_len, cl, row0):
    has_s0 = s0 is not None
    n_chunks = seq_len // cl
    rb0 = row0 // cl
    state = (RET_HEADS, RET_DK, RET_DV)
    in_specs = [pl.BlockSpec((cl, RET_IN), lambda b, c: (rb0 + b * n_chunks + c, 0))]
    args = [p]
    if has_s0:
        in_specs.append(pl.BlockSpec((1, 1) + state, lambda b, c: (layer, b, 0, 0, 0)))
        args.append(s0)
    in_specs += [_resident((cl, cl)), _resident((RET_HEADS, 1, 128))]
    args += list(_ret_chunk_tables(cl))
    if layer == 0:
        assert states is None
        state_spec = pl.BlockSpec((N_RET, 1) + state, lambda b, c: (0, b, 0, 0, 0))
        aliases = {}
    else:
        in_specs.append(pl.BlockSpec(memory_space=pl.ANY))
        args.append(states)
        state_spec = pl.BlockSpec((1, 1) + state, lambda b, c: (layer, b, 0, 0, 0))
        aliases = {len(args) - 1: 1}
    return pl.pallas_call(
        functools.partial(_ret_core_kernel, has_s0, n_chunks, layer),
        grid=(n_seq, n_chunks),
        in_specs=in_specs,
        out_specs=[pl.BlockSpec((cl, RET_V), lambda b, c: (b * n_chunks + c, 0)), state_spec],
        out_shape=[jax.ShapeDtypeStruct((n_seq * seq_len, RET_V), BF16),
                   jax.ShapeDtypeStruct((N_RET, n_seq) + state, F32)],
        scratch_shapes=[pltpu.VMEM(state, F32)],
        input_output_aliases=aliases,
        compiler_params=_params("parallel", "arbitrary"),
        name="ret_core",
    )(*args)


def _residual_router(acc, x_ref, mod, gain_ref, wr_ref, br_ref, xo_ref, h2_ref, lg_ref):
    gate1 = mod[:, 2:3, :]
    xn = x_ref[...] + _per_group(acc, lambda a3: a3 * gate1)
    xo_ref[...] = xn
    h2 = _norm_mod(xn, gain_ref, mod, 3)
    h2_ref[...] = _pack_bf16_pairs(h2)
    hh = jnp.dot(h2.astype(BF16), wr_ref[...], preferred_element_type=F32)
    lg_ref[...] = hh[:, :ROUTER_LANES] + hh[:, ROUTER_LANES:] + br_ref[...]


def _mix_out_specs(rows):
    return [_row_spec(rows, D), _row_spec(rows, DP), _row_spec(rows, ROUTER_LANES)]


_MIX_OUT_SHAPE = [
    jax.ShapeDtypeStruct((T_ALL, D), F32),
    jax.ShapeDtypeStruct((T_ALL, DP), U32),
    jax.ShapeDtypeStruct((T_ALL, ROUTER_LANES), F32),
]


def _router_specs():
    return [_resident((D, 2 * ROUTER_LANES)), _resident((1, ROUTER_LANES))]


def _ret_out_kernel(n_prompt_tiles, layer, yp_ref, ys_ref, w_hbm, x_ref, modp_ref, mods_ref, gain_ref,
                    wr_ref, br_ref, xo_ref, h2_ref, lg_ref, w_s, w_stage, w_sems):
    _load_weight_bf16(w_hbm, layer, w_s, w_stage, w_sems)
    is_prompt = pl.program_id(0) < n_prompt_tiles
    yin = jnp.where(is_prompt, yp_ref[...], ys_ref[...])
    acc = jnp.dot(yin, w_s[...], preferred_element_type=F32)
    mod = _tile_mod((modp_ref, mods_ref), is_prompt)
    _residual_router(acc, x_ref, mod, gain_ref, wr_ref, br_ref, xo_ref, h2_ref, lg_ref)


def _ret_out_call(y_prompt, y_sample, w_out, layer, x, mod, gain, w_r, b_r):
    rows = ROWS_WIDE
    return pl.pallas_call(
        functools.partial(_ret_out_kernel, _n_prompt_tiles(rows), layer),
        grid=(_n_tiles(rows),),
        in_specs=[_prompt_rows_spec(rows, RET_V), _sample_rows_spec(rows, RET_V),
                  pl.BlockSpec(memory_space=pl.ANY), _row_spec(rows, D), *_mod_specs(rows),
                  _resident((1, D))] + _router_specs(),
        out_specs=_mix_out_specs(rows),
        out_shape=_MIX_OUT_SHAPE,
        scratch_shapes=_weight_scratch(RET_V, D),
        compiler_params=_params("arbitrary"),
        name="ret_out",
    )(y_prompt, y_sample, w_out, x, *mod, gain, w_r, b_r)


GM_TN = 512


_GELU_C = float(np.sqrt(2.0 / np.pi))


def _gelu_tanh(x):
    hx = 0.5 * x
    return hx * jnp.tanh(x * (_GELU_C + (_GELU_C * 0.044715) * (x * x))) + hx


def _gm_proj_kernel(n_tiles, n_prompt_tiles, layer, *refs):
    x_ref, yg0_ref, yg1_ref, gates_ref = refs[:4]
    gain_ref, w_hbm, b_ref, lg_ref, lb_ref = refs[8:13]
    xo_ref, u_ref, v_ref, vs_ref, hb_s, vraw_s, stat_s, sum_s, w_s, w_stage, w_sems = refs[-11:]
    _load_weight_bf16(w_hbm, layer, w_s, w_stage, w_sems)
    i = pl.program_id(0)
    slot_a = i % 2
    slot_b = 1 - slot_a
    rows = x_ref.shape[0]

    @pl.when(i == 0)
    def _():
        hb_s[1] = jnp.zeros((rows, D), BF16)
        vraw_s[1] = jnp.zeros((rows, GM_HALF), BF16)
        stat_s[1] = jnp.zeros((rows, 256), F32)

    def add_row(x, row_ref, lo, width):
        x3 = x.reshape(rows // 8, 8, width) + row_ref[:, lo:lo + width]
        return x3.reshape(rows, width)

    def mul_row(x, row_ref, lo, width):
        x3 = x.reshape(rows // 8, 8, width) * row_ref[:, lo:lo + width]
        return x3.reshape(rows, width)

    def gelu_chunk(slot, lo):
        z = jnp.dot(hb_s[slot], w_s[:, lo:lo + GM_TN], preferred_element_type=F32)
        return _gelu_tanh(add_row(z, b_ref, lo, GM_TN).astype(BF16))

    def stage_b_chunk(lo):
        u_ref[:, lo:lo + GM_TN] = gelu_chunk(slot_b, lo)
        for k in range(lo, lo + GM_TN, 128):
            vk = vraw_s[slot_b, :, k:k + 128].astype(F32) * stat_s[slot_b, :, 0:128] + stat_s[slot_b, :, 128:256]
            vn = add_row(mul_row(vk, lg_ref, k, 128), lb_ref, k, 128)
            v_ref[:, k:k + 128] = vn.astype(BF16)
            vs_ref[0, :, k:k + 128] = vn

    chunks = list(range(0, GM_HALF, GM_TN))
    lead = 2
    for lo in chunks[:lead]:
        stage_b_chunk(lo)

    a_is_prompt = jnp.minimum(i, n_tiles - 1) < n_prompt_tiles
    x = _add_moe(x_ref[...], yg0_ref, yg1_ref, gates_ref, _tile_mod(refs[4:6], a_is_prompt))
    xo_ref[...] = x
    hb = _norm_mod(x, gain_ref, _tile_mod(refs[6:8], a_is_prompt), 0).astype(BF16)
    hb_s[slot_a] = hb

    for n, lo in enumerate(chunks):
        gz = gelu_chunk(slot_a, GM_HALF + lo)
        vraw_s[slot_a, :, lo:lo + GM_TN] = gz
        gf = gz.astype(F32)
        pieces = [gf[:, k:k + 128] for k in range(0, GM_TN, 128)]
        t1 = functools.reduce(lambda p, q: p + q, pieces)
        t2 = functools.reduce(lambda p, q: p + q, [p * p for p in pieces])
        if n == 0:
            sum_s[:, 0:128] = t1
            sum_s[:, 128:256] = t2
        else:
            sum_s[:, 0:128] += t1
            sum_s[:, 128:256] += t2
        if n + lead < len(chunks):
            stage_b_chunk(chunks[n + lead])
    mu = jnp.sum(sum_s[:, 0:128], axis=-1, keepdims=True) * (1.0 / GM_HALF)
    var = jnp.sum(sum_s[:, 128:256], axis=-1, keepdims=True) * (1.0 / GM_HALF) - mu * mu
    rstd = lax.rsqrt(var + EPS)
    stat_s[slot_a, :, 0:128] = jnp.broadcast_to(rstd, (rows, 128))
    stat_s[slot_a, :, 128:256] = jnp.broadcast_to(-mu * rstd, (rows, 128))

    if layer == 0:
        @pl.when(i - 1 >= n_prompt_tiles)
        def _():
            for later in range(1, N_GM):
                vs_ref[later] = jnp.zeros((rows, GM_HALF), F32)


def _gm_proj_call(x, prev, mod, gain, w_in, layer, b_in, ln_g, ln_b, vs_all):
    rows = ROWS_GM_PROJ
    n, npt = _n_tiles(rows), _n_prompt_tiles(rows)

    def stage_a(spec):
        return pl.BlockSpec(spec.block_shape, lambda i, m=spec.index_map: m(jnp.minimum(i, n - 1)))

    def stage_b(spec):
        return pl.BlockSpec(spec.block_shape, lambda i, m=spec.index_map: m(jnp.maximum(i - 1, 0)))

    in_specs = [stage_a(s) for s in [_row_spec(rows, D)] + _prev_specs(rows) + _mod_specs(rows)] + [
        _resident((1, D)), pl.BlockSpec(memory_space=pl.ANY), _resident((8, GM_FFN)),
        _resident((8, GM_HALF)), _resident((8, GM_HALF))]
    rows8 = [jnp.broadcast_to(r.reshape(1, -1), (8, r.size)) for r in (b_in, ln_g, ln_b)]
    args = [x, *prev, *mod, gain, w_in, *rows8]
    if layer == 0:
        assert vs_all is None
        vs_spec = pl.BlockSpec((N_GM, rows, GM_HALF), lambda i: (0, jnp.maximum(i - 1 - npt, 0), 0))
        aliases = {}
    else:
        in_specs.append(pl.BlockSpec(memory_space=pl.ANY))
        args.append(vs_all)
        vs_spec = pl.BlockSpec((1, rows, GM_HALF), lambda i: (layer, jnp.maximum(i - 1 - npt, 0), 0))
        aliases = {len(args) - 1: 3}
    return pl.pallas_call(
        functools.partial(_gm_proj_kernel, n, npt, layer),
        grid=(n + 1,),
        in_specs=in_specs,
        out_specs=[stage_a(_row_spec(rows, D)), stage_b(_row_spec(rows, GM_HALF)),
                   stage_b(_row_spec(rows, GM_HALF)), vs_spec],
        out_shape=[jax.ShapeDtypeStruct((T_ALL, D), F32),
                   jax.ShapeDtypeStruct((T_ALL, GM_HALF), BF16),
                   jax.ShapeDtypeStruct((T_ALL, GM_HALF), BF16),
                   jax.ShapeDtypeStruct((N_GM, T_SAMPLE, GM_HALF), F32)],
        scratch_shapes=[pltpu.VMEM((2, rows, D), BF16), pltpu.VMEM((2, rows, GM_HALF), BF16),
                        pltpu.VMEM((2, rows, 256), F32), pltpu.VMEM((rows, 256), F32),
                        *_weight_scratch(D, GM_FFN)],
        input_output_aliases=aliases,
        compiler_params=_params("arbitrary"),
        name="gm_proj",
    )(*args)


def _gm_out_kernel(n_prompt_tiles, layer, u_ref, v_ref, ws_ref, bs_ref, w_hbm, bo_ref, x_ref, modp_ref,
                   mods_ref, gain_ref, wr_ref, br_ref, xo_ref, h2_ref, lg_ref, w_s, w_stage, w_sems):
    _load_weight_bf16(w_hbm, layer, w_s, w_stage, w_sems)
    rows = u_ref.shape[0]
    mod = _tile_mod((modp_ref, mods_ref), pl.program_id(0) < n_prompt_tiles)
    pieces = []
    for r0 in range(0, rows, GM_MIX):
        acc = jnp.zeros((GM_MIX, D), F32)
        for g in range(GM_GROUPS):
            lo = g * GM_GDIM
            sp = jnp.dot(ws_ref[0, g], v_ref[r0:r0 + GM_MIX, lo:lo + GM_GDIM],
                         preferred_element_type=F32) + bs_ref[0, g]
            gated = (u_ref[r0:r0 + GM_MIX, lo:lo + GM_GDIM].astype(F32) * sp).astype(BF16)
            acc = acc + jnp.dot(gated, w_s[lo:lo + GM_GDIM, :], preferred_element_type=F32)
        pieces.append(acc)
    acc = jnp.concatenate(pieces, axis=0) + bo_ref[...]
    _residual_router(acc, x_ref, mod, gain_ref, wr_ref, br_ref, xo_ref, h2_ref, lg_ref)


def _gm_block_diag(w_s, b_s):
    mats, biases = [], []
    for cl in (GM_CHUNK, DEC_SEQ):
        tri = jnp.tril(jnp.ones((cl, cl), bool))
        blk = jnp.where(tri[None], w_s[:, :cl, :cl], 0.0)
        reps = GM_MIX // cl
        eye = jnp.eye(reps, dtype=w_s.dtype)
        bd = jnp.einsum("ab,gts->gatbs", eye, blk).reshape(GM_GROUPS, GM_MIX, GM_MIX)
        mats.append(bd)
        biases.append(jnp.tile(b_s[:, :cl], (1, reps))[:, :, None])
    return jnp.stack(mats).astype(BF16), jnp.stack(biases).astype(F32)


def _gm_out_call(u, v, ws_bd, bs_bd, w_out, layer, b_out, x, mod, gain, w_r, b_r):
    rows = ROWS_WIDE
    npt = _n_prompt_tiles(rows)

    def variant(i):
        return jnp.where(i >= npt, 1, 0)

    return pl.pallas_call(
        functools.partial(_gm_out_kernel, npt, layer),
        grid=(_n_tiles(rows),),
        in_specs=[_row_spec(rows, GM_HALF), _row_spec(rows, GM_HALF),
                  pl.BlockSpec((1, GM_GROUPS, GM_MIX, GM_MIX), lambda i: (variant(i), 0, 0, 0)),
                  pl.BlockSpec((1, GM_GROUPS, GM_MIX, 1), lambda i: (variant(i), 0, 0, 0)),
                  pl.BlockSpec(memory_space=pl.ANY), _resident((1, D)), _row_spec(rows, D),
                  *_mod_specs(rows), _resident((1, D))] + _router_specs(),
        out_specs=_mix_out_specs(rows),
        out_shape=_MIX_OUT_SHAPE,
        scratch_shapes=_weight_scratch(GM_HALF, D),
        compiler_params=_params("arbitrary"),
        name="gm_out",
    )(u, v, ws_bd, bs_bd, w_out, b_out, x, *mod, gain, w_r, b_r)


def _expert_kernel(layer, be_ref, bv_ref, nb_ref, nx_ref, xb_ref, w1_hbm, w3_hbm, w2_hbm, yb_ref,
                   st1, st3, st2, w1s, w3s, w2s, sems, slot_ref):
    b = pl.program_id(0)

    def weight_copies(e, slot):
        return (pltpu.make_async_copy(w1_hbm.at[layer, e], st1.at[slot], sems.at[slot, 0]),
                pltpu.make_async_copy(w3_hbm.at[layer, e], st3.at[slot], sems.at[slot, 1]),
                pltpu.make_async_copy(w2_hbm.at[layer, e], st2.at[slot], sems.at[slot, 2]))

    @pl.when(b == 0)
    def _():
        slot_ref[0] = 0
        for cp in weight_copies(be_ref[0], 0):
            cp.start()

    @pl.when(b < nb_ref[0])
    def _():
        prev_e = be_ref[jnp.maximum(b - 1, 0)]

        @pl.when((b == 0) | (be_ref[b] != prev_e))
        def _():
            slot = slot_ref[0]
            for cp in weight_copies(be_ref[b], slot):
                cp.wait()
            w1s[...] = st1[slot].astype(BF16)
            w3s[...] = st3[slot].astype(BF16)
            w2s[...] = st2[slot].astype(BF16)

            @pl.when(nx_ref[b] >= 0)
            def _():
                for cp in weight_copies(nx_ref[b], 1 - slot):
                    cp.start()

            slot_ref[0] = 1 - slot

        valid = bv_ref[b]

        def run_rows(n):
            row = lax.broadcasted_iota(jnp.int32, (n, 1), 0)
            xw = jnp.where(row < valid, xb_ref[0:n], U32(0))
            x = _unpack_bf16_pairs(xw).astype(BF16)
            a = jnp.dot(x, w1s[...], preferred_element_type=F32)
            c = jnp.dot(x, w3s[...], preferred_element_type=F32)
            h = (_silu(a) * c).astype(BF16)
            yb_ref[0:n] = _pack_bf16_pairs(jnp.dot(h, w2s[...], preferred_element_type=F32))
            if n < EXP_BLOCK:
                yb_ref[n:EXP_BLOCK] = jnp.zeros((EXP_BLOCK - n, DP), U32)

        for n in range(EXP_ROW_STEP, EXP_BLOCK + 1, EXP_ROW_STEP):
            pl.when((valid > n - EXP_ROW_STEP) & (valid <= n))(functools.partial(run_rows, n))


def _expert_call(blk_e, blk_valid, n_blk, blk_next, xb, w1, w3, w2, layer):
    def blk(b, be, bv, nb, nx):
        return (jnp.minimum(b, nb[0] - 1), 0)

    up, down = (D, MOE_HIDDEN), (MOE_HIDDEN, D)
    grid_spec = pltpu.PrefetchScalarGridSpec(
        num_scalar_prefetch=4,
        grid=(N_EXP_BLOCKS,),
        in_specs=[pl.BlockSpec((EXP_BLOCK, DP), blk)] + [pl.BlockSpec(memory_space=pl.ANY)] * 3,
        out_specs=pl.BlockSpec((EXP_BLOCK, DP), blk),
        scratch_shapes=[pltpu.VMEM((2,) + up, F32), pltpu.VMEM((2,) + up, F32), pltpu.VMEM((2,) + down, F32),
                        pltpu.VMEM(up, BF16), pltpu.VMEM(up, BF16), pltpu.VMEM(down, BF16),
                        pltpu.SemaphoreType.DMA((2, 3)), pltpu.SMEM((1,), jnp.int32)],
    )
    return pl.pallas_call(
        functools.partial(_expert_kernel, layer),
        grid_spec=grid_spec,
        out_shape=jax.ShapeDtypeStruct((P_ROWS, DP), U32),
        compiler_params=_params("arbitrary"),
        name="experts",
    )(blk_e, blk_valid, n_blk, blk_next, xb, w1, w3, w2)


def _route_kernel(lg_ref, dest_ref, gates_ref, meta_ref, cnt_ref):
    t = pl.program_id(0)

    @pl.when(t == 0)
    def _():
        cnt_ref[...] = jnp.zeros_like(cnt_ref)

    pl.when(t < N_ROUTE_TILES)(functools.partial(_route_tile, t, lg_ref, dest_ref, gates_ref, cnt_ref))
    pl.when(t == N_ROUTE_TILES)(functools.partial(_route_finish, dest_ref, meta_ref, cnt_ref))


ROUTE_RANK_BITS = 16
assert P_ROWS < 2 ** ROUTE_RANK_BITS


def _route_tile(t, lg_ref, dest_ref, gates_ref, cnt_ref):
    tm = ROUTE_TM
    lt = lg_ref[...].T
    el = lt[0:MOE_EXPERTS]
    gl = lt[MOE_EXPERTS:MOE_EXPERTS + 8]
    gidx = lax.broadcasted_iota(jnp.int32, (8, tm), 0)
    neg = jnp.float32(-jnp.inf)
    gl = jnp.where(gidx < MOE_GROUPS, gl, neg)
    gmax = jnp.max(gl, axis=0, keepdims=True)
    grp = jnp.min(jnp.where(gl == gmax, gidx, MOE_GROUPS), axis=0, keepdims=True)
    eidx = lax.broadcasted_iota(jnp.int32, (MOE_EXPERTS, tm), 0)
    els = jnp.where((eidx >> 3) == grp, el, neg)
    m1 = jnp.max(els, axis=0, keepdims=True)
    i1 = jnp.min(jnp.where(els == m1, eidx, MOE_EXPERTS), axis=0, keepdims=True)
    els2 = jnp.where(eidx == i1, neg, els)
    m2 = jnp.max(els2, axis=0, keepdims=True)
    i2 = jnp.min(jnp.where(els2 == m2, eidx, MOE_EXPERTS), axis=0, keepdims=True)
    sel1 = eidx == i1
    sel2 = eidx == i2
    cnt = jnp.where(sel1 | sel2, 1.0, 0.0)

    lane = ROUTER_LANES
    before = (lax.broadcasted_iota(jnp.int32, (lane, lane), 0)
              < lax.broadcasted_iota(jnp.int32, (lane, lane), 1))
    tri = jnp.where(before, 1.0, 0.0).astype(BF16)
    run = cnt_ref[...]
    r1, r2 = [], []
    for k in range(tm // lane):
        piece = slice(k * lane, (k + 1) * lane)
        ck = cnt[:, piece]
        pos = run + jnp.dot(ck.astype(BF16), tri, preferred_element_type=F32)
        r1.append(jnp.sum(jnp.where(sel1[:, piece], pos, 0.0), axis=0, keepdims=True))
        r2.append(jnp.sum(jnp.where(sel2[:, piece], pos, 0.0), axis=0, keepdims=True))
        run = run + jnp.sum(ck, axis=1, keepdims=True)
    cnt_ref[...] = run
    rank = jnp.concatenate([jnp.concatenate(r1, axis=1), jnp.concatenate(r2, axis=1)], axis=0)
    eid = jnp.concatenate([i1, i2], axis=0)
    dest_ref[:, pl.ds(pl.multiple_of(t * tm, tm), tm)] = (eid << ROUTE_RANK_BITS) + rank.astype(jnp.int32)

    g_w = 1.0 / jnp.sum(jnp.exp(gl - gmax), axis=0, keepdims=True)
    e21 = jnp.exp(m2 - m1)
    p1 = 1.0 / (1.0 + e21)
    rid = lax.broadcasted_iota(jnp.int32, (ROUTER_LANES, tm), 0)
    gt = jnp.where(rid == 0, g_w * p1, jnp.where(rid == 1, g_w * (e21 * p1), 0.0))
    gates_ref[...] = gt.T


def _route_finish(dest_ref, meta_ref, cnt_ref):
    counts = cnt_ref[...]
    nblk = jnp.floor((counts + (EXP_BLOCK - 1.0)) * (1.0 / EXP_BLOCK))
    r = lax.broadcasted_iota(jnp.int32, (MOE_EXPERTS, MOE_EXPERTS), 0)
    c = lax.broadcasted_iota(jnp.int32, (MOE_EXPERTS, MOE_EXPERTS), 1)
    nblk_row = jnp.sum(jnp.where(r == c, nblk, 0.0), axis=0, keepdims=True)
    bstart = jnp.sum(jnp.where(c < r, nblk_row, 0.0), axis=1, keepdims=True)
    bend = bstart + nblk
    bidx = lax.broadcasted_iota(jnp.int32, (1, META_LANES), 1).astype(F32)
    blk_e = jnp.minimum(jnp.sum(jnp.where(bidx >= bend, 1.0, 0.0), axis=0, keepdims=True),
                        MOE_EXPERTS - 1.0)
    erow = lax.broadcasted_iota(jnp.int32, (MOE_EXPERTS, META_LANES), 0).astype(F32)
    mine = erow == blk_e
    cnt_b = jnp.sum(jnp.where(mine, counts, 0.0), axis=0, keepdims=True)
    start_b = jnp.sum(jnp.where(mine, bstart, 0.0), axis=0, keepdims=True)
    valid = jnp.clip(cnt_b - (bidx - start_b) * EXP_BLOCK, 0.0, float(EXP_BLOCK))
    n_blk = jnp.sum(nblk, axis=0, keepdims=True)
    end_b = jnp.sum(jnp.where(mine, bend, 0.0), axis=0, keepdims=True)
    nxt = jnp.minimum(jnp.sum(jnp.where(end_b >= bend, 1.0, 0.0), axis=0, keepdims=True),
                      MOE_EXPERTS - 1.0)
    nxt = jnp.where(end_b < n_blk, nxt, -1.0)
    mrow = lax.broadcasted_iota(jnp.int32, (8, META_LANES), 0)
    meta = jnp.where(mrow == 0, blk_e, jnp.where(mrow == 1, valid, jnp.where(
        mrow == 2, n_blk, jnp.where(mrow == 3, nxt, 0.0))))
    meta_ref[...] = meta.astype(jnp.int32)

    base = (bstart * EXP_BLOCK).astype(jnp.int32)
    packed = dest_ref[...]
    eid = packed >> ROUTE_RANK_BITS
    row = packed & (2 ** ROUTE_RANK_BITS - 1)
    for e in range(MOE_EXPERTS):
        row = row + jnp.where(eid == e, base[e:e + 1, :], 0)
    dest_ref[...] = row


def _route_call(logits):
    last = N_ROUTE_TILES - 1
    return pl.pallas_call(
        _route_kernel,
        grid=(N_ROUTE_TILES + 1,),
        in_specs=[pl.BlockSpec((ROUTE_TM, ROUTER_LANES), lambda t: (jnp.minimum(t, last), 0))],
        out_specs=[pl.BlockSpec((MOE_TOPK, T_ALL), lambda t: (0, 0)),
                   pl.BlockSpec((ROUTE_TM, ROUTER_LANES), lambda t: (jnp.minimum(t, last), 0)),
                   pl.BlockSpec((8, META_LANES), lambda t: (0, 0))],
        out_shape=[jax.ShapeDtypeStruct((MOE_TOPK, T_ALL), jnp.int32),
                   jax.ShapeDtypeStruct((T_ALL, ROUTER_LANES), F32),
                   jax.ShapeDtypeStruct((8, META_LANES), jnp.int32)],
        scratch_shapes=[pltpu.VMEM((MOE_EXPERTS, 1), F32)],
        compiler_params=_params("arbitrary"),
        name="route",
    )(logits)


def _sc_mesh():
    return plsc.VectorSubcoreMesh(core_axis_name="c", subcore_axis_name="s")


def _sc_token_offset(j):
    wid = lax.axis_index("s") * SC_CORES + lax.axis_index("c")
    return pl.multiple_of(wid * SC_ROWS_PER_WORKER + j * SC_CHUNK, 8)


def _dispatch_body(h_hbm, d0_hbm, d1_hbm, out_hbm, i0_v, i1_v, rows_v, sem0, sem1, sem2):
    @pl.loop(0, SC_ROWS_PER_WORKER // SC_CHUNK)
    def _(j):
        off = _sc_token_offset(j)
        c0 = pltpu.async_copy(d0_hbm.at[pl.ds(off, SC_CHUNK)], i0_v, sem0)
        c1 = pltpu.async_copy(d1_hbm.at[pl.ds(off, SC_CHUNK)], i1_v, sem1)
        c2 = pltpu.async_copy(h_hbm.at[pl.ds(off, SC_CHUNK)], rows_v, sem2)
        c0.wait()
        c1.wait()
        c2.wait()
        s0 = pltpu.async_copy(rows_v, out_hbm.at[i0_v], sem0)
        s1 = pltpu.async_copy(rows_v, out_hbm.at[i1_v], sem1)
        s0.wait()
        s1.wait()


def _dispatch_call(h2, dest0, dest1):
    return pl.kernel(
        _dispatch_body,
        out_type=jax.ShapeDtypeStruct((P_ROWS, DP), U32),
        mesh=_sc_mesh(),
        scratch_types=[pltpu.VMEM((SC_CHUNK,), jnp.int32), pltpu.VMEM((SC_CHUNK,), jnp.int32),
                       pltpu.VMEM((SC_CHUNK, DP), U32),
                       pltpu.SemaphoreType.DMA, pltpu.SemaphoreType.DMA, pltpu.SemaphoreType.DMA],
        name="moe_dispatch",
    )(h2, dest0, dest1)


def _combine_body(yb_hbm, d0_hbm, d1_hbm, o0_hbm, o1_hbm, i0_v, i1_v, r0_v, r1_v, sem0, sem1):
    @pl.loop(0, SC_ROWS_PER_WORKER // SC_CHUNK)
    def _(j):
        off = _sc_token_offset(j)
        c0 = pltpu.async_copy(d0_hbm.at[pl.ds(off, SC_CHUNK)], i0_v, sem0)
        c1 = pltpu.async_copy(d1_hbm.at[pl.ds(off, SC_CHUNK)], i1_v, sem1)
        c0.wait()
        c1.wait()
        g0 = pltpu.async_copy(yb_hbm.at[i0_v], r0_v, sem0)
        g1 = pltpu.async_copy(yb_hbm.at[i1_v], r1_v, sem1)
        g0.wait()
        g1.wait()
        w0 = pltpu.async_copy(r0_v, o0_hbm.at[pl.ds(off, SC_CHUNK)], sem0)
        w1 = pltpu.async_copy(r1_v, o1_hbm.at[pl.ds(off, SC_CHUNK)], sem1)
        w0.wait()
        w1.wait()


def _combine_call(yb, dest0, dest1):
    out = jax.ShapeDtypeStruct((T_ALL, DP), U32)
    return pl.kernel(
        _combine_body,
        out_type=(out, out),
        mesh=_sc_mesh(),
        scratch_types=[pltpu.VMEM((SC_CHUNK,), jnp.int32), pltpu.VMEM((SC_CHUNK,), jnp.int32),
                       pltpu.VMEM((SC_CHUNK, DP), U32), pltpu.VMEM((SC_CHUNK, DP), U32),
                       pltpu.SemaphoreType.DMA, pltpu.SemaphoreType.DMA],
        name="moe_combine",
    )(yb, dest0, dest1)


def _moe_rows(h2, dest, meta, w1, w3, w2, layer):
    dest0, dest1 = dest[0], dest[1]
    xb = _dispatch_call(h2, dest0, dest1)
    yb = _expert_call(meta[0, :N_EXP_BLOCKS], meta[1, :N_EXP_BLOCKS], meta[2, :1], meta[3, :N_EXP_BLOCKS],
                      xb, w1, w3, w2, layer)
    return _combine_call(yb, dest0, dest1)


def _final_kernel(is_prompt, x_ref, yg0_ref, yg1_ref, gates_ref, modp_ref, mods_ref, gain_ref, o_ref):
    x = _add_moe(x_ref[...], yg0_ref, yg1_ref, gates_ref, _tile_mod((modp_ref, mods_ref), is_prompt))
    o_ref[...] = _rms(x) * gain_ref[...]


def _final_call(x, prev, gain, row0, n_rows):
    rows = ROWS_WIDE
    tile0 = row0 // rows
    is_prompt = row0 < T_PROMPT
    assert row0 + n_rows <= T_PROMPT or not is_prompt

    def tile(width):
        return pl.BlockSpec((rows, width), lambda i: (tile0 + i, 0))

    mod_specs = [pl.BlockSpec(s.block_shape, lambda i, m=s.index_map: m(tile0 + i)) for s in _mod_specs(rows)]
    return pl.pallas_call(
        functools.partial(_final_kernel, is_prompt),
        grid=(n_rows // rows,),
        in_specs=[tile(D), tile(DP), tile(DP), tile(ROUTER_LANES), *mod_specs, _resident((1, D))],
        out_specs=pl.BlockSpec((rows, D), lambda i: (i, 0)),
        out_shape=jax.ShapeDtypeStruct((n_rows, D), F32),
        compiler_params=_params("parallel"),
        name="final_norm",
    )(x, *prev, gain)


def _rope_tables():
    pos = np.concatenate([np.tile(np.arange(SEQ), BATCH),
                          np.tile(PAST_LEN + np.arange(DEC_SEQ), DEC_BATCH)]).astype(np.float32)
    inv = (ROPE_BASE ** (-np.arange(ROPE_HALF, dtype=np.float32) / ROPE_HALF)).astype(np.float32)
    ang = (pos[:, None] * inv[None, :]).astype(np.float32).astype(np.float64)
    return jnp.asarray(np.cos(ang), F32), jnp.asarray(np.sin(ang), F32)


def kernel(x_prompt, x_sample, c_prompt, c_sample, state_ret, ada_w, ada_b, norm1_g, norm2_g, ret_w_in,
           ret_w_out, gm_w_in, gm_b_in, gm_ln_g, gm_ln_b, gm_w_s, gm_b_s, gm_w_out, gm_b_out, moe_w_rg,
           moe_b_rg, moe_w_re, moe_b_re, moe_w1, moe_w3, moe_w2, final_g):
    x = (x_prompt.reshape(T_PROMPT, D), x_sample.reshape(T_SAMPLE, D))
    c_all = jnp.concatenate([c_prompt, c_sample], axis=0)
    cos, sin = _rope_tables()
    dec = _ret_row_scales()

    mod_all = _ada_call(c_all, ada_w, ada_b).reshape(DEPTH, N_SEQ, 6, D)

    def layer_params(i):
        mod = (mod_all[i, :BATCH], mod_all[i, BATCH:])
        w_r = jnp.pad(jnp.concatenate([moe_w_re[i], moe_w_rg[i]], axis=1),
                      ((0, 0), (0, ROUTER_LANES - MOE_GROUPS - MOE_EXPERTS)))
        w_r_hi = w_r.astype(BF16)
        w_r_lo = (w_r - w_r_hi.astype(F32)).astype(BF16)
        w_r = jnp.concatenate([w_r_hi, w_r_lo], axis=1)
        b_r = jnp.pad(jnp.concatenate([moe_b_re[i].reshape(-1), moe_b_rg[i]]),
                      (0, ROUTER_LANES - MOE_GROUPS - MOE_EXPERTS)).reshape(1, ROUTER_LANES)
        return mod, w_r, b_r

    ret_prompt = ret_sample = gm_sample = None
    prev = None
    for i in range(DEPTH):
        j = i // 2
        mod, w_r, b_r = layer_params(i)
        g1 = norm1_g[i].reshape(1, D)
        g2 = norm2_g[i].reshape(1, D)
        if i % 2 == 0:
            x, p = _ret_proj_call(x, prev, mod, g1, ret_w_in, j, cos, sin, dec)
            y_p, ret_prompt = _ret_core_call(p, None, ret_prompt, j, BATCH, SEQ, RET_CHUNK_PROMPT, 0)
            y_s, ret_sample = _ret_core_call(p, state_ret, ret_sample, j, DEC_BATCH, DEC_SEQ,
                                             RET_CHUNK_SAMPLE, T_PROMPT)
            x, h2, logits = _ret_out_call(y_p, y_s, ret_w_out, j, x, mod, g2, w_r, b_r)
        else:
            x, u, v, gm_sample = _gm_proj_call(x, prev, mod, g1, gm_w_in, j, gm_b_in[j], gm_ln_g[j],
                                               gm_ln_b[j], gm_sample)
            ws_bd, bs_bd = _gm_block_diag(gm_w_s[j], gm_b_s[j])
            x, h2, logits = _gm_out_call(u, v, ws_bd, bs_bd, gm_w_out, j,
                                         gm_b_out[j].reshape(1, D), x, mod, g2, w_r, b_r)
        dest, gates, meta = _route_call(logits)
        yg0, yg1 = _moe_rows(h2, dest, meta, moe_w1, moe_w3, moe_w2, i)
        prev = (yg0, yg1, gates, *mod)

    fg = final_g.reshape(1, D)
    y_prompt = _final_call(x, prev, fg, 0, T_PROMPT).reshape(BATCH, SEQ, D)
    y_sample = _final_call(x, prev, fg, T_PROMPT, T_SAMPLE).reshape(DEC_BATCH, DEC_SEQ, D)
    return (y_prompt, y_sample, ret_prompt, ret_sample,
            gm_sample.reshape(N_GM, DEC_BATCH, DEC_SEQ, GM_HALF))
```

```python
import functools

import numpy as np
import jax
import jax.numpy as jnp
from jax import lax
from jax.experimental import pallas as pl
from jax.experimental.pallas import tpu as pltpu
from jax.experimental.pallas import tpu_sc as plsc

F32 = jnp.float32
BF16 = jnp.bfloat16
U32 = jnp.uint32

D = 1024
BATCH, SEQ = 4, 4096
DEC_BATCH, DEC_SEQ = 16, 64
PAST_LEN = 4096
DEPTH = 4
N_RET = (DEPTH + 1) // 2
N_GM = DEPTH // 2
N_SEQ = BATCH + DEC_BATCH

RET_HEADS, RET_DK, RET_DV = 4, 256, 512
RET_QK = RET_HEADS * RET_DK
RET_V = RET_HEADS * RET_DV
RET_IN = 2 * RET_QK + 2 * RET_V
ROPE_BASE = 10000.0
ROPE_HALF = RET_DK // 2

GM_FFN = 6 * D
GM_HALF = GM_FFN // 2
GM_GROUPS = 4
GM_GDIM = GM_HALF // GM_GROUPS
GM_CHUNK = 128

MOE_GROUPS, MOE_PER_GROUP = 4, 8
MOE_EXPERTS = MOE_GROUPS * MOE_PER_GROUP
MOE_TOPK = 2
MOE_HIDDEN = 512
EPS = 1e-6

GROUP = DEC_SEQ
T_PROMPT = BATCH * SEQ
T_SAMPLE = DEC_BATCH * DEC_SEQ
T_ALL = T_PROMPT + T_SAMPLE
N_GROUPS = T_ALL // GROUP
ROWS_WIDE = 512
ROWS_GM_PROJ = 256

RET_CHUNK_PROMPT = 256
RET_CHUNK_SAMPLE = DEC_SEQ

GM_MIX = 256

EXP_BLOCK = 512
EXP_ROW_STEP = 128
N_ASSIGN = T_ALL * MOE_TOPK
N_EXP_BLOCKS = -(-(N_ASSIGN + MOE_EXPERTS * (EXP_BLOCK - 1)) // EXP_BLOCK)
P_ROWS = N_EXP_BLOCKS * EXP_BLOCK
ROUTER_LANES = 128
ROUTE_TM = 1024
N_ROUTE_TILES = T_ALL // ROUTE_TM
META_LANES = 256
assert META_LANES >= N_EXP_BLOCKS

DP = D // 2
SC_CORES, SC_SUBCORES = 2, 16
SC_WORKERS = SC_CORES * SC_SUBCORES
SC_ROWS_PER_WORKER = T_ALL // SC_WORKERS
SC_CHUNK = 32
assert SC_ROWS_PER_WORKER % SC_CHUNK == 0 and SC_CHUNK % 8 == 0

V7X_VMEM_LIMIT_BYTES = 56 * 1024 * 1024


def _params(*sem):
    return pltpu.CompilerParams(dimension_semantics=sem, vmem_limit_bytes=V7X_VMEM_LIMIT_BYTES)


def _resident(shape):
    nd = len(shape)
    return pl.BlockSpec(shape, lambda *_: (0,) * nd, pipeline_mode=pl.Buffered(1))


WEIGHT_STAGE_BYTES = 3 * 1024 * 1024


def _weight_scratch(k, n):
    chunk = k
    while chunk * n * 4 > WEIGHT_STAGE_BYTES:
        assert chunk % 16 == 0
        chunk //= 2
    return [pltpu.VMEM((k, n), BF16), pltpu.VMEM((2, chunk, n), F32), pltpu.SemaphoreType.DMA((2,))]


def _load_weight_bf16(w_hbm, layer, w_s, stage, sems):
    k = w_s.shape[0]
    chunk = stage.shape[1]

    def copy(c):
        return pltpu.make_async_copy(w_hbm.at[layer, pl.ds(c * chunk, chunk)], stage.at[c % 2], sems.at[c % 2])

    @pl.when(pl.program_id(0) == 0)
    def _():
        copy(0).start()
        for c in range(k // chunk):
            if c + 1 < k // chunk:
                copy(c + 1).start()
            copy(c).wait()
            w_s[c * chunk:(c + 1) * chunk, :] = stage[c % 2].astype(BF16)


def _rms(x):
    return x * lax.rsqrt(jnp.mean(x * x, axis=-1, keepdims=True) + EPS)


def _silu(x):
    return x * jax.nn.sigmoid(x)


def _per_group(x2d, fn):
    rows = x2d.shape[0]
    return fn(x2d.reshape(rows // GROUP, GROUP, D)).reshape(rows, D)


def _tile_mod(mod_refs, is_prompt):
    modp_ref, mods_ref = mod_refs
    return jnp.where(is_prompt, jnp.broadcast_to(modp_ref[...], mods_ref.shape), mods_ref[...])


def _norm_mod(x, gain_ref, mod, shift_idx):
    y = _rms(x) * gain_ref[...]
    scale = mod[:, shift_idx + 1:shift_idx + 2, :]
    shift = mod[:, shift_idx:shift_idx + 1, :]
    return _per_group(y, lambda y3: y3 * (1.0 + scale) + shift)


def _pack_bf16_pairs(x):
    lo = lax.bitcast_convert_type(x[:, :DP].astype(BF16).astype(F32), U32)
    hi = lax.bitcast_convert_type(x[:, DP:].astype(BF16).astype(F32), U32)
    return (lo >> 16) | (hi & U32(0xFFFF0000))


def _unpack_bf16_pairs(w):
    lo = lax.bitcast_convert_type(w << 16, F32)
    hi = lax.bitcast_convert_type(w & U32(0xFFFF0000), F32)
    return jnp.concatenate([lo, hi], axis=1)


def _add_moe(x, yg0_ref, yg1_ref, gates_ref, mod_prev):
    g = gates_ref[...]
    y = g[:, 0:1] * _unpack_bf16_pairs(yg0_ref[...]) + g[:, 1:2] * _unpack_bf16_pairs(yg1_ref[...])
    gate2 = mod_prev[:, 5:6, :]
    return x + _per_group(y, lambda y3: y3 * gate2)


ADA_TN = 1536


def _ada_kernel(c_ref, w_ref, b_ref, o_ref):
    c = c_ref[...]
    s = _silu(c).astype(BF16)
    o_ref[0] = jnp.dot(s, w_ref[0].astype(BF16), preferred_element_type=F32) + b_ref[0]


def _ada_call(c_all, ada_w, ada_b):
    return pl.pallas_call(
        _ada_kernel,
        grid=(DEPTH, 6 * D // ADA_TN),
        in_specs=[
            pl.BlockSpec((N_SEQ, D), lambda i, j: (0, 0)),
            pl.BlockSpec((1, D, ADA_TN), lambda i, j: (i, 0, j)),
            pl.BlockSpec((1, 1, ADA_TN), lambda i, j: (i, 0, j)),
        ],
        out_specs=pl.BlockSpec((1, N_SEQ, ADA_TN), lambda i, j: (i, 0, j)),
        out_shape=jax.ShapeDtypeStruct((DEPTH, N_SEQ, 6 * D), F32),
        compiler_params=_params("parallel", "parallel"),
        name="ada_modulation",
    )(c_all, ada_w, ada_b.reshape(DEPTH, 1, 6 * D))


def _n_tiles(rows):
    return T_ALL // rows


def _n_prompt_tiles(rows):
    return T_PROMPT // rows


def _row_spec(rows, width):
    return pl.BlockSpec((rows, width), lambda i: (i, 0))


def _mod_specs(rows):
    npt = _n_prompt_tiles(rows)
    return [pl.BlockSpec((1, 6, D), lambda i: (jnp.minimum(i * rows // SEQ, BATCH - 1), 0, 0)),
            pl.BlockSpec((rows // DEC_SEQ, 6, D), lambda i: (jnp.maximum(i - npt, 0), 0, 0))]


def _prev_specs(rows):
    return [_row_spec(rows, DP), _row_spec(rows, DP), _row_spec(rows, ROUTER_LANES)] + _mod_specs(rows)


def _prompt_rows_spec(rows, width):
    last = _n_prompt_tiles(rows) - 1
    return pl.BlockSpec((rows, width), lambda i: (jnp.minimum(i, last), 0))


def _sample_rows_spec(rows, width):
    npt = _n_prompt_tiles(rows)
    return pl.BlockSpec((rows, width), lambda i: (jnp.maximum(i - npt, 0), 0))


def _ret_proj_kernel(has_prev, n_prompt_tiles, layer, *refs):
    is_prompt = pl.program_id(0) < n_prompt_tiles
    if has_prev:
        x_ref, yg0_ref, yg1_ref, gates_ref = refs[:4]
        x = _add_moe(x_ref[...], yg0_ref, yg1_ref, gates_ref, _tile_mod(refs[4:6], is_prompt))
        refs = refs[6:]
    else:
        x = jnp.where(is_prompt, refs[0][...], refs[1][...])
        refs = refs[2:]
    mod = _tile_mod(refs[:2], is_prompt)
    gain_ref, w_hbm, cos_ref, sin_ref, dec_ref, xo_ref, p_ref, w_s, w_stage, w_sems = refs[2:]
    _load_weight_bf16(w_hbm, layer, w_s, w_stage, w_sems)
    xo_ref[...] = x
    hb = _norm_mod(x, gain_ref, mod, 0).astype(BF16)
    cos = cos_ref[...]
    sin = sin_ref[...]
    dec = dec_ref[...]
    for j in range(2 * RET_HEADS):
        lo = j * RET_DK
        acc = jnp.dot(hb, w_s[:, lo:lo + RET_DK], preferred_element_type=F32)
        x1 = acc[:, :ROPE_HALF]
        x2 = acc[:, ROPE_HALF:]
        scale = dec[:, j:j + 1]
        p_ref[:, lo:lo + ROPE_HALF] = ((x1 * cos - x2 * sin) * scale).astype(BF16)
        p_ref[:, lo + ROPE_HALF:lo + RET_DK] = ((x1 * sin + x2 * cos) * scale).astype(BF16)
    for j in range(2 * RET_HEADS):
        lo = 2 * RET_QK + j * RET_DV
        acc = jnp.dot(hb, w_s[:, lo:lo + RET_DV], preferred_element_type=F32)
        if j >= RET_HEADS:
            acc = _silu(acc)
        p_ref[:, lo:lo + RET_DV] = acc.astype(BF16)


def _ret_proj_call(x, prev, mod, gain, w_in, layer, cos, sin, dec):
    rows = ROWS_WIDE
    has_prev = prev is not None
    if has_prev:
        in_specs = [_row_spec(rows, D)] + _prev_specs(rows)
        args = [x] + list(prev)
    else:
        in_specs = [_prompt_rows_spec(rows, D), _sample_rows_spec(rows, D)]
        args = list(x)
    in_specs += _mod_specs(rows) + [
        _resident((1, D)), pl.BlockSpec(memory_space=pl.ANY),
        _row_spec(rows, ROPE_HALF), _row_spec(rows, ROPE_HALF), _row_spec(rows, 2 * RET_HEADS)]
    args += [*mod, gain, w_in, cos, sin, dec]
    return pl.pallas_call(
        functools.partial(_ret_proj_kernel, has_prev, _n_prompt_tiles(rows), layer),
        grid=(_n_tiles(rows),),
        in_specs=in_specs,
        out_specs=[_row_spec(rows, D), _row_spec(rows, RET_IN)],
        out_shape=[jax.ShapeDtypeStruct((T_ALL, D), F32), jax.ShapeDtypeStruct((T_ALL, RET_IN), BF16)],
        scratch_shapes=_weight_scratch(D, RET_IN),
        compiler_params=_params("arbitrary"),
        name="ret_proj",
    )(*args)


def _ret_core_kernel(has_s0, n_chunks, layer, *refs):
    refs = list(refs)
    p_ref = refs.pop(0)
    s0_ref = refs.pop(0) if has_s0 else None
    causal_ref, cd_ref = refs[:2]
    y_ref, so_ref, s_ref = refs[-3:]
    c = pl.program_id(1)

    @pl.when(c == 0)
    def _():
        if has_s0:
            s_ref[...] = s0_ref[0, 0]
        else:
            s_ref[...] = jnp.zeros_like(s_ref)

    for h in range(RET_HEADS):
        qb = p_ref[:, h * RET_DK:(h + 1) * RET_DK]
        kb = p_ref[:, RET_QK + h * RET_DK:RET_QK + (h + 1) * RET_DK]
        vb = p_ref[:, 2 * RET_QK + h * RET_DV:2 * RET_QK + (h + 1) * RET_DV]
        gb = p_ref[:, 2 * RET_QK + RET_V + h * RET_DV:2 * RET_QK + RET_V + (h + 1) * RET_DV]
        scores = lax.dot_general(qb, kb, (((1,), (1,)), ((), ())), preferred_element_type=F32)
        scores = scores * causal_ref[...]
        s_old = s_ref[h]
        o = (jnp.dot(scores.astype(BF16), vb, preferred_element_type=F32)
             + jnp.dot(qb, s_old.astype(BF16), preferred_element_type=F32))
        s_ref[h] = cd_ref[h][:, 0:1] * (s_old + lax.dot_general(
            kb, vb, (((0,), (0,)), ((), ())), preferred_element_type=F32))
        y_ref[:, h * RET_DV:(h + 1) * RET_DV] = (gb.astype(F32) * _rms(o)).astype(BF16)

    @pl.when(c == n_chunks - 1)
    def _():
        so_ref[0, 0] = s_ref[...]
        if layer == 0:
            for later in range(1, N_RET):
                so_ref[later, 0] = jnp.zeros_like(s_ref)


def _ret_log_gamma():
    return np.log1p(-np.exp2(-5.0 - np.arange(RET_HEADS, dtype=np.float64)))


def _ret_chunk_tables(cl):
    idx = np.arange(cl)
    causal = (idx[:, None] >= idx[None, :]).astype(np.float32)
    cd = np.broadcast_to(np.exp(_ret_log_gamma() * cl)[:, None, None], (RET_HEADS, 1, 128))
    return jnp.asarray(causal, F32), jnp.asarray(cd, F32)


def _ret_row_scales():
    c = np.concatenate([np.arange(T_PROMPT) % RET_CHUNK_PROMPT,
                        np.arange(T_SAMPLE) % RET_CHUNK_SAMPLE]).astype(np.float64)
    e = (c[:, None] + 1.0) * _ret_log_gamma()[None, :]
    return jnp.asarray(np.concatenate([np.exp(e), np.exp(-e) * RET_DK ** -0.5], axis=1), F32)


def _ret_core_call(p, s0, states, layer, n_seq, seq_len, cl, row0):
    has_s0 = s0 is not None
    n_chunks = seq_len // cl
    rb0 = row0 // cl
    state = (RET_HEADS, RET_DK, RET_DV)
    in_specs = [pl.BlockSpec((cl, RET_IN), lambda b, c: (rb0 + b * n_chunks + c, 0))]
    args = [p]
    if has_s0:
        in_specs.append(pl.BlockSpec((1, 1) + state, lambda b, c: (layer, b, 0, 0, 0)))
        args.append(s0)
    in_specs += [_resident((cl, cl)), _resident((RET_HEADS, 1, 128))]
    args += list(_ret_chunk_tables(cl))
    if layer == 0:
        assert states is None
        state_spec = pl.BlockSpec((N_RET, 1) + state, lambda b, c: (0, b, 0, 0, 0))
        aliases = {}
    else:
        in_specs.append(pl.BlockSpec(memory_space=pl.ANY))
        args.append(states)
        state_spec = pl.BlockSpec((1, 1) + state, lambda b, c: (layer, b, 0, 0, 0))
        aliases = {len(args) - 1: 1}
    return pl.pallas_call(
        functools.partial(_ret_core_kernel, has_s0, n_chunks, layer),
        grid=(n_seq, n_chunks),
        in_specs=in_specs,
        out_specs=[pl.BlockSpec((cl, RET_V), lambda b, c: (b * n_chunks + c, 0)), state_spec],
        out_shape=[jax.ShapeDtypeStruct((n_seq * seq_len, RET_V), BF16),
                   jax.ShapeDtypeStruct((N_RET, n_seq) + state, F32)],
        scratch_shapes=[pltpu.VMEM(state, F32)],
        input_output_aliases=aliases,
        compiler_params=_params("parallel", "arbitrary"),
        name="ret_core",
    )(*args)


def _residual_router(acc, x_ref, mod, gain_ref, wr_ref, br_ref, xo_ref, h2_ref, lg_ref):
    gate1 = mod[:, 2:3, :]
    xn = x_ref[...] + _per_group(acc, lambda a3: a3 * gate1)
    xo_ref[...] = xn
    h2 = _norm_mod(xn, gain_ref, mod, 3)
    h2_ref[...] = _pack_bf16_pairs(h2)
    hh = jnp.dot(h2.astype(BF16), wr_ref[...], preferred_element_type=F32)
    lg_ref[...] = hh[:, :ROUTER_LANES] + hh[:, ROUTER_LANES:] + br_ref[...]


def _mix_out_specs(rows):
    return [_row_spec(rows, D), _row_spec(rows, DP), _row_spec(rows, ROUTER_LANES)]


_MIX_OUT_SHAPE = [
    jax.ShapeDtypeStruct((T_ALL, D), F32),
    jax.ShapeDtypeStruct((T_ALL, DP), U32),
    jax.ShapeDtypeStruct((T_ALL, ROUTER_LANES), F32),
]


def _router_specs():
    return [_resident((D, 2 * ROUTER_LANES)), _resident((1, ROUTER_LANES))]


def _ret_out_kernel(n_prompt_tiles, layer, yp_ref, ys_ref, w_hbm, x_ref, modp_ref, mods_ref, gain_ref,
                    wr_ref, br_ref, xo_ref, h2_ref, lg_ref, w_s, w_stage, w_sems):
    _load_weight_bf16(w_hbm, layer, w_s, w_stage, w_sems)
    is_prompt = pl.program_id(0) < n_prompt_tiles
    yin = jnp.where(is_prompt, yp_ref[...], ys_ref[...])
    acc = jnp.dot(yin, w_s[...], preferred_element_type=F32)
    mod = _tile_mod((modp_ref, mods_ref), is_prompt)
    _residual_router(acc, x_ref, mod, gain_ref, wr_ref, br_ref, xo_ref, h2_ref, lg_ref)


def _ret_out_call(y_prompt, y_sample, w_out, layer, x, mod, gain, w_r, b_r):
    rows = ROWS_WIDE
    return pl.pallas_call(
        functools.partial(_ret_out_kernel, _n_prompt_tiles(rows), layer),
        grid=(_n_tiles(rows),),
        in_specs=[_prompt_rows_spec(rows, RET_V), _sample_rows_spec(rows, RET_V),
                  pl.BlockSpec(memory_space=pl.ANY), _row_spec(rows, D), *_mod_specs(rows),
                  _resident((1, D))] + _router_specs(),
        out_specs=_mix_out_specs(rows),
        out_shape=_MIX_OUT_SHAPE,
        scratch_shapes=_weight_scratch(RET_V, D),
        compiler_params=_params("arbitrary"),
        name="ret_out",
    )(y_prompt, y_sample, w_out, x, *mod, gain, w_r, b_r)


GM_TN = 512


_GELU_C = float(np.sqrt(2.0 / np.pi))


def _gelu_tanh(x):
    hx = 0.5 * x
    return hx * jnp.tanh(x * (_GELU_C + (_GELU_C * 0.044715) * (x * x))) + hx


def _gm_proj_kernel(n_tiles, n_prompt_tiles, layer, *refs):
    x_ref, yg0_ref, yg1_ref, gates_ref = refs[:4]
    gain_ref, w_hbm, b_ref, lg_ref, lb_ref = refs[8:13]
    (xo_ref, u_ref, v_ref, vs_ref, hb_cur, hb_prev, vraw_cur, vraw_prev, stat_prev, sum_s,
     w_s, w_stage, w_sems) = refs[-13:]
    _load_weight_bf16(w_hbm, layer, w_s, w_stage, w_sems)
    i = pl.program_id(0)
    rows = x_ref.shape[0]

    @pl.when(i == 0)
    def _():
        hb_prev[...] = jnp.zeros_like(hb_prev)
        vraw_prev[...] = jnp.zeros_like(vraw_prev)
        stat_prev[...] = jnp.zeros_like(stat_prev)

    def add_row(x, row_ref, lo, width):
        x3 = x.reshape(rows // 8, 8, width) + row_ref[:, lo:lo + width]
        return x3.reshape(rows, width)

    def mul_row(x, row_ref, lo, width):
        x3 = x.reshape(rows // 8, 8, width) * row_ref[:, lo:lo + width]
        return x3.reshape(rows, width)

    def proj_chunk(hb_ref, lo):
        z = jnp.dot(hb_ref[...], w_s[:, lo:lo + GM_TN], preferred_element_type=F32)
        return add_row(z, b_ref, lo, GM_TN).astype(BF16)

    def stage_b_chunk(lo):
        u_ref[:, lo:lo + GM_TN] = _gelu_tanh(proj_chunk(hb_prev, lo))
        for k in range(lo, lo + GM_TN, 128):
            vk = vraw_prev[:, k:k + 128].astype(F32) * stat_prev[:, 0:128] + stat_prev[:, 128:256]
            vn = add_row(mul_row(vk, lg_ref, k, 128), lb_ref, k, 128)
            v_ref[:, k:k + 128] = vn.astype(BF16)
            vs_ref[0, :, k:k + 128] = vn

    def stage_a_chunk(n, lo):
        gz = _gelu_tanh(proj_chunk(hb_cur, GM_HALF + lo))
        vraw_cur[:, lo:lo + GM_TN] = gz
        gf = gz.astype(F32)
        pieces = [gf[:, k:k + 128] for k in range(0, GM_TN, 128)]
        t1 = functools.reduce(lambda p, q: p + q, pieces)
        t2 = functools.reduce(lambda p, q: p + q, [p * p for p in pieces])
        if n == 0:
            sum_s[:, 0:128] = t1
            sum_s[:, 128:256] = t2
        else:
            sum_s[:, 0:128] += t1
            sum_s[:, 128:256] += t2

    chunks = list(range(0, GM_HALF, GM_TN))
    stage_b_chunk(chunks[0])
    a_is_prompt = jnp.minimum(i, n_tiles - 1) < n_prompt_tiles
    x = _add_moe(x_ref[...], yg0_ref, yg1_ref, gates_ref, _tile_mod(refs[4:6], a_is_prompt))
    xo_ref[...] = x
    hb_cur[...] = _norm_mod(x, gain_ref, _tile_mod(refs[6:8], a_is_prompt), 0).astype(BF16)
    for n, lo in enumerate(chunks):
        stage_a_chunk(n, lo)
        if 0 < n < len(chunks) - 1:
            stage_b_chunk(lo)
    mu = jnp.sum(sum_s[:, 0:128], axis=-1, keepdims=True) * (1.0 / GM_HALF)
    var = jnp.sum(sum_s[:, 128:256], axis=-1, keepdims=True) * (1.0 / GM_HALF) - mu * mu
    rstd = lax.rsqrt(var + EPS)
    stage_b_chunk(chunks[-1])
    stat_prev[:, 0:128] = jnp.broadcast_to(rstd, (rows, 128))
    stat_prev[:, 128:256] = jnp.broadcast_to(-mu * rstd, (rows, 128))
    hb_prev[...] = hb_cur[...]
    vraw_prev[...] = vraw_cur[...]

    if layer == 0:
        @pl.when(i - 1 >= n_prompt_tiles)
        def _():
            for later in range(1, N_GM):
                vs_ref[later] = jnp.zeros((rows, GM_HALF), F32)


def _gm_proj_call(x, prev, mod, gain, w_in, layer, b_in, ln_g, ln_b, vs_all):
    rows = ROWS_GM_PROJ
    n, npt = _n_tiles(rows), _n_prompt_tiles(rows)

    def stage_a(spec):
        return pl.BlockSpec(spec.block_shape, lambda i, m=spec.index_map: m(jnp.minimum(i, n - 1)))

    def stage_b(spec):
        return pl.BlockSpec(spec.block_shape, lambda i, m=spec.index_map: m(jnp.maximum(i - 1, 0)))

    in_specs = [stage_a(s) for s in [_row_spec(rows, D)] + _prev_specs(rows) + _mod_specs(rows)] + [
        _resident((1, D)), pl.BlockSpec(memory_space=pl.ANY), _resident((8, GM_FFN)),
        _resident((8, GM_HALF)), _resident((8, GM_HALF))]
    rows8 = [jnp.broadcast_to(r.reshape(1, -1), (8, r.size)) for r in (b_in, ln_g, ln_b)]
    args = [x, *prev, *mod, gain, w_in, *rows8]
    if layer == 0:
        assert vs_all is None
        vs_spec = pl.BlockSpec((N_GM, rows, GM_HALF), lambda i: (0, jnp.maximum(i - 1 - npt, 0), 0))
        aliases = {}
    else:
        in_specs.append(pl.BlockSpec(memory_space=pl.ANY))
        args.append(vs_all)
        vs_spec = pl.BlockSpec((1, rows, GM_HALF), lambda i: (layer, jnp.maximum(i - 1 - npt, 0), 0))
        aliases = {len(args) - 1: 3}
    return pl.pallas_call(
        functools.partial(_gm_proj_kernel, n, npt, layer),
        grid=(n + 1,),
        in_specs=in_specs,
        out_specs=[stage_a(_row_spec(rows, D)), stage_b(_row_spec(rows, GM_HALF)),
                   stage_b(_row_spec(rows, GM_HALF)), vs_spec],
        out_shape=[jax.ShapeDtypeStruct((T_ALL, D), F32),
                   jax.ShapeDtypeStruct((T_ALL, GM_HALF), BF16),
                   jax.ShapeDtypeStruct((T_ALL, GM_HALF), BF16),
                   jax.ShapeDtypeStruct((N_GM, T_SAMPLE, GM_HALF), F32)],
        scratch_shapes=[pltpu.VMEM((rows, D), BF16), pltpu.VMEM((rows, D), BF16),
                        pltpu.VMEM((rows, GM_HALF), BF16), pltpu.VMEM((rows, GM_HALF), BF16),
                        pltpu.VMEM((rows, 256), F32), pltpu.VMEM((rows, 256), F32),
                        *_weight_scratch(D, GM_FFN)],
        input_output_aliases=aliases,
        compiler_params=_params("arbitrary"),
        name="gm_proj",
    )(*args)


def _gm_out_kernel(n_prompt_tiles, layer, u_ref, v_ref, ws_ref, bs_ref, w_hbm, bo_ref, x_ref, modp_ref,
                   mods_ref, gain_ref, wr_ref, br_ref, xo_ref, h2_ref, lg_ref, w_s, w_stage, w_sems):
    _load_weight_bf16(w_hbm, layer, w_s, w_stage, w_sems)
    rows = u_ref.shape[0]
    mod = _tile_mod((modp_ref, mods_ref), pl.program_id(0) < n_prompt_tiles)
    pieces = []
    for r0 in range(0, rows, GM_MIX):
        acc = jnp.zeros((GM_MIX, D), F32)
        for g in range(GM_GROUPS):
            lo = g * GM_GDIM
            sp = jnp.dot(ws_ref[0, g], v_ref[r0:r0 + GM_MIX, lo:lo + GM_GDIM],
                         preferred_element_type=F32) + bs_ref[0, g]
            gated = (u_ref[r0:r0 + GM_MIX, lo:lo + GM_GDIM].astype(F32) * sp).astype(BF16)
            acc = acc + jnp.dot(gated, w_s[lo:lo + GM_GDIM, :], preferred_element_type=F32)
        pieces.append(acc)
    acc = jnp.concatenate(pieces, axis=0) + bo_ref[...]
    _residual_router(acc, x_ref, mod, gain_ref, wr_ref, br_ref, xo_ref, h2_ref, lg_ref)


def _gm_block_diag(w_s, b_s):
    mats, biases = [], []
    for cl in (GM_CHUNK, DEC_SEQ):
        tri = jnp.tril(jnp.ones((cl, cl), bool))
        blk = jnp.where(tri[None], w_s[:, :cl, :cl], 0.0)
        reps = GM_MIX // cl
        eye = jnp.eye(reps, dtype=w_s.dtype)
        bd = jnp.einsum("ab,gts->gatbs", eye, blk).reshape(GM_GROUPS, GM_MIX, GM_MIX)
        mats.append(bd)
        biases.append(jnp.tile(b_s[:, :cl], (1, reps))[:, :, None])
    return jnp.stack(mats).astype(BF16), jnp.stack(biases).astype(F32)


def _gm_out_call(u, v, ws_bd, bs_bd, w_out, layer, b_out, x, mod, gain, w_r, b_r):
    rows = ROWS_WIDE
    npt = _n_prompt_tiles(rows)

    def variant(i):
        return jnp.where(i >= npt, 1, 0)

    return pl.pallas_call(
        functools.partial(_gm_out_kernel, npt, layer),
        grid=(_n_tiles(rows),),
        in_specs=[_row_spec(rows, GM_HALF), _row_spec(rows, GM_HALF),
                  pl.BlockSpec((1, GM_GROUPS, GM_MIX, GM_MIX), lambda i: (variant(i), 0, 0, 0)),
                  pl.BlockSpec((1, GM_GROUPS, GM_MIX, 1), lambda i: (variant(i), 0, 0, 0)),
                  pl.BlockSpec(memory_space=pl.ANY), _resident((1, D)), _row_spec(rows, D),
                  *_mod_specs(rows), _resident((1, D))] + _router_specs(),
        out_specs=_mix_out_specs(rows),
        out_shape=_MIX_OUT_SHAPE,
        scratch_shapes=_weight_scratch(GM_HALF, D),
        compiler_params=_params("arbitrary"),
        name="gm_out",
    )(u, v, ws_bd, bs_bd, w_out, b_out, x, *mod, gain, w_r, b_r)


def _expert_kernel(layer, be_ref, bv_ref, nb_ref, nx_ref, xb_ref, w1_hbm, w3_hbm, w2_hbm, yb_ref,
                   st1, st3, st2, w1s, w3s, w2s, sems, slot_ref):
    b = pl.program_id(0)

    def weight_copies(e, slot):
        return (pltpu.make_async_copy(w1_hbm.at[layer, e], st1.at[slot], sems.at[slot, 0]),
                pltpu.make_async_copy(w3_hbm.at[layer, e], st3.at[slot], sems.at[slot, 1]),
                pltpu.make_async_copy(w2_hbm.at[layer, e], st2.at[slot], sems.at[slot, 2]))

    @pl.when(b == 0)
    def _():
        slot_ref[0] = 0
        for cp in weight_copies(be_ref[0], 0):
            cp.start()

    @pl.when(b < nb_ref[0])
    def _():
        prev_e = be_ref[jnp.maximum(b - 1, 0)]

        @pl.when((b == 0) | (be_ref[b] != prev_e))
        def _():
            slot = slot_ref[0]
            for cp in weight_copies(be_ref[b], slot):
                cp.wait()
            w1s[...] = st1[slot].astype(BF16)
            w3s[...] = st3[slot].astype(BF16)
            w2s[...] = st2[slot].astype(BF16)

            @pl.when(nx_ref[b] >= 0)
            def _():
                for cp in weight_copies(nx_ref[b], 1 - slot):
                    cp.start()

            slot_ref[0] = 1 - slot

        valid = bv_ref[b]

        def run_rows(n):
            row = lax.broadcasted_iota(jnp.int32, (n, 1), 0)
            xw = jnp.where(row < valid, xb_ref[0:n], U32(0))
            x = _unpack_bf16_pairs(xw).astype(BF16)
            a = jnp.dot(x, w1s[...], preferred_element_type=F32)
            c = jnp.dot(x, w3s[...], preferred_element_type=F32)
            h = (_silu(a) * c).astype(BF16)
            yb_ref[0:n] = _pack_bf16_pairs(jnp.dot(h, w2s[...], preferred_element_type=F32))
            if n < EXP_BLOCK:
                yb_ref[n:EXP_BLOCK] = jnp.zeros((EXP_BLOCK - n, DP), U32)

        for n in range(EXP_ROW_STEP, EXP_BLOCK + 1, EXP_ROW_STEP):
            pl.when((valid > n - EXP_ROW_STEP) & (valid <= n))(functools.partial(run_rows, n))


def _expert_call(blk_e, blk_valid, n_blk, blk_next, xb, w1, w3, w2, layer):
    def blk(b, be, bv, nb, nx):
        return (jnp.minimum(b, nb[0] - 1), 0)

    up, down = (D, MOE_HIDDEN), (MOE_HIDDEN, D)
    grid_spec = pltpu.PrefetchScalarGridSpec(
        num_scalar_prefetch=4,
        grid=(N_EXP_BLOCKS,),
        in_specs=[pl.BlockSpec((EXP_BLOCK, DP), blk)] + [pl.BlockSpec(memory_space=pl.ANY)] * 3,
        out_specs=pl.BlockSpec((EXP_BLOCK, DP), blk),
        scratch_shapes=[pltpu.VMEM((2,) + up, F32), pltpu.VMEM((2,) + up, F32), pltpu.VMEM((2,) + down, F32),
                        pltpu.VMEM(up, BF16), pltpu.VMEM(up, BF16), pltpu.VMEM(down, BF16),
                        pltpu.SemaphoreType.DMA((2, 3)), pltpu.SMEM((1,), jnp.int32)],
    )
    return pl.pallas_call(
        functools.partial(_expert_kernel, layer),
        grid_spec=grid_spec,
        out_shape=jax.ShapeDtypeStruct((P_ROWS, DP), U32),
        compiler_params=_params("arbitrary"),
        name="experts",
    )(blk_e, blk_valid, n_blk, blk_next, xb, w1, w3, w2)


def _route_kernel(lg_ref, dest_ref, gates_ref, meta_ref, cnt_ref):
    t = pl.program_id(0)

    @pl.when(t == 0)
    def _():
        cnt_ref[...] = jnp.zeros_like(cnt_ref)

    pl.when(t < N_ROUTE_TILES)(functools.partial(_route_tile, t, lg_ref, dest_ref, gates_ref, cnt_ref))
    pl.when(t == N_ROUTE_TILES)(functools.partial(_route_finish, dest_ref, meta_ref, cnt_ref))


ROUTE_RANK_BITS = 16
assert P_ROWS < 2 ** ROUTE_RANK_BITS


def _route_tile(t, lg_ref, dest_ref, gates_ref, cnt_ref):
    tm = ROUTE_TM
    lt = lg_ref[...].T
    el = lt[0:MOE_EXPERTS]
    gl = lt[MOE_EXPERTS:MOE_EXPERTS + 8]
    gidx = lax.broadcasted_iota(jnp.int32, (8, tm), 0)
    neg = jnp.float32(-jnp.inf)
    gl = jnp.where(gidx < MOE_GROUPS, gl, neg)
    gmax = jnp.max(gl, axis=0, keepdims=True)
    grp = jnp.min(jnp.where(gl == gmax, gidx, MOE_GROUPS), axis=0, keepdims=True)
    eidx = lax.broadcasted_iota(jnp.int32, (MOE_EXPERTS, tm), 0)
    els = jnp.where((eidx >> 3) == grp, el, neg)
    m1 = jnp.max(els, axis=0, keepdims=True)
    i1 = jnp.min(jnp.where(els == m1, eidx, MOE_EXPERTS), axis=0, keepdims=True)
    els2 = jnp.where(eidx == i1, neg, els)
    m2 = jnp.max(els2, axis=0, keepdims=True)
    i2 = jnp.min(jnp.where(els2 == m2, eidx, MOE_EXPERTS), axis=0, keepdims=True)
    sel1 = eidx == i1
    sel2 = eidx == i2
    cnt = jnp.where(sel1 | sel2, 1.0, 0.0)

    lane = ROUTER_LANES
    before = (lax.broadcasted_iota(jnp.int32, (lane, lane), 0)
              < lax.broadcasted_iota(jnp.int32, (lane, lane), 1))
    tri = jnp.where(before, 1.0, 0.0).astype(BF16)
    run = cnt_ref[...]
    r1, r2 = [], []
    for k in range(tm // lane):
        piece = slice(k * lane, (k + 1) * lane)
        ck = cnt[:, piece]
        pos = run + jnp.dot(ck.astype(BF16), tri, preferred_element_type=F32)
        r1.append(jnp.sum(jnp.where(sel1[:, piece], pos, 0.0), axis=0, keepdims=True))
        r2.append(jnp.sum(jnp.where(sel2[:, piece], pos, 0.0), axis=0, keepdims=True))
        run = run + jnp.sum(ck, axis=1, keepdims=True)
    cnt_ref[...] = run
    rank = jnp.concatenate([jnp.concatenate(r1, axis=1), jnp.concatenate(r2, axis=1)], axis=0)
    eid = jnp.concatenate([i1, i2], axis=0)
    dest_ref[:, pl.ds(pl.multiple_of(t * tm, tm), tm)] = (eid << ROUTE_RANK_BITS) + rank.astype(jnp.int32)

    g_w = 1.0 / jnp.sum(jnp.exp(gl - gmax), axis=0, keepdims=True)
    e21 = jnp.exp(m2 - m1)
    p1 = 1.0 / (1.0 + e21)
    rid = lax.broadcasted_iota(jnp.int32, (ROUTER_LANES, tm), 0)
    gt = jnp.where(rid == 0, g_w * p1, jnp.where(rid == 1, g_w * (e21 * p1), 0.0))
    gates_ref[...] = gt.T


def _route_finish(dest_ref, meta_ref, cnt_ref):
    counts = cnt_ref[...]
    nblk = jnp.floor((counts + (EXP_BLOCK - 1.0)) * (1.0 / EXP_BLOCK))
    r = lax.broadcasted_iota(jnp.int32, (MOE_EXPERTS, MOE_EXPERTS), 0)
    c = lax.broadcasted_iota(jnp.int32, (MOE_EXPERTS, MOE_EXPERTS), 1)
    nblk_row = jnp.sum(jnp.where(r == c, nblk, 0.0), axis=0, keepdims=True)
    bstart = jnp.sum(jnp.where(c < r, nblk_row, 0.0), axis=1, keepdims=True)
    bend = bstart + nblk
    bidx = lax.broadcasted_iota(jnp.int32, (1, META_LANES), 1).astype(F32)
    blk_e = jnp.minimum(jnp.sum(jnp.where(bidx >= bend, 1.0, 0.0), axis=0, keepdims=True),
                        MOE_EXPERTS - 1.0)
    erow = lax.broadcasted_iota(jnp.int32, (MOE_EXPERTS, META_LANES), 0).astype(F32)
    mine = erow == blk_e
    cnt_b = jnp.sum(jnp.where(mine, counts, 0.0), axis=0, keepdims=True)
    start_b = jnp.sum(jnp.where(mine, bstart, 0.0), axis=0, keepdims=True)
    valid = jnp.clip(cnt_b - (bidx - start_b) * EXP_BLOCK, 0.0, float(EXP_BLOCK))
    n_blk = jnp.sum(nblk, axis=0, keepdims=True)
    end_b = jnp.sum(jnp.where(mine, bend, 0.0), axis=0, keepdims=True)
    nxt = jnp.minimum(jnp.sum(jnp.where(end_b >= bend, 1.0, 0.0), axis=0, keepdims=True),
                      MOE_EXPERTS - 1.0)
    nxt = jnp.where(end_b < n_blk, nxt, -1.0)
    mrow = lax.broadcasted_iota(jnp.int32, (8, META_LANES), 0)
    meta = jnp.where(mrow == 0, blk_e, jnp.where(mrow == 1, valid, jnp.where(
        mrow == 2, n_blk, jnp.where(mrow == 3, nxt, 0.0))))
    meta_ref[...] = meta.astype(jnp.int32)

    base = (bstart * EXP_BLOCK).astype(jnp.int32)
    packed = dest_ref[...]
    eid = packed >> ROUTE_RANK_BITS
    row = packed & (2 ** ROUTE_RANK_BITS - 1)
    for e in range(MOE_EXPERTS):
        row = row + jnp.where(eid == e, base[e:e + 1, :], 0)
    dest_ref[...] = row


def _route_call(logits):
    last = N_ROUTE_TILES - 1
    return pl.pallas_call(
        _route_kernel,
        grid=(N_ROUTE_TILES + 1,),
        in_specs=[pl.BlockSpec((ROUTE_TM, ROUTER_LANES), lambda t: (jnp.minimum(t, last), 0))],
        out_specs=[pl.BlockSpec((MOE_TOPK, T_ALL), lambda t: (0, 0)),
                   pl.BlockSpec((ROUTE_TM, ROUTER_LANES), lambda t: (jnp.minimum(t, last), 0)),
                   pl.BlockSpec((8, META_LANES), lambda t: (0, 0))],
        out_shape=[jax.ShapeDtypeStruct((MOE_TOPK, T_ALL), jnp.int32),
                   jax.ShapeDtypeStruct((T_ALL, ROUTER_LANES), F32),
                   jax.ShapeDtypeStruct((8, META_LANES), jnp.int32)],
        scratch_shapes=[pltpu.VMEM((MOE_EXPERTS, 1), F32)],
        compiler_params=_params("arbitrary"),
        name="route",
    )(logits)


def _sc_mesh():
    return plsc.VectorSubcoreMesh(core_axis_name="c", subcore_axis_name="s")


def _sc_token_offset(j):
    wid = lax.axis_index("s") * SC_CORES + lax.axis_index("c")
    return pl.multiple_of(wid * SC_ROWS_PER_WORKER + j * SC_CHUNK, 8)


def _dispatch_body(h_hbm, d0_hbm, d1_hbm, out_hbm, i0_v, i1_v, rows_v, sem0, sem1, sem2):
    @pl.loop(0, SC_ROWS_PER_WORKER // SC_CHUNK)
    def _(j):
        off = _sc_token_offset(j)
        c0 = pltpu.async_copy(d0_hbm.at[pl.ds(off, SC_CHUNK)], i0_v, sem0)
        c1 = pltpu.async_copy(d1_hbm.at[pl.ds(off, SC_CHUNK)], i1_v, sem1)
        c2 = pltpu.async_copy(h_hbm.at[pl.ds(off, SC_CHUNK)], rows_v, sem2)
        c0.wait()
        c1.wait()
        c2.wait()
        s0 = pltpu.async_copy(rows_v, out_hbm.at[i0_v], sem0)
        s1 = pltpu.async_copy(rows_v, out_hbm.at[i1_v], sem1)
        s0.wait()
        s1.wait()


def _dispatch_call(h2, dest0, dest1):
    return pl.kernel(
        _dispatch_body,
        out_type=jax.ShapeDtypeStruct((P_ROWS, DP), U32),
        mesh=_sc_mesh(),
        scratch_types=[pltpu.VMEM((SC_CHUNK,), jnp.int32), pltpu.VMEM((SC_CHUNK,), jnp.int32),
                       pltpu.VMEM((SC_CHUNK, DP), U32),
                       pltpu.SemaphoreType.DMA, pltpu.SemaphoreType.DMA, pltpu.SemaphoreType.DMA],
        name="moe_dispatch",
    )(h2, dest0, dest1)


def _combine_body(yb_hbm, d0_hbm, d1_hbm, o0_hbm, o1_hbm, i0_v, i1_v, r0_v, r1_v, sem0, sem1):
    @pl.loop(0, SC_ROWS_PER_WORKER // SC_CHUNK)
    def _(j):
        off = _sc_token_offset(j)
        c0 = pltpu.async_copy(d0_hbm.at[pl.ds(off, SC_CHUNK)], i0_v, sem0)
        c1 = pltpu.async_copy(d1_hbm.at[pl.ds(off, SC_CHUNK)], i1_v, sem1)
        c0.wait()
        c1.wait()
        g0 = pltpu.async_copy(yb_hbm.at[i0_v], r0_v, sem0)
        g1 = pltpu.async_copy(yb_hbm.at[i1_v], r1_v, sem1)
        g0.wait()
        g1.wait()
        w0 = pltpu.async_copy(r0_v, o0_hbm.at[pl.ds(off, SC_CHUNK)], sem0)
        w1 = pltpu.async_copy(r1_v, o1_hbm.at[pl.ds(off, SC_CHUNK)], sem1)
        w0.wait()
        w1.wait()


def _combine_call(yb, dest0, dest1):
    out = jax.ShapeDtypeStruct((T_ALL, DP), U32)
    return pl.kernel(
        _combine_body,
        out_type=(out, out),
        mesh=_sc_mesh(),
        scratch_types=[pltpu.VMEM((SC_CHUNK,), jnp.int32), pltpu.VMEM((SC_CHUNK,), jnp.int32),
                       pltpu.VMEM((SC_CHUNK, DP), U32), pltpu.VMEM((SC_CHUNK, DP), U32),
                       pltpu.SemaphoreType.DMA, pltpu.SemaphoreType.DMA],
        name="moe_combine",
    )(yb, dest0, dest1)


def _moe_rows(h2, dest, meta, w1, w3, w2, layer):
    dest0, dest1 = dest[0], dest[1]
    xb = _dispatch_call(h2, dest0, dest1)
    yb = _expert_call(meta[0, :N_EXP_BLOCKS], meta[1, :N_EXP_BLOCKS], meta[2, :1], meta[3, :N_EXP_BLOCKS],
                      xb, w1, w3, w2, layer)
    return _combine_call(yb, dest0, dest1)


def _final_kernel(is_prompt, x_ref, yg0_ref, yg1_ref, gates_ref, modp_ref, mods_ref, gain_ref, o_ref):
    x = _add_moe(x_ref[...], yg0_ref, yg1_ref, gates_ref, _tile_mod((modp_ref, mods_ref), is_prompt))
    o_ref[...] = _rms(x) * gain_ref[...]


def _final_call(x, prev, gain, row0, n_rows):
    rows = ROWS_WIDE
    tile0 = row0 // rows
    is_prompt = row0 < T_PROMPT
    assert row0 + n_rows <= T_PROMPT or not is_prompt

    def tile(width):
        return pl.BlockSpec((rows, width), lambda i: (tile0 + i, 0))

    mod_specs = [pl.BlockSpec(s.block_shape, lambda i, m=s.index_map: m(tile0 + i)) for s in _mod_specs(rows)]
    return pl.pallas_call(
        functools.partial(_final_kernel, is_prompt),
        grid=(n_rows // rows,),
        in_specs=[tile(D), tile(DP), tile(DP), tile(ROUTER_LANES), *mod_specs, _resident((1, D))],
        out_specs=pl.BlockSpec((rows, D), lambda i: (i, 0)),
        out_shape=jax.ShapeDtypeStruct((n_rows, D), F32),
        compiler_params=_params("parallel"),
        name="final_norm",
    )(x, *prev, gain)


def _rope_tables():
    pos = np.concatenate([np.tile(np.arange(SEQ), BATCH),
                          np.tile(PAST_LEN + np.arange(DEC_SEQ), DEC_BATCH)]).astype(np.float32)
    inv = (ROPE_BASE ** (-np.arange(ROPE_HALF, dtype=np.float32) / ROPE_HALF)).astype(np.float32)
    ang = (pos[:, None] * inv[None, :]).astype(np.float32).astype(np.float64)
    return jnp.asarray(np.cos(ang), F32), jnp.asarray(np.sin(ang), F32)


def kernel(x_prompt, x_sample, c_prompt, c_sample, state_ret, ada_w, ada_b, norm1_g, norm2_g, ret_w_in,
           ret_w_out, gm_w_in, gm_b_in, gm_ln_g, gm_ln_b, gm_w_s, gm_b_s, gm_w_out, gm_b_out, moe_w_rg,
           moe_b_rg, moe_w_re, moe_b_re, moe_w1, moe_w3, moe_w2, final_g):
    x = (x_prompt.reshape(T_PROMPT, D), x_sample.reshape(T_SAMPLE, D))
    c_all = jnp.concatenate([c_prompt, c_sample], axis=0)
    cos, sin = _rope_tables()
    dec = _ret_row_scales()

    mod_all = _ada_call(c_all, ada_w, ada_b).reshape(DEPTH, N_SEQ, 6, D)

    def layer_params(i):
        mod = (mod_all[i, :BATCH], mod_all[i, BATCH:])
        w_r = jnp.pad(jnp.concatenate([moe_w_re[i], moe_w_rg[i]], axis=1),
                      ((0, 0), (0, ROUTER_LANES - MOE_GROUPS - MOE_EXPERTS)))
        w_r_hi = w_r.astype(BF16)
        w_r_lo = (w_r - w_r_hi.astype(F32)).astype(BF16)
        w_r = jnp.concatenate([w_r_hi, w_r_lo], axis=1)
        b_r = jnp.pad(jnp.concatenate([moe_b_re[i].reshape(-1), moe_b_rg[i]]),
                      (0, ROUTER_LANES - MOE_GROUPS - MOE_EXPERTS)).reshape(1, ROUTER_LANES)
        return mod, w_r, b_r

    ret_prompt = ret_sample = gm_sample = None
    prev = None
    for i in range(DEPTH):
        j = i // 2
        mod, w_r, b_r = layer_params(i)
        g1 = norm1_g[i].reshape(1, D)
        g2 = norm2_g[i].reshape(1, D)
        if i % 2 == 0:
            x, p = _ret_proj_call(x, prev, mod, g1, ret_w_in, j, cos, sin, dec)
            y_p, ret_prompt = _ret_core_call(p, None, ret_prompt, j, BATCH, SEQ, RET_CHUNK_PROMPT, 0)
            y_s, ret_sample = _ret_core_call(p, state_ret, ret_sample, j, DEC_BATCH, DEC_SEQ,
                                             RET_CHUNK_SAMPLE, T_PROMPT)
            x, h2, logits = _ret_out_call(y_p, y_s, ret_w_out, j, x, mod, g2, w_r, b_r)
        else:
            x, u, v, gm_sample = _gm_proj_call(x, prev, mod, g1, gm_w_in, j, gm_b_in[j], gm_ln_g[j],
                                               gm_ln_b[j], gm_sample)
            ws_bd, bs_bd = _gm_block_diag(gm_w_s[j], gm_b_s[j])
            x, h2, logits = _gm_out_call(u, v, ws_bd, bs_bd, gm_w_out, j,
                                         gm_b_out[j].reshape(1, D), x, mod, g2, w_r, b_r)
        dest, gates, meta = _route_call(logits)
        yg0, yg1 = _moe_rows(h2, dest, meta, moe_w1, moe_w3, moe_w2, i)
        prev = (yg0, yg1, gates, *mod)

    fg = final_g.reshape(1, D)
    y_prompt = _final_call(x, prev, fg, 0, T_PROMPT).reshape(BATCH, SEQ, D)
    y_sample = _final_call(x, prev, fg, T_PROMPT, T_SAMPLE).reshape(DEC_BATCH, DEC_SEQ, D)
    return (y_prompt, y_sample, ret_prompt, ret_sample,
            gm_sample.reshape(N_GM, DEC_BATCH, DEC_SEQ, GM_HALF))
```

```python
import functools

import numpy as np
import jax
import jax.numpy as jnp
from jax import lax
from jax.experimental import pallas as pl
from jax.experimental.pallas import tpu as pltpu
from jax.experimental.pallas import tpu_sc as plsc

F32 = jnp.float32
BF16 = jnp.bfloat16
U32 = jnp.uint32

D = 1024
BATCH, SEQ = 4, 4096
DEC_BATCH, DEC_SEQ = 16, 64
PAST_LEN = 4096
DEPTH = 4
N_RET = (DEPTH + 1) // 2
N_GM = DEPTH // 2
N_SEQ = BATCH + DEC_BATCH

RET_HEADS, RET_DK, RET_DV = 4, 256, 512
RET_QK = RET_HEADS * RET_DK
RET_V = RET_HEADS * RET_DV
RET_IN = 2 * RET_QK + 2 * RET_V
ROPE_BASE = 10000.0
ROPE_HALF = RET_DK // 2

GM_FFN = 6 * D
GM_HALF = GM_FFN // 2
GM_GROUPS = 4
GM_GDIM = GM_HALF // GM_GROUPS
GM_CHUNK = 128

MOE_GROUPS, MOE_PER_GROUP = 4, 8
MOE_EXPERTS = MOE_GROUPS * MOE_PER_GROUP
MOE_TOPK = 2
MOE_HIDDEN = 512
EPS = 1e-6

GROUP = DEC_SEQ
T_PROMPT = BATCH * SEQ
T_SAMPLE = DEC_BATCH * DEC_SEQ
T_ALL = T_PROMPT + T_SAMPLE
N_GROUPS = T_ALL // GROUP
ROWS_WIDE = 512
ROWS_GM_PROJ = 256

RET_CHUNK_PROMPT = 256
RET_CHUNK_SAMPLE = DEC_SEQ

GM_MIX = 256

EXP_BLOCK = 512
EXP_ROW_STEP = 128
N_ASSIGN = T_ALL * MOE_TOPK
N_EXP_BLOCKS = -(-(N_ASSIGN + MOE_EXPERTS * (EXP_BLOCK - 1)) // EXP_BLOCK)
P_ROWS = N_EXP_BLOCKS * EXP_BLOCK
ROUTER_LANES = 128
ROUTE_TM = 1024
N_ROUTE_TILES = T_ALL // ROUTE_TM
META_LANES = 256
assert META_LANES >= N_EXP_BLOCKS

DP = D // 2
SC_CORES, SC_SUBCORES = 2, 16
SC_WORKERS = SC_CORES * SC_SUBCORES
SC_ROWS_PER_WORKER = T_ALL // SC_WORKERS
SC_CHUNK = 32
assert SC_ROWS_PER_WORKER % SC_CHUNK == 0 and SC_CHUNK % 8 == 0

V7X_VMEM_LIMIT_BYTES = 56 * 1024 * 1024


def _params(*sem):
    return pltpu.CompilerParams(dimension_semantics=sem, vmem_limit_bytes=V7X_VMEM_LIMIT_BYTES)


def _resident(shape):
    nd = len(shape)
    return pl.BlockSpec(shape, lambda *_: (0,) * nd, pipeline_mode=pl.Buffered(1))


WEIGHT_STAGE_BYTES = 3 * 1024 * 1024


def _weight_scratch(k, n):
    chunk = k
    while chunk * n * 4 > WEIGHT_STAGE_BYTES:
        assert chunk % 16 == 0
        chunk //= 2
    return [pltpu.VMEM((k, n), BF16), pltpu.VMEM((2, chunk, n), F32), pltpu.SemaphoreType.DMA((2,))]


def _load_weight_bf16(w_hbm, layer, w_s, stage, sems):
    k = w_s.shape[0]
    chunk = stage.shape[1]

    def copy(c):
        return pltpu.make_async_copy(w_hbm.at[layer, pl.ds(c * chunk, chunk)], stage.at[c % 2], sems.at[c % 2])

    @pl.when(pl.program_id(0) == 0)
    def _():
        copy(0).start()
        for c in range(k // chunk):
            if c + 1 < k // chunk:
                copy(c + 1).start()
            copy(c).wait()
            w_s[c * chunk:(c + 1) * chunk, :] = stage[c % 2].astype(BF16)


def _rms(x):
    return x * lax.rsqrt(jnp.mean(x * x, axis=-1, keepdims=True) + EPS)


def _silu(x):
    return x * jax.nn.sigmoid(x)


def _per_group(x2d, fn):
    rows = x2d.shape[0]
    return fn(x2d.reshape(rows // GROUP, GROUP, D)).reshape(rows, D)


def _tile_mod(mod_refs, is_prompt):
    modp_ref, mods_ref = mod_refs
    return jnp.where(is_prompt, jnp.broadcast_to(modp_ref[...], mods_ref.shape), mods_ref[...])


def _norm_mod(x, gain_ref, mod, shift_idx):
    y = _rms(x) * gain_ref[...]
    scale = mod[:, shift_idx + 1:shift_idx + 2, :]
    shift = mod[:, shift_idx:shift_idx + 1, :]
    return _per_group(y, lambda y3: y3 * (1.0 + scale) + shift)


def _pack_bf16_pairs(x):
    lo = lax.bitcast_convert_type(x[:, :DP].astype(BF16).astype(F32), U32)
    hi = lax.bitcast_convert_type(x[:, DP:].astype(BF16).astype(F32), U32)
    return (lo >> 16) | (hi & U32(0xFFFF0000))


def _unpack_bf16_pairs(w):
    lo = lax.bitcast_convert_type(w << 16, F32)
    hi = lax.bitcast_convert_type(w & U32(0xFFFF0000), F32)
    return jnp.concatenate([lo, hi], axis=1)


def _add_moe(x, yg0_ref, yg1_ref, gates_ref, mod_prev):
    g = gates_ref[...]
    y = g[:, 0:1] * _unpack_bf16_pairs(yg0_ref[...]) + g[:, 1:2] * _unpack_bf16_pairs(yg1_ref[...])
    gate2 = mod_prev[:, 5:6, :]
    return x + _per_group(y, lambda y3: y3 * gate2)


ADA_TN = 1536


def _ada_kernel(c_ref, w_ref, b_ref, o_ref):
    c = c_ref[...]
    s = _silu(c).astype(BF16)
    o_ref[0] = jnp.dot(s, w_ref[0].astype(BF16), preferred_element_type=F32) + b_ref[0]


def _ada_call(c_all, ada_w, ada_b):
    return pl.pallas_call(
        _ada_kernel,
        grid=(DEPTH, 6 * D // ADA_TN),
        in_specs=[
            pl.BlockSpec((N_SEQ, D), lambda i, j: (0, 0)),
            pl.BlockSpec((1, D, ADA_TN), lambda i, j: (i, 0, j)),
            pl.BlockSpec((1, 1, ADA_TN), lambda i, j: (i, 0, j)),
        ],
        out_specs=pl.BlockSpec((1, N_SEQ, ADA_TN), lambda i, j: (i, 0, j)),
        out_shape=jax.ShapeDtypeStruct((DEPTH, N_SEQ, 6 * D), F32),
        compiler_params=_params("parallel", "parallel"),
        name="ada_modulation",
    )(c_all, ada_w, ada_b.reshape(DEPTH, 1, 6 * D))


def _n_tiles(rows):
    return T_ALL // rows


def _n_prompt_tiles(rows):
    return T_PROMPT // rows


def _row_spec(rows, width):
    return pl.BlockSpec((rows, width), lambda i: (i, 0))


def _mod_specs(rows):
    npt = _n_prompt_tiles(rows)
    return [pl.BlockSpec((1, 6, D), lambda i: (jnp.minimum(i * rows // SEQ, BATCH - 1), 0, 0)),
            pl.BlockSpec((rows // DEC_SEQ, 6, D), lambda i: (jnp.maximum(i - npt, 0), 0, 0))]


def _prev_specs(rows):
    return [_row_spec(rows, DP), _row_spec(rows, DP), _row_spec(rows, ROUTER_LANES)] + _mod_specs(rows)


def _prompt_rows_spec(rows, width):
    last = _n_prompt_tiles(rows) - 1
    return pl.BlockSpec((rows, width), lambda i: (jnp.minimum(i, last), 0))


def _sample_rows_spec(rows, width):
    npt = _n_prompt_tiles(rows)
    return pl.BlockSpec((rows, width), lambda i: (jnp.maximum(i - npt, 0), 0))


def _ret_proj_kernel(has_prev, n_prompt_tiles, layer, *refs):
    is_prompt = pl.program_id(0) < n_prompt_tiles
    if has_prev:
        x_ref, yg0_ref, yg1_ref, gates_ref = refs[:4]
        x = _add_moe(x_ref[...], yg0_ref, yg1_ref, gates_ref, _tile_mod(refs[4:6], is_prompt))
        refs = refs[6:]
    else:
        x = jnp.where(is_prompt, refs[0][...], refs[1][...])
        refs = refs[2:]
    mod = _tile_mod(refs[:2], is_prompt)
    gain_ref, w_hbm, cos_ref, sin_ref, dec_ref, xo_ref, p_ref, w_s, w_stage, w_sems = refs[2:]
    _load_weight_bf16(w_hbm, layer, w_s, w_stage, w_sems)
    xo_ref[...] = x
    hb = _norm_mod(x, gain_ref, mod, 0).astype(BF16)
    cos = cos_ref[...]
    sin = sin_ref[...]
    dec = dec_ref[...]
    for j in range(2 * RET_HEADS):
        lo = j * RET_DK
        acc = jnp.dot(hb, w_s[:, lo:lo + RET_DK], preferred_element_type=F32)
        x1 = acc[:, :ROPE_HALF]
        x2 = acc[:, ROPE_HALF:]
        scale = dec[:, j:j + 1]
        p_ref[:, lo:lo + ROPE_HALF] = ((x1 * cos - x2 * sin) * scale).astype(BF16)
        p_ref[:, lo + ROPE_HALF:lo + RET_DK] = ((x1 * sin + x2 * cos) * scale).astype(BF16)
    for j in range(2 * RET_HEADS):
        lo = 2 * RET_QK + j * RET_DV
        acc = jnp.dot(hb, w_s[:, lo:lo + RET_DV], preferred_element_type=F32)
        if j >= RET_HEADS:
            acc = _silu(acc)
        p_ref[:, lo:lo + RET_DV] = acc.astype(BF16)


def _ret_proj_call(x, prev, mod, gain, w_in, layer, cos, sin, dec):
    rows = ROWS_WIDE
    has_prev = prev is not None
    if has_prev:
        in_specs = [_row_spec(rows, D)] + _prev_specs(rows)
        args = [x] + list(prev)
    else:
        in_specs = [_prompt_rows_spec(rows, D), _sample_rows_spec(rows, D)]
        args = list(x)
    in_specs += _mod_specs(rows) + [
        _resident((1, D)), pl.BlockSpec(memory_space=pl.ANY),
        _row_spec(rows, ROPE_HALF), _row_spec(rows, ROPE_HALF), _row_spec(rows, 2 * RET_HEADS)]
    args += [*mod, gain, w_in, cos, sin, dec]
    return pl.pallas_call(
        functools.partial(_ret_proj_kernel, has_prev, _n_prompt_tiles(rows), layer),
        grid=(_n_tiles(rows),),
        in_specs=in_specs,
        out_specs=[_row_spec(rows, D), _row_spec(rows, RET_IN)],
        out_shape=[jax.ShapeDtypeStruct((T_ALL, D), F32), jax.ShapeDtypeStruct((T_ALL, RET_IN), BF16)],
        scratch_shapes=_weight_scratch(D, RET_IN),
        compiler_params=_params("arbitrary"),
        name="ret_proj",
    )(*args)


def _ret_core_kernel(has_s0, n_chunks, layer, *refs):
    refs = list(refs)
    p_ref = refs.pop(0)
    s0_ref = refs.pop(0) if has_s0 else None
    causal_ref, cd_ref = refs[:2]
    y_ref, so_ref, s_ref = refs[-3:]
    c = pl.program_id(1)

    @pl.when(c == 0)
    def _():
        if has_s0:
            s_ref[...] = s0_ref[0, 0]
        else:
            s_ref[...] = jnp.zeros_like(s_ref)

    for h in range(RET_HEADS):
        qb = p_ref[:, h * RET_DK:(h + 1) * RET_DK]
        kb = p_ref[:, RET_QK + h * RET_DK:RET_QK + (h + 1) * RET_DK]
        vb = p_ref[:, 2 * RET_QK + h * RET_DV:2 * RET_QK + (h + 1) * RET_DV]
        gb = p_ref[:, 2 * RET_QK + RET_V + h * RET_DV:2 * RET_QK + RET_V + (h + 1) * RET_DV]
        scores = lax.dot_general(qb, kb, (((1,), (1,)), ((), ())), preferred_element_type=F32)
        scores = scores * causal_ref[...]
        s_old = s_ref[h]
        o = (jnp.dot(scores.astype(BF16), vb, preferred_element_type=F32)
             + jnp.dot(qb, s_old.astype(BF16), preferred_element_type=F32))
        s_ref[h] = cd_ref[h][:, 0:1] * (s_old + lax.dot_general(
            kb, vb, (((0,), (0,)), ((), ())), preferred_element_type=F32))
        y_ref[:, h * RET_DV:(h + 1) * RET_DV] = (gb.astype(F32) * _rms(o)).astype(BF16)

    @pl.when(c == n_chunks - 1)
    def _():
        so_ref[0, 0] = s_ref[...]
        if layer == 0:
            for later in range(1, N_RET):
                so_ref[later, 0] = jnp.zeros_like(s_ref)


def _ret_log_gamma():
    return np.log1p(-np.exp2(-5.0 - np.arange(RET_HEADS, dtype=np.float64)))


def _ret_chunk_tables(cl):
    idx = np.arange(cl)
    causal = (idx[:, None] >= idx[None, :]).astype(np.float32)
    cd = np.broadcast_to(np.exp(_ret_log_gamma() * cl)[:, None, None], (RET_HEADS, 1, 128))
    return jnp.asarray(causal, F32), jnp.asarray(cd, F32)


def _ret_row_scales():
    c = np.concatenate([np.arange(T_PROMPT) % RET_CHUNK_PROMPT,
                        np.arange(T_SAMPLE) % RET_CHUNK_SAMPLE]).astype(np.float64)
    e = (c[:, None] + 1.0) * _ret_log_gamma()[None, :]
    return jnp.asarray(np.concatenate([np.exp(e), np.exp(-e) * RET_DK ** -0.5], axis=1), F32)


def _ret_core_call(p, s0, states, layer, n_seq, seq_len, cl, row0):
    has_s0 = s0 is not None
    n_chunks = seq_len // cl
    rb0 = row0 // cl
    state = (RET_HEADS, RET_DK, RET_DV)
    in_specs = [pl.BlockSpec((cl, RET_IN), lambda b, c: (rb0 + b * n_chunks + c, 0))]
    args = [p]
    if has_s0:
        in_specs.append(pl.BlockSpec((1, 1) + state, lambda b, c: (layer, b, 0, 0, 0)))
        args.append(s0)
    in_specs += [_resident((cl, cl)), _resident((RET_HEADS, 1, 128))]
    args += list(_ret_chunk_tables(cl))
    if layer == 0:
        assert states is None
        state_spec = pl.BlockSpec((N_RET, 1) + state, lambda b, c: (0, b, 0, 0, 0))
        aliases = {}
    else:
        in_specs.append(pl.BlockSpec(memory_space=pl.ANY))
        args.append(states)
        state_spec = pl.BlockSpec((1, 1) + state, lambda b, c: (layer, b, 0, 0, 0))
        aliases = {len(args) - 1: 1}
    return pl.pallas_call(
        functools.partial(_ret_core_kernel, has_s0, n_chunks, layer),
        grid=(n_seq, n_chunks),
        in_specs=in_specs,
        out_specs=[pl.BlockSpec((cl, RET_V), lambda b, c: (b * n_chunks + c, 0)), state_spec],
        out_shape=[jax.ShapeDtypeStruct((n_seq * seq_len, RET_V), BF16),
                   jax.ShapeDtypeStruct((N_RET, n_seq) + state, F32)],
        scratch_shapes=[pltpu.VMEM(state, F32)],
        input_output_aliases=aliases,
        compiler_params=_params("parallel", "arbitrary"),
        name="ret_core",
    )(*args)


def _residual_router(acc, x_ref, mod, gain_ref, wr_ref, br_ref, xo_ref, h2_ref, lg_ref):
    gate1 = mod[:, 2:3, :]
    xn = x_ref[...] + _per_group(acc, lambda a3: a3 * gate1)
    xo_ref[...] = xn
    h2 = _norm_mod(xn, gain_ref, mod, 3)
    h2_ref[...] = _pack_bf16_pairs(h2)
    hh = jnp.dot(h2.astype(BF16), wr_ref[...], preferred_element_type=F32)
    lg_ref[...] = hh[:, :ROUTER_LANES] + hh[:, ROUTER_LANES:] + br_ref[...]


def _mix_out_specs(rows):
    return [_row_spec(rows, D), _row_spec(rows, DP), _row_spec(rows, ROUTER_LANES)]


_MIX_OUT_SHAPE = [
    jax.ShapeDtypeStruct((T_ALL, D), F32),
    jax.ShapeDtypeStruct((T_ALL, DP), U32),
    jax.ShapeDtypeStruct((T_ALL, ROUTER_LANES), F32),
]


def _router_specs():
    return [_resident((D, 2 * ROUTER_LANES)), _resident((1, ROUTER_LANES))]


def _ret_out_kernel(n_prompt_tiles, layer, yp_ref, ys_ref, w_hbm, x_ref, modp_ref, mods_ref, gain_ref,
                    wr_ref, br_ref, xo_ref, h2_ref, lg_ref, w_s, w_stage, w_sems):
    _load_weight_bf16(w_hbm, layer, w_s, w_stage, w_sems)
    is_prompt = pl.program_id(0) < n_prompt_tiles
    yin = jnp.where(is_prompt, yp_ref[...], ys_ref[...])
    acc = jnp.dot(yin, w_s[...], preferred_element_type=F32)
    mod = _tile_mod((modp_ref, mods_ref), is_prompt)
    _residual_router(acc, x_ref, mod, gain_ref, wr_ref, br_ref, xo_ref, h2_ref, lg_ref)


def _ret_out_call(y_prompt, y_sample, w_out, layer, x, mod, gain, w_r, b_r):
    rows = ROWS_WIDE
    return pl.pallas_call(
        functools.partial(_ret_out_kernel, _n_prompt_tiles(rows), layer),
        grid=(_n_tiles(rows),),
        in_specs=[_prompt_rows_spec(rows, RET_V), _sample_rows_spec(rows, RET_V),
                  pl.BlockSpec(memory_space=pl.ANY), _row_spec(rows, D), *_mod_specs(rows),
                  _resident((1, D))] + _router_specs(),
        out_specs=_mix_out_specs(rows),
        out_shape=_MIX_OUT_SHAPE,
        scratch_shapes=_weight_scratch(RET_V, D),
        compiler_params=_params("arbitrary"),
        name="ret_out",
    )(y_prompt, y_sample, w_out, x, *mod, gain, w_r, b_r)


GM_TN = 512


_GELU_C = float(np.sqrt(2.0 / np.pi))


def _gelu_tanh(x):
    hx = 0.5 * x
    return hx * jnp.tanh(x * (_GELU_C + (_GELU_C * 0.044715) * (x * x))) + hx


def _gm_proj_kernel(n_tiles, n_prompt_tiles, layer, *refs):
    x_ref, yg0_ref, yg1_ref, gates_ref = refs[:4]
    gain_ref, w_hbm, b_ref, lg_ref, lb_ref = refs[8:13]
    (xo_ref, u_ref, v_ref, vs_ref, hb_cur, hb_prev, vraw_cur, vraw_prev, stat_prev, sum_s,
     w_s, w_stage, w_sems) = refs[-13:]
    _load_weight_bf16(w_hbm, layer, w_s, w_stage, w_sems)
    i = pl.program_id(0)
    rows = x_ref.shape[0]

    @pl.when(i == 0)
    def _():
        hb_prev[...] = jnp.zeros_like(hb_prev)
        vraw_prev[...] = jnp.zeros_like(vraw_prev)
        stat_prev[...] = jnp.zeros_like(stat_prev)

    def add_row(x, row_ref, lo, width):
        x3 = x.reshape(rows // 8, 8, width) + row_ref[:, lo:lo + width]
        return x3.reshape(rows, width)

    def mul_row(x, row_ref, lo, width):
        x3 = x.reshape(rows // 8, 8, width) * row_ref[:, lo:lo + width]
        return x3.reshape(rows, width)

    def proj_chunk(hb_ref, lo):
        z = jnp.dot(hb_ref[...], w_s[:, lo:lo + GM_TN], preferred_element_type=F32)
        return add_row(z, b_ref, lo, GM_TN).astype(BF16)

    def stage_b_chunk(lo):
        u_ref[:, lo:lo + GM_TN] = _gelu_tanh(proj_chunk(hb_prev, lo))
        for k in range(lo, lo + GM_TN, 128):
            vk = vraw_prev[:, k:k + 128].astype(F32) * stat_prev[:, 0:128] + stat_prev[:, 128:256]
            vn = add_row(mul_row(vk, lg_ref, k, 128), lb_ref, k, 128)
            v_ref[:, k:k + 128] = vn.astype(BF16)
            vs_ref[0, :, k:k + 128] = vn

    def stage_a_chunk(n, lo):
        gz = _gelu_tanh(proj_chunk(hb_cur, GM_HALF + lo))
        vraw_cur[:, lo:lo + GM_TN] = gz
        gf = gz.astype(F32)
        pieces = [gf[:, k:k + 128] for k in range(0, GM_TN, 128)]
        t1 = functools.reduce(lambda p, q: p + q, pieces)
        t2 = functools.reduce(lambda p, q: p + q, [p * p for p in pieces])
        if n == 0:
            sum_s[:, 0:128] = t1
            sum_s[:, 128:256] = t2
        else:
            sum_s[:, 0:128] += t1
            sum_s[:, 128:256] += t2

    chunks = list(range(0, GM_HALF, GM_TN))
    stage_b_chunk(chunks[0])
    a_is_prompt = jnp.minimum(i, n_tiles - 1) < n_prompt_tiles
    x = _add_moe(x_ref[...], yg0_ref, yg1_ref, gates_ref, _tile_mod(refs[4:6], a_is_prompt))
    xo_ref[...] = x
    hb_cur[...] = _norm_mod(x, gain_ref, _tile_mod(refs[6:8], a_is_prompt), 0).astype(BF16)
    for n, lo in enumerate(chunks):
        stage_a_chunk(n, lo)
        if 0 < n < len(chunks) - 1:
            stage_b_chunk(lo)
    mu = jnp.sum(sum_s[:, 0:128], axis=-1, keepdims=True) * (1.0 / GM_HALF)
    var = jnp.sum(sum_s[:, 128:256], axis=-1, keepdims=True) * (1.0 / GM_HALF) - mu * mu
    rstd = lax.rsqrt(var + EPS)
    stage_b_chunk(chunks[-1])
    stat_prev[:, 0:128] = jnp.broadcast_to(rstd, (rows, 128))
    stat_prev[:, 128:256] = jnp.broadcast_to(-mu * rstd, (rows, 128))
    hb_prev[...] = hb_cur[...]
    vraw_prev[...] = vraw_cur[...]

    if layer == 0:
        @pl.when(i - 1 >= n_prompt_tiles)
        def _():
            for later in range(1, N_GM):
                vs_ref[later] = jnp.zeros((rows, GM_HALF), F32)


def _gm_proj_call(x, prev, mod, gain, w_in, layer, b_in, ln_g, ln_b, vs_all):
    rows = ROWS_GM_PROJ
    n, npt = _n_tiles(rows), _n_prompt_tiles(rows)

    def stage_a(spec):
        return pl.BlockSpec(spec.block_shape, lambda i, m=spec.index_map: m(jnp.minimum(i, n - 1)))

    def stage_b(spec):
        return pl.BlockSpec(spec.block_shape, lambda i, m=spec.index_map: m(jnp.maximum(i - 1, 0)))

    in_specs = [stage_a(s) for s in [_row_spec(rows, D)] + _prev_specs(rows) + _mod_specs(rows)] + [
        _resident((1, D)), pl.BlockSpec(memory_space=pl.ANY), _resident((8, GM_FFN)),
        _resident((8, GM_HALF)), _resident((8, GM_HALF))]
    rows8 = [jnp.broadcast_to(r.reshape(1, -1), (8, r.size)) for r in (b_in, ln_g, ln_b)]
    args = [x, *prev, *mod, gain, w_in, *rows8]
    if layer == 0:
        assert vs_all is None
        vs_spec = pl.BlockSpec((N_GM, rows, GM_HALF), lambda i: (0, jnp.maximum(i - 1 - npt, 0), 0))
        aliases = {}
    else:
        in_specs.append(pl.BlockSpec(memory_space=pl.ANY))
        args.append(vs_all)
        vs_spec = pl.BlockSpec((1, rows, GM_HALF), lambda i: (layer, jnp.maximum(i - 1 - npt, 0), 0))
        aliases = {len(args) - 1: 3}
    return pl.pallas_call(
        functools.partial(_gm_proj_kernel, n, npt, layer),
        grid=(n + 1,),
        in_specs=in_specs,
        out_specs=[stage_a(_row_spec(rows, D)), stage_b(_row_spec(rows, GM_HALF)),
                   stage_b(_row_spec(rows, GM_HALF)), vs_spec],
        out_shape=[jax.ShapeDtypeStruct((T_ALL, D), F32),
                   jax.ShapeDtypeStruct((T_ALL, GM_HALF), BF16),
                   jax.ShapeDtypeStruct((T_ALL, GM_HALF), BF16),
                   jax.ShapeDtypeStruct((N_GM, T_SAMPLE, GM_HALF), F32)],
        scratch_shapes=[pltpu.VMEM((rows, D), BF16), pltpu.VMEM((rows, D), BF16),
                        pltpu.VMEM((rows, GM_HALF), BF16), pltpu.VMEM((rows, GM_HALF), BF16),
                        pltpu.VMEM((rows, 256), F32), pltpu.VMEM((rows, 256), F32),
                        *_weight_scratch(D, GM_FFN)],
        input_output_aliases=aliases,
        compiler_params=_params("arbitrary"),
        name="gm_proj",
    )(*args)


def _gm_out_kernel(n_prompt_tiles, layer, u_ref, v_ref, ws_ref, bs_ref, w_hbm, bo_ref, x_ref, modp_ref,
                   mods_ref, gain_ref, wr_ref, br_ref, xo_ref, h2_ref, lg_ref, w_s, w_stage, w_sems):
    _load_weight_bf16(w_hbm, layer, w_s, w_stage, w_sems)
    rows = u_ref.shape[0]
    mod = _tile_mod((modp_ref, mods_ref), pl.program_id(0) < n_prompt_tiles)
    pieces = []
    for r0 in range(0, rows, GM_MIX):
        acc = jnp.zeros((GM_MIX, D), F32)
        for g in range(GM_GROUPS):
            lo = g * GM_GDIM
            sp = jnp.dot(ws_ref[0, g], v_ref[r0:r0 + GM_MIX, lo:lo + GM_GDIM],
                         preferred_element_type=F32) + bs_ref[0, g]
            gated = (u_ref[r0:r0 + GM_MIX, lo:lo + GM_GDIM].astype(F32) * sp).astype(BF16)
            acc = acc + jnp.dot(gated, w_s[lo:lo + GM_GDIM, :], preferred_element_type=F32)
        pieces.append(acc)
    acc = jnp.concatenate(pieces, axis=0) + bo_ref[...]
    _residual_router(acc, x_ref, mod, gain_ref, wr_ref, br_ref, xo_ref, h2_ref, lg_ref)


def _gm_block_diag(w_s, b_s):
    mats, biases = [], []
    for cl in (GM_CHUNK, DEC_SEQ):
        tri = jnp.tril(jnp.ones((cl, cl), bool))
        blk = jnp.where(tri[None], w_s[:, :cl, :cl], 0.0)
        reps = GM_MIX // cl
        eye = jnp.eye(reps, dtype=w_s.dtype)
        bd = jnp.einsum("ab,gts->gatbs", eye, blk).reshape(GM_GROUPS, GM_MIX, GM_MIX)
        mats.append(bd)
        biases.append(jnp.tile(b_s[:, :cl], (1, reps))[:, :, None])
    return jnp.stack(mats).astype(BF16), jnp.stack(biases).astype(F32)


def _gm_out_call(u, v, ws_bd, bs_bd, w_out, layer, b_out, x, mod, gain, w_r, b_r):
    rows = ROWS_WIDE
    npt = _n_prompt_tiles(rows)

    def variant(i):
        return jnp.where(i >= npt, 1, 0)

    return pl.pallas_call(
        functools.partial(_gm_out_kernel, npt, layer),
        grid=(_n_tiles(rows),),
        in_specs=[_row_spec(rows, GM_HALF), _row_spec(rows, GM_HALF),
                  pl.BlockSpec((1, GM_GROUPS, GM_MIX, GM_MIX), lambda i: (variant(i), 0, 0, 0)),
                  pl.BlockSpec((1, GM_GROUPS, GM_MIX, 1), lambda i: (variant(i), 0, 0, 0)),
                  pl.BlockSpec(memory_space=pl.ANY), _resident((1, D)), _row_spec(rows, D),
                  *_mod_specs(rows), _resident((1, D))] + _router_specs(),
        out_specs=_mix_out_specs(rows),
        out_shape=_MIX_OUT_SHAPE,
        scratch_shapes=_weight_scratch(GM_HALF, D),
        compiler_params=_params("arbitrary"),
        name="gm_out",
    )(u, v, ws_bd, bs_bd, w_out, b_out, x, *mod, gain, w_r, b_r)


def _expert_kernel(layer, be_ref, bv_ref, nb_ref, nx_ref, xb_ref, w1_hbm, w3_hbm, w2_hbm, yb_ref,
                   st1, st3, st2, w1s, w3s, w2s, sems, slot_ref):
    b = pl.program_id(0)

    def weight_copies(e, slot):
        return (pltpu.make_async_copy(w1_hbm.at[layer, e], st1.at[slot], sems.at[slot, 0]),
                pltpu.make_async_copy(w3_hbm.at[layer, e], st3.at[slot], sems.at[slot, 1]),
                pltpu.make_async_copy(w2_hbm.at[layer, e], st2.at[slot], sems.at[slot, 2]))

    @pl.when(b == 0)
    def _():
        slot_ref[0] = 0
        for cp in weight_copies(be_ref[0], 0):
            cp.start()

    @pl.when(b < nb_ref[0])
    def _():
        prev_e = be_ref[jnp.maximum(b - 1, 0)]

        @pl.when((b == 0) | (be_ref[b] != prev_e))
        def _():
            slot = slot_ref[0]
            for cp in weight_copies(be_ref[b], slot):
                cp.wait()
            w1s[...] = st1[slot].astype(BF16)
            w3s[...] = st3[slot].astype(BF16)
            w2s[...] = st2[slot].astype(BF16)

            @pl.when(nx_ref[b] >= 0)
            def _():
                for cp in weight_copies(nx_ref[b], 1 - slot):
                    cp.start()

            slot_ref[0] = 1 - slot

        valid = bv_ref[b]

        def run_rows(n):
            row = lax.broadcasted_iota(jnp.int32, (n, 1), 0)
            xw = jnp.where(row < valid, xb_ref[0:n], U32(0))
            x = _unpack_bf16_pairs(xw).astype(BF16)
            a = jnp.dot(x, w1s[...], preferred_element_type=F32)
            c = jnp.dot(x, w3s[...], preferred_element_type=F32)
            h = (_silu(a) * c).astype(BF16)
            yb_ref[0:n] = _pack_bf16_pairs(jnp.dot(h, w2s[...], preferred_element_type=F32))
            if n < EXP_BLOCK:
                yb_ref[n:EXP_BLOCK] = jnp.zeros((EXP_BLOCK - n, DP), U32)

        for n in range(EXP_ROW_STEP, EXP_BLOCK + 1, EXP_ROW_STEP):
            pl.when((valid > n - EXP_ROW_STEP) & (valid <= n))(functools.partial(run_rows, n))


def _expert_call(blk_e, blk_valid, n_blk, blk_next, xb, w1, w3, w2, layer):
    def blk(b, be, bv, nb, nx):
        return (jnp.minimum(b, nb[0] - 1), 0)

    up, down = (D, MOE_HIDDEN), (MOE_HIDDEN, D)
    grid_spec = pltpu.PrefetchScalarGridSpec(
        num_scalar_prefetch=4,
        grid=(N_EXP_BLOCKS,),
        in_specs=[pl.BlockSpec((EXP_BLOCK, DP), blk)] + [pl.BlockSpec(memory_space=pl.ANY)] * 3,
        out_specs=pl.BlockSpec((EXP_BLOCK, DP), blk),
        scratch_shapes=[pltpu.VMEM((2,) + up, F32), pltpu.VMEM((2,) + up, F32), pltpu.VMEM((2,) + down, F32),
                        pltpu.VMEM(up, BF16), pltpu.VMEM(up, BF16), pltpu.VMEM(down, BF16),
                        pltpu.SemaphoreType.DMA((2, 3)), pltpu.SMEM((1,), jnp.int32)],
    )
    return pl.pallas_call(
        functools.partial(_expert_kernel, layer),
        grid_spec=grid_spec,
        out_shape=jax.ShapeDtypeStruct((P_ROWS, DP), U32),
        compiler_params=_params("arbitrary"),
        name="experts",
    )(blk_e, blk_valid, n_blk, blk_next, xb, w1, w3, w2)


def _route_kernel(lg_ref, dest_ref, gates_ref, meta_ref, cnt_ref):
    t = pl.program_id(0)

    @pl.when(t == 0)
    def _():
        cnt_ref[...] = jnp.zeros_like(cnt_ref)

    pl.when(t < N_ROUTE_TILES)(functools.partial(_route_tile, t, lg_ref, dest_ref, gates_ref, cnt_ref))
    pl.when(t == N_ROUTE_TILES)(functools.partial(_route_finish, dest_ref, meta_ref, cnt_ref))


ROUTE_RANK_BITS = 16
assert P_ROWS < 2 ** ROUTE_RANK_BITS


def _route_tile(t, lg_ref, dest_ref, gates_ref, cnt_ref):
    tm = ROUTE_TM
    lt = lg_ref[...].T
    el = lt[0:MOE_EXPERTS]
    gl = lt[MOE_EXPERTS:MOE_EXPERTS + 8]
    gidx = lax.broadcasted_iota(jnp.int32, (8, tm), 0)
    neg = jnp.float32(-jnp.inf)
    gl = jnp.where(gidx < MOE_GROUPS, gl, neg)
    gmax = jnp.max(gl, axis=0, keepdims=True)
    grp = jnp.min(jnp.where(gl == gmax, gidx, MOE_GROUPS), axis=0, keepdims=True)
    eidx = lax.broadcasted_iota(jnp.int32, (MOE_EXPERTS, tm), 0)
    els = jnp.where((eidx >> 3) == grp, el, neg)
    m1 = jnp.max(els, axis=0, keepdims=True)
    i1 = jnp.min(jnp.where(els == m1, eidx, MOE_EXPERTS), axis=0, keepdims=True)
    els2 = jnp.where(eidx == i1, neg, els)
    m2 = jnp.max(els2, axis=0, keepdims=True)
    i2 = jnp.min(jnp.where(els2 == m2, eidx, MOE_EXPERTS), axis=0, keepdims=True)
    sel1 = eidx == i1
    sel2 = eidx == i2
    cnt = jnp.where(sel1 | sel2, 1.0, 0.0)

    lane = ROUTER_LANES
    before = (lax.broadcasted_iota(jnp.int32, (lane, lane), 0)
              < lax.broadcasted_iota(jnp.int32, (lane, lane), 1))
    tri = jnp.where(before, 1.0, 0.0).astype(BF16)
    run = cnt_ref[...]
    r1, r2 = [], []
    for k in range(tm // lane):
        piece = slice(k * lane, (k + 1) * lane)
        ck = cnt[:, piece]
        pos = run + jnp.dot(ck.astype(BF16), tri, preferred_element_type=F32)
        r1.append(jnp.sum(jnp.where(sel1[:, piece], pos, 0.0), axis=0, keepdims=True))
        r2.append(jnp.sum(jnp.where(sel2[:, piece], pos, 0.0), axis=0, keepdims=True))
        run = run + jnp.sum(ck, axis=1, keepdims=True)
    cnt_ref[...] = run
    rank = jnp.concatenate([jnp.concatenate(r1, axis=1), jnp.concatenate(r2, axis=1)], axis=0)
    eid = jnp.concatenate([i1, i2], axis=0)
    dest_ref[:, pl.ds(pl.multiple_of(t * tm, tm), tm)] = (eid << ROUTE_RANK_BITS) + rank.astype(jnp.int32)

    g_w = 1.0 / jnp.sum(jnp.exp(gl - gmax), axis=0, keepdims=True)
    e21 = jnp.exp(m2 - m1)
    p1 = 1.0 / (1.0 + e21)
    rid = lax.broadcasted_iota(jnp.int32, (ROUTER_LANES, tm), 0)
    gt = jnp.where(rid == 0, g_w * p1, jnp.where(rid == 1, g_w * (e21 * p1), 0.0))
    gates_ref[...] = gt.T


def _route_finish(dest_ref, meta_ref, cnt_ref):
    counts = cnt_ref[...]
    nblk = jnp.floor((counts + (EXP_BLOCK - 1.0)) * (1.0 / EXP_BLOCK))
    r = lax.broadcasted_iota(jnp.int32, (MOE_EXPERTS, MOE_EXPERTS), 0)
    c = lax.broadcasted_iota(jnp.int32, (MOE_EXPERTS, MOE_EXPERTS), 1)
    nblk_row = jnp.sum(jnp.where(r == c, nblk, 0.0), axis=0, keepdims=True)
    bstart = jnp.sum(jnp.where(c < r, nblk_row, 0.0), axis=1, keepdims=True)
    bend = bstart + nblk
    bidx = lax.broadcasted_iota(jnp.int32, (1, META_LANES), 1).astype(F32)
    blk_e = jnp.minimum(jnp.sum(jnp.where(bidx >= bend, 1.0, 0.0), axis=0, keepdims=True),
                        MOE_EXPERTS - 1.0)
    erow = lax.broadcasted_iota(jnp.int32, (MOE_EXPERTS, META_LANES), 0).astype(F32)
    mine = erow == blk_e
    cnt_b = jnp.sum(jnp.where(mine, counts, 0.0), axis=0, keepdims=True)
    start_b = jnp.sum(jnp.where(mine, bstart, 0.0), axis=0, keepdims=True)
    valid = jnp.clip(cnt_b - (bidx - start_b) * EXP_BLOCK, 0.0, float(EXP_BLOCK))
    n_blk = jnp.sum(nblk, axis=0, keepdims=True)
    end_b = jnp.sum(jnp.where(mine, bend, 0.0), axis=0, keepdims=True)
    nxt = jnp.minimum(jnp.sum(jnp.where(end_b >= bend, 1.0, 0.0), axis=0, keepdims=True),
                      MOE_EXPERTS - 1.0)
    nxt = jnp.where(end_b < n_blk, nxt, -1.0)
    mrow = lax.broadcasted_iota(jnp.int32, (8, META_LANES), 0)
    meta = jnp.where(mrow == 0, blk_e, jnp.where(mrow == 1, valid, jnp.where(
        mrow == 2, n_blk, jnp.where(mrow == 3, nxt, 0.0))))
    meta_ref[...] = meta.astype(jnp.int32)

    base = (bstart * EXP_BLOCK).astype(jnp.int32)
    packed = dest_ref[...]
    eid = packed >> ROUTE_RANK_BITS
    row = packed & (2 ** ROUTE_RANK_BITS - 1)
    for e in range(MOE_EXPERTS):
        row = row + jnp.where(eid == e, base[e:e + 1, :], 0)
    dest_ref[...] = row


def _route_call(logits):
    last = N_ROUTE_TILES - 1
    return pl.pallas_call(
        _route_kernel,
        grid=(N_ROUTE_TILES + 1,),
        in_specs=[pl.BlockSpec((ROUTE_TM, ROUTER_LANES), lambda t: (jnp.minimum(t, last), 0))],
        out_specs=[pl.BlockSpec((MOE_TOPK, T_ALL), lambda t: (0, 0)),
                   pl.BlockSpec((ROUTE_TM, ROUTER_LANES), lambda t: (jnp.minimum(t, last), 0)),
                   pl.BlockSpec((8, META_LANES), lambda t: (0, 0))],
        out_shape=[jax.ShapeDtypeStruct((MOE_TOPK, T_ALL), jnp.int32),
                   jax.ShapeDtypeStruct((T_ALL, ROUTER_LANES), F32),
                   jax.ShapeDtypeStruct((8, META_LANES), jnp.int32)],
        scratch_shapes=[pltpu.VMEM((MOE_EXPERTS, 1), F32)],
        compiler_params=_params("arbitrary"),
        name="route",
    )(logits)


def _sc_mesh():
    return plsc.VectorSubcoreMesh(core_axis_name="c", subcore_axis_name="s")


def _sc_token_offset(j):
    wid = lax.axis_index("s") * SC_CORES + lax.axis_index("c")
    return pl.multiple_of(wid * SC_ROWS_PER_WORKER + j * SC_CHUNK, 8)


SC_N_CHUNKS = SC_ROWS_PER_WORKER // SC_CHUNK
assert SC_N_CHUNKS % 2 == 1


def _start(copies):
    for cp in copies:
        cp.start()


def _wait(copies):
    for cp in copies:
        cp.wait()


def _dispatch_body(h_hbm, d0_hbm, d1_hbm, out_hbm, i0a, i1a, rows_a, i0b, i1b, rows_b, la, lb, sa, sb):
    sets = {"a": (i0a, i1a, rows_a, la, sa), "b": (i0b, i1b, rows_b, lb, sb)}

    def loads(j, s):
        i0, i1, rows, lsem, _ = sets[s]
        src = pl.ds(_sc_token_offset(j), SC_CHUNK)
        return (pltpu.make_async_copy(d0_hbm.at[src], i0, lsem),
                pltpu.make_async_copy(d1_hbm.at[src], i1, lsem),
                pltpu.make_async_copy(h_hbm.at[src], rows, lsem))

    def scatters(s):
        i0, i1, rows, _, ssem = sets[s]
        return (pltpu.make_async_copy(rows, out_hbm.at[i0], ssem),
                pltpu.make_async_copy(rows, out_hbm.at[i1], ssem))

    _start(loads(0, "a"))

    @pl.loop(0, SC_N_CHUNKS // 2)
    def _(p):
        a = 2 * p
        _start(loads(a + 1, "b"))
        _wait(loads(a, "a"))
        _start(scatters("a"))
        _wait(loads(a + 1, "b"))
        _start(scatters("b"))
        _wait(scatters("a"))
        _start(loads(a + 2, "a"))
        _wait(scatters("b"))

    _wait(loads(SC_N_CHUNKS - 1, "a"))
    _start(scatters("a"))
    _wait(scatters("a"))


def _dispatch_call(h2, dest0, dest1):
    buffers = [pltpu.VMEM((SC_CHUNK,), jnp.int32), pltpu.VMEM((SC_CHUNK,), jnp.int32),
               pltpu.VMEM((SC_CHUNK, DP), U32)]
    return pl.kernel(
        _dispatch_body,
        out_type=jax.ShapeDtypeStruct((P_ROWS, DP), U32),
        mesh=_sc_mesh(),
        scratch_types=buffers + buffers + [pltpu.SemaphoreType.DMA] * 4,
        name="moe_dispatch",
    )(h2, dest0, dest1)


def _combine_body(yb_hbm, d0_hbm, d1_hbm, o0_hbm, o1_hbm, i0a, i1a, r0a, r1a, i0b, i1b, r0b, r1b,
                  ia, ib, ga, gb, wa, wb):
    sets = {"a": (i0a, i1a, r0a, r1a, ia, ga, wa), "b": (i0b, i1b, r0b, r1b, ib, gb, wb)}

    def index_loads(j, s):
        i0, i1, _, _, isem, _, _ = sets[s]
        src = pl.ds(_sc_token_offset(j), SC_CHUNK)
        return (pltpu.make_async_copy(d0_hbm.at[src], i0, isem), pltpu.make_async_copy(d1_hbm.at[src], i1, isem))

    def gathers(s):
        i0, i1, r0, r1, _, gsem, _ = sets[s]
        return (pltpu.make_async_copy(yb_hbm.at[i0], r0, gsem), pltpu.make_async_copy(yb_hbm.at[i1], r1, gsem))

    def writes(j, s):
        _, _, r0, r1, _, _, wsem = sets[s]
        dst = pl.ds(_sc_token_offset(j), SC_CHUNK)
        return (pltpu.make_async_copy(r0, o0_hbm.at[dst], wsem), pltpu.make_async_copy(r1, o1_hbm.at[dst], wsem))

    _start(index_loads(0, "a"))
    _wait(index_loads(0, "a"))
    _start(gathers("a"))

    @pl.loop(0, SC_N_CHUNKS // 2)
    def _(p):
        a = 2 * p
        _start(index_loads(a + 1, "b"))
        _wait(index_loads(a + 1, "b"))
        _wait(gathers("a"))
        _start(writes(a, "a"))
        _start(gathers("b"))
        _start(index_loads(a + 2, "a"))
        _wait(index_loads(a + 2, "a"))
        _wait(writes(a, "a"))
        _wait(gathers("b"))
        _start(writes(a + 1, "b"))
        _start(gathers("a"))
        _wait(writes(a + 1, "b"))

    _wait(gathers("a"))
    _start(writes(SC_N_CHUNKS - 1, "a"))
    _wait(writes(SC_N_CHUNKS - 1, "a"))


def _combine_call(yb, dest0, dest1):
    out = jax.ShapeDtypeStruct((T_ALL, DP), U32)
    buffers = [pltpu.VMEM((SC_CHUNK,), jnp.int32), pltpu.VMEM((SC_CHUNK,), jnp.int32),
               pltpu.VMEM((SC_CHUNK, DP), U32), pltpu.VMEM((SC_CHUNK, DP), U32)]
    return pl.kernel(
        _combine_body,
        out_type=(out, out),
        mesh=_sc_mesh(),
        scratch_types=buffers + buffers + [pltpu.SemaphoreType.DMA] * 6,
        name="moe_combine",
    )(yb, dest0, dest1)


def _moe_rows(h2, dest, meta, w1, w3, w2, layer):
    dest0, dest1 = dest[0], dest[1]
    xb = _dispatch_call(h2, dest0, dest1)
    yb = _expert_call(meta[0, :N_EXP_BLOCKS], meta[1, :N_EXP_BLOCKS], meta[2, :1], meta[3, :N_EXP_BLOCKS],
                      xb, w1, w3, w2, layer)
    return _combine_call(yb, dest0, dest1)


def _final_kernel(is_prompt, x_ref, yg0_ref, yg1_ref, gates_ref, modp_ref, mods_ref, gain_ref, o_ref):
    x = _add_moe(x_ref[...], yg0_ref, yg1_ref, gates_ref, _tile_mod((modp_ref, mods_ref), is_prompt))
    o_ref[...] = _rms(x) * gain_ref[...]


def _final_call(x, prev, gain, row0, n_rows):
    rows = ROWS_WIDE
    tile0 = row0 // rows
    is_prompt = row0 < T_PROMPT
    assert row0 + n_rows <= T_PROMPT or not is_prompt

    def tile(width):
        return pl.BlockSpec((rows, width), lambda i: (tile0 + i, 0))

    mod_specs = [pl.BlockSpec(s.block_shape, lambda i, m=s.index_map: m(tile0 + i)) for s in _mod_specs(rows)]
    return pl.pallas_call(
        functools.partial(_final_kernel, is_prompt),
        grid=(n_rows // rows,),
        in_specs=[tile(D), tile(DP), tile(DP), tile(ROUTER_LANES), *mod_specs, _resident((1, D))],
        out_specs=pl.BlockSpec((rows, D), lambda i: (i, 0)),
        out_shape=jax.ShapeDtypeStruct((n_rows, D), F32),
        compiler_params=_params("parallel"),
        name="final_norm",
    )(x, *prev, gain)


def _rope_tables():
    pos = np.concatenate([np.tile(np.arange(SEQ), BATCH),
                          np.tile(PAST_LEN + np.arange(DEC_SEQ), DEC_BATCH)]).astype(np.float32)
    inv = (ROPE_BASE ** (-np.arange(ROPE_HALF, dtype=np.float32) / ROPE_HALF)).astype(np.float32)
    ang = (pos[:, None] * inv[None, :]).astype(np.float32).astype(np.float64)
    return jnp.asarray(np.cos(ang), F32), jnp.asarray(np.sin(ang), F32)


def kernel(x_prompt, x_sample, c_prompt, c_sample, state_ret, ada_w, ada_b, norm1_g, norm2_g, ret_w_in,
           ret_w_out, gm_w_in, gm_b_in, gm_ln_g, gm_ln_b, gm_w_s, gm_b_s, gm_w_out, gm_b_out, moe_w_rg,
           moe_b_rg, moe_w_re, moe_b_re, moe_w1, moe_w3, moe_w2, final_g):
    x = (x_prompt.reshape(T_PROMPT, D), x_sample.reshape(T_SAMPLE, D))
    c_all = jnp.concatenate([c_prompt, c_sample], axis=0)
    cos, sin = _rope_tables()
    dec = _ret_row_scales()

    mod_all = _ada_call(c_all, ada_w, ada_b).reshape(DEPTH, N_SEQ, 6, D)

    def layer_params(i):
        mod = (mod_all[i, :BATCH], mod_all[i, BATCH:])
        w_r = jnp.pad(jnp.concatenate([moe_w_re[i], moe_w_rg[i]], axis=1),
                      ((0, 0), (0, ROUTER_LANES - MOE_GROUPS - MOE_EXPERTS)))
        w_r_hi = w_r.astype(BF16)
        w_r_lo = (w_r - w_r_hi.astype(F32)).astype(BF16)
        w_r = jnp.concatenate([w_r_hi, w_r_lo], axis=1)
        b_r = jnp.pad(jnp.concatenate([moe_b_re[i].reshape(-1), moe_b_rg[i]]),
                      (0, ROUTER_LANES - MOE_GROUPS - MOE_EXPERTS)).reshape(1, ROUTER_LANES)
        return mod, w_r, b_r

    ret_prompt = ret_sample = gm_sample = None
    prev = None
    for i in range(DEPTH):
        j = i // 2
        mod, w_r, b_r = layer_params(i)
        g1 = norm1_g[i].reshape(1, D)
        g2 = norm2_g[i].reshape(1, D)
        if i % 2 == 0:
            x, p = _ret_proj_call(x, prev, mod, g1, ret_w_in, j, cos, sin, dec)
            y_p, ret_prompt = _ret_core_call(p, None, ret_prompt, j, BATCH, SEQ, RET_CHUNK_PROMPT, 0)
            y_s, ret_sample = _ret_core_call(p, state_ret, ret_sample, j, DEC_BATCH, DEC_SEQ,
                                             RET_CHUNK_SAMPLE, T_PROMPT)
            x, h2, logits = _ret_out_call(y_p, y_s, ret_w_out, j, x, mod, g2, w_r, b_r)
        else:
            x, u, v, gm_sample = _gm_proj_call(x, prev, mod, g1, gm_w_in, j, gm_b_in[j], gm_ln_g[j],
                                               gm_ln_b[j], gm_sample)
            ws_bd, bs_bd = _gm_block_diag(gm_w_s[j], gm_b_s[j])
            x, h2, logits = _gm_out_call(u, v, ws_bd, bs_bd, gm_w_out, j,
                                         gm_b_out[j].reshape(1, D), x, mod, g2, w_r, b_r)
        dest, gates, meta = _route_call(logits)
        yg0, yg1 = _moe_rows(h2, dest, meta, moe_w1, moe_w3, moe_w2, i)
        prev = (yg0, yg1, gates, *mod)

    fg = final_g.reshape(1, D)
    y_prompt = _final_call(x, prev, fg, 0, T_PROMPT).reshape(BATCH, SEQ, D)
    y_sample = _final_call(x, prev, fg, T_PROMPT, T_SAMPLE).reshape(DEC_BATCH, DEC_SEQ, D)
    return (y_prompt, y_sample, ret_prompt, ret_sample,
            gm_sample.reshape(N_GM, DEC_BATCH, DEC_SEQ, GM_HALF))
```

```python
import functools

import numpy as np
import jax
import jax.numpy as jnp
from jax import lax
from jax.experimental import pallas as pl
from jax.experimental.pallas import tpu as pltpu
from jax.experimental.pallas import tpu_sc as plsc

F32 = jnp.float32
BF16 = jnp.bfloat16
U32 = jnp.uint32

D = 1024
BATCH, SEQ = 4, 4096
DEC_BATCH, DEC_SEQ = 16, 64
PAST_LEN = 4096
DEPTH = 4
N_RET = (DEPTH + 1) // 2
N_GM = DEPTH // 2
N_SEQ = BATCH + DEC_BATCH

RET_HEADS, RET_DK, RET_DV = 4, 256, 512
RET_QK = RET_HEADS * RET_DK
RET_V = RET_HEADS * RET_DV
RET_IN = 2 * RET_QK + 2 * RET_V
ROPE_BASE = 10000.0
ROPE_HALF = RET_DK // 2

GM_FFN = 6 * D
GM_HALF = GM_FFN // 2
GM_GROUPS = 4
GM_GDIM = GM_HALF // GM_GROUPS
GM_CHUNK = 128

MOE_GROUPS, MOE_PER_GROUP = 4, 8
MOE_EXPERTS = MOE_GROUPS * MOE_PER_GROUP
MOE_TOPK = 2
MOE_HIDDEN = 512
EPS = 1e-6

GROUP = DEC_SEQ
T_PROMPT = BATCH * SEQ
T_SAMPLE = DEC_BATCH * DEC_SEQ
T_ALL = T_PROMPT + T_SAMPLE
N_GROUPS = T_ALL // GROUP
ROWS_WIDE = 512
ROWS_GM_PROJ = 256

RET_CHUNK_PROMPT = 256
RET_CHUNK_SAMPLE = DEC_SEQ

GM_MIX = 256

EXP_BLOCK = 512
EXP_ROW_STEP = 128
N_ASSIGN = T_ALL * MOE_TOPK
N_EXP_BLOCKS = -(-(N_ASSIGN + MOE_EXPERTS * (EXP_BLOCK - 1)) // EXP_BLOCK)
P_ROWS = N_EXP_BLOCKS * EXP_BLOCK
ROUTER_LANES = 128
ROUTE_TM = 1024
N_ROUTE_TILES = T_ALL // ROUTE_TM
META_LANES = 256
assert META_LANES >= N_EXP_BLOCKS

DP = D // 2
SC_CORES, SC_SUBCORES = 2, 16
SC_WORKERS = SC_CORES * SC_SUBCORES
SC_ROWS_PER_WORKER = T_ALL // SC_WORKERS
SC_CHUNK = 32
assert SC_ROWS_PER_WORKER % SC_CHUNK == 0 and SC_CHUNK % 8 == 0

V7X_VMEM_LIMIT_BYTES = 56 * 1024 * 1024


def _params(*sem):
    return pltpu.CompilerParams(dimension_semantics=sem, vmem_limit_bytes=V7X_VMEM_LIMIT_BYTES)


def _resident(shape):
    nd = len(shape)
    return pl.BlockSpec(shape, lambda *_: (0,) * nd, pipeline_mode=pl.Buffered(1))


WEIGHT_STAGE_BYTES = 3 * 1024 * 1024


def _weight_scratch(k, n):
    chunk = k
    while chunk * n * 4 > WEIGHT_STAGE_BYTES:
        assert chunk % 16 == 0
        chunk //= 2
    return [pltpu.VMEM((k, n), BF16), pltpu.VMEM((2, chunk, n), F32), pltpu.SemaphoreType.DMA((2,))]


def _load_weight_bf16(w_hbm, layer, w_s, stage, sems):
    k = w_s.shape[0]
    chunk = stage.shape[1]

    def copy(c):
        return pltpu.make_async_copy(w_hbm.at[layer, pl.ds(c * chunk, chunk)], stage.at[c % 2], sems.at[c % 2])

    @pl.when(pl.program_id(0) == 0)
    def _():
        copy(0).start()
        for c in range(k // chunk):
            if c + 1 < k // chunk:
                copy(c + 1).start()
            copy(c).wait()
            w_s[c * chunk:(c + 1) * chunk, :] = stage[c % 2].astype(BF16)


def _rms(x):
    return x * lax.rsqrt(jnp.mean(x * x, axis=-1, keepdims=True) + EPS)


def _silu(x):
    return x * jax.nn.sigmoid(x)


def _per_group(x2d, fn):
    rows = x2d.shape[0]
    return fn(x2d.reshape(rows // GROUP, GROUP, D)).reshape(rows, D)


def _tile_mod(mod_refs, is_prompt):
    modp_ref, mods_ref = mod_refs
    return jnp.where(is_prompt, jnp.broadcast_to(modp_ref[...], mods_ref.shape), mods_ref[...])


def _norm_mod(x, gain_ref, mod, shift_idx):
    y = _rms(x) * gain_ref[...]
    scale = mod[:, shift_idx + 1:shift_idx + 2, :]
    shift = mod[:, shift_idx:shift_idx + 1, :]
    return _per_group(y, lambda y3: y3 * (1.0 + scale) + shift)


def _pack_bf16_pairs(x):
    lo = lax.bitcast_convert_type(x[:, :DP].astype(BF16).astype(F32), U32)
    hi = lax.bitcast_convert_type(x[:, DP:].astype(BF16).astype(F32), U32)
    return (lo >> 16) | (hi & U32(0xFFFF0000))


def _unpack_bf16_pairs(w):
    lo = lax.bitcast_convert_type(w << 16, F32)
    hi = lax.bitcast_convert_type(w & U32(0xFFFF0000), F32)
    return jnp.concatenate([lo, hi], axis=1)


def _add_moe(x, yg0_ref, yg1_ref, gates_ref, mod_prev):
    g = gates_ref[...]
    y = g[:, 0:1] * _unpack_bf16_pairs(yg0_ref[...]) + g[:, 1:2] * _unpack_bf16_pairs(yg1_ref[...])
    gate2 = mod_prev[:, 5:6, :]
    return x + _per_group(y, lambda y3: y3 * gate2)


ADA_TN = 1536


def _ada_kernel(c_ref, w_ref, b_ref, o_ref):
    c = c_ref[...]
    s = _silu(c).astype(BF16)
    o_ref[0] = jnp.dot(s, w_ref[0].astype(BF16), preferred_element_type=F32) + b_ref[0]


def _ada_call(c_all, ada_w, ada_b):
    return pl.pallas_call(
        _ada_kernel,
        grid=(DEPTH, 6 * D // ADA_TN),
        in_specs=[
            pl.BlockSpec((N_SEQ, D), lambda i, j: (0, 0)),
            pl.BlockSpec((1, D, ADA_TN), lambda i, j: (i, 0, j)),
            pl.BlockSpec((1, 1, ADA_TN), lambda i, j: (i, 0, j)),
        ],
        out_specs=pl.BlockSpec((1, N_SEQ, ADA_TN), lambda i, j: (i, 0, j)),
        out_shape=jax.ShapeDtypeStruct((DEPTH, N_SEQ, 6 * D), F32),
        compiler_params=_params("parallel", "parallel"),
        name="ada_modulation",
    )(c_all, ada_w, ada_b.reshape(DEPTH, 1, 6 * D))


def _n_tiles(rows):
    return T_ALL // rows


def _n_prompt_tiles(rows):
    return T_PROMPT // rows


def _row_spec(rows, width):
    return pl.BlockSpec((rows, width), lambda i: (i, 0))


def _mod_specs(rows):
    npt = _n_prompt_tiles(rows)
    return [pl.BlockSpec((1, 6, D), lambda i: (jnp.minimum(i * rows // SEQ, BATCH - 1), 0, 0)),
            pl.BlockSpec((rows // DEC_SEQ, 6, D), lambda i: (jnp.maximum(i - npt, 0), 0, 0))]


def _prev_specs(rows):
    return [_row_spec(rows, DP), _row_spec(rows, DP), _row_spec(rows, ROUTER_LANES)] + _mod_specs(rows)


def _stage_a_spec(spec, n_tiles):
    return pl.BlockSpec(spec.block_shape, lambda i, m=spec.index_map: m(jnp.minimum(i, n_tiles - 1)))


def _stage_b_spec(spec):
    return pl.BlockSpec(spec.block_shape, lambda i, m=spec.index_map: m(jnp.maximum(i - 1, 0)))


def _prompt_rows_spec(rows, width):
    last = _n_prompt_tiles(rows) - 1
    return pl.BlockSpec((rows, width), lambda i: (jnp.minimum(i, last), 0))


def _sample_rows_spec(rows, width):
    npt = _n_prompt_tiles(rows)
    return pl.BlockSpec((rows, width), lambda i: (jnp.maximum(i - npt, 0), 0))


def _ret_proj_kernel(has_prev, n_tiles, n_prompt_tiles, layer, *refs):
    i = pl.program_id(0)
    is_prompt = jnp.minimum(i, n_tiles - 1) < n_prompt_tiles
    head = refs[:6] if has_prev else refs[:2]
    refs = refs[len(head):]
    mod_refs = refs[:2]
    (gain_ref, w_hbm, cos_ref, sin_ref, dec_ref, xo_ref, p_ref,
     hb_cur, hb_prev, w_s, w_stage, w_sems) = refs[2:]
    _load_weight_bf16(w_hbm, layer, w_s, w_stage, w_sems)

    @pl.when(i == 0)
    def _():
        hb_prev[...] = jnp.zeros_like(hb_prev)

    def qk_head(j):
        lo = j * RET_DK
        acc = jnp.dot(hb_prev[...], w_s[:, lo:lo + RET_DK], preferred_element_type=F32)
        x1 = acc[:, :ROPE_HALF]
        x2 = acc[:, ROPE_HALF:]
        cos = cos_ref[...]
        sin = sin_ref[...]
        scale = dec_ref[:, j:j + 1]
        p_ref[:, lo:lo + ROPE_HALF] = ((x1 * cos - x2 * sin) * scale).astype(BF16)
        p_ref[:, lo + ROPE_HALF:lo + RET_DK] = ((x1 * sin + x2 * cos) * scale).astype(BF16)

    def vg_head(j):
        lo = 2 * RET_QK + j * RET_DV
        acc = jnp.dot(hb_prev[...], w_s[:, lo:lo + RET_DV], preferred_element_type=F32)
        if j >= RET_HEADS:
            acc = _silu(acc)
        p_ref[:, lo:lo + RET_DV] = acc.astype(BF16)

    vg_head(0)
    vg_head(1)
    if has_prev:
        x_ref, yg0_ref, yg1_ref, gates_ref = head[:4]
        x = _add_moe(x_ref[...], yg0_ref, yg1_ref, gates_ref, _tile_mod(head[4:6], is_prompt))
    else:
        x = jnp.where(is_prompt, head[0][...], head[1][...])
    xo_ref[...] = x
    hb_cur[...] = _norm_mod(x, gain_ref, _tile_mod(mod_refs, is_prompt), 0).astype(BF16)
    for j in range(2, 2 * RET_HEADS):
        vg_head(j)
    for j in range(2 * RET_HEADS):
        qk_head(j)
    hb_prev[...] = hb_cur[...]


def _ret_proj_call(x, prev, mod, gain, w_in, layer, cos, sin, dec):
    rows = ROWS_WIDE
    n = _n_tiles(rows)
    has_prev = prev is not None
    if has_prev:
        in_specs = [_row_spec(rows, D)] + _prev_specs(rows)
        args = [x] + list(prev)
    else:
        in_specs = [_prompt_rows_spec(rows, D), _sample_rows_spec(rows, D)]
        args = list(x)
    in_specs = [_stage_a_spec(s, n) for s in in_specs + _mod_specs(rows)] + [
        _resident((1, D)), pl.BlockSpec(memory_space=pl.ANY)] + [
        _stage_b_spec(s) for s in (_row_spec(rows, ROPE_HALF), _row_spec(rows, ROPE_HALF),
                                   _row_spec(rows, 2 * RET_HEADS))]
    args += [*mod, gain, w_in, cos, sin, dec]
    return pl.pallas_call(
        functools.partial(_ret_proj_kernel, has_prev, n, _n_prompt_tiles(rows), layer),
        grid=(n + 1,),
        in_specs=in_specs,
        out_specs=[_stage_a_spec(_row_spec(rows, D), n), _stage_b_spec(_row_spec(rows, RET_IN))],
        out_shape=[jax.ShapeDtypeStruct((T_ALL, D), F32), jax.ShapeDtypeStruct((T_ALL, RET_IN), BF16)],
        scratch_shapes=[pltpu.VMEM((rows, D), BF16), pltpu.VMEM((rows, D), BF16),
                        *_weight_scratch(D, RET_IN)],
        compiler_params=_params("arbitrary"),
        name="ret_proj",
    )(*args)


def _ret_core_kernel(has_s0, n_chunks, layer, *refs):
    refs = list(refs)
    p_ref = refs.pop(0)
    s0_ref = refs.pop(0) if has_s0 else None
    causal_ref, cd_ref = refs[:2]
    y_ref, so_ref, s_ref = refs[-3:]
    c = pl.program_id(1)

    @pl.when(c == 0)
    def _():
        if has_s0:
            s_ref[...] = s0_ref[0, 0]
        else:
            s_ref[...] = jnp.zeros_like(s_ref)

    for h in range(RET_HEADS):
        qb = p_ref[:, h * RET_DK:(h + 1) * RET_DK]
        kb = p_ref[:, RET_QK + h * RET_DK:RET_QK + (h + 1) * RET_DK]
        vb = p_ref[:, 2 * RET_QK + h * RET_DV:2 * RET_QK + (h + 1) * RET_DV]
        gb = p_ref[:, 2 * RET_QK + RET_V + h * RET_DV:2 * RET_QK + RET_V + (h + 1) * RET_DV]
        scores = lax.dot_general(qb, kb, (((1,), (1,)), ((), ())), preferred_element_type=F32)
        scores = scores * causal_ref[...]
        s_old = s_ref[h]
        o = (jnp.dot(scores.astype(BF16), vb, preferred_element_type=F32)
             + jnp.dot(qb, s_old.astype(BF16), preferred_element_type=F32))
        s_ref[h] = cd_ref[h][:, 0:1] * (s_old + lax.dot_general(
            kb, vb, (((0,), (0,)), ((), ())), preferred_element_type=F32))
        y_ref[:, h * RET_DV:(h + 1) * RET_DV] = (gb.astype(F32) * _rms(o)).astype(BF16)

    @pl.when(c == n_chunks - 1)
    def _():
        so_ref[0, 0] = s_ref[...]
        if layer == 0:
            for later in range(1, N_RET):
                so_ref[later, 0] = jnp.zeros_like(s_ref)


def _ret_log_gamma():
    return np.log1p(-np.exp2(-5.0 - np.arange(RET_HEADS, dtype=np.float64)))


def _ret_chunk_tables(cl):
    idx = np.arange(cl)
    causal = (idx[:, None] >= idx[None, :]).astype(np.float32)
    cd = np.broadcast_to(np.exp(_ret_log_gamma() * cl)[:, None, None], (RET_HEADS, 1, 128))
    return jnp.asarray(causal, F32), jnp.asarray(cd, F32)


def _ret_row_scales():
    c = np.concatenate([np.arange(T_PROMPT) % RET_CHUNK_PROMPT,
                        np.arange(T_SAMPLE) % RET_CHUNK_SAMPLE]).astype(np.float64)
    e = (c[:, None] + 1.0) * _ret_log_gamma()[None, :]
    return jnp.asarray(np.concatenate([np.exp(e), np.exp(-e) * RET_DK ** -0.5], axis=1), F32)


def _ret_core_call(p, s0, states, layer, n_seq, seq_len, cl, row0):
    has_s0 = s0 is not None
    n_chunks = seq_len // cl
    rb0 = row0 // cl
    state = (RET_HEADS, RET_DK, RET_DV)
    in_specs = [pl.BlockSpec((cl, RET_IN), lambda b, c: (rb0 + b * n_chunks + c, 0))]
    args = [p]
    if has_s0:
        in_specs.append(pl.BlockSpec((1, 1) + state, lambda b, c: (layer, b, 0, 0, 0)))
        args.append(s0)
    in_specs += [_resident((cl, cl)), _resident((RET_HEADS, 1, 128))]
    args += list(_ret_chunk_tables(cl))
    if layer == 0:
        assert states is None
        state_spec = pl.BlockSpec((N_RET, 1) + state, lambda b, c: (0, b, 0, 0, 0))
        aliases = {}
    else:
        in_specs.append(pl.BlockSpec(memory_space=pl.ANY))
        args.append(states)
        state_spec = pl.BlockSpec((1, 1) + state, lambda b, c: (layer, b, 0, 0, 0))
        aliases = {len(args) - 1: 1}
    return pl.pallas_call(
        functools.partial(_ret_core_kernel, has_s0, n_chunks, layer),
        grid=(n_seq, n_chunks),
        in_specs=in_specs,
        out_specs=[pl.BlockSpec((cl, RET_V), lambda b, c: (b * n_chunks + c, 0)), state_spec],
        out_shape=[jax.ShapeDtypeStruct((n_seq * seq_len, RET_V), BF16),
                   jax.ShapeDtypeStruct((N_RET, n_seq) + state, F32)],
        scratch_shapes=[pltpu.VMEM(state, F32)],
        input_output_aliases=aliases,
        compiler_params=_params("parallel", "arbitrary"),
        name="ret_core",
    )(*args)


def _residual_router(acc, x_ref, mod, gain_ref, wr_ref, br_ref, xo_ref, h2_ref, lg_ref):
    gate1 = mod[:, 2:3, :]
    xn = x_ref[...] + _per_group(acc, lambda a3: a3 * gate1)
    xo_ref[...] = xn
    h2 = _norm_mod(xn, gain_ref, mod, 3)
    h2_ref[...] = _pack_bf16_pairs(h2)
    hh = jnp.dot(h2.astype(BF16), wr_ref[...], preferred_element_type=F32)
    lg_ref[...] = hh[:, :ROUTER_LANES] + hh[:, ROUTER_LANES:] + br_ref[...]


def _mix_out_specs(rows):
    return [_row_spec(rows, D), _row_spec(rows, DP), _row_spec(rows, ROUTER_LANES)]


_MIX_OUT_SHAPE = [
    jax.ShapeDtypeStruct((T_ALL, D), F32),
    jax.ShapeDtypeStruct((T_ALL, DP), U32),
    jax.ShapeDtypeStruct((T_ALL, ROUTER_LANES), F32),
]


def _router_specs():
    return [_resident((D, 2 * ROUTER_LANES)), _resident((1, ROUTER_LANES))]


def _ret_out_kernel(n_prompt_tiles, layer, yp_ref, ys_ref, w_hbm, x_ref, modp_ref, mods_ref, gain_ref,
                    wr_ref, br_ref, xo_ref, h2_ref, lg_ref, w_s, w_stage, w_sems):
    _load_weight_bf16(w_hbm, layer, w_s, w_stage, w_sems)
    is_prompt = pl.program_id(0) < n_prompt_tiles
    yin = jnp.where(is_prompt, yp_ref[...], ys_ref[...])
    acc = jnp.dot(yin, w_s[...], preferred_element_type=F32)
    mod = _tile_mod((modp_ref, mods_ref), is_prompt)
    _residual_router(acc, x_ref, mod, gain_ref, wr_ref, br_ref, xo_ref, h2_ref, lg_ref)


def _ret_out_call(y_prompt, y_sample, w_out, layer, x, mod, gain, w_r, b_r):
    rows = ROWS_WIDE
    return pl.pallas_call(
        functools.partial(_ret_out_kernel, _n_prompt_tiles(rows), layer),
        grid=(_n_tiles(rows),),
        in_specs=[_prompt_rows_spec(rows, RET_V), _sample_rows_spec(rows, RET_V),
                  pl.BlockSpec(memory_space=pl.ANY), _row_spec(rows, D), *_mod_specs(rows),
                  _resident((1, D))] + _router_specs(),
        out_specs=_mix_out_specs(rows),
        out_shape=_MIX_OUT_SHAPE,
        scratch_shapes=_weight_scratch(RET_V, D),
        compiler_params=_params("arbitrary"),
        name="ret_out",
    )(y_prompt, y_sample, w_out, x, *mod, gain, w_r, b_r)


GM_TN = 512


_GELU_C = float(np.sqrt(2.0 / np.pi))


def _gelu_tanh(x):
    hx = 0.5 * x
    return hx * jnp.tanh(x * (_GELU_C + (_GELU_C * 0.044715) * (x * x))) + hx


def _gm_proj_kernel(n_tiles, n_prompt_tiles, layer, *refs):
    x_ref, yg0_ref, yg1_ref, gates_ref = refs[:4]
    gain_ref, w_hbm, b_ref, lg_ref, lb_ref = refs[8:13]
    (xo_ref, u_ref, v_ref, vs_ref, hb_cur, hb_prev, vraw_cur, vraw_prev, stat_prev, sum_s,
     w_s, w_stage, w_sems) = refs[-13:]
    _load_weight_bf16(w_hbm, layer, w_s, w_stage, w_sems)
    i = pl.program_id(0)
    rows = x_ref.shape[0]

    @pl.when(i == 0)
    def _():
        hb_prev[...] = jnp.zeros_like(hb_prev)
        vraw_prev[...] = jnp.zeros_like(vraw_prev)
        stat_prev[...] = jnp.zeros_like(stat_prev)

    def add_row(x, row_ref, lo, width):
        x3 = x.reshape(rows // 8, 8, width) + row_ref[:, lo:lo + width]
        return x3.reshape(rows, width)

    def mul_row(x, row_ref, lo, width):
        x3 = x.reshape(rows // 8, 8, width) * row_ref[:, lo:lo + width]
        return x3.reshape(rows, width)

    def proj_chunk(hb_ref, lo):
        z = jnp.dot(hb_ref[...], w_s[:, lo:lo + GM_TN], preferred_element_type=F32)
        return add_row(z, b_ref, lo, GM_TN).astype(BF16)

    def stage_b_chunk(lo):
        u_ref[:, lo:lo + GM_TN] = _gelu_tanh(proj_chunk(hb_prev, lo))
        for k in range(lo, lo + GM_TN, 128):
            vk = vraw_prev[:, k:k + 128].astype(F32) * stat_prev[:, 0:128] + stat_prev[:, 128:256]
            vn = add_row(mul_row(vk, lg_ref, k, 128), lb_ref, k, 128)
            v_ref[:, k:k + 128] = vn.astype(BF16)
            vs_ref[0, :, k:k + 128] = vn

    def stage_a_chunk(n, lo):
        gz = _gelu_tanh(proj_chunk(hb_cur, GM_HALF + lo))
        vraw_cur[:, lo:lo + GM_TN] = gz
        gf = gz.astype(F32)
        pieces = [gf[:, k:k + 128] for k in range(0, GM_TN, 128)]
        t1 = functools.reduce(lambda p, q: p + q, pieces)
        t2 = functools.reduce(lambda p, q: p + q, [p * p for p in pieces])
        if n == 0:
            sum_s[:, 0:128] = t1
            sum_s[:, 128:256] = t2
        else:
            sum_s[:, 0:128] += t1
            sum_s[:, 128:256] += t2

    chunks = list(range(0, GM_HALF, GM_TN))
    stage_b_chunk(chunks[0])
    a_is_prompt = jnp.minimum(i, n_tiles - 1) < n_prompt_tiles
    x = _add_moe(x_ref[...], yg0_ref, yg1_ref, gates_ref, _tile_mod(refs[4:6], a_is_prompt))
    xo_ref[...] = x
    hb_cur[...] = _norm_mod(x, gain_ref, _tile_mod(refs[6:8], a_is_prompt), 0).astype(BF16)
    for n, lo in enumerate(chunks):
        stage_a_chunk(n, lo)
        if 0 < n < len(chunks) - 1:
            stage_b_chunk(lo)
    mu = jnp.sum(sum_s[:, 0:128], axis=-1, keepdims=True) * (1.0 / GM_HALF)
    var = jnp.sum(sum_s[:, 128:256], axis=-1, keepdims=True) * (1.0 / GM_HALF) - mu * mu
    rstd = lax.rsqrt(var + EPS)
    stage_b_chunk(chunks[-1])
    stat_prev[:, 0:128] = jnp.broadcast_to(rstd, (rows, 128))
    stat_prev[:, 128:256] = jnp.broadcast_to(-mu * rstd, (rows, 128))
    hb_prev[...] = hb_cur[...]
    vraw_prev[...] = vraw_cur[...]

    if layer == 0:
        @pl.when(i - 1 >= n_prompt_tiles)
        def _():
            for later in range(1, N_GM):
                vs_ref[later] = jnp.zeros((rows, GM_HALF), F32)


def _gm_proj_call(x, prev, mod, gain, w_in, layer, b_in, ln_g, ln_b, vs_all):
    rows = ROWS_GM_PROJ
    n, npt = _n_tiles(rows), _n_prompt_tiles(rows)

    def stage_a(spec):
        return _stage_a_spec(spec, n)

    stage_b = _stage_b_spec
    in_specs = [stage_a(s) for s in [_row_spec(rows, D)] + _prev_specs(rows) + _mod_specs(rows)] + [
        _resident((1, D)), pl.BlockSpec(memory_space=pl.ANY), _resident((8, GM_FFN)),
        _resident((8, GM_HALF)), _resident((8, GM_HALF))]
    rows8 = [jnp.broadcast_to(r.reshape(1, -1), (8, r.size)) for r in (b_in, ln_g, ln_b)]
    args = [x, *prev, *mod, gain, w_in, *rows8]
    if layer == 0:
        assert vs_all is None
        vs_spec = pl.BlockSpec((N_GM, rows, GM_HALF), lambda i: (0, jnp.maximum(i - 1 - npt, 0), 0))
        aliases = {}
    else:
        in_specs.append(pl.BlockSpec(memory_space=pl.ANY))
        args.append(vs_all)
        vs_spec = pl.BlockSpec((1, rows, GM_HALF), lambda i: (layer, jnp.maximum(i - 1 - npt, 0), 0))
        aliases = {len(args) - 1: 3}
    return pl.pallas_call(
        functools.partial(_gm_proj_kernel, n, npt, layer),
        grid=(n + 1,),
        in_specs=in_specs,
        out_specs=[stage_a(_row_spec(rows, D)), stage_b(_row_spec(rows, GM_HALF)),
                   stage_b(_row_spec(rows, GM_HALF)), vs_spec],
        out_shape=[jax.ShapeDtypeStruct((T_ALL, D), F32),
                   jax.ShapeDtypeStruct((T_ALL, GM_HALF), BF16),
                   jax.ShapeDtypeStruct((T_ALL, GM_HALF), BF16),
                   jax.ShapeDtypeStruct((N_GM, T_SAMPLE, GM_HALF), F32)],
        scratch_shapes=[pltpu.VMEM((rows, D), BF16), pltpu.VMEM((rows, D), BF16),
                        pltpu.VMEM((rows, GM_HALF), BF16), pltpu.VMEM((rows, GM_HALF), BF16),
                        pltpu.VMEM((rows, 256), F32), pltpu.VMEM((rows, 256), F32),
                        *_weight_scratch(D, GM_FFN)],
        input_output_aliases=aliases,
        compiler_params=_params("arbitrary"),
        name="gm_proj",
    )(*args)


def _gm_out_kernel(n_prompt_tiles, layer, u_ref, v_ref, ws_ref, bs_ref, w_hbm, bo_ref, x_ref, modp_ref,
                   mods_ref, gain_ref, wr_ref, br_ref, xo_ref, h2_ref, lg_ref, w_s, w_stage, w_sems):
    _load_weight_bf16(w_hbm, layer, w_s, w_stage, w_sems)
    rows = u_ref.shape[0]
    mod = _tile_mod((modp_ref, mods_ref), pl.program_id(0) < n_prompt_tiles)
    pieces = []
    for r0 in range(0, rows, GM_MIX):
        acc = jnp.zeros((GM_MIX, D), F32)
        for g in range(GM_GROUPS):
            lo = g * GM_GDIM
            sp = jnp.dot(ws_ref[0, g], v_ref[r0:r0 + GM_MIX, lo:lo + GM_GDIM],
                         preferred_element_type=F32) + bs_ref[0, g]
            gated = (u_ref[r0:r0 + GM_MIX, lo:lo + GM_GDIM].astype(F32) * sp).astype(BF16)
            acc = acc + jnp.dot(gated, w_s[lo:lo + GM_GDIM, :], preferred_element_type=F32)
        pieces.append(acc)
    acc = jnp.concatenate(pieces, axis=0) + bo_ref[...]
    _residual_router(acc, x_ref, mod, gain_ref, wr_ref, br_ref, xo_ref, h2_ref, lg_ref)


def _gm_block_diag(w_s, b_s):
    mats, biases = [], []
    for cl in (GM_CHUNK, DEC_SEQ):
        tri = jnp.tril(jnp.ones((cl, cl), bool))
        blk = jnp.where(tri[None], w_s[:, :cl, :cl], 0.0)
        reps = GM_MIX // cl
        eye = jnp.eye(reps, dtype=w_s.dtype)
        bd = jnp.einsum("ab,gts->gatbs", eye, blk).reshape(GM_GROUPS, GM_MIX, GM_MIX)
        mats.append(bd)
        biases.append(jnp.tile(b_s[:, :cl], (1, reps))[:, :, None])
    return jnp.stack(mats).astype(BF16), jnp.stack(biases).astype(F32)


def _gm_out_call(u, v, ws_bd, bs_bd, w_out, layer, b_out, x, mod, gain, w_r, b_r):
    rows = ROWS_WIDE
    npt = _n_prompt_tiles(rows)

    def variant(i):
        return jnp.where(i >= npt, 1, 0)

    return pl.pallas_call(
        functools.partial(_gm_out_kernel, npt, layer),
        grid=(_n_tiles(rows),),
        in_specs=[_row_spec(rows, GM_HALF), _row_spec(rows, GM_HALF),
                  pl.BlockSpec((1, GM_GROUPS, GM_MIX, GM_MIX), lambda i: (variant(i), 0, 0, 0)),
                  pl.BlockSpec((1, GM_GROUPS, GM_MIX, 1), lambda i: (variant(i), 0, 0, 0)),
                  pl.BlockSpec(memory_space=pl.ANY), _resident((1, D)), _row_spec(rows, D),
                  *_mod_specs(rows), _resident((1, D))] + _router_specs(),
        out_specs=_mix_out_specs(rows),
        out_shape=_MIX_OUT_SHAPE,
        scratch_shapes=_weight_scratch(GM_HALF, D),
        compiler_params=_params("arbitrary"),
        name="gm_out",
    )(u, v, ws_bd, bs_bd, w_out, b_out, x, *mod, gain, w_r, b_r)


def _expert_kernel(layer, be_ref, bv_ref, nb_ref, nx_ref, xb_ref, w1_hbm, w3_hbm, w2_hbm, yb_ref,
                   st1, st3, st2, w1s, w3s, w2s, sems, slot_ref):
    b = pl.program_id(0)

    def weight_copies(e, slot):
        return (pltpu.make_async_copy(w1_hbm.at[layer, e], st1.at[slot], sems.at[slot, 0]),
                pltpu.make_async_copy(w3_hbm.at[layer, e], st3.at[slot], sems.at[slot, 1]),
                pltpu.make_async_copy(w2_hbm.at[layer, e], st2.at[slot], sems.at[slot, 2]))

    @pl.when(b == 0)
    def _():
        slot_ref[0] = 0
        for cp in weight_copies(be_ref[0], 0):
            cp.start()

    @pl.when(b < nb_ref[0])
    def _():
        prev_e = be_ref[jnp.maximum(b - 1, 0)]

        @pl.when((b == 0) | (be_ref[b] != prev_e))
        def _():
            slot = slot_ref[0]
            for cp in weight_copies(be_ref[b], slot):
                cp.wait()
            w1s[...] = st1[slot].astype(BF16)
            w3s[...] = st3[slot].astype(BF16)
            w2s[...] = st2[slot].astype(BF16)

            @pl.when(nx_ref[b] >= 0)
            def _():
                for cp in weight_copies(nx_ref[b], 1 - slot):
                    cp.start()

            slot_ref[0] = 1 - slot

        valid = bv_ref[b]

        def run_rows(n):
            row = lax.broadcasted_iota(jnp.int32, (n, 1), 0)
            xw = jnp.where(row < valid, xb_ref[0:n], U32(0))
            x = _unpack_bf16_pairs(xw).astype(BF16)
            a = jnp.dot(x, w1s[...], preferred_element_type=F32)
            c = jnp.dot(x, w3s[...], preferred_element_type=F32)
            h = (_silu(a) * c).astype(BF16)
            yb_ref[0:n] = _pack_bf16_pairs(jnp.dot(h, w2s[...], preferred_element_type=F32))
            if n < EXP_BLOCK:
                yb_ref[n:EXP_BLOCK] = jnp.zeros((EXP_BLOCK - n, DP), U32)

        for n in range(EXP_ROW_STEP, EXP_BLOCK + 1, EXP_ROW_STEP):
            pl.when((valid > n - EXP_ROW_STEP) & (valid <= n))(functools.partial(run_rows, n))


def _expert_call(blk_e, blk_valid, n_blk, blk_next, xb, w1, w3, w2, layer):
    def blk(b, be, bv, nb, nx):
        return (jnp.minimum(b, nb[0] - 1), 0)

    up, down = (D, MOE_HIDDEN), (MOE_HIDDEN, D)
    grid_spec = pltpu.PrefetchScalarGridSpec(
        num_scalar_prefetch=4,
        grid=(N_EXP_BLOCKS,),
        in_specs=[pl.BlockSpec((EXP_BLOCK, DP), blk)] + [pl.BlockSpec(memory_space=pl.ANY)] * 3,
        out_specs=pl.BlockSpec((EXP_BLOCK, DP), blk),
        scratch_shapes=[pltpu.VMEM((2,) + up, F32), pltpu.VMEM((2,) + up, F32), pltpu.VMEM((2,) + down, F32),
                        pltpu.VMEM(up, BF16), pltpu.VMEM(up, BF16), pltpu.VMEM(down, BF16),
                        pltpu.SemaphoreType.DMA((2, 3)), pltpu.SMEM((1,), jnp.int32)],
    )
    return pl.pallas_call(
        functools.partial(_expert_kernel, layer),
        grid_spec=grid_spec,
        out_shape=jax.ShapeDtypeStruct((P_ROWS, DP), U32),
        compiler_params=_params("arbitrary"),
        name="experts",
    )(blk_e, blk_valid, n_blk, blk_next, xb, w1, w3, w2)


def _route_kernel(lg_ref, dest_ref, gates_ref, meta_ref, cnt_ref):
    t = pl.program_id(0)

    @pl.when(t == 0)
    def _():
        cnt_ref[...] = jnp.zeros_like(cnt_ref)

    pl.when(t < N_ROUTE_TILES)(functools.partial(_route_tile, t, lg_ref, dest_ref, gates_ref, cnt_ref))
    pl.when(t == N_ROUTE_TILES)(functools.partial(_route_finish, dest_ref, meta_ref, cnt_ref))


ROUTE_RANK_BITS = 16
assert P_ROWS < 2 ** ROUTE_RANK_BITS


def _route_tile(t, lg_ref, dest_ref, gates_ref, cnt_ref):
    tm = ROUTE_TM
    lt = lg_ref[...].T
    el = lt[0:MOE_EXPERTS]
    gl = lt[MOE_EXPERTS:MOE_EXPERTS + 8]
    gidx = lax.broadcasted_iota(jnp.int32, (8, tm), 0)
    neg = jnp.float32(-jnp.inf)
    gl = jnp.where(gidx < MOE_GROUPS, gl, neg)
    gmax = jnp.max(gl, axis=0, keepdims=True)
    grp = jnp.min(jnp.where(gl == gmax, gidx, MOE_GROUPS), axis=0, keepdims=True)
    eidx = lax.broadcasted_iota(jnp.int32, (MOE_EXPERTS, tm), 0)
    els = jnp.where((eidx >> 3) == grp, el, neg)
    m1 = jnp.max(els, axis=0, keepdims=True)
    i1 = jnp.min(jnp.where(els == m1, eidx, MOE_EXPERTS), axis=0, keepdims=True)
    els2 = jnp.where(eidx == i1, neg, els)
    m2 = jnp.max(els2, axis=0, keepdims=True)
    i2 = jnp.min(jnp.where(els2 == m2, eidx, MOE_EXPERTS), axis=0, keepdims=True)
    sel1 = eidx == i1
    sel2 = eidx == i2
    cnt = jnp.where(sel1 | sel2, 1.0, 0.0)

    lane = ROUTER_LANES
    before = (lax.broadcasted_iota(jnp.int32, (lane, lane), 0)
              < lax.broadcasted_iota(jnp.int32, (lane, lane), 1))
    tri = jnp.where(before, 1.0, 0.0).astype(BF16)
    run = cnt_ref[...]
    r1, r2 = [], []
    for k in range(tm // lane):
        piece = slice(k * lane, (k + 1) * lane)
        ck = cnt[:, piece]
        pos = run + jnp.dot(ck.astype(BF16), tri, preferred_element_type=F32)
        r1.append(jnp.sum(jnp.where(sel1[:, piece], pos, 0.0), axis=0, keepdims=True))
        r2.append(jnp.sum(jnp.where(sel2[:, piece], pos, 0.0), axis=0, keepdims=True))
        run = run + jnp.sum(ck, axis=1, keepdims=True)
    cnt_ref[...] = run
    rank = jnp.concatenate([jnp.concatenate(r1, axis=1), jnp.concatenate(r2, axis=1)], axis=0)
    eid = jnp.concatenate([i1, i2], axis=0)
    dest_ref[:, pl.ds(pl.multiple_of(t * tm, tm), tm)] = (eid << ROUTE_RANK_BITS) + rank.astype(jnp.int32)

    g_w = 1.0 / jnp.sum(jnp.exp(gl - gmax), axis=0, keepdims=True)
    e21 = jnp.exp(m2 - m1)
    p1 = 1.0 / (1.0 + e21)
    rid = lax.broadcasted_iota(jnp.int32, (ROUTER_LANES, tm), 0)
    gt = jnp.where(rid == 0, g_w * p1, jnp.where(rid == 1, g_w * (e21 * p1), 0.0))
    gates_ref[...] = gt.T


def _route_finish(dest_ref, meta_ref, cnt_ref):
    counts = cnt_ref[...]
    nblk = jnp.floor((counts + (EXP_BLOCK - 1.0)) * (1.0 / EXP_BLOCK))
    r = lax.broadcasted_iota(jnp.int32, (MOE_EXPERTS, MOE_EXPERTS), 0)
    c = lax.broadcasted_iota(jnp.int32, (MOE_EXPERTS, MOE_EXPERTS), 1)
    nblk_row = jnp.sum(jnp.where(r == c, nblk, 0.0), axis=0, keepdims=True)
    bstart = jnp.sum(jnp.where(c < r, nblk_row, 0.0), axis=1, keepdims=True)
    bend = bstart + nblk
    bidx = lax.broadcasted_iota(jnp.int32, (1, META_LANES), 1).astype(F32)
    blk_e = jnp.minimum(jnp.sum(jnp.where(bidx >= bend, 1.0, 0.0), axis=0, keepdims=True),
                        MOE_EXPERTS - 1.0)
    erow = lax.broadcasted_iota(jnp.int32, (MOE_EXPERTS, META_LANES), 0).astype(F32)
    mine = erow == blk_e
    cnt_b = jnp.sum(jnp.where(mine, counts, 0.0), axis=0, keepdims=True)
    start_b = jnp.sum(jnp.where(mine, bstart, 0.0), axis=0, keepdims=True)
    valid = jnp.clip(cnt_b - (bidx - start_b) * EXP_BLOCK, 0.0, float(EXP_BLOCK))
    n_blk = jnp.sum(nblk, axis=0, keepdims=True)
    end_b = jnp.sum(jnp.where(mine, bend, 0.0), axis=0, keepdims=True)
    nxt = jnp.minimum(jnp.sum(jnp.where(end_b >= bend, 1.0, 0.0), axis=0, keepdims=True),
                      MOE_EXPERTS - 1.0)
    nxt = jnp.where(end_b < n_blk, nxt, -1.0)
    mrow = lax.broadcasted_iota(jnp.int32, (8, META_LANES), 0)
    meta = jnp.where(mrow == 0, blk_e, jnp.where(mrow == 1, valid, jnp.where(
        mrow == 2, n_blk, jnp.where(mrow == 3, nxt, 0.0))))
    meta_ref[...] = meta.astype(jnp.int32)

    base = (bstart * EXP_BLOCK).astype(jnp.int32)
    packed = dest_ref[...]
    eid = packed >> ROUTE_RANK_BITS
    row = packed & (2 ** ROUTE_RANK_BITS - 1)
    for e in range(MOE_EXPERTS):
        row = row + jnp.where(eid == e, base[e:e + 1, :], 0)
    dest_ref[...] = row


def _route_call(logits):
    last = N_ROUTE_TILES - 1
    return pl.pallas_call(
        _route_kernel,
        grid=(N_ROUTE_TILES + 1,),
        in_specs=[pl.BlockSpec((ROUTE_TM, ROUTER_LANES), lambda t: (jnp.minimum(t, last), 0))],
        out_specs=[pl.BlockSpec((MOE_TOPK, T_ALL), lambda t: (0, 0)),
                   pl.BlockSpec((ROUTE_TM, ROUTER_LANES), lambda t: (jnp.minimum(t, last), 0)),
                   pl.BlockSpec((8, META_LANES), lambda t: (0, 0))],
        out_shape=[jax.ShapeDtypeStruct((MOE_TOPK, T_ALL), jnp.int32),
                   jax.ShapeDtypeStruct((T_ALL, ROUTER_LANES), F32),
                   jax.ShapeDtypeStruct((8, META_LANES), jnp.int32)],
        scratch_shapes=[pltpu.VMEM((MOE_EXPERTS, 1), F32)],
        compiler_params=_params("arbitrary"),
        name="route",
    )(logits)


def _sc_mesh():
    return plsc.VectorSubcoreMesh(core_axis_name="c", subcore_axis_name="s")


def _sc_token_offset(j):
    wid = lax.axis_index("s") * SC_CORES + lax.axis_index("c")
    return pl.multiple_of(wid * SC_ROWS_PER_WORKER + j * SC_CHUNK, 8)


SC_N_CHUNKS = SC_ROWS_PER_WORKER // SC_CHUNK
assert SC_N_CHUNKS % 2 == 1


def _start(copies):
    for cp in copies:
        cp.start()


def _wait(copies):
    for cp in copies:
        cp.wait()


def _dispatch_body(h_hbm, d0_hbm, d1_hbm, out_hbm, i0a, i1a, rows_a, i0b, i1b, rows_b, la, lb, sa, sb):
    sets = {"a": (i0a, i1a, rows_a, la, sa), "b": (i0b, i1b, rows_b, lb, sb)}

    def loads(j, s):
        i0, i1, rows, lsem, _ = sets[s]
        src = pl.ds(_sc_token_offset(j), SC_CHUNK)
        return (pltpu.make_async_copy(d0_hbm.at[src], i0, lsem),
                pltpu.make_async_copy(d1_hbm.at[src], i1, lsem),
                pltpu.make_async_copy(h_hbm.at[src], rows, lsem))

    def scatters(s):
        i0, i1, rows, _, ssem = sets[s]
        return (pltpu.make_async_copy(rows, out_hbm.at[i0], ssem),
                pltpu.make_async_copy(rows, out_hbm.at[i1], ssem))

    _start(loads(0, "a"))

    @pl.loop(0, SC_N_CHUNKS // 2)
    def _(p):
        a = 2 * p
        _start(loads(a + 1, "b"))
        _wait(loads(a, "a"))
        _start(scatters("a"))
        _wait(loads(a + 1, "b"))
        _start(scatters("b"))
        _wait(scatters("a"))
        _start(loads(a + 2, "a"))
        _wait(scatters("b"))

    _wait(loads(SC_N_CHUNKS - 1, "a"))
    _start(scatters("a"))
    _wait(scatters("a"))


def _dispatch_call(h2, dest0, dest1):
    buffers = [pltpu.VMEM((SC_CHUNK,), jnp.int32), pltpu.VMEM((SC_CHUNK,), jnp.int32),
               pltpu.VMEM((SC_CHUNK, DP), U32)]
    return pl.kernel(
        _dispatch_body,
        out_type=jax.ShapeDtypeStruct((P_ROWS, DP), U32),
        mesh=_sc_mesh(),
        scratch_types=buffers + buffers + [pltpu.SemaphoreType.DMA] * 4,
        name="moe_dispatch",
    )(h2, dest0, dest1)


def _combine_body(yb_hbm, d0_hbm, d1_hbm, o0_hbm, o1_hbm, i0a, i1a, r0a, r1a, i0b, i1b, r0b, r1b,
                  ia, ib, ga, gb, wa, wb):
    sets = {"a": (i0a, i1a, r0a, r1a, ia, ga, wa), "b": (i0b, i1b, r0b, r1b, ib, gb, wb)}

    def index_loads(j, s):
        i0, i1, _, _, isem, _, _ = sets[s]
        src = pl.ds(_sc_token_offset(j), SC_CHUNK)
        return (pltpu.make_async_copy(d0_hbm.at[src], i0, isem), pltpu.make_async_copy(d1_hbm.at[src], i1, isem))

    def gathers(s):
        i0, i1, r0, r1, _, gsem, _ = sets[s]
        return (pltpu.make_async_copy(yb_hbm.at[i0], r0, gsem), pltpu.make_async_copy(yb_hbm.at[i1], r1, gsem))

    def writes(j, s):
        _, _, r0, r1, _, _, wsem = sets[s]
        dst = pl.ds(_sc_token_offset(j), SC_CHUNK)
        return (pltpu.make_async_copy(r0, o0_hbm.at[dst], wsem), pltpu.make_async_copy(r1, o1_hbm.at[dst], wsem))

    _start(index_loads(0, "a"))
    _wait(index_loads(0, "a"))
    _start(gathers("a"))

    @pl.loop(0, SC_N_CHUNKS // 2)
    def _(p):
        a = 2 * p
        _start(index_loads(a + 1, "b"))
        _wait(index_loads(a + 1, "b"))
        _wait(gathers("a"))
        _start(writes(a, "a"))
        _start(gathers("b"))
        _start(index_loads(a + 2, "a"))
        _wait(index_loads(a + 2, "a"))
        _wait(writes(a, "a"))
        _wait(gathers("b"))
        _start(writes(a + 1, "b"))
        _start(gathers("a"))
        _wait(writes(a + 1, "b"))

    _wait(gathers("a"))
    _start(writes(SC_N_CHUNKS - 1, "a"))
    _wait(writes(SC_N_CHUNKS - 1, "a"))


def _combine_call(yb, dest0, dest1):
    out = jax.ShapeDtypeStruct((T_ALL, DP), U32)
    buffers = [pltpu.VMEM((SC_CHUNK,), jnp.int32), pltpu.VMEM((SC_CHUNK,), jnp.int32),
               pltpu.VMEM((SC_CHUNK, DP), U32), pltpu.VMEM((SC_CHUNK, DP), U32)]
    return pl.kernel(
        _combine_body,
        out_type=(out, out),
        mesh=_sc_mesh(),
        scratch_types=buffers + buffers + [pltpu.SemaphoreType.DMA] * 6,
        name="moe_combine",
    )(yb, dest0, dest1)


def _moe_rows(h2, dest, meta, w1, w3, w2, layer):
    dest0, dest1 = dest[0], dest[1]
    xb = _dispatch_call(h2, dest0, dest1)
    yb = _expert_call(meta[0, :N_EXP_BLOCKS], meta[1, :N_EXP_BLOCKS], meta[2, :1], meta[3, :N_EXP_BLOCKS],
                      xb, w1, w3, w2, layer)
    return _combine_call(yb, dest0, dest1)


def _final_kernel(is_prompt, x_ref, yg0_ref, yg1_ref, gates_ref, modp_ref, mods_ref, gain_ref, o_ref):
    x = _add_moe(x_ref[...], yg0_ref, yg1_ref, gates_ref, _tile_mod((modp_ref, mods_ref), is_prompt))
    o_ref[...] = _rms(x) * gain_ref[...]


def _final_call(x, prev, gain, row0, n_rows):
    rows = ROWS_WIDE
    tile0 = row0 // rows
    is_prompt = row0 < T_PROMPT
    assert row0 + n_rows <= T_PROMPT or not is_prompt

    def tile(width):
        return pl.BlockSpec((rows, width), lambda i: (tile0 + i, 0))

    mod_specs = [pl.BlockSpec(s.block_shape, lambda i, m=s.index_map: m(tile0 + i)) for s in _mod_specs(rows)]
    return pl.pallas_call(
        functools.partial(_final_kernel, is_prompt),
        grid=(n_rows // rows,),
        in_specs=[tile(D), tile(DP), tile(DP), tile(ROUTER_LANES), *mod_specs, _resident((1, D))],
        out_specs=pl.BlockSpec((rows, D), lambda i: (i, 0)),
        out_shape=jax.ShapeDtypeStruct((n_rows, D), F32),
        compiler_params=_params("parallel"),
        name="final_norm",
    )(x, *prev, gain)


def _rope_tables():
    pos = np.concatenate([np.tile(np.arange(SEQ), BATCH),
                          np.tile(PAST_LEN + np.arange(DEC_SEQ), DEC_BATCH)]).astype(np.float32)
    inv = (ROPE_BASE ** (-np.arange(ROPE_HALF, dtype=np.float32) / ROPE_HALF)).astype(np.float32)
    ang = (pos[:, None] * inv[None, :]).astype(np.float32).astype(np.float64)
    return jnp.asarray(np.cos(ang), F32), jnp.asarray(np.sin(ang), F32)


def kernel(x_prompt, x_sample, c_prompt, c_sample, state_ret, ada_w, ada_b, norm1_g, norm2_g, ret_w_in,
           ret_w_out, gm_w_in, gm_b_in, gm_ln_g, gm_ln_b, gm_w_s, gm_b_s, gm_w_out, gm_b_out, moe_w_rg,
           moe_b_rg, moe_w_re, moe_b_re, moe_w1, moe_w3, moe_w2, final_g):
    x = (x_prompt.reshape(T_PROMPT, D), x_sample.reshape(T_SAMPLE, D))
    c_all = jnp.concatenate([c_prompt, c_sample], axis=0)
    cos, sin = _rope_tables()
    dec = _ret_row_scales()

    mod_all = _ada_call(c_all, ada_w, ada_b).reshape(DEPTH, N_SEQ, 6, D)

    def layer_params(i):
        mod = (mod_all[i, :BATCH], mod_all[i, BATCH:])
        w_r = jnp.pad(jnp.concatenate([moe_w_re[i], moe_w_rg[i]], axis=1),
                      ((0, 0), (0, ROUTER_LANES - MOE_GROUPS - MOE_EXPERTS)))
        w_r_hi = w_r.astype(BF16)
        w_r_lo = (w_r - w_r_hi.astype(F32)).astype(BF16)
        w_r = jnp.concatenate([w_r_hi, w_r_lo], axis=1)
        b_r = jnp.pad(jnp.concatenate([moe_b_re[i].reshape(-1), moe_b_rg[i]]),
                      (0, ROUTER_LANES - MOE_GROUPS - MOE_EXPERTS)).reshape(1, ROUTER_LANES)
        return mod, w_r, b_r

    ret_prompt = ret_sample = gm_sample = None
    prev = None
    for i in range(DEPTH):
        j = i // 2
        mod, w_r, b_r = layer_params(i)
        g1 = norm1_g[i].reshape(1, D)
        g2 = norm2_g[i].reshape(1, D)
        if i % 2 == 0:
            x, p = _ret_proj_call(x, prev, mod, g1, ret_w_in, j, cos, sin, dec)
            y_p, ret_prompt = _ret_core_call(p, None, ret_prompt, j, BATCH, SEQ, RET_CHUNK_PROMPT, 0)
            y_s, ret_sample = _ret_core_call(p, state_ret, ret_sample, j, DEC_BATCH, DEC_SEQ,
                                             RET_CHUNK_SAMPLE, T_PROMPT)
            x, h2, logits = _ret_out_call(y_p, y_s, ret_w_out, j, x, mod, g2, w_r, b_r)
        else:
            x, u, v, gm_sample = _gm_proj_call(x, prev, mod, g1, gm_w_in, j, gm_b_in[j], gm_ln_g[j],
                                               gm_ln_b[j], gm_sample)
            ws_bd, bs_bd = _gm_block_diag(gm_w_s[j], gm_b_s[j])
            x, h2, logits = _gm_out_call(u, v, ws_bd, bs_bd, gm_w_out, j,
                                         gm_b_out[j].reshape(1, D), x, mod, g2, w_r, b_r)
        dest, gates, meta = _route_call(logits)
        yg0, yg1 = _moe_rows(h2, dest, meta, moe_w1, moe_w3, moe_w2, i)
        prev = (yg0, yg1, gates, *mod)

    fg = final_g.reshape(1, D)
    y_prompt = _final_call(x, prev, fg, 0, T_PROMPT).reshape(BATCH, SEQ, D)
    y_sample = _final_call(x, prev, fg, T_PROMPT, T_SAMPLE).reshape(DEC_BATCH, DEC_SEQ, D)
    return (y_prompt, y_sample, ret_prompt, ret_sample,
            gm_sample.reshape(N_GM, DEC_BATCH, DEC_SEQ, GM_HALF))
```

```python
import functools

import numpy as np
import jax
import jax.numpy as jnp
from jax import lax
from jax.experimental import pallas as pl
from jax.experimental.pallas import tpu as pltpu
from jax.experimental.pallas import tpu_sc as plsc

F32 = jnp.float32
BF16 = jnp.bfloat16
U32 = jnp.uint32

D = 1024
BATCH, SEQ = 4, 4096
DEC_BATCH, DEC_SEQ = 16, 64
PAST_LEN = 4096
DEPTH = 4
N_RET = (DEPTH + 1) // 2
N_GM = DEPTH // 2
N_SEQ = BATCH + DEC_BATCH

RET_HEADS, RET_DK, RET_DV = 4, 256, 512
RET_QK = RET_HEADS * RET_DK
RET_V = RET_HEADS * RET_DV
RET_IN = 2 * RET_QK + 2 * RET_V
ROPE_BASE = 10000.0
ROPE_HALF = RET_DK // 2
ROPE_TABLE_LANES = 3 * ROPE_HALF

GM_FFN = 6 * D
GM_HALF = GM_FFN // 2
GM_GROUPS = 4
GM_GDIM = GM_HALF // GM_GROUPS
GM_CHUNK = 128

MOE_GROUPS, MOE_PER_GROUP = 4, 8
MOE_EXPERTS = MOE_GROUPS * MOE_PER_GROUP
MOE_TOPK = 2
MOE_HIDDEN = 512
EPS = 1e-6

GROUP = DEC_SEQ
T_PROMPT = BATCH * SEQ
T_SAMPLE = DEC_BATCH * DEC_SEQ
T_ALL = T_PROMPT + T_SAMPLE
N_GROUPS = T_ALL // GROUP
ROWS_WIDE = 512
ROWS_GM_PROJ = 256

RET_CHUNK_PROMPT = 256
RET_CHUNK_SAMPLE = DEC_SEQ

GM_MIX = 256

EXP_BLOCK = 512
EXP_ROW_STEP = 128
N_ASSIGN = T_ALL * MOE_TOPK
N_EXP_BLOCKS = -(-(N_ASSIGN + MOE_EXPERTS * (EXP_BLOCK - 1)) // EXP_BLOCK)
P_ROWS = N_EXP_BLOCKS * EXP_BLOCK
ROUTER_LANES = 128
ROUTE_TM = 1024
N_ROUTE_TILES = T_ALL // ROUTE_TM
META_LANES = 256
assert META_LANES >= N_EXP_BLOCKS

DP = D // 2
SC_CORES, SC_SUBCORES = 2, 16
SC_WORKERS = SC_CORES * SC_SUBCORES
SC_ROWS_PER_WORKER = T_ALL // SC_WORKERS
SC_CHUNK = 32
assert SC_ROWS_PER_WORKER % SC_CHUNK == 0 and SC_CHUNK % 8 == 0

V7X_VMEM_LIMIT_BYTES = 56 * 1024 * 1024


def _params(*sem):
    return pltpu.CompilerParams(dimension_semantics=sem, vmem_limit_bytes=V7X_VMEM_LIMIT_BYTES)


def _resident(shape):
    nd = len(shape)
    return pl.BlockSpec(shape, lambda *_: (0,) * nd, pipeline_mode=pl.Buffered(1))


WEIGHT_STAGE_BYTES = 3 * 1024 * 1024


def _weight_scratch(k, n):
    chunk = k
    while chunk * n * 4 > WEIGHT_STAGE_BYTES:
        assert chunk % 16 == 0
        chunk //= 2
    return [pltpu.VMEM((k, n), BF16), pltpu.VMEM((2, chunk, n), F32), pltpu.SemaphoreType.DMA((2,))]


def _load_weight_bf16(w_hbm, layer, w_s, stage, sems):
    k = w_s.shape[0]
    chunk = stage.shape[1]

    def copy(c):
        return pltpu.make_async_copy(w_hbm.at[layer, pl.ds(c * chunk, chunk)], stage.at[c % 2], sems.at[c % 2])

    @pl.when(pl.program_id(0) == 0)
    def _():
        copy(0).start()
        for c in range(k // chunk):
            if c + 1 < k // chunk:
                copy(c + 1).start()
            copy(c).wait()
            w_s[c * chunk:(c + 1) * chunk, :] = stage[c % 2].astype(BF16)


def _rms(x):
    return x * lax.rsqrt(jnp.mean(x * x, axis=-1, keepdims=True) + EPS)


def _silu(x):
    return x * jax.nn.sigmoid(x)


def _per_group(x2d, fn):
    rows = x2d.shape[0]
    return fn(x2d.reshape(rows // GROUP, GROUP, D)).reshape(rows, D)


def _tile_mod(mod_refs, is_prompt):
    modp_ref, mods_ref = mod_refs
    return jnp.where(is_prompt, jnp.broadcast_to(modp_ref[...], mods_ref.shape), mods_ref[...])


def _norm_mod(x, gain_ref, mod, shift_idx):
    y = _rms(x) * gain_ref[...]
    scale = mod[:, shift_idx + 1:shift_idx + 2, :]
    shift = mod[:, shift_idx:shift_idx + 1, :]
    return _per_group(y, lambda y3: y3 * (1.0 + scale) + shift)


def _pack_bf16_pairs(x):
    lo = lax.bitcast_convert_type(x[:, :DP].astype(BF16).astype(F32), U32)
    hi = lax.bitcast_convert_type(x[:, DP:].astype(BF16).astype(F32), U32)
    return (lo >> 16) | (hi & U32(0xFFFF0000))


def _unpack_bf16_pairs(w):
    lo = lax.bitcast_convert_type(w << 16, F32)
    hi = lax.bitcast_convert_type(w & U32(0xFFFF0000), F32)
    return jnp.concatenate([lo, hi], axis=1)


def _add_moe(x, yg0_ref, yg1_ref, gates_ref, mod_prev):
    g = gates_ref[...]
    y = g[:, 0:1] * _unpack_bf16_pairs(yg0_ref[...]) + g[:, 1:2] * _unpack_bf16_pairs(yg1_ref[...])
    gate2 = mod_prev[:, 5:6, :]
    return x + _per_group(y, lambda y3: y3 * gate2)


ADA_TN = 1536


def _ada_kernel(c_ref, w_ref, b_ref, o_ref):
    c = c_ref[...]
    s = _silu(c).astype(BF16)
    o_ref[0] = jnp.dot(s, w_ref[0].astype(BF16), preferred_element_type=F32) + b_ref[0]


def _ada_call(c_all, ada_w, ada_b):
    return pl.pallas_call(
        _ada_kernel,
        grid=(DEPTH, 6 * D // ADA_TN),
        in_specs=[
            pl.BlockSpec((N_SEQ, D), lambda i, j: (0, 0)),
            pl.BlockSpec((1, D, ADA_TN), lambda i, j: (i, 0, j)),
            pl.BlockSpec((1, 1, ADA_TN), lambda i, j: (i, 0, j)),
        ],
        out_specs=pl.BlockSpec((1, N_SEQ, ADA_TN), lambda i, j: (i, 0, j)),
        out_shape=jax.ShapeDtypeStruct((DEPTH, N_SEQ, 6 * D), F32),
        compiler_params=_params("parallel", "parallel"),
        name="ada_modulation",
    )(c_all, ada_w, ada_b.reshape(DEPTH, 1, 6 * D))


def _n_tiles(rows):
    return T_ALL // rows


def _n_prompt_tiles(rows):
    return T_PROMPT // rows


def _row_spec(rows, width):
    return pl.BlockSpec((rows, width), lambda i: (i, 0))


def _mod_specs(rows):
    npt = _n_prompt_tiles(rows)
    return [pl.BlockSpec((1, 6, D), lambda i: (jnp.minimum(i * rows // SEQ, BATCH - 1), 0, 0)),
            pl.BlockSpec((rows // DEC_SEQ, 6, D), lambda i: (jnp.maximum(i - npt, 0), 0, 0))]


def _prev_specs(rows):
    return [_row_spec(rows, DP), _row_spec(rows, DP), _row_spec(rows, ROUTER_LANES)] + _mod_specs(rows)


def _stage_a_spec(spec, n_tiles):
    return pl.BlockSpec(spec.block_shape, lambda i, m=spec.index_map: m(jnp.minimum(i, n_tiles - 1)))


def _stage_b_spec(spec):
    return pl.BlockSpec(spec.block_shape, lambda i, m=spec.index_map: m(jnp.maximum(i - 1, 0)))


def _prompt_rows_spec(rows, width):
    last = _n_prompt_tiles(rows) - 1
    return pl.BlockSpec((rows, width), lambda i: (jnp.minimum(i, last), 0))


def _sample_rows_spec(rows, width):
    npt = _n_prompt_tiles(rows)
    return pl.BlockSpec((rows, width), lambda i: (jnp.maximum(i - npt, 0), 0))


def _ret_proj_kernel(has_prev, n_tiles, n_prompt_tiles, layer, *refs):
    i = pl.program_id(0)
    is_prompt = jnp.minimum(i, n_tiles - 1) < n_prompt_tiles
    head = refs[:6] if has_prev else refs[:2]
    refs = refs[len(head):]
    mod_refs = refs[:2]
    gain_ref, w_hbm, rope_ref, xo_ref, p_ref, hb_cur, hb_prev, w_s, w_stage, w_sems = refs[2:]
    _load_weight_bf16(w_hbm, layer, w_s, w_stage, w_sems)

    @pl.when(i == 0)
    def _():
        hb_prev[...] = jnp.zeros_like(hb_prev)

    def qk_head(j):
        lo = j * RET_DK
        acc = jnp.dot(hb_prev[...], w_s[:, lo:lo + RET_DK], preferred_element_type=F32)
        x1 = acc[:, :ROPE_HALF]
        x2 = acc[:, ROPE_HALF:]
        cos = rope_ref[:, 0:ROPE_HALF]
        sin = rope_ref[:, ROPE_HALF:2 * ROPE_HALF]
        scale = rope_ref[:, 2 * ROPE_HALF + j:2 * ROPE_HALF + j + 1]
        p_ref[:, lo:lo + ROPE_HALF] = ((x1 * cos - x2 * sin) * scale).astype(BF16)
        p_ref[:, lo + ROPE_HALF:lo + RET_DK] = ((x1 * sin + x2 * cos) * scale).astype(BF16)

    def vg_head(j):
        lo = 2 * RET_QK + j * RET_DV
        acc = jnp.dot(hb_prev[...], w_s[:, lo:lo + RET_DV], preferred_element_type=F32)
        if j >= RET_HEADS:
            acc = _silu(acc)
        p_ref[:, lo:lo + RET_DV] = acc.astype(BF16)

    vg_head(0)
    vg_head(1)
    if has_prev:
        x_ref, yg0_ref, yg1_ref, gates_ref = head[:4]
        x = _add_moe(x_ref[...], yg0_ref, yg1_ref, gates_ref, _tile_mod(head[4:6], is_prompt))
    else:
        x = jnp.where(is_prompt, head[0][...], head[1][...])
    xo_ref[...] = x
    hb_cur[...] = _norm_mod(x, gain_ref, _tile_mod(mod_refs, is_prompt), 0).astype(BF16)
    for j in range(2, 2 * RET_HEADS):
        vg_head(j)
    for j in range(2 * RET_HEADS):
        qk_head(j)
    hb_prev[...] = hb_cur[...]


def _ret_proj_call(x, prev, mod, gain, w_in, layer, rope):
    rows = ROWS_WIDE
    n = _n_tiles(rows)
    has_prev = prev is not None
    if has_prev:
        in_specs = [_row_spec(rows, D)] + _prev_specs(rows)
        args = [x] + list(prev)
    else:
        in_specs = [_prompt_rows_spec(rows, D), _sample_rows_spec(rows, D)]
        args = list(x)
    in_specs = [_stage_a_spec(s, n) for s in in_specs + _mod_specs(rows)] + [
        _resident((1, D)), pl.BlockSpec(memory_space=pl.ANY),
        _stage_b_spec(_row_spec(rows, ROPE_TABLE_LANES))]
    args += [*mod, gain, w_in, rope]
    return pl.pallas_call(
        functools.partial(_ret_proj_kernel, has_prev, n, _n_prompt_tiles(rows), layer),
        grid=(n + 1,),
        in_specs=in_specs,
        out_specs=[_stage_a_spec(_row_spec(rows, D), n), _stage_b_spec(_row_spec(rows, RET_IN))],
        out_shape=[jax.ShapeDtypeStruct((T_ALL, D), F32), jax.ShapeDtypeStruct((T_ALL, RET_IN), BF16)],
        scratch_shapes=[pltpu.VMEM((rows, D), BF16), pltpu.VMEM((rows, D), BF16),
                        *_weight_scratch(D, RET_IN)],
        compiler_params=_params("arbitrary"),
        name="ret_proj",
    )(*args)


def _ret_core_kernel(has_s0, n_chunks, layer, *refs):
    refs = list(refs)
    p_ref = refs.pop(0)
    s0_ref = refs.pop(0) if has_s0 else None
    causal_ref, cd_ref = refs[:2]
    y_ref, so_ref, s_ref = refs[-3:]
    c = pl.program_id(1)

    @pl.when(c == 0)
    def _():
        if has_s0:
            s_ref[...] = s0_ref[0, 0]
        else:
            s_ref[...] = jnp.zeros_like(s_ref)

    for h in range(RET_HEADS):
        qb = p_ref[:, h * RET_DK:(h + 1) * RET_DK]
        kb = p_ref[:, RET_QK + h * RET_DK:RET_QK + (h + 1) * RET_DK]
        vb = p_ref[:, 2 * RET_QK + h * RET_DV:2 * RET_QK + (h + 1) * RET_DV]
        gb = p_ref[:, 2 * RET_QK + RET_V + h * RET_DV:2 * RET_QK + RET_V + (h + 1) * RET_DV]
        scores = lax.dot_general(qb, kb, (((1,), (1,)), ((), ())), preferred_element_type=F32)
        scores = scores * causal_ref[...]
        s_old = s_ref[h]
        o = (jnp.dot(scores.astype(BF16), vb, preferred_element_type=F32)
             + jnp.dot(qb, s_old.astype(BF16), preferred_element_type=F32))
        s_ref[h] = cd_ref[h][:, 0:1] * (s_old + lax.dot_general(
            kb, vb, (((0,), (0,)), ((), ())), preferred_element_type=F32))
        y_ref[:, h * RET_DV:(h + 1) * RET_DV] = (gb.astype(F32) * _rms(o)).astype(BF16)

    @pl.when(c == n_chunks - 1)
    def _():
        so_ref[0, 0] = s_ref[...]
        if layer == 0:
            for later in range(1, N_RET):
                so_ref[later, 0] = jnp.zeros_like(s_ref)


def _ret_log_gamma():
    return np.log1p(-np.exp2(-5.0 - np.arange(RET_HEADS, dtype=np.float64)))


def _ret_chunk_tables(cl):
    idx = np.arange(cl)
    causal = (idx[:, None] >= idx[None, :]).astype(np.float32)
    cd = np.broadcast_to(np.exp(_ret_log_gamma() * cl)[:, None, None], (RET_HEADS, 1, 128))
    return jnp.asarray(causal, F32), jnp.asarray(cd, F32)


def _ret_row_scales():
    c = np.concatenate([np.arange(T_PROMPT) % RET_CHUNK_PROMPT,
                        np.arange(T_SAMPLE) % RET_CHUNK_SAMPLE]).astype(np.float64)
    e = (c[:, None] + 1.0) * _ret_log_gamma()[None, :]
    return np.concatenate([np.exp(e), np.exp(-e) * RET_DK ** -0.5], axis=1)


def _ret_core_call(p, s0, states, layer, n_seq, seq_len, cl, row0):
    has_s0 = s0 is not None
    n_chunks = seq_len // cl
    rb0 = row0 // cl
    state = (RET_HEADS, RET_DK, RET_DV)
    in_specs = [pl.BlockSpec((cl, RET_IN), lambda b, c: (rb0 + b * n_chunks + c, 0))]
    args = [p]
    if has_s0:
        in_specs.append(pl.BlockSpec((1, 1) + state, lambda b, c: (layer, b, 0, 0, 0)))
        args.append(s0)
    in_specs += [_resident((cl, cl)), _resident((RET_HEADS, 1, 128))]
    args += list(_ret_chunk_tables(cl))
    if layer == 0:
        assert states is None
        state_spec = pl.BlockSpec((N_RET, 1) + state, lambda b, c: (0, b, 0, 0, 0))
        aliases = {}
    else:
        in_specs.append(pl.BlockSpec(memory_space=pl.ANY))
        args.append(states)
        state_spec = pl.BlockSpec((1, 1) + state, lambda b, c: (layer, b, 0, 0, 0))
        aliases = {len(args) - 1: 1}
    return pl.pallas_call(
        functools.partial(_ret_core_kernel, has_s0, n_chunks, layer),
        grid=(n_seq, n_chunks),
        in_specs=in_specs,
        out_specs=[pl.BlockSpec((cl, RET_V), lambda b, c: (b * n_chunks + c, 0)), state_spec],
        out_shape=[jax.ShapeDtypeStruct((n_seq * seq_len, RET_V), BF16),
                   jax.ShapeDtypeStruct((N_RET, n_seq) + state, F32)],
        scratch_shapes=[pltpu.VMEM(state, F32)],
        input_output_aliases=aliases,
        compiler_params=_params("parallel", "arbitrary"),
        name="ret_core",
    )(*args)


def _residual_router(acc, x_ref, mod, gain_ref, wr_ref, br_ref, xo_ref, h2_ref, lg_ref):
    gate1 = mod[:, 2:3, :]
    xn = x_ref[...] + _per_group(acc, lambda a3: a3 * gate1)
    xo_ref[...] = xn
    h2 = _norm_mod(xn, gain_ref, mod, 3)
    h2_ref[...] = _pack_bf16_pairs(h2)
    hh = jnp.dot(h2.astype(BF16), wr_ref[...], preferred_element_type=F32)
    lg_ref[...] = hh[:, :ROUTER_LANES] + hh[:, ROUTER_LANES:] + br_ref[...]


def _mix_out_specs(rows):
    return [_row_spec(rows, D), _row_spec(rows, DP), _row_spec(rows, ROUTER_LANES)]


_MIX_OUT_SHAPE = [
    jax.ShapeDtypeStruct((T_ALL, D), F32),
    jax.ShapeDtypeStruct((T_ALL, DP), U32),
    jax.ShapeDtypeStruct((T_ALL, ROUTER_LANES), F32),
]


def _router_specs():
    return [_resident((D, 2 * ROUTER_LANES)), _resident((1, ROUTER_LANES))]


def _ret_out_kernel(n_prompt_tiles, layer, yp_ref, ys_ref, w_hbm, x_ref, modp_ref, mods_ref, gain_ref,
                    wr_ref, br_ref, xo_ref, h2_ref, lg_ref, w_s, w_stage, w_sems):
    _load_weight_bf16(w_hbm, layer, w_s, w_stage, w_sems)
    is_prompt = pl.program_id(0) < n_prompt_tiles
    yin = jnp.where(is_prompt, yp_ref[...], ys_ref[...])
    acc = jnp.dot(yin, w_s[...], preferred_element_type=F32)
    mod = _tile_mod((modp_ref, mods_ref), is_prompt)
    _residual_router(acc, x_ref, mod, gain_ref, wr_ref, br_ref, xo_ref, h2_ref, lg_ref)


def _ret_out_call(y_prompt, y_sample, w_out, layer, x, mod, gain, w_r, b_r):
    rows = ROWS_WIDE
    return pl.pallas_call(
        functools.partial(_ret_out_kernel, _n_prompt_tiles(rows), layer),
        grid=(_n_tiles(rows),),
        in_specs=[_prompt_rows_spec(rows, RET_V), _sample_rows_spec(rows, RET_V),
                  pl.BlockSpec(memory_space=pl.ANY), _row_spec(rows, D), *_mod_specs(rows),
                  _resident((1, D))] + _router_specs(),
        out_specs=_mix_out_specs(rows),
        out_shape=_MIX_OUT_SHAPE,
        scratch_shapes=_weight_scratch(RET_V, D),
        compiler_params=_params("arbitrary"),
        name="ret_out",
    )(y_prompt, y_sample, w_out, x, *mod, gain, w_r, b_r)


GM_TN = 512


_GELU_C = float(np.sqrt(2.0 / np.pi))


def _gelu_tanh(x):
    hx = 0.5 * x
    return hx * jnp.tanh(x * (_GELU_C + (_GELU_C * 0.044715) * (x * x))) + hx


def _gm_proj_kernel(n_tiles, n_prompt_tiles, layer, *refs):
    x_ref, yg0_ref, yg1_ref, gates_ref = refs[:4]
    gain_ref, w_hbm, b_ref, lg_ref, lb_ref = refs[8:13]
    (xo_ref, uv_ref, vs_ref, hb_cur, hb_prev, vraw_cur, vraw_prev, stat_prev, sum_s,
     w_s, w_stage, w_sems) = refs[-12:]
    _load_weight_bf16(w_hbm, layer, w_s, w_stage, w_sems)
    i = pl.program_id(0)
    rows = x_ref.shape[0]

    @pl.when(i == 0)
    def _():
        hb_prev[...] = jnp.zeros_like(hb_prev)
        vraw_prev[...] = jnp.zeros_like(vraw_prev)
        stat_prev[...] = jnp.zeros_like(stat_prev)

    def add_row(x, row_ref, lo, width):
        x3 = x.reshape(rows // 8, 8, width) + row_ref[:, lo:lo + width]
        return x3.reshape(rows, width)

    def mul_row(x, row_ref, lo, width):
        x3 = x.reshape(rows // 8, 8, width) * row_ref[:, lo:lo + width]
        return x3.reshape(rows, width)

    def proj_chunk(hb_ref, lo):
        z = jnp.dot(hb_ref[...], w_s[:, lo:lo + GM_TN], preferred_element_type=F32)
        return add_row(z, b_ref, lo, GM_TN).astype(BF16)

    def stage_b_chunk(lo):
        uv_ref[:, lo:lo + GM_TN] = _gelu_tanh(proj_chunk(hb_prev, lo))
        for k in range(lo, lo + GM_TN, 128):
            vk = vraw_prev[:, k:k + 128].astype(F32) * stat_prev[:, 0:128] + stat_prev[:, 128:256]
            vn = add_row(mul_row(vk, lg_ref, k, 128), lb_ref, k, 128)
            uv_ref[:, GM_HALF + k:GM_HALF + k + 128] = vn.astype(BF16)
            vs_ref[0, :, k:k + 128] = vn

    def stage_a_chunk(n, lo):
        gz = _gelu_tanh(proj_chunk(hb_cur, GM_HALF + lo))
        vraw_cur[:, lo:lo + GM_TN] = gz
        gf = gz.astype(F32)
        pieces = [gf[:, k:k + 128] for k in range(0, GM_TN, 128)]
        t1 = functools.reduce(lambda p, q: p + q, pieces)
        t2 = functools.reduce(lambda p, q: p + q, [p * p for p in pieces])
        if n == 0:
            sum_s[:, 0:128] = t1
            sum_s[:, 128:256] = t2
        else:
            sum_s[:, 0:128] += t1
            sum_s[:, 128:256] += t2

    chunks = list(range(0, GM_HALF, GM_TN))
    stage_b_chunk(chunks[0])
    a_is_prompt = jnp.minimum(i, n_tiles - 1) < n_prompt_tiles
    x = _add_moe(x_ref[...], yg0_ref, yg1_ref, gates_ref, _tile_mod(refs[4:6], a_is_prompt))
    xo_ref[...] = x
    hb_cur[...] = _norm_mod(x, gain_ref, _tile_mod(refs[6:8], a_is_prompt), 0).astype(BF16)
    for n, lo in enumerate(chunks):
        stage_a_chunk(n, lo)
        if 0 < n < len(chunks) - 1:
            stage_b_chunk(lo)
    mu = jnp.sum(sum_s[:, 0:128], axis=-1, keepdims=True) * (1.0 / GM_HALF)
    var = jnp.sum(sum_s[:, 128:256], axis=-1, keepdims=True) * (1.0 / GM_HALF) - mu * mu
    rstd = lax.rsqrt(var + EPS)
    stage_b_chunk(chunks[-1])
    stat_prev[:, 0:128] = jnp.broadcast_to(rstd, (rows, 128))
    stat_prev[:, 128:256] = jnp.broadcast_to(-mu * rstd, (rows, 128))
    hb_prev[...] = hb_cur[...]
    vraw_prev[...] = vraw_cur[...]

    if layer == 0:
        @pl.when(i - 1 >= n_prompt_tiles)
        def _():
            for later in range(1, N_GM):
                vs_ref[later] = jnp.zeros((rows, GM_HALF), F32)


def _gm_proj_call(x, prev, mod, gain, w_in, layer, b_in, ln_g, ln_b, vs_all):
    rows = ROWS_GM_PROJ
    n, npt = _n_tiles(rows), _n_prompt_tiles(rows)

    def stage_a(spec):
        return _stage_a_spec(spec, n)

    stage_b = _stage_b_spec
    in_specs = [stage_a(s) for s in [_row_spec(rows, D)] + _prev_specs(rows) + _mod_specs(rows)] + [
        _resident((1, D)), pl.BlockSpec(memory_space=pl.ANY), _resident((8, GM_FFN)),
        _resident((8, GM_HALF)), _resident((8, GM_HALF))]
    rows8 = [jnp.broadcast_to(r.reshape(1, -1), (8, r.size)) for r in (b_in, ln_g, ln_b)]
    args = [x, *prev, *mod, gain, w_in, *rows8]
    if layer == 0:
        assert vs_all is None
        vs_spec = pl.BlockSpec((N_GM, rows, GM_HALF), lambda i: (0, jnp.maximum(i - 1 - npt, 0), 0))
        aliases = {}
    else:
        in_specs.append(pl.BlockSpec(memory_space=pl.ANY))
        args.append(vs_all)
        vs_spec = pl.BlockSpec((1, rows, GM_HALF), lambda i: (layer, jnp.maximum(i - 1 - npt, 0), 0))
        aliases = {len(args) - 1: 2}
    return pl.pallas_call(
        functools.partial(_gm_proj_kernel, n, npt, layer),
        grid=(n + 1,),
        in_specs=in_specs,
        out_specs=[stage_a(_row_spec(rows, D)), stage_b(_row_spec(rows, GM_FFN)), vs_spec],
        out_shape=[jax.ShapeDtypeStruct((T_ALL, D), F32),
                   jax.ShapeDtypeStruct((T_ALL, GM_FFN), BF16),
                   jax.ShapeDtypeStruct((N_GM, T_SAMPLE, GM_HALF), F32)],
        scratch_shapes=[pltpu.VMEM((rows, D), BF16), pltpu.VMEM((rows, D), BF16),
                        pltpu.VMEM((rows, GM_HALF), BF16), pltpu.VMEM((rows, GM_HALF), BF16),
                        pltpu.VMEM((rows, 256), F32), pltpu.VMEM((rows, 256), F32),
                        *_weight_scratch(D, GM_FFN)],
        input_output_aliases=aliases,
        compiler_params=_params("arbitrary"),
        name="gm_proj",
    )(*args)


def _gm_out_kernel(n_prompt_tiles, layer, uv_ref, ws_ref, bs_ref, w_hbm, bo_ref, x_ref, modp_ref,
                   mods_ref, gain_ref, wr_ref, br_ref, xo_ref, h2_ref, lg_ref, w_s, w_stage, w_sems):
    _load_weight_bf16(w_hbm, layer, w_s, w_stage, w_sems)
    rows = uv_ref.shape[0]
    mod = _tile_mod((modp_ref, mods_ref), pl.program_id(0) < n_prompt_tiles)
    pieces = []
    for r0 in range(0, rows, GM_MIX):
        acc = jnp.zeros((GM_MIX, D), F32)
        for g in range(GM_GROUPS):
            lo = g * GM_GDIM
            sp = jnp.dot(ws_ref[0, g], uv_ref[r0:r0 + GM_MIX, GM_HALF + lo:GM_HALF + lo + GM_GDIM],
                         preferred_element_type=F32) + bs_ref[0, g]
            gated = (uv_ref[r0:r0 + GM_MIX, lo:lo + GM_GDIM].astype(F32) * sp).astype(BF16)
            acc = acc + jnp.dot(gated, w_s[lo:lo + GM_GDIM, :], preferred_element_type=F32)
        pieces.append(acc)
    acc = jnp.concatenate(pieces, axis=0) + bo_ref[...]
    _residual_router(acc, x_ref, mod, gain_ref, wr_ref, br_ref, xo_ref, h2_ref, lg_ref)


def _gm_block_diag(w_s, b_s):
    mats, biases = [], []
    for cl in (GM_CHUNK, DEC_SEQ):
        tri = jnp.tril(jnp.ones((cl, cl), bool))
        blk = jnp.where(tri[None], w_s[:, :cl, :cl], 0.0)
        reps = GM_MIX // cl
        eye = jnp.eye(reps, dtype=w_s.dtype)
        bd = jnp.einsum("ab,gts->gatbs", eye, blk).reshape(GM_GROUPS, GM_MIX, GM_MIX)
        mats.append(bd)
        biases.append(jnp.tile(b_s[:, :cl], (1, reps))[:, :, None])
    return jnp.stack(mats).astype(BF16), jnp.stack(biases).astype(F32)


def _gm_out_call(uv, ws_bd, bs_bd, w_out, layer, b_out, x, mod, gain, w_r, b_r):
    rows = ROWS_WIDE
    npt = _n_prompt_tiles(rows)

    def variant(i):
        return jnp.where(i >= npt, 1, 0)

    return pl.pallas_call(
        functools.partial(_gm_out_kernel, npt, layer),
        grid=(_n_tiles(rows),),
        in_specs=[_row_spec(rows, GM_FFN),
                  pl.BlockSpec((1, GM_GROUPS, GM_MIX, GM_MIX), lambda i: (variant(i), 0, 0, 0)),
                  pl.BlockSpec((1, GM_GROUPS, GM_MIX, 1), lambda i: (variant(i), 0, 0, 0)),
                  pl.BlockSpec(memory_space=pl.ANY), _resident((1, D)), _row_spec(rows, D),
                  *_mod_specs(rows), _resident((1, D))] + _router_specs(),
        out_specs=_mix_out_specs(rows),
        out_shape=_MIX_OUT_SHAPE,
        scratch_shapes=_weight_scratch(GM_HALF, D),
        compiler_params=_params("arbitrary"),
        name="gm_out",
    )(uv, ws_bd, bs_bd, w_out, b_out, x, *mod, gain, w_r, b_r)


def _expert_kernel(layer, be_ref, bv_ref, nb_ref, nx_ref, xb_ref, w1_hbm, w3_hbm, w2_hbm, yb_ref,
                   st1, st3, st2, w1s, w3s, w2s, sems, slot_ref):
    b = pl.program_id(0)

    def weight_copies(e, slot):
        return (pltpu.make_async_copy(w1_hbm.at[layer, e], st1.at[slot], sems.at[slot, 0]),
                pltpu.make_async_copy(w3_hbm.at[layer, e], st3.at[slot], sems.at[slot, 1]),
                pltpu.make_async_copy(w2_hbm.at[layer, e], st2.at[slot], sems.at[slot, 2]))

    @pl.when(b == 0)
    def _():
        slot_ref[0] = 0
        for cp in weight_copies(be_ref[0], 0):
            cp.start()

    @pl.when(b < nb_ref[0])
    def _():
        prev_e = be_ref[jnp.maximum(b - 1, 0)]

        @pl.when((b == 0) | (be_ref[b] != prev_e))
        def _():
            slot = slot_ref[0]
            for cp in weight_copies(be_ref[b], slot):
                cp.wait()
            w1s[...] = st1[slot].astype(BF16)
            w3s[...] = st3[slot].astype(BF16)
            w2s[...] = st2[slot].astype(BF16)

            @pl.when(nx_ref[b] >= 0)
            def _():
                for cp in weight_copies(nx_ref[b], 1 - slot):
                    cp.start()

            slot_ref[0] = 1 - slot

        valid = bv_ref[b]

        def run_rows(n):
            row = lax.broadcasted_iota(jnp.int32, (n, 1), 0)
            xw = jnp.where(row < valid, xb_ref[0:n], U32(0))
            x = _unpack_bf16_pairs(xw).astype(BF16)
            a = jnp.dot(x, w1s[...], preferred_element_type=F32)
            c = jnp.dot(x, w3s[...], preferred_element_type=F32)
            h = (_silu(a) * c).astype(BF16)
            yb_ref[0:n] = _pack_bf16_pairs(jnp.dot(h, w2s[...], preferred_element_type=F32))
            if n < EXP_BLOCK:
                yb_ref[n:EXP_BLOCK] = jnp.zeros((EXP_BLOCK - n, DP), U32)

        for n in range(EXP_ROW_STEP, EXP_BLOCK + 1, EXP_ROW_STEP):
            pl.when((valid > n - EXP_ROW_STEP) & (valid <= n))(functools.partial(run_rows, n))


def _expert_call(blk_e, blk_valid, n_blk, blk_next, xb, w1, w3, w2, layer):
    def blk(b, be, bv, nb, nx):
        return (jnp.minimum(b, nb[0] - 1), 0)

    up, down = (D, MOE_HIDDEN), (MOE_HIDDEN, D)
    grid_spec = pltpu.PrefetchScalarGridSpec(
        num_scalar_prefetch=4,
        grid=(N_EXP_BLOCKS,),
        in_specs=[pl.BlockSpec((EXP_BLOCK, DP), blk)] + [pl.BlockSpec(memory_space=pl.ANY)] * 3,
        out_specs=pl.BlockSpec((EXP_BLOCK, DP), blk),
        scratch_shapes=[pltpu.VMEM((2,) + up, F32), pltpu.VMEM((2,) + up, F32), pltpu.VMEM((2,) + down, F32),
                        pltpu.VMEM(up, BF16), pltpu.VMEM(up, BF16), pltpu.VMEM(down, BF16),
                        pltpu.SemaphoreType.DMA((2, 3)), pltpu.SMEM((1,), jnp.int32)],
    )
    return pl.pallas_call(
        functools.partial(_expert_kernel, layer),
        grid_spec=grid_spec,
        out_shape=jax.ShapeDtypeStruct((P_ROWS, DP), U32),
        compiler_params=_params("arbitrary"),
        name="experts",
    )(blk_e, blk_valid, n_blk, blk_next, xb, w1, w3, w2)


def _route_kernel(lg_ref, dest_ref, gates_ref, meta_ref, cnt_ref):
    t = pl.program_id(0)

    @pl.when(t == 0)
    def _():
        cnt_ref[...] = jnp.zeros_like(cnt_ref)

    pl.when(t < N_ROUTE_TILES)(functools.partial(_route_tile, t, lg_ref, dest_ref, gates_ref, cnt_ref))
    pl.when(t == N_ROUTE_TILES)(functools.partial(_route_finish, dest_ref, meta_ref, cnt_ref))


ROUTE_RANK_BITS = 16
assert P_ROWS < 2 ** ROUTE_RANK_BITS


def _route_tile(t, lg_ref, dest_ref, gates_ref, cnt_ref):
    tm = ROUTE_TM
    lt = lg_ref[...].T
    el = lt[0:MOE_EXPERTS]
    gl = lt[MOE_EXPERTS:MOE_EXPERTS + 8]
    gidx = lax.broadcasted_iota(jnp.int32, (8, tm), 0)
    neg = jnp.float32(-jnp.inf)
    gl = jnp.where(gidx < MOE_GROUPS, gl, neg)
    gmax = jnp.max(gl, axis=0, keepdims=True)
    grp = jnp.min(jnp.where(gl == gmax, gidx, MOE_GROUPS), axis=0, keepdims=True)
    eidx = lax.broadcasted_iota(jnp.int32, (MOE_EXPERTS, tm), 0)
    els = jnp.where((eidx >> 3) == grp, el, neg)
    m1 = jnp.max(els, axis=0, keepdims=True)
    i1 = jnp.min(jnp.where(els == m1, eidx, MOE_EXPERTS), axis=0, keepdims=True)
    els2 = jnp.where(eidx == i1, neg, els)
    m2 = jnp.max(els2, axis=0, keepdims=True)
    i2 = jnp.min(jnp.where(els2 == m2, eidx, MOE_EXPERTS), axis=0, keepdims=True)
    sel1 = eidx == i1
    sel2 = eidx == i2
    cnt = jnp.where(sel1 | sel2, 1.0, 0.0)

    lane = ROUTER_LANES
    before = (lax.broadcasted_iota(jnp.int32, (lane, lane), 0)
              < lax.broadcasted_iota(jnp.int32, (lane, lane), 1))
    tri = jnp.where(before, 1.0, 0.0).astype(BF16)
    run = cnt_ref[...]
    r1, r2 = [], []
    for k in range(tm // lane):
        piece = slice(k * lane, (k + 1) * lane)
        ck = cnt[:, piece]
        pos = run + jnp.dot(ck.astype(BF16), tri, preferred_element_type=F32)
        r1.append(jnp.sum(jnp.where(sel1[:, piece], pos, 0.0), axis=0, keepdims=True))
        r2.append(jnp.sum(jnp.where(sel2[:, piece], pos, 0.0), axis=0, keepdims=True))
        run = run + jnp.sum(ck, axis=1, keepdims=True)
    cnt_ref[...] = run
    rank = jnp.concatenate([jnp.concatenate(r1, axis=1), jnp.concatenate(r2, axis=1)], axis=0)
    eid = jnp.concatenate([i1, i2], axis=0)
    dest_ref[:, pl.ds(pl.multiple_of(t * tm, tm), tm)] = (eid << ROUTE_RANK_BITS) + rank.astype(jnp.int32)

    g_w = 1.0 / jnp.sum(jnp.exp(gl - gmax), axis=0, keepdims=True)
    e21 = jnp.exp(m2 - m1)
    p1 = 1.0 / (1.0 + e21)
    rid = lax.broadcasted_iota(jnp.int32, (ROUTER_LANES, tm), 0)
    gt = jnp.where(rid == 0, g_w * p1, jnp.where(rid == 1, g_w * (e21 * p1), 0.0))
    gates_ref[...] = gt.T


def _route_finish(dest_ref, meta_ref, cnt_ref):
    counts = cnt_ref[...]
    nblk = jnp.floor((counts + (EXP_BLOCK - 1.0)) * (1.0 / EXP_BLOCK))
    r = lax.broadcasted_iota(jnp.int32, (MOE_EXPERTS, MOE_EXPERTS), 0)
    c = lax.broadcasted_iota(jnp.int32, (MOE_EXPERTS, MOE_EXPERTS), 1)
    nblk_row = jnp.sum(jnp.where(r == c, nblk, 0.0), axis=0, keepdims=True)
    bstart = jnp.sum(jnp.where(c < r, nblk_row, 0.0), axis=1, keepdims=True)
    bend = bstart + nblk
    bidx = lax.broadcasted_iota(jnp.int32, (1, META_LANES), 1).astype(F32)
    blk_e = jnp.minimum(jnp.sum(jnp.where(bidx >= bend, 1.0, 0.0), axis=0, keepdims=True),
                        MOE_EXPERTS - 1.0)
    erow = lax.broadcasted_iota(jnp.int32, (MOE_EXPERTS, META_LANES), 0).astype(F32)
    mine = erow == blk_e
    cnt_b = jnp.sum(jnp.where(mine, counts, 0.0), axis=0, keepdims=True)
    start_b = jnp.sum(jnp.where(mine, bstart, 0.0), axis=0, keepdims=True)
    valid = jnp.clip(cnt_b - (bidx - start_b) * EXP_BLOCK, 0.0, float(EXP_BLOCK))
    n_blk = jnp.sum(nblk, axis=0, keepdims=True)
    end_b = jnp.sum(jnp.where(mine, bend, 0.0), axis=0, keepdims=True)
    nxt = jnp.minimum(jnp.sum(jnp.where(end_b >= bend, 1.0, 0.0), axis=0, keepdims=True),
                      MOE_EXPERTS - 1.0)
    nxt = jnp.where(end_b < n_blk, nxt, -1.0)
    mrow = lax.broadcasted_iota(jnp.int32, (8, META_LANES), 0)
    meta = jnp.where(mrow == 0, blk_e, jnp.where(mrow == 1, valid, jnp.where(
        mrow == 2, n_blk, jnp.where(mrow == 3, nxt, 0.0))))
    meta_ref[...] = meta.astype(jnp.int32)

    base = (bstart * EXP_BLOCK).astype(jnp.int32)
    packed = dest_ref[...]
    eid = packed >> ROUTE_RANK_BITS
    row = packed & (2 ** ROUTE_RANK_BITS - 1)
    for e in range(MOE_EXPERTS):
        row = row + jnp.where(eid == e, base[e:e + 1, :], 0)
    dest_ref[...] = row


def _route_call(logits):
    last = N_ROUTE_TILES - 1
    return pl.pallas_call(
        _route_kernel,
        grid=(N_ROUTE_TILES + 1,),
        in_specs=[pl.BlockSpec((ROUTE_TM, ROUTER_LANES), lambda t: (jnp.minimum(t, last), 0))],
        out_specs=[pl.BlockSpec((MOE_TOPK, T_ALL), lambda t: (0, 0)),
                   pl.BlockSpec((ROUTE_TM, ROUTER_LANES), lambda t: (jnp.minimum(t, last), 0)),
                   pl.BlockSpec((8, META_LANES), lambda t: (0, 0))],
        out_shape=[jax.ShapeDtypeStruct((MOE_TOPK, T_ALL), jnp.int32),
                   jax.ShapeDtypeStruct((T_ALL, ROUTER_LANES), F32),
                   jax.ShapeDtypeStruct((8, META_LANES), jnp.int32)],
        scratch_shapes=[pltpu.VMEM((MOE_EXPERTS, 1), F32)],
        compiler_params=_params("arbitrary"),
        name="route",
    )(logits)


def _sc_mesh():
    return plsc.VectorSubcoreMesh(core_axis_name="c", subcore_axis_name="s")


def _sc_token_offset(j):
    wid = lax.axis_index("s") * SC_CORES + lax.axis_index("c")
    return pl.multiple_of(wid * SC_ROWS_PER_WORKER + j * SC_CHUNK, 8)


SC_N_CHUNKS = SC_ROWS_PER_WORKER // SC_CHUNK
assert SC_N_CHUNKS % 2 == 1


def _start(copies):
    for cp in copies:
        cp.start()


def _wait(copies):
    for cp in copies:
        cp.wait()


def _dispatch_body(h_hbm, d0_hbm, d1_hbm, out_hbm, i0a, i1a, rows_a, i0b, i1b, rows_b, la, lb, sa, sb):
    sets = {"a": (i0a, i1a, rows_a, la, sa), "b": (i0b, i1b, rows_b, lb, sb)}

    def loads(j, s):
        i0, i1, rows, lsem, _ = sets[s]
        src = pl.ds(_sc_token_offset(j), SC_CHUNK)
        return (pltpu.make_async_copy(d0_hbm.at[src], i0, lsem),
                pltpu.make_async_copy(d1_hbm.at[src], i1, lsem),
                pltpu.make_async_copy(h_hbm.at[src], rows, lsem))

    def scatters(s):
        i0, i1, rows, _, ssem = sets[s]
        return (pltpu.make_async_copy(rows, out_hbm.at[i0], ssem),
                pltpu.make_async_copy(rows, out_hbm.at[i1], ssem))

    _start(loads(0, "a"))

    @pl.loop(0, SC_N_CHUNKS // 2)
    def _(p):
        a = 2 * p
        _start(loads(a + 1, "b"))
        _wait(loads(a, "a"))
        _start(scatters("a"))
        _wait(loads(a + 1, "b"))
        _start(scatters("b"))
        _wait(scatters("a"))
        _start(loads(a + 2, "a"))
        _wait(scatters("b"))

    _wait(loads(SC_N_CHUNKS - 1, "a"))
    _start(scatters("a"))
    _wait(scatters("a"))


def _dispatch_call(h2, dest0, dest1):
    buffers = [pltpu.VMEM((SC_CHUNK,), jnp.int32), pltpu.VMEM((SC_CHUNK,), jnp.int32),
               pltpu.VMEM((SC_CHUNK, DP), U32)]
    return pl.kernel(
        _dispatch_body,
        out_type=jax.ShapeDtypeStruct((P_ROWS, DP), U32),
        mesh=_sc_mesh(),
        scratch_types=buffers + buffers + [pltpu.SemaphoreType.DMA] * 4,
        name="moe_dispatch",
    )(h2, dest0, dest1)


def _combine_body(yb_hbm, d0_hbm, d1_hbm, o0_hbm, o1_hbm, i0a, i1a, r0a, r1a, i0b, i1b, r0b, r1b,
                  ia, ib, ga, gb, wa, wb):
    sets = {"a": (i0a, i1a, r0a, r1a, ia, ga, wa), "b": (i0b, i1b, r0b, r1b, ib, gb, wb)}

    def index_loads(j, s):
        i0, i1, _, _, isem, _, _ = sets[s]
        src = pl.ds(_sc_token_offset(j), SC_CHUNK)
        return (pltpu.make_async_copy(d0_hbm.at[src], i0, isem), pltpu.make_async_copy(d1_hbm.at[src], i1, isem))

    def gathers(s):
        i0, i1, r0, r1, _, gsem, _ = sets[s]
        return (pltpu.make_async_copy(yb_hbm.at[i0], r0, gsem), pltpu.make_async_copy(yb_hbm.at[i1], r1, gsem))

    def writes(j, s):
        _, _, r0, r1, _, _, wsem = sets[s]
        dst = pl.ds(_sc_token_offset(j), SC_CHUNK)
        return (pltpu.make_async_copy(r0, o0_hbm.at[dst], wsem), pltpu.make_async_copy(r1, o1_hbm.at[dst], wsem))

    _start(index_loads(0, "a"))
    _wait(index_loads(0, "a"))
    _start(gathers("a"))

    @pl.loop(0, SC_N_CHUNKS // 2)
    def _(p):
        a = 2 * p
        _start(index_loads(a + 1, "b"))
        _wait(index_loads(a + 1, "b"))
        _wait(gathers("a"))
        _start(writes(a, "a"))
        _start(gathers("b"))
        _start(index_loads(a + 2, "a"))
        _wait(index_loads(a + 2, "a"))
        _wait(writes(a, "a"))
        _wait(gathers("b"))
        _start(writes(a + 1, "b"))
        _start(gathers("a"))
        _wait(writes(a + 1, "b"))

    _wait(gathers("a"))
    _start(writes(SC_N_CHUNKS - 1, "a"))
    _wait(writes(SC_N_CHUNKS - 1, "a"))


def _combine_call(yb, dest0, dest1):
    out = jax.ShapeDtypeStruct((T_ALL, DP), U32)
    buffers = [pltpu.VMEM((SC_CHUNK,), jnp.int32), pltpu.VMEM((SC_CHUNK,), jnp.int32),
               pltpu.VMEM((SC_CHUNK, DP), U32), pltpu.VMEM((SC_CHUNK, DP), U32)]
    return pl.kernel(
        _combine_body,
        out_type=(out, out),
        mesh=_sc_mesh(),
        scratch_types=buffers + buffers + [pltpu.SemaphoreType.DMA] * 6,
        name="moe_combine",
    )(yb, dest0, dest1)


def _moe_rows(h2, dest, meta, w1, w3, w2, layer):
    dest0, dest1 = dest[0], dest[1]
    xb = _dispatch_call(h2, dest0, dest1)
    yb = _expert_call(meta[0, :N_EXP_BLOCKS], meta[1, :N_EXP_BLOCKS], meta[2, :1], meta[3, :N_EXP_BLOCKS],
                      xb, w1, w3, w2, layer)
    return _combine_call(yb, dest0, dest1)


def _final_kernel(is_prompt, x_ref, yg0_ref, yg1_ref, gates_ref, modp_ref, mods_ref, gain_ref, o_ref):
    x = _add_moe(x_ref[...], yg0_ref, yg1_ref, gates_ref, _tile_mod((modp_ref, mods_ref), is_prompt))
    o_ref[...] = _rms(x) * gain_ref[...]


def _final_call(x, prev, gain, row0, n_rows):
    rows = ROWS_WIDE
    tile0 = row0 // rows
    is_prompt = row0 < T_PROMPT
    assert row0 + n_rows <= T_PROMPT or not is_prompt

    def tile(width):
        return pl.BlockSpec((rows, width), lambda i: (tile0 + i, 0))

    mod_specs = [pl.BlockSpec(s.block_shape, lambda i, m=s.index_map: m(tile0 + i)) for s in _mod_specs(rows)]
    return pl.pallas_call(
        functools.partial(_final_kernel, is_prompt),
        grid=(n_rows // rows,),
        in_specs=[tile(D), tile(DP), tile(DP), tile(ROUTER_LANES), *mod_specs, _resident((1, D))],
        out_specs=pl.BlockSpec((rows, D), lambda i: (i, 0)),
        out_shape=jax.ShapeDtypeStruct((n_rows, D), F32),
        compiler_params=_params("parallel"),
        name="final_norm",
    )(x, *prev, gain)


def _rope_table():
    pos = np.concatenate([np.tile(np.arange(SEQ), BATCH),
                          np.tile(PAST_LEN + np.arange(DEC_SEQ), DEC_BATCH)]).astype(np.float32)
    inv = (ROPE_BASE ** (-np.arange(ROPE_HALF, dtype=np.float32) / ROPE_HALF)).astype(np.float32)
    ang = (pos[:, None] * inv[None, :]).astype(np.float32).astype(np.float64)
    pad = np.zeros((T_ALL, ROPE_TABLE_LANES - 2 * ROPE_HALF - 2 * RET_HEADS))
    return jnp.asarray(np.concatenate([np.cos(ang), np.sin(ang), _ret_row_scales(), pad], axis=1), F32)


def kernel(x_prompt, x_sample, c_prompt, c_sample, state_ret, ada_w, ada_b, norm1_g, norm2_g, ret_w_in,
           ret_w_out, gm_w_in, gm_b_in, gm_ln_g, gm_ln_b, gm_w_s, gm_b_s, gm_w_out, gm_b_out, moe_w_rg,
           moe_b_rg, moe_w_re, moe_b_re, moe_w1, moe_w3, moe_w2, final_g):
    x = (x_prompt.reshape(T_PROMPT, D), x_sample.reshape(T_SAMPLE, D))
    c_all = jnp.concatenate([c_prompt, c_sample], axis=0)
    rope = _rope_table()

    mod_all = _ada_call(c_all, ada_w, ada_b).reshape(DEPTH, N_SEQ, 6, D)

    def layer_params(i):
        mod = (mod_all[i, :BATCH], mod_all[i, BATCH:])
        w_r = jnp.pad(jnp.concatenate([moe_w_re[i], moe_w_rg[i]], axis=1),
                      ((0, 0), (0, ROUTER_LANES - MOE_GROUPS - MOE_EXPERTS)))
        w_r_hi = w_r.astype(BF16)
        w_r_lo = (w_r - w_r_hi.astype(F32)).astype(BF16)
        w_r = jnp.concatenate([w_r_hi, w_r_lo], axis=1)
        b_r = jnp.pad(jnp.concatenate([moe_b_re[i].reshape(-1), moe_b_rg[i]]),
                      (0, ROUTER_LANES - MOE_GROUPS - MOE_EXPERTS)).reshape(1, ROUTER_LANES)
        return mod, w_r, b_r

    ret_prompt = ret_sample = gm_sample = None
    prev = None
    for i in range(DEPTH):
        j = i // 2
        mod, w_r, b_r = layer_params(i)
        g1 = norm1_g[i].reshape(1, D)
        g2 = norm2_g[i].reshape(1, D)
        if i % 2 == 0:
            x, p = _ret_proj_call(x, prev, mod, g1, ret_w_in, j, rope)
            y_p, ret_prompt = _ret_core_call(p, None, ret_prompt, j, BATCH, SEQ, RET_CHUNK_PROMPT, 0)
            y_s, ret_sample = _ret_core_call(p, state_ret, ret_sample, j, DEC_BATCH, DEC_SEQ,
                                             RET_CHUNK_SAMPLE, T_PROMPT)
            x, h2, logits = _ret_out_call(y_p, y_s, ret_w_out, j, x, mod, g2, w_r, b_r)
        else:
            x, uv, gm_sample = _gm_proj_call(x, prev, mod, g1, gm_w_in, j, gm_b_in[j], gm_ln_g[j],
                                             gm_ln_b[j], gm_sample)
            ws_bd, bs_bd = _gm_block_diag(gm_w_s[j], gm_b_s[j])
            x, h2, logits = _gm_out_call(uv, ws_bd, bs_bd, gm_w_out, j,
                                         gm_b_out[j].reshape(1, D), x, mod, g2, w_r, b_r)
        dest, gates, meta = _route_call(logits)
        yg0, yg1 = _moe_rows(h2, dest, meta, moe_w1, moe_w3, moe_w2, i)
        prev = (yg0, yg1, gates, *mod)

    fg = final_g.reshape(1, D)
    y_prompt = _final_call(x, prev, fg, 0, T_PROMPT).reshape(BATCH, SEQ, D)
    y_sample = _final_call(x, prev, fg, T_PROMPT, T_SAMPLE).reshape(DEC_BATCH, DEC_SEQ, D)
    return (y_prompt, y_sample, ret_prompt, ret_sample,
            gm_sample.reshape(N_GM, DEC_BATCH, DEC_SEQ, GM_HALF))
```

```python
import functools

import numpy as np
import jax
import jax.numpy as jnp
from jax import lax
from jax.experimental import pallas as pl
from jax.experimental.pallas import tpu as pltpu
from jax.experimental.pallas import tpu_sc as plsc

F32 = jnp.float32
BF16 = jnp.bfloat16
U32 = jnp.uint32

D = 1024
BATCH, SEQ = 4, 4096
DEC_BATCH, DEC_SEQ = 16, 64
PAST_LEN = 4096
DEPTH = 4
N_RET = (DEPTH + 1) // 2
N_GM = DEPTH // 2
N_SEQ = BATCH + DEC_BATCH

RET_HEADS, RET_DK, RET_DV = 4, 256, 512
RET_QK = RET_HEADS * RET_DK
RET_V = RET_HEADS * RET_DV
RET_IN = 2 * RET_QK + 2 * RET_V
ROPE_BASE = 10000.0
ROPE_HALF = RET_DK // 2
ROPE_TABLE_LANES = 3 * ROPE_HALF

GM_FFN = 6 * D
GM_HALF = GM_FFN // 2
GM_GROUPS = 4
GM_GDIM = GM_HALF // GM_GROUPS
GM_CHUNK = 128

MOE_GROUPS, MOE_PER_GROUP = 4, 8
MOE_EXPERTS = MOE_GROUPS * MOE_PER_GROUP
MOE_TOPK = 2
MOE_HIDDEN = 512
EPS = 1e-6

GROUP = DEC_SEQ
T_PROMPT = BATCH * SEQ
T_SAMPLE = DEC_BATCH * DEC_SEQ
T_ALL = T_PROMPT + T_SAMPLE
N_GROUPS = T_ALL // GROUP
ROWS_WIDE = 512
ROWS_GM_PROJ = 512

RET_CHUNK_PROMPT = 256
RET_CHUNK_SAMPLE = DEC_SEQ

GM_MIX = 256

EXP_BLOCK = 512
EXP_ROW_STEP = 128
N_ASSIGN = T_ALL * MOE_TOPK
N_EXP_BLOCKS = -(-(N_ASSIGN + MOE_EXPERTS * (EXP_BLOCK - 1)) // EXP_BLOCK)
P_ROWS = N_EXP_BLOCKS * EXP_BLOCK
ROUTER_LANES = 128
ROUTE_TM = 1024
N_ROUTE_TILES = T_ALL // ROUTE_TM
META_LANES = 256
assert META_LANES >= N_EXP_BLOCKS

DP = D // 2
SC_CORES, SC_SUBCORES = 2, 16
SC_WORKERS = SC_CORES * SC_SUBCORES
SC_ROWS_PER_WORKER = T_ALL // SC_WORKERS
SC_CHUNK = 32
assert SC_ROWS_PER_WORKER % SC_CHUNK == 0 and SC_CHUNK % 8 == 0

V7X_VMEM_LIMIT_BYTES = 56 * 1024 * 1024


def _params(*sem):
    return pltpu.CompilerParams(dimension_semantics=sem, vmem_limit_bytes=V7X_VMEM_LIMIT_BYTES)


def _resident(shape):
    nd = len(shape)
    return pl.BlockSpec(shape, lambda *_: (0,) * nd, pipeline_mode=pl.Buffered(1))


WEIGHT_STAGE_BYTES = 3 * 512 * 1024


def _weight_scratch(k, n):
    chunk = k
    while chunk * n * 4 > WEIGHT_STAGE_BYTES:
        assert chunk % 16 == 0
        chunk //= 2
    return [pltpu.VMEM((k, n), BF16), pltpu.VMEM((2, chunk, n), F32), pltpu.SemaphoreType.DMA((2,))]


def _load_weight_bf16(w_hbm, layer, w_s, stage, sems):
    k = w_s.shape[0]
    chunk = stage.shape[1]

    def copy(c):
        return pltpu.make_async_copy(w_hbm.at[layer, pl.ds(c * chunk, chunk)], stage.at[c % 2], sems.at[c % 2])

    @pl.when(pl.program_id(0) == 0)
    def _():
        copy(0).start()
        for c in range(k // chunk):
            if c + 1 < k // chunk:
                copy(c + 1).start()
            copy(c).wait()
            w_s[c * chunk:(c + 1) * chunk, :] = stage[c % 2].astype(BF16)


def _rms(x):
    return x * lax.rsqrt(jnp.mean(x * x, axis=-1, keepdims=True) + EPS)


def _silu(x):
    return x * jax.nn.sigmoid(x)


def _per_group(x2d, fn):
    rows = x2d.shape[0]
    return fn(x2d.reshape(rows // GROUP, GROUP, D)).reshape(rows, D)


def _tile_mod(mod_refs, is_prompt):
    modp_ref, mods_ref = mod_refs
    return jnp.where(is_prompt, jnp.broadcast_to(modp_ref[...], mods_ref.shape), mods_ref[...])


def _norm_mod(x, gain_ref, mod, shift_idx):
    y = _rms(x) * gain_ref[...]
    scale = mod[:, shift_idx + 1:shift_idx + 2, :]
    shift = mod[:, shift_idx:shift_idx + 1, :]
    return _per_group(y, lambda y3: y3 * (1.0 + scale) + shift)


def _pack_bf16_pairs(x):
    lo = lax.bitcast_convert_type(x[:, :DP].astype(BF16).astype(F32), U32)
    hi = lax.bitcast_convert_type(x[:, DP:].astype(BF16).astype(F32), U32)
    return (lo >> 16) | (hi & U32(0xFFFF0000))


def _unpack_bf16_pairs(w):
    lo = lax.bitcast_convert_type(w << 16, F32)
    hi = lax.bitcast_convert_type(w & U32(0xFFFF0000), F32)
    return jnp.concatenate([lo, hi], axis=1)


def _add_moe(x, yg0_ref, yg1_ref, gates_ref, mod_prev):
    g = gates_ref[...]
    y = g[:, 0:1] * _unpack_bf16_pairs(yg0_ref[...]) + g[:, 1:2] * _unpack_bf16_pairs(yg1_ref[...])
    gate2 = mod_prev[:, 5:6, :]
    return x + _per_group(y, lambda y3: y3 * gate2)


ADA_TN = 1536


def _ada_kernel(c_ref, w_ref, b_ref, o_ref):
    c = c_ref[...]
    s = _silu(c).astype(BF16)
    o_ref[0] = jnp.dot(s, w_ref[0].astype(BF16), preferred_element_type=F32) + b_ref[0]


def _ada_call(c_all, ada_w, ada_b):
    return pl.pallas_call(
        _ada_kernel,
        grid=(DEPTH, 6 * D // ADA_TN),
        in_specs=[
            pl.BlockSpec((N_SEQ, D), lambda i, j: (0, 0)),
            pl.BlockSpec((1, D, ADA_TN), lambda i, j: (i, 0, j)),
            pl.BlockSpec((1, 1, ADA_TN), lambda i, j: (i, 0, j)),
        ],
        out_specs=pl.BlockSpec((1, N_SEQ, ADA_TN), lambda i, j: (i, 0, j)),
        out_shape=jax.ShapeDtypeStruct((DEPTH, N_SEQ, 6 * D), F32),
        compiler_params=_params("parallel", "parallel"),
        name="ada_modulation",
    )(c_all, ada_w, ada_b.reshape(DEPTH, 1, 6 * D))


def _n_tiles(rows):
    return T_ALL // rows


def _n_prompt_tiles(rows):
    return T_PROMPT // rows


def _row_spec(rows, width):
    return pl.BlockSpec((rows, width), lambda i: (i, 0))


def _mod_specs(rows):
    npt = _n_prompt_tiles(rows)
    return [pl.BlockSpec((1, 6, D), lambda i: (jnp.minimum(i * rows // SEQ, BATCH - 1), 0, 0)),
            pl.BlockSpec((rows // DEC_SEQ, 6, D), lambda i: (jnp.maximum(i - npt, 0), 0, 0))]


def _prev_specs(rows):
    return [_row_spec(rows, DP), _row_spec(rows, DP), _row_spec(rows, ROUTER_LANES)] + _mod_specs(rows)


def _stage_a_spec(spec, n_tiles):
    return pl.BlockSpec(spec.block_shape, lambda i, m=spec.index_map: m(jnp.minimum(i, n_tiles - 1)))


def _stage_b_spec(spec):
    return pl.BlockSpec(spec.block_shape, lambda i, m=spec.index_map: m(jnp.maximum(i - 1, 0)))


def _prompt_rows_spec(rows, width):
    last = _n_prompt_tiles(rows) - 1
    return pl.BlockSpec((rows, width), lambda i: (jnp.minimum(i, last), 0))


def _sample_rows_spec(rows, width):
    npt = _n_prompt_tiles(rows)
    return pl.BlockSpec((rows, width), lambda i: (jnp.maximum(i - npt, 0), 0))


def _ret_proj_kernel(has_prev, n_tiles, n_prompt_tiles, layer, *refs):
    i = pl.program_id(0)
    is_prompt = jnp.minimum(i, n_tiles - 1) < n_prompt_tiles
    head = refs[:6] if has_prev else refs[:2]
    refs = refs[len(head):]
    mod_refs = refs[:2]
    gain_ref, w_hbm, rope_ref, xo_ref, p_ref, hb_cur, hb_prev, w_s, w_stage, w_sems = refs[2:]
    _load_weight_bf16(w_hbm, layer, w_s, w_stage, w_sems)

    @pl.when(i == 0)
    def _():
        hb_prev[...] = jnp.zeros_like(hb_prev)

    def qk_head(j):
        lo = j * RET_DK
        acc = jnp.dot(hb_prev[...], w_s[:, lo:lo + RET_DK], preferred_element_type=F32)
        x1 = acc[:, :ROPE_HALF]
        x2 = acc[:, ROPE_HALF:]
        cos = rope_ref[:, 0:ROPE_HALF]
        sin = rope_ref[:, ROPE_HALF:2 * ROPE_HALF]
        scale = rope_ref[:, 2 * ROPE_HALF + j:2 * ROPE_HALF + j + 1]
        p_ref[:, lo:lo + ROPE_HALF] = ((x1 * cos - x2 * sin) * scale).astype(BF16)
        p_ref[:, lo + ROPE_HALF:lo + RET_DK] = ((x1 * sin + x2 * cos) * scale).astype(BF16)

    def vg_head(j):
        lo = 2 * RET_QK + j * RET_DV
        acc = jnp.dot(hb_prev[...], w_s[:, lo:lo + RET_DV], preferred_element_type=F32)
        if j >= RET_HEADS:
            acc = _silu(acc)
        p_ref[:, lo:lo + RET_DV] = acc.astype(BF16)

    vg_head(0)
    vg_head(1)
    if has_prev:
        x_ref, yg0_ref, yg1_ref, gates_ref = head[:4]
        x = _add_moe(x_ref[...], yg0_ref, yg1_ref, gates_ref, _tile_mod(head[4:6], is_prompt))
    else:
        x = jnp.where(is_prompt, head[0][...], head[1][...])
    xo_ref[...] = x
    hb_cur[...] = _norm_mod(x, gain_ref, _tile_mod(mod_refs, is_prompt), 0).astype(BF16)
    for j in range(2, 2 * RET_HEADS):
        vg_head(j)
    for j in range(2 * RET_HEADS):
        qk_head(j)
    hb_prev[...] = hb_cur[...]


def _ret_proj_call(x, prev, mod, gain, w_in, layer, rope):
    rows = ROWS_WIDE
    n = _n_tiles(rows)
    has_prev = prev is not None
    if has_prev:
        in_specs = [_row_spec(rows, D)] + _prev_specs(rows)
        args = [x] + list(prev)
    else:
        in_specs = [_prompt_rows_spec(rows, D), _sample_rows_spec(rows, D)]
        args = list(x)
    in_specs = [_stage_a_spec(s, n) for s in in_specs + _mod_specs(rows)] + [
        _resident((1, D)), pl.BlockSpec(memory_space=pl.ANY),
        _stage_b_spec(_row_spec(rows, ROPE_TABLE_LANES))]
    args += [*mod, gain, w_in, rope]
    return pl.pallas_call(
        functools.partial(_ret_proj_kernel, has_prev, n, _n_prompt_tiles(rows), layer),
        grid=(n + 1,),
        in_specs=in_specs,
        out_specs=[_stage_a_spec(_row_spec(rows, D), n), _stage_b_spec(_row_spec(rows, RET_IN))],
        out_shape=[jax.ShapeDtypeStruct((T_ALL, D), F32), jax.ShapeDtypeStruct((T_ALL, RET_IN), BF16)],
        scratch_shapes=[pltpu.VMEM((rows, D), BF16), pltpu.VMEM((rows, D), BF16),
                        *_weight_scratch(D, RET_IN)],
        compiler_params=_params("arbitrary"),
        name="ret_proj",
    )(*args)


def _ret_core_kernel(has_s0, n_chunks, layer, *refs):
    refs = list(refs)
    p_ref = refs.pop(0)
    s0_ref = refs.pop(0) if has_s0 else None
    causal_ref, cd_ref = refs[:2]
    y_ref, so_ref, s_ref = refs[-3:]
    c = pl.program_id(1)

    @pl.when(c == 0)
    def _():
        if has_s0:
            s_ref[...] = s0_ref[0, 0]
        else:
            s_ref[...] = jnp.zeros_like(s_ref)

    for h in range(RET_HEADS):
        qb = p_ref[:, h * RET_DK:(h + 1) * RET_DK]
        kb = p_ref[:, RET_QK + h * RET_DK:RET_QK + (h + 1) * RET_DK]
        vb = p_ref[:, 2 * RET_QK + h * RET_DV:2 * RET_QK + (h + 1) * RET_DV]
        gb = p_ref[:, 2 * RET_QK + RET_V + h * RET_DV:2 * RET_QK + RET_V + (h + 1) * RET_DV]
        scores = lax.dot_general(qb, kb, (((1,), (1,)), ((), ())), preferred_element_type=F32)
        scores = scores * causal_ref[...]
        s_old = s_ref[h]
        o = (jnp.dot(scores.astype(BF16), vb, preferred_element_type=F32)
             + jnp.dot(qb, s_old.astype(BF16), preferred_element_type=F32))
        s_ref[h] = cd_ref[h][:, 0:1] * (s_old + lax.dot_general(
            kb, vb, (((0,), (0,)), ((), ())), preferred_element_type=F32))
        y_ref[:, h * RET_DV:(h + 1) * RET_DV] = (gb.astype(F32) * _rms(o)).astype(BF16)

    @pl.when(c == n_chunks - 1)
    def _():
        so_ref[0, 0] = s_ref[...]
        if layer == 0:
            for later in range(1, N_RET):
                so_ref[later, 0] = jnp.zeros_like(s_ref)


def _ret_log_gamma():
    return np.log1p(-np.exp2(-5.0 - np.arange(RET_HEADS, dtype=np.float64)))


def _ret_chunk_tables(cl):
    idx = np.arange(cl)
    causal = (idx[:, None] >= idx[None, :]).astype(np.float32)
    cd = np.broadcast_to(np.exp(_ret_log_gamma() * cl)[:, None, None], (RET_HEADS, 1, 128))
    return jnp.asarray(causal, F32), jnp.asarray(cd, F32)


def _ret_row_scales():
    c = np.concatenate([np.arange(T_PROMPT) % RET_CHUNK_PROMPT,
                        np.arange(T_SAMPLE) % RET_CHUNK_SAMPLE]).astype(np.float64)
    e = (c[:, None] + 1.0) * _ret_log_gamma()[None, :]
    return np.concatenate([np.exp(e), np.exp(-e) * RET_DK ** -0.5], axis=1)


def _ret_core_call(p, s0, states, layer, n_seq, seq_len, cl, row0):
    has_s0 = s0 is not None
    n_chunks = seq_len // cl
    rb0 = row0 // cl
    state = (RET_HEADS, RET_DK, RET_DV)
    in_specs = [pl.BlockSpec((cl, RET_IN), lambda b, c: (rb0 + b * n_chunks + c, 0))]
    args = [p]
    if has_s0:
        in_specs.append(pl.BlockSpec((1, 1) + state, lambda b, c: (layer, b, 0, 0, 0)))
        args.append(s0)
    in_specs += [_resident((cl, cl)), _resident((RET_HEADS, 1, 128))]
    args += list(_ret_chunk_tables(cl))
    if layer == 0:
        assert states is None
        state_spec = pl.BlockSpec((N_RET, 1) + state, lambda b, c: (0, b, 0, 0, 0))
        aliases = {}
    else:
        in_specs.append(pl.BlockSpec(memory_space=pl.ANY))
        args.append(states)
        state_spec = pl.BlockSpec((1, 1) + state, lambda b, c: (layer, b, 0, 0, 0))
        aliases = {len(args) - 1: 1}
    return pl.pallas_call(
        functools.partial(_ret_core_kernel, has_s0, n_chunks, layer),
        grid=(n_seq, n_chunks),
        in_specs=in_specs,
        out_specs=[pl.BlockSpec((cl, RET_V), lambda b, c: (b * n_chunks + c, 0)), state_spec],
        out_shape=[jax.ShapeDtypeStruct((n_seq * seq_len, RET_V), BF16),
                   jax.ShapeDtypeStruct((N_RET, n_seq) + state, F32)],
        scratch_shapes=[pltpu.VMEM(state, F32)],
        input_output_aliases=aliases,
        compiler_params=_params("parallel", "arbitrary"),
        name="ret_core",
    )(*args)


def _residual_router(acc, x_ref, mod, gain_ref, wr_ref, br_ref, xo_ref, h2_ref, lg_ref):
    gate1 = mod[:, 2:3, :]
    xn = x_ref[...] + _per_group(acc, lambda a3: a3 * gate1)
    xo_ref[...] = xn
    h2 = _norm_mod(xn, gain_ref, mod, 3)
    h2_ref[...] = _pack_bf16_pairs(h2)
    hh = jnp.dot(h2.astype(BF16), wr_ref[...], preferred_element_type=F32)
    lg_ref[...] = hh[:, :ROUTER_LANES] + hh[:, ROUTER_LANES:] + br_ref[...]


def _mix_out_specs(rows):
    return [_row_spec(rows, D), _row_spec(rows, DP), _row_spec(rows, ROUTER_LANES)]


_MIX_OUT_SHAPE = [
    jax.ShapeDtypeStruct((T_ALL, D), F32),
    jax.ShapeDtypeStruct((T_ALL, DP), U32),
    jax.ShapeDtypeStruct((T_ALL, ROUTER_LANES), F32),
]


def _router_specs():
    return [_resident((D, 2 * ROUTER_LANES)), _resident((1, ROUTER_LANES))]


def _ret_out_kernel(n_prompt_tiles, layer, yp_ref, ys_ref, w_hbm, x_ref, modp_ref, mods_ref, gain_ref,
                    wr_ref, br_ref, xo_ref, h2_ref, lg_ref, w_s, w_stage, w_sems):
    _load_weight_bf16(w_hbm, layer, w_s, w_stage, w_sems)
    is_prompt = pl.program_id(0) < n_prompt_tiles
    yin = jnp.where(is_prompt, yp_ref[...], ys_ref[...])
    acc = jnp.dot(yin, w_s[...], preferred_element_type=F32)
    mod = _tile_mod((modp_ref, mods_ref), is_prompt)
    _residual_router(acc, x_ref, mod, gain_ref, wr_ref, br_ref, xo_ref, h2_ref, lg_ref)


def _ret_out_call(y_prompt, y_sample, w_out, layer, x, mod, gain, w_r, b_r):
    rows = ROWS_WIDE
    return pl.pallas_call(
        functools.partial(_ret_out_kernel, _n_prompt_tiles(rows), layer),
        grid=(_n_tiles(rows),),
        in_specs=[_prompt_rows_spec(rows, RET_V), _sample_rows_spec(rows, RET_V),
                  pl.BlockSpec(memory_space=pl.ANY), _row_spec(rows, D), *_mod_specs(rows),
                  _resident((1, D))] + _router_specs(),
        out_specs=_mix_out_specs(rows),
        out_shape=_MIX_OUT_SHAPE,
        scratch_shapes=_weight_scratch(RET_V, D),
        compiler_params=_params("arbitrary"),
        name="ret_out",
    )(y_prompt, y_sample, w_out, x, *mod, gain, w_r, b_r)


GM_TN = 512


_GELU_C = float(np.sqrt(2.0 / np.pi))


def _gelu_tanh(x):
    hx = 0.5 * x
    return hx * jnp.tanh(x * (_GELU_C + (_GELU_C * 0.044715) * (x * x))) + hx


def _gm_proj_kernel(n_tiles, n_prompt_tiles, layer, *refs):
    x_ref, yg0_ref, yg1_ref, gates_ref = refs[:4]
    gain_ref, w_hbm, b_ref, lg_ref, lb_ref = refs[8:13]
    (xo_ref, uv_ref, vs_hbm, hb_cur, hb_prev, vraw_cur, vraw_prev, stat_prev, sum_s,
     vs_stage, vs_sems, w_s, w_stage, w_sems) = refs[-14:]
    _load_weight_bf16(w_hbm, layer, w_s, w_stage, w_sems)
    i = pl.program_id(0)
    rows = x_ref.shape[0]

    @pl.when(i == 0)
    def _():
        hb_prev[...] = jnp.zeros_like(hb_prev)
        vraw_prev[...] = jnp.zeros_like(vraw_prev)
        stat_prev[...] = jnp.zeros_like(stat_prev)

    def add_row(x, row_ref, lo, width):
        x3 = x.reshape(rows // 8, 8, width) + row_ref[:, lo:lo + width]
        return x3.reshape(rows, width)

    def mul_row(x, row_ref, lo, width):
        x3 = x.reshape(rows // 8, 8, width) * row_ref[:, lo:lo + width]
        return x3.reshape(rows, width)

    def proj_chunk(hb_ref, lo):
        z = jnp.dot(hb_ref[...], w_s[:, lo:lo + GM_TN], preferred_element_type=F32)
        return add_row(z, b_ref, lo, GM_TN).astype(BF16)

    def v_norm_piece(k):
        vk = vraw_prev[:, k:k + 128].astype(F32) * stat_prev[:, 0:128] + stat_prev[:, 128:256]
        return add_row(mul_row(vk, lg_ref, k, 128), lb_ref, k, 128)

    def stage_b_chunk(lo):
        uv_ref[:, lo:lo + GM_TN] = _gelu_tanh(proj_chunk(hb_prev, lo))
        for k in range(lo, lo + GM_TN, 128):
            uv_ref[:, GM_HALF + k:GM_HALF + k + 128] = v_norm_piece(k).astype(BF16)

    def stage_a_chunk(n, lo):
        gz = _gelu_tanh(proj_chunk(hb_cur, GM_HALF + lo))
        vraw_cur[:, lo:lo + GM_TN] = gz
        gf = gz.astype(F32)
        pieces = [gf[:, k:k + 128] for k in range(0, GM_TN, 128)]
        t1 = functools.reduce(lambda p, q: p + q, pieces)
        t2 = functools.reduce(lambda p, q: p + q, [p * p for p in pieces])
        if n == 0:
            sum_s[:, 0:128] = t1
            sum_s[:, 128:256] = t2
        else:
            sum_s[:, 0:128] += t1
            sum_s[:, 128:256] += t2

    chunks = list(range(0, GM_HALF, GM_TN))
    stage_b_chunk(chunks[0])
    a_is_prompt = jnp.minimum(i, n_tiles - 1) < n_prompt_tiles
    x = _add_moe(x_ref[...], yg0_ref, yg1_ref, gates_ref, _tile_mod(refs[4:6], a_is_prompt))
    xo_ref[...] = x
    hb_cur[...] = _norm_mod(x, gain_ref, _tile_mod(refs[6:8], a_is_prompt), 0).astype(BF16)
    for n, lo in enumerate(chunks):
        stage_a_chunk(n, lo)
        if 0 < n < len(chunks) - 1:
            stage_b_chunk(lo)
    mu = jnp.sum(sum_s[:, 0:128], axis=-1, keepdims=True) * (1.0 / GM_HALF)
    var = jnp.sum(sum_s[:, 128:256], axis=-1, keepdims=True) * (1.0 / GM_HALF) - mu * mu
    rstd = lax.rsqrt(var + EPS)
    stage_b_chunk(chunks[-1])

    @pl.when(i - 1 >= n_prompt_tiles)
    def _():
        row0 = pl.multiple_of((i - 1 - n_prompt_tiles) * rows, rows)

        def vs_copy(slot, lay, lo):
            return pltpu.make_async_copy(
                vs_stage.at[slot], vs_hbm.at[lay, pl.ds(row0, rows), pl.ds(lo, GM_TN)], vs_sems.at[slot])

        work = [(layer, lo) for lo in chunks]
        if layer == 0:
            work += [(later, lo) for later in range(1, N_GM) for lo in chunks]
        for n, (lay, lo) in enumerate(work):
            slot = n % 2
            if n >= 2:
                vs_copy(slot, *work[n - 2]).wait()
            if lay == layer:
                for k in range(0, GM_TN, 128):
                    vs_stage[slot, :, k:k + 128] = v_norm_piece(lo + k)
            else:
                vs_stage[slot] = jnp.zeros((rows, GM_TN), F32)
            vs_copy(slot, lay, lo).start()
        for n in range(max(len(work) - 2, 0), len(work)):
            vs_copy(n % 2, *work[n]).wait()

    stat_prev[:, 0:128] = jnp.broadcast_to(rstd, (rows, 128))
    stat_prev[:, 128:256] = jnp.broadcast_to(-mu * rstd, (rows, 128))
    hb_prev[...] = hb_cur[...]
    vraw_prev[...] = vraw_cur[...]


def _gm_proj_call(x, prev, mod, gain, w_in, layer, b_in, ln_g, ln_b, vs_all):
    rows = ROWS_GM_PROJ
    n, npt = _n_tiles(rows), _n_prompt_tiles(rows)

    def stage_a(spec):
        return _stage_a_spec(spec, n)

    stage_b = _stage_b_spec
    in_specs = [stage_a(s) for s in [_row_spec(rows, D)] + _prev_specs(rows) + _mod_specs(rows)] + [
        _resident((1, D)), pl.BlockSpec(memory_space=pl.ANY), _resident((8, GM_FFN)),
        _resident((8, GM_HALF)), _resident((8, GM_HALF))]
    rows8 = [jnp.broadcast_to(r.reshape(1, -1), (8, r.size)) for r in (b_in, ln_g, ln_b)]
    args = [x, *prev, *mod, gain, w_in, *rows8]
    aliases = {}
    if layer > 0:
        in_specs.append(pl.BlockSpec(memory_space=pl.ANY))
        args.append(vs_all)
        aliases = {len(args) - 1: 2}
    return pl.pallas_call(
        functools.partial(_gm_proj_kernel, n, npt, layer),
        grid=(n + 1,),
        in_specs=in_specs,
        out_specs=[stage_a(_row_spec(rows, D)), stage_b(_row_spec(rows, GM_FFN)),
                   pl.BlockSpec(memory_space=pl.ANY)],
        out_shape=[jax.ShapeDtypeStruct((T_ALL, D), F32),
                   jax.ShapeDtypeStruct((T_ALL, GM_FFN), BF16),
                   jax.ShapeDtypeStruct((N_GM, T_SAMPLE, GM_HALF), F32)],
        scratch_shapes=[pltpu.VMEM((rows, D), BF16), pltpu.VMEM((rows, D), BF16),
                        pltpu.VMEM((rows, GM_HALF), BF16), pltpu.VMEM((rows, GM_HALF), BF16),
                        pltpu.VMEM((rows, 256), F32), pltpu.VMEM((rows, 256), F32),
                        pltpu.VMEM((2, rows, GM_TN), F32), pltpu.SemaphoreType.DMA((2,)),
                        *_weight_scratch(D, GM_FFN)],
        input_output_aliases=aliases,
        compiler_params=_params("arbitrary"),
        name="gm_proj",
    )(*args)


def _gm_out_kernel(n_prompt_tiles, layer, uv_ref, ws_ref, bs_ref, w_hbm, bo_ref, x_ref, modp_ref,
                   mods_ref, gain_ref, wr_ref, br_ref, xo_ref, h2_ref, lg_ref, w_s, w_stage, w_sems):
    _load_weight_bf16(w_hbm, layer, w_s, w_stage, w_sems)
    rows = uv_ref.shape[0]
    mod = _tile_mod((modp_ref, mods_ref), pl.program_id(0) < n_prompt_tiles)
    pieces = []
    for r0 in range(0, rows, GM_MIX):
        acc = jnp.zeros((GM_MIX, D), F32)
        for g in range(GM_GROUPS):
            lo = g * GM_GDIM
            sp = jnp.dot(ws_ref[0, g], uv_ref[r0:r0 + GM_MIX, GM_HALF + lo:GM_HALF + lo + GM_GDIM],
                         preferred_element_type=F32) + bs_ref[0, g]
            gated = (uv_ref[r0:r0 + GM_MIX, lo:lo + GM_GDIM].astype(F32) * sp).astype(BF16)
            acc = acc + jnp.dot(gated, w_s[lo:lo + GM_GDIM, :], preferred_element_type=F32)
        pieces.append(acc)
    acc = jnp.concatenate(pieces, axis=0) + bo_ref[...]
    _residual_router(acc, x_ref, mod, gain_ref, wr_ref, br_ref, xo_ref, h2_ref, lg_ref)


def _gm_block_diag(w_s, b_s):
    mats, biases = [], []
    for cl in (GM_CHUNK, DEC_SEQ):
        tri = jnp.tril(jnp.ones((cl, cl), bool))
        blk = jnp.where(tri[None], w_s[:, :cl, :cl], 0.0)
        reps = GM_MIX // cl
        eye = jnp.eye(reps, dtype=w_s.dtype)
        bd = jnp.einsum("ab,gts->gatbs", eye, blk).reshape(GM_GROUPS, GM_MIX, GM_MIX)
        mats.append(bd)
        biases.append(jnp.tile(b_s[:, :cl], (1, reps))[:, :, None])
    return jnp.stack(mats).astype(BF16), jnp.stack(biases).astype(F32)


def _gm_out_call(uv, ws_bd, bs_bd, w_out, layer, b_out, x, mod, gain, w_r, b_r):
    rows = ROWS_WIDE
    npt = _n_prompt_tiles(rows)

    def variant(i):
        return jnp.where(i >= npt, 1, 0)

    return pl.pallas_call(
        functools.partial(_gm_out_kernel, npt, layer),
        grid=(_n_tiles(rows),),
        in_specs=[_row_spec(rows, GM_FFN),
                  pl.BlockSpec((1, GM_GROUPS, GM_MIX, GM_MIX), lambda i: (variant(i), 0, 0, 0)),
                  pl.BlockSpec((1, GM_GROUPS, GM_MIX, 1), lambda i: (variant(i), 0, 0, 0)),
                  pl.BlockSpec(memory_space=pl.ANY), _resident((1, D)), _row_spec(rows, D),
                  *_mod_specs(rows), _resident((1, D))] + _router_specs(),
        out_specs=_mix_out_specs(rows),
        out_shape=_MIX_OUT_SHAPE,
        scratch_shapes=_weight_scratch(GM_HALF, D),
        compiler_params=_params("arbitrary"),
        name="gm_out",
    )(uv, ws_bd, bs_bd, w_out, b_out, x, *mod, gain, w_r, b_r)


def _expert_kernel(layer, be_ref, bv_ref, nb_ref, nx_ref, xb_ref, w1_hbm, w3_hbm, w2_hbm, yb_ref,
                   st1, st3, st2, w1s, w3s, w2s, sems, slot_ref):
    b = pl.program_id(0)

    def weight_copies(e, slot):
        return (pltpu.make_async_copy(w1_hbm.at[layer, e], st1.at[slot], sems.at[slot, 0]),
                pltpu.make_async_copy(w3_hbm.at[layer, e], st3.at[slot], sems.at[slot, 1]),
                pltpu.make_async_copy(w2_hbm.at[layer, e], st2.at[slot], sems.at[slot, 2]))

    @pl.when(b == 0)
    def _():
        slot_ref[0] = 0
        for cp in weight_copies(be_ref[0], 0):
            cp.start()

    @pl.when(b < nb_ref[0])
    def _():
        prev_e = be_ref[jnp.maximum(b - 1, 0)]

        @pl.when((b == 0) | (be_ref[b] != prev_e))
        def _():
            slot = slot_ref[0]
            for cp in weight_copies(be_ref[b], slot):
                cp.wait()
            w1s[...] = st1[slot].astype(BF16)
            w3s[...] = st3[slot].astype(BF16)
            w2s[...] = st2[slot].astype(BF16)

            @pl.when(nx_ref[b] >= 0)
            def _():
                for cp in weight_copies(nx_ref[b], 1 - slot):
                    cp.start()

            slot_ref[0] = 1 - slot

        valid = bv_ref[b]

        def run_rows(n):
            row = lax.broadcasted_iota(jnp.int32, (n, 1), 0)
            xw = jnp.where(row < valid, xb_ref[0:n], U32(0))
            x = _unpack_bf16_pairs(xw).astype(BF16)
            a = jnp.dot(x, w1s[...], preferred_element_type=F32)
            c = jnp.dot(x, w3s[...], preferred_element_type=F32)
            h = (_silu(a) * c).astype(BF16)
            yb_ref[0:n] = _pack_bf16_pairs(jnp.dot(h, w2s[...], preferred_element_type=F32))
            if n < EXP_BLOCK:
                yb_ref[n:EXP_BLOCK] = jnp.zeros((EXP_BLOCK - n, DP), U32)

        for n in range(EXP_ROW_STEP, EXP_BLOCK + 1, EXP_ROW_STEP):
            pl.when((valid > n - EXP_ROW_STEP) & (valid <= n))(functools.partial(run_rows, n))


def _expert_call(blk_e, blk_valid, n_blk, blk_next, xb, w1, w3, w2, layer):
    def blk(b, be, bv, nb, nx):
        return (jnp.minimum(b, nb[0] - 1), 0)

    up, down = (D, MOE_HIDDEN), (MOE_HIDDEN, D)
    grid_spec = pltpu.PrefetchScalarGridSpec(
        num_scalar_prefetch=4,
        grid=(N_EXP_BLOCKS,),
        in_specs=[pl.BlockSpec((EXP_BLOCK, DP), blk)] + [pl.BlockSpec(memory_space=pl.ANY)] * 3,
        out_specs=pl.BlockSpec((EXP_BLOCK, DP), blk),
        scratch_shapes=[pltpu.VMEM((2,) + up, F32), pltpu.VMEM((2,) + up, F32), pltpu.VMEM((2,) + down, F32),
                        pltpu.VMEM(up, BF16), pltpu.VMEM(up, BF16), pltpu.VMEM(down, BF16),
                        pltpu.SemaphoreType.DMA((2, 3)), pltpu.SMEM((1,), jnp.int32)],
    )
    return pl.pallas_call(
        functools.partial(_expert_kernel, layer),
        grid_spec=grid_spec,
        out_shape=jax.ShapeDtypeStruct((P_ROWS, DP), U32),
        compiler_params=_params("arbitrary"),
        name="experts",
    )(blk_e, blk_valid, n_blk, blk_next, xb, w1, w3, w2)


def _route_kernel(lg_ref, dest_ref, gates_ref, meta_ref, cnt_ref):
    t = pl.program_id(0)

    @pl.when(t == 0)
    def _():
        cnt_ref[...] = jnp.zeros_like(cnt_ref)

    pl.when(t < N_ROUTE_TILES)(functools.partial(_route_tile, t, lg_ref, dest_ref, gates_ref, cnt_ref))
    pl.when(t == N_ROUTE_TILES)(functools.partial(_route_finish, dest_ref, meta_ref, cnt_ref))


ROUTE_RANK_BITS = 16
assert P_ROWS < 2 ** ROUTE_RANK_BITS


def _route_tile(t, lg_ref, dest_ref, gates_ref, cnt_ref):
    tm = ROUTE_TM
    lt = lg_ref[...].T
    el = lt[0:MOE_EXPERTS]
    gl = lt[MOE_EXPERTS:MOE_EXPERTS + 8]
    gidx = lax.broadcasted_iota(jnp.int32, (8, tm), 0)
    neg = jnp.float32(-jnp.inf)
    gl = jnp.where(gidx < MOE_GROUPS, gl, neg)
    gmax = jnp.max(gl, axis=0, keepdims=True)
    grp = jnp.min(jnp.where(gl == gmax, gidx, MOE_GROUPS), axis=0, keepdims=True)
    eidx = lax.broadcasted_iota(jnp.int32, (MOE_EXPERTS, tm), 0)
    els = jnp.where((eidx >> 3) == grp, el, neg)
    m1 = jnp.max(els, axis=0, keepdims=True)
    i1 = jnp.min(jnp.where(els == m1, eidx, MOE_EXPERTS), axis=0, keepdims=True)
    els2 = jnp.where(eidx == i1, neg, els)
    m2 = jnp.max(els2, axis=0, keepdims=True)
    i2 = jnp.min(jnp.where(els2 == m2, eidx, MOE_EXPERTS), axis=0, keepdims=True)
    sel1 = eidx == i1
    sel2 = eidx == i2
    cnt = jnp.where(sel1 | sel2, 1.0, 0.0)

    lane = ROUTER_LANES
    before = (lax.broadcasted_iota(jnp.int32, (lane, lane), 0)
              < lax.broadcasted_iota(jnp.int32, (lane, lane), 1))
    tri = jnp.where(before, 1.0, 0.0).astype(BF16)
    run = cnt_ref[...]
    r1, r2 = [], []
    for k in range(tm // lane):
        piece = slice(k * lane, (k + 1) * lane)
        ck = cnt[:, piece]
        pos = run + jnp.dot(ck.astype(BF16), tri, preferred_element_type=F32)
        r1.append(jnp.sum(jnp.where(sel1[:, piece], pos, 0.0), axis=0, keepdims=True))
        r2.append(jnp.sum(jnp.where(sel2[:, piece], pos, 0.0), axis=0, keepdims=True))
        run = run + jnp.sum(ck, axis=1, keepdims=True)
    cnt_ref[...] = run
    rank = jnp.concatenate([jnp.concatenate(r1, axis=1), jnp.concatenate(r2, axis=1)], axis=0)
    eid = jnp.concatenate([i1, i2], axis=0)
    dest_ref[:, pl.ds(pl.multiple_of(t * tm, tm), tm)] = (eid << ROUTE_RANK_BITS) + rank.astype(jnp.int32)

    g_w = 1.0 / jnp.sum(jnp.exp(gl - gmax), axis=0, keepdims=True)
    e21 = jnp.exp(m2 - m1)
    p1 = 1.0 / (1.0 + e21)
    rid = lax.broadcasted_iota(jnp.int32, (ROUTER_LANES, tm), 0)
    gt = jnp.where(rid == 0, g_w * p1, jnp.where(rid == 1, g_w * (e21 * p1), 0.0))
    gates_ref[...] = gt.T


def _route_finish(dest_ref, meta_ref, cnt_ref):
    counts = cnt_ref[...]
    nblk = jnp.floor((counts + (EXP_BLOCK - 1.0)) * (1.0 / EXP_BLOCK))
    r = lax.broadcasted_iota(jnp.int32, (MOE_EXPERTS, MOE_EXPERTS), 0)
    c = lax.broadcasted_iota(jnp.int32, (MOE_EXPERTS, MOE_EXPERTS), 1)
    nblk_row = jnp.sum(jnp.where(r == c, nblk, 0.0), axis=0, keepdims=True)
    bstart = jnp.sum(jnp.where(c < r, nblk_row, 0.0), axis=1, keepdims=True)
    bend = bstart + nblk
    bidx = lax.broadcasted_iota(jnp.int32, (1, META_LANES), 1).astype(F32)
    blk_e = jnp.minimum(jnp.sum(jnp.where(bidx >= bend, 1.0, 0.0), axis=0, keepdims=True),
                        MOE_EXPERTS - 1.0)
    erow = lax.broadcasted_iota(jnp.int32, (MOE_EXPERTS, META_LANES), 0).astype(F32)
    mine = erow == blk_e
    cnt_b = jnp.sum(jnp.where(mine, counts, 0.0), axis=0, keepdims=True)
    start_b = jnp.sum(jnp.where(mine, bstart, 0.0), axis=0, keepdims=True)
    valid = jnp.clip(cnt_b - (bidx - start_b) * EXP_BLOCK, 0.0, float(EXP_BLOCK))
    n_blk = jnp.sum(nblk, axis=0, keepdims=True)
    end_b = jnp.sum(jnp.where(mine, bend, 0.0), axis=0, keepdims=True)
    nxt = jnp.minimum(jnp.sum(jnp.where(end_b >= bend, 1.0, 0.0), axis=0, keepdims=True),
                      MOE_EXPERTS - 1.0)
    nxt = jnp.where(end_b < n_blk, nxt, -1.0)
    mrow = lax.broadcasted_iota(jnp.int32, (8, META_LANES), 0)
    meta = jnp.where(mrow == 0, blk_e, jnp.where(mrow == 1, valid, jnp.where(
        mrow == 2, n_blk, jnp.where(mrow == 3, nxt, 0.0))))
    meta_ref[...] = meta.astype(jnp.int32)

    base = (bstart * EXP_BLOCK).astype(jnp.int32)
    packed = dest_ref[...]
    eid = packed >> ROUTE_RANK_BITS
    row = packed & (2 ** ROUTE_RANK_BITS - 1)
    for e in range(MOE_EXPERTS):
        row = row + jnp.where(eid == e, base[e:e + 1, :], 0)
    dest_ref[...] = row


def _route_call(logits):
    last = N_ROUTE_TILES - 1
    return pl.pallas_call(
        _route_kernel,
        grid=(N_ROUTE_TILES + 1,),
        in_specs=[pl.BlockSpec((ROUTE_TM, ROUTER_LANES), lambda t: (jnp.minimum(t, last), 0))],
        out_specs=[pl.BlockSpec((MOE_TOPK, T_ALL), lambda t: (0, 0)),
                   pl.BlockSpec((ROUTE_TM, ROUTER_LANES), lambda t: (jnp.minimum(t, last), 0)),
                   pl.BlockSpec((8, META_LANES), lambda t: (0, 0))],
        out_shape=[jax.ShapeDtypeStruct((MOE_TOPK, T_ALL), jnp.int32),
                   jax.ShapeDtypeStruct((T_ALL, ROUTER_LANES), F32),
                   jax.ShapeDtypeStruct((8, META_LANES), jnp.int32)],
        scratch_shapes=[pltpu.VMEM((MOE_EXPERTS, 1), F32)],
        compiler_params=_params("arbitrary"),
        name="route",
    )(logits)


def _sc_mesh():
    return plsc.VectorSubcoreMesh(core_axis_name="c", subcore_axis_name="s")


def _sc_token_offset(j):
    wid = lax.axis_index("s") * SC_CORES + lax.axis_index("c")
    return pl.multiple_of(wid * SC_ROWS_PER_WORKER + j * SC_CHUNK, 8)


SC_N_CHUNKS = SC_ROWS_PER_WORKER // SC_CHUNK
assert SC_N_CHUNKS % 2 == 1


def _start(copies):
    for cp in copies:
        cp.start()


def _wait(copies):
    for cp in copies:
        cp.wait()


def _dispatch_body(h_hbm, d0_hbm, d1_hbm, out_hbm, i0a, i1a, rows_a, i0b, i1b, rows_b, la, lb, sa, sb):
    sets = {"a": (i0a, i1a, rows_a, la, sa), "b": (i0b, i1b, rows_b, lb, sb)}

    def loads(j, s):
        i0, i1, rows, lsem, _ = sets[s]
        src = pl.ds(_sc_token_offset(j), SC_CHUNK)
        return (pltpu.make_async_copy(d0_hbm.at[src], i0, lsem),
                pltpu.make_async_copy(d1_hbm.at[src], i1, lsem),
                pltpu.make_async_copy(h_hbm.at[src], rows, lsem))

    def scatters(s):
        i0, i1, rows, _, ssem = sets[s]
        return (pltpu.make_async_copy(rows, out_hbm.at[i0], ssem),
                pltpu.make_async_copy(rows, out_hbm.at[i1], ssem))

    _start(loads(0, "a"))

    @pl.loop(0, SC_N_CHUNKS // 2)
    def _(p):
        a = 2 * p
        _start(loads(a + 1, "b"))
        _wait(loads(a, "a"))
        _start(scatters("a"))
        _wait(loads(a + 1, "b"))
        _start(scatters("b"))
        _wait(scatters("a"))
        _start(loads(a + 2, "a"))
        _wait(scatters("b"))

    _wait(loads(SC_N_CHUNKS - 1, "a"))
    _start(scatters("a"))
    _wait(scatters("a"))


def _dispatch_call(h2, dest0, dest1):
    buffers = [pltpu.VMEM((SC_CHUNK,), jnp.int32), pltpu.VMEM((SC_CHUNK,), jnp.int32),
               pltpu.VMEM((SC_CHUNK, DP), U32)]
    return pl.kernel(
        _dispatch_body,
        out_type=jax.ShapeDtypeStruct((P_ROWS, DP), U32),
        mesh=_sc_mesh(),
        scratch_types=buffers + buffers + [pltpu.SemaphoreType.DMA] * 4,
        name="moe_dispatch",
    )(h2, dest0, dest1)


def _combine_body(yb_hbm, d0_hbm, d1_hbm, o0_hbm, o1_hbm, i0a, i1a, r0a, r1a, i0b, i1b, r0b, r1b,
                  ia, ib, ga, gb, wa, wb):
    sets = {"a": (i0a, i1a, r0a, r1a, ia, ga, wa), "b": (i0b, i1b, r0b, r1b, ib, gb, wb)}

    def index_loads(j, s):
        i0, i1, _, _, isem, _, _ = sets[s]
        src = pl.ds(_sc_token_offset(j), SC_CHUNK)
        return (pltpu.make_async_copy(d0_hbm.at[src], i0, isem), pltpu.make_async_copy(d1_hbm.at[src], i1, isem))

    def gathers(s):
        i0, i1, r0, r1, _, gsem, _ = sets[s]
        return (pltpu.make_async_copy(yb_hbm.at[i0], r0, gsem), pltpu.make_async_copy(yb_hbm.at[i1], r1, gsem))

    def writes(j, s):
        _, _, r0, r1, _, _, wsem = sets[s]
        dst = pl.ds(_sc_token_offset(j), SC_CHUNK)
        return (pltpu.make_async_copy(r0, o0_hbm.at[dst], wsem), pltpu.make_async_copy(r1, o1_hbm.at[dst], wsem))

    _start(index_loads(0, "a"))
    _wait(index_loads(0, "a"))
    _start(gathers("a"))

    @pl.loop(0, SC_N_CHUNKS // 2)
    def _(p):
        a = 2 * p
        _start(index_loads(a + 1, "b"))
        _wait(index_loads(a + 1, "b"))
        _wait(gathers("a"))
        _start(writes(a, "a"))
        _start(gathers("b"))
        _start(index_loads(a + 2, "a"))
        _wait(index_loads(a + 2, "a"))
        _wait(writes(a, "a"))
        _wait(gathers("b"))
        _start(writes(a + 1, "b"))
        _start(gathers("a"))
        _wait(writes(a + 1, "b"))

    _wait(gathers("a"))
    _start(writes(SC_N_CHUNKS - 1, "a"))
    _wait(writes(SC_N_CHUNKS - 1, "a"))


def _combine_call(yb, dest0, dest1):
    out = jax.ShapeDtypeStruct((T_ALL, DP), U32)
    buffers = [pltpu.VMEM((SC_CHUNK,), jnp.int32), pltpu.VMEM((SC_CHUNK,), jnp.int32),
               pltpu.VMEM((SC_CHUNK, DP), U32), pltpu.VMEM((SC_CHUNK, DP), U32)]
    return pl.kernel(
        _combine_body,
        out_type=(out, out),
        mesh=_sc_mesh(),
        scratch_types=buffers + buffers + [pltpu.SemaphoreType.DMA] * 6,
        name="moe_combine",
    )(yb, dest0, dest1)


def _moe_rows(h2, dest, meta, w1, w3, w2, layer):
    dest0, dest1 = dest[0], dest[1]
    xb = _dispatch_call(h2, dest0, dest1)
    yb = _expert_call(meta[0, :N_EXP_BLOCKS], meta[1, :N_EXP_BLOCKS], meta[2, :1], meta[3, :N_EXP_BLOCKS],
                      xb, w1, w3, w2, layer)
    return _combine_call(yb, dest0, dest1)


def _final_kernel(is_prompt, x_ref, yg0_ref, yg1_ref, gates_ref, modp_ref, mods_ref, gain_ref, o_ref):
    x = _add_moe(x_ref[...], yg0_ref, yg1_ref, gates_ref, _tile_mod((modp_ref, mods_ref), is_prompt))
    o_ref[...] = _rms(x) * gain_ref[...]


def _final_call(x, prev, gain, row0, n_rows):
    rows = ROWS_WIDE
    tile0 = row0 // rows
    is_prompt = row0 < T_PROMPT
    assert row0 + n_rows <= T_PROMPT or not is_prompt

    def tile(width):
        return pl.BlockSpec((rows, width), lambda i: (tile0 + i, 0))

    mod_specs = [pl.BlockSpec(s.block_shape, lambda i, m=s.index_map: m(tile0 + i)) for s in _mod_specs(rows)]
    return pl.pallas_call(
        functools.partial(_final_kernel, is_prompt),
        grid=(n_rows // rows,),
        in_specs=[tile(D), tile(DP), tile(DP), tile(ROUTER_LANES), *mod_specs, _resident((1, D))],
        out_specs=pl.BlockSpec((rows, D), lambda i: (i, 0)),
        out_shape=jax.ShapeDtypeStruct((n_rows, D), F32),
        compiler_params=_params("parallel"),
        name="final_norm",
    )(x, *prev, gain)


def _rope_table():
    pos = np.concatenate([np.tile(np.arange(SEQ), BATCH),
                          np.tile(PAST_LEN + np.arange(DEC_SEQ), DEC_BATCH)]).astype(np.float32)
    inv = (ROPE_BASE ** (-np.arange(ROPE_HALF, dtype=np.float32) / ROPE_HALF)).astype(np.float32)
    ang = (pos[:, None] * inv[None, :]).astype(np.float32).astype(np.float64)
    pad = np.zeros((T_ALL, ROPE_TABLE_LANES - 2 * ROPE_HALF - 2 * RET_HEADS))
    return jnp.asarray(np.concatenate([np.cos(ang), np.sin(ang), _ret_row_scales(), pad], axis=1), F32)


def kernel(x_prompt, x_sample, c_prompt, c_sample, state_ret, ada_w, ada_b, norm1_g, norm2_g, ret_w_in,
           ret_w_out, gm_w_in, gm_b_in, gm_ln_g, gm_ln_b, gm_w_s, gm_b_s, gm_w_out, gm_b_out, moe_w_rg,
           moe_b_rg, moe_w_re, moe_b_re, moe_w1, moe_w3, moe_w2, final_g):
    x = (x_prompt.reshape(T_PROMPT, D), x_sample.reshape(T_SAMPLE, D))
    c_all = jnp.concatenate([c_prompt, c_sample], axis=0)
    rope = _rope_table()

    mod_all = _ada_call(c_all, ada_w, ada_b).reshape(DEPTH, N_SEQ, 6, D)

    def layer_params(i):
        mod = (mod_all[i, :BATCH], mod_all[i, BATCH:])
        w_r = jnp.pad(jnp.concatenate([moe_w_re[i], moe_w_rg[i]], axis=1),
                      ((0, 0), (0, ROUTER_LANES - MOE_GROUPS - MOE_EXPERTS)))
        w_r_hi = w_r.astype(BF16)
        w_r_lo = (w_r - w_r_hi.astype(F32)).astype(BF16)
        w_r = jnp.concatenate([w_r_hi, w_r_lo], axis=1)
        b_r = jnp.pad(jnp.concatenate([moe_b_re[i].reshape(-1), moe_b_rg[i]]),
                      (0, ROUTER_LANES - MOE_GROUPS - MOE_EXPERTS)).reshape(1, ROUTER_LANES)
        return mod, w_r, b_r

    ret_prompt = ret_sample = gm_sample = None
    prev = None
    for i in range(DEPTH):
        j = i // 2
        mod, w_r, b_r = layer_params(i)
        g1 = norm1_g[i].reshape(1, D)
        g2 = norm2_g[i].reshape(1, D)
        if i % 2 == 0:
            x, p = _ret_proj_call(x, prev, mod, g1, ret_w_in, j, rope)
            y_p, ret_prompt = _ret_core_call(p, None, ret_prompt, j, BATCH, SEQ, RET_CHUNK_PROMPT, 0)
            y_s, ret_sample = _ret_core_call(p, state_ret, ret_sample, j, DEC_BATCH, DEC_SEQ,
                                             RET_CHUNK_SAMPLE, T_PROMPT)
            x, h2, logits = _ret_out_call(y_p, y_s, ret_w_out, j, x, mod, g2, w_r, b_r)
        else:
            x, uv, gm_sample = _gm_proj_call(x, prev, mod, g1, gm_w_in, j, gm_b_in[j], gm_ln_g[j],
                                             gm_ln_b[j], gm_sample)
            ws_bd, bs_bd = _gm_block_diag(gm_w_s[j], gm_b_s[j])
            x, h2, logits = _gm_out_call(uv, ws_bd, bs_bd, gm_w_out, j,
                                         gm_b_out[j].reshape(1, D), x, mod, g2, w_r, b_r)
        dest, gates, meta = _route_call(logits)
        yg0, yg1 = _moe_rows(h2, dest, meta, moe_w1, moe_w3, moe_w2, i)
        prev = (yg0, yg1, gates, *mod)

    fg = final_g.reshape(1, D)
    y_prompt = _final_call(x, prev, fg, 0, T_PROMPT).reshape(BATCH, SEQ, D)
    y_sample = _final_call(x, prev, fg, T_PROMPT, T_SAMPLE).reshape(DEC_BATCH, DEC_SEQ, D)
    return (y_prompt, y_sample, ret_prompt, ret_sample,
            gm_sample.reshape(N_GM, DEC_BATCH, DEC_SEQ, GM_HALF))
```

```python
import functools

import numpy as np
import jax
import jax.numpy as jnp
from jax import lax
from jax.experimental import pallas as pl
from jax.experimental.pallas import tpu as pltpu
from jax.experimental.pallas import tpu_sc as plsc

F32 = jnp.float32
BF16 = jnp.bfloat16
U32 = jnp.uint32

D = 1024
BATCH, SEQ = 4, 4096
DEC_BATCH, DEC_SEQ = 16, 64
PAST_LEN = 4096
DEPTH = 4
N_RET = (DEPTH + 1) // 2
N_GM = DEPTH // 2
N_SEQ = BATCH + DEC_BATCH

RET_HEADS, RET_DK, RET_DV = 4, 256, 512
RET_QK = RET_HEADS * RET_DK
RET_V = RET_HEADS * RET_DV
RET_IN = 2 * RET_QK + 2 * RET_V
ROPE_BASE = 10000.0
ROPE_HALF = RET_DK // 2
ROPE_TABLE_LANES = 3 * ROPE_HALF

GM_FFN = 6 * D
GM_HALF = GM_FFN // 2
GM_GROUPS = 4
GM_GDIM = GM_HALF // GM_GROUPS
GM_CHUNK = 128

MOE_GROUPS, MOE_PER_GROUP = 4, 8
MOE_EXPERTS = MOE_GROUPS * MOE_PER_GROUP
MOE_TOPK = 2
MOE_HIDDEN = 512
EPS = 1e-6

GROUP = DEC_SEQ
T_PROMPT = BATCH * SEQ
T_SAMPLE = DEC_BATCH * DEC_SEQ
T_ALL = T_PROMPT + T_SAMPLE
N_GROUPS = T_ALL // GROUP
ROWS_WIDE = 512
ROWS_GM_PROJ = 256

RET_CHUNK_PROMPT = 256
RET_CHUNK_SAMPLE = DEC_SEQ

GM_MIX = 256

EXP_BLOCK = 512
EXP_ROW_STEP = 128
N_ASSIGN = T_ALL * MOE_TOPK
N_EXP_BLOCKS = -(-(N_ASSIGN + MOE_EXPERTS * (EXP_BLOCK - 1)) // EXP_BLOCK)
P_ROWS = N_EXP_BLOCKS * EXP_BLOCK
ROUTER_LANES = 128
ROUTE_TM = 1024
N_ROUTE_TILES = T_ALL // ROUTE_TM
META_LANES = 256
assert META_LANES >= N_EXP_BLOCKS

DP = D // 2
SC_CORES, SC_SUBCORES = 2, 16
SC_WORKERS = SC_CORES * SC_SUBCORES
SC_ROWS_PER_WORKER = T_ALL // SC_WORKERS
SC_CHUNK = 32
assert SC_ROWS_PER_WORKER % SC_CHUNK == 0 and SC_CHUNK % 8 == 0

V7X_VMEM_LIMIT_BYTES = 56 * 1024 * 1024


def _params(*sem):
    return pltpu.CompilerParams(dimension_semantics=sem, vmem_limit_bytes=V7X_VMEM_LIMIT_BYTES)


def _resident(shape):
    nd = len(shape)
    return pl.BlockSpec(shape, lambda *_: (0,) * nd, pipeline_mode=pl.Buffered(1))


WEIGHT_STAGE_BYTES = 3 * 1024 * 1024


def _weight_scratch(k, n):
    chunk = k
    while chunk * n * 4 > WEIGHT_STAGE_BYTES:
        assert chunk % 16 == 0
        chunk //= 2
    return [pltpu.VMEM((k, n), BF16), pltpu.VMEM((2, chunk, n), F32), pltpu.SemaphoreType.DMA((2,))]


def _load_weight_bf16(w_hbm, layer, w_s, stage, sems):
    k = w_s.shape[0]
    chunk = stage.shape[1]

    def copy(c):
        return pltpu.make_async_copy(w_hbm.at[layer, pl.ds(c * chunk, chunk)], stage.at[c % 2], sems.at[c % 2])

    @pl.when(pl.program_id(0) == 0)
    def _():
        copy(0).start()
        for c in range(k // chunk):
            if c + 1 < k // chunk:
                copy(c + 1).start()
            copy(c).wait()
            w_s[c * chunk:(c + 1) * chunk, :] = stage[c % 2].astype(BF16)


def _rms(x):
    return x * lax.rsqrt(jnp.mean(x * x, axis=-1, keepdims=True) + EPS)


def _silu(x):
    return x * jax.nn.sigmoid(x)


def _per_group(x2d, fn):
    rows = x2d.shape[0]
    return fn(x2d.reshape(rows // GROUP, GROUP, D)).reshape(rows, D)


def _tile_mod(mod_refs, is_prompt):
    modp_ref, mods_ref = mod_refs
    return jnp.where(is_prompt, jnp.broadcast_to(modp_ref[...], mods_ref.shape), mods_ref[...])


def _norm_mod(x, gain_ref, mod, shift_idx):
    y = _rms(x) * gain_ref[...]
    scale = mod[:, shift_idx + 1:shift_idx + 2, :]
    shift = mod[:, shift_idx:shift_idx + 1, :]
    return _per_group(y, lambda y3: y3 * (1.0 + scale) + shift)


def _pack_bf16_pairs(x):
    lo = lax.bitcast_convert_type(x[:, :DP].astype(BF16).astype(F32), U32)
    hi = lax.bitcast_convert_type(x[:, DP:].astype(BF16).astype(F32), U32)
    return (lo >> 16) | (hi & U32(0xFFFF0000))


def _unpack_bf16_pairs(w):
    lo = lax.bitcast_convert_type(w << 16, F32)
    hi = lax.bitcast_convert_type(w & U32(0xFFFF0000), F32)
    return jnp.concatenate([lo, hi], axis=1)


def _add_moe(x, yg0_ref, yg1_ref, gates_ref, mod_prev):
    g = gates_ref[...]
    y = g[:, 0:1] * _unpack_bf16_pairs(yg0_ref[...]) + g[:, 1:2] * _unpack_bf16_pairs(yg1_ref[...])
    gate2 = mod_prev[:, 5:6, :]
    return x + _per_group(y, lambda y3: y3 * gate2)


ADA_TN = 1536


def _ada_kernel(c_ref, w_ref, b_ref, o_ref):
    c = c_ref[...]
    s = _silu(c).astype(BF16)
    o_ref[0] = jnp.dot(s, w_ref[0].astype(BF16), preferred_element_type=F32) + b_ref[0]


def _ada_call(c_all, ada_w, ada_b):
    return pl.pallas_call(
        _ada_kernel,
        grid=(DEPTH, 6 * D // ADA_TN),
        in_specs=[
            pl.BlockSpec((N_SEQ, D), lambda i, j: (0, 0)),
            pl.BlockSpec((1, D, ADA_TN), lambda i, j: (i, 0, j)),
            pl.BlockSpec((1, 1, ADA_TN), lambda i, j: (i, 0, j)),
        ],
        out_specs=pl.BlockSpec((1, N_SEQ, ADA_TN), lambda i, j: (i, 0, j)),
        out_shape=jax.ShapeDtypeStruct((DEPTH, N_SEQ, 6 * D), F32),
        compiler_params=_params("parallel", "parallel"),
        name="ada_modulation",
    )(c_all, ada_w, ada_b.reshape(DEPTH, 1, 6 * D))


def _n_tiles(rows):
    return T_ALL // rows


def _n_prompt_tiles(rows):
    return T_PROMPT // rows


def _row_spec(rows, width):
    return pl.BlockSpec((rows, width), lambda i: (i, 0))


def _mod_specs(rows):
    npt = _n_prompt_tiles(rows)
    return [pl.BlockSpec((1, 6, D), lambda i: (jnp.minimum(i * rows // SEQ, BATCH - 1), 0, 0)),
            pl.BlockSpec((rows // DEC_SEQ, 6, D), lambda i: (jnp.maximum(i - npt, 0), 0, 0))]


def _prev_specs(rows):
    return [_row_spec(rows, DP), _row_spec(rows, DP), _row_spec(rows, ROUTER_LANES)] + _mod_specs(rows)


def _stage_a_spec(spec, n_tiles):
    return pl.BlockSpec(spec.block_shape, lambda i, m=spec.index_map: m(jnp.minimum(i, n_tiles - 1)))


def _stage_b_spec(spec):
    return pl.BlockSpec(spec.block_shape, lambda i, m=spec.index_map: m(jnp.maximum(i - 1, 0)))


def _prompt_rows_spec(rows, width):
    last = _n_prompt_tiles(rows) - 1
    return pl.BlockSpec((rows, width), lambda i: (jnp.minimum(i, last), 0))


def _sample_rows_spec(rows, width):
    npt = _n_prompt_tiles(rows)
    return pl.BlockSpec((rows, width), lambda i: (jnp.maximum(i - npt, 0), 0))


def _ret_proj_kernel(has_prev, n_tiles, n_prompt_tiles, layer, *refs):
    i = pl.program_id(0)
    is_prompt = jnp.minimum(i, n_tiles - 1) < n_prompt_tiles
    head = refs[:6] if has_prev else refs[:2]
    refs = refs[len(head):]
    mod_refs = refs[:2]
    gain_ref, w_hbm, rope_ref, xo_ref, p_ref, hb_cur, hb_prev, w_s, w_stage, w_sems = refs[2:]
    _load_weight_bf16(w_hbm, layer, w_s, w_stage, w_sems)

    @pl.when(i == 0)
    def _():
        hb_prev[...] = jnp.zeros_like(hb_prev)

    def qk_head(j):
        lo = j * RET_DK
        acc = jnp.dot(hb_prev[...], w_s[:, lo:lo + RET_DK], preferred_element_type=F32)
        x1 = acc[:, :ROPE_HALF]
        x2 = acc[:, ROPE_HALF:]
        cos = rope_ref[:, 0:ROPE_HALF]
        sin = rope_ref[:, ROPE_HALF:2 * ROPE_HALF]
        scale = rope_ref[:, 2 * ROPE_HALF + j:2 * ROPE_HALF + j + 1]
        p_ref[:, lo:lo + ROPE_HALF] = ((x1 * cos - x2 * sin) * scale).astype(BF16)
        p_ref[:, lo + ROPE_HALF:lo + RET_DK] = ((x1 * sin + x2 * cos) * scale).astype(BF16)

    def vg_head(j):
        lo = 2 * RET_QK + j * RET_DV
        acc = jnp.dot(hb_prev[...], w_s[:, lo:lo + RET_DV], preferred_element_type=F32)
        if j >= RET_HEADS:
            acc = _silu(acc)
        p_ref[:, lo:lo + RET_DV] = acc.astype(BF16)

    vg_head(0)
    vg_head(1)
    if has_prev:
        x_ref, yg0_ref, yg1_ref, gates_ref = head[:4]
        x = _add_moe(x_ref[...], yg0_ref, yg1_ref, gates_ref, _tile_mod(head[4:6], is_prompt))
    else:
        x = jnp.where(is_prompt, head[0][...], head[1][...])
    xo_ref[...] = x
    hb_cur[...] = _norm_mod(x, gain_ref, _tile_mod(mod_refs, is_prompt), 0).astype(BF16)
    for j in range(2, 2 * RET_HEADS):
        vg_head(j)
    for j in range(2 * RET_HEADS):
        qk_head(j)
    hb_prev[...] = hb_cur[...]


def _ret_proj_call(x, prev, mod, gain, w_in, layer, rope):
    rows = ROWS_WIDE
    n = _n_tiles(rows)
    has_prev = prev is not None
    if has_prev:
        in_specs = [_row_spec(rows, D)] + _prev_specs(rows)
        args = [x] + list(prev)
    else:
        in_specs = [_prompt_rows_spec(rows, D), _sample_rows_spec(rows, D)]
        args = list(x)
    in_specs = [_stage_a_spec(s, n) for s in in_specs + _mod_specs(rows)] + [
        _resident((1, D)), pl.BlockSpec(memory_space=pl.ANY),
        _stage_b_spec(_row_spec(rows, ROPE_TABLE_LANES))]
    args += [*mod, gain, w_in, rope]
    return pl.pallas_call(
        functools.partial(_ret_proj_kernel, has_prev, n, _n_prompt_tiles(rows), layer),
        grid=(n + 1,),
        in_specs=in_specs,
        out_specs=[_stage_a_spec(_row_spec(rows, D), n), _stage_b_spec(_row_spec(rows, RET_IN))],
        out_shape=[jax.ShapeDtypeStruct((T_ALL, D), F32), jax.ShapeDtypeStruct((T_ALL, RET_IN), BF16)],
        scratch_shapes=[pltpu.VMEM((rows, D), BF16), pltpu.VMEM((rows, D), BF16),
                        *_weight_scratch(D, RET_IN)],
        compiler_params=_params("arbitrary"),
        name="ret_proj",
    )(*args)


def _ret_core_kernel(has_s0, n_chunks, layer, *refs):
    refs = list(refs)
    p_ref = refs.pop(0)
    s0_ref = refs.pop(0) if has_s0 else None
    causal_ref, cd_ref = refs[:2]
    y_ref, so_ref, s_ref = refs[-3:]
    c = pl.program_id(1)

    @pl.when(c == 0)
    def _():
        if has_s0:
            s_ref[...] = s0_ref[0, 0]
        else:
            s_ref[...] = jnp.zeros_like(s_ref)

    for h in range(RET_HEADS):
        qb = p_ref[:, h * RET_DK:(h + 1) * RET_DK]
        kb = p_ref[:, RET_QK + h * RET_DK:RET_QK + (h + 1) * RET_DK]
        vb = p_ref[:, 2 * RET_QK + h * RET_DV:2 * RET_QK + (h + 1) * RET_DV]
        gb = p_ref[:, 2 * RET_QK + RET_V + h * RET_DV:2 * RET_QK + RET_V + (h + 1) * RET_DV]
        scores = lax.dot_general(qb, kb, (((1,), (1,)), ((), ())), preferred_element_type=F32)
        scores = scores * causal_ref[...]
        s_old = s_ref[h]
        o = (jnp.dot(scores.astype(BF16), vb, preferred_element_type=F32)
             + jnp.dot(qb, s_old.astype(BF16), preferred_element_type=F32))
        s_ref[h] = cd_ref[h][:, 0:1] * (s_old + lax.dot_general(
            kb, vb, (((0,), (0,)), ((), ())), preferred_element_type=F32))
        y_ref[:, h * RET_DV:(h + 1) * RET_DV] = (gb.astype(F32) * _rms(o)).astype(BF16)

    @pl.when(c == n_chunks - 1)
    def _():
        so_ref[0, 0] = s_ref[...]
        if layer == 0:
            for later in range(1, N_RET):
                so_ref[later, 0] = jnp.zeros_like(s_ref)


def _ret_log_gamma():
    return np.log1p(-np.exp2(-5.0 - np.arange(RET_HEADS, dtype=np.float64)))


def _ret_chunk_tables(cl):
    idx = np.arange(cl)
    causal = (idx[:, None] >= idx[None, :]).astype(np.float32)
    cd = np.broadcast_to(np.exp(_ret_log_gamma() * cl)[:, None, None], (RET_HEADS, 1, 128))
    return jnp.asarray(causal, F32), jnp.asarray(cd, F32)


def _ret_row_scales():
    c = np.concatenate([np.arange(T_PROMPT) % RET_CHUNK_PROMPT,
                        np.arange(T_SAMPLE) % RET_CHUNK_SAMPLE]).astype(np.float64)
    e = (c[:, None] + 1.0) * _ret_log_gamma()[None, :]
    return np.concatenate([np.exp(e), np.exp(-e) * RET_DK ** -0.5], axis=1)


def _ret_core_call(p, s0, states, layer, n_seq, seq_len, cl, row0):
    has_s0 = s0 is not None
    n_chunks = seq_len // cl
    rb0 = row0 // cl
    state = (RET_HEADS, RET_DK, RET_DV)
    in_specs = [pl.BlockSpec((cl, RET_IN), lambda b, c: (rb0 + b * n_chunks + c, 0))]
    args = [p]
    if has_s0:
        in_specs.append(pl.BlockSpec((1, 1) + state, lambda b, c: (layer, b, 0, 0, 0)))
        args.append(s0)
    in_specs += [_resident((cl, cl)), _resident((RET_HEADS, 1, 128))]
    args += list(_ret_chunk_tables(cl))
    if layer == 0:
        assert states is None
        state_spec = pl.BlockSpec((N_RET, 1) + state, lambda b, c: (0, b, 0, 0, 0))
        aliases = {}
    else:
        in_specs.append(pl.BlockSpec(memory_space=pl.ANY))
        args.append(states)
        state_spec = pl.BlockSpec((1, 1) + state, lambda b, c: (layer, b, 0, 0, 0))
        aliases = {len(args) - 1: 1}
    return pl.pallas_call(
        functools.partial(_ret_core_kernel, has_s0, n_chunks, layer),
        grid=(n_seq, n_chunks),
        in_specs=in_specs,
        out_specs=[pl.BlockSpec((cl, RET_V), lambda b, c: (b * n_chunks + c, 0)), state_spec],
        out_shape=[jax.ShapeDtypeStruct((n_seq * seq_len, RET_V), BF16),
                   jax.ShapeDtypeStruct((N_RET, n_seq) + state, F32)],
        scratch_shapes=[pltpu.VMEM(state, F32)],
        input_output_aliases=aliases,
        compiler_params=_params("parallel", "arbitrary"),
        name="ret_core",
    )(*args)


def _residual_router(acc, x_ref, mod, gain_ref, wr_ref, br_ref, xo_ref, h2_ref, lg_ref):
    gate1 = mod[:, 2:3, :]
    xn = x_ref[...] + _per_group(acc, lambda a3: a3 * gate1)
    xo_ref[...] = xn
    h2 = _norm_mod(xn, gain_ref, mod, 3)
    h2_ref[...] = _pack_bf16_pairs(h2)
    hh = jnp.dot(h2.astype(BF16), wr_ref[...], preferred_element_type=F32)
    lg_ref[...] = hh[:, :ROUTER_LANES] + hh[:, ROUTER_LANES:] + br_ref[...]


def _mix_out_specs(rows):
    return [_row_spec(rows, D), _row_spec(rows, DP), _row_spec(rows, ROUTER_LANES)]


_MIX_OUT_SHAPE = [
    jax.ShapeDtypeStruct((T_ALL, D), F32),
    jax.ShapeDtypeStruct((T_ALL, DP), U32),
    jax.ShapeDtypeStruct((T_ALL, ROUTER_LANES), F32),
]


def _router_specs():
    return [_resident((D, 2 * ROUTER_LANES)), _resident((1, ROUTER_LANES))]


def _ret_out_kernel(n_prompt_tiles, layer, yp_ref, ys_ref, w_hbm, x_ref, modp_ref, mods_ref, gain_ref,
                    wr_ref, br_ref, xo_ref, h2_ref, lg_ref, w_s, w_stage, w_sems):
    _load_weight_bf16(w_hbm, layer, w_s, w_stage, w_sems)
    is_prompt = pl.program_id(0) < n_prompt_tiles
    yin = jnp.where(is_prompt, yp_ref[...], ys_ref[...])
    acc = jnp.dot(yin, w_s[...], preferred_element_type=F32)
    mod = _tile_mod((modp_ref, mods_ref), is_prompt)
    _residual_router(acc, x_ref, mod, gain_ref, wr_ref, br_ref, xo_ref, h2_ref, lg_ref)


def _ret_out_call(y_prompt, y_sample, w_out, layer, x, mod, gain, w_r, b_r):
    rows = ROWS_WIDE
    return pl.pallas_call(
        functools.partial(_ret_out_kernel, _n_prompt_tiles(rows), layer),
        grid=(_n_tiles(rows),),
        in_specs=[_prompt_rows_spec(rows, RET_V), _sample_rows_spec(rows, RET_V),
                  pl.BlockSpec(memory_space=pl.ANY), _row_spec(rows, D), *_mod_specs(rows),
                  _resident((1, D))] + _router_specs(),
        out_specs=_mix_out_specs(rows),
        out_shape=_MIX_OUT_SHAPE,
        scratch_shapes=_weight_scratch(RET_V, D),
        compiler_params=_params("arbitrary"),
        name="ret_out",
    )(y_prompt, y_sample, w_out, x, *mod, gain, w_r, b_r)


GM_TN = 512


_GELU_C = float(np.sqrt(2.0 / np.pi))


def _gelu_tanh(x):
    hx = 0.5 * x
    return hx * jnp.tanh(x * (_GELU_C + (_GELU_C * 0.044715) * (x * x))) + hx


def _gm_proj_kernel(n_tiles, n_prompt_tiles, layer, *refs):
    x_ref, yg0_ref, yg1_ref, gates_ref = refs[:4]
    gain_ref, w_hbm, b_ref, lg_ref, lb_ref = refs[8:13]
    (xo_ref, uv_ref, vs_hbm, hb_cur, hb_prev, vraw_cur, vraw_prev, stat_prev, sum_s,
     vs_stage, vs_sems, w_s, w_stage, w_sems) = refs[-14:]
    _load_weight_bf16(w_hbm, layer, w_s, w_stage, w_sems)
    i = pl.program_id(0)
    rows = x_ref.shape[0]

    @pl.when(i == 0)
    def _():
        hb_prev[...] = jnp.zeros_like(hb_prev)
        vraw_prev[...] = jnp.zeros_like(vraw_prev)
        stat_prev[...] = jnp.zeros_like(stat_prev)

    def add_row(x, row_ref, lo, width):
        x3 = x.reshape(rows // 8, 8, width) + row_ref[:, lo:lo + width]
        return x3.reshape(rows, width)

    def mul_row(x, row_ref, lo, width):
        x3 = x.reshape(rows // 8, 8, width) * row_ref[:, lo:lo + width]
        return x3.reshape(rows, width)

    def proj_chunk(hb_ref, lo):
        z = jnp.dot(hb_ref[...], w_s[:, lo:lo + GM_TN], preferred_element_type=F32)
        return add_row(z, b_ref, lo, GM_TN).astype(BF16)

    def v_norm_piece(k):
        vk = vraw_prev[:, k:k + 128].astype(F32) * stat_prev[:, 0:128] + stat_prev[:, 128:256]
        return add_row(mul_row(vk, lg_ref, k, 128), lb_ref, k, 128)

    def stage_b_chunk(lo):
        uv_ref[:, lo:lo + GM_TN] = _gelu_tanh(proj_chunk(hb_prev, lo))
        for k in range(lo, lo + GM_TN, 128):
            uv_ref[:, GM_HALF + k:GM_HALF + k + 128] = v_norm_piece(k).astype(BF16)

    def stage_a_chunk(n, lo):
        gz = _gelu_tanh(proj_chunk(hb_cur, GM_HALF + lo))
        vraw_cur[:, lo:lo + GM_TN] = gz
        gf = gz.astype(F32)
        pieces = [gf[:, k:k + 128] for k in range(0, GM_TN, 128)]
        t1 = functools.reduce(lambda p, q: p + q, pieces)
        t2 = functools.reduce(lambda p, q: p + q, [p * p for p in pieces])
        if n == 0:
            sum_s[:, 0:128] = t1
            sum_s[:, 128:256] = t2
        else:
            sum_s[:, 0:128] += t1
            sum_s[:, 128:256] += t2

    chunks = list(range(0, GM_HALF, GM_TN))
    stage_b_chunk(chunks[0])
    a_is_prompt = jnp.minimum(i, n_tiles - 1) < n_prompt_tiles
    x = _add_moe(x_ref[...], yg0_ref, yg1_ref, gates_ref, _tile_mod(refs[4:6], a_is_prompt))
    xo_ref[...] = x
    hb_cur[...] = _norm_mod(x, gain_ref, _tile_mod(refs[6:8], a_is_prompt), 0).astype(BF16)
    for n, lo in enumerate(chunks):
        stage_a_chunk(n, lo)
        if 0 < n < len(chunks) - 1:
            stage_b_chunk(lo)
    mu = jnp.sum(sum_s[:, 0:128], axis=-1, keepdims=True) * (1.0 / GM_HALF)
    var = jnp.sum(sum_s[:, 128:256], axis=-1, keepdims=True) * (1.0 / GM_HALF) - mu * mu
    rstd = lax.rsqrt(var + EPS)
    stage_b_chunk(chunks[-1])

    @pl.when(i - 1 >= n_prompt_tiles)
    def _():
        row0 = pl.multiple_of((i - 1 - n_prompt_tiles) * rows, rows)

        def vs_copy(slot, lay, lo):
            return pltpu.make_async_copy(
                vs_stage.at[slot], vs_hbm.at[lay, pl.ds(row0, rows), pl.ds(lo, GM_TN)], vs_sems.at[slot])

        work = [(layer, lo) for lo in chunks]
        if layer == 0:
            work += [(later, lo) for later in range(1, N_GM) for lo in chunks]
        for n, (lay, lo) in enumerate(work):
            slot = n % 2
            if n >= 2:
                vs_copy(slot, *work[n - 2]).wait()
            if lay == layer:
                for k in range(0, GM_TN, 128):
                    vs_stage[slot, :, k:k + 128] = v_norm_piece(lo + k)
            else:
                vs_stage[slot] = jnp.zeros((rows, GM_TN), F32)
            vs_copy(slot, lay, lo).start()
        for n in range(max(len(work) - 2, 0), len(work)):
            vs_copy(n % 2, *work[n]).wait()

    stat_prev[:, 0:128] = jnp.broadcast_to(rstd, (rows, 128))
    stat_prev[:, 128:256] = jnp.broadcast_to(-mu * rstd, (rows, 128))
    hb_prev[...] = hb_cur[...]
    vraw_prev[...] = vraw_cur[...]


def _gm_proj_call(x, prev, mod, gain, w_in, layer, b_in, ln_g, ln_b, vs_all):
    rows = ROWS_GM_PROJ
    n, npt = _n_tiles(rows), _n_prompt_tiles(rows)

    def stage_a(spec):
        return _stage_a_spec(spec, n)

    stage_b = _stage_b_spec
    in_specs = [stage_a(s) for s in [_row_spec(rows, D)] + _prev_specs(rows) + _mod_specs(rows)] + [
        _resident((1, D)), pl.BlockSpec(memory_space=pl.ANY), _resident((8, GM_FFN)),
        _resident((8, GM_HALF)), _resident((8, GM_HALF))]
    rows8 = [jnp.broadcast_to(r.reshape(1, -1), (8, r.size)) for r in (b_in, ln_g, ln_b)]
    args = [x, *prev, *mod, gain, w_in, *rows8]
    aliases = {}
    if layer > 0:
        in_specs.append(pl.BlockSpec(memory_space=pl.ANY))
        args.append(vs_all)
        aliases = {len(args) - 1: 2}
    return pl.pallas_call(
        functools.partial(_gm_proj_kernel, n, npt, layer),
        grid=(n + 1,),
        in_specs=in_specs,
        out_specs=[stage_a(_row_spec(rows, D)), stage_b(_row_spec(rows, GM_FFN)),
                   pl.BlockSpec(memory_space=pl.ANY)],
        out_shape=[jax.ShapeDtypeStruct((T_ALL, D), F32),
                   jax.ShapeDtypeStruct((T_ALL, GM_FFN), BF16),
                   jax.ShapeDtypeStruct((N_GM, T_SAMPLE, GM_HALF), F32)],
        scratch_shapes=[pltpu.VMEM((rows, D), BF16), pltpu.VMEM((rows, D), BF16),
                        pltpu.VMEM((rows, GM_HALF), BF16), pltpu.VMEM((rows, GM_HALF), BF16),
                        pltpu.VMEM((rows, 256), F32), pltpu.VMEM((rows, 256), F32),
                        pltpu.VMEM((2, rows, GM_TN), F32), pltpu.SemaphoreType.DMA((2,)),
                        *_weight_scratch(D, GM_FFN)],
        input_output_aliases=aliases,
        compiler_params=_params("arbitrary"),
        name="gm_proj",
    )(*args)


def _gm_out_kernel(n_prompt_tiles, layer, uv_ref, ws_ref, bs_ref, w_hbm, bo_ref, x_ref, modp_ref,
                   mods_ref, gain_ref, wr_ref, br_ref, xo_ref, h2_ref, lg_ref, w_s, w_stage, w_sems):
    _load_weight_bf16(w_hbm, layer, w_s, w_stage, w_sems)
    rows = uv_ref.shape[0]
    mod = _tile_mod((modp_ref, mods_ref), pl.program_id(0) < n_prompt_tiles)
    pieces = []
    for r0 in range(0, rows, GM_MIX):
        acc = jnp.zeros((GM_MIX, D), F32)
        for g in range(GM_GROUPS):
            lo = g * GM_GDIM
            sp = jnp.dot(ws_ref[0, g], uv_ref[r0:r0 + GM_MIX, GM_HALF + lo:GM_HALF + lo + GM_GDIM],
                         preferred_element_type=F32) + bs_ref[0, g]
            gated = (uv_ref[r0:r0 + GM_MIX, lo:lo + GM_GDIM].astype(F32) * sp).astype(BF16)
            acc = acc + jnp.dot(gated, w_s[lo:lo + GM_GDIM, :], preferred_element_type=F32)
        pieces.append(acc)
    acc = jnp.concatenate(pieces, axis=0) + bo_ref[...]
    _residual_router(acc, x_ref, mod, gain_ref, wr_ref, br_ref, xo_ref, h2_ref, lg_ref)


def _gm_block_diag(w_s, b_s):
    mats, biases = [], []
    for cl in (GM_CHUNK, DEC_SEQ):
        tri = jnp.tril(jnp.ones((cl, cl), bool))
        blk = jnp.where(tri[None], w_s[:, :cl, :cl], 0.0)
        reps = GM_MIX // cl
        eye = jnp.eye(reps, dtype=w_s.dtype)
        bd = jnp.einsum("ab,gts->gatbs", eye, blk).reshape(GM_GROUPS, GM_MIX, GM_MIX)
        mats.append(bd)
        biases.append(jnp.tile(b_s[:, :cl], (1, reps))[:, :, None])
    return jnp.stack(mats).astype(BF16), jnp.stack(biases).astype(F32)


def _gm_out_call(uv, ws_bd, bs_bd, w_out, layer, b_out, x, mod, gain, w_r, b_r):
    rows = ROWS_WIDE
    npt = _n_prompt_tiles(rows)

    def variant(i):
        return jnp.where(i >= npt, 1, 0)

    return pl.pallas_call(
        functools.partial(_gm_out_kernel, npt, layer),
        grid=(_n_tiles(rows),),
        in_specs=[_row_spec(rows, GM_FFN),
                  pl.BlockSpec((1, GM_GROUPS, GM_MIX, GM_MIX), lambda i: (variant(i), 0, 0, 0)),
                  pl.BlockSpec((1, GM_GROUPS, GM_MIX, 1), lambda i: (variant(i), 0, 0, 0)),
                  pl.BlockSpec(memory_space=pl.ANY), _resident((1, D)), _row_spec(rows, D),
                  *_mod_specs(rows), _resident((1, D))] + _router_specs(),
        out_specs=_mix_out_specs(rows),
        out_shape=_MIX_OUT_SHAPE,
        scratch_shapes=_weight_scratch(GM_HALF, D),
        compiler_params=_params("arbitrary"),
        name="gm_out",
    )(uv, ws_bd, bs_bd, w_out, b_out, x, *mod, gain, w_r, b_r)


def _expert_kernel(layer, be_ref, bv_ref, nb_ref, nx_ref, xb_ref, w1_hbm, w3_hbm, w2_hbm, yb_ref,
                   st1, st3, st2, w1s, w3s, w2s, sems, slot_ref):
    b = pl.program_id(0)

    def weight_copies(e, slot):
        return (pltpu.make_async_copy(w1_hbm.at[layer, e], st1.at[slot], sems.at[slot, 0]),
                pltpu.make_async_copy(w3_hbm.at[layer, e], st3.at[slot], sems.at[slot, 1]),
                pltpu.make_async_copy(w2_hbm.at[layer, e], st2.at[slot], sems.at[slot, 2]))

    @pl.when(b == 0)
    def _():
        slot_ref[0] = 0
        for cp in weight_copies(be_ref[0], 0):
            cp.start()

    @pl.when(b < nb_ref[0])
    def _():
        prev_e = be_ref[jnp.maximum(b - 1, 0)]

        @pl.when((b == 0) | (be_ref[b] != prev_e))
        def _():
            slot = slot_ref[0]
            for cp in weight_copies(be_ref[b], slot):
                cp.wait()
            w1s[...] = st1[slot].astype(BF16)
            w3s[...] = st3[slot].astype(BF16)
            w2s[...] = st2[slot].astype(BF16)

            @pl.when(nx_ref[b] >= 0)
            def _():
                for cp in weight_copies(nx_ref[b], 1 - slot):
                    cp.start()

            slot_ref[0] = 1 - slot

        valid = bv_ref[b]

        def run_rows(n):
            row = lax.broadcasted_iota(jnp.int32, (n, 1), 0)
            xw = jnp.where(row < valid, xb_ref[0:n], U32(0))
            x = _unpack_bf16_pairs(xw).astype(BF16)
            a = jnp.dot(x, w1s[...], preferred_element_type=F32)
            c = jnp.dot(x, w3s[...], preferred_element_type=F32)
            h = (_silu(a) * c).astype(BF16)
            yb_ref[0:n] = _pack_bf16_pairs(jnp.dot(h, w2s[...], preferred_element_type=F32))
            if n < EXP_BLOCK:
                yb_ref[n:EXP_BLOCK] = jnp.zeros((EXP_BLOCK - n, DP), U32)

        for n in range(EXP_ROW_STEP, EXP_BLOCK + 1, EXP_ROW_STEP):
            pl.when((valid > n - EXP_ROW_STEP) & (valid <= n))(functools.partial(run_rows, n))


def _expert_call(blk_e, blk_valid, n_blk, blk_next, xb, w1, w3, w2, layer):
    def blk(b, be, bv, nb, nx):
        return (jnp.minimum(b, nb[0] - 1), 0)

    up, down = (D, MOE_HIDDEN), (MOE_HIDDEN, D)
    grid_spec = pltpu.PrefetchScalarGridSpec(
        num_scalar_prefetch=4,
        grid=(N_EXP_BLOCKS,),
        in_specs=[pl.BlockSpec((EXP_BLOCK, DP), blk)] + [pl.BlockSpec(memory_space=pl.ANY)] * 3,
        out_specs=pl.BlockSpec((EXP_BLOCK, DP), blk),
        scratch_shapes=[pltpu.VMEM((2,) + up, F32), pltpu.VMEM((2,) + up, F32), pltpu.VMEM((2,) + down, F32),
                        pltpu.VMEM(up, BF16), pltpu.VMEM(up, BF16), pltpu.VMEM(down, BF16),
                        pltpu.SemaphoreType.DMA((2, 3)), pltpu.SMEM((1,), jnp.int32)],
    )
    return pl.pallas_call(
        functools.partial(_expert_kernel, layer),
        grid_spec=grid_spec,
        out_shape=jax.ShapeDtypeStruct((P_ROWS, DP), U32),
        compiler_params=_params("arbitrary"),
        name="experts",
    )(blk_e, blk_valid, n_blk, blk_next, xb, w1, w3, w2)


def _route_kernel(lg_ref, dest_ref, gates_ref, meta_ref, cnt_ref):
    t = pl.program_id(0)

    @pl.when(t == 0)
    def _():
        cnt_ref[...] = jnp.zeros_like(cnt_ref)

    pl.when(t < N_ROUTE_TILES)(functools.partial(_route_tile, t, lg_ref, dest_ref, gates_ref, cnt_ref))
    pl.when(t == N_ROUTE_TILES)(functools.partial(_route_finish, dest_ref, meta_ref, cnt_ref))


ROUTE_RANK_BITS = 16
assert P_ROWS < 2 ** ROUTE_RANK_BITS


def _route_tile(t, lg_ref, dest_ref, gates_ref, cnt_ref):
    tm = ROUTE_TM
    lt = lg_ref[...].T
    el = lt[0:MOE_EXPERTS]
    gl = lt[MOE_EXPERTS:MOE_EXPERTS + 8]
    gidx = lax.broadcasted_iota(jnp.int32, (8, tm), 0)
    neg = jnp.float32(-jnp.inf)
    gl = jnp.where(gidx < MOE_GROUPS, gl, neg)
    gmax = jnp.max(gl, axis=0, keepdims=True)
    grp = jnp.min(jnp.where(gl == gmax, gidx, MOE_GROUPS), axis=0, keepdims=True)
    eidx = lax.broadcasted_iota(jnp.int32, (MOE_EXPERTS, tm), 0)
    els = jnp.where((eidx >> 3) == grp, el, neg)
    m1 = jnp.max(els, axis=0, keepdims=True)
    i1 = jnp.min(jnp.where(els == m1, eidx, MOE_EXPERTS), axis=0, keepdims=True)
    els2 = jnp.where(eidx == i1, neg, els)
    m2 = jnp.max(els2, axis=0, keepdims=True)
    i2 = jnp.min(jnp.where(els2 == m2, eidx, MOE_EXPERTS), axis=0, keepdims=True)
    sel1 = eidx == i1
    sel2 = eidx == i2
    cnt = jnp.where(sel1 | sel2, 1.0, 0.0)

    lane = ROUTER_LANES
    before = (lax.broadcasted_iota(jnp.int32, (lane, lane), 0)
              < lax.broadcasted_iota(jnp.int32, (lane, lane), 1))
    tri = jnp.where(before, 1.0, 0.0).astype(BF16)
    run = cnt_ref[...]
    r1, r2 = [], []
    for k in range(tm // lane):
        piece = slice(k * lane, (k + 1) * lane)
        ck = cnt[:, piece]
        pos = run + jnp.dot(ck.astype(BF16), tri, preferred_element_type=F32)
        r1.append(jnp.sum(jnp.where(sel1[:, piece], pos, 0.0), axis=0, keepdims=True))
        r2.append(jnp.sum(jnp.where(sel2[:, piece], pos, 0.0), axis=0, keepdims=True))
        run = run + jnp.sum(ck, axis=1, keepdims=True)
    cnt_ref[...] = run
    rank = jnp.concatenate([jnp.concatenate(r1, axis=1), jnp.concatenate(r2, axis=1)], axis=0)
    eid = jnp.concatenate([i1, i2], axis=0)
    dest_ref[:, pl.ds(pl.multiple_of(t * tm, tm), tm)] = (eid << ROUTE_RANK_BITS) + rank.astype(jnp.int32)

    g_w = 1.0 / jnp.sum(jnp.exp(gl - gmax), axis=0, keepdims=True)
    e21 = jnp.exp(m2 - m1)
    p1 = 1.0 / (1.0 + e21)
    rid = lax.broadcasted_iota(jnp.int32, (ROUTER_LANES, tm), 0)
    gt = jnp.where(rid == 0, g_w * p1, jnp.where(rid == 1, g_w * (e21 * p1), 0.0))
    gates_ref[...] = gt.T


def _route_finish(dest_ref, meta_ref, cnt_ref):
    counts = cnt_ref[...]
    nblk = jnp.floor((counts + (EXP_BLOCK - 1.0)) * (1.0 / EXP_BLOCK))
    r = lax.broadcasted_iota(jnp.int32, (MOE_EXPERTS, MOE_EXPERTS), 0)
    c = lax.broadcasted_iota(jnp.int32, (MOE_EXPERTS, MOE_EXPERTS), 1)
    nblk_row = jnp.sum(jnp.where(r == c, nblk, 0.0), axis=0, keepdims=True)
    bstart = jnp.sum(jnp.where(c < r, nblk_row, 0.0), axis=1, keepdims=True)
    bend = bstart + nblk
    bidx = lax.broadcasted_iota(jnp.int32, (1, META_LANES), 1).astype(F32)
    blk_e = jnp.minimum(jnp.sum(jnp.where(bidx >= bend, 1.0, 0.0), axis=0, keepdims=True),
                        MOE_EXPERTS - 1.0)
    erow = lax.broadcasted_iota(jnp.int32, (MOE_EXPERTS, META_LANES), 0).astype(F32)
    mine = erow == blk_e
    cnt_b = jnp.sum(jnp.where(mine, counts, 0.0), axis=0, keepdims=True)
    start_b = jnp.sum(jnp.where(mine, bstart, 0.0), axis=0, keepdims=True)
    valid = jnp.clip(cnt_b - (bidx - start_b) * EXP_BLOCK, 0.0, float(EXP_BLOCK))
    n_blk = jnp.sum(nblk, axis=0, keepdims=True)
    end_b = jnp.sum(jnp.where(mine, bend, 0.0), axis=0, keepdims=True)
    nxt = jnp.minimum(jnp.sum(jnp.where(end_b >= bend, 1.0, 0.0), axis=0, keepdims=True),
                      MOE_EXPERTS - 1.0)
    nxt = jnp.where(end_b < n_blk, nxt, -1.0)
    mrow = lax.broadcasted_iota(jnp.int32, (8, META_LANES), 0)
    meta = jnp.where(mrow == 0, blk_e, jnp.where(mrow == 1, valid, jnp.where(
        mrow == 2, n_blk, jnp.where(mrow == 3, nxt, 0.0))))
    meta_ref[...] = meta.astype(jnp.int32)

    base = (bstart * EXP_BLOCK).astype(jnp.int32)
    packed = dest_ref[...]
    eid = packed >> ROUTE_RANK_BITS
    row = packed & (2 ** ROUTE_RANK_BITS - 1)
    for e in range(MOE_EXPERTS):
        row = row + jnp.where(eid == e, base[e:e + 1, :], 0)
    dest_ref[...] = row


def _route_call(logits):
    last = N_ROUTE_TILES - 1
    return pl.pallas_call(
        _route_kernel,
        grid=(N_ROUTE_TILES + 1,),
        in_specs=[pl.BlockSpec((ROUTE_TM, ROUTER_LANES), lambda t: (jnp.minimum(t, last), 0))],
        out_specs=[pl.BlockSpec((MOE_TOPK, T_ALL), lambda t: (0, 0)),
                   pl.BlockSpec((ROUTE_TM, ROUTER_LANES), lambda t: (jnp.minimum(t, last), 0)),
                   pl.BlockSpec((8, META_LANES), lambda t: (0, 0))],
        out_shape=[jax.ShapeDtypeStruct((MOE_TOPK, T_ALL), jnp.int32),
                   jax.ShapeDtypeStruct((T_ALL, ROUTER_LANES), F32),
                   jax.ShapeDtypeStruct((8, META_LANES), jnp.int32)],
        scratch_shapes=[pltpu.VMEM((MOE_EXPERTS, 1), F32)],
        compiler_params=_params("arbitrary"),
        name="route",
    )(logits)


def _sc_mesh():
    return plsc.VectorSubcoreMesh(core_axis_name="c", subcore_axis_name="s")


def _sc_token_offset(j):
    wid = lax.axis_index("s") * SC_CORES + lax.axis_index("c")
    return pl.multiple_of(wid * SC_ROWS_PER_WORKER + j * SC_CHUNK, 8)


SC_N_CHUNKS = SC_ROWS_PER_WORKER // SC_CHUNK
assert SC_N_CHUNKS % 2 == 1


def _start(copies):
    for cp in copies:
        cp.start()


def _wait(copies):
    for cp in copies:
        cp.wait()


def _dispatch_body(h_hbm, d0_hbm, d1_hbm, out_hbm, i0a, i1a, rows_a, i0b, i1b, rows_b, la, lb, sa, sb):
    sets = {"a": (i0a, i1a, rows_a, la, sa), "b": (i0b, i1b, rows_b, lb, sb)}

    def loads(j, s):
        i0, i1, rows, lsem, _ = sets[s]
        src = pl.ds(_sc_token_offset(j), SC_CHUNK)
        return (pltpu.make_async_copy(d0_hbm.at[src], i0, lsem),
                pltpu.make_async_copy(d1_hbm.at[src], i1, lsem),
                pltpu.make_async_copy(h_hbm.at[src], rows, lsem))

    def scatters(s):
        i0, i1, rows, _, ssem = sets[s]
        return (pltpu.make_async_copy(rows, out_hbm.at[i0], ssem),
                pltpu.make_async_copy(rows, out_hbm.at[i1], ssem))

    _start(loads(0, "a"))

    @pl.loop(0, SC_N_CHUNKS // 2)
    def _(p):
        a = 2 * p
        _start(loads(a + 1, "b"))
        _wait(loads(a, "a"))
        _start(scatters("a"))
        _wait(loads(a + 1, "b"))
        _start(scatters("b"))
        _wait(scatters("a"))
        _start(loads(a + 2, "a"))
        _wait(scatters("b"))

    _wait(loads(SC_N_CHUNKS - 1, "a"))
    _start(scatters("a"))
    _wait(scatters("a"))


def _dispatch_call(h2, dest0, dest1):
    buffers = [pltpu.VMEM((SC_CHUNK,), jnp.int32), pltpu.VMEM((SC_CHUNK,), jnp.int32),
               pltpu.VMEM((SC_CHUNK, DP), U32)]
    return pl.kernel(
        _dispatch_body,
        out_type=jax.ShapeDtypeStruct((P_ROWS, DP), U32),
        mesh=_sc_mesh(),
        scratch_types=buffers + buffers + [pltpu.SemaphoreType.DMA] * 4,
        name="moe_dispatch",
    )(h2, dest0, dest1)


def _combine_body(yb_hbm, d0_hbm, d1_hbm, o0_hbm, o1_hbm, i0a, i1a, r0a, r1a, i0b, i1b, r0b, r1b,
                  ia, ib, ga, gb, wa, wb):
    sets = {"a": (i0a, i1a, r0a, r1a, ia, ga, wa), "b": (i0b, i1b, r0b, r1b, ib, gb, wb)}

    def index_loads(j, s):
        i0, i1, _, _, isem, _, _ = sets[s]
        src = pl.ds(_sc_token_offset(j), SC_CHUNK)
        return (pltpu.make_async_copy(d0_hbm.at[src], i0, isem), pltpu.make_async_copy(d1_hbm.at[src], i1, isem))

    def gathers(s):
        i0, i1, r0, r1, _, gsem, _ = sets[s]
        return (pltpu.make_async_copy(yb_hbm.at[i0], r0, gsem), pltpu.make_async_copy(yb_hbm.at[i1], r1, gsem))

    def writes(j, s):
        _, _, r0, r1, _, _, wsem = sets[s]
        dst = pl.ds(_sc_token_offset(j), SC_CHUNK)
        return (pltpu.make_async_copy(r0, o0_hbm.at[dst], wsem), pltpu.make_async_copy(r1, o1_hbm.at[dst], wsem))

    _start(index_loads(0, "a"))
    _wait(index_loads(0, "a"))
    _start(gathers("a"))

    @pl.loop(0, SC_N_CHUNKS // 2)
    def _(p):
        a = 2 * p
        _start(index_loads(a + 1, "b"))
        _wait(index_loads(a + 1, "b"))
        _wait(gathers("a"))
        _start(writes(a, "a"))
        _start(gathers("b"))
        _start(index_loads(a + 2, "a"))
        _wait(index_loads(a + 2, "a"))
        _wait(writes(a, "a"))
        _wait(gathers("b"))
        _start(writes(a + 1, "b"))
        _start(gathers("a"))
        _wait(writes(a + 1, "b"))

    _wait(gathers("a"))
    _start(writes(SC_N_CHUNKS - 1, "a"))
    _wait(writes(SC_N_CHUNKS - 1, "a"))


def _combine_call(yb, dest0, dest1):
    out = jax.ShapeDtypeStruct((T_ALL, DP), U32)
    buffers = [pltpu.VMEM((SC_CHUNK,), jnp.int32), pltpu.VMEM((SC_CHUNK,), jnp.int32),
               pltpu.VMEM((SC_CHUNK, DP), U32), pltpu.VMEM((SC_CHUNK, DP), U32)]
    return pl.kernel(
        _combine_body,
        out_type=(out, out),
        mesh=_sc_mesh(),
        scratch_types=buffers + buffers + [pltpu.SemaphoreType.DMA] * 6,
        name="moe_combine",
    )(yb, dest0, dest1)


def _moe_rows(h2, dest, meta, w1, w3, w2, layer):
    dest0, dest1 = dest[0], dest[1]
    xb = _dispatch_call(h2, dest0, dest1)
    yb = _expert_call(meta[0, :N_EXP_BLOCKS], meta[1, :N_EXP_BLOCKS], meta[2, :1], meta[3, :N_EXP_BLOCKS],
                      xb, w1, w3, w2, layer)
    return _combine_call(yb, dest0, dest1)


def _final_kernel(is_prompt, x_ref, yg0_ref, yg1_ref, gates_ref, modp_ref, mods_ref, gain_ref, o_ref):
    x = _add_moe(x_ref[...], yg0_ref, yg1_ref, gates_ref, _tile_mod((modp_ref, mods_ref), is_prompt))
    o_ref[...] = _rms(x) * gain_ref[...]


def _final_call(x, prev, gain, row0, n_rows):
    rows = ROWS_WIDE
    tile0 = row0 // rows
    is_prompt = row0 < T_PROMPT
    assert row0 + n_rows <= T_PROMPT or not is_prompt

    def tile(width):
        return pl.BlockSpec((rows, width), lambda i: (tile0 + i, 0))

    mod_specs = [pl.BlockSpec(s.block_shape, lambda i, m=s.index_map: m(tile0 + i)) for s in _mod_specs(rows)]
    return pl.pallas_call(
        functools.partial(_final_kernel, is_prompt),
        grid=(n_rows // rows,),
        in_specs=[tile(D), tile(DP), tile(DP), tile(ROUTER_LANES), *mod_specs, _resident((1, D))],
        out_specs=pl.BlockSpec((rows, D), lambda i: (i, 0)),
        out_shape=jax.ShapeDtypeStruct((n_rows, D), F32),
        compiler_params=_params("parallel"),
        name="final_norm",
    )(x, *prev, gain)


def _rope_table():
    pos = np.concatenate([np.tile(np.arange(SEQ), BATCH),
                          np.tile(PAST_LEN + np.arange(DEC_SEQ), DEC_BATCH)]).astype(np.float32)
    inv = (ROPE_BASE ** (-np.arange(ROPE_HALF, dtype=np.float32) / ROPE_HALF)).astype(np.float32)
    ang = (pos[:, None] * inv[None, :]).astype(np.float32).astype(np.float64)
    pad = np.zeros((T_ALL, ROPE_TABLE_LANES - 2 * ROPE_HALF - 2 * RET_HEADS))
    return jnp.asarray(np.concatenate([np.cos(ang), np.sin(ang), _ret_row_scales(), pad], axis=1), F32)


def kernel(x_prompt, x_sample, c_prompt, c_sample, state_ret, ada_w, ada_b, norm1_g, norm2_g, ret_w_in,
           ret_w_out, gm_w_in, gm_b_in, gm_ln_g, gm_ln_b, gm_w_s, gm_b_s, gm_w_out, gm_b_out, moe_w_rg,
           moe_b_rg, moe_w_re, moe_b_re, moe_w1, moe_w3, moe_w2, final_g):
    x = (x_prompt.reshape(T_PROMPT, D), x_sample.reshape(T_SAMPLE, D))
    c_all = jnp.concatenate([c_prompt, c_sample], axis=0)
    rope = _rope_table()

    mod_all = _ada_call(c_all, ada_w, ada_b).reshape(DEPTH, N_SEQ, 6, D)

    def layer_params(i):
        mod = (mod_all[i, :BATCH], mod_all[i, BATCH:])
        w_r = jnp.pad(jnp.concatenate([moe_w_re[i], moe_w_rg[i]], axis=1),
                      ((0, 0), (0, ROUTER_LANES - MOE_GROUPS - MOE_EXPERTS)))
        w_r_hi = w_r.astype(BF16)
        w_r_lo = (w_r - w_r_hi.astype(F32)).astype(BF16)
        w_r = jnp.concatenate([w_r_hi, w_r_lo], axis=1)
        b_r = jnp.pad(jnp.concatenate([moe_b_re[i].reshape(-1), moe_b_rg[i]]),
                      (0, ROUTER_LANES - MOE_GROUPS - MOE_EXPERTS)).reshape(1, ROUTER_LANES)
        return mod, w_r, b_r

    ret_prompt = ret_sample = gm_sample = None
    prev = None
    for i in range(DEPTH):
        j = i // 2
        mod, w_r, b_r = layer_params(i)
        g1 = norm1_g[i].reshape(1, D)
        g2 = norm2_g[i].reshape(1, D)
        if i % 2 == 0:
            x, p = _ret_proj_call(x, prev, mod, g1, ret_w_in, j, rope)
            y_p, ret_prompt = _ret_core_call(p, None, ret_prompt, j, BATCH, SEQ, RET_CHUNK_PROMPT, 0)
            y_s, ret_sample = _ret_core_call(p, state_ret, ret_sample, j, DEC_BATCH, DEC_SEQ,
                                             RET_CHUNK_SAMPLE, T_PROMPT)
            x, h2, logits = _ret_out_call(y_p, y_s, ret_w_out, j, x, mod, g2, w_r, b_r)
        else:
            x, uv, gm_sample = _gm_proj_call(x, prev, mod, g1, gm_w_in, j, gm_b_in[j], gm_ln_g[j],
                                             gm_ln_b[j], gm_sample)
            ws_bd, bs_bd = _gm_block_diag(gm_w_s[j], gm_b_s[j])
            x, h2, logits = _gm_out_call(uv, ws_bd, bs_bd, gm_w_out, j,
                                         gm_b_out[j].reshape(1, D), x, mod, g2, w_r, b_r)
        dest, gates, meta = _route_call(logits)
        yg0, yg1 = _moe_rows(h2, dest, meta, moe_w1, moe_w3, moe_w2, i)
        prev = (yg0, yg1, gates, *mod)

    fg = final_g.reshape(1, D)
    y_prompt = _final_call(x, prev, fg, 0, T_PROMPT).reshape(BATCH, SEQ, D)
    y_sample = _final_call(x, prev, fg, T_PROMPT, T_SAMPLE).reshape(DEC_BATCH, DEC_SEQ, D)
    return (y_prompt, y_sample, ret_prompt, ret_sample,
            gm_sample.reshape(N_GM, DEC_BATCH, DEC_SEQ, GM_HALF))
```

```python
import functools

import numpy as np
import jax
import jax.numpy as jnp
from jax import lax
from jax.experimental import pallas as pl
from jax.experimental.pallas import tpu as pltpu
from jax.experimental.pallas import tpu_sc as plsc

F32 = jnp.float32
BF16 = jnp.bfloat16
U32 = jnp.uint32

D = 1024
BATCH, SEQ = 4, 4096
DEC_BATCH, DEC_SEQ = 16, 64
PAST_LEN = 4096
DEPTH = 4
N_RET = (DEPTH + 1) // 2
N_GM = DEPTH // 2
N_SEQ = BATCH + DEC_BATCH

RET_HEADS, RET_DK, RET_DV = 4, 256, 512
RET_QK = RET_HEADS * RET_DK
RET_V = RET_HEADS * RET_DV
RET_IN = 2 * RET_QK + 2 * RET_V
ROPE_BASE = 10000.0
ROPE_HALF = RET_DK // 2
ROPE_TABLE_LANES = 3 * ROPE_HALF

GM_FFN = 6 * D
GM_HALF = GM_FFN // 2
GM_GROUPS = 4
GM_GDIM = GM_HALF // GM_GROUPS
GM_CHUNK = 128

MOE_GROUPS, MOE_PER_GROUP = 4, 8
MOE_EXPERTS = MOE_GROUPS * MOE_PER_GROUP
MOE_TOPK = 2
MOE_HIDDEN = 512
EPS = 1e-6

GROUP = DEC_SEQ
T_PROMPT = BATCH * SEQ
T_SAMPLE = DEC_BATCH * DEC_SEQ
T_ALL = T_PROMPT + T_SAMPLE
N_GROUPS = T_ALL // GROUP
ROWS_WIDE = 512
ROWS_GM_PROJ = 256
ROWS_FINAL = 1024

RET_CHUNK_PROMPT = 256
RET_CHUNK_SAMPLE = DEC_SEQ

GM_MIX = 256

EXP_BLOCK = 1024
EXP_ROW_STEP = 128
N_ASSIGN = T_ALL * MOE_TOPK
N_EXP_BLOCKS = -(-(N_ASSIGN + MOE_EXPERTS * (EXP_BLOCK - 1)) // EXP_BLOCK)
P_ROWS = N_EXP_BLOCKS * EXP_BLOCK
ROUTER_LANES = 128
ROUTE_TM = 1024
N_ROUTE_TILES = T_ALL // ROUTE_TM
META_LANES = 256
assert META_LANES >= N_EXP_BLOCKS

DP = D // 2
SC_CORES, SC_SUBCORES = 2, 16
SC_WORKERS = SC_CORES * SC_SUBCORES
SC_ROWS_PER_WORKER = T_ALL // SC_WORKERS
SC_CHUNK = 32
assert SC_ROWS_PER_WORKER % SC_CHUNK == 0 and SC_CHUNK % 8 == 0

V7X_VMEM_LIMIT_BYTES = 56 * 1024 * 1024


def _params(*sem):
    return pltpu.CompilerParams(dimension_semantics=sem, vmem_limit_bytes=V7X_VMEM_LIMIT_BYTES)


def _resident(shape):
    nd = len(shape)
    return pl.BlockSpec(shape, lambda *_: (0,) * nd, pipeline_mode=pl.Buffered(1))


WEIGHT_STAGE_BYTES = 3 * 1024 * 1024


def _weight_scratch(k, n):
    chunk = k
    while chunk * n * 4 > WEIGHT_STAGE_BYTES:
        assert chunk % 16 == 0
        chunk //= 2
    return [pltpu.VMEM((k, n), BF16), pltpu.VMEM((2, chunk, n), F32), pltpu.SemaphoreType.DMA((2,))]


def _load_weight_bf16(w_hbm, layer, w_s, stage, sems):
    k = w_s.shape[0]
    chunk = stage.shape[1]

    def copy(c):
        return pltpu.make_async_copy(w_hbm.at[layer, pl.ds(c * chunk, chunk)], stage.at[c % 2], sems.at[c % 2])

    @pl.when(pl.program_id(0) == 0)
    def _():
        copy(0).start()
        for c in range(k // chunk):
            if c + 1 < k // chunk:
                copy(c + 1).start()
            copy(c).wait()
            w_s[c * chunk:(c + 1) * chunk, :] = stage[c % 2].astype(BF16)


def _rms(x):
    return x * lax.rsqrt(jnp.mean(x * x, axis=-1, keepdims=True) + EPS)


def _silu(x):
    return x * jax.nn.sigmoid(x)


def _per_group(x2d, fn):
    rows = x2d.shape[0]
    return fn(x2d.reshape(rows // GROUP, GROUP, D)).reshape(rows, D)


def _tile_mod(mod_refs, is_prompt):
    modp_ref, mods_ref = mod_refs
    return jnp.where(is_prompt, jnp.broadcast_to(modp_ref[...], mods_ref.shape), mods_ref[...])


def _norm_mod(x, gain_ref, mod, shift_idx):
    y = _rms(x) * gain_ref[...]
    scale = mod[:, shift_idx + 1:shift_idx + 2, :]
    shift = mod[:, shift_idx:shift_idx + 1, :]
    return _per_group(y, lambda y3: y3 * (1.0 + scale) + shift)


def _pack_bf16_pairs(x):
    lo = lax.bitcast_convert_type(x[:, :DP].astype(BF16).astype(F32), U32)
    hi = lax.bitcast_convert_type(x[:, DP:].astype(BF16).astype(F32), U32)
    return (lo >> 16) | (hi & U32(0xFFFF0000))


def _unpack_bf16_pairs(w):
    lo = lax.bitcast_convert_type(w << 16, F32)
    hi = lax.bitcast_convert_type(w & U32(0xFFFF0000), F32)
    return jnp.concatenate([lo, hi], axis=1)


def _add_moe(x, yg0_ref, yg1_ref, gates_ref, mod_prev):
    g = gates_ref[...]
    y = g[:, 0:1] * _unpack_bf16_pairs(yg0_ref[...]) + g[:, 1:2] * _unpack_bf16_pairs(yg1_ref[...])
    gate2 = mod_prev[:, 5:6, :]
    return x + _per_group(y, lambda y3: y3 * gate2)


ADA_TN = 1536


def _ada_kernel(c_ref, w_ref, b_ref, o_ref):
    c = c_ref[...]
    s = _silu(c).astype(BF16)
    o_ref[0] = jnp.dot(s, w_ref[0].astype(BF16), preferred_element_type=F32) + b_ref[0]


def _ada_call(c_all, ada_w, ada_b):
    return pl.pallas_call(
        _ada_kernel,
        grid=(DEPTH, 6 * D // ADA_TN),
        in_specs=[
            pl.BlockSpec((N_SEQ, D), lambda i, j: (0, 0)),
            pl.BlockSpec((1, D, ADA_TN), lambda i, j: (i, 0, j)),
            pl.BlockSpec((1, 1, ADA_TN), lambda i, j: (i, 0, j)),
        ],
        out_specs=pl.BlockSpec((1, N_SEQ, ADA_TN), lambda i, j: (i, 0, j)),
        out_shape=jax.ShapeDtypeStruct((DEPTH, N_SEQ, 6 * D), F32),
        compiler_params=_params("parallel", "parallel"),
        name="ada_modulation",
    )(c_all, ada_w, ada_b.reshape(DEPTH, 1, 6 * D))


def _n_tiles(rows):
    return T_ALL // rows


def _n_prompt_tiles(rows):
    return T_PROMPT // rows


def _row_spec(rows, width):
    return pl.BlockSpec((rows, width), lambda i: (i, 0))


def _mod_specs(rows):
    npt = _n_prompt_tiles(rows)
    return [pl.BlockSpec((1, 6, D), lambda i: (jnp.minimum(i * rows // SEQ, BATCH - 1), 0, 0)),
            pl.BlockSpec((rows // DEC_SEQ, 6, D), lambda i: (jnp.maximum(i - npt, 0), 0, 0))]


def _prev_specs(rows):
    return [_row_spec(rows, DP), _row_spec(rows, DP), _row_spec(rows, ROUTER_LANES)] + _mod_specs(rows)


def _stage_a_spec(spec, n_tiles):
    return pl.BlockSpec(spec.block_shape, lambda i, m=spec.index_map: m(jnp.minimum(i, n_tiles - 1)))


def _stage_b_spec(spec):
    return pl.BlockSpec(spec.block_shape, lambda i, m=spec.index_map: m(jnp.maximum(i - 1, 0)))


def _prompt_rows_spec(rows, width):
    last = _n_prompt_tiles(rows) - 1
    return pl.BlockSpec((rows, width), lambda i: (jnp.minimum(i, last), 0))


def _sample_rows_spec(rows, width):
    npt = _n_prompt_tiles(rows)
    return pl.BlockSpec((rows, width), lambda i: (jnp.maximum(i - npt, 0), 0))


def _ret_proj_kernel(has_prev, n_tiles, n_prompt_tiles, layer, *refs):
    i = pl.program_id(0)
    is_prompt = jnp.minimum(i, n_tiles - 1) < n_prompt_tiles
    head = refs[:6] if has_prev else refs[:2]
    refs = refs[len(head):]
    mod_refs = refs[:2]
    gain_ref, w_hbm, rope_ref, xo_ref, p_ref, hb_cur, hb_prev, w_s, w_stage, w_sems = refs[2:]
    _load_weight_bf16(w_hbm, layer, w_s, w_stage, w_sems)

    @pl.when(i == 0)
    def _():
        hb_prev[...] = jnp.zeros_like(hb_prev)

    def qk_head(j):
        lo = j * RET_DK
        acc = jnp.dot(hb_prev[...], w_s[:, lo:lo + RET_DK], preferred_element_type=F32)
        x1 = acc[:, :ROPE_HALF]
        x2 = acc[:, ROPE_HALF:]
        cos = rope_ref[:, 0:ROPE_HALF]
        sin = rope_ref[:, ROPE_HALF:2 * ROPE_HALF]
        scale = rope_ref[:, 2 * ROPE_HALF + j:2 * ROPE_HALF + j + 1]
        p_ref[:, lo:lo + ROPE_HALF] = ((x1 * cos - x2 * sin) * scale).astype(BF16)
        p_ref[:, lo + ROPE_HALF:lo + RET_DK] = ((x1 * sin + x2 * cos) * scale).astype(BF16)

    def vg_head(j):
        lo = 2 * RET_QK + j * RET_DV
        acc = jnp.dot(hb_prev[...], w_s[:, lo:lo + RET_DV], preferred_element_type=F32)
        if j >= RET_HEADS:
            acc = _silu(acc)
        p_ref[:, lo:lo + RET_DV] = acc.astype(BF16)

    vg_head(0)
    vg_head(1)
    if has_prev:
        x_ref, yg0_ref, yg1_ref, gates_ref = head[:4]
        x = _add_moe(x_ref[...], yg0_ref, yg1_ref, gates_ref, _tile_mod(head[4:6], is_prompt))
    else:
        x = jnp.where(is_prompt, head[0][...], head[1][...])
    xo_ref[...] = x
    hb_cur[...] = _norm_mod(x, gain_ref, _tile_mod(mod_refs, is_prompt), 0).astype(BF16)
    for j in range(2, 2 * RET_HEADS):
        vg_head(j)
    for j in range(2 * RET_HEADS):
        qk_head(j)
    hb_prev[...] = hb_cur[...]


def _ret_proj_call(x, prev, mod, gain, w_in, layer, rope):
    rows = ROWS_WIDE
    n = _n_tiles(rows)
    has_prev = prev is not None
    if has_prev:
        in_specs = [_row_spec(rows, D)] + _prev_specs(rows)
        args = [x] + list(prev)
    else:
        in_specs = [_prompt_rows_spec(rows, D), _sample_rows_spec(rows, D)]
        args = list(x)
    in_specs = [_stage_a_spec(s, n) for s in in_specs + _mod_specs(rows)] + [
        _resident((1, D)), pl.BlockSpec(memory_space=pl.ANY),
        _stage_b_spec(_row_spec(rows, ROPE_TABLE_LANES))]
    args += [*mod, gain, w_in, rope]
    return pl.pallas_call(
        functools.partial(_ret_proj_kernel, has_prev, n, _n_prompt_tiles(rows), layer),
        grid=(n + 1,),
        in_specs=in_specs,
        out_specs=[_stage_a_spec(_row_spec(rows, D), n), _stage_b_spec(_row_spec(rows, RET_IN))],
        out_shape=[jax.ShapeDtypeStruct((T_ALL, D), F32), jax.ShapeDtypeStruct((T_ALL, RET_IN), BF16)],
        scratch_shapes=[pltpu.VMEM((rows, D), BF16), pltpu.VMEM((rows, D), BF16),
                        *_weight_scratch(D, RET_IN)],
        compiler_params=_params("arbitrary"),
        name="ret_proj",
    )(*args)


def _ret_core_kernel(has_s0, n_chunks, layer, *refs):
    refs = list(refs)
    p_ref = refs.pop(0)
    s0_ref = refs.pop(0) if has_s0 else None
    causal_ref, cd_ref = refs[:2]
    y_ref, so_ref, s_ref = refs[-3:]
    c = pl.program_id(1)

    @pl.when(c == 0)
    def _():
        if has_s0:
            s_ref[...] = s0_ref[0, 0]
        else:
            s_ref[...] = jnp.zeros_like(s_ref)

    for h in range(RET_HEADS):
        qb = p_ref[:, h * RET_DK:(h + 1) * RET_DK]
        kb = p_ref[:, RET_QK + h * RET_DK:RET_QK + (h + 1) * RET_DK]
        vb = p_ref[:, 2 * RET_QK + h * RET_DV:2 * RET_QK + (h + 1) * RET_DV]
        gb = p_ref[:, 2 * RET_QK + RET_V + h * RET_DV:2 * RET_QK + RET_V + (h + 1) * RET_DV]
        scores = lax.dot_general(qb, kb, (((1,), (1,)), ((), ())), preferred_element_type=F32)
        scores = scores * causal_ref[...]
        s_old = s_ref[h]
        o = (jnp.dot(scores.astype(BF16), vb, preferred_element_type=F32)
             + jnp.dot(qb, s_old.astype(BF16), preferred_element_type=F32))
        s_ref[h] = cd_ref[h][:, 0:1] * (s_old + lax.dot_general(
            kb, vb, (((0,), (0,)), ((), ())), preferred_element_type=F32))
        y_ref[:, h * RET_DV:(h + 1) * RET_DV] = (gb.astype(F32) * _rms(o)).astype(BF16)

    @pl.when(c == n_chunks - 1)
    def _():
        so_ref[0, 0] = s_ref[...]
        if layer == 0:
            for later in range(1, N_RET):
                so_ref[later, 0] = jnp.zeros_like(s_ref)


def _ret_log_gamma():
    return np.log1p(-np.exp2(-5.0 - np.arange(RET_HEADS, dtype=np.float64)))


def _ret_chunk_tables(cl):
    idx = np.arange(cl)
    causal = (idx[:, None] >= idx[None, :]).astype(np.float32)
    cd = np.broadcast_to(np.exp(_ret_log_gamma() * cl)[:, None, None], (RET_HEADS, 1, 128))
    return jnp.asarray(causal, F32), jnp.asarray(cd, F32)


def _ret_row_scales():
    c = np.concatenate([np.arange(T_PROMPT) % RET_CHUNK_PROMPT,
                        np.arange(T_SAMPLE) % RET_CHUNK_SAMPLE]).astype(np.float64)
    e = (c[:, None] + 1.0) * _ret_log_gamma()[None, :]
    return np.concatenate([np.exp(e), np.exp(-e) * RET_DK ** -0.5], axis=1)


def _ret_core_call(p, s0, states, layer, n_seq, seq_len, cl, row0):
    has_s0 = s0 is not None
    n_chunks = seq_len // cl
    rb0 = row0 // cl
    state = (RET_HEADS, RET_DK, RET_DV)
    in_specs = [pl.BlockSpec((cl, RET_IN), lambda b, c: (rb0 + b * n_chunks + c, 0))]
    args = [p]
    if has_s0:
        in_specs.append(pl.BlockSpec((1, 1) + state, lambda b, c: (layer, b, 0, 0, 0)))
        args.append(s0)
    in_specs += [_resident((cl, cl)), _resident((RET_HEADS, 1, 128))]
    args += list(_ret_chunk_tables(cl))
    if layer == 0:
        assert states is None
        state_spec = pl.BlockSpec((N_RET, 1) + state, lambda b, c: (0, b, 0, 0, 0))
        aliases = {}
    else:
        in_specs.append(pl.BlockSpec(memory_space=pl.ANY))
        args.append(states)
        state_spec = pl.BlockSpec((1, 1) + state, lambda b, c: (layer, b, 0, 0, 0))
        aliases = {len(args) - 1: 1}
    return pl.pallas_call(
        functools.partial(_ret_core_kernel, has_s0, n_chunks, layer),
        grid=(n_seq, n_chunks),
        in_specs=in_specs,
        out_specs=[pl.BlockSpec((cl, RET_V), lambda b, c: (b * n_chunks + c, 0)), state_spec],
        out_shape=[jax.ShapeDtypeStruct((n_seq * seq_len, RET_V), BF16),
                   jax.ShapeDtypeStruct((N_RET, n_seq) + state, F32)],
        scratch_shapes=[pltpu.VMEM(state, F32)],
        input_output_aliases=aliases,
        compiler_params=_params("parallel", "arbitrary"),
        name="ret_core",
    )(*args)


def _residual_router(acc, x_ref, mod, gain_ref, wr_ref, br_ref, xo_ref, h2_ref, lg_ref):
    gate1 = mod[:, 2:3, :]
    xn = x_ref[...] + _per_group(acc, lambda a3: a3 * gate1)
    xo_ref[...] = xn
    h2 = _norm_mod(xn, gain_ref, mod, 3)
    h2_ref[...] = _pack_bf16_pairs(h2)
    hh = jnp.dot(h2.astype(BF16), wr_ref[...], preferred_element_type=F32)
    lg_ref[...] = hh[:, :ROUTER_LANES] + hh[:, ROUTER_LANES:] + br_ref[...]


def _mix_out_specs(rows):
    return [_row_spec(rows, D), _row_spec(rows, DP), _row_spec(rows, ROUTER_LANES)]


_MIX_OUT_SHAPE = [
    jax.ShapeDtypeStruct((T_ALL, D), F32),
    jax.ShapeDtypeStruct((T_ALL, DP), U32),
    jax.ShapeDtypeStruct((T_ALL, ROUTER_LANES), F32),
]


def _router_specs():
    return [_resident((D, 2 * ROUTER_LANES)), _resident((1, ROUTER_LANES))]


def _ret_out_kernel(n_prompt_tiles, layer, yp_ref, ys_ref, w_hbm, x_ref, modp_ref, mods_ref, gain_ref,
                    wr_ref, br_ref, xo_ref, h2_ref, lg_ref, w_s, w_stage, w_sems):
    _load_weight_bf16(w_hbm, layer, w_s, w_stage, w_sems)
    is_prompt = pl.program_id(0) < n_prompt_tiles
    yin = jnp.where(is_prompt, yp_ref[...], ys_ref[...])
    acc = jnp.dot(yin, w_s[...], preferred_element_type=F32)
    mod = _tile_mod((modp_ref, mods_ref), is_prompt)
    _residual_router(acc, x_ref, mod, gain_ref, wr_ref, br_ref, xo_ref, h2_ref, lg_ref)


def _ret_out_call(y_prompt, y_sample, w_out, layer, x, mod, gain, w_r, b_r):
    rows = ROWS_WIDE
    return pl.pallas_call(
        functools.partial(_ret_out_kernel, _n_prompt_tiles(rows), layer),
        grid=(_n_tiles(rows),),
        in_specs=[_prompt_rows_spec(rows, RET_V), _sample_rows_spec(rows, RET_V),
                  pl.BlockSpec(memory_space=pl.ANY), _row_spec(rows, D), *_mod_specs(rows),
                  _resident((1, D))] + _router_specs(),
        out_specs=_mix_out_specs(rows),
        out_shape=_MIX_OUT_SHAPE,
        scratch_shapes=_weight_scratch(RET_V, D),
        compiler_params=_params("arbitrary"),
        name="ret_out",
    )(y_prompt, y_sample, w_out, x, *mod, gain, w_r, b_r)


GM_TN = 512


_GELU_C = float(np.sqrt(2.0 / np.pi))


def _gelu_tanh(x):
    hx = 0.5 * x
    return hx * jnp.tanh(x * (_GELU_C + (_GELU_C * 0.044715) * (x * x))) + hx


def _gm_proj_kernel(n_tiles, n_prompt_tiles, layer, *refs):
    x_ref, yg0_ref, yg1_ref, gates_ref = refs[:4]
    gain_ref, w_hbm, b_ref, lg_ref, lb_ref = refs[8:13]
    (xo_ref, uv_ref, vs_ref, hb_cur, hb_prev, vraw_cur, vraw_prev, stat_prev, sum_s,
     w_s, w_stage, w_sems) = refs[-12:]
    _load_weight_bf16(w_hbm, layer, w_s, w_stage, w_sems)
    i = pl.program_id(0)
    rows = x_ref.shape[0]

    @pl.when(i == 0)
    def _():
        hb_prev[...] = jnp.zeros_like(hb_prev)
        vraw_prev[...] = jnp.zeros_like(vraw_prev)
        stat_prev[...] = jnp.zeros_like(stat_prev)

    def add_row(x, row_ref, lo, width):
        x3 = x.reshape(rows // 8, 8, width) + row_ref[:, lo:lo + width]
        return x3.reshape(rows, width)

    def mul_row(x, row_ref, lo, width):
        x3 = x.reshape(rows // 8, 8, width) * row_ref[:, lo:lo + width]
        return x3.reshape(rows, width)

    def proj_chunk(hb_ref, lo):
        z = jnp.dot(hb_ref[...], w_s[:, lo:lo + GM_TN], preferred_element_type=F32)
        return add_row(z, b_ref, lo, GM_TN).astype(BF16)

    def stage_b_chunk(lo):
        uv_ref[:, lo:lo + GM_TN] = _gelu_tanh(proj_chunk(hb_prev, lo))
        for k in range(lo, lo + GM_TN, 128):
            vk = vraw_prev[:, k:k + 128].astype(F32) * stat_prev[:, 0:128] + stat_prev[:, 128:256]
            vn = add_row(mul_row(vk, lg_ref, k, 128), lb_ref, k, 128)
            uv_ref[:, GM_HALF + k:GM_HALF + k + 128] = vn.astype(BF16)
            vs_ref[0, :, k:k + 128] = vn

    def stage_a_chunk(n, lo):
        gz = _gelu_tanh(proj_chunk(hb_cur, GM_HALF + lo))
        vraw_cur[:, lo:lo + GM_TN] = gz
        gf = gz.astype(F32)
        pieces = [gf[:, k:k + 128] for k in range(0, GM_TN, 128)]
        t1 = functools.reduce(lambda p, q: p + q, pieces)
        t2 = functools.reduce(lambda p, q: p + q, [p * p for p in pieces])
        if n == 0:
            sum_s[:, 0:128] = t1
            sum_s[:, 128:256] = t2
        else:
            sum_s[:, 0:128] += t1
            sum_s[:, 128:256] += t2

    chunks = list(range(0, GM_HALF, GM_TN))
    stage_b_chunk(chunks[0])
    a_is_prompt = jnp.minimum(i, n_tiles - 1) < n_prompt_tiles
    x = _add_moe(x_ref[...], yg0_ref, yg1_ref, gates_ref, _tile_mod(refs[4:6], a_is_prompt))
    xo_ref[...] = x
    hb_cur[...] = _norm_mod(x, gain_ref, _tile_mod(refs[6:8], a_is_prompt), 0).astype(BF16)
    for n, lo in enumerate(chunks):
        stage_a_chunk(n, lo)
        if 0 < n < len(chunks) - 1:
            stage_b_chunk(lo)
    mu = jnp.sum(sum_s[:, 0:128], axis=-1, keepdims=True) * (1.0 / GM_HALF)
    var = jnp.sum(sum_s[:, 128:256], axis=-1, keepdims=True) * (1.0 / GM_HALF) - mu * mu
    rstd = lax.rsqrt(var + EPS)
    stage_b_chunk(chunks[-1])
    stat_prev[:, 0:128] = jnp.broadcast_to(rstd, (rows, 128))
    stat_prev[:, 128:256] = jnp.broadcast_to(-mu * rstd, (rows, 128))
    hb_prev[...] = hb_cur[...]
    vraw_prev[...] = vraw_cur[...]

    if layer == 0:
        @pl.when(i - 1 >= n_prompt_tiles)
        def _():
            for later in range(1, N_GM):
                vs_ref[later] = jnp.zeros((rows, GM_HALF), F32)


def _gm_proj_call(x, prev, mod, gain, w_in, layer, b_in, ln_g, ln_b, vs_all):
    rows = ROWS_GM_PROJ
    n, npt = _n_tiles(rows), _n_prompt_tiles(rows)

    def stage_a(spec):
        return _stage_a_spec(spec, n)

    stage_b = _stage_b_spec
    in_specs = [stage_a(s) for s in [_row_spec(rows, D)] + _prev_specs(rows) + _mod_specs(rows)] + [
        _resident((1, D)), pl.BlockSpec(memory_space=pl.ANY), _resident((8, GM_FFN)),
        _resident((8, GM_HALF)), _resident((8, GM_HALF))]
    rows8 = [jnp.broadcast_to(r.reshape(1, -1), (8, r.size)) for r in (b_in, ln_g, ln_b)]
    args = [x, *prev, *mod, gain, w_in, *rows8]
    if layer == 0:
        assert vs_all is None
        vs_spec = pl.BlockSpec((N_GM, rows, GM_HALF), lambda i: (0, jnp.maximum(i - 1 - npt, 0), 0))
        aliases = {}
    else:
        in_specs.append(pl.BlockSpec(memory_space=pl.ANY))
        args.append(vs_all)
        vs_spec = pl.BlockSpec((1, rows, GM_HALF), lambda i: (layer, jnp.maximum(i - 1 - npt, 0), 0))
        aliases = {len(args) - 1: 2}
    return pl.pallas_call(
        functools.partial(_gm_proj_kernel, n, npt, layer),
        grid=(n + 1,),
        in_specs=in_specs,
        out_specs=[stage_a(_row_spec(rows, D)), stage_b(_row_spec(rows, GM_FFN)), vs_spec],
        out_shape=[jax.ShapeDtypeStruct((T_ALL, D), F32),
                   jax.ShapeDtypeStruct((T_ALL, GM_FFN), BF16),
                   jax.ShapeDtypeStruct((N_GM, T_SAMPLE, GM_HALF), F32)],
        scratch_shapes=[pltpu.VMEM((rows, D), BF16), pltpu.VMEM((rows, D), BF16),
                        pltpu.VMEM((rows, GM_HALF), BF16), pltpu.VMEM((rows, GM_HALF), BF16),
                        pltpu.VMEM((rows, 256), F32), pltpu.VMEM((rows, 256), F32),
                        *_weight_scratch(D, GM_FFN)],
        input_output_aliases=aliases,
        compiler_params=_params("arbitrary"),
        name="gm_proj",
    )(*args)


def _gm_out_kernel(n_prompt_tiles, layer, uv_ref, ws_ref, bs_ref, w_hbm, bo_ref, x_ref, modp_ref,
                   mods_ref, gain_ref, wr_ref, br_ref, xo_ref, h2_ref, lg_ref, w_s, w_stage, w_sems):
    _load_weight_bf16(w_hbm, layer, w_s, w_stage, w_sems)
    rows = uv_ref.shape[0]
    mod = _tile_mod((modp_ref, mods_ref), pl.program_id(0) < n_prompt_tiles)
    pieces = []
    for r0 in range(0, rows, GM_MIX):
        acc = jnp.zeros((GM_MIX, D), F32)
        for g in range(GM_GROUPS):
            lo = g * GM_GDIM
            sp = jnp.dot(ws_ref[0, g], uv_ref[r0:r0 + GM_MIX, GM_HALF + lo:GM_HALF + lo + GM_GDIM],
                         preferred_element_type=F32) + bs_ref[0, g]
            gated = (uv_ref[r0:r0 + GM_MIX, lo:lo + GM_GDIM].astype(F32) * sp).astype(BF16)
            acc = acc + jnp.dot(gated, w_s[lo:lo + GM_GDIM, :], preferred_element_type=F32)
        pieces.append(acc)
    acc = jnp.concatenate(pieces, axis=0) + bo_ref[...]
    _residual_router(acc, x_ref, mod, gain_ref, wr_ref, br_ref, xo_ref, h2_ref, lg_ref)


def _gm_block_diag(w_s, b_s):
    mats, biases = [], []
    for cl in (GM_CHUNK, DEC_SEQ):
        tri = jnp.tril(jnp.ones((cl, cl), bool))
        blk = jnp.where(tri[None], w_s[:, :cl, :cl], 0.0)
        reps = GM_MIX // cl
        eye = jnp.eye(reps, dtype=w_s.dtype)
        bd = jnp.einsum("ab,gts->gatbs", eye, blk).reshape(GM_GROUPS, GM_MIX, GM_MIX)
        mats.append(bd)
        biases.append(jnp.tile(b_s[:, :cl], (1, reps))[:, :, None])
    return jnp.stack(mats).astype(BF16), jnp.stack(biases).astype(F32)


def _gm_out_call(uv, ws_bd, bs_bd, w_out, layer, b_out, x, mod, gain, w_r, b_r):
    rows = ROWS_WIDE
    npt = _n_prompt_tiles(rows)

    def variant(i):
        return jnp.where(i >= npt, 1, 0)

    return pl.pallas_call(
        functools.partial(_gm_out_kernel, npt, layer),
        grid=(_n_tiles(rows),),
        in_specs=[_row_spec(rows, GM_FFN),
                  pl.BlockSpec((1, GM_GROUPS, GM_MIX, GM_MIX), lambda i: (variant(i), 0, 0, 0)),
                  pl.BlockSpec((1, GM_GROUPS, GM_MIX, 1), lambda i: (variant(i), 0, 0, 0)),
                  pl.BlockSpec(memory_space=pl.ANY), _resident((1, D)), _row_spec(rows, D),
                  *_mod_specs(rows), _resident((1, D))] + _router_specs(),
        out_specs=_mix_out_specs(rows),
        out_shape=_MIX_OUT_SHAPE,
        scratch_shapes=_weight_scratch(GM_HALF, D),
        compiler_params=_params("arbitrary"),
        name="gm_out",
    )(uv, ws_bd, bs_bd, w_out, b_out, x, *mod, gain, w_r, b_r)


def _expert_kernel(layer, be_ref, bv_ref, nb_ref, nx_ref, xb_ref, w1_hbm, w3_hbm, w2_hbm, yb_ref,
                   st1, st3, st2, w1s, w3s, w2s, sems, slot_ref):
    b = pl.program_id(0)

    def weight_copies(e, slot):
        return (pltpu.make_async_copy(w1_hbm.at[layer, e], st1.at[slot], sems.at[slot, 0]),
                pltpu.make_async_copy(w3_hbm.at[layer, e], st3.at[slot], sems.at[slot, 1]),
                pltpu.make_async_copy(w2_hbm.at[layer, e], st2.at[slot], sems.at[slot, 2]))

    @pl.when(b == 0)
    def _():
        slot_ref[0] = 0
        for cp in weight_copies(be_ref[0], 0):
            cp.start()

    @pl.when(b < nb_ref[0])
    def _():
        prev_e = be_ref[jnp.maximum(b - 1, 0)]

        @pl.when((b == 0) | (be_ref[b] != prev_e))
        def _():
            slot = slot_ref[0]
            for cp in weight_copies(be_ref[b], slot):
                cp.wait()
            w1s[...] = st1[slot].astype(BF16)
            w3s[...] = st3[slot].astype(BF16)
            w2s[...] = st2[slot].astype(BF16)

            @pl.when(nx_ref[b] >= 0)
            def _():
                for cp in weight_copies(nx_ref[b], 1 - slot):
                    cp.start()

            slot_ref[0] = 1 - slot

        valid = bv_ref[b]

        def run_rows(n):
            row = lax.broadcasted_iota(jnp.int32, (n, 1), 0)
            xw = jnp.where(row < valid, xb_ref[0:n], U32(0))
            x = _unpack_bf16_pairs(xw).astype(BF16)
            a = jnp.dot(x, w1s[...], preferred_element_type=F32)
            c = jnp.dot(x, w3s[...], preferred_element_type=F32)
            h = (_silu(a) * c).astype(BF16)
            yb_ref[0:n] = _pack_bf16_pairs(jnp.dot(h, w2s[...], preferred_element_type=F32))
            if n < EXP_BLOCK:
                yb_ref[n:EXP_BLOCK] = jnp.zeros((EXP_BLOCK - n, DP), U32)

        for n in range(EXP_ROW_STEP, EXP_BLOCK + 1, EXP_ROW_STEP):
            pl.when((valid > n - EXP_ROW_STEP) & (valid <= n))(functools.partial(run_rows, n))


def _expert_call(blk_e, blk_valid, n_blk, blk_next, xb, w1, w3, w2, layer):
    def blk(b, be, bv, nb, nx):
        return (jnp.minimum(b, nb[0] - 1), 0)

    up, down = (D, MOE_HIDDEN), (MOE_HIDDEN, D)
    grid_spec = pltpu.PrefetchScalarGridSpec(
        num_scalar_prefetch=4,
        grid=(N_EXP_BLOCKS,),
        in_specs=[pl.BlockSpec((EXP_BLOCK, DP), blk)] + [pl.BlockSpec(memory_space=pl.ANY)] * 3,
        out_specs=pl.BlockSpec((EXP_BLOCK, DP), blk),
        scratch_shapes=[pltpu.VMEM((2,) + up, F32), pltpu.VMEM((2,) + up, F32), pltpu.VMEM((2,) + down, F32),
                        pltpu.VMEM(up, BF16), pltpu.VMEM(up, BF16), pltpu.VMEM(down, BF16),
                        pltpu.SemaphoreType.DMA((2, 3)), pltpu.SMEM((1,), jnp.int32)],
    )
    return pl.pallas_call(
        functools.partial(_expert_kernel, layer),
        grid_spec=grid_spec,
        out_shape=jax.ShapeDtypeStruct((P_ROWS, DP), U32),
        compiler_params=_params("arbitrary"),
        name="experts",
    )(blk_e, blk_valid, n_blk, blk_next, xb, w1, w3, w2)


def _route_kernel(lg_ref, dest_ref, gates_ref, meta_ref, cnt_ref):
    t = pl.program_id(0)

    @pl.when(t == 0)
    def _():
        cnt_ref[...] = jnp.zeros_like(cnt_ref)

    pl.when(t < N_ROUTE_TILES)(functools.partial(_route_tile, t, lg_ref, dest_ref, gates_ref, cnt_ref))
    pl.when(t == N_ROUTE_TILES)(functools.partial(_route_finish, dest_ref, meta_ref, cnt_ref))


ROUTE_RANK_BITS = 16
assert N_ASSIGN < 2 ** ROUTE_RANK_BITS


def _route_tile(t, lg_ref, dest_ref, gates_ref, cnt_ref):
    tm = ROUTE_TM
    lt = lg_ref[...].T
    el = lt[0:MOE_EXPERTS]
    gl = lt[MOE_EXPERTS:MOE_EXPERTS + 8]
    gidx = lax.broadcasted_iota(jnp.int32, (8, tm), 0)
    neg = jnp.float32(-jnp.inf)
    gl = jnp.where(gidx < MOE_GROUPS, gl, neg)
    gmax = jnp.max(gl, axis=0, keepdims=True)
    grp = jnp.min(jnp.where(gl == gmax, gidx, MOE_GROUPS), axis=0, keepdims=True)
    eidx = lax.broadcasted_iota(jnp.int32, (MOE_EXPERTS, tm), 0)
    els = jnp.where((eidx >> 3) == grp, el, neg)
    m1 = jnp.max(els, axis=0, keepdims=True)
    i1 = jnp.min(jnp.where(els == m1, eidx, MOE_EXPERTS), axis=0, keepdims=True)
    els2 = jnp.where(eidx == i1, neg, els)
    m2 = jnp.max(els2, axis=0, keepdims=True)
    i2 = jnp.min(jnp.where(els2 == m2, eidx, MOE_EXPERTS), axis=0, keepdims=True)
    sel1 = eidx == i1
    sel2 = eidx == i2
    cnt = jnp.where(sel1 | sel2, 1.0, 0.0)

    lane = ROUTER_LANES
    before = (lax.broadcasted_iota(jnp.int32, (lane, lane), 0)
              < lax.broadcasted_iota(jnp.int32, (lane, lane), 1))
    tri = jnp.where(before, 1.0, 0.0).astype(BF16)
    run = cnt_ref[...]
    r1, r2 = [], []
    for k in range(tm // lane):
        piece = slice(k * lane, (k + 1) * lane)
        ck = cnt[:, piece]
        pos = run + jnp.dot(ck.astype(BF16), tri, preferred_element_type=F32)
        r1.append(jnp.sum(jnp.where(sel1[:, piece], pos, 0.0), axis=0, keepdims=True))
        r2.append(jnp.sum(jnp.where(sel2[:, piece], pos, 0.0), axis=0, keepdims=True))
        run = run + jnp.sum(ck, axis=1, keepdims=True)
    cnt_ref[...] = run
    rank = jnp.concatenate([jnp.concatenate(r1, axis=1), jnp.concatenate(r2, axis=1)], axis=0)
    eid = jnp.concatenate([i1, i2], axis=0)
    dest_ref[:, pl.ds(pl.multiple_of(t * tm, tm), tm)] = (eid << ROUTE_RANK_BITS) + rank.astype(jnp.int32)

    g_w = 1.0 / jnp.sum(jnp.exp(gl - gmax), axis=0, keepdims=True)
    e21 = jnp.exp(m2 - m1)
    p1 = 1.0 / (1.0 + e21)
    rid = lax.broadcasted_iota(jnp.int32, (ROUTER_LANES, tm), 0)
    gt = jnp.where(rid == 0, g_w * p1, jnp.where(rid == 1, g_w * (e21 * p1), 0.0))
    gates_ref[...] = gt.T


def _route_finish(dest_ref, meta_ref, cnt_ref):
    counts = cnt_ref[...]
    nblk = jnp.floor((counts + (EXP_BLOCK - 1.0)) * (1.0 / EXP_BLOCK))
    r = lax.broadcasted_iota(jnp.int32, (MOE_EXPERTS, MOE_EXPERTS), 0)
    c = lax.broadcasted_iota(jnp.int32, (MOE_EXPERTS, MOE_EXPERTS), 1)
    nblk_row = jnp.sum(jnp.where(r == c, nblk, 0.0), axis=0, keepdims=True)
    bstart = jnp.sum(jnp.where(c < r, nblk_row, 0.0), axis=1, keepdims=True)
    bend = bstart + nblk
    bidx = lax.broadcasted_iota(jnp.int32, (1, META_LANES), 1).astype(F32)
    blk_e = jnp.minimum(jnp.sum(jnp.where(bidx >= bend, 1.0, 0.0), axis=0, keepdims=True),
                        MOE_EXPERTS - 1.0)
    erow = lax.broadcasted_iota(jnp.int32, (MOE_EXPERTS, META_LANES), 0).astype(F32)
    mine = erow == blk_e
    cnt_b = jnp.sum(jnp.where(mine, counts, 0.0), axis=0, keepdims=True)
    start_b = jnp.sum(jnp.where(mine, bstart, 0.0), axis=0, keepdims=True)
    valid = jnp.clip(cnt_b - (bidx - start_b) * EXP_BLOCK, 0.0, float(EXP_BLOCK))
    n_blk = jnp.sum(nblk, axis=0, keepdims=True)
    end_b = jnp.sum(jnp.where(mine, bend, 0.0), axis=0, keepdims=True)
    nxt = jnp.minimum(jnp.sum(jnp.where(end_b >= bend, 1.0, 0.0), axis=0, keepdims=True),
                      MOE_EXPERTS - 1.0)
    nxt = jnp.where(end_b < n_blk, nxt, -1.0)
    mrow = lax.broadcasted_iota(jnp.int32, (8, META_LANES), 0)
    meta = jnp.where(mrow == 0, blk_e, jnp.where(mrow == 1, valid, jnp.where(
        mrow == 2, n_blk, jnp.where(mrow == 3, nxt, 0.0))))
    meta_ref[...] = meta.astype(jnp.int32)

    base = (bstart * EXP_BLOCK).astype(jnp.int32)
    packed = dest_ref[...]
    eid = packed >> ROUTE_RANK_BITS
    row = packed & (2 ** ROUTE_RANK_BITS - 1)
    for e in range(MOE_EXPERTS):
        row = row + jnp.where(eid == e, base[e:e + 1, :], 0)
    dest_ref[...] = row


def _route_call(logits):
    last = N_ROUTE_TILES - 1
    return pl.pallas_call(
        _route_kernel,
        grid=(N_ROUTE_TILES + 1,),
        in_specs=[pl.BlockSpec((ROUTE_TM, ROUTER_LANES), lambda t: (jnp.minimum(t, last), 0))],
        out_specs=[pl.BlockSpec((MOE_TOPK, T_ALL), lambda t: (0, 0)),
                   pl.BlockSpec((ROUTE_TM, ROUTER_LANES), lambda t: (jnp.minimum(t, last), 0)),
                   pl.BlockSpec((8, META_LANES), lambda t: (0, 0))],
        out_shape=[jax.ShapeDtypeStruct((MOE_TOPK, T_ALL), jnp.int32),
                   jax.ShapeDtypeStruct((T_ALL, ROUTER_LANES), F32),
                   jax.ShapeDtypeStruct((8, META_LANES), jnp.int32)],
        scratch_shapes=[pltpu.VMEM((MOE_EXPERTS, 1), F32)],
        compiler_params=_params("arbitrary"),
        name="route",
    )(logits)


def _sc_mesh():
    return plsc.VectorSubcoreMesh(core_axis_name="c", subcore_axis_name="s")


def _sc_token_offset(j):
    wid = lax.axis_index("s") * SC_CORES + lax.axis_index("c")
    return pl.multiple_of(wid * SC_ROWS_PER_WORKER + j * SC_CHUNK, 8)


SC_N_CHUNKS = SC_ROWS_PER_WORKER // SC_CHUNK
assert SC_N_CHUNKS % 2 == 1


def _start(copies):
    for cp in copies:
        cp.start()


def _wait(copies):
    for cp in copies:
        cp.wait()


def _dispatch_body(h_hbm, d0_hbm, d1_hbm, out_hbm, i0a, i1a, rows_a, i0b, i1b, rows_b, la, lb, sa, sb):
    sets = {"a": (i0a, i1a, rows_a, la, sa), "b": (i0b, i1b, rows_b, lb, sb)}

    def loads(j, s):
        i0, i1, rows, lsem, _ = sets[s]
        src = pl.ds(_sc_token_offset(j), SC_CHUNK)
        return (pltpu.make_async_copy(d0_hbm.at[src], i0, lsem),
                pltpu.make_async_copy(d1_hbm.at[src], i1, lsem),
                pltpu.make_async_copy(h_hbm.at[src], rows, lsem))

    def scatters(s):
        i0, i1, rows, _, ssem = sets[s]
        return (pltpu.make_async_copy(rows, out_hbm.at[i0], ssem),
                pltpu.make_async_copy(rows, out_hbm.at[i1], ssem))

    _start(loads(0, "a"))

    @pl.loop(0, SC_N_CHUNKS // 2)
    def _(p):
        a = 2 * p
        _start(loads(a + 1, "b"))
        _wait(loads(a, "a"))
        _start(scatters("a"))
        _wait(loads(a + 1, "b"))
        _start(scatters("b"))
        _wait(scatters("a"))
        _start(loads(a + 2, "a"))
        _wait(scatters("b"))

    _wait(loads(SC_N_CHUNKS - 1, "a"))
    _start(scatters("a"))
    _wait(scatters("a"))


def _dispatch_call(h2, dest0, dest1):
    buffers = [pltpu.VMEM((SC_CHUNK,), jnp.int32), pltpu.VMEM((SC_CHUNK,), jnp.int32),
               pltpu.VMEM((SC_CHUNK, DP), U32)]
    return pl.kernel(
        _dispatch_body,
        out_type=jax.ShapeDtypeStruct((P_ROWS, DP), U32),
        mesh=_sc_mesh(),
        scratch_types=buffers + buffers + [pltpu.SemaphoreType.DMA] * 4,
        name="moe_dispatch",
    )(h2, dest0, dest1)


def _combine_body(yb_hbm, d0_hbm, d1_hbm, o0_hbm, o1_hbm, i0a, i1a, r0a, r1a, i0b, i1b, r0b, r1b,
                  ia, ib, ga, gb, wa, wb):
    sets = {"a": (i0a, i1a, r0a, r1a, ia, ga, wa), "b": (i0b, i1b, r0b, r1b, ib, gb, wb)}

    def index_loads(j, s):
        i0, i1, _, _, isem, _, _ = sets[s]
        src = pl.ds(_sc_token_offset(j), SC_CHUNK)
        return (pltpu.make_async_copy(d0_hbm.at[src], i0, isem), pltpu.make_async_copy(d1_hbm.at[src], i1, isem))

    def gathers(s):
        i0, i1, r0, r1, _, gsem, _ = sets[s]
        return (pltpu.make_async_copy(yb_hbm.at[i0], r0, gsem), pltpu.make_async_copy(yb_hbm.at[i1], r1, gsem))

    def writes(j, s):
        _, _, r0, r1, _, _, wsem = sets[s]
        dst = pl.ds(_sc_token_offset(j), SC_CHUNK)
        return (pltpu.make_async_copy(r0, o0_hbm.at[dst], wsem), pltpu.make_async_copy(r1, o1_hbm.at[dst], wsem))

    _start(index_loads(0, "a"))
    _wait(index_loads(0, "a"))
    _start(gathers("a"))

    @pl.loop(0, SC_N_CHUNKS // 2)
    def _(p):
        a = 2 * p
        _start(index_loads(a + 1, "b"))
        _wait(index_loads(a + 1, "b"))
        _wait(gathers("a"))
        _start(writes(a, "a"))
        _start(gathers("b"))
        _start(index_loads(a + 2, "a"))
        _wait(index_loads(a + 2, "a"))
        _wait(writes(a, "a"))
        _wait(gathers("b"))
        _start(writes(a + 1, "b"))
        _start(gathers("a"))
        _wait(writes(a + 1, "b"))

    _wait(gathers("a"))
    _start(writes(SC_N_CHUNKS - 1, "a"))
    _wait(writes(SC_N_CHUNKS - 1, "a"))


def _combine_call(yb, dest0, dest1):
    out = jax.ShapeDtypeStruct((T_ALL, DP), U32)
    buffers = [pltpu.VMEM((SC_CHUNK,), jnp.int32), pltpu.VMEM((SC_CHUNK,), jnp.int32),
               pltpu.VMEM((SC_CHUNK, DP), U32), pltpu.VMEM((SC_CHUNK, DP), U32)]
    return pl.kernel(
        _combine_body,
        out_type=(out, out),
        mesh=_sc_mesh(),
        scratch_types=buffers + buffers + [pltpu.SemaphoreType.DMA] * 6,
        name="moe_combine",
    )(yb, dest0, dest1)


def _moe_rows(h2, dest, meta, w1, w3, w2, layer):
    dest0, dest1 = dest[0], dest[1]
    xb = _dispatch_call(h2, dest0, dest1)
    yb = _expert_call(meta[0, :N_EXP_BLOCKS], meta[1, :N_EXP_BLOCKS], meta[2, :1], meta[3, :N_EXP_BLOCKS],
                      xb, w1, w3, w2, layer)
    return _combine_call(yb, dest0, dest1)


def _final_kernel(is_prompt, x_ref, yg0_ref, yg1_ref, gates_ref, modp_ref, mods_ref, gain_ref, o_ref):
    x = _add_moe(x_ref[...], yg0_ref, yg1_ref, gates_ref, _tile_mod((modp_ref, mods_ref), is_prompt))
    o_ref[...] = _rms(x) * gain_ref[...]


def _final_call(x, prev, gain, row0, n_rows):
    rows = ROWS_FINAL
    tile0 = row0 // rows
    is_prompt = row0 < T_PROMPT
    assert row0 + n_rows <= T_PROMPT or not is_prompt

    def tile(width):
        return pl.BlockSpec((rows, width), lambda i: (tile0 + i, 0))

    mod_specs = [pl.BlockSpec(s.block_shape, lambda i, m=s.index_map: m(tile0 + i)) for s in _mod_specs(rows)]
    return pl.pallas_call(
        functools.partial(_final_kernel, is_prompt),
        grid=(n_rows // rows,),
        in_specs=[tile(D), tile(DP), tile(DP), tile(ROUTER_LANES), *mod_specs, _resident((1, D))],
        out_specs=pl.BlockSpec((rows, D), lambda i: (i, 0)),
        out_shape=jax.ShapeDtypeStruct((n_rows, D), F32),
        compiler_params=_params("parallel"),
        name="final_norm",
    )(x, *prev, gain)


def _rope_table():
    pos = np.concatenate([np.tile(np.arange(SEQ), BATCH),
                          np.tile(PAST_LEN + np.arange(DEC_SEQ), DEC_BATCH)]).astype(np.float32)
    inv = (ROPE_BASE ** (-np.arange(ROPE_HALF, dtype=np.float32) / ROPE_HALF)).astype(np.float32)
    ang = (pos[:, None] * inv[None, :]).astype(np.float32).astype(np.float64)
    pad = np.zeros((T_ALL, ROPE_TABLE_LANES - 2 * ROPE_HALF - 2 * RET_HEADS))
    return jnp.asarray(np.concatenate([np.cos(ang), np.sin(ang), _ret_row_scales(), pad], axis=1), F32)


def kernel(x_prompt, x_sample, c_prompt, c_sample, state_ret, ada_w, ada_b, norm1_g, norm2_g, ret_w_in,
           ret_w_out, gm_w_in, gm_b_in, gm_ln_g, gm_ln_b, gm_w_s, gm_b_s, gm_w_out, gm_b_out, moe_w_rg,
           moe_b_rg, moe_w_re, moe_b_re, moe_w1, moe_w3, moe_w2, final_g):
    x = (x_prompt.reshape(T_PROMPT, D), x_sample.reshape(T_SAMPLE, D))
    c_all = jnp.concatenate([c_prompt, c_sample], axis=0)
    rope = _rope_table()

    mod_all = _ada_call(c_all, ada_w, ada_b).reshape(DEPTH, N_SEQ, 6, D)

    def layer_params(i):
        mod = (mod_all[i, :BATCH], mod_all[i, BATCH:])
        w_r = jnp.pad(jnp.concatenate([moe_w_re[i], moe_w_rg[i]], axis=1),
                      ((0, 0), (0, ROUTER_LANES - MOE_GROUPS - MOE_EXPERTS)))
        w_r_hi = w_r.astype(BF16)
        w_r_lo = (w_r - w_r_hi.astype(F32)).astype(BF16)
        w_r = jnp.concatenate([w_r_hi, w_r_lo], axis=1)
        b_r = jnp.pad(jnp.concatenate([moe_b_re[i].reshape(-1), moe_b_rg[i]]),
                      (0, ROUTER_LANES - MOE_GROUPS - MOE_EXPERTS)).reshape(1, ROUTER_LANES)
        return mod, w_r, b_r

    ret_prompt = ret_sample = gm_sample = None
    prev = None
    for i in range(DEPTH):
        j = i // 2
        mod, w_r, b_r = layer_params(i)
        g1 = norm1_g[i].reshape(1, D)
        g2 = norm2_g[i].reshape(1, D)
        if i % 2 == 0:
            x, p = _ret_proj_call(x, prev, mod, g1, ret_w_in, j, rope)
            y_p, ret_prompt = _ret_core_call(p, None, ret_prompt, j, BATCH, SEQ, RET_CHUNK_PROMPT, 0)
            y_s, ret_sample = _ret_core_call(p, state_ret, ret_sample, j, DEC_BATCH, DEC_SEQ,
                                             RET_CHUNK_SAMPLE, T_PROMPT)
            x, h2, logits = _ret_out_call(y_p, y_s, ret_w_out, j, x, mod, g2, w_r, b_r)
        else:
            x, uv, gm_sample = _gm_proj_call(x, prev, mod, g1, gm_w_in, j, gm_b_in[j], gm_ln_g[j],
                                             gm_ln_b[j], gm_sample)
            ws_bd, bs_bd = _gm_block_diag(gm_w_s[j], gm_b_s[j])
            x, h2, logits = _gm_out_call(uv, ws_bd, bs_bd, gm_w_out, j,
                                         gm_b_out[j].reshape(1, D), x, mod, g2, w_r, b_r)
        dest, gates, meta = _route_call(logits)
        yg0, yg1 = _moe_rows(h2, dest, meta, moe_w1, moe_w3, moe_w2, i)
        prev = (yg0, yg1, gates, *mod)

    fg = final_g.reshape(1, D)
    y_prompt = _final_call(x, prev, fg, 0, T_PROMPT).reshape(BATCH, SEQ, D)
    y_sample = _final_call(x, prev, fg, T_PROMPT, T_SAMPLE).reshape(DEC_BATCH, DEC_SEQ, D)
    return (y_prompt, y_sample, ret_prompt, ret_sample,
            gm_sample.reshape(N_GM, DEC_BATCH, DEC_SEQ, GM_HALF))
```

```python
import functools

import numpy as np
import jax
import jax.numpy as jnp
from jax import lax
from jax.experimental import pallas as pl
from jax.experimental.pallas import tpu as pltpu
from jax.experimental.pallas import tpu_sc as plsc

F32 = jnp.float32
BF16 = jnp.bfloat16
U32 = jnp.uint32

D = 1024
BATCH, SEQ = 4, 4096
DEC_BATCH, DEC_SEQ = 16, 64
PAST_LEN = 4096
DEPTH = 4
N_RET = (DEPTH + 1) // 2
N_GM = DEPTH // 2
N_SEQ = BATCH + DEC_BATCH

RET_HEADS, RET_DK, RET_DV = 4, 256, 512
RET_QK = RET_HEADS * RET_DK
RET_V = RET_HEADS * RET_DV
RET_IN = 2 * RET_QK + 2 * RET_V
ROPE_BASE = 10000.0
ROPE_HALF = RET_DK // 2
ROPE_TABLE_LANES = 3 * ROPE_HALF

GM_FFN = 6 * D
GM_HALF = GM_FFN // 2
GM_GROUPS = 4
GM_GDIM = GM_HALF // GM_GROUPS
GM_CHUNK = 128

MOE_GROUPS, MOE_PER_GROUP = 4, 8
MOE_EXPERTS = MOE_GROUPS * MOE_PER_GROUP
MOE_TOPK = 2
MOE_HIDDEN = 512
EPS = 1e-6

GROUP = DEC_SEQ
T_PROMPT = BATCH * SEQ
T_SAMPLE = DEC_BATCH * DEC_SEQ
T_ALL = T_PROMPT + T_SAMPLE
N_GROUPS = T_ALL // GROUP
ROWS_WIDE = 512
ROWS_GM_PROJ = 256
ROWS_FINAL = 1024

RET_CHUNK_PROMPT = 256
RET_CHUNK_SAMPLE = DEC_SEQ

GM_MIX = 256

EXP_BLOCK = 1024
EXP_ROW_STEP = 128
N_ASSIGN = T_ALL * MOE_TOPK
N_EXP_BLOCKS = -(-(N_ASSIGN + MOE_EXPERTS * (EXP_BLOCK - 1)) // EXP_BLOCK)
P_ROWS = N_EXP_BLOCKS * EXP_BLOCK
ROUTER_LANES = 128
ROUTE_TM = 1024
N_ROUTE_TILES = T_ALL // ROUTE_TM
META_LANES = 256
assert META_LANES >= N_EXP_BLOCKS

DP = D // 2
SC_CORES, SC_SUBCORES = 2, 16
SC_WORKERS = SC_CORES * SC_SUBCORES
SC_ROWS_PER_WORKER = T_ALL // SC_WORKERS
SC_CHUNK = 32
assert SC_ROWS_PER_WORKER % SC_CHUNK == 0 and SC_CHUNK % 8 == 0

V7X_VMEM_LIMIT_BYTES = 56 * 1024 * 1024


def _params(*sem):
    return pltpu.CompilerParams(dimension_semantics=sem, vmem_limit_bytes=V7X_VMEM_LIMIT_BYTES)


def _resident(shape):
    nd = len(shape)
    return pl.BlockSpec(shape, lambda *_: (0,) * nd, pipeline_mode=pl.Buffered(1))


WEIGHT_STAGE_BYTES = 3 * 1024 * 1024


def _weight_scratch(k, n):
    chunk = k
    while chunk * n * 4 > WEIGHT_STAGE_BYTES:
        assert chunk % 16 == 0
        chunk //= 2
    return [pltpu.VMEM((k, n), BF16), pltpu.VMEM((2, chunk, n), F32), pltpu.SemaphoreType.DMA((2,))]


def _load_weight_bf16(w_hbm, layer, w_s, stage, sems):
    k = w_s.shape[0]
    chunk = stage.shape[1]

    def copy(c):
        return pltpu.make_async_copy(w_hbm.at[layer, pl.ds(c * chunk, chunk)], stage.at[c % 2], sems.at[c % 2])

    @pl.when(pl.program_id(0) == 0)
    def _():
        copy(0).start()
        for c in range(k // chunk):
            if c + 1 < k // chunk:
                copy(c + 1).start()
            copy(c).wait()
            w_s[c * chunk:(c + 1) * chunk, :] = stage[c % 2].astype(BF16)


def _rms(x):
    return x * lax.rsqrt(jnp.mean(x * x, axis=-1, keepdims=True) + EPS)


def _silu(x):
    return x * jax.nn.sigmoid(x)


def _per_group(x2d, fn):
    rows = x2d.shape[0]
    return fn(x2d.reshape(rows // GROUP, GROUP, D)).reshape(rows, D)


def _tile_mod(mod_refs, is_prompt):
    modp_ref, mods_ref = mod_refs
    return jnp.where(is_prompt, jnp.broadcast_to(modp_ref[...], mods_ref.shape), mods_ref[...])


def _norm_mod(x, gain_ref, mod, shift_idx):
    y = _rms(x) * gain_ref[...]
    scale = mod[:, shift_idx + 1:shift_idx + 2, :]
    shift = mod[:, shift_idx:shift_idx + 1, :]
    return _per_group(y, lambda y3: y3 * (1.0 + scale) + shift)


def _pack_bf16_pairs(x):
    lo = lax.bitcast_convert_type(x[:, :DP].astype(BF16).astype(F32), U32)
    hi = lax.bitcast_convert_type(x[:, DP:].astype(BF16).astype(F32), U32)
    return (lo >> 16) | (hi & U32(0xFFFF0000))


def _unpack_bf16_pairs(w):
    lo = lax.bitcast_convert_type(w << 16, F32)
    hi = lax.bitcast_convert_type(w & U32(0xFFFF0000), F32)
    return jnp.concatenate([lo, hi], axis=1)


def _add_moe(x, yg0_ref, yg1_ref, gates_ref, mod_prev):
    g = gates_ref[...]
    y = g[:, 0:1] * _unpack_bf16_pairs(yg0_ref[...]) + g[:, 1:2] * _unpack_bf16_pairs(yg1_ref[...])
    gate2 = mod_prev[:, 5:6, :]
    return x + _per_group(y, lambda y3: y3 * gate2)


ADA_TN = 1536


def _ada_kernel(c_ref, w_ref, b_ref, o_ref):
    c = c_ref[...]
    s = _silu(c).astype(BF16)
    o_ref[0] = jnp.dot(s, w_ref[0].astype(BF16), preferred_element_type=F32) + b_ref[0]


def _ada_call(c_all, ada_w, ada_b):
    return pl.pallas_call(
        _ada_kernel,
        grid=(DEPTH, 6 * D // ADA_TN),
        in_specs=[
            pl.BlockSpec((N_SEQ, D), lambda i, j: (0, 0)),
            pl.BlockSpec((1, D, ADA_TN), lambda i, j: (i, 0, j)),
            pl.BlockSpec((1, 1, ADA_TN), lambda i, j: (i, 0, j)),
        ],
        out_specs=pl.BlockSpec((1, N_SEQ, ADA_TN), lambda i, j: (i, 0, j)),
        out_shape=jax.ShapeDtypeStruct((DEPTH, N_SEQ, 6 * D), F32),
        compiler_params=_params("parallel", "parallel"),
        name="ada_modulation",
    )(c_all, ada_w, ada_b.reshape(DEPTH, 1, 6 * D))


def _n_tiles(rows):
    return T_ALL // rows


def _n_prompt_tiles(rows):
    return T_PROMPT // rows


def _row_spec(rows, width):
    return pl.BlockSpec((rows, width), lambda i: (i, 0))


def _mod_specs(rows):
    npt = _n_prompt_tiles(rows)
    return [pl.BlockSpec((1, 6, D), lambda i: (jnp.minimum(i * rows // SEQ, BATCH - 1), 0, 0)),
            pl.BlockSpec((rows // DEC_SEQ, 6, D), lambda i: (jnp.maximum(i - npt, 0), 0, 0))]


def _prev_specs(rows):
    return [_row_spec(rows, DP), _row_spec(rows, DP), _row_spec(rows, ROUTER_LANES)] + _mod_specs(rows)


def _stage_a_spec(spec, n_tiles):
    return pl.BlockSpec(spec.block_shape, lambda i, m=spec.index_map: m(jnp.minimum(i, n_tiles - 1)))


def _stage_b_spec(spec):
    return pl.BlockSpec(spec.block_shape, lambda i, m=spec.index_map: m(jnp.maximum(i - 1, 0)))


def _prompt_rows_spec(rows, width):
    last = _n_prompt_tiles(rows) - 1
    return pl.BlockSpec((rows, width), lambda i: (jnp.minimum(i, last), 0))


def _sample_rows_spec(rows, width):
    npt = _n_prompt_tiles(rows)
    return pl.BlockSpec((rows, width), lambda i: (jnp.maximum(i - npt, 0), 0))


def _ret_proj_kernel(has_prev, n_tiles, n_prompt_tiles, layer, *refs):
    i = pl.program_id(0)
    is_prompt = jnp.minimum(i, n_tiles - 1) < n_prompt_tiles
    head = refs[:6] if has_prev else refs[:2]
    refs = refs[len(head):]
    mod_refs = refs[:2]
    gain_ref, w_hbm, rope_ref, xo_ref, p_ref, hb_cur, hb_prev, w_s, w_stage, w_sems = refs[2:]
    _load_weight_bf16(w_hbm, layer, w_s, w_stage, w_sems)

    @pl.when(i == 0)
    def _():
        hb_prev[...] = jnp.zeros_like(hb_prev)

    def qk_head(j):
        lo = j * RET_DK
        acc = jnp.dot(hb_prev[...], w_s[:, lo:lo + RET_DK], preferred_element_type=F32)
        x1 = acc[:, :ROPE_HALF]
        x2 = acc[:, ROPE_HALF:]
        cos = rope_ref[:, 0:ROPE_HALF]
        sin = rope_ref[:, ROPE_HALF:2 * ROPE_HALF]
        scale = rope_ref[:, 2 * ROPE_HALF + j:2 * ROPE_HALF + j + 1]
        p_ref[:, lo:lo + ROPE_HALF] = ((x1 * cos - x2 * sin) * scale).astype(BF16)
        p_ref[:, lo + ROPE_HALF:lo + RET_DK] = ((x1 * sin + x2 * cos) * scale).astype(BF16)

    def vg_head(j):
        lo = 2 * RET_QK + j * RET_DV
        acc = jnp.dot(hb_prev[...], w_s[:, lo:lo + RET_DV], preferred_element_type=F32)
        if j >= RET_HEADS:
            acc = _silu(acc)
        p_ref[:, lo:lo + RET_DV] = acc.astype(BF16)

    vg_head(0)
    vg_head(1)
    if has_prev:
        x_ref, yg0_ref, yg1_ref, gates_ref = head[:4]
        x = _add_moe(x_ref[...], yg0_ref, yg1_ref, gates_ref, _tile_mod(head[4:6], is_prompt))
    else:
        x = jnp.where(is_prompt, head[0][...], head[1][...])
    xo_ref[...] = x
    hb_cur[...] = _norm_mod(x, gain_ref, _tile_mod(mod_refs, is_prompt), 0).astype(BF16)
    for j in range(2, 2 * RET_HEADS):
        vg_head(j)
    for j in range(2 * RET_HEADS):
        qk_head(j)
    hb_prev[...] = hb_cur[...]


def _ret_proj_call(x, prev, mod, gain, w_in, layer, rope):
    rows = ROWS_WIDE
    n = _n_tiles(rows)
    has_prev = prev is not None
    if has_prev:
        in_specs = [_row_spec(rows, D)] + _prev_specs(rows)
        args = [x] + list(prev)
    else:
        in_specs = [_prompt_rows_spec(rows, D), _sample_rows_spec(rows, D)]
        args = list(x)
    in_specs = [_stage_a_spec(s, n) for s in in_specs + _mod_specs(rows)] + [
        _resident((1, D)), pl.BlockSpec(memory_space=pl.ANY),
        _stage_b_spec(_row_spec(rows, ROPE_TABLE_LANES))]
    args += [*mod, gain, w_in, rope]
    return pl.pallas_call(
        functools.partial(_ret_proj_kernel, has_prev, n, _n_prompt_tiles(rows), layer),
        grid=(n + 1,),
        in_specs=in_specs,
        out_specs=[_stage_a_spec(_row_spec(rows, D), n), _stage_b_spec(_row_spec(rows, RET_IN))],
        out_shape=[jax.ShapeDtypeStruct((T_ALL, D), F32), jax.ShapeDtypeStruct((T_ALL, RET_IN), BF16)],
        scratch_shapes=[pltpu.VMEM((rows, D), BF16), pltpu.VMEM((rows, D), BF16),
                        *_weight_scratch(D, RET_IN)],
        compiler_params=_params("arbitrary"),
        name="ret_proj",
    )(*args)


RET_SEQS_PER_STEP = 2


def _ret_core_kernel(has_s0, n_chunks, layer, *refs):
    refs = list(refs)
    p_refs = [refs.pop(0) for _ in range(RET_SEQS_PER_STEP)]
    s0_ref = refs.pop(0) if has_s0 else None
    causal_ref, cd_ref = refs[:2]
    y_ref, so_ref, s_ref = refs[-3:]
    c = pl.program_id(1)

    @pl.when(c == 0)
    def _():
        if has_s0:
            s_ref[...] = s0_ref[0]
        else:
            s_ref[...] = jnp.zeros_like(s_ref)

    for h in range(RET_HEADS):
        for q, p_ref in enumerate(p_refs):
            qb = p_ref[:, h * RET_DK:(h + 1) * RET_DK]
            kb = p_ref[:, RET_QK + h * RET_DK:RET_QK + (h + 1) * RET_DK]
            vb = p_ref[:, 2 * RET_QK + h * RET_DV:2 * RET_QK + (h + 1) * RET_DV]
            gb = p_ref[:, 2 * RET_QK + RET_V + h * RET_DV:2 * RET_QK + RET_V + (h + 1) * RET_DV]
            scores = lax.dot_general(qb, kb, (((1,), (1,)), ((), ())), preferred_element_type=F32)
            scores = scores * causal_ref[...]
            s_old = s_ref[q, h]
            o = (jnp.dot(scores.astype(BF16), vb, preferred_element_type=F32)
                 + jnp.dot(qb, s_old.astype(BF16), preferred_element_type=F32))
            s_ref[q, h] = cd_ref[h][:, 0:1] * (s_old + lax.dot_general(
                kb, vb, (((0,), (0,)), ((), ())), preferred_element_type=F32))
            y_ref[q, :, h * RET_DV:(h + 1) * RET_DV] = (gb.astype(F32) * _rms(o)).astype(BF16)

    @pl.when(c == n_chunks - 1)
    def _():
        so_ref[0] = s_ref[...]
        if layer == 0:
            for later in range(1, N_RET):
                so_ref[later] = jnp.zeros_like(s_ref)


def _ret_log_gamma():
    return np.log1p(-np.exp2(-5.0 - np.arange(RET_HEADS, dtype=np.float64)))


def _ret_chunk_tables(cl):
    idx = np.arange(cl)
    causal = (idx[:, None] >= idx[None, :]).astype(np.float32)
    cd = np.broadcast_to(np.exp(_ret_log_gamma() * cl)[:, None, None], (RET_HEADS, 1, 128))
    return jnp.asarray(causal, F32), jnp.asarray(cd, F32)


def _ret_row_scales():
    c = np.concatenate([np.arange(T_PROMPT) % RET_CHUNK_PROMPT,
                        np.arange(T_SAMPLE) % RET_CHUNK_SAMPLE]).astype(np.float64)
    e = (c[:, None] + 1.0) * _ret_log_gamma()[None, :]
    return np.concatenate([np.exp(e), np.exp(-e) * RET_DK ** -0.5], axis=1)


def _ret_core_call(p, s0, states, layer, n_seq, seq_len, cl, row0):
    has_s0 = s0 is not None
    n_chunks = seq_len // cl
    rb0 = row0 // cl
    per = RET_SEQS_PER_STEP
    assert n_seq % per == 0
    states_blk = (per, RET_HEADS, RET_DK, RET_DV)
    in_specs = [pl.BlockSpec((cl, RET_IN), lambda b, c, q=q: (rb0 + (per * b + q) * n_chunks + c, 0))
                for q in range(per)]
    args = [p] * per
    if has_s0:
        in_specs.append(pl.BlockSpec((1,) + states_blk, lambda b, c: (layer, b, 0, 0, 0)))
        args.append(s0)
    in_specs += [_resident((cl, cl)), _resident((RET_HEADS, 1, 128))]
    args += list(_ret_chunk_tables(cl))
    if layer == 0:
        assert states is None
        state_spec = pl.BlockSpec((N_RET,) + states_blk, lambda b, c: (0, b, 0, 0, 0))
        aliases = {}
    else:
        in_specs.append(pl.BlockSpec(memory_space=pl.ANY))
        args.append(states)
        state_spec = pl.BlockSpec((1,) + states_blk, lambda b, c: (layer, b, 0, 0, 0))
        aliases = {len(args) - 1: 1}
    y, states = pl.pallas_call(
        functools.partial(_ret_core_kernel, has_s0, n_chunks, layer),
        grid=(n_seq // per, n_chunks),
        in_specs=in_specs,
        out_specs=[pl.BlockSpec((per, cl, RET_V), lambda b, c: (b, c, 0)), state_spec],
        out_shape=[jax.ShapeDtypeStruct((n_seq, seq_len, RET_V), BF16),
                   jax.ShapeDtypeStruct((N_RET, n_seq) + states_blk[1:], F32)],
        scratch_shapes=[pltpu.VMEM(states_blk, F32)],
        input_output_aliases=aliases,
        compiler_params=_params("parallel", "arbitrary"),
        name="ret_core",
    )(*args)
    return y.reshape(n_seq * seq_len, RET_V), states


def _residual_router(acc, x_ref, mod, gain_ref, wr_ref, br_ref, xo_ref, h2_ref, lg_ref):
    gate1 = mod[:, 2:3, :]
    xn = x_ref[...] + _per_group(acc, lambda a3: a3 * gate1)
    xo_ref[...] = xn
    h2 = _norm_mod(xn, gain_ref, mod, 3)
    h2_ref[...] = _pack_bf16_pairs(h2)
    hh = jnp.dot(h2.astype(BF16), wr_ref[...], preferred_element_type=F32)
    lg_ref[...] = hh[:, :ROUTER_LANES] + hh[:, ROUTER_LANES:] + br_ref[...]


def _mix_out_specs(rows):
    return [_row_spec(rows, D), _row_spec(rows, DP), _row_spec(rows, ROUTER_LANES)]


_MIX_OUT_SHAPE = [
    jax.ShapeDtypeStruct((T_ALL, D), F32),
    jax.ShapeDtypeStruct((T_ALL, DP), U32),
    jax.ShapeDtypeStruct((T_ALL, ROUTER_LANES), F32),
]


def _router_specs():
    return [_resident((D, 2 * ROUTER_LANES)), _resident((1, ROUTER_LANES))]


def _ret_out_kernel(n_prompt_tiles, layer, yp_ref, ys_ref, w_hbm, x_ref, modp_ref, mods_ref, gain_ref,
                    wr_ref, br_ref, xo_ref, h2_ref, lg_ref, w_s, w_stage, w_sems):
    _load_weight_bf16(w_hbm, layer, w_s, w_stage, w_sems)
    is_prompt = pl.program_id(0) < n_prompt_tiles
    yin = jnp.where(is_prompt, yp_ref[...], ys_ref[...])
    acc = jnp.dot(yin, w_s[...], preferred_element_type=F32)
    mod = _tile_mod((modp_ref, mods_ref), is_prompt)
    _residual_router(acc, x_ref, mod, gain_ref, wr_ref, br_ref, xo_ref, h2_ref, lg_ref)


def _ret_out_call(y_prompt, y_sample, w_out, layer, x, mod, gain, w_r, b_r):
    rows = ROWS_WIDE
    return pl.pallas_call(
        functools.partial(_ret_out_kernel, _n_prompt_tiles(rows), layer),
        grid=(_n_tiles(rows),),
        in_specs=[_prompt_rows_spec(rows, RET_V), _sample_rows_spec(rows, RET_V),
                  pl.BlockSpec(memory_space=pl.ANY), _row_spec(rows, D), *_mod_specs(rows),
                  _resident((1, D))] + _router_specs(),
        out_specs=_mix_out_specs(rows),
        out_shape=_MIX_OUT_SHAPE,
        scratch_shapes=_weight_scratch(RET_V, D),
        compiler_params=_params("arbitrary"),
        name="ret_out",
    )(y_prompt, y_sample, w_out, x, *mod, gain, w_r, b_r)


GM_TN = 512


_GELU_C = float(np.sqrt(2.0 / np.pi))


def _gelu_tanh(x):
    hx = 0.5 * x
    return hx * jnp.tanh(x * (_GELU_C + (_GELU_C * 0.044715) * (x * x))) + hx


def _gm_proj_kernel(n_tiles, n_prompt_tiles, layer, *refs):
    x_ref, yg0_ref, yg1_ref, gates_ref = refs[:4]
    gain_ref, w_hbm, b_ref, lg_ref, lb_ref = refs[8:13]
    (xo_ref, uv_ref, vs_ref, hb_cur, hb_prev, vraw_cur, vraw_prev, stat_prev, sum_s,
     w_s, w_stage, w_sems) = refs[-12:]
    _load_weight_bf16(w_hbm, layer, w_s, w_stage, w_sems)
    i = pl.program_id(0)
    rows = x_ref.shape[0]

    @pl.when(i == 0)
    def _():
        hb_prev[...] = jnp.zeros_like(hb_prev)
        vraw_prev[...] = jnp.zeros_like(vraw_prev)
        stat_prev[...] = jnp.zeros_like(stat_prev)

    def add_row(x, row_ref, lo, width):
        x3 = x.reshape(rows // 8, 8, width) + row_ref[:, lo:lo + width]
        return x3.reshape(rows, width)

    def mul_row(x, row_ref, lo, width):
        x3 = x.reshape(rows // 8, 8, width) * row_ref[:, lo:lo + width]
        return x3.reshape(rows, width)

    def proj_chunk(hb_ref, lo):
        z = jnp.dot(hb_ref[...], w_s[:, lo:lo + GM_TN], preferred_element_type=F32)
        return add_row(z, b_ref, lo, GM_TN).astype(BF16)

    def stage_b_chunk(lo):
        uv_ref[:, lo:lo + GM_TN] = _gelu_tanh(proj_chunk(hb_prev, lo))
        for k in range(lo, lo + GM_TN, 128):
            vk = vraw_prev[:, k:k + 128].astype(F32) * stat_prev[:, 0:128] + stat_prev[:, 128:256]
            vn = add_row(mul_row(vk, lg_ref, k, 128), lb_ref, k, 128)
            uv_ref[:, GM_HALF + k:GM_HALF + k + 128] = vn.astype(BF16)
            vs_ref[0, :, k:k + 128] = vn

    def stage_a_chunk(n, lo):
        gz = _gelu_tanh(proj_chunk(hb_cur, GM_HALF + lo))
        vraw_cur[:, lo:lo + GM_TN] = gz
        gf = gz.astype(F32)
        pieces = [gf[:, k:k + 128] for k in range(0, GM_TN, 128)]
        t1 = functools.reduce(lambda p, q: p + q, pieces)
        t2 = functools.reduce(lambda p, q: p + q, [p * p for p in pieces])
        if n == 0:
            sum_s[:, 0:128] = t1
            sum_s[:, 128:256] = t2
        else:
            sum_s[:, 0:128] += t1
            sum_s[:, 128:256] += t2

    chunks = list(range(0, GM_HALF, GM_TN))
    stage_b_chunk(chunks[0])
    a_is_prompt = jnp.minimum(i, n_tiles - 1) < n_prompt_tiles
    x = _add_moe(x_ref[...], yg0_ref, yg1_ref, gates_ref, _tile_mod(refs[4:6], a_is_prompt))
    xo_ref[...] = x
    hb_cur[...] = _norm_mod(x, gain_ref, _tile_mod(refs[6:8], a_is_prompt), 0).astype(BF16)
    for n, lo in enumerate(chunks):
        stage_a_chunk(n, lo)
        if 0 < n < len(chunks) - 1:
            stage_b_chunk(lo)
    mu = jnp.sum(sum_s[:, 0:128], axis=-1, keepdims=True) * (1.0 / GM_HALF)
    var = jnp.sum(sum_s[:, 128:256], axis=-1, keepdims=True) * (1.0 / GM_HALF) - mu * mu
    rstd = lax.rsqrt(var + EPS)
    stage_b_chunk(chunks[-1])
    stat_prev[:, 0:128] = jnp.broadcast_to(rstd, (rows, 128))
    stat_prev[:, 128:256] = jnp.broadcast_to(-mu * rstd, (rows, 128))
    hb_prev[...] = hb_cur[...]
    vraw_prev[...] = vraw_cur[...]

    if layer == 0:
        @pl.when(i - 1 >= n_prompt_tiles)
        def _():
            for later in range(1, N_GM):
                vs_ref[later] = jnp.zeros((rows, GM_HALF), F32)


def _gm_proj_call(x, prev, mod, gain, w_in, layer, b_in, ln_g, ln_b, vs_all):
    rows = ROWS_GM_PROJ
    n, npt = _n_tiles(rows), _n_prompt_tiles(rows)

    def stage_a(spec):
        return _stage_a_spec(spec, n)

    stage_b = _stage_b_spec
    in_specs = [stage_a(s) for s in [_row_spec(rows, D)] + _prev_specs(rows) + _mod_specs(rows)] + [
        _resident((1, D)), pl.BlockSpec(memory_space=pl.ANY), _resident((8, GM_FFN)),
        _resident((8, GM_HALF)), _resident((8, GM_HALF))]
    rows8 = [jnp.broadcast_to(r.reshape(1, -1), (8, r.size)) for r in (b_in, ln_g, ln_b)]
    args = [x, *prev, *mod, gain, w_in, *rows8]
    if layer == 0:
        assert vs_all is None
        vs_spec = pl.BlockSpec((N_GM, rows, GM_HALF), lambda i: (0, jnp.maximum(i - 1 - npt, 0), 0))
        aliases = {}
    else:
        in_specs.append(pl.BlockSpec(memory_space=pl.ANY))
        args.append(vs_all)
        vs_spec = pl.BlockSpec((1, rows, GM_HALF), lambda i: (layer, jnp.maximum(i - 1 - npt, 0), 0))
        aliases = {len(args) - 1: 2}
    return pl.pallas_call(
        functools.partial(_gm_proj_kernel, n, npt, layer),
        grid=(n + 1,),
        in_specs=in_specs,
        out_specs=[stage_a(_row_spec(rows, D)), stage_b(_row_spec(rows, GM_FFN)), vs_spec],
        out_shape=[jax.ShapeDtypeStruct((T_ALL, D), F32),
                   jax.ShapeDtypeStruct((T_ALL, GM_FFN), BF16),
                   jax.ShapeDtypeStruct((N_GM, T_SAMPLE, GM_HALF), F32)],
        scratch_shapes=[pltpu.VMEM((rows, D), BF16), pltpu.VMEM((rows, D), BF16),
                        pltpu.VMEM((rows, GM_HALF), BF16), pltpu.VMEM((rows, GM_HALF), BF16),
                        pltpu.VMEM((rows, 256), F32), pltpu.VMEM((rows, 256), F32),
                        *_weight_scratch(D, GM_FFN)],
        input_output_aliases=aliases,
        compiler_params=_params("arbitrary"),
        name="gm_proj",
    )(*args)


def _gm_out_kernel(n_prompt_tiles, layer, uv_ref, ws_ref, bs_ref, w_hbm, bo_ref, x_ref, modp_ref,
                   mods_ref, gain_ref, wr_ref, br_ref, xo_ref, h2_ref, lg_ref, w_s, w_stage, w_sems):
    _load_weight_bf16(w_hbm, layer, w_s, w_stage, w_sems)
    rows = uv_ref.shape[0]
    mod = _tile_mod((modp_ref, mods_ref), pl.program_id(0) < n_prompt_tiles)
    pieces = []
    for r0 in range(0, rows, GM_MIX):
        acc = jnp.zeros((GM_MIX, D), F32)
        for g in range(GM_GROUPS):
            lo = g * GM_GDIM
            sp = jnp.dot(ws_ref[0, g], uv_ref[r0:r0 + GM_MIX, GM_HALF + lo:GM_HALF + lo + GM_GDIM],
                         preferred_element_type=F32) + bs_ref[0, g]
            gated = (uv_ref[r0:r0 + GM_MIX, lo:lo + GM_GDIM].astype(F32) * sp).astype(BF16)
            acc = acc + jnp.dot(gated, w_s[lo:lo + GM_GDIM, :], preferred_element_type=F32)
        pieces.append(acc)
    acc = jnp.concatenate(pieces, axis=0) + bo_ref[...]
    _residual_router(acc, x_ref, mod, gain_ref, wr_ref, br_ref, xo_ref, h2_ref, lg_ref)


def _gm_block_diag(w_s, b_s):
    mats, biases = [], []
    for cl in (GM_CHUNK, DEC_SEQ):
        tri = jnp.tril(jnp.ones((cl, cl), bool))
        blk = jnp.where(tri[None], w_s[:, :cl, :cl], 0.0)
        reps = GM_MIX // cl
        eye = jnp.eye(reps, dtype=w_s.dtype)
        bd = jnp.einsum("ab,gts->gatbs", eye, blk).reshape(GM_GROUPS, GM_MIX, GM_MIX)
        mats.append(bd)
        biases.append(jnp.tile(b_s[:, :cl], (1, reps))[:, :, None])
    return jnp.stack(mats).astype(BF16), jnp.stack(biases).astype(F32)


def _gm_out_call(uv, ws_bd, bs_bd, w_out, layer, b_out, x, mod, gain, w_r, b_r):
    rows = ROWS_WIDE
    npt = _n_prompt_tiles(rows)

    def variant(i):
        return jnp.where(i >= npt, 1, 0)

    return pl.pallas_call(
        functools.partial(_gm_out_kernel, npt, layer),
        grid=(_n_tiles(rows),),
        in_specs=[_row_spec(rows, GM_FFN),
                  pl.BlockSpec((1, GM_GROUPS, GM_MIX, GM_MIX), lambda i: (variant(i), 0, 0, 0)),
                  pl.BlockSpec((1, GM_GROUPS, GM_MIX, 1), lambda i: (variant(i), 0, 0, 0)),
                  pl.BlockSpec(memory_space=pl.ANY), _resident((1, D)), _row_spec(rows, D),
                  *_mod_specs(rows), _resident((1, D))] + _router_specs(),
        out_specs=_mix_out_specs(rows),
        out_shape=_MIX_OUT_SHAPE,
        scratch_shapes=_weight_scratch(GM_HALF, D),
        compiler_params=_params("arbitrary"),
        name="gm_out",
    )(uv, ws_bd, bs_bd, w_out, b_out, x, *mod, gain, w_r, b_r)


def _expert_kernel(layer, be_ref, bv_ref, nb_ref, nx_ref, xb_ref, w1_hbm, w3_hbm, w2_hbm, yb_ref,
                   st1, st3, st2, w1s, w3s, w2s, sems, slot_ref):
    b = pl.program_id(0)

    def weight_copies(e, slot):
        return (pltpu.make_async_copy(w1_hbm.at[layer, e], st1.at[slot], sems.at[slot, 0]),
                pltpu.make_async_copy(w3_hbm.at[layer, e], st3.at[slot], sems.at[slot, 1]),
                pltpu.make_async_copy(w2_hbm.at[layer, e], st2.at[slot], sems.at[slot, 2]))

    @pl.when(b == 0)
    def _():
        slot_ref[0] = 0
        for cp in weight_copies(be_ref[0], 0):
            cp.start()

    @pl.when(b < nb_ref[0])
    def _():
        prev_e = be_ref[jnp.maximum(b - 1, 0)]

        @pl.when((b == 0) | (be_ref[b] != prev_e))
        def _():
            slot = slot_ref[0]
            for cp in weight_copies(be_ref[b], slot):
                cp.wait()
            w1s[...] = st1[slot].astype(BF16)
            w3s[...] = st3[slot].astype(BF16)
            w2s[...] = st2[slot].astype(BF16)

            @pl.when(nx_ref[b] >= 0)
            def _():
                for cp in weight_copies(nx_ref[b], 1 - slot):
                    cp.start()

            slot_ref[0] = 1 - slot

        valid = bv_ref[b]

        def run_rows(n):
            row = lax.broadcasted_iota(jnp.int32, (n, 1), 0)
            xw = jnp.where(row < valid, xb_ref[0:n], U32(0))
            x = _unpack_bf16_pairs(xw).astype(BF16)
            a = jnp.dot(x, w1s[...], preferred_element_type=F32)
            c = jnp.dot(x, w3s[...], preferred_element_type=F32)
            h = (_silu(a) * c).astype(BF16)
            yb_ref[0:n] = _pack_bf16_pairs(jnp.dot(h, w2s[...], preferred_element_type=F32))
            if n < EXP_BLOCK:
                yb_ref[n:EXP_BLOCK] = jnp.zeros((EXP_BLOCK - n, DP), U32)

        for n in range(EXP_ROW_STEP, EXP_BLOCK + 1, EXP_ROW_STEP):
            pl.when((valid > n - EXP_ROW_STEP) & (valid <= n))(functools.partial(run_rows, n))


def _expert_call(blk_e, blk_valid, n_blk, blk_next, xb, w1, w3, w2, layer):
    def blk(b, be, bv, nb, nx):
        return (jnp.minimum(b, nb[0] - 1), 0)

    up, down = (D, MOE_HIDDEN), (MOE_HIDDEN, D)
    grid_spec = pltpu.PrefetchScalarGridSpec(
        num_scalar_prefetch=4,
        grid=(N_EXP_BLOCKS,),
        in_specs=[pl.BlockSpec((EXP_BLOCK, DP), blk)] + [pl.BlockSpec(memory_space=pl.ANY)] * 3,
        out_specs=pl.BlockSpec((EXP_BLOCK, DP), blk),
        scratch_shapes=[pltpu.VMEM((2,) + up, F32), pltpu.VMEM((2,) + up, F32), pltpu.VMEM((2,) + down, F32),
                        pltpu.VMEM(up, BF16), pltpu.VMEM(up, BF16), pltpu.VMEM(down, BF16),
                        pltpu.SemaphoreType.DMA((2, 3)), pltpu.SMEM((1,), jnp.int32)],
    )
    return pl.pallas_call(
        functools.partial(_expert_kernel, layer),
        grid_spec=grid_spec,
        out_shape=jax.ShapeDtypeStruct((P_ROWS, DP), U32),
        compiler_params=_params("arbitrary"),
        name="experts",
    )(blk_e, blk_valid, n_blk, blk_next, xb, w1, w3, w2)


def _route_kernel(lg_ref, dest_ref, gates_ref, meta_ref, cnt_ref):
    t = pl.program_id(0)

    @pl.when(t == 0)
    def _():
        cnt_ref[...] = jnp.zeros_like(cnt_ref)

    pl.when(t < N_ROUTE_TILES)(functools.partial(_route_tile, t, lg_ref, dest_ref, gates_ref, cnt_ref))
    pl.when(t == N_ROUTE_TILES)(functools.partial(_route_finish, dest_ref, meta_ref, cnt_ref))


ROUTE_RANK_BITS = 16
assert N_ASSIGN < 2 ** ROUTE_RANK_BITS


def _route_tile(t, lg_ref, dest_ref, gates_ref, cnt_ref):
    tm = ROUTE_TM
    lt = lg_ref[...].T
    el = lt[0:MOE_EXPERTS]
    gl = lt[MOE_EXPERTS:MOE_EXPERTS + 8]
    gidx = lax.broadcasted_iota(jnp.int32, (8, tm), 0)
    neg = jnp.float32(-jnp.inf)
    gl = jnp.where(gidx < MOE_GROUPS, gl, neg)
    gmax = jnp.max(gl, axis=0, keepdims=True)
    grp = jnp.min(jnp.where(gl == gmax, gidx, MOE_GROUPS), axis=0, keepdims=True)
    eidx = lax.broadcasted_iota(jnp.int32, (MOE_EXPERTS, tm), 0)
    els = jnp.where((eidx >> 3) == grp, el, neg)
    m1 = jnp.max(els, axis=0, keepdims=True)
    i1 = jnp.min(jnp.where(els == m1, eidx, MOE_EXPERTS), axis=0, keepdims=True)
    els2 = jnp.where(eidx == i1, neg, els)
    m2 = jnp.max(els2, axis=0, keepdims=True)
    i2 = jnp.min(jnp.where(els2 == m2, eidx, MOE_EXPERTS), axis=0, keepdims=True)
    sel1 = eidx == i1
    sel2 = eidx == i2
    cnt = jnp.where(sel1 | sel2, 1.0, 0.0)

    lane = ROUTER_LANES
    before = (lax.broadcasted_iota(jnp.int32, (lane, lane), 0)
              < lax.broadcasted_iota(jnp.int32, (lane, lane), 1))
    tri = jnp.where(before, 1.0, 0.0).astype(BF16)
    run = cnt_ref[...]
    r1, r2 = [], []
    for k in range(tm // lane):
        piece = slice(k * lane, (k + 1) * lane)
        ck = cnt[:, piece]
        pos = run + jnp.dot(ck.astype(BF16), tri, preferred_element_type=F32)
        r1.append(jnp.sum(jnp.where(sel1[:, piece], pos, 0.0), axis=0, keepdims=True))
        r2.append(jnp.sum(jnp.where(sel2[:, piece], pos, 0.0), axis=0, keepdims=True))
        run = run + jnp.sum(ck, axis=1, keepdims=True)
    cnt_ref[...] = run
    rank = jnp.concatenate([jnp.concatenate(r1, axis=1), jnp.concatenate(r2, axis=1)], axis=0)
    eid = jnp.concatenate([i1, i2], axis=0)
    dest_ref[:, pl.ds(pl.multiple_of(t * tm, tm), tm)] = (eid << ROUTE_RANK_BITS) + rank.astype(jnp.int32)

    g_w = 1.0 / jnp.sum(jnp.exp(gl - gmax), axis=0, keepdims=True)
    e21 = jnp.exp(m2 - m1)
    p1 = 1.0 / (1.0 + e21)
    rid = lax.broadcasted_iota(jnp.int32, (ROUTER_LANES, tm), 0)
    gt = jnp.where(rid == 0, g_w * p1, jnp.where(rid == 1, g_w * (e21 * p1), 0.0))
    gates_ref[...] = gt.T


def _route_finish(dest_ref, meta_ref, cnt_ref):
    counts = cnt_ref[...]
    nblk = jnp.floor((counts + (EXP_BLOCK - 1.0)) * (1.0 / EXP_BLOCK))
    r = lax.broadcasted_iota(jnp.int32, (MOE_EXPERTS, MOE_EXPERTS), 0)
    c = lax.broadcasted_iota(jnp.int32, (MOE_EXPERTS, MOE_EXPERTS), 1)
    nblk_row = jnp.sum(jnp.where(r == c, nblk, 0.0), axis=0, keepdims=True)
    bstart = jnp.sum(jnp.where(c < r, nblk_row, 0.0), axis=1, keepdims=True)
    bend = bstart + nblk
    bidx = lax.broadcasted_iota(jnp.int32, (1, META_LANES), 1).astype(F32)
    blk_e = jnp.minimum(jnp.sum(jnp.where(bidx >= bend, 1.0, 0.0), axis=0, keepdims=True),
                        MOE_EXPERTS - 1.0)
    erow = lax.broadcasted_iota(jnp.int32, (MOE_EXPERTS, META_LANES), 0).astype(F32)
    mine = erow == blk_e
    cnt_b = jnp.sum(jnp.where(mine, counts, 0.0), axis=0, keepdims=True)
    start_b = jnp.sum(jnp.where(mine, bstart, 0.0), axis=0, keepdims=True)
    valid = jnp.clip(cnt_b - (bidx - start_b) * EXP_BLOCK, 0.0, float(EXP_BLOCK))
    n_blk = jnp.sum(nblk, axis=0, keepdims=True)
    end_b = jnp.sum(jnp.where(mine, bend, 0.0), axis=0, keepdims=True)
    nxt = jnp.minimum(jnp.sum(jnp.where(end_b >= bend, 1.0, 0.0), axis=0, keepdims=True),
                      MOE_EXPERTS - 1.0)
    nxt = jnp.where(end_b < n_blk, nxt, -1.0)
    mrow = lax.broadcasted_iota(jnp.int32, (8, META_LANES), 0)
    meta = jnp.where(mrow == 0, blk_e, jnp.where(mrow == 1, valid, jnp.where(
        mrow == 2, n_blk, jnp.where(mrow == 3, nxt, 0.0))))
    meta_ref[...] = meta.astype(jnp.int32)

    base = (bstart * EXP_BLOCK).astype(jnp.int32)
    packed = dest_ref[...]
    eid = packed >> ROUTE_RANK_BITS
    row = packed & (2 ** ROUTE_RANK_BITS - 1)
    for e in range(MOE_EXPERTS):
        row = row + jnp.where(eid == e, base[e:e + 1, :], 0)
    dest_ref[...] = row


def _route_call(logits):
    last = N_ROUTE_TILES - 1
    return pl.pallas_call(
        _route_kernel,
        grid=(N_ROUTE_TILES + 1,),
        in_specs=[pl.BlockSpec((ROUTE_TM, ROUTER_LANES), lambda t: (jnp.minimum(t, last), 0))],
        out_specs=[pl.BlockSpec((MOE_TOPK, T_ALL), lambda t: (0, 0)),
                   pl.BlockSpec((ROUTE_TM, ROUTER_LANES), lambda t: (jnp.minimum(t, last), 0)),
                   pl.BlockSpec((8, META_LANES), lambda t: (0, 0))],
        out_shape=[jax.ShapeDtypeStruct((MOE_TOPK, T_ALL), jnp.int32),
                   jax.ShapeDtypeStruct((T_ALL, ROUTER_LANES), F32),
                   jax.ShapeDtypeStruct((8, META_LANES), jnp.int32)],
        scratch_shapes=[pltpu.VMEM((MOE_EXPERTS, 1), F32)],
        compiler_params=_params("arbitrary"),
        name="route",
    )(logits)


def _sc_mesh():
    return plsc.VectorSubcoreMesh(core_axis_name="c", subcore_axis_name="s")


def _sc_token_offset(j):
    wid = lax.axis_index("s") * SC_CORES + lax.axis_index("c")
    return pl.multiple_of(wid * SC_ROWS_PER_WORKER + j * SC_CHUNK, 8)


SC_N_CHUNKS = SC_ROWS_PER_WORKER // SC_CHUNK
assert SC_N_CHUNKS % 2 == 1


def _start(copies):
    for cp in copies:
        cp.start()


def _wait(copies):
    for cp in copies:
        cp.wait()


def _dispatch_body(h_hbm, d0_hbm, d1_hbm, out_hbm, i0a, i1a, rows_a, i0b, i1b, rows_b, la, lb, sa, sb):
    sets = {"a": (i0a, i1a, rows_a, la, sa), "b": (i0b, i1b, rows_b, lb, sb)}

    def loads(j, s):
        i0, i1, rows, lsem, _ = sets[s]
        src = pl.ds(_sc_token_offset(j), SC_CHUNK)
        return (pltpu.make_async_copy(d0_hbm.at[src], i0, lsem),
                pltpu.make_async_copy(d1_hbm.at[src], i1, lsem),
                pltpu.make_async_copy(h_hbm.at[src], rows, lsem))

    def scatters(s):
        i0, i1, rows, _, ssem = sets[s]
        return (pltpu.make_async_copy(rows, out_hbm.at[i0], ssem),
                pltpu.make_async_copy(rows, out_hbm.at[i1], ssem))

    _start(loads(0, "a"))

    @pl.loop(0, SC_N_CHUNKS // 2)
    def _(p):
        a = 2 * p
        _start(loads(a + 1, "b"))
        _wait(loads(a, "a"))
        _start(scatters("a"))
        _wait(loads(a + 1, "b"))
        _start(scatters("b"))
        _wait(scatters("a"))
        _start(loads(a + 2, "a"))
        _wait(scatters("b"))

    _wait(loads(SC_N_CHUNKS - 1, "a"))
    _start(scatters("a"))
    _wait(scatters("a"))


def _dispatch_call(h2, dest0, dest1):
    buffers = [pltpu.VMEM((SC_CHUNK,), jnp.int32), pltpu.VMEM((SC_CHUNK,), jnp.int32),
               pltpu.VMEM((SC_CHUNK, DP), U32)]
    return pl.kernel(
        _dispatch_body,
        out_type=jax.ShapeDtypeStruct((P_ROWS, DP), U32),
        mesh=_sc_mesh(),
        scratch_types=buffers + buffers + [pltpu.SemaphoreType.DMA] * 4,
        name="moe_dispatch",
    )(h2, dest0, dest1)


def _combine_body(yb_hbm, d0_hbm, d1_hbm, o0_hbm, o1_hbm, i0a, i1a, r0a, r1a, i0b, i1b, r0b, r1b,
                  ia, ib, ga, gb, wa, wb):
    sets = {"a": (i0a, i1a, r0a, r1a, ia, ga, wa), "b": (i0b, i1b, r0b, r1b, ib, gb, wb)}

    def index_loads(j, s):
        i0, i1, _, _, isem, _, _ = sets[s]
        src = pl.ds(_sc_token_offset(j), SC_CHUNK)
        return (pltpu.make_async_copy(d0_hbm.at[src], i0, isem), pltpu.make_async_copy(d1_hbm.at[src], i1, isem))

    def gathers(s):
        i0, i1, r0, r1, _, gsem, _ = sets[s]
        return (pltpu.make_async_copy(yb_hbm.at[i0], r0, gsem), pltpu.make_async_copy(yb_hbm.at[i1], r1, gsem))

    def writes(j, s):
        _, _, r0, r1, _, _, wsem = sets[s]
        dst = pl.ds(_sc_token_offset(j), SC_CHUNK)
        return (pltpu.make_async_copy(r0, o0_hbm.at[dst], wsem), pltpu.make_async_copy(r1, o1_hbm.at[dst], wsem))

    _start(index_loads(0, "a"))
    _wait(index_loads(0, "a"))
    _start(gathers("a"))

    @pl.loop(0, SC_N_CHUNKS // 2)
    def _(p):
        a = 2 * p
        _start(index_loads(a + 1, "b"))
        _wait(index_loads(a + 1, "b"))
        _wait(gathers("a"))
        _start(writes(a, "a"))
        _start(gathers("b"))
        _start(index_loads(a + 2, "a"))
        _wait(index_loads(a + 2, "a"))
        _wait(writes(a, "a"))
        _wait(gathers("b"))
        _start(writes(a + 1, "b"))
        _start(gathers("a"))
        _wait(writes(a + 1, "b"))

    _wait(gathers("a"))
    _start(writes(SC_N_CHUNKS - 1, "a"))
    _wait(writes(SC_N_CHUNKS - 1, "a"))


def _combine_call(yb, dest0, dest1):
    out = jax.ShapeDtypeStruct((T_ALL, DP), U32)
    buffers = [pltpu.VMEM((SC_CHUNK,), jnp.int32), pltpu.VMEM((SC_CHUNK,), jnp.int32),
               pltpu.VMEM((SC_CHUNK, DP), U32), pltpu.VMEM((SC_CHUNK, DP), U32)]
    return pl.kernel(
        _combine_body,
        out_type=(out, out),
        mesh=_sc_mesh(),
        scratch_types=buffers + buffers + [pltpu.SemaphoreType.DMA] * 6,
        name="moe_combine",
    )(yb, dest0, dest1)


def _moe_rows(h2, dest, meta, w1, w3, w2, layer):
    dest0, dest1 = dest[0], dest[1]
    xb = _dispatch_call(h2, dest0, dest1)
    yb = _expert_call(meta[0, :N_EXP_BLOCKS], meta[1, :N_EXP_BLOCKS], meta[2, :1], meta[3, :N_EXP_BLOCKS],
                      xb, w1, w3, w2, layer)
    return _combine_call(yb, dest0, dest1)


def _final_kernel(is_prompt, x_ref, yg0_ref, yg1_ref, gates_ref, modp_ref, mods_ref, gain_ref, o_ref):
    x = _add_moe(x_ref[...], yg0_ref, yg1_ref, gates_ref, _tile_mod((modp_ref, mods_ref), is_prompt))
    o_ref[...] = _rms(x) * gain_ref[...]


def _final_call(x, prev, gain, row0, n_rows):
    rows = ROWS_FINAL
    tile0 = row0 // rows
    is_prompt = row0 < T_PROMPT
    assert row0 + n_rows <= T_PROMPT or not is_prompt

    def tile(width):
        return pl.BlockSpec((rows, width), lambda i: (tile0 + i, 0))

    mod_specs = [pl.BlockSpec(s.block_shape, lambda i, m=s.index_map: m(tile0 + i)) for s in _mod_specs(rows)]
    return pl.pallas_call(
        functools.partial(_final_kernel, is_prompt),
        grid=(n_rows // rows,),
        in_specs=[tile(D), tile(DP), tile(DP), tile(ROUTER_LANES), *mod_specs, _resident((1, D))],
        out_specs=pl.BlockSpec((rows, D), lambda i: (i, 0)),
        out_shape=jax.ShapeDtypeStruct((n_rows, D), F32),
        compiler_params=_params("parallel"),
        name="final_norm",
    )(x, *prev, gain)


def _rope_table():
    pos = np.concatenate([np.tile(np.arange(SEQ), BATCH),
                          np.tile(PAST_LEN + np.arange(DEC_SEQ), DEC_BATCH)]).astype(np.float32)
    inv = (ROPE_BASE ** (-np.arange(ROPE_HALF, dtype=np.float32) / ROPE_HALF)).astype(np.float32)
    ang = (pos[:, None] * inv[None, :]).astype(np.float32).astype(np.float64)
    pad = np.zeros((T_ALL, ROPE_TABLE_LANES - 2 * ROPE_HALF - 2 * RET_HEADS))
    return jnp.asarray(np.concatenate([np.cos(ang), np.sin(ang), _ret_row_scales(), pad], axis=1), F32)


def kernel(x_prompt, x_sample, c_prompt, c_sample, state_ret, ada_w, ada_b, norm1_g, norm2_g, ret_w_in,
           ret_w_out, gm_w_in, gm_b_in, gm_ln_g, gm_ln_b, gm_w_s, gm_b_s, gm_w_out, gm_b_out, moe_w_rg,
           moe_b_rg, moe_w_re, moe_b_re, moe_w1, moe_w3, moe_w2, final_g):
    x = (x_prompt.reshape(T_PROMPT, D), x_sample.reshape(T_SAMPLE, D))
    c_all = jnp.concatenate([c_prompt, c_sample], axis=0)
    rope = _rope_table()

    mod_all = _ada_call(c_all, ada_w, ada_b).reshape(DEPTH, N_SEQ, 6, D)

    def layer_params(i):
        mod = (mod_all[i, :BATCH], mod_all[i, BATCH:])
        w_r = jnp.pad(jnp.concatenate([moe_w_re[i], moe_w_rg[i]], axis=1),
                      ((0, 0), (0, ROUTER_LANES - MOE_GROUPS - MOE_EXPERTS)))
        w_r_hi = w_r.astype(BF16)
        w_r_lo = (w_r - w_r_hi.astype(F32)).astype(BF16)
        w_r = jnp.concatenate([w_r_hi, w_r_lo], axis=1)
        b_r = jnp.pad(jnp.concatenate([moe_b_re[i].reshape(-1), moe_b_rg[i]]),
                      (0, ROUTER_LANES - MOE_GROUPS - MOE_EXPERTS)).reshape(1, ROUTER_LANES)
        return mod, w_r, b_r

    ret_prompt = ret_sample = gm_sample = None
    prev = None
    for i in range(DEPTH):
        j = i // 2
        mod, w_r, b_r = layer_params(i)
        g1 = norm1_g[i].reshape(1, D)
        g2 = norm2_g[i].reshape(1, D)
        if i % 2 == 0:
            x, p = _ret_proj_call(x, prev, mod, g1, ret_w_in, j, rope)
            y_p, ret_prompt = _ret_core_call(p, None, ret_prompt, j, BATCH, SEQ, RET_CHUNK_PROMPT, 0)
            y_s, ret_sample = _ret_core_call(p, state_ret, ret_sample, j, DEC_BATCH, DEC_SEQ,
                                             RET_CHUNK_SAMPLE, T_PROMPT)
            x, h2, logits = _ret_out_call(y_p, y_s, ret_w_out, j, x, mod, g2, w_r, b_r)
        else:
            x, uv, gm_sample = _gm_proj_call(x, prev, mod, g1, gm_w_in, j, gm_b_in[j], gm_ln_g[j],
                                             gm_ln_b[j], gm_sample)
            ws_bd, bs_bd = _gm_block_diag(gm_w_s[j], gm_b_s[j])
            x, h2, logits = _gm_out_call(uv, ws_bd, bs_bd, gm_w_out, j,
                                         gm_b_out[j].reshape(1, D), x, mod, g2, w_r, b_r)
        dest, gates, meta = _route_call(logits)
        yg0, yg1 = _moe_rows(h2, dest, meta, moe_w1, moe_w3, moe_w2, i)
        prev = (yg0, yg1, gates, *mod)

    fg = final_g.reshape(1, D)
    y_prompt = _final_call(x, prev, fg, 0, T_PROMPT).reshape(BATCH, SEQ, D)
    y_sample = _final_call(x, prev, fg, T_PROMPT, T_SAMPLE).reshape(DEC_BATCH, DEC_SEQ, D)
    return (y_prompt, y_sample, ret_prompt, ret_sample,
            gm_sample.reshape(N_GM, DEC_BATCH, DEC_SEQ, GM_HALF))
```

```python
import functools

import numpy as np
import jax
import jax.numpy as jnp
from jax import lax
from jax.experimental import pallas as pl
from jax.experimental.pallas import tpu as pltpu
from jax.experimental.pallas import tpu_sc as plsc

F32 = jnp.float32
BF16 = jnp.bfloat16
U32 = jnp.uint32

D = 1024
BATCH, SEQ = 4, 4096
DEC_BATCH, DEC_SEQ = 16, 64
PAST_LEN = 4096
DEPTH = 4
N_RET = (DEPTH + 1) // 2
N_GM = DEPTH // 2
N_SEQ = BATCH + DEC_BATCH

RET_HEADS, RET_DK, RET_DV = 4, 256, 512
RET_QK = RET_HEADS * RET_DK
RET_V = RET_HEADS * RET_DV
RET_IN = 2 * RET_QK + 2 * RET_V
ROPE_BASE = 10000.0
ROPE_HALF = RET_DK // 2
ROPE_TABLE_LANES = 3 * ROPE_HALF

GM_FFN = 6 * D
GM_HALF = GM_FFN // 2
GM_GROUPS = 4
GM_GDIM = GM_HALF // GM_GROUPS
GM_CHUNK = 128

MOE_GROUPS, MOE_PER_GROUP = 4, 8
MOE_EXPERTS = MOE_GROUPS * MOE_PER_GROUP
MOE_TOPK = 2
MOE_HIDDEN = 512
EPS = 1e-6

GROUP = DEC_SEQ
T_PROMPT = BATCH * SEQ
T_SAMPLE = DEC_BATCH * DEC_SEQ
T_ALL = T_PROMPT + T_SAMPLE
N_GROUPS = T_ALL // GROUP
ROWS_WIDE = 512
ROWS_GM_PROJ = 256
ROWS_FINAL = 1024
ROWS_RET_OUT = 1024

RET_CHUNK_PROMPT = 256
RET_CHUNK_SAMPLE = DEC_SEQ

GM_MIX = 256

EXP_BLOCK = 1024
EXP_ROW_STEP = 128
N_ASSIGN = T_ALL * MOE_TOPK
N_EXP_BLOCKS = -(-(N_ASSIGN + MOE_EXPERTS * (EXP_BLOCK - 1)) // EXP_BLOCK)
P_ROWS = N_EXP_BLOCKS * EXP_BLOCK
ROUTER_LANES = 128
ROUTE_TM = 1024
N_ROUTE_TILES = T_ALL // ROUTE_TM
META_LANES = 256
assert META_LANES >= N_EXP_BLOCKS

DP = D // 2
SC_CORES, SC_SUBCORES = 2, 16
SC_WORKERS = SC_CORES * SC_SUBCORES
SC_ROWS_PER_WORKER = T_ALL // SC_WORKERS
SC_CHUNK = 32
assert SC_ROWS_PER_WORKER % SC_CHUNK == 0 and SC_CHUNK % 8 == 0

V7X_VMEM_LIMIT_BYTES = 56 * 1024 * 1024


def _params(*sem):
    return pltpu.CompilerParams(dimension_semantics=sem, vmem_limit_bytes=V7X_VMEM_LIMIT_BYTES)


def _resident(shape):
    nd = len(shape)
    return pl.BlockSpec(shape, lambda *_: (0,) * nd, pipeline_mode=pl.Buffered(1))


WEIGHT_STAGE_BYTES = 3 * 1024 * 1024


def _weight_scratch(k, n):
    chunk = k
    while chunk * n * 4 > WEIGHT_STAGE_BYTES:
        assert chunk % 16 == 0
        chunk //= 2
    return [pltpu.VMEM((k, n), BF16), pltpu.VMEM((2, chunk, n), F32), pltpu.SemaphoreType.DMA((2,))]


def _load_weight_bf16(w_hbm, layer, w_s, stage, sems):
    k = w_s.shape[0]
    chunk = stage.shape[1]

    def copy(c):
        return pltpu.make_async_copy(w_hbm.at[layer, pl.ds(c * chunk, chunk)], stage.at[c % 2], sems.at[c % 2])

    @pl.when(pl.program_id(0) == 0)
    def _():
        copy(0).start()
        for c in range(k // chunk):
            if c + 1 < k // chunk:
                copy(c + 1).start()
            copy(c).wait()
            w_s[c * chunk:(c + 1) * chunk, :] = stage[c % 2].astype(BF16)


def _rms(x):
    return x * lax.rsqrt(jnp.mean(x * x, axis=-1, keepdims=True) + EPS)


def _silu(x):
    return x * jax.nn.sigmoid(x)


def _per_group(x2d, fn):
    rows = x2d.shape[0]
    return fn(x2d.reshape(rows // GROUP, GROUP, D)).reshape(rows, D)


def _tile_mod(mod_refs, is_prompt):
    modp_ref, mods_ref = mod_refs
    return jnp.where(is_prompt, jnp.broadcast_to(modp_ref[...], mods_ref.shape), mods_ref[...])


def _norm_mod(x, gain_ref, mod, shift_idx):
    y = _rms(x) * gain_ref[...]
    scale = mod[:, shift_idx + 1:shift_idx + 2, :]
    shift = mod[:, shift_idx:shift_idx + 1, :]
    return _per_group(y, lambda y3: y3 * (1.0 + scale) + shift)


def _pack_bf16_pairs(x):
    lo = lax.bitcast_convert_type(x[:, :DP].astype(BF16).astype(F32), U32)
    hi = lax.bitcast_convert_type(x[:, DP:].astype(BF16).astype(F32), U32)
    return (lo >> 16) | (hi & U32(0xFFFF0000))


def _unpack_bf16_pairs(w):
    lo = lax.bitcast_convert_type(w << 16, F32)
    hi = lax.bitcast_convert_type(w & U32(0xFFFF0000), F32)
    return jnp.concatenate([lo, hi], axis=1)


def _add_moe(x, yg0_ref, yg1_ref, gates_ref, mod_prev):
    g = gates_ref[...]
    y = g[:, 0:1] * _unpack_bf16_pairs(yg0_ref[...]) + g[:, 1:2] * _unpack_bf16_pairs(yg1_ref[...])
    gate2 = mod_prev[:, 5:6, :]
    return x + _per_group(y, lambda y3: y3 * gate2)


ADA_TN = 1536


def _ada_kernel(c_ref, w_ref, b_ref, o_ref):
    c = c_ref[...]
    s = _silu(c).astype(BF16)
    o_ref[0] = jnp.dot(s, w_ref[0].astype(BF16), preferred_element_type=F32) + b_ref[0]


def _ada_call(c_all, ada_w, ada_b):
    return pl.pallas_call(
        _ada_kernel,
        grid=(DEPTH, 6 * D // ADA_TN),
        in_specs=[
            pl.BlockSpec((N_SEQ, D), lambda i, j: (0, 0)),
            pl.BlockSpec((1, D, ADA_TN), lambda i, j: (i, 0, j)),
            pl.BlockSpec((1, 1, ADA_TN), lambda i, j: (i, 0, j)),
        ],
        out_specs=pl.BlockSpec((1, N_SEQ, ADA_TN), lambda i, j: (i, 0, j)),
        out_shape=jax.ShapeDtypeStruct((DEPTH, N_SEQ, 6 * D), F32),
        compiler_params=_params("parallel", "parallel"),
        name="ada_modulation",
    )(c_all, ada_w, ada_b.reshape(DEPTH, 1, 6 * D))


def _n_tiles(rows):
    return T_ALL // rows


def _n_prompt_tiles(rows):
    return T_PROMPT // rows


def _row_spec(rows, width):
    return pl.BlockSpec((rows, width), lambda i: (i, 0))


def _mod_specs(rows):
    npt = _n_prompt_tiles(rows)
    return [pl.BlockSpec((1, 6, D), lambda i: (jnp.minimum(i * rows // SEQ, BATCH - 1), 0, 0)),
            pl.BlockSpec((rows // DEC_SEQ, 6, D), lambda i: (jnp.maximum(i - npt, 0), 0, 0))]


def _prev_specs(rows):
    return [_row_spec(rows, DP), _row_spec(rows, DP), _row_spec(rows, ROUTER_LANES)] + _mod_specs(rows)


def _stage_a_spec(spec, n_tiles):
    return pl.BlockSpec(spec.block_shape, lambda i, m=spec.index_map: m(jnp.minimum(i, n_tiles - 1)))


def _stage_b_spec(spec):
    return pl.BlockSpec(spec.block_shape, lambda i, m=spec.index_map: m(jnp.maximum(i - 1, 0)))


def _prompt_rows_spec(rows, width):
    last = _n_prompt_tiles(rows) - 1
    return pl.BlockSpec((rows, width), lambda i: (jnp.minimum(i, last), 0))


def _sample_rows_spec(rows, width):
    npt = _n_prompt_tiles(rows)
    return pl.BlockSpec((rows, width), lambda i: (jnp.maximum(i - npt, 0), 0))


def _ret_proj_kernel(has_prev, n_tiles, n_prompt_tiles, layer, *refs):
    i = pl.program_id(0)
    is_prompt = jnp.minimum(i, n_tiles - 1) < n_prompt_tiles
    head = refs[:6] if has_prev else refs[:2]
    refs = refs[len(head):]
    mod_refs = refs[:2]
    gain_ref, w_hbm, rope_ref, xo_ref, p_ref, hb_cur, hb_prev, w_s, w_stage, w_sems = refs[2:]
    _load_weight_bf16(w_hbm, layer, w_s, w_stage, w_sems)

    @pl.when(i == 0)
    def _():
        hb_prev[...] = jnp.zeros_like(hb_prev)

    def qk_head(j):
        lo = j * RET_DK
        acc = jnp.dot(hb_prev[...], w_s[:, lo:lo + RET_DK], preferred_element_type=F32)
        x1 = acc[:, :ROPE_HALF]
        x2 = acc[:, ROPE_HALF:]
        cos = rope_ref[:, 0:ROPE_HALF]
        sin = rope_ref[:, ROPE_HALF:2 * ROPE_HALF]
        scale = rope_ref[:, 2 * ROPE_HALF + j:2 * ROPE_HALF + j + 1]
        p_ref[:, lo:lo + ROPE_HALF] = ((x1 * cos - x2 * sin) * scale).astype(BF16)
        p_ref[:, lo + ROPE_HALF:lo + RET_DK] = ((x1 * sin + x2 * cos) * scale).astype(BF16)

    def vg_head(j):
        lo = 2 * RET_QK + j * RET_DV
        acc = jnp.dot(hb_prev[...], w_s[:, lo:lo + RET_DV], preferred_element_type=F32)
        if j >= RET_HEADS:
            acc = _silu(acc)
        p_ref[:, lo:lo + RET_DV] = acc.astype(BF16)

    vg_head(0)
    vg_head(1)
    if has_prev:
        x_ref, yg0_ref, yg1_ref, gates_ref = head[:4]
        x = _add_moe(x_ref[...], yg0_ref, yg1_ref, gates_ref, _tile_mod(head[4:6], is_prompt))
    else:
        x = jnp.where(is_prompt, head[0][...], head[1][...])
    xo_ref[...] = x
    hb_cur[...] = _norm_mod(x, gain_ref, _tile_mod(mod_refs, is_prompt), 0).astype(BF16)
    for j in range(2, 2 * RET_HEADS):
        vg_head(j)
    for j in range(2 * RET_HEADS):
        qk_head(j)
    hb_prev[...] = hb_cur[...]


def _ret_proj_call(x, prev, mod, gain, w_in, layer, rope):
    rows = ROWS_WIDE
    n = _n_tiles(rows)
    has_prev = prev is not None
    if has_prev:
        in_specs = [_row_spec(rows, D)] + _prev_specs(rows)
        args = [x] + list(prev)
    else:
        in_specs = [_prompt_rows_spec(rows, D), _sample_rows_spec(rows, D)]
        args = list(x)
    in_specs = [_stage_a_spec(s, n) for s in in_specs + _mod_specs(rows)] + [
        _resident((1, D)), pl.BlockSpec(memory_space=pl.ANY),
        _stage_b_spec(_row_spec(rows, ROPE_TABLE_LANES))]
    args += [*mod, gain, w_in, rope]
    return pl.pallas_call(
        functools.partial(_ret_proj_kernel, has_prev, n, _n_prompt_tiles(rows), layer),
        grid=(n + 1,),
        in_specs=in_specs,
        out_specs=[_stage_a_spec(_row_spec(rows, D), n), _stage_b_spec(_row_spec(rows, RET_IN))],
        out_shape=[jax.ShapeDtypeStruct((T_ALL, D), F32), jax.ShapeDtypeStruct((T_ALL, RET_IN), BF16)],
        scratch_shapes=[pltpu.VMEM((rows, D), BF16), pltpu.VMEM((rows, D), BF16),
                        *_weight_scratch(D, RET_IN)],
        compiler_params=_params("arbitrary"),
        name="ret_proj",
    )(*args)


RET_SEQS_PER_STEP = 2


def _ret_core_kernel(has_s0, n_chunks, layer, *refs):
    refs = list(refs)
    p_refs = [refs.pop(0) for _ in range(RET_SEQS_PER_STEP)]
    s0_ref = refs.pop(0) if has_s0 else None
    causal_ref, cd_ref = refs[:2]
    y_ref, so_ref, s_ref = refs[-3:]
    c = pl.program_id(1)

    @pl.when(c == 0)
    def _():
        if has_s0:
            s_ref[...] = s0_ref[0]
        else:
            s_ref[...] = jnp.zeros_like(s_ref)

    for h in range(RET_HEADS):
        for q, p_ref in enumerate(p_refs):
            qb = p_ref[:, h * RET_DK:(h + 1) * RET_DK]
            kb = p_ref[:, RET_QK + h * RET_DK:RET_QK + (h + 1) * RET_DK]
            vb = p_ref[:, 2 * RET_QK + h * RET_DV:2 * RET_QK + (h + 1) * RET_DV]
            gb = p_ref[:, 2 * RET_QK + RET_V + h * RET_DV:2 * RET_QK + RET_V + (h + 1) * RET_DV]
            scores = lax.dot_general(qb, kb, (((1,), (1,)), ((), ())), preferred_element_type=F32)
            scores = scores * causal_ref[...]
            s_old = s_ref[q, h]
            o = (jnp.dot(scores.astype(BF16), vb, preferred_element_type=F32)
                 + jnp.dot(qb, s_old.astype(BF16), preferred_element_type=F32))
            s_ref[q, h] = cd_ref[h][:, 0:1] * (s_old + lax.dot_general(
                kb, vb, (((0,), (0,)), ((), ())), preferred_element_type=F32))
            y_ref[q, :, h * RET_DV:(h + 1) * RET_DV] = (gb.astype(F32) * _rms(o)).astype(BF16)

    @pl.when(c == n_chunks - 1)
    def _():
        so_ref[0] = s_ref[...]
        if layer == 0:
            for later in range(1, N_RET):
                so_ref[later] = jnp.zeros_like(s_ref)


def _ret_log_gamma():
    return np.log1p(-np.exp2(-5.0 - np.arange(RET_HEADS, dtype=np.float64)))


def _ret_chunk_tables(cl):
    idx = np.arange(cl)
    causal = (idx[:, None] >= idx[None, :]).astype(np.float32)
    cd = np.broadcast_to(np.exp(_ret_log_gamma() * cl)[:, None, None], (RET_HEADS, 1, 128))
    return jnp.asarray(causal, F32), jnp.asarray(cd, F32)


def _ret_row_scales():
    c = np.concatenate([np.arange(T_PROMPT) % RET_CHUNK_PROMPT,
                        np.arange(T_SAMPLE) % RET_CHUNK_SAMPLE]).astype(np.float64)
    e = (c[:, None] + 1.0) * _ret_log_gamma()[None, :]
    return np.concatenate([np.exp(e), np.exp(-e) * RET_DK ** -0.5], axis=1)


def _ret_core_call(p, s0, states, layer, n_seq, seq_len, cl, row0):
    has_s0 = s0 is not None
    n_chunks = seq_len // cl
    rb0 = row0 // cl
    per = RET_SEQS_PER_STEP
    assert n_seq % per == 0
    states_blk = (per, RET_HEADS, RET_DK, RET_DV)
    in_specs = [pl.BlockSpec((cl, RET_IN), lambda b, c, q=q: (rb0 + (per * b + q) * n_chunks + c, 0))
                for q in range(per)]
    args = [p] * per
    if has_s0:
        in_specs.append(pl.BlockSpec((1,) + states_blk, lambda b, c: (layer, b, 0, 0, 0)))
        args.append(s0)
    in_specs += [_resident((cl, cl)), _resident((RET_HEADS, 1, 128))]
    args += list(_ret_chunk_tables(cl))
    if layer == 0:
        assert states is None
        state_spec = pl.BlockSpec((N_RET,) + states_blk, lambda b, c: (0, b, 0, 0, 0))
        aliases = {}
    else:
        in_specs.append(pl.BlockSpec(memory_space=pl.ANY))
        args.append(states)
        state_spec = pl.BlockSpec((1,) + states_blk, lambda b, c: (layer, b, 0, 0, 0))
        aliases = {len(args) - 1: 1}
    y, states = pl.pallas_call(
        functools.partial(_ret_core_kernel, has_s0, n_chunks, layer),
        grid=(n_seq // per, n_chunks),
        in_specs=in_specs,
        out_specs=[pl.BlockSpec((per, cl, RET_V), lambda b, c: (b, c, 0)), state_spec],
        out_shape=[jax.ShapeDtypeStruct((n_seq, seq_len, RET_V), BF16),
                   jax.ShapeDtypeStruct((N_RET, n_seq) + states_blk[1:], F32)],
        scratch_shapes=[pltpu.VMEM(states_blk, F32)],
        input_output_aliases=aliases,
        compiler_params=_params("parallel", "arbitrary"),
        name="ret_core",
    )(*args)
    return y.reshape(n_seq * seq_len, RET_V), states


def _residual_router(acc, x_ref, mod, gain_ref, wr_ref, br_ref, xo_ref, h2_ref, lg_ref):
    gate1 = mod[:, 2:3, :]
    xn = x_ref[...] + _per_group(acc, lambda a3: a3 * gate1)
    xo_ref[...] = xn
    h2 = _norm_mod(xn, gain_ref, mod, 3)
    h2_ref[...] = _pack_bf16_pairs(h2)
    hh = jnp.dot(h2.astype(BF16), wr_ref[...], preferred_element_type=F32)
    lg_ref[...] = hh[:, :ROUTER_LANES] + hh[:, ROUTER_LANES:] + br_ref[...]


def _mix_out_specs(rows):
    return [_row_spec(rows, D), _row_spec(rows, DP), _row_spec(rows, ROUTER_LANES)]


_MIX_OUT_SHAPE = [
    jax.ShapeDtypeStruct((T_ALL, D), F32),
    jax.ShapeDtypeStruct((T_ALL, DP), U32),
    jax.ShapeDtypeStruct((T_ALL, ROUTER_LANES), F32),
]


def _router_specs():
    return [_resident((D, 2 * ROUTER_LANES)), _resident((1, ROUTER_LANES))]


def _ret_out_kernel(n_prompt_tiles, layer, yp_ref, ys_ref, w_hbm, x_ref, modp_ref, mods_ref, gain_ref,
                    wr_ref, br_ref, xo_ref, h2_ref, lg_ref, w_s, w_stage, w_sems):
    _load_weight_bf16(w_hbm, layer, w_s, w_stage, w_sems)
    is_prompt = pl.program_id(0) < n_prompt_tiles
    yin = jnp.where(is_prompt, yp_ref[...], ys_ref[...])
    acc = jnp.dot(yin, w_s[...], preferred_element_type=F32)
    mod = _tile_mod((modp_ref, mods_ref), is_prompt)
    _residual_router(acc, x_ref, mod, gain_ref, wr_ref, br_ref, xo_ref, h2_ref, lg_ref)


def _ret_out_call(y_prompt, y_sample, w_out, layer, x, mod, gain, w_r, b_r):
    rows = ROWS_RET_OUT
    return pl.pallas_call(
        functools.partial(_ret_out_kernel, _n_prompt_tiles(rows), layer),
        grid=(_n_tiles(rows),),
        in_specs=[_prompt_rows_spec(rows, RET_V), _sample_rows_spec(rows, RET_V),
                  pl.BlockSpec(memory_space=pl.ANY), _row_spec(rows, D), *_mod_specs(rows),
                  _resident((1, D))] + _router_specs(),
        out_specs=_mix_out_specs(rows),
        out_shape=_MIX_OUT_SHAPE,
        scratch_shapes=_weight_scratch(RET_V, D),
        compiler_params=_params("arbitrary"),
        name="ret_out",
    )(y_prompt, y_sample, w_out, x, *mod, gain, w_r, b_r)


GM_TN = 512


_GELU_C = float(np.sqrt(2.0 / np.pi))


def _gelu_tanh(x):
    hx = 0.5 * x
    return hx * jnp.tanh(x * (_GELU_C + (_GELU_C * 0.044715) * (x * x))) + hx


def _gm_proj_kernel(n_tiles, n_prompt_tiles, layer, *refs):
    x_ref, yg0_ref, yg1_ref, gates_ref = refs[:4]
    gain_ref, w_hbm, b_ref, lg_ref, lb_ref = refs[8:13]
    (xo_ref, uv_ref, vs_ref, hb_cur, hb_prev, vraw_cur, vraw_prev, stat_prev, sum_s,
     w_s, w_stage, w_sems) = refs[-12:]
    _load_weight_bf16(w_hbm, layer, w_s, w_stage, w_sems)
    i = pl.program_id(0)
    rows = x_ref.shape[0]

    @pl.when(i == 0)
    def _():
        hb_prev[...] = jnp.zeros_like(hb_prev)
        vraw_prev[...] = jnp.zeros_like(vraw_prev)
        stat_prev[...] = jnp.zeros_like(stat_prev)

    def add_row(x, row_ref, lo, width):
        x3 = x.reshape(rows // 8, 8, width) + row_ref[:, lo:lo + width]
        return x3.reshape(rows, width)

    def mul_row(x, row_ref, lo, width):
        x3 = x.reshape(rows // 8, 8, width) * row_ref[:, lo:lo + width]
        return x3.reshape(rows, width)

    def proj_chunk(hb_ref, lo):
        z = jnp.dot(hb_ref[...], w_s[:, lo:lo + GM_TN], preferred_element_type=F32)
        return add_row(z, b_ref, lo, GM_TN).astype(BF16)

    def stage_b_chunk(lo):
        uv_ref[:, lo:lo + GM_TN] = _gelu_tanh(proj_chunk(hb_prev, lo))
        for k in range(lo, lo + GM_TN, 128):
            vk = vraw_prev[:, k:k + 128].astype(F32) * stat_prev[:, 0:128] + stat_prev[:, 128:256]
            vn = add_row(mul_row(vk, lg_ref, k, 128), lb_ref, k, 128)
            uv_ref[:, GM_HALF + k:GM_HALF + k + 128] = vn.astype(BF16)
            vs_ref[0, :, k:k + 128] = vn

    def stage_a_chunk(n, lo):
        gz = _gelu_tanh(proj_chunk(hb_cur, GM_HALF + lo))
        vraw_cur[:, lo:lo + GM_TN] = gz
        gf = gz.astype(F32)
        pieces = [gf[:, k:k + 128] for k in range(0, GM_TN, 128)]
        t1 = functools.reduce(lambda p, q: p + q, pieces)
        t2 = functools.reduce(lambda p, q: p + q, [p * p for p in pieces])
        if n == 0:
            sum_s[:, 0:128] = t1
            sum_s[:, 128:256] = t2
        else:
            sum_s[:, 0:128] += t1
            sum_s[:, 128:256] += t2

    chunks = list(range(0, GM_HALF, GM_TN))
    stage_b_chunk(chunks[0])
    a_is_prompt = jnp.minimum(i, n_tiles - 1) < n_prompt_tiles
    x = _add_moe(x_ref[...], yg0_ref, yg1_ref, gates_ref, _tile_mod(refs[4:6], a_is_prompt))
    xo_ref[...] = x
    hb_cur[...] = _norm_mod(x, gain_ref, _tile_mod(refs[6:8], a_is_prompt), 0).astype(BF16)
    for n, lo in enumerate(chunks):
        stage_a_chunk(n, lo)
        if 0 < n < len(chunks) - 1:
            stage_b_chunk(lo)
    mu = jnp.sum(sum_s[:, 0:128], axis=-1, keepdims=True) * (1.0 / GM_HALF)
    var = jnp.sum(sum_s[:, 128:256], axis=-1, keepdims=True) * (1.0 / GM_HALF) - mu * mu
    rstd = lax.rsqrt(var + EPS)
    stage_b_chunk(chunks[-1])
    stat_prev[:, 0:128] = jnp.broadcast_to(rstd, (rows, 128))
    stat_prev[:, 128:256] = jnp.broadcast_to(-mu * rstd, (rows, 128))
    hb_prev[...] = hb_cur[...]
    vraw_prev[...] = vraw_cur[...]

    if layer == 0:
        @pl.when(i - 1 >= n_prompt_tiles)
        def _():
            for later in range(1, N_GM):
                vs_ref[later] = jnp.zeros((rows, GM_HALF), F32)


def _gm_proj_call(x, prev, mod, gain, w_in, layer, b_in, ln_g, ln_b, vs_all):
    rows = ROWS_GM_PROJ
    n, npt = _n_tiles(rows), _n_prompt_tiles(rows)

    def stage_a(spec):
        return _stage_a_spec(spec, n)

    stage_b = _stage_b_spec
    in_specs = [stage_a(s) for s in [_row_spec(rows, D)] + _prev_specs(rows) + _mod_specs(rows)] + [
        _resident((1, D)), pl.BlockSpec(memory_space=pl.ANY), _resident((8, GM_FFN)),
        _resident((8, GM_HALF)), _resident((8, GM_HALF))]
    rows8 = [jnp.broadcast_to(r.reshape(1, -1), (8, r.size)) for r in (b_in, ln_g, ln_b)]
    args = [x, *prev, *mod, gain, w_in, *rows8]
    if layer == 0:
        assert vs_all is None
        vs_spec = pl.BlockSpec((N_GM, rows, GM_HALF), lambda i: (0, jnp.maximum(i - 1 - npt, 0), 0))
        aliases = {}
    else:
        in_specs.append(pl.BlockSpec(memory_space=pl.ANY))
        args.append(vs_all)
        vs_spec = pl.BlockSpec((1, rows, GM_HALF), lambda i: (layer, jnp.maximum(i - 1 - npt, 0), 0))
        aliases = {len(args) - 1: 2}
    return pl.pallas_call(
        functools.partial(_gm_proj_kernel, n, npt, layer),
        grid=(n + 1,),
        in_specs=in_specs,
        out_specs=[stage_a(_row_spec(rows, D)), stage_b(_row_spec(rows, GM_FFN)), vs_spec],
        out_shape=[jax.ShapeDtypeStruct((T_ALL, D), F32),
                   jax.ShapeDtypeStruct((T_ALL, GM_FFN), BF16),
                   jax.ShapeDtypeStruct((N_GM, T_SAMPLE, GM_HALF), F32)],
        scratch_shapes=[pltpu.VMEM((rows, D), BF16), pltpu.VMEM((rows, D), BF16),
                        pltpu.VMEM((rows, GM_HALF), BF16), pltpu.VMEM((rows, GM_HALF), BF16),
                        pltpu.VMEM((rows, 256), F32), pltpu.VMEM((rows, 256), F32),
                        *_weight_scratch(D, GM_FFN)],
        input_output_aliases=aliases,
        compiler_params=_params("arbitrary"),
        name="gm_proj",
    )(*args)


def _gm_out_kernel(n_prompt_tiles, layer, uv_ref, ws_ref, bs_ref, w_hbm, bo_ref, x_ref, modp_ref,
                   mods_ref, gain_ref, wr_ref, br_ref, xo_ref, h2_ref, lg_ref, w_s, w_stage, w_sems):
    _load_weight_bf16(w_hbm, layer, w_s, w_stage, w_sems)
    rows = uv_ref.shape[0]
    mod = _tile_mod((modp_ref, mods_ref), pl.program_id(0) < n_prompt_tiles)
    pieces = []
    for r0 in range(0, rows, GM_MIX):
        acc = jnp.zeros((GM_MIX, D), F32)
        for g in range(GM_GROUPS):
            lo = g * GM_GDIM
            sp = jnp.dot(ws_ref[0, g], uv_ref[r0:r0 + GM_MIX, GM_HALF + lo:GM_HALF + lo + GM_GDIM],
                         preferred_element_type=F32) + bs_ref[0, g]
            gated = (uv_ref[r0:r0 + GM_MIX, lo:lo + GM_GDIM].astype(F32) * sp).astype(BF16)
            acc = acc + jnp.dot(gated, w_s[lo:lo + GM_GDIM, :], preferred_element_type=F32)
        pieces.append(acc)
    acc = jnp.concatenate(pieces, axis=0) + bo_ref[...]
    _residual_router(acc, x_ref, mod, gain_ref, wr_ref, br_ref, xo_ref, h2_ref, lg_ref)


def _gm_block_diag(w_s, b_s):
    mats, biases = [], []
    for cl in (GM_CHUNK, DEC_SEQ):
        tri = jnp.tril(jnp.ones((cl, cl), bool))
        blk = jnp.where(tri[None], w_s[:, :cl, :cl], 0.0)
        reps = GM_MIX // cl
        eye = jnp.eye(reps, dtype=w_s.dtype)
        bd = jnp.einsum("ab,gts->gatbs", eye, blk).reshape(GM_GROUPS, GM_MIX, GM_MIX)
        mats.append(bd)
        biases.append(jnp.tile(b_s[:, :cl], (1, reps))[:, :, None])
    return jnp.stack(mats).astype(BF16), jnp.stack(biases).astype(F32)


def _gm_out_call(uv, ws_bd, bs_bd, w_out, layer, b_out, x, mod, gain, w_r, b_r):
    rows = ROWS_WIDE
    npt = _n_prompt_tiles(rows)

    def variant(i):
        return jnp.where(i >= npt, 1, 0)

    return pl.pallas_call(
        functools.partial(_gm_out_kernel, npt, layer),
        grid=(_n_tiles(rows),),
        in_specs=[_row_spec(rows, GM_FFN),
                  pl.BlockSpec((1, GM_GROUPS, GM_MIX, GM_MIX), lambda i: (variant(i), 0, 0, 0)),
                  pl.BlockSpec((1, GM_GROUPS, GM_MIX, 1), lambda i: (variant(i), 0, 0, 0)),
                  pl.BlockSpec(memory_space=pl.ANY), _resident((1, D)), _row_spec(rows, D),
                  *_mod_specs(rows), _resident((1, D))] + _router_specs(),
        out_specs=_mix_out_specs(rows),
        out_shape=_MIX_OUT_SHAPE,
        scratch_shapes=_weight_scratch(GM_HALF, D),
        compiler_params=_params("arbitrary"),
        name="gm_out",
    )(uv, ws_bd, bs_bd, w_out, b_out, x, *mod, gain, w_r, b_r)


def _expert_kernel(layer, be_ref, bv_ref, nb_ref, nx_ref, xb_ref, w1_hbm, w3_hbm, w2_hbm, yb_ref,
                   st1, st3, st2, w1s, w3s, w2s, sems, slot_ref):
    b = pl.program_id(0)

    def weight_copies(e, slot):
        return (pltpu.make_async_copy(w1_hbm.at[layer, e], st1.at[slot], sems.at[slot, 0]),
                pltpu.make_async_copy(w3_hbm.at[layer, e], st3.at[slot], sems.at[slot, 1]),
                pltpu.make_async_copy(w2_hbm.at[layer, e], st2.at[slot], sems.at[slot, 2]))

    @pl.when(b == 0)
    def _():
        slot_ref[0] = 0
        for cp in weight_copies(be_ref[0], 0):
            cp.start()

    @pl.when(b < nb_ref[0])
    def _():
        prev_e = be_ref[jnp.maximum(b - 1, 0)]

        @pl.when((b == 0) | (be_ref[b] != prev_e))
        def _():
            slot = slot_ref[0]
            for cp in weight_copies(be_ref[b], slot):
                cp.wait()
            w1s[...] = st1[slot].astype(BF16)
            w3s[...] = st3[slot].astype(BF16)
            w2s[...] = st2[slot].astype(BF16)

            @pl.when(nx_ref[b] >= 0)
            def _():
                for cp in weight_copies(nx_ref[b], 1 - slot):
                    cp.start()

            slot_ref[0] = 1 - slot

        valid = bv_ref[b]

        def run_rows(n):
            row = lax.broadcasted_iota(jnp.int32, (n, 1), 0)
            xw = jnp.where(row < valid, xb_ref[0:n], U32(0))
            x = _unpack_bf16_pairs(xw).astype(BF16)
            a = jnp.dot(x, w1s[...], preferred_element_type=F32)
            c = jnp.dot(x, w3s[...], preferred_element_type=F32)
            h = (_silu(a) * c).astype(BF16)
            yb_ref[0:n] = _pack_bf16_pairs(jnp.dot(h, w2s[...], preferred_element_type=F32))
            if n < EXP_BLOCK:
                yb_ref[n:EXP_BLOCK] = jnp.zeros((EXP_BLOCK - n, DP), U32)

        for n in range(EXP_ROW_STEP, EXP_BLOCK + 1, EXP_ROW_STEP):
            pl.when((valid > n - EXP_ROW_STEP) & (valid <= n))(functools.partial(run_rows, n))


def _expert_call(blk_e, blk_valid, n_blk, blk_next, xb, w1, w3, w2, layer):
    def blk(b, be, bv, nb, nx):
        return (jnp.minimum(b, nb[0] - 1), 0)

    up, down = (D, MOE_HIDDEN), (MOE_HIDDEN, D)
    grid_spec = pltpu.PrefetchScalarGridSpec(
        num_scalar_prefetch=4,
        grid=(N_EXP_BLOCKS,),
        in_specs=[pl.BlockSpec((EXP_BLOCK, DP), blk)] + [pl.BlockSpec(memory_space=pl.ANY)] * 3,
        out_specs=pl.BlockSpec((EXP_BLOCK, DP), blk),
        scratch_shapes=[pltpu.VMEM((2,) + up, F32), pltpu.VMEM((2,) + up, F32), pltpu.VMEM((2,) + down, F32),
                        pltpu.VMEM(up, BF16), pltpu.VMEM(up, BF16), pltpu.VMEM(down, BF16),
                        pltpu.SemaphoreType.DMA((2, 3)), pltpu.SMEM((1,), jnp.int32)],
    )
    return pl.pallas_call(
        functools.partial(_expert_kernel, layer),
        grid_spec=grid_spec,
        out_shape=jax.ShapeDtypeStruct((P_ROWS, DP), U32),
        compiler_params=_params("arbitrary"),
        name="experts",
    )(blk_e, blk_valid, n_blk, blk_next, xb, w1, w3, w2)


def _route_kernel(lg_ref, dest_ref, gates_ref, meta_ref, cnt_ref):
    t = pl.program_id(0)

    @pl.when(t == 0)
    def _():
        cnt_ref[...] = jnp.zeros_like(cnt_ref)

    pl.when(t < N_ROUTE_TILES)(functools.partial(_route_tile, t, lg_ref, dest_ref, gates_ref, cnt_ref))
    pl.when(t == N_ROUTE_TILES)(functools.partial(_route_finish, dest_ref, meta_ref, cnt_ref))


ROUTE_RANK_BITS = 16
assert N_ASSIGN < 2 ** ROUTE_RANK_BITS


def _route_tile(t, lg_ref, dest_ref, gates_ref, cnt_ref):
    tm = ROUTE_TM
    lt = lg_ref[...].T
    el = lt[0:MOE_EXPERTS]
    gl = lt[MOE_EXPERTS:MOE_EXPERTS + 8]
    gidx = lax.broadcasted_iota(jnp.int32, (8, tm), 0)
    neg = jnp.float32(-jnp.inf)
    gl = jnp.where(gidx < MOE_GROUPS, gl, neg)
    gmax = jnp.max(gl, axis=0, keepdims=True)
    grp = jnp.min(jnp.where(gl == gmax, gidx, MOE_GROUPS), axis=0, keepdims=True)
    eidx = lax.broadcasted_iota(jnp.int32, (MOE_EXPERTS, tm), 0)
    els = jnp.where((eidx >> 3) == grp, el, neg)
    m1 = jnp.max(els, axis=0, keepdims=True)
    i1 = jnp.min(jnp.where(els == m1, eidx, MOE_EXPERTS), axis=0, keepdims=True)
    els2 = jnp.where(eidx == i1, neg, els)
    m2 = jnp.max(els2, axis=0, keepdims=True)
    i2 = jnp.min(jnp.where(els2 == m2, eidx, MOE_EXPERTS), axis=0, keepdims=True)
    sel1 = eidx == i1
    sel2 = eidx == i2
    cnt = jnp.where(sel1 | sel2, 1.0, 0.0)

    lane = ROUTER_LANES
    before = (lax.broadcasted_iota(jnp.int32, (lane, lane), 0)
              < lax.broadcasted_iota(jnp.int32, (lane, lane), 1))
    tri = jnp.where(before, 1.0, 0.0).astype(BF16)
    run = cnt_ref[...]
    r1, r2 = [], []
    for k in range(tm // lane):
        piece = slice(k * lane, (k + 1) * lane)
        ck = cnt[:, piece]
        pos = run + jnp.dot(ck.astype(BF16), tri, preferred_element_type=F32)
        r1.append(jnp.sum(jnp.where(sel1[:, piece], pos, 0.0), axis=0, keepdims=True))
        r2.append(jnp.sum(jnp.where(sel2[:, piece], pos, 0.0), axis=0, keepdims=True))
        run = run + jnp.sum(ck, axis=1, keepdims=True)
    cnt_ref[...] = run
    rank = jnp.concatenate([jnp.concatenate(r1, axis=1), jnp.concatenate(r2, axis=1)], axis=0)
    eid = jnp.concatenate([i1, i2], axis=0)
    dest_ref[:, pl.ds(pl.multiple_of(t * tm, tm), tm)] = (eid << ROUTE_RANK_BITS) + rank.astype(jnp.int32)

    g_w = 1.0 / jnp.sum(jnp.exp(gl - gmax), axis=0, keepdims=True)
    e21 = jnp.exp(m2 - m1)
    p1 = 1.0 / (1.0 + e21)
    rid = lax.broadcasted_iota(jnp.int32, (ROUTER_LANES, tm), 0)
    gt = jnp.where(rid == 0, g_w * p1, jnp.where(rid == 1, g_w * (e21 * p1), 0.0))
    gates_ref[...] = gt.T


def _route_finish(dest_ref, meta_ref, cnt_ref):
    counts = cnt_ref[...]
    nblk = jnp.floor((counts + (EXP_BLOCK - 1.0)) * (1.0 / EXP_BLOCK))
    r = lax.broadcasted_iota(jnp.int32, (MOE_EXPERTS, MOE_EXPERTS), 0)
    c = lax.broadcasted_iota(jnp.int32, (MOE_EXPERTS, MOE_EXPERTS), 1)
    nblk_row = jnp.sum(jnp.where(r == c, nblk, 0.0), axis=0, keepdims=True)
    bstart = jnp.sum(jnp.where(c < r, nblk_row, 0.0), axis=1, keepdims=True)
    bend = bstart + nblk
    bidx = lax.broadcasted_iota(jnp.int32, (1, META_LANES), 1).astype(F32)
    blk_e = jnp.minimum(jnp.sum(jnp.where(bidx >= bend, 1.0, 0.0), axis=0, keepdims=True),
                        MOE_EXPERTS - 1.0)
    erow = lax.broadcasted_iota(jnp.int32, (MOE_EXPERTS, META_LANES), 0).astype(F32)
    mine = erow == blk_e
    cnt_b = jnp.sum(jnp.where(mine, counts, 0.0), axis=0, keepdims=True)
    start_b = jnp.sum(jnp.where(mine, bstart, 0.0), axis=0, keepdims=True)
    valid = jnp.clip(cnt_b - (bidx - start_b) * EXP_BLOCK, 0.0, float(EXP_BLOCK))
    n_blk = jnp.sum(nblk, axis=0, keepdims=True)
    end_b = jnp.sum(jnp.where(mine, bend, 0.0), axis=0, keepdims=True)
    nxt = jnp.minimum(jnp.sum(jnp.where(end_b >= bend, 1.0, 0.0), axis=0, keepdims=True),
                      MOE_EXPERTS - 1.0)
    nxt = jnp.where(end_b < n_blk, nxt, -1.0)
    mrow = lax.broadcasted_iota(jnp.int32, (8, META_LANES), 0)
    meta = jnp.where(mrow == 0, blk_e, jnp.where(mrow == 1, valid, jnp.where(
        mrow == 2, n_blk, jnp.where(mrow == 3, nxt, 0.0))))
    meta_ref[...] = meta.astype(jnp.int32)

    base = (bstart * EXP_BLOCK).astype(jnp.int32)
    packed = dest_ref[...]
    eid = packed >> ROUTE_RANK_BITS
    row = packed & (2 ** ROUTE_RANK_BITS - 1)
    for e in range(MOE_EXPERTS):
        row = row + jnp.where(eid == e, base[e:e + 1, :], 0)
    dest_ref[...] = row


def _route_call(logits):
    last = N_ROUTE_TILES - 1
    return pl.pallas_call(
        _route_kernel,
        grid=(N_ROUTE_TILES + 1,),
        in_specs=[pl.BlockSpec((ROUTE_TM, ROUTER_LANES), lambda t: (jnp.minimum(t, last), 0))],
        out_specs=[pl.BlockSpec((MOE_TOPK, T_ALL), lambda t: (0, 0)),
                   pl.BlockSpec((ROUTE_TM, ROUTER_LANES), lambda t: (jnp.minimum(t, last), 0)),
                   pl.BlockSpec((8, META_LANES), lambda t: (0, 0))],
        out_shape=[jax.ShapeDtypeStruct((MOE_TOPK, T_ALL), jnp.int32),
                   jax.ShapeDtypeStruct((T_ALL, ROUTER_LANES), F32),
                   jax.ShapeDtypeStruct((8, META_LANES), jnp.int32)],
        scratch_shapes=[pltpu.VMEM((MOE_EXPERTS, 1), F32)],
        compiler_params=_params("arbitrary"),
        name="route",
    )(logits)


def _sc_mesh():
    return plsc.VectorSubcoreMesh(core_axis_name="c", subcore_axis_name="s")


def _sc_token_offset(j):
    wid = lax.axis_index("s") * SC_CORES + lax.axis_index("c")
    return pl.multiple_of(wid * SC_ROWS_PER_WORKER + j * SC_CHUNK, 8)


SC_N_CHUNKS = SC_ROWS_PER_WORKER // SC_CHUNK
assert SC_N_CHUNKS % 2 == 1


def _start(copies):
    for cp in copies:
        cp.start()


def _wait(copies):
    for cp in copies:
        cp.wait()


def _dispatch_body(h_hbm, d0_hbm, d1_hbm, out_hbm, i0a, i1a, rows_a, i0b, i1b, rows_b, la, lb, sa, sb):
    sets = {"a": (i0a, i1a, rows_a, la, sa), "b": (i0b, i1b, rows_b, lb, sb)}

    def loads(j, s):
        i0, i1, rows, lsem, _ = sets[s]
        src = pl.ds(_sc_token_offset(j), SC_CHUNK)
        return (pltpu.make_async_copy(d0_hbm.at[src], i0, lsem),
                pltpu.make_async_copy(d1_hbm.at[src], i1, lsem),
                pltpu.make_async_copy(h_hbm.at[src], rows, lsem))

    def scatters(s):
        i0, i1, rows, _, ssem = sets[s]
        return (pltpu.make_async_copy(rows, out_hbm.at[i0], ssem),
                pltpu.make_async_copy(rows, out_hbm.at[i1], ssem))

    _start(loads(0, "a"))

    @pl.loop(0, SC_N_CHUNKS // 2)
    def _(p):
        a = 2 * p
        _start(loads(a + 1, "b"))
        _wait(loads(a, "a"))
        _start(scatters("a"))
        _wait(loads(a + 1, "b"))
        _start(scatters("b"))
        _wait(scatters("a"))
        _start(loads(a + 2, "a"))
        _wait(scatters("b"))

    _wait(loads(SC_N_CHUNKS - 1, "a"))
    _start(scatters("a"))
    _wait(scatters("a"))


def _dispatch_call(h2, dest0, dest1):
    buffers = [pltpu.VMEM((SC_CHUNK,), jnp.int32), pltpu.VMEM((SC_CHUNK,), jnp.int32),
               pltpu.VMEM((SC_CHUNK, DP), U32)]
    return pl.kernel(
        _dispatch_body,
        out_type=jax.ShapeDtypeStruct((P_ROWS, DP), U32),
        mesh=_sc_mesh(),
        scratch_types=buffers + buffers + [pltpu.SemaphoreType.DMA] * 4,
        name="moe_dispatch",
    )(h2, dest0, dest1)


def _combine_body(yb_hbm, d0_hbm, d1_hbm, o0_hbm, o1_hbm, i0a, i1a, r0a, r1a, i0b, i1b, r0b, r1b,
                  ia, ib, ga, gb, wa, wb):
    sets = {"a": (i0a, i1a, r0a, r1a, ia, ga, wa), "b": (i0b, i1b, r0b, r1b, ib, gb, wb)}

    def index_loads(j, s):
        i0, i1, _, _, isem, _, _ = sets[s]
        src = pl.ds(_sc_token_offset(j), SC_CHUNK)
        return (pltpu.make_async_copy(d0_hbm.at[src], i0, isem), pltpu.make_async_copy(d1_hbm.at[src], i1, isem))

    def gathers(s):
        i0, i1, r0, r1, _, gsem, _ = sets[s]
        return (pltpu.make_async_copy(yb_hbm.at[i0], r0, gsem), pltpu.make_async_copy(yb_hbm.at[i1], r1, gsem))

    def writes(j, s):
        _, _, r0, r1, _, _, wsem = sets[s]
        dst = pl.ds(_sc_token_offset(j), SC_CHUNK)
        return (pltpu.make_async_copy(r0, o0_hbm.at[dst], wsem), pltpu.make_async_copy(r1, o1_hbm.at[dst], wsem))

    _start(index_loads(0, "a"))
    _wait(index_loads(0, "a"))
    _start(gathers("a"))

    @pl.loop(0, SC_N_CHUNKS // 2)
    def _(p):
        a = 2 * p
        _start(index_loads(a + 1, "b"))
        _wait(index_loads(a + 1, "b"))
        _wait(gathers("a"))
        _start(writes(a, "a"))
        _start(gathers("b"))
        _start(index_loads(a + 2, "a"))
        _wait(index_loads(a + 2, "a"))
        _wait(writes(a, "a"))
        _wait(gathers("b"))
        _start(writes(a + 1, "b"))
        _start(gathers("a"))
        _wait(writes(a + 1, "b"))

    _wait(gathers("a"))
    _start(writes(SC_N_CHUNKS - 1, "a"))
    _wait(writes(SC_N_CHUNKS - 1, "a"))


def _combine_call(yb, dest0, dest1):
    out = jax.ShapeDtypeStruct((T_ALL, DP), U32)
    buffers = [pltpu.VMEM((SC_CHUNK,), jnp.int32), pltpu.VMEM((SC_CHUNK,), jnp.int32),
               pltpu.VMEM((SC_CHUNK, DP), U32), pltpu.VMEM((SC_CHUNK, DP), U32)]
    return pl.kernel(
        _combine_body,
        out_type=(out, out),
        mesh=_sc_mesh(),
        scratch_types=buffers + buffers + [pltpu.SemaphoreType.DMA] * 6,
        name="moe_combine",
    )(yb, dest0, dest1)


def _moe_rows(h2, dest, meta, w1, w3, w2, layer):
    dest0, dest1 = dest[0], dest[1]
    xb = _dispatch_call(h2, dest0, dest1)
    yb = _expert_call(meta[0, :N_EXP_BLOCKS], meta[1, :N_EXP_BLOCKS], meta[2, :1], meta[3, :N_EXP_BLOCKS],
                      xb, w1, w3, w2, layer)
    return _combine_call(yb, dest0, dest1)


def _final_kernel(is_prompt, x_ref, yg0_ref, yg1_ref, gates_ref, modp_ref, mods_ref, gain_ref, o_ref):
    x = _add_moe(x_ref[...], yg0_ref, yg1_ref, gates_ref, _tile_mod((modp_ref, mods_ref), is_prompt))
    o_ref[...] = _rms(x) * gain_ref[...]


def _final_call(x, prev, gain, row0, n_rows):
    rows = ROWS_FINAL
    tile0 = row0 // rows
    is_prompt = row0 < T_PROMPT
    assert row0 + n_rows <= T_PROMPT or not is_prompt

    def tile(width):
        return pl.BlockSpec((rows, width), lambda i: (tile0 + i, 0))

    mod_specs = [pl.BlockSpec(s.block_shape, lambda i, m=s.index_map: m(tile0 + i)) for s in _mod_specs(rows)]
    return pl.pallas_call(
        functools.partial(_final_kernel, is_prompt),
        grid=(n_rows // rows,),
        in_specs=[tile(D), tile(DP), tile(DP), tile(ROUTER_LANES), *mod_specs, _resident((1, D))],
        out_specs=pl.BlockSpec((rows, D), lambda i: (i, 0)),
        out_shape=jax.ShapeDtypeStruct((n_rows, D), F32),
        compiler_params=_params("parallel"),
        name="final_norm",
    )(x, *prev, gain)


def _rope_table():
    pos = np.concatenate([np.tile(np.arange(SEQ), BATCH),
                          np.tile(PAST_LEN + np.arange(DEC_SEQ), DEC_BATCH)]).astype(np.float32)
    inv = (ROPE_BASE ** (-np.arange(ROPE_HALF, dtype=np.float32) / ROPE_HALF)).astype(np.float32)
    ang = (pos[:, None] * inv[None, :]).astype(np.float32).astype(np.float64)
    pad = np.zeros((T_ALL, ROPE_TABLE_LANES - 2 * ROPE_HALF - 2 * RET_HEADS))
    return jnp.asarray(np.concatenate([np.cos(ang), np.sin(ang), _ret_row_scales(), pad], axis=1), F32)


def kernel(x_prompt, x_sample, c_prompt, c_sample, state_ret, ada_w, ada_b, norm1_g, norm2_g, ret_w_in,
           ret_w_out, gm_w_in, gm_b_in, gm_ln_g, gm_ln_b, gm_w_s, gm_b_s, gm_w_out, gm_b_out, moe_w_rg,
           moe_b_rg, moe_w_re, moe_b_re, moe_w1, moe_w3, moe_w2, final_g):
    x = (x_prompt.reshape(T_PROMPT, D), x_sample.reshape(T_SAMPLE, D))
    c_all = jnp.concatenate([c_prompt, c_sample], axis=0)
    rope = _rope_table()

    mod_all = _ada_call(c_all, ada_w, ada_b).reshape(DEPTH, N_SEQ, 6, D)

    def layer_params(i):
        mod = (mod_all[i, :BATCH], mod_all[i, BATCH:])
        w_r = jnp.pad(jnp.concatenate([moe_w_re[i], moe_w_rg[i]], axis=1),
                      ((0, 0), (0, ROUTER_LANES - MOE_GROUPS - MOE_EXPERTS)))
        w_r_hi = w_r.astype(BF16)
        w_r_lo = (w_r - w_r_hi.astype(F32)).astype(BF16)
        w_r = jnp.concatenate([w_r_hi, w_r_lo], axis=1)
        b_r = jnp.pad(jnp.concatenate([moe_b_re[i].reshape(-1), moe_b_rg[i]]),
                      (0, ROUTER_LANES - MOE_GROUPS - MOE_EXPERTS)).reshape(1, ROUTER_LANES)
        return mod, w_r, b_r

    ret_prompt = ret_sample = gm_sample = None
    prev = None
    for i in range(DEPTH):
        j = i // 2
        mod, w_r, b_r = layer_params(i)
        g1 = norm1_g[i].reshape(1, D)
        g2 = norm2_g[i].reshape(1, D)
        if i % 2 == 0:
            x, p = _ret_proj_call(x, prev, mod, g1, ret_w_in, j, rope)
            y_p, ret_prompt = _ret_core_call(p, None, ret_prompt, j, BATCH, SEQ, RET_CHUNK_PROMPT, 0)
            y_s, ret_sample = _ret_core_call(p, state_ret, ret_sample, j, DEC_BATCH, DEC_SEQ,
                                             RET_CHUNK_SAMPLE, T_PROMPT)
            x, h2, logits = _ret_out_call(y_p, y_s, ret_w_out, j, x, mod, g2, w_r, b_r)
        else:
            x, uv, gm_sample = _gm_proj_call(x, prev, mod, g1, gm_w_in, j, gm_b_in[j], gm_ln_g[j],
                                             gm_ln_b[j], gm_sample)
            ws_bd, bs_bd = _gm_block_diag(gm_w_s[j], gm_b_s[j])
            x, h2, logits = _gm_out_call(uv, ws_bd, bs_bd, gm_w_out, j,
                                         gm_b_out[j].reshape(1, D), x, mod, g2, w_r, b_r)
        dest, gates, meta = _route_call(logits)
        yg0, yg1 = _moe_rows(h2, dest, meta, moe_w1, moe_w3, moe_w2, i)
        prev = (yg0, yg1, gates, *mod)

    fg = final_g.reshape(1, D)
    y_prompt = _final_call(x, prev, fg, 0, T_PROMPT).reshape(BATCH, SEQ, D)
    y_sample = _final_call(x, prev, fg, T_PROMPT, T_SAMPLE).reshape(DEC_BATCH, DEC_SEQ, D)
    return (y_prompt, y_sample, ret_prompt, ret_sample,
            gm_sample.reshape(N_GM, DEC_BATCH, DEC_SEQ, GM_HALF))
```

```python
import functools

import numpy as np
import jax
import jax.numpy as jnp
from jax import lax
from jax.experimental import pallas as pl
from jax.experimental.pallas import tpu as pltpu
from jax.experimental.pallas import tpu_sc as plsc

F32 = jnp.float32
BF16 = jnp.bfloat16
U32 = jnp.uint32

D = 1024
BATCH, SEQ = 4, 4096
DEC_BATCH, DEC_SEQ = 16, 64
PAST_LEN = 4096
DEPTH = 4
N_RET = (DEPTH + 1) // 2
N_GM = DEPTH // 2
N_SEQ = BATCH + DEC_BATCH

RET_HEADS, RET_DK, RET_DV = 4, 256, 512
RET_QK = RET_HEADS * RET_DK
RET_V = RET_HEADS * RET_DV
RET_IN = 2 * RET_QK + 2 * RET_V
ROPE_BASE = 10000.0
ROPE_HALF = RET_DK // 2
ROPE_TABLE_LANES = 3 * ROPE_HALF

GM_FFN = 6 * D
GM_HALF = GM_FFN // 2
GM_GROUPS = 4
GM_GDIM = GM_HALF // GM_GROUPS
GM_CHUNK = 128

MOE_GROUPS, MOE_PER_GROUP = 4, 8
MOE_EXPERTS = MOE_GROUPS * MOE_PER_GROUP
MOE_TOPK = 2
MOE_HIDDEN = 512
EPS = 1e-6

GROUP = DEC_SEQ
T_PROMPT = BATCH * SEQ
T_SAMPLE = DEC_BATCH * DEC_SEQ
T_ALL = T_PROMPT + T_SAMPLE
N_GROUPS = T_ALL // GROUP
ROWS_WIDE = 512
ROWS_GM_PROJ = 256
ROWS_FINAL = 1024
ROWS_RET_OUT = 1024

RET_CHUNK_PROMPT = 256
RET_CHUNK_SAMPLE = DEC_SEQ

GM_MIX = 256

EXP_BLOCK = 1024
EXP_ROW_STEP = 128
N_ASSIGN = T_ALL * MOE_TOPK
N_EXP_BLOCKS = -(-(N_ASSIGN + MOE_EXPERTS * (EXP_BLOCK - 1)) // EXP_BLOCK)
P_ROWS = N_EXP_BLOCKS * EXP_BLOCK
ROUTER_LANES = 128
ROUTE_TM = 1024
N_ROUTE_TILES = T_ALL // ROUTE_TM
META_LANES = 256
assert META_LANES >= N_EXP_BLOCKS

DP = D // 2
SC_CORES, SC_SUBCORES = 2, 16
SC_WORKERS = SC_CORES * SC_SUBCORES
SC_ROWS_PER_WORKER = T_ALL // SC_WORKERS
SC_CHUNK = 32
assert SC_ROWS_PER_WORKER % SC_CHUNK == 0 and SC_CHUNK % 8 == 0

V7X_VMEM_LIMIT_BYTES = 56 * 1024 * 1024


def _params(*sem):
    return pltpu.CompilerParams(dimension_semantics=sem, vmem_limit_bytes=V7X_VMEM_LIMIT_BYTES)


def _resident(shape):
    nd = len(shape)
    return pl.BlockSpec(shape, lambda *_: (0,) * nd, pipeline_mode=pl.Buffered(1))


WEIGHT_STAGE_BYTES = 3 * 1024 * 1024


def _weight_scratch(k, n):
    chunk = k
    while chunk * n * 4 > WEIGHT_STAGE_BYTES:
        assert chunk % 16 == 0
        chunk //= 2
    return [pltpu.VMEM((k, n), BF16), pltpu.VMEM((2, chunk, n), F32), pltpu.SemaphoreType.DMA((2,))]


def _load_weight_bf16(w_hbm, layer, w_s, stage, sems):
    k = w_s.shape[0]
    chunk = stage.shape[1]

    def copy(c):
        return pltpu.make_async_copy(w_hbm.at[layer, pl.ds(c * chunk, chunk)], stage.at[c % 2], sems.at[c % 2])

    @pl.when(pl.program_id(0) == 0)
    def _():
        copy(0).start()
        for c in range(k // chunk):
            if c + 1 < k // chunk:
                copy(c + 1).start()
            copy(c).wait()
            w_s[c * chunk:(c + 1) * chunk, :] = stage[c % 2].astype(BF16)


def _rms(x):
    return x * lax.rsqrt(jnp.mean(x * x, axis=-1, keepdims=True) + EPS)


def _silu(x):
    return x * jax.nn.sigmoid(x)


def _per_group(x2d, fn):
    rows = x2d.shape[0]
    return fn(x2d.reshape(rows // GROUP, GROUP, D)).reshape(rows, D)


def _tile_mod(mod_refs, is_prompt):
    modp_ref, mods_ref = mod_refs
    return jnp.where(is_prompt, jnp.broadcast_to(modp_ref[...], mods_ref.shape), mods_ref[...])


def _norm_mod(x, gain_ref, mod, shift_idx):
    y = _rms(x) * gain_ref[...]
    scale = mod[:, shift_idx + 1:shift_idx + 2, :]
    shift = mod[:, shift_idx:shift_idx + 1, :]
    return _per_group(y, lambda y3: y3 * (1.0 + scale) + shift)


def _pack_bf16_pairs(x):
    lo = lax.bitcast_convert_type(x[:, :DP].astype(BF16).astype(F32), U32)
    hi = lax.bitcast_convert_type(x[:, DP:].astype(BF16).astype(F32), U32)
    return (lo >> 16) | (hi & U32(0xFFFF0000))


def _unpack_bf16_pairs(w):
    lo = lax.bitcast_convert_type(w << 16, F32)
    hi = lax.bitcast_convert_type(w & U32(0xFFFF0000), F32)
    return jnp.concatenate([lo, hi], axis=1)


def _add_moe(x, yg0_ref, yg1_ref, gates_ref, mod_prev):
    g = gates_ref[...]
    y = g[:, 0:1] * _unpack_bf16_pairs(yg0_ref[...]) + g[:, 1:2] * _unpack_bf16_pairs(yg1_ref[...])
    gate2 = mod_prev[:, 5:6, :]
    return x + _per_group(y, lambda y3: y3 * gate2)


ADA_TN = 3072


def _ada_kernel(c_ref, w_ref, b_ref, o_ref):
    c = c_ref[...]
    s = _silu(c).astype(BF16)
    o_ref[0] = jnp.dot(s, w_ref[0].astype(BF16), preferred_element_type=F32) + b_ref[0]


def _ada_call(c_all, ada_w, ada_b):
    return pl.pallas_call(
        _ada_kernel,
        grid=(DEPTH, 6 * D // ADA_TN),
        in_specs=[
            pl.BlockSpec((N_SEQ, D), lambda i, j: (0, 0)),
            pl.BlockSpec((1, D, ADA_TN), lambda i, j: (i, 0, j)),
            pl.BlockSpec((1, 1, ADA_TN), lambda i, j: (i, 0, j)),
        ],
        out_specs=pl.BlockSpec((1, N_SEQ, ADA_TN), lambda i, j: (i, 0, j)),
        out_shape=jax.ShapeDtypeStruct((DEPTH, N_SEQ, 6 * D), F32),
        compiler_params=_params("parallel", "parallel"),
        name="ada_modulation",
    )(c_all, ada_w, ada_b.reshape(DEPTH, 1, 6 * D))


def _n_tiles(rows):
    return T_ALL // rows


def _n_prompt_tiles(rows):
    return T_PROMPT // rows


def _row_spec(rows, width):
    return pl.BlockSpec((rows, width), lambda i: (i, 0))


def _mod_specs(rows):
    npt = _n_prompt_tiles(rows)
    return [pl.BlockSpec((1, 6, D), lambda i: (jnp.minimum(i * rows // SEQ, BATCH - 1), 0, 0)),
            pl.BlockSpec((rows // DEC_SEQ, 6, D), lambda i: (jnp.maximum(i - npt, 0), 0, 0))]


def _prev_specs(rows):
    return [_row_spec(rows, DP), _row_spec(rows, DP), _row_spec(rows, ROUTER_LANES)] + _mod_specs(rows)


def _stage_a_spec(spec, n_tiles):
    return pl.BlockSpec(spec.block_shape, lambda i, m=spec.index_map: m(jnp.minimum(i, n_tiles - 1)))


def _stage_b_spec(spec):
    return pl.BlockSpec(spec.block_shape, lambda i, m=spec.index_map: m(jnp.maximum(i - 1, 0)))


def _prompt_rows_spec(rows, width):
    last = _n_prompt_tiles(rows) - 1
    return pl.BlockSpec((rows, width), lambda i: (jnp.minimum(i, last), 0))


def _sample_rows_spec(rows, width):
    npt = _n_prompt_tiles(rows)
    return pl.BlockSpec((rows, width), lambda i: (jnp.maximum(i - npt, 0), 0))


def _ret_proj_kernel(has_prev, n_tiles, n_prompt_tiles, layer, *refs):
    i = pl.program_id(0)
    is_prompt = jnp.minimum(i, n_tiles - 1) < n_prompt_tiles
    head = refs[:6] if has_prev else refs[:2]
    refs = refs[len(head):]
    mod_refs = refs[:2]
    gain_ref, w_hbm, rope_ref, xo_ref, p_ref, hb_cur, hb_prev, w_s, w_stage, w_sems = refs[2:]
    _load_weight_bf16(w_hbm, layer, w_s, w_stage, w_sems)

    @pl.when(i == 0)
    def _():
        hb_prev[...] = jnp.zeros_like(hb_prev)

    def qk_head(j):
        lo = j * RET_DK
        acc = jnp.dot(hb_prev[...], w_s[:, lo:lo + RET_DK], preferred_element_type=F32)
        x1 = acc[:, :ROPE_HALF]
        x2 = acc[:, ROPE_HALF:]
        cos = rope_ref[:, 0:ROPE_HALF]
        sin = rope_ref[:, ROPE_HALF:2 * ROPE_HALF]
        scale = rope_ref[:, 2 * ROPE_HALF + j:2 * ROPE_HALF + j + 1]
        p_ref[:, lo:lo + ROPE_HALF] = ((x1 * cos - x2 * sin) * scale).astype(BF16)
        p_ref[:, lo + ROPE_HALF:lo + RET_DK] = ((x1 * sin + x2 * cos) * scale).astype(BF16)

    def vg_head(j):
        lo = 2 * RET_QK + j * RET_DV
        acc = jnp.dot(hb_prev[...], w_s[:, lo:lo + RET_DV], preferred_element_type=F32)
        if j >= RET_HEADS:
            acc = _silu(acc)
        p_ref[:, lo:lo + RET_DV] = acc.astype(BF16)

    vg_head(0)
    vg_head(1)
    if has_prev:
        x_ref, yg0_ref, yg1_ref, gates_ref = head[:4]
        x = _add_moe(x_ref[...], yg0_ref, yg1_ref, gates_ref, _tile_mod(head[4:6], is_prompt))
    else:
        x = jnp.where(is_prompt, head[0][...], head[1][...])
    xo_ref[...] = x
    hb_cur[...] = _norm_mod(x, gain_ref, _tile_mod(mod_refs, is_prompt), 0).astype(BF16)
    for j in range(2, 2 * RET_HEADS):
        vg_head(j)
    for j in range(2 * RET_HEADS):
        qk_head(j)
    hb_prev[...] = hb_cur[...]


def _ret_proj_call(x, prev, mod, gain, w_in, layer, rope):
    rows = ROWS_WIDE
    n = _n_tiles(rows)
    has_prev = prev is not None
    if has_prev:
        in_specs = [_row_spec(rows, D)] + _prev_specs(rows)
        args = [x] + list(prev)
    else:
        in_specs = [_prompt_rows_spec(rows, D), _sample_rows_spec(rows, D)]
        args = list(x)
    in_specs = [_stage_a_spec(s, n) for s in in_specs + _mod_specs(rows)] + [
        _resident((1, D)), pl.BlockSpec(memory_space=pl.ANY),
        _stage_b_spec(_row_spec(rows, ROPE_TABLE_LANES))]
    args += [*mod, gain, w_in, rope]
    return pl.pallas_call(
        functools.partial(_ret_proj_kernel, has_prev, n, _n_prompt_tiles(rows), layer),
        grid=(n + 1,),
        in_specs=in_specs,
        out_specs=[_stage_a_spec(_row_spec(rows, D), n), _stage_b_spec(_row_spec(rows, RET_IN))],
        out_shape=[jax.ShapeDtypeStruct((T_ALL, D), F32), jax.ShapeDtypeStruct((T_ALL, RET_IN), BF16)],
        scratch_shapes=[pltpu.VMEM((rows, D), BF16), pltpu.VMEM((rows, D), BF16),
                        *_weight_scratch(D, RET_IN)],
        compiler_params=_params("arbitrary"),
        name="ret_proj",
    )(*args)


RET_SEQS_PER_STEP = 2


def _ret_core_kernel(has_s0, n_chunks, layer, *refs):
    refs = list(refs)
    p_refs = [refs.pop(0) for _ in range(RET_SEQS_PER_STEP)]
    s0_ref = refs.pop(0) if has_s0 else None
    causal_ref, cd_ref = refs[:2]
    y_ref, so_ref, s_ref = refs[-3:]
    c = pl.program_id(1)

    @pl.when(c == 0)
    def _():
        if has_s0:
            s_ref[...] = s0_ref[0]
        else:
            s_ref[...] = jnp.zeros_like(s_ref)

    for h in range(RET_HEADS):
        for q, p_ref in enumerate(p_refs):
            qb = p_ref[:, h * RET_DK:(h + 1) * RET_DK]
            kb = p_ref[:, RET_QK + h * RET_DK:RET_QK + (h + 1) * RET_DK]
            vb = p_ref[:, 2 * RET_QK + h * RET_DV:2 * RET_QK + (h + 1) * RET_DV]
            gb = p_ref[:, 2 * RET_QK + RET_V + h * RET_DV:2 * RET_QK + RET_V + (h + 1) * RET_DV]
            scores = lax.dot_general(qb, kb, (((1,), (1,)), ((), ())), preferred_element_type=F32)
            scores = scores * causal_ref[...]
            s_old = s_ref[q, h]
            o = (jnp.dot(scores.astype(BF16), vb, preferred_element_type=F32)
                 + jnp.dot(qb, s_old.astype(BF16), preferred_element_type=F32))
            s_ref[q, h] = cd_ref[h][:, 0:1] * (s_old + lax.dot_general(
                kb, vb, (((0,), (0,)), ((), ())), preferred_element_type=F32))
            y_ref[q, :, h * RET_DV:(h + 1) * RET_DV] = (gb.astype(F32) * _rms(o)).astype(BF16)

    @pl.when(c == n_chunks - 1)
    def _():
        so_ref[0] = s_ref[...]
        if layer == 0:
            for later in range(1, N_RET):
                so_ref[later] = jnp.zeros_like(s_ref)


def _ret_log_gamma():
    return np.log1p(-np.exp2(-5.0 - np.arange(RET_HEADS, dtype=np.float64)))


def _ret_chunk_tables(cl):
    idx = np.arange(cl)
    causal = (idx[:, None] >= idx[None, :]).astype(np.float32)
    cd = np.broadcast_to(np.exp(_ret_log_gamma() * cl)[:, None, None], (RET_HEADS, 1, 128))
    return jnp.asarray(causal, F32), jnp.asarray(cd, F32)


def _ret_row_scales():
    c = np.concatenate([np.arange(T_PROMPT) % RET_CHUNK_PROMPT,
                        np.arange(T_SAMPLE) % RET_CHUNK_SAMPLE]).astype(np.float64)
    e = (c[:, None] + 1.0) * _ret_log_gamma()[None, :]
    return np.concatenate([np.exp(e), np.exp(-e) * RET_DK ** -0.5], axis=1)


def _ret_core_call(p, s0, states, layer, n_seq, seq_len, cl, row0):
    has_s0 = s0 is not None
    n_chunks = seq_len // cl
    rb0 = row0 // cl
    per = RET_SEQS_PER_STEP
    assert n_seq % per == 0
    states_blk = (per, RET_HEADS, RET_DK, RET_DV)
    in_specs = [pl.BlockSpec((cl, RET_IN), lambda b, c, q=q: (rb0 + (per * b + q) * n_chunks + c, 0))
                for q in range(per)]
    args = [p] * per
    if has_s0:
        in_specs.append(pl.BlockSpec((1,) + states_blk, lambda b, c: (layer, b, 0, 0, 0)))
        args.append(s0)
    in_specs += [_resident((cl, cl)), _resident((RET_HEADS, 1, 128))]
    args += list(_ret_chunk_tables(cl))
    if layer == 0:
        assert states is None
        state_spec = pl.BlockSpec((N_RET,) + states_blk, lambda b, c: (0, b, 0, 0, 0))
        aliases = {}
    else:
        in_specs.append(pl.BlockSpec(memory_space=pl.ANY))
        args.append(states)
        state_spec = pl.BlockSpec((1,) + states_blk, lambda b, c: (layer, b, 0, 0, 0))
        aliases = {len(args) - 1: 1}
    y, states = pl.pallas_call(
        functools.partial(_ret_core_kernel, has_s0, n_chunks, layer),
        grid=(n_seq // per, n_chunks),
        in_specs=in_specs,
        out_specs=[pl.BlockSpec((per, cl, RET_V), lambda b, c: (b, c, 0)), state_spec],
        out_shape=[jax.ShapeDtypeStruct((n_seq, seq_len, RET_V), BF16),
                   jax.ShapeDtypeStruct((N_RET, n_seq) + states_blk[1:], F32)],
        scratch_shapes=[pltpu.VMEM(states_blk, F32)],
        input_output_aliases=aliases,
        compiler_params=_params("parallel", "arbitrary"),
        name="ret_core",
    )(*args)
    return y.reshape(n_seq * seq_len, RET_V), states


def _residual_router(acc, x_ref, mod, gain_ref, wr_ref, br_ref, xo_ref, h2_ref, lg_ref):
    gate1 = mod[:, 2:3, :]
    xn = x_ref[...] + _per_group(acc, lambda a3: a3 * gate1)
    xo_ref[...] = xn
    h2 = _norm_mod(xn, gain_ref, mod, 3)
    h2_ref[...] = _pack_bf16_pairs(h2)
    hh = jnp.dot(h2.astype(BF16), wr_ref[...], preferred_element_type=F32)
    lg_ref[...] = hh[:, :ROUTER_LANES] + hh[:, ROUTER_LANES:] + br_ref[...]


def _mix_out_specs(rows):
    return [_row_spec(rows, D), _row_spec(rows, DP), _row_spec(rows, ROUTER_LANES)]


_MIX_OUT_SHAPE = [
    jax.ShapeDtypeStruct((T_ALL, D), F32),
    jax.ShapeDtypeStruct((T_ALL, DP), U32),
    jax.ShapeDtypeStruct((T_ALL, ROUTER_LANES), F32),
]


def _router_specs():
    return [_resident((D, 2 * ROUTER_LANES)), _resident((1, ROUTER_LANES))]


def _ret_out_kernel(n_prompt_tiles, layer, yp_ref, ys_ref, w_hbm, x_ref, modp_ref, mods_ref, gain_ref,
                    wr_ref, br_ref, xo_ref, h2_ref, lg_ref, w_s, w_stage, w_sems):
    _load_weight_bf16(w_hbm, layer, w_s, w_stage, w_sems)
    is_prompt = pl.program_id(0) < n_prompt_tiles
    yin = jnp.where(is_prompt, yp_ref[...], ys_ref[...])
    acc = jnp.dot(yin, w_s[...], preferred_element_type=F32)
    mod = _tile_mod((modp_ref, mods_ref), is_prompt)
    _residual_router(acc, x_ref, mod, gain_ref, wr_ref, br_ref, xo_ref, h2_ref, lg_ref)


def _ret_out_call(y_prompt, y_sample, w_out, layer, x, mod, gain, w_r, b_r):
    rows = ROWS_RET_OUT
    return pl.pallas_call(
        functools.partial(_ret_out_kernel, _n_prompt_tiles(rows), layer),
        grid=(_n_tiles(rows),),
        in_specs=[_prompt_rows_spec(rows, RET_V), _sample_rows_spec(rows, RET_V),
                  pl.BlockSpec(memory_space=pl.ANY), _row_spec(rows, D), *_mod_specs(rows),
                  _resident((1, D))] + _router_specs(),
        out_specs=_mix_out_specs(rows),
        out_shape=_MIX_OUT_SHAPE,
        scratch_shapes=_weight_scratch(RET_V, D),
        compiler_params=_params("arbitrary"),
        name="ret_out",
    )(y_prompt, y_sample, w_out, x, *mod, gain, w_r, b_r)


GM_TN = 1024


_GELU_C = float(np.sqrt(2.0 / np.pi))


def _gelu_tanh(x):
    hx = 0.5 * x
    return hx * jnp.tanh(x * (_GELU_C + (_GELU_C * 0.044715) * (x * x))) + hx


def _gm_proj_kernel(n_tiles, n_prompt_tiles, layer, *refs):
    x_ref, yg0_ref, yg1_ref, gates_ref = refs[:4]
    gain_ref, w_hbm, b_ref, lg_ref, lb_ref = refs[8:13]
    (xo_ref, uv_ref, vs_ref, hb_cur, hb_prev, vraw_cur, vraw_prev, stat_prev, sum_s,
     w_s, w_stage, w_sems) = refs[-12:]
    _load_weight_bf16(w_hbm, layer, w_s, w_stage, w_sems)
    i = pl.program_id(0)
    rows = x_ref.shape[0]

    @pl.when(i == 0)
    def _():
        hb_prev[...] = jnp.zeros_like(hb_prev)
        vraw_prev[...] = jnp.zeros_like(vraw_prev)
        stat_prev[...] = jnp.zeros_like(stat_prev)

    def add_row(x, row_ref, lo, width):
        x3 = x.reshape(rows // 8, 8, width) + row_ref[:, lo:lo + width]
        return x3.reshape(rows, width)

    def mul_row(x, row_ref, lo, width):
        x3 = x.reshape(rows // 8, 8, width) * row_ref[:, lo:lo + width]
        return x3.reshape(rows, width)

    def proj_chunk(hb_ref, lo):
        z = jnp.dot(hb_ref[...], w_s[:, lo:lo + GM_TN], preferred_element_type=F32)
        return add_row(z, b_ref, lo, GM_TN).astype(BF16)

    def stage_b_chunk(lo):
        uv_ref[:, lo:lo + GM_TN] = _gelu_tanh(proj_chunk(hb_prev, lo))
        for k in range(lo, lo + GM_TN, 128):
            vk = vraw_prev[:, k:k + 128].astype(F32) * stat_prev[:, 0:128] + stat_prev[:, 128:256]
            vn = add_row(mul_row(vk, lg_ref, k, 128), lb_ref, k, 128)
            uv_ref[:, GM_HALF + k:GM_HALF + k + 128] = vn.astype(BF16)
            vs_ref[0, :, k:k + 128] = vn

    def stage_a_chunk(n, lo):
        gz = _gelu_tanh(proj_chunk(hb_cur, GM_HALF + lo))
        vraw_cur[:, lo:lo + GM_TN] = gz
        gf = gz.astype(F32)
        pieces = [gf[:, k:k + 128] for k in range(0, GM_TN, 128)]
        t1 = functools.reduce(lambda p, q: p + q, pieces)
        t2 = functools.reduce(lambda p, q: p + q, [p * p for p in pieces])
        if n == 0:
            sum_s[:, 0:128] = t1
            sum_s[:, 128:256] = t2
        else:
            sum_s[:, 0:128] += t1
            sum_s[:, 128:256] += t2

    chunks = list(range(0, GM_HALF, GM_TN))
    stage_b_chunk(chunks[0])
    a_is_prompt = jnp.minimum(i, n_tiles - 1) < n_prompt_tiles
    x = _add_moe(x_ref[...], yg0_ref, yg1_ref, gates_ref, _tile_mod(refs[4:6], a_is_prompt))
    xo_ref[...] = x
    hb_cur[...] = _norm_mod(x, gain_ref, _tile_mod(refs[6:8], a_is_prompt), 0).astype(BF16)
    for n, lo in enumerate(chunks):
        stage_a_chunk(n, lo)
        if 0 < n < len(chunks) - 1:
            stage_b_chunk(lo)
    mu = jnp.sum(sum_s[:, 0:128], axis=-1, keepdims=True) * (1.0 / GM_HALF)
    var = jnp.sum(sum_s[:, 128:256], axis=-1, keepdims=True) * (1.0 / GM_HALF) - mu * mu
    rstd = lax.rsqrt(var + EPS)
    stage_b_chunk(chunks[-1])
    stat_prev[:, 0:128] = jnp.broadcast_to(rstd, (rows, 128))
    stat_prev[:, 128:256] = jnp.broadcast_to(-mu * rstd, (rows, 128))
    hb_prev[...] = hb_cur[...]
    vraw_prev[...] = vraw_cur[...]

    if layer == 0:
        @pl.when(i - 1 >= n_prompt_tiles)
        def _():
            for later in range(1, N_GM):
                vs_ref[later] = jnp.zeros((rows, GM_HALF), F32)


def _gm_proj_call(x, prev, mod, gain, w_in, layer, b_in, ln_g, ln_b, vs_all):
    rows = ROWS_GM_PROJ
    n, npt = _n_tiles(rows), _n_prompt_tiles(rows)

    def stage_a(spec):
        return _stage_a_spec(spec, n)

    stage_b = _stage_b_spec
    in_specs = [stage_a(s) for s in [_row_spec(rows, D)] + _prev_specs(rows) + _mod_specs(rows)] + [
        _resident((1, D)), pl.BlockSpec(memory_space=pl.ANY), _resident((8, GM_FFN)),
        _resident((8, GM_HALF)), _resident((8, GM_HALF))]
    rows8 = [jnp.broadcast_to(r.reshape(1, -1), (8, r.size)) for r in (b_in, ln_g, ln_b)]
    args = [x, *prev, *mod, gain, w_in, *rows8]
    if layer == 0:
        assert vs_all is None
        vs_spec = pl.BlockSpec((N_GM, rows, GM_HALF), lambda i: (0, jnp.maximum(i - 1 - npt, 0), 0))
        aliases = {}
    else:
        in_specs.append(pl.BlockSpec(memory_space=pl.ANY))
        args.append(vs_all)
        vs_spec = pl.BlockSpec((1, rows, GM_HALF), lambda i: (layer, jnp.maximum(i - 1 - npt, 0), 0))
        aliases = {len(args) - 1: 2}
    return pl.pallas_call(
        functools.partial(_gm_proj_kernel, n, npt, layer),
        grid=(n + 1,),
        in_specs=in_specs,
        out_specs=[stage_a(_row_spec(rows, D)), stage_b(_row_spec(rows, GM_FFN)), vs_spec],
        out_shape=[jax.ShapeDtypeStruct((T_ALL, D), F32),
                   jax.ShapeDtypeStruct((T_ALL, GM_FFN), BF16),
                   jax.ShapeDtypeStruct((N_GM, T_SAMPLE, GM_HALF), F32)],
        scratch_shapes=[pltpu.VMEM((rows, D), BF16), pltpu.VMEM((rows, D), BF16),
                        pltpu.VMEM((rows, GM_HALF), BF16), pltpu.VMEM((rows, GM_HALF), BF16),
                        pltpu.VMEM((rows, 256), F32), pltpu.VMEM((rows, 256), F32),
                        *_weight_scratch(D, GM_FFN)],
        input_output_aliases=aliases,
        compiler_params=_params("arbitrary"),
        name="gm_proj",
    )(*args)


def _gm_out_kernel(n_prompt_tiles, layer, uv_ref, ws_ref, bs_ref, w_hbm, bo_ref, x_ref, modp_ref,
                   mods_ref, gain_ref, wr_ref, br_ref, xo_ref, h2_ref, lg_ref, w_s, w_stage, w_sems):
    _load_weight_bf16(w_hbm, layer, w_s, w_stage, w_sems)
    rows = uv_ref.shape[0]
    mod = _tile_mod((modp_ref, mods_ref), pl.program_id(0) < n_prompt_tiles)
    pieces = []
    for r0 in range(0, rows, GM_MIX):
        acc = jnp.zeros((GM_MIX, D), F32)
        for g in range(GM_GROUPS):
            lo = g * GM_GDIM
            sp = jnp.dot(ws_ref[0, g], uv_ref[r0:r0 + GM_MIX, GM_HALF + lo:GM_HALF + lo + GM_GDIM],
                         preferred_element_type=F32) + bs_ref[0, g]
            gated = (uv_ref[r0:r0 + GM_MIX, lo:lo + GM_GDIM].astype(F32) * sp).astype(BF16)
            acc = acc + jnp.dot(gated, w_s[lo:lo + GM_GDIM, :], preferred_element_type=F32)
        pieces.append(acc)
    acc = jnp.concatenate(pieces, axis=0) + bo_ref[...]
    _residual_router(acc, x_ref, mod, gain_ref, wr_ref, br_ref, xo_ref, h2_ref, lg_ref)


def _gm_block_diag(w_s, b_s):
    mats, biases = [], []
    for cl in (GM_CHUNK, DEC_SEQ):
        tri = jnp.tril(jnp.ones((cl, cl), bool))
        blk = jnp.where(tri[None], w_s[:, :cl, :cl], 0.0)
        reps = GM_MIX // cl
        eye = jnp.eye(reps, dtype=w_s.dtype)
        bd = jnp.einsum("ab,gts->gatbs", eye, blk).reshape(GM_GROUPS, GM_MIX, GM_MIX)
        mats.append(bd)
        biases.append(jnp.tile(b_s[:, :cl], (1, reps))[:, :, None])
    return jnp.stack(mats).astype(BF16), jnp.stack(biases).astype(F32)


def _gm_out_call(uv, ws_bd, bs_bd, w_out, layer, b_out, x, mod, gain, w_r, b_r):
    rows = ROWS_WIDE
    npt = _n_prompt_tiles(rows)

    def variant(i):
        return jnp.where(i >= npt, 1, 0)

    return pl.pallas_call(
        functools.partial(_gm_out_kernel, npt, layer),
        grid=(_n_tiles(rows),),
        in_specs=[_row_spec(rows, GM_FFN),
                  pl.BlockSpec((1, GM_GROUPS, GM_MIX, GM_MIX), lambda i: (variant(i), 0, 0, 0)),
                  pl.BlockSpec((1, GM_GROUPS, GM_MIX, 1), lambda i: (variant(i), 0, 0, 0)),
                  pl.BlockSpec(memory_space=pl.ANY), _resident((1, D)), _row_spec(rows, D),
                  *_mod_specs(rows), _resident((1, D))] + _router_specs(),
        out_specs=_mix_out_specs(rows),
        out_shape=_MIX_OUT_SHAPE,
        scratch_shapes=_weight_scratch(GM_HALF, D),
        compiler_params=_params("arbitrary"),
        name="gm_out",
    )(uv, ws_bd, bs_bd, w_out, b_out, x, *mod, gain, w_r, b_r)


def _expert_kernel(layer, be_ref, bv_ref, nb_ref, nx_ref, xb_ref, w1_hbm, w3_hbm, w2_hbm, yb_ref,
                   st1, st3, st2, w1s, w3s, w2s, sems, slot_ref):
    b = pl.program_id(0)

    def weight_copies(e, slot):
        return (pltpu.make_async_copy(w1_hbm.at[layer, e], st1.at[slot], sems.at[slot, 0]),
                pltpu.make_async_copy(w3_hbm.at[layer, e], st3.at[slot], sems.at[slot, 1]),
                pltpu.make_async_copy(w2_hbm.at[layer, e], st2.at[slot], sems.at[slot, 2]))

    @pl.when(b == 0)
    def _():
        slot_ref[0] = 0
        for cp in weight_copies(be_ref[0], 0):
            cp.start()

    @pl.when(b < nb_ref[0])
    def _():
        prev_e = be_ref[jnp.maximum(b - 1, 0)]

        @pl.when((b == 0) | (be_ref[b] != prev_e))
        def _():
            slot = slot_ref[0]
            for cp in weight_copies(be_ref[b], slot):
                cp.wait()
            w1s[...] = st1[slot].astype(BF16)
            w3s[...] = st3[slot].astype(BF16)
            w2s[...] = st2[slot].astype(BF16)

            @pl.when(nx_ref[b] >= 0)
            def _():
                for cp in weight_copies(nx_ref[b], 1 - slot):
                    cp.start()

            slot_ref[0] = 1 - slot

        valid = bv_ref[b]

        def run_rows(n):
            row = lax.broadcasted_iota(jnp.int32, (n, 1), 0)
            xw = jnp.where(row < valid, xb_ref[0:n], U32(0))
            x = _unpack_bf16_pairs(xw).astype(BF16)
            a = jnp.dot(x, w1s[...], preferred_element_type=F32)
            c = jnp.dot(x, w3s[...], preferred_element_type=F32)
            h = (_silu(a) * c).astype(BF16)
            yb_ref[0:n] = _pack_bf16_pairs(jnp.dot(h, w2s[...], preferred_element_type=F32))
            if n < EXP_BLOCK:
                yb_ref[n:EXP_BLOCK] = jnp.zeros((EXP_BLOCK - n, DP), U32)

        for n in range(EXP_ROW_STEP, EXP_BLOCK + 1, EXP_ROW_STEP):
            pl.when((valid > n - EXP_ROW_STEP) & (valid <= n))(functools.partial(run_rows, n))


def _expert_call(blk_e, blk_valid, n_blk, blk_next, xb, w1, w3, w2, layer):
    def blk(b, be, bv, nb, nx):
        return (jnp.minimum(b, nb[0] - 1), 0)

    up, down = (D, MOE_HIDDEN), (MOE_HIDDEN, D)
    grid_spec = pltpu.PrefetchScalarGridSpec(
        num_scalar_prefetch=4,
        grid=(N_EXP_BLOCKS,),
        in_specs=[pl.BlockSpec((EXP_BLOCK, DP), blk)] + [pl.BlockSpec(memory_space=pl.ANY)] * 3,
        out_specs=pl.BlockSpec((EXP_BLOCK, DP), blk),
        scratch_shapes=[pltpu.VMEM((2,) + up, F32), pltpu.VMEM((2,) + up, F32), pltpu.VMEM((2,) + down, F32),
                        pltpu.VMEM(up, BF16), pltpu.VMEM(up, BF16), pltpu.VMEM(down, BF16),
                        pltpu.SemaphoreType.DMA((2, 3)), pltpu.SMEM((1,), jnp.int32)],
    )
    return pl.pallas_call(
        functools.partial(_expert_kernel, layer),
        grid_spec=grid_spec,
        out_shape=jax.ShapeDtypeStruct((P_ROWS, DP), U32),
        compiler_params=_params("arbitrary"),
        name="experts",
    )(blk_e, blk_valid, n_blk, blk_next, xb, w1, w3, w2)


def _route_kernel(lg_ref, dest_ref, gates_ref, meta_ref, cnt_ref):
    t = pl.program_id(0)

    @pl.when(t == 0)
    def _():
        cnt_ref[...] = jnp.zeros_like(cnt_ref)

    pl.when(t < N_ROUTE_TILES)(functools.partial(_route_tile, t, lg_ref, dest_ref, gates_ref, cnt_ref))
    pl.when(t == N_ROUTE_TILES)(functools.partial(_route_finish, dest_ref, meta_ref, cnt_ref))


ROUTE_RANK_BITS = 16
assert N_ASSIGN < 2 ** ROUTE_RANK_BITS


def _route_tile(t, lg_ref, dest_ref, gates_ref, cnt_ref):
    tm = ROUTE_TM
    lt = lg_ref[...].T
    el = lt[0:MOE_EXPERTS]
    gl = lt[MOE_EXPERTS:MOE_EXPERTS + 8]
    gidx = lax.broadcasted_iota(jnp.int32, (8, tm), 0)
    neg = jnp.float32(-jnp.inf)
    gl = jnp.where(gidx < MOE_GROUPS, gl, neg)
    gmax = jnp.max(gl, axis=0, keepdims=True)
    grp = jnp.min(jnp.where(gl == gmax, gidx, MOE_GROUPS), axis=0, keepdims=True)
    eidx = lax.broadcasted_iota(jnp.int32, (MOE_EXPERTS, tm), 0)
    els = jnp.where((eidx >> 3) == grp, el, neg)
    m1 = jnp.max(els, axis=0, keepdims=True)
    i1 = jnp.min(jnp.where(els == m1, eidx, MOE_EXPERTS), axis=0, keepdims=True)
    els2 = jnp.where(eidx == i1, neg, els)
    m2 = jnp.max(els2, axis=0, keepdims=True)
    i2 = jnp.min(jnp.where(els2 == m2, eidx, MOE_EXPERTS), axis=0, keepdims=True)
    sel1 = eidx == i1
    sel2 = eidx == i2
    cnt = jnp.where(sel1 | sel2, 1.0, 0.0)

    lane = ROUTER_LANES
    before = (lax.broadcasted_iota(jnp.int32, (lane, lane), 0)
              < lax.broadcasted_iota(jnp.int32, (lane, lane), 1))
    tri = jnp.where(before, 1.0, 0.0).astype(BF16)
    run = cnt_ref[...]
    r1, r2 = [], []
    for k in range(tm // lane):
        piece = slice(k * lane, (k + 1) * lane)
        ck = cnt[:, piece]
        pos = run + jnp.dot(ck.astype(BF16), tri, preferred_element_type=F32)
        r1.append(jnp.sum(jnp.where(sel1[:, piece], pos, 0.0), axis=0, keepdims=True))
        r2.append(jnp.sum(jnp.where(sel2[:, piece], pos, 0.0), axis=0, keepdims=True))
        run = run + jnp.sum(ck, axis=1, keepdims=True)
    cnt_ref[...] = run
    rank = jnp.concatenate([jnp.concatenate(r1, axis=1), jnp.concatenate(r2, axis=1)], axis=0)
    eid = jnp.concatenate([i1, i2], axis=0)
    dest_ref[:, pl.ds(pl.multiple_of(t * tm, tm), tm)] = (eid << ROUTE_RANK_BITS) + rank.astype(jnp.int32)

    g_w = 1.0 / jnp.sum(jnp.exp(gl - gmax), axis=0, keepdims=True)
    e21 = jnp.exp(m2 - m1)
    p1 = 1.0 / (1.0 + e21)
    rid = lax.broadcasted_iota(jnp.int32, (ROUTER_LANES, tm), 0)
    gt = jnp.where(rid == 0, g_w * p1, jnp.where(rid == 1, g_w * (e21 * p1), 0.0))
    gates_ref[...] = gt.T


def _route_finish(dest_ref, meta_ref, cnt_ref):
    counts = cnt_ref[...]
    nblk = jnp.floor((counts + (EXP_BLOCK - 1.0)) * (1.0 / EXP_BLOCK))
    r = lax.broadcasted_iota(jnp.int32, (MOE_EXPERTS, MOE_EXPERTS), 0)
    c = lax.broadcasted_iota(jnp.int32, (MOE_EXPERTS, MOE_EXPERTS), 1)
    nblk_row = jnp.sum(jnp.where(r == c, nblk, 0.0), axis=0, keepdims=True)
    bstart = jnp.sum(jnp.where(c < r, nblk_row, 0.0), axis=1, keepdims=True)
    bend = bstart + nblk
    bidx = lax.broadcasted_iota(jnp.int32, (1, META_LANES), 1).astype(F32)
    blk_e = jnp.minimum(jnp.sum(jnp.where(bidx >= bend, 1.0, 0.0), axis=0, keepdims=True),
                        MOE_EXPERTS - 1.0)
    erow = lax.broadcasted_iota(jnp.int32, (MOE_EXPERTS, META_LANES), 0).astype(F32)
    mine = erow == blk_e
    cnt_b = jnp.sum(jnp.where(mine, counts, 0.0), axis=0, keepdims=True)
    start_b = jnp.sum(jnp.where(mine, bstart, 0.0), axis=0, keepdims=True)
    valid = jnp.clip(cnt_b - (bidx - start_b) * EXP_BLOCK, 0.0, float(EXP_BLOCK))
    n_blk = jnp.sum(nblk, axis=0, keepdims=True)
    end_b = jnp.sum(jnp.where(mine, bend, 0.0), axis=0, keepdims=True)
    nxt = jnp.minimum(jnp.sum(jnp.where(end_b >= bend, 1.0, 0.0), axis=0, keepdims=True),
                      MOE_EXPERTS - 1.0)
    nxt = jnp.where(end_b < n_blk, nxt, -1.0)
    mrow = lax.broadcasted_iota(jnp.int32, (8, META_LANES), 0)
    meta = jnp.where(mrow == 0, blk_e, jnp.where(mrow == 1, valid, jnp.where(
        mrow == 2, n_blk, jnp.where(mrow == 3, nxt, 0.0))))
    meta_ref[...] = meta.astype(jnp.int32)

    base = (bstart * EXP_BLOCK).astype(jnp.int32)
    packed = dest_ref[...]
    eid = packed >> ROUTE_RANK_BITS
    row = packed & (2 ** ROUTE_RANK_BITS - 1)
    for e in range(MOE_EXPERTS):
        row = row + jnp.where(eid == e, base[e:e + 1, :], 0)
    dest_ref[...] = row


def _route_call(logits):
    last = N_ROUTE_TILES - 1
    return pl.pallas_call(
        _route_kernel,
        grid=(N_ROUTE_TILES + 1,),
        in_specs=[pl.BlockSpec((ROUTE_TM, ROUTER_LANES), lambda t: (jnp.minimum(t, last), 0))],
        out_specs=[pl.BlockSpec((MOE_TOPK, T_ALL), lambda t: (0, 0)),
                   pl.BlockSpec((ROUTE_TM, ROUTER_LANES), lambda t: (jnp.minimum(t, last), 0)),
                   pl.BlockSpec((8, META_LANES), lambda t: (0, 0))],
        out_shape=[jax.ShapeDtypeStruct((MOE_TOPK, T_ALL), jnp.int32),
                   jax.ShapeDtypeStruct((T_ALL, ROUTER_LANES), F32),
                   jax.ShapeDtypeStruct((8, META_LANES), jnp.int32)],
        scratch_shapes=[pltpu.VMEM((MOE_EXPERTS, 1), F32)],
        compiler_params=_params("arbitrary"),
        name="route",
    )(logits)


def _sc_mesh():
    return plsc.VectorSubcoreMesh(core_axis_name="c", subcore_axis_name="s")


def _sc_token_offset(j):
    wid = lax.axis_index("s") * SC_CORES + lax.axis_index("c")
    return pl.multiple_of(wid * SC_ROWS_PER_WORKER + j * SC_CHUNK, 8)


SC_N_CHUNKS = SC_ROWS_PER_WORKER // SC_CHUNK
assert SC_N_CHUNKS % 2 == 1


def _start(copies):
    for cp in copies:
        cp.start()


def _wait(copies):
    for cp in copies:
        cp.wait()


def _dispatch_body(h_hbm, d0_hbm, d1_hbm, out_hbm, i0a, i1a, rows_a, i0b, i1b, rows_b, la, lb, sa, sb):
    sets = {"a": (i0a, i1a, rows_a, la, sa), "b": (i0b, i1b, rows_b, lb, sb)}

    def loads(j, s):
        i0, i1, rows, lsem, _ = sets[s]
        src = pl.ds(_sc_token_offset(j), SC_CHUNK)
        return (pltpu.make_async_copy(d0_hbm.at[src], i0, lsem),
                pltpu.make_async_copy(d1_hbm.at[src], i1, lsem),
                pltpu.make_async_copy(h_hbm.at[src], rows, lsem))

    def scatters(s):
        i0, i1, rows, _, ssem = sets[s]
        return (pltpu.make_async_copy(rows, out_hbm.at[i0], ssem),
                pltpu.make_async_copy(rows, out_hbm.at[i1], ssem))

    _start(loads(0, "a"))

    @pl.loop(0, SC_N_CHUNKS // 2)
    def _(p):
        a = 2 * p
        _start(loads(a + 1, "b"))
        _wait(loads(a, "a"))
        _start(scatters("a"))
        _wait(loads(a + 1, "b"))
        _start(scatters("b"))
        _wait(scatters("a"))
        _start(loads(a + 2, "a"))
        _wait(scatters("b"))

    _wait(loads(SC_N_CHUNKS - 1, "a"))
    _start(scatters("a"))
    _wait(scatters("a"))


def _dispatch_call(h2, dest0, dest1):
    buffers = [pltpu.VMEM((SC_CHUNK,), jnp.int32), pltpu.VMEM((SC_CHUNK,), jnp.int32),
               pltpu.VMEM((SC_CHUNK, DP), U32)]
    return pl.kernel(
        _dispatch_body,
        out_type=jax.ShapeDtypeStruct((P_ROWS, DP), U32),
        mesh=_sc_mesh(),
        scratch_types=buffers + buffers + [pltpu.SemaphoreType.DMA] * 4,
        name="moe_dispatch",
    )(h2, dest0, dest1)


def _combine_body(yb_hbm, d0_hbm, d1_hbm, o0_hbm, o1_hbm, i0a, i1a, r0a, r1a, i0b, i1b, r0b, r1b,
                  ia, ib, ga, gb, wa, wb):
    sets = {"a": (i0a, i1a, r0a, r1a, ia, ga, wa), "b": (i0b, i1b, r0b, r1b, ib, gb, wb)}

    def index_loads(j, s):
        i0, i1, _, _, isem, _, _ = sets[s]
        src = pl.ds(_sc_token_offset(j), SC_CHUNK)
        return (pltpu.make_async_copy(d0_hbm.at[src], i0, isem), pltpu.make_async_copy(d1_hbm.at[src], i1, isem))

    def gathers(s):
        i0, i1, r0, r1, _, gsem, _ = sets[s]
        return (pltpu.make_async_copy(yb_hbm.at[i0], r0, gsem), pltpu.make_async_copy(yb_hbm.at[i1], r1, gsem))

    def writes(j, s):
        _, _, r0, r1, _, _, wsem = sets[s]
        dst = pl.ds(_sc_token_offset(j), SC_CHUNK)
        return (pltpu.make_async_copy(r0, o0_hbm.at[dst], wsem), pltpu.make_async_copy(r1, o1_hbm.at[dst], wsem))

    _start(index_loads(0, "a"))
    _wait(index_loads(0, "a"))
    _start(gathers("a"))

    @pl.loop(0, SC_N_CHUNKS // 2)
    def _(p):
        a = 2 * p
        _start(index_loads(a + 1, "b"))
        _wait(index_loads(a + 1, "b"))
        _wait(gathers("a"))
        _start(writes(a, "a"))
        _start(gathers("b"))
        _start(index_loads(a + 2, "a"))
        _wait(index_loads(a + 2, "a"))
        _wait(writes(a, "a"))
        _wait(gathers("b"))
        _start(writes(a + 1, "b"))
        _start(gathers("a"))
        _wait(writes(a + 1, "b"))

    _wait(gathers("a"))
    _start(writes(SC_N_CHUNKS - 1, "a"))
    _wait(writes(SC_N_CHUNKS - 1, "a"))


def _combine_call(yb, dest0, dest1):
    out = jax.ShapeDtypeStruct((T_ALL, DP), U32)
    buffers = [pltpu.VMEM((SC_CHUNK,), jnp.int32), pltpu.VMEM((SC_CHUNK,), jnp.int32),
               pltpu.VMEM((SC_CHUNK, DP), U32), pltpu.VMEM((SC_CHUNK, DP), U32)]
    return pl.kernel(
        _combine_body,
        out_type=(out, out),
        mesh=_sc_mesh(),
        scratch_types=buffers + buffers + [pltpu.SemaphoreType.DMA] * 6,
        name="moe_combine",
    )(yb, dest0, dest1)


def _moe_rows(h2, dest, meta, w1, w3, w2, layer):
    dest0, dest1 = dest[0], dest[1]
    xb = _dispatch_call(h2, dest0, dest1)
    yb = _expert_call(meta[0, :N_EXP_BLOCKS], meta[1, :N_EXP_BLOCKS], meta[2, :1], meta[3, :N_EXP_BLOCKS],
                      xb, w1, w3, w2, layer)
    return _combine_call(yb, dest0, dest1)


def _final_kernel(is_prompt, x_ref, yg0_ref, yg1_ref, gates_ref, modp_ref, mods_ref, gain_ref, o_ref):
    x = _add_moe(x_ref[...], yg0_ref, yg1_ref, gates_ref, _tile_mod((modp_ref, mods_ref), is_prompt))
    o_ref[...] = _rms(x) * gain_ref[...]


def _final_call(x, prev, gain, row0, n_rows):
    rows = ROWS_FINAL
    tile0 = row0 // rows
    is_prompt = row0 < T_PROMPT
    assert row0 + n_rows <= T_PROMPT or not is_prompt

    def tile(width):
        return pl.BlockSpec((rows, width), lambda i: (tile0 + i, 0))

    mod_specs = [pl.BlockSpec(s.block_shape, lambda i, m=s.index_map: m(tile0 + i)) for s in _mod_specs(rows)]
    return pl.pallas_call(
        functools.partial(_final_kernel, is_prompt),
        grid=(n_rows // rows,),
        in_specs=[tile(D), tile(DP), tile(DP), tile(ROUTER_LANES), *mod_specs, _resident((1, D))],
        out_specs=pl.BlockSpec((rows, D), lambda i: (i, 0)),
        out_shape=jax.ShapeDtypeStruct((n_rows, D), F32),
        compiler_params=_params("parallel"),
        name="final_norm",
    )(x, *prev, gain)


def _rope_table():
    pos = np.concatenate([np.tile(np.arange(SEQ), BATCH),
                          np.tile(PAST_LEN + np.arange(DEC_SEQ), DEC_BATCH)]).astype(np.float32)
    inv = (ROPE_BASE ** (-np.arange(ROPE_HALF, dtype=np.float32) / ROPE_HALF)).astype(np.float32)
    ang = (pos[:, None] * inv[None, :]).astype(np.float32).astype(np.float64)
    pad = np.zeros((T_ALL, ROPE_TABLE_LANES - 2 * ROPE_HALF - 2 * RET_HEADS))
    return jnp.asarray(np.concatenate([np.cos(ang), np.sin(ang), _ret_row_scales(), pad], axis=1), F32)


def kernel(x_prompt, x_sample, c_prompt, c_sample, state_ret, ada_w, ada_b, norm1_g, norm2_g, ret_w_in,
           ret_w_out, gm_w_in, gm_b_in, gm_ln_g, gm_ln_b, gm_w_s, gm_b_s, gm_w_out, gm_b_out, moe_w_rg,
           moe_b_rg, moe_w_re, moe_b_re, moe_w1, moe_w3, moe_w2, final_g):
    x = (x_prompt.reshape(T_PROMPT, D), x_sample.reshape(T_SAMPLE, D))
    c_all = jnp.concatenate([c_prompt, c_sample], axis=0)
    rope = _rope_table()

    mod_all = _ada_call(c_all, ada_w, ada_b).reshape(DEPTH, N_SEQ, 6, D)

    def layer_params(i):
        mod = (mod_all[i, :BATCH], mod_all[i, BATCH:])
        w_r = jnp.pad(jnp.concatenate([moe_w_re[i], moe_w_rg[i]], axis=1),
                      ((0, 0), (0, ROUTER_LANES - MOE_GROUPS - MOE_EXPERTS)))
        w_r_hi = w_r.astype(BF16)
        w_r_lo = (w_r - w_r_hi.astype(F32)).astype(BF16)
        w_r = jnp.concatenate([w_r_hi, w_r_lo], axis=1)
        b_r = jnp.pad(jnp.concatenate([moe_b_re[i].reshape(-1), moe_b_rg[i]]),
                      (0, ROUTER_LANES - MOE_GROUPS - MOE_EXPERTS)).reshape(1, ROUTER_LANES)
        return mod, w_r, b_r

    ret_prompt = ret_sample = gm_sample = None
    prev = None
    for i in range(DEPTH):
        j = i // 2
        mod, w_r, b_r = layer_params(i)
        g1 = norm1_g[i].reshape(1, D)
        g2 = norm2_g[i].reshape(1, D)
        if i % 2 == 0:
            x, p = _ret_proj_call(x, prev, mod, g1, ret_w_in, j, rope)
            y_p, ret_prompt = _ret_core_call(p, None, ret_prompt, j, BATCH, SEQ, RET_CHUNK_PROMPT, 0)
            y_s, ret_sample = _ret_core_call(p, state_ret, ret_sample, j, DEC_BATCH, DEC_SEQ,
                                             RET_CHUNK_SAMPLE, T_PROMPT)
            x, h2, logits = _ret_out_call(y_p, y_s, ret_w_out, j, x, mod, g2, w_r, b_r)
        else:
            x, uv, gm_sample = _gm_proj_call(x, prev, mod, g1, gm_w_in, j, gm_b_in[j], gm_ln_g[j],
                                             gm_ln_b[j], gm_sample)
            ws_bd, bs_bd = _gm_block_diag(gm_w_s[j], gm_b_s[j])
            x, h2, logits = _gm_out_call(uv, ws_bd, bs_bd, gm_w_out, j,
                                         gm_b_out[j].reshape(1, D), x, mod, g2, w_r, b_r)
        dest, gates, meta = _route_call(logits)
        yg0, yg1 = _moe_rows(h2, dest, meta, moe_w1, moe_w3, moe_w2, i)
        prev = (yg0, yg1, gates, *mod)

    fg = final_g.reshape(1, D)
    y_prompt = _final_call(x, prev, fg, 0, T_PROMPT).reshape(BATCH, SEQ, D)
    y_sample = _final_call(x, prev, fg, T_PROMPT, T_SAMPLE).reshape(DEC_BATCH, DEC_SEQ, D)
    return (y_prompt, y_sample, ret_prompt, ret_sample,
            gm_sample.reshape(N_GM, DEC_BATCH, DEC_SEQ, GM_HALF))
```

```python
import functools

import numpy as np
import jax
import jax.numpy as jnp
from jax import lax
from jax.experimental import pallas as pl
from jax.experimental.pallas import tpu as pltpu
from jax.experimental.pallas import tpu_sc as plsc

F32 = jnp.float32
BF16 = jnp.bfloat16
U32 = jnp.uint32

D = 1024
BATCH, SEQ = 4, 4096
DEC_BATCH, DEC_SEQ = 16, 64
PAST_LEN = 4096
DEPTH = 4
N_RET = (DEPTH + 1) // 2
N_GM = DEPTH // 2
N_SEQ = BATCH + DEC_BATCH

RET_HEADS, RET_DK, RET_DV = 4, 256, 512
RET_QK = RET_HEADS * RET_DK
RET_V = RET_HEADS * RET_DV
RET_IN = 2 * RET_QK + 2 * RET_V
ROPE_BASE = 10000.0
ROPE_HALF = RET_DK // 2
ROPE_TABLE_LANES = 3 * ROPE_HALF

GM_FFN = 6 * D
GM_HALF = GM_FFN // 2
GM_GROUPS = 4
GM_GDIM = GM_HALF // GM_GROUPS
GM_CHUNK = 128

MOE_GROUPS, MOE_PER_GROUP = 4, 8
MOE_EXPERTS = MOE_GROUPS * MOE_PER_GROUP
MOE_TOPK = 2
MOE_HIDDEN = 512
EPS = 1e-6

GROUP = DEC_SEQ
T_PROMPT = BATCH * SEQ
T_SAMPLE = DEC_BATCH * DEC_SEQ
T_ALL = T_PROMPT + T_SAMPLE
N_GROUPS = T_ALL // GROUP
ROWS_WIDE = 512
ROWS_GM_PROJ = 256
ROWS_FINAL = 1024
ROWS_RET_OUT = 1024

RET_CHUNK_PROMPT = 256
RET_CHUNK_SAMPLE = DEC_SEQ

GM_MIX = 256

EXP_BLOCK = 1024
EXP_ROW_STEP = 128
N_ASSIGN = T_ALL * MOE_TOPK
N_EXP_BLOCKS = -(-(N_ASSIGN + MOE_EXPERTS * (EXP_BLOCK - 1)) // EXP_BLOCK)
P_ROWS = N_EXP_BLOCKS * EXP_BLOCK
ROUTER_LANES = 128
ROUTE_TM = 1024
N_ROUTE_TILES = T_ALL // ROUTE_TM
META_LANES = 256
assert META_LANES >= N_EXP_BLOCKS

DP = D // 2
SC_CORES, SC_SUBCORES = 2, 16
SC_WORKERS = SC_CORES * SC_SUBCORES
SC_ROWS_PER_WORKER = T_ALL // SC_WORKERS
SC_CHUNK = 32
assert SC_ROWS_PER_WORKER % SC_CHUNK == 0 and SC_CHUNK % 8 == 0

V7X_VMEM_LIMIT_BYTES = 56 * 1024 * 1024


def _params(*sem):
    return pltpu.CompilerParams(dimension_semantics=sem, vmem_limit_bytes=V7X_VMEM_LIMIT_BYTES)


def _resident(shape):
    nd = len(shape)
    return pl.BlockSpec(shape, lambda *_: (0,) * nd, pipeline_mode=pl.Buffered(1))


WEIGHT_STAGE_BYTES = 3 * 1024 * 1024


def _weight_scratch(k, n):
    chunk = k
    while chunk * n * 4 > WEIGHT_STAGE_BYTES:
        assert chunk % 16 == 0
        chunk //= 2
    return [pltpu.VMEM((k, n), BF16), pltpu.VMEM((2, chunk, n), F32), pltpu.SemaphoreType.DMA((2,))]


def _load_weight_bf16(w_hbm, layer, w_s, stage, sems):
    k = w_s.shape[0]
    chunk = stage.shape[1]

    def copy(c):
        return pltpu.make_async_copy(w_hbm.at[layer, pl.ds(c * chunk, chunk)], stage.at[c % 2], sems.at[c % 2])

    @pl.when(pl.program_id(0) == 0)
    def _():
        copy(0).start()
        for c in range(k // chunk):
            if c + 1 < k // chunk:
                copy(c + 1).start()
            copy(c).wait()
            w_s[c * chunk:(c + 1) * chunk, :] = stage[c % 2].astype(BF16)


def _rms(x):
    return x * lax.rsqrt(jnp.mean(x * x, axis=-1, keepdims=True) + EPS)


def _silu(x):
    return x * jax.nn.sigmoid(x)


def _per_group(x2d, fn):
    rows = x2d.shape[0]
    return fn(x2d.reshape(rows // GROUP, GROUP, D)).reshape(rows, D)


def _tile_mod(mod_refs, is_prompt):
    modp_ref, mods_ref = mod_refs
    return jnp.where(is_prompt, jnp.broadcast_to(modp_ref[...], mods_ref.shape), mods_ref[...])


def _norm_mod(x, gain_ref, mod, shift_idx):
    y = _rms(x) * gain_ref[...]
    scale = mod[:, shift_idx + 1:shift_idx + 2, :]
    shift = mod[:, shift_idx:shift_idx + 1, :]
    return _per_group(y, lambda y3: y3 * (1.0 + scale) + shift)


def _pack_bf16_pairs(x):
    lo = lax.bitcast_convert_type(x[:, :DP].astype(BF16).astype(F32), U32)
    hi = lax.bitcast_convert_type(x[:, DP:].astype(BF16).astype(F32), U32)
    return (lo >> 16) | (hi & U32(0xFFFF0000))


def _unpack_bf16_pairs(w):
    lo = lax.bitcast_convert_type(w << 16, F32)
    hi = lax.bitcast_convert_type(w & U32(0xFFFF0000), F32)
    return jnp.concatenate([lo, hi], axis=1)


def _add_moe(x, yg0_ref, yg1_ref, gates_ref, mod_prev):
    g = gates_ref[...]
    y = g[:, 0:1] * _unpack_bf16_pairs(yg0_ref[...]) + g[:, 1:2] * _unpack_bf16_pairs(yg1_ref[...])
    gate2 = mod_prev[:, 5:6, :]
    return x + _per_group(y, lambda y3: y3 * gate2)


ADA_TN = 3072


def _ada_kernel(c_ref, w_ref, b_ref, o_ref):
    c = c_ref[...]
    s = _silu(c).astype(BF16)
    o_ref[0] = jnp.dot(s, w_ref[0].astype(BF16), preferred_element_type=F32) + b_ref[0]


def _ada_call(c_all, ada_w, ada_b):
    return pl.pallas_call(
        _ada_kernel,
        grid=(DEPTH, 6 * D // ADA_TN),
        in_specs=[
            pl.BlockSpec((N_SEQ, D), lambda i, j: (0, 0)),
            pl.BlockSpec((1, D, ADA_TN), lambda i, j: (i, 0, j)),
            pl.BlockSpec((1, 1, ADA_TN), lambda i, j: (i, 0, j)),
        ],
        out_specs=pl.BlockSpec((1, N_SEQ, ADA_TN), lambda i, j: (i, 0, j)),
        out_shape=jax.ShapeDtypeStruct((DEPTH, N_SEQ, 6 * D), F32),
        compiler_params=_params("parallel", "parallel"),
        name="ada_modulation",
    )(c_all, ada_w, ada_b.reshape(DEPTH, 1, 6 * D))


def _n_tiles(rows):
    return T_ALL // rows


def _n_prompt_tiles(rows):
    return T_PROMPT // rows


def _row_spec(rows, width):
    return pl.BlockSpec((rows, width), lambda i: (i, 0))


def _mod_specs(rows):
    npt = _n_prompt_tiles(rows)
    return [pl.BlockSpec((1, 6, D), lambda i: (jnp.minimum(i * rows // SEQ, BATCH - 1), 0, 0)),
            pl.BlockSpec((rows // DEC_SEQ, 6, D), lambda i: (jnp.maximum(i - npt, 0), 0, 0))]


def _prev_specs(rows):
    return [_row_spec(rows, DP), _row_spec(rows, DP), _row_spec(rows, ROUTER_LANES)] + _mod_specs(rows)


def _stage_a_spec(spec, n_tiles):
    return pl.BlockSpec(spec.block_shape, lambda i, m=spec.index_map: m(jnp.minimum(i, n_tiles - 1)))


def _stage_b_spec(spec):
    return pl.BlockSpec(spec.block_shape, lambda i, m=spec.index_map: m(jnp.maximum(i - 1, 0)))


def _prompt_rows_spec(rows, width):
    last = _n_prompt_tiles(rows) - 1
    return pl.BlockSpec((rows, width), lambda i: (jnp.minimum(i, last), 0))


def _sample_rows_spec(rows, width):
    npt = _n_prompt_tiles(rows)
    return pl.BlockSpec((rows, width), lambda i: (jnp.maximum(i - npt, 0), 0))


def _ret_proj_kernel(has_prev, n_tiles, n_prompt_tiles, layer, *refs):
    i = pl.program_id(0)
    is_prompt = jnp.minimum(i, n_tiles - 1) < n_prompt_tiles
    head = refs[:6] if has_prev else refs[:2]
    refs = refs[len(head):]
    mod_refs = refs[:2]
    gain_ref, w_hbm, rope_ref, xo_ref, p_ref, hb_cur, hb_prev, w_s, w_stage, w_sems = refs[2:]
    _load_weight_bf16(w_hbm, layer, w_s, w_stage, w_sems)

    @pl.when(i == 0)
    def _():
        hb_prev[...] = jnp.zeros_like(hb_prev)

    def qk_head(j):
        lo = j * RET_DK
        acc = jnp.dot(hb_prev[...], w_s[:, lo:lo + RET_DK], preferred_element_type=F32)
        x1 = acc[:, :ROPE_HALF]
        x2 = acc[:, ROPE_HALF:]
        cos = rope_ref[:, 0:ROPE_HALF]
        sin = rope_ref[:, ROPE_HALF:2 * ROPE_HALF]
        scale = rope_ref[:, 2 * ROPE_HALF + j:2 * ROPE_HALF + j + 1]
        p_ref[:, lo:lo + ROPE_HALF] = ((x1 * cos - x2 * sin) * scale).astype(BF16)
        p_ref[:, lo + ROPE_HALF:lo + RET_DK] = ((x1 * sin + x2 * cos) * scale).astype(BF16)

    def vg_pair(j):
        lo = 2 * RET_QK + j * 2 * RET_DV
        acc = jnp.dot(hb_prev[...], w_s[:, lo:lo + 2 * RET_DV], preferred_element_type=F32)
        if j >= RET_HEADS // 2:
            acc = _silu(acc)
        p_ref[:, lo:lo + 2 * RET_DV] = acc.astype(BF16)

    vg_pair(0)
    if has_prev:
        x_ref, yg0_ref, yg1_ref, gates_ref = head[:4]
        x = _add_moe(x_ref[...], yg0_ref, yg1_ref, gates_ref, _tile_mod(head[4:6], is_prompt))
    else:
        x = jnp.where(is_prompt, head[0][...], head[1][...])
    xo_ref[...] = x
    hb_cur[...] = _norm_mod(x, gain_ref, _tile_mod(mod_refs, is_prompt), 0).astype(BF16)
    for j in range(1, RET_HEADS):
        vg_pair(j)
    for j in range(2 * RET_HEADS):
        qk_head(j)
    hb_prev[...] = hb_cur[...]


def _ret_proj_call(x, prev, mod, gain, w_in, layer, rope):
    rows = ROWS_WIDE
    n = _n_tiles(rows)
    has_prev = prev is not None
    if has_prev:
        in_specs = [_row_spec(rows, D)] + _prev_specs(rows)
        args = [x] + list(prev)
    else:
        in_specs = [_prompt_rows_spec(rows, D), _sample_rows_spec(rows, D)]
        args = list(x)
    in_specs = [_stage_a_spec(s, n) for s in in_specs + _mod_specs(rows)] + [
        _resident((1, D)), pl.BlockSpec(memory_space=pl.ANY),
        _stage_b_spec(_row_spec(rows, ROPE_TABLE_LANES))]
    args += [*mod, gain, w_in, rope]
    return pl.pallas_call(
        functools.partial(_ret_proj_kernel, has_prev, n, _n_prompt_tiles(rows), layer),
        grid=(n + 1,),
        in_specs=in_specs,
        out_specs=[_stage_a_spec(_row_spec(rows, D), n), _stage_b_spec(_row_spec(rows, RET_IN))],
        out_shape=[jax.ShapeDtypeStruct((T_ALL, D), F32), jax.ShapeDtypeStruct((T_ALL, RET_IN), BF16)],
        scratch_shapes=[pltpu.VMEM((rows, D), BF16), pltpu.VMEM((rows, D), BF16),
                        *_weight_scratch(D, RET_IN)],
        compiler_params=_params("arbitrary"),
        name="ret_proj",
    )(*args)


RET_SEQS_PER_STEP = 2


def _ret_core_kernel(has_s0, n_chunks, layer, *refs):
    refs = list(refs)
    p_refs = [refs.pop(0) for _ in range(RET_SEQS_PER_STEP)]
    s0_ref = refs.pop(0) if has_s0 else None
    causal_ref, cd_ref = refs[:2]
    y_ref, so_ref, s_ref = refs[-3:]
    c = pl.program_id(1)

    @pl.when(c == 0)
    def _():
        if has_s0:
            s_ref[...] = s0_ref[0]
        else:
            s_ref[...] = jnp.zeros_like(s_ref)

    for h in range(RET_HEADS):
        for q, p_ref in enumerate(p_refs):
            qb = p_ref[:, h * RET_DK:(h + 1) * RET_DK]
            kb = p_ref[:, RET_QK + h * RET_DK:RET_QK + (h + 1) * RET_DK]
            vb = p_ref[:, 2 * RET_QK + h * RET_DV:2 * RET_QK + (h + 1) * RET_DV]
            gb = p_ref[:, 2 * RET_QK + RET_V + h * RET_DV:2 * RET_QK + RET_V + (h + 1) * RET_DV]
            scores = lax.dot_general(qb, kb, (((1,), (1,)), ((), ())), preferred_element_type=F32)
            scores = scores * causal_ref[...]
            s_old = s_ref[q, h]
            o = (jnp.dot(scores.astype(BF16), vb, preferred_element_type=F32)
                 + jnp.dot(qb, s_old.astype(BF16), preferred_element_type=F32))
            s_ref[q, h] = cd_ref[h][:, 0:1] * (s_old + lax.dot_general(
                kb, vb, (((0,), (0,)), ((), ())), preferred_element_type=F32))
            y_ref[q, :, h * RET_DV:(h + 1) * RET_DV] = (gb.astype(F32) * _rms(o)).astype(BF16)

    @pl.when(c == n_chunks - 1)
    def _():
        so_ref[0] = s_ref[...]
        if layer == 0:
            for later in range(1, N_RET):
                so_ref[later] = jnp.zeros_like(s_ref)


def _ret_log_gamma():
    return np.log1p(-np.exp2(-5.0 - np.arange(RET_HEADS, dtype=np.float64)))


def _ret_chunk_tables(cl):
    idx = np.arange(cl)
    causal = (idx[:, None] >= idx[None, :]).astype(np.float32)
    cd = np.broadcast_to(np.exp(_ret_log_gamma() * cl)[:, None, None], (RET_HEADS, 1, 128))
    return jnp.asarray(causal, F32), jnp.asarray(cd, F32)


def _ret_row_scales():
    c = np.concatenate([np.arange(T_PROMPT) % RET_CHUNK_PROMPT,
                        np.arange(T_SAMPLE) % RET_CHUNK_SAMPLE]).astype(np.float64)
    e = (c[:, None] + 1.0) * _ret_log_gamma()[None, :]
    return np.concatenate([np.exp(e), np.exp(-e) * RET_DK ** -0.5], axis=1)


def _ret_core_call(p, s0, states, layer, n_seq, seq_len, cl, row0):
    has_s0 = s0 is not None
    n_chunks = seq_len // cl
    rb0 = row0 // cl
    per = RET_SEQS_PER_STEP
    assert n_seq % per == 0
    states_blk = (per, RET_HEADS, RET_DK, RET_DV)
    in_specs = [pl.BlockSpec((cl, RET_IN), lambda b, c, q=q: (rb0 + (per * b + q) * n_chunks + c, 0))
                for q in range(per)]
    args = [p] * per
    if has_s0:
        in_specs.append(pl.BlockSpec((1,) + states_blk, lambda b, c: (layer, b, 0, 0, 0)))
        args.append(s0)
    in_specs += [_resident((cl, cl)), _resident((RET_HEADS, 1, 128))]
    args += list(_ret_chunk_tables(cl))
    if layer == 0:
        assert states is None
        state_spec = pl.BlockSpec((N_RET,) + states_blk, lambda b, c: (0, b, 0, 0, 0))
        aliases = {}
    else:
        in_specs.append(pl.BlockSpec(memory_space=pl.ANY))
        args.append(states)
        state_spec = pl.BlockSpec((1,) + states_blk, lambda b, c: (layer, b, 0, 0, 0))
        aliases = {len(args) - 1: 1}
    y, states = pl.pallas_call(
        functools.partial(_ret_core_kernel, has_s0, n_chunks, layer),
        grid=(n_seq // per, n_chunks),
        in_specs=in_specs,
        out_specs=[pl.BlockSpec((per, cl, RET_V), lambda b, c: (b, c, 0)), state_spec],
        out_shape=[jax.ShapeDtypeStruct((n_seq, seq_len, RET_V), BF16),
                   jax.ShapeDtypeStruct((N_RET, n_seq) + states_blk[1:], F32)],
        scratch_shapes=[pltpu.VMEM(states_blk, F32)],
        input_output_aliases=aliases,
        compiler_params=_params("parallel", "arbitrary"),
        name="ret_core",
    )(*args)
    return y.reshape(n_seq * seq_len, RET_V), states


def _residual_router(acc, x_ref, mod, gain_ref, wr_ref, br_ref, xo_ref, h2_ref, lg_ref):
    gate1 = mod[:, 2:3, :]
    xn = x_ref[...] + _per_group(acc, lambda a3: a3 * gate1)
    xo_ref[...] = xn
    h2 = _norm_mod(xn, gain_ref, mod, 3)
    h2_ref[...] = _pack_bf16_pairs(h2)
    hh = jnp.dot(h2.astype(BF16), wr_ref[...], preferred_element_type=F32)
    lg_ref[...] = hh[:, :ROUTER_LANES] + hh[:, ROUTER_LANES:] + br_ref[...]


def _mix_out_specs(rows):
    return [_row_spec(rows, D), _row_spec(rows, DP), _row_spec(rows, ROUTER_LANES)]


_MIX_OUT_SHAPE = [
    jax.ShapeDtypeStruct((T_ALL, D), F32),
    jax.ShapeDtypeStruct((T_ALL, DP), U32),
    jax.ShapeDtypeStruct((T_ALL, ROUTER_LANES), F32),
]


def _router_specs():
    return [_resident((D, 2 * ROUTER_LANES)), _resident((1, ROUTER_LANES))]


def _ret_out_kernel(n_prompt_tiles, layer, yp_ref, ys_ref, w_hbm, x_ref, modp_ref, mods_ref, gain_ref,
                    wr_ref, br_ref, xo_ref, h2_ref, lg_ref, w_s, w_stage, w_sems):
    _load_weight_bf16(w_hbm, layer, w_s, w_stage, w_sems)
    is_prompt = pl.program_id(0) < n_prompt_tiles
    yin = jnp.where(is_prompt, yp_ref[...], ys_ref[...])
    acc = jnp.dot(yin, w_s[...], preferred_element_type=F32)
    mod = _tile_mod((modp_ref, mods_ref), is_prompt)
    _residual_router(acc, x_ref, mod, gain_ref, wr_ref, br_ref, xo_ref, h2_ref, lg_ref)


def _ret_out_call(y_prompt, y_sample, w_out, layer, x, mod, gain, w_r, b_r):
    rows = ROWS_RET_OUT
    return pl.pallas_call(
        functools.partial(_ret_out_kernel, _n_prompt_tiles(rows), layer),
        grid=(_n_tiles(rows),),
        in_specs=[_prompt_rows_spec(rows, RET_V), _sample_rows_spec(rows, RET_V),
                  pl.BlockSpec(memory_space=pl.ANY), _row_spec(rows, D), *_mod_specs(rows),
                  _resident((1, D))] + _router_specs(),
        out_specs=_mix_out_specs(rows),
        out_shape=_MIX_OUT_SHAPE,
        scratch_shapes=_weight_scratch(RET_V, D),
        compiler_params=_params("arbitrary"),
        name="ret_out",
    )(y_prompt, y_sample, w_out, x, *mod, gain, w_r, b_r)


GM_TN = 1024


_GELU_C = float(np.sqrt(2.0 / np.pi))


def _gelu_tanh(x):
    hx = 0.5 * x
    return hx * jnp.tanh(x * (_GELU_C + (_GELU_C * 0.044715) * (x * x))) + hx


def _gm_proj_kernel(n_tiles, n_prompt_tiles, layer, *refs):
    x_ref, yg0_ref, yg1_ref, gates_ref = refs[:4]
    gain_ref, w_hbm, b_ref, lg_ref, lb_ref = refs[8:13]
    (xo_ref, uv_ref, vs_ref, hb_cur, hb_prev, vraw_cur, vraw_prev, stat_prev, sum_s,
     w_s, w_stage, w_sems) = refs[-12:]
    _load_weight_bf16(w_hbm, layer, w_s, w_stage, w_sems)
    i = pl.program_id(0)
    rows = x_ref.shape[0]

    @pl.when(i == 0)
    def _():
        hb_prev[...] = jnp.zeros_like(hb_prev)
        vraw_prev[...] = jnp.zeros_like(vraw_prev)
        stat_prev[...] = jnp.zeros_like(stat_prev)

    def add_row(x, row_ref, lo, width):
        x3 = x.reshape(rows // 8, 8, width) + row_ref[:, lo:lo + width]
        return x3.reshape(rows, width)

    def mul_row(x, row_ref, lo, width):
        x3 = x.reshape(rows // 8, 8, width) * row_ref[:, lo:lo + width]
        return x3.reshape(rows, width)

    def proj_chunk(hb_ref, lo):
        z = jnp.dot(hb_ref[...], w_s[:, lo:lo + GM_TN], preferred_element_type=F32)
        return add_row(z, b_ref, lo, GM_TN).astype(BF16)

    def stage_b_chunk(lo):
        uv_ref[:, lo:lo + GM_TN] = _gelu_tanh(proj_chunk(hb_prev, lo))
        for k in range(lo, lo + GM_TN, 128):
            vk = vraw_prev[:, k:k + 128].astype(F32) * stat_prev[:, 0:128] + stat_prev[:, 128:256]
            vn = add_row(mul_row(vk, lg_ref, k, 128), lb_ref, k, 128)
            uv_ref[:, GM_HALF + k:GM_HALF + k + 128] = vn.astype(BF16)
            vs_ref[0, :, k:k + 128] = vn

    def stage_a_chunk(n, lo):
        gz = _gelu_tanh(proj_chunk(hb_cur, GM_HALF + lo))
        vraw_cur[:, lo:lo + GM_TN] = gz
        gf = gz.astype(F32)
        pieces = [gf[:, k:k + 128] for k in range(0, GM_TN, 128)]
        t1 = functools.reduce(lambda p, q: p + q, pieces)
        t2 = functools.reduce(lambda p, q: p + q, [p * p for p in pieces])
        if n == 0:
            sum_s[:, 0:128] = t1
            sum_s[:, 128:256] = t2
        else:
            sum_s[:, 0:128] += t1
            sum_s[:, 128:256] += t2

    chunks = list(range(0, GM_HALF, GM_TN))
    stage_b_chunk(chunks[0])
    a_is_prompt = jnp.minimum(i, n_tiles - 1) < n_prompt_tiles
    x = _add_moe(x_ref[...], yg0_ref, yg1_ref, gates_ref, _tile_mod(refs[4:6], a_is_prompt))
    xo_ref[...] = x
    hb_cur[...] = _norm_mod(x, gain_ref, _tile_mod(refs[6:8], a_is_prompt), 0).astype(BF16)
    for n, lo in enumerate(chunks):
        stage_a_chunk(n, lo)
        if 0 < n < len(chunks) - 1:
            stage_b_chunk(lo)
    mu = jnp.sum(sum_s[:, 0:128], axis=-1, keepdims=True) * (1.0 / GM_HALF)
    var = jnp.sum(sum_s[:, 128:256], axis=-1, keepdims=True) * (1.0 / GM_HALF) - mu * mu
    rstd = lax.rsqrt(var + EPS)
    stage_b_chunk(chunks[-1])
    stat_prev[:, 0:128] = jnp.broadcast_to(rstd, (rows, 128))
    stat_prev[:, 128:256] = jnp.broadcast_to(-mu * rstd, (rows, 128))
    hb_prev[...] = hb_cur[...]
    vraw_prev[...] = vraw_cur[...]

    if layer == 0:
        @pl.when(i - 1 >= n_prompt_tiles)
        def _():
            for later in range(1, N_GM):
                vs_ref[later] = jnp.zeros((rows, GM_HALF), F32)


def _gm_proj_call(x, prev, mod, gain, w_in, layer, b_in, ln_g, ln_b, vs_all):
    rows = ROWS_GM_PROJ
    n, npt = _n_tiles(rows), _n_prompt_tiles(rows)

    def stage_a(spec):
        return _stage_a_spec(spec, n)

    stage_b = _stage_b_spec
    in_specs = [stage_a(s) for s in [_row_spec(rows, D)] + _prev_specs(rows) + _mod_specs(rows)] + [
        _resident((1, D)), pl.BlockSpec(memory_space=pl.ANY), _resident((8, GM_FFN)),
        _resident((8, GM_HALF)), _resident((8, GM_HALF))]
    rows8 = [jnp.broadcast_to(r.reshape(1, -1), (8, r.size)) for r in (b_in, ln_g, ln_b)]
    args = [x, *prev, *mod, gain, w_in, *rows8]
    if layer == 0:
        assert vs_all is None
        vs_spec = pl.BlockSpec((N_GM, rows, GM_HALF), lambda i: (0, jnp.maximum(i - 1 - npt, 0), 0))
        aliases = {}
    else:
        in_specs.append(pl.BlockSpec(memory_space=pl.ANY))
        args.append(vs_all)
        vs_spec = pl.BlockSpec((1, rows, GM_HALF), lambda i: (layer, jnp.maximum(i - 1 - npt, 0), 0))
        aliases = {len(args) - 1: 2}
    return pl.pallas_call(
        functools.partial(_gm_proj_kernel, n, npt, layer),
        grid=(n + 1,),
        in_specs=in_specs,
        out_specs=[stage_a(_row_spec(rows, D)), stage_b(_row_spec(rows, GM_FFN)), vs_spec],
        out_shape=[jax.ShapeDtypeStruct((T_ALL, D), F32),
                   jax.ShapeDtypeStruct((T_ALL, GM_FFN), BF16),
                   jax.ShapeDtypeStruct((N_GM, T_SAMPLE, GM_HALF), F32)],
        scratch_shapes=[pltpu.VMEM((rows, D), BF16), pltpu.VMEM((rows, D), BF16),
                        pltpu.VMEM((rows, GM_HALF), BF16), pltpu.VMEM((rows, GM_HALF), BF16),
                        pltpu.VMEM((rows, 256), F32), pltpu.VMEM((rows, 256), F32),
                        *_weight_scratch(D, GM_FFN)],
        input_output_aliases=aliases,
        compiler_params=_params("arbitrary"),
        name="gm_proj",
    )(*args)


def _gm_out_kernel(n_prompt_tiles, layer, uv_ref, ws_ref, bs_ref, w_hbm, bo_ref, x_ref, modp_ref,
                   mods_ref, gain_ref, wr_ref, br_ref, xo_ref, h2_ref, lg_ref, w_s, w_stage, w_sems):
    _load_weight_bf16(w_hbm, layer, w_s, w_stage, w_sems)
    rows = uv_ref.shape[0]
    mod = _tile_mod((modp_ref, mods_ref), pl.program_id(0) < n_prompt_tiles)
    pieces = []
    for r0 in range(0, rows, GM_MIX):
        acc = jnp.zeros((GM_MIX, D), F32)
        for g in range(GM_GROUPS):
            lo = g * GM_GDIM
            sp = jnp.dot(ws_ref[0, g], uv_ref[r0:r0 + GM_MIX, GM_HALF + lo:GM_HALF + lo + GM_GDIM],
                         preferred_element_type=F32) + bs_ref[0, g]
            gated = (uv_ref[r0:r0 + GM_MIX, lo:lo + GM_GDIM].astype(F32) * sp).astype(BF16)
            acc = acc + jnp.dot(gated, w_s[lo:lo + GM_GDIM, :], preferred_element_type=F32)
        pieces.append(acc)
    acc = jnp.concatenate(pieces, axis=0) + bo_ref[...]
    _residual_router(acc, x_ref, mod, gain_ref, wr_ref, br_ref, xo_ref, h2_ref, lg_ref)


def _gm_block_diag(w_s, b_s):
    mats, biases = [], []
    for cl in (GM_CHUNK, DEC_SEQ):
        tri = jnp.tril(jnp.ones((cl, cl), bool))
        blk = jnp.where(tri[None], w_s[:, :cl, :cl], 0.0)
        reps = GM_MIX // cl
        eye = jnp.eye(reps, dtype=w_s.dtype)
        bd = jnp.einsum("ab,gts->gatbs", eye, blk).reshape(GM_GROUPS, GM_MIX, GM_MIX)
        mats.append(bd)
        biases.append(jnp.tile(b_s[:, :cl], (1, reps))[:, :, None])
    return jnp.stack(mats).astype(BF16), jnp.stack(biases).astype(F32)


def _gm_out_call(uv, ws_bd, bs_bd, w_out, layer, b_out, x, mod, gain, w_r, b_r):
    rows = ROWS_WIDE
    npt = _n_prompt_tiles(rows)

    def variant(i):
        return jnp.where(i >= npt, 1, 0)

    return pl.pallas_call(
        functools.partial(_gm_out_kernel, npt, layer),
        grid=(_n_tiles(rows),),
        in_specs=[_row_spec(rows, GM_FFN),
                  pl.BlockSpec((1, GM_GROUPS, GM_MIX, GM_MIX), lambda i: (variant(i), 0, 0, 0)),
                  pl.BlockSpec((1, GM_GROUPS, GM_MIX, 1), lambda i: (variant(i), 0, 0, 0)),
                  pl.BlockSpec(memory_space=pl.ANY), _resident((1, D)), _row_spec(rows, D),
                  *_mod_specs(rows), _resident((1, D))] + _router_specs(),
        out_specs=_mix_out_specs(rows),
        out_shape=_MIX_OUT_SHAPE,
        scratch_shapes=_weight_scratch(GM_HALF, D),
        compiler_params=_params("arbitrary"),
        name="gm_out",
    )(uv, ws_bd, bs_bd, w_out, b_out, x, *mod, gain, w_r, b_r)


def _expert_kernel(layer, be_ref, bv_ref, nb_ref, nx_ref, xb_ref, w1_hbm, w3_hbm, w2_hbm, yb_ref,
                   st1, st3, st2, w1s, w3s, w2s, sems, slot_ref):
    b = pl.program_id(0)

    def weight_copies(e, slot):
        return (pltpu.make_async_copy(w1_hbm.at[layer, e], st1.at[slot], sems.at[slot, 0]),
                pltpu.make_async_copy(w3_hbm.at[layer, e], st3.at[slot], sems.at[slot, 1]),
                pltpu.make_async_copy(w2_hbm.at[layer, e], st2.at[slot], sems.at[slot, 2]))

    @pl.when(b == 0)
    def _():
        slot_ref[0] = 0
        for cp in weight_copies(be_ref[0], 0):
            cp.start()

    @pl.when(b < nb_ref[0])
    def _():
        prev_e = be_ref[jnp.maximum(b - 1, 0)]

        @pl.when((b == 0) | (be_ref[b] != prev_e))
        def _():
            slot = slot_ref[0]
            for cp in weight_copies(be_ref[b], slot):
                cp.wait()
            w1s[...] = st1[slot].astype(BF16)
            w3s[...] = st3[slot].astype(BF16)
            w2s[...] = st2[slot].astype(BF16)

            @pl.when(nx_ref[b] >= 0)
            def _():
                for cp in weight_copies(nx_ref[b], 1 - slot):
                    cp.start()

            slot_ref[0] = 1 - slot

        valid = bv_ref[b]

        def run_rows(n):
            row = lax.broadcasted_iota(jnp.int32, (n, 1), 0)
            xw = jnp.where(row < valid, xb_ref[0:n], U32(0))
            x = _unpack_bf16_pairs(xw).astype(BF16)
            a = jnp.dot(x, w1s[...], preferred_element_type=F32)
            c = jnp.dot(x, w3s[...], preferred_element_type=F32)
            h = (_silu(a) * c).astype(BF16)
            yb_ref[0:n] = _pack_bf16_pairs(jnp.dot(h, w2s[...], preferred_element_type=F32))
            if n < EXP_BLOCK:
                yb_ref[n:EXP_BLOCK] = jnp.zeros((EXP_BLOCK - n, DP), U32)

        for n in range(EXP_ROW_STEP, EXP_BLOCK + 1, EXP_ROW_STEP):
            pl.when((valid > n - EXP_ROW_STEP) & (valid <= n))(functools.partial(run_rows, n))


def _expert_call(blk_e, blk_valid, n_blk, blk_next, xb, w1, w3, w2, layer):
    def blk(b, be, bv, nb, nx):
        return (jnp.minimum(b, nb[0] - 1), 0)

    up, down = (D, MOE_HIDDEN), (MOE_HIDDEN, D)
    grid_spec = pltpu.PrefetchScalarGridSpec(
        num_scalar_prefetch=4,
        grid=(N_EXP_BLOCKS,),
        in_specs=[pl.BlockSpec((EXP_BLOCK, DP), blk)] + [pl.BlockSpec(memory_space=pl.ANY)] * 3,
        out_specs=pl.BlockSpec((EXP_BLOCK, DP), blk),
        scratch_shapes=[pltpu.VMEM((2,) + up, F32), pltpu.VMEM((2,) + up, F32), pltpu.VMEM((2,) + down, F32),
                        pltpu.VMEM(up, BF16), pltpu.VMEM(up, BF16), pltpu.VMEM(down, BF16),
                        pltpu.SemaphoreType.DMA((2, 3)), pltpu.SMEM((1,), jnp.int32)],
    )
    return pl.pallas_call(
        functools.partial(_expert_kernel, layer),
        grid_spec=grid_spec,
        out_shape=jax.ShapeDtypeStruct((P_ROWS, DP), U32),
        compiler_params=_params("arbitrary"),
        name="experts",
    )(blk_e, blk_valid, n_blk, blk_next, xb, w1, w3, w2)


def _route_kernel(lg_ref, dest_ref, gates_ref, meta_ref, cnt_ref):
    t = pl.program_id(0)

    @pl.when(t == 0)
    def _():
        cnt_ref[...] = jnp.zeros_like(cnt_ref)

    pl.when(t < N_ROUTE_TILES)(functools.partial(_route_tile, t, lg_ref, dest_ref, gates_ref, cnt_ref))
    pl.when(t == N_ROUTE_TILES)(functools.partial(_route_finish, dest_ref, meta_ref, cnt_ref))


ROUTE_RANK_BITS = 16
assert N_ASSIGN < 2 ** ROUTE_RANK_BITS


def _route_tile(t, lg_ref, dest_ref, gates_ref, cnt_ref):
    tm = ROUTE_TM
    lt = lg_ref[...].T
    el = lt[0:MOE_EXPERTS]
    gl = lt[MOE_EXPERTS:MOE_EXPERTS + 8]
    gidx = lax.broadcasted_iota(jnp.int32, (8, tm), 0)
    neg = jnp.float32(-jnp.inf)
    gl = jnp.where(gidx < MOE_GROUPS, gl, neg)
    gmax = jnp.max(gl, axis=0, keepdims=True)
    grp = jnp.min(jnp.where(gl == gmax, gidx, MOE_GROUPS), axis=0, keepdims=True)
    eidx = lax.broadcasted_iota(jnp.int32, (MOE_EXPERTS, tm), 0)
    els = jnp.where((eidx >> 3) == grp, el, neg)
    m1 = jnp.max(els, axis=0, keepdims=True)
    i1 = jnp.min(jnp.where(els == m1, eidx, MOE_EXPERTS), axis=0, keepdims=True)
    els2 = jnp.where(eidx == i1, neg, els)
    m2 = jnp.max(els2, axis=0, keepdims=True)
    i2 = jnp.min(jnp.where(els2 == m2, eidx, MOE_EXPERTS), axis=0, keepdims=True)
    sel1 = eidx == i1
    sel2 = eidx == i2
    cnt = jnp.where(sel1 | sel2, 1.0, 0.0)

    lane = ROUTER_LANES
    before = (lax.broadcasted_iota(jnp.int32, (lane, lane), 0)
              < lax.broadcasted_iota(jnp.int32, (lane, lane), 1))
    tri = jnp.where(before, 1.0, 0.0).astype(BF16)
    run = cnt_ref[...]
    r1, r2 = [], []
    for k in range(tm // lane):
        piece = slice(k * lane, (k + 1) * lane)
        ck = cnt[:, piece]
        pos = run + jnp.dot(ck.astype(BF16), tri, preferred_element_type=F32)
        r1.append(jnp.sum(jnp.where(sel1[:, piece], pos, 0.0), axis=0, keepdims=True))
        r2.append(jnp.sum(jnp.where(sel2[:, piece], pos, 0.0), axis=0, keepdims=True))
        run = run + jnp.sum(ck, axis=1, keepdims=True)
    cnt_ref[...] = run
    rank = jnp.concatenate([jnp.concatenate(r1, axis=1), jnp.concatenate(r2, axis=1)], axis=0)
    eid = jnp.concatenate([i1, i2], axis=0)
    dest_ref[:, pl.ds(pl.multiple_of(t * tm, tm), tm)] = (eid << ROUTE_RANK_BITS) + rank.astype(jnp.int32)

    g_w = 1.0 / jnp.sum(jnp.exp(gl - gmax), axis=0, keepdims=True)
    e21 = jnp.exp(m2 - m1)
    p1 = 1.0 / (1.0 + e21)
    rid = lax.broadcasted_iota(jnp.int32, (ROUTER_LANES, tm), 0)
    gt = jnp.where(rid == 0, g_w * p1, jnp.where(rid == 1, g_w * (e21 * p1), 0.0))
    gates_ref[...] = gt.T


def _route_finish(dest_ref, meta_ref, cnt_ref):
    counts = cnt_ref[...]
    nblk = jnp.floor((counts + (EXP_BLOCK - 1.0)) * (1.0 / EXP_BLOCK))
    r = lax.broadcasted_iota(jnp.int32, (MOE_EXPERTS, MOE_EXPERTS), 0)
    c = lax.broadcasted_iota(jnp.int32, (MOE_EXPERTS, MOE_EXPERTS), 1)
    nblk_row = jnp.sum(jnp.where(r == c, nblk, 0.0), axis=0, keepdims=True)
    bstart = jnp.sum(jnp.where(c < r, nblk_row, 0.0), axis=1, keepdims=True)
    bend = bstart + nblk
    bidx = lax.broadcasted_iota(jnp.int32, (1, META_LANES), 1).astype(F32)
    blk_e = jnp.minimum(jnp.sum(jnp.where(bidx >= bend, 1.0, 0.0), axis=0, keepdims=True),
                        MOE_EXPERTS - 1.0)
    erow = lax.broadcasted_iota(jnp.int32, (MOE_EXPERTS, META_LANES), 0).astype(F32)
    mine = erow == blk_e
    cnt_b = jnp.sum(jnp.where(mine, counts, 0.0), axis=0, keepdims=True)
    start_b = jnp.sum(jnp.where(mine, bstart, 0.0), axis=0, keepdims=True)
    valid = jnp.clip(cnt_b - (bidx - start_b) * EXP_BLOCK, 0.0, float(EXP_BLOCK))
    n_blk = jnp.sum(nblk, axis=0, keepdims=True)
    end_b = jnp.sum(jnp.where(mine, bend, 0.0), axis=0, keepdims=True)
    nxt = jnp.minimum(jnp.sum(jnp.where(end_b >= bend, 1.0, 0.0), axis=0, keepdims=True),
                      MOE_EXPERTS - 1.0)
    nxt = jnp.where(end_b < n_blk, nxt, -1.0)
    mrow = lax.broadcasted_iota(jnp.int32, (8, META_LANES), 0)
    meta = jnp.where(mrow == 0, blk_e, jnp.where(mrow == 1, valid, jnp.where(
        mrow == 2, n_blk, jnp.where(mrow == 3, nxt, 0.0))))
    meta_ref[...] = meta.astype(jnp.int32)

    base = (bstart * EXP_BLOCK).astype(jnp.int32)
    packed = dest_ref[...]
    eid = packed >> ROUTE_RANK_BITS
    row = packed & (2 ** ROUTE_RANK_BITS - 1)
    for e in range(MOE_EXPERTS):
        row = row + jnp.where(eid == e, base[e:e + 1, :], 0)
    dest_ref[...] = row


def _route_call(logits):
    last = N_ROUTE_TILES - 1
    return pl.pallas_call(
        _route_kernel,
        grid=(N_ROUTE_TILES + 1,),
        in_specs=[pl.BlockSpec((ROUTE_TM, ROUTER_LANES), lambda t: (jnp.minimum(t, last), 0))],
        out_specs=[pl.BlockSpec((MOE_TOPK, T_ALL), lambda t: (0, 0)),
                   pl.BlockSpec((ROUTE_TM, ROUTER_LANES), lambda t: (jnp.minimum(t, last), 0)),
                   pl.BlockSpec((8, META_LANES), lambda t: (0, 0))],
        out_shape=[jax.ShapeDtypeStruct((MOE_TOPK, T_ALL), jnp.int32),
                   jax.ShapeDtypeStruct((T_ALL, ROUTER_LANES), F32),
                   jax.ShapeDtypeStruct((8, META_LANES), jnp.int32)],
        scratch_shapes=[pltpu.VMEM((MOE_EXPERTS, 1), F32)],
        compiler_params=_params("arbitrary"),
        name="route",
    )(logits)


def _sc_mesh():
    return plsc.VectorSubcoreMesh(core_axis_name="c", subcore_axis_name="s")


def _sc_token_offset(j):
    wid = lax.axis_index("s") * SC_CORES + lax.axis_index("c")
    return pl.multiple_of(wid * SC_ROWS_PER_WORKER + j * SC_CHUNK, 8)


SC_N_CHUNKS = SC_ROWS_PER_WORKER // SC_CHUNK
assert SC_N_CHUNKS % 2 == 1


def _start(copies):
    for cp in copies:
        cp.start()


def _wait(copies):
    for cp in copies:
        cp.wait()


def _dispatch_body(h_hbm, d0_hbm, d1_hbm, out_hbm, i0a, i1a, rows_a, i0b, i1b, rows_b, la, lb, sa, sb):
    sets = {"a": (i0a, i1a, rows_a, la, sa), "b": (i0b, i1b, rows_b, lb, sb)}

    def loads(j, s):
        i0, i1, rows, lsem, _ = sets[s]
        src = pl.ds(_sc_token_offset(j), SC_CHUNK)
        return (pltpu.make_async_copy(d0_hbm.at[src], i0, lsem),
                pltpu.make_async_copy(d1_hbm.at[src], i1, lsem),
                pltpu.make_async_copy(h_hbm.at[src], rows, lsem))

    def scatters(s):
        i0, i1, rows, _, ssem = sets[s]
        return (pltpu.make_async_copy(rows, out_hbm.at[i0], ssem),
                pltpu.make_async_copy(rows, out_hbm.at[i1], ssem))

    _start(loads(0, "a"))

    @pl.loop(0, SC_N_CHUNKS // 2)
    def _(p):
        a = 2 * p
        _start(loads(a + 1, "b"))
        _wait(loads(a, "a"))
        _start(scatters("a"))
        _wait(loads(a + 1, "b"))
        _start(scatters("b"))
        _wait(scatters("a"))
        _start(loads(a + 2, "a"))
        _wait(scatters("b"))

    _wait(loads(SC_N_CHUNKS - 1, "a"))
    _start(scatters("a"))
    _wait(scatters("a"))


def _dispatch_call(h2, dest0, dest1):
    buffers = [pltpu.VMEM((SC_CHUNK,), jnp.int32), pltpu.VMEM((SC_CHUNK,), jnp.int32),
               pltpu.VMEM((SC_CHUNK, DP), U32)]
    return pl.kernel(
        _dispatch_body,
        out_type=jax.ShapeDtypeStruct((P_ROWS, DP), U32),
        mesh=_sc_mesh(),
        scratch_types=buffers + buffers + [pltpu.SemaphoreType.DMA] * 4,
        name="moe_dispatch",
    )(h2, dest0, dest1)


def _combine_body(yb_hbm, d0_hbm, d1_hbm, o0_hbm, o1_hbm, i0a, i1a, r0a, r1a, i0b, i1b, r0b, r1b,
                  ia, ib, ga, gb, wa, wb):
    sets = {"a": (i0a, i1a, r0a, r1a, ia, ga, wa), "b": (i0b, i1b, r0b, r1b, ib, gb, wb)}

    def index_loads(j, s):
        i0, i1, _, _, isem, _, _ = sets[s]
        src = pl.ds(_sc_token_offset(j), SC_CHUNK)
        return (pltpu.make_async_copy(d0_hbm.at[src], i0, isem), pltpu.make_async_copy(d1_hbm.at[src], i1, isem))

    def gathers(s):
        i0, i1, r0, r1, _, gsem, _ = sets[s]
        return (pltpu.make_async_copy(yb_hbm.at[i0], r0, gsem), pltpu.make_async_copy(yb_hbm.at[i1], r1, gsem))

    def writes(j, s):
        _, _, r0, r1, _, _, wsem = sets[s]
        dst = pl.ds(_sc_token_offset(j), SC_CHUNK)
        return (pltpu.make_async_copy(r0, o0_hbm.at[dst], wsem), pltpu.make_async_copy(r1, o1_hbm.at[dst], wsem))

    _start(index_loads(0, "a"))
    _wait(index_loads(0, "a"))
    _start(gathers("a"))

    @pl.loop(0, SC_N_CHUNKS // 2)
    def _(p):
        a = 2 * p
        _start(index_loads(a + 1, "b"))
        _wait(index_loads(a + 1, "b"))
        _wait(gathers("a"))
        _start(writes(a, "a"))
        _start(gathers("b"))
        _start(index_loads(a + 2, "a"))
        _wait(index_loads(a + 2, "a"))
        _wait(writes(a, "a"))
        _wait(gathers("b"))
        _start(writes(a + 1, "b"))
        _start(gathers("a"))
        _wait(writes(a + 1, "b"))

    _wait(gathers("a"))
    _start(writes(SC_N_CHUNKS - 1, "a"))
    _wait(writes(SC_N_CHUNKS - 1, "a"))


def _combine_call(yb, dest0, dest1):
    out = jax.ShapeDtypeStruct((T_ALL, DP), U32)
    buffers = [pltpu.VMEM((SC_CHUNK,), jnp.int32), pltpu.VMEM((SC_CHUNK,), jnp.int32),
               pltpu.VMEM((SC_CHUNK, DP), U32), pltpu.VMEM((SC_CHUNK, DP), U32)]
    return pl.kernel(
        _combine_body,
        out_type=(out, out),
        mesh=_sc_mesh(),
        scratch_types=buffers + buffers + [pltpu.SemaphoreType.DMA] * 6,
        name="moe_combine",
    )(yb, dest0, dest1)


def _moe_rows(h2, dest, meta, w1, w3, w2, layer):
    dest0, dest1 = dest[0], dest[1]
    xb = _dispatch_call(h2, dest0, dest1)
    yb = _expert_call(meta[0, :N_EXP_BLOCKS], meta[1, :N_EXP_BLOCKS], meta[2, :1], meta[3, :N_EXP_BLOCKS],
                      xb, w1, w3, w2, layer)
    return _combine_call(yb, dest0, dest1)


def _final_kernel(is_prompt, x_ref, yg0_ref, yg1_ref, gates_ref, modp_ref, mods_ref, gain_ref, o_ref):
    x = _add_moe(x_ref[...], yg0_ref, yg1_ref, gates_ref, _tile_mod((modp_ref, mods_ref), is_prompt))
    o_ref[...] = _rms(x) * gain_ref[...]


def _final_call(x, prev, gain, row0, n_rows):
    rows = ROWS_FINAL
    tile0 = row0 // rows
    is_prompt = row0 < T_PROMPT
    assert row0 + n_rows <= T_PROMPT or not is_prompt

    def tile(width):
        return pl.BlockSpec((rows, width), lambda i: (tile0 + i, 0))

    mod_specs = [pl.BlockSpec(s.block_shape, lambda i, m=s.index_map: m(tile0 + i)) for s in _mod_specs(rows)]
    return pl.pallas_call(
        functools.partial(_final_kernel, is_prompt),
        grid=(n_rows // rows,),
        in_specs=[tile(D), tile(DP), tile(DP), tile(ROUTER_LANES), *mod_specs, _resident((1, D))],
        out_specs=pl.BlockSpec((rows, D), lambda i: (i, 0)),
        out_shape=jax.ShapeDtypeStruct((n_rows, D), F32),
        compiler_params=_params("parallel"),
        name="final_norm",
    )(x, *prev, gain)


def _rope_table():
    pos = np.concatenate([np.tile(np.arange(SEQ), BATCH),
                          np.tile(PAST_LEN + np.arange(DEC_SEQ), DEC_BATCH)]).astype(np.float32)
    inv = (ROPE_BASE ** (-np.arange(ROPE_HALF, dtype=np.float32) / ROPE_HALF)).astype(np.float32)
    ang = (pos[:, None] * inv[None, :]).astype(np.float32).astype(np.float64)
    pad = np.zeros((T_ALL, ROPE_TABLE_LANES - 2 * ROPE_HALF - 2 * RET_HEADS))
    return jnp.asarray(np.concatenate([np.cos(ang), np.sin(ang), _ret_row_scales(), pad], axis=1), F32)


def kernel(x_prompt, x_sample, c_prompt, c_sample, state_ret, ada_w, ada_b, norm1_g, norm2_g, ret_w_in,
           ret_w_out, gm_w_in, gm_b_in, gm_ln_g, gm_ln_b, gm_w_s, gm_b_s, gm_w_out, gm_b_out, moe_w_rg,
           moe_b_rg, moe_w_re, moe_b_re, moe_w1, moe_w3, moe_w2, final_g):
    x = (x_prompt.reshape(T_PROMPT, D), x_sample.reshape(T_SAMPLE, D))
    c_all = jnp.concatenate([c_prompt, c_sample], axis=0)
    rope = _rope_table()

    mod_all = _ada_call(c_all, ada_w, ada_b).reshape(DEPTH, N_SEQ, 6, D)

    def layer_params(i):
        mod = (mod_all[i, :BATCH], mod_all[i, BATCH:])
        w_r = jnp.pad(jnp.concatenate([moe_w_re[i], moe_w_rg[i]], axis=1),
                      ((0, 0), (0, ROUTER_LANES - MOE_GROUPS - MOE_EXPERTS)))
        w_r_hi = w_r.astype(BF16)
        w_r_lo = (w_r - w_r_hi.astype(F32)).astype(BF16)
        w_r = jnp.concatenate([w_r_hi, w_r_lo], axis=1)
        b_r = jnp.pad(jnp.concatenate([moe_b_re[i].reshape(-1), moe_b_rg[i]]),
                      (0, ROUTER_LANES - MOE_GROUPS - MOE_EXPERTS)).reshape(1, ROUTER_LANES)
        return mod, w_r, b_r

    ret_prompt = ret_sample = gm_sample = None
    prev = None
    for i in range(DEPTH):
        j = i // 2
        mod, w_r, b_r = layer_params(i)
        g1 = norm1_g[i].reshape(1, D)
        g2 = norm2_g[i].reshape(1, D)
        if i % 2 == 0:
            x, p = _ret_proj_call(x, prev, mod, g1, ret_w_in, j, rope)
            y_p, ret_prompt = _ret_core_call(p, None, ret_prompt, j, BATCH, SEQ, RET_CHUNK_PROMPT, 0)
            y_s, ret_sample = _ret_core_call(p, state_ret, ret_sample, j, DEC_BATCH, DEC_SEQ,
                                             RET_CHUNK_SAMPLE, T_PROMPT)
            x, h2, logits = _ret_out_call(y_p, y_s, ret_w_out, j, x, mod, g2, w_r, b_r)
        else:
            x, uv, gm_sample = _gm_proj_call(x, prev, mod, g1, gm_w_in, j, gm_b_in[j], gm_ln_g[j],
                                             gm_ln_b[j], gm_sample)
            ws_bd, bs_bd = _gm_block_diag(gm_w_s[j], gm_b_s[j])
            x, h2, logits = _gm_out_call(uv, ws_bd, bs_bd, gm_w_out, j,
                                         gm_b_out[j].reshape(1, D), x, mod, g2, w_r, b_r)
        dest, gates, meta = _route_call(logits)
        yg0, yg1 = _moe_rows(h2, dest, meta, moe_w1, moe_w3, moe_w2, i)
        prev = (yg0, yg1, gates, *mod)

    fg = final_g.reshape(1, D)
    y_prompt = _final_call(x, prev, fg, 0, T_PROMPT).reshape(BATCH, SEQ, D)
    y_sample = _final_call(x, prev, fg, T_PROMPT, T_SAMPLE).reshape(DEC_BATCH, DEC_SEQ, D)
    return (y_prompt, y_sample, ret_prompt, ret_sample,
            gm_sample.reshape(N_GM, DEC_BATCH, DEC_SEQ, GM_HALF))
```

```python
import functools

import numpy as np
import jax
import jax.numpy as jnp
from jax import lax
from jax.experimental import pallas as pl
from jax.experimental.pallas import tpu as pltpu
from jax.experimental.pallas import tpu_sc as plsc

F32 = jnp.float32
BF16 = jnp.bfloat16
U32 = jnp.uint32

D = 1024
BATCH, SEQ = 4, 4096
DEC_BATCH, DEC_SEQ = 16, 64
PAST_LEN = 4096
DEPTH = 4
N_RET = (DEPTH + 1) // 2
N_GM = DEPTH // 2
N_SEQ = BATCH + DEC_BATCH

RET_HEADS, RET_DK, RET_DV = 4, 256, 512
RET_QK = RET_HEADS * RET_DK
RET_V = RET_HEADS * RET_DV
RET_IN = 2 * RET_QK + 2 * RET_V
ROPE_BASE = 10000.0
ROPE_HALF = RET_DK // 2
ROPE_TABLE_LANES = 3 * ROPE_HALF

GM_FFN = 6 * D
GM_HALF = GM_FFN // 2
GM_GROUPS = 4
GM_GDIM = GM_HALF // GM_GROUPS
GM_CHUNK = 128

MOE_GROUPS, MOE_PER_GROUP = 4, 8
MOE_EXPERTS = MOE_GROUPS * MOE_PER_GROUP
MOE_TOPK = 2
MOE_HIDDEN = 512
EPS = 1e-6

GROUP = DEC_SEQ
T_PROMPT = BATCH * SEQ
T_SAMPLE = DEC_BATCH * DEC_SEQ
T_ALL = T_PROMPT + T_SAMPLE
N_GROUPS = T_ALL // GROUP
ROWS_WIDE = 512
ROWS_GM_PROJ = 256
ROWS_FINAL = 1024
ROWS_RET_OUT = 1024

RET_CHUNK_PROMPT = 256
RET_CHUNK_SAMPLE = DEC_SEQ

GM_MIX = 256

EXP_BLOCK = 1024
EXP_ROW_STEP = 128
N_ASSIGN = T_ALL * MOE_TOPK
N_EXP_BLOCKS = -(-(N_ASSIGN + MOE_EXPERTS * (EXP_BLOCK - 1)) // EXP_BLOCK)
P_ROWS = N_EXP_BLOCKS * EXP_BLOCK
ROUTER_LANES = 128
ROUTE_TM = 1024
N_ROUTE_TILES = T_ALL // ROUTE_TM
META_LANES = 256
assert META_LANES >= N_EXP_BLOCKS

DP = D // 2
SC_CORES, SC_SUBCORES = 2, 16
SC_WORKERS = SC_CORES * SC_SUBCORES
SC_ROWS_PER_WORKER = T_ALL // SC_WORKERS
SC_CHUNK = 32
assert SC_ROWS_PER_WORKER % SC_CHUNK == 0 and SC_CHUNK % 8 == 0

V7X_VMEM_LIMIT_BYTES = 56 * 1024 * 1024


def _params(*sem):
    return pltpu.CompilerParams(dimension_semantics=sem, vmem_limit_bytes=V7X_VMEM_LIMIT_BYTES)


def _resident(shape):
    nd = len(shape)
    return pl.BlockSpec(shape, lambda *_: (0,) * nd, pipeline_mode=pl.Buffered(1))


WEIGHT_STAGE_BYTES = 3 * 1024 * 1024


def _weight_scratch(k, n):
    chunk = k
    while chunk * n * 4 > WEIGHT_STAGE_BYTES:
        assert chunk % 16 == 0
        chunk //= 2
    return [pltpu.VMEM((k, n), BF16), pltpu.VMEM((2, chunk, n), F32), pltpu.SemaphoreType.DMA((2,))]


def _load_weight_bf16(w_hbm, layer, w_s, stage, sems):
    k = w_s.shape[0]
    chunk = stage.shape[1]

    def copy(c):
        return pltpu.make_async_copy(w_hbm.at[layer, pl.ds(c * chunk, chunk)], stage.at[c % 2], sems.at[c % 2])

    @pl.when(pl.program_id(0) == 0)
    def _():
        copy(0).start()
        for c in range(k // chunk):
            if c + 1 < k // chunk:
                copy(c + 1).start()
            copy(c).wait()
            w_s[c * chunk:(c + 1) * chunk, :] = stage[c % 2].astype(BF16)


def _rms(x):
    return x * lax.rsqrt(jnp.mean(x * x, axis=-1, keepdims=True) + EPS)


def _silu(x):
    return x * jax.nn.sigmoid(x)


def _per_group(x2d, fn):
    rows = x2d.shape[0]
    return fn(x2d.reshape(rows // GROUP, GROUP, D)).reshape(rows, D)


def _tile_mod(mod_refs, is_prompt):
    modp_ref, mods_ref = mod_refs
    return jnp.where(is_prompt, jnp.broadcast_to(modp_ref[...], mods_ref.shape), mods_ref[...])


def _norm_mod(x, gain_ref, mod, shift_idx):
    y = _rms(x) * gain_ref[...]
    scale = mod[:, shift_idx + 1:shift_idx + 2, :]
    shift = mod[:, shift_idx:shift_idx + 1, :]
    return _per_group(y, lambda y3: y3 * (1.0 + scale) + shift)


def _pack_bf16_pairs(x):
    lo = lax.bitcast_convert_type(x[:, :DP].astype(BF16).astype(F32), U32)
    hi = lax.bitcast_convert_type(x[:, DP:].astype(BF16).astype(F32), U32)
    return (lo >> 16) | (hi & U32(0xFFFF0000))


def _unpack_bf16_pairs(w):
    lo = lax.bitcast_convert_type(w << 16, F32)
    hi = lax.bitcast_convert_type(w & U32(0xFFFF0000), F32)
    return jnp.concatenate([lo, hi], axis=1)


def _add_moe(x, yg0_ref, yg1_ref, gates_ref, mod_prev):
    g = gates_ref[...]
    y = g[:, 0:1] * _unpack_bf16_pairs(yg0_ref[...]) + g[:, 1:2] * _unpack_bf16_pairs(yg1_ref[...])
    gate2 = mod_prev[:, 5:6, :]
    return x + _per_group(y, lambda y3: y3 * gate2)


ADA_TN = 3072


def _ada_kernel(c_ref, w_ref, b_ref, o_ref):
    c = c_ref[...]
    s = _silu(c).astype(BF16)
    o_ref[0] = jnp.dot(s, w_ref[0].astype(BF16), preferred_element_type=F32) + b_ref[0]


def _ada_call(c_all, ada_w, ada_b):
    return pl.pallas_call(
        _ada_kernel,
        grid=(DEPTH, 6 * D // ADA_TN),
        in_specs=[
            pl.BlockSpec((N_SEQ, D), lambda i, j: (0, 0)),
            pl.BlockSpec((1, D, ADA_TN), lambda i, j: (i, 0, j)),
            pl.BlockSpec((1, 1, ADA_TN), lambda i, j: (i, 0, j)),
        ],
        out_specs=pl.BlockSpec((1, N_SEQ, ADA_TN), lambda i, j: (i, 0, j)),
        out_shape=jax.ShapeDtypeStruct((DEPTH, N_SEQ, 6 * D), F32),
        compiler_params=_params("parallel", "parallel"),
        name="ada_modulation",
    )(c_all, ada_w, ada_b.reshape(DEPTH, 1, 6 * D))


def _n_tiles(rows):
    return T_ALL // rows


def _n_prompt_tiles(rows):
    return T_PROMPT // rows


def _row_spec(rows, width):
    return pl.BlockSpec((rows, width), lambda i: (i, 0))


def _mod_specs(rows):
    npt = _n_prompt_tiles(rows)
    return [pl.BlockSpec((1, 6, D), lambda i: (jnp.minimum(i * rows // SEQ, BATCH - 1), 0, 0)),
            pl.BlockSpec((rows // DEC_SEQ, 6, D), lambda i: (jnp.maximum(i - npt, 0), 0, 0))]


def _prev_specs(rows):
    return [_row_spec(rows, DP), _row_spec(rows, DP), _row_spec(rows, ROUTER_LANES)] + _mod_specs(rows)


def _stage_a_spec(spec, n_tiles):
    return pl.BlockSpec(spec.block_shape, lambda i, m=spec.index_map: m(jnp.minimum(i, n_tiles - 1)))


def _stage_b_spec(spec):
    return pl.BlockSpec(spec.block_shape, lambda i, m=spec.index_map: m(jnp.maximum(i - 1, 0)))


def _prompt_rows_spec(rows, width):
    last = _n_prompt_tiles(rows) - 1
    return pl.BlockSpec((rows, width), lambda i: (jnp.minimum(i, last), 0))


def _sample_rows_spec(rows, width):
    npt = _n_prompt_tiles(rows)
    return pl.BlockSpec((rows, width), lambda i: (jnp.maximum(i - npt, 0), 0))


def _ret_proj_kernel(has_prev, n_tiles, n_prompt_tiles, layer, *refs):
    i = pl.program_id(0)
    is_prompt = jnp.minimum(i, n_tiles - 1) < n_prompt_tiles
    head = refs[:6] if has_prev else refs[:2]
    refs = refs[len(head):]
    mod_refs = refs[:2]
    gain_ref, w_hbm, rope_ref, xo_ref, p_ref, hb_cur, hb_prev, w_s, w_stage, w_sems = refs[2:]
    _load_weight_bf16(w_hbm, layer, w_s, w_stage, w_sems)

    @pl.when(i == 0)
    def _():
        hb_prev[...] = jnp.zeros_like(hb_prev)

    def qk_head(j):
        lo = j * RET_DK
        acc = jnp.dot(hb_prev[...], w_s[:, lo:lo + RET_DK], preferred_element_type=F32)
        x1 = acc[:, :ROPE_HALF]
        x2 = acc[:, ROPE_HALF:]
        cos = rope_ref[:, 0:ROPE_HALF]
        sin = rope_ref[:, ROPE_HALF:2 * ROPE_HALF]
        scale = rope_ref[:, 2 * ROPE_HALF + j:2 * ROPE_HALF + j + 1]
        p_ref[:, lo:lo + ROPE_HALF] = ((x1 * cos - x2 * sin) * scale).astype(BF16)
        p_ref[:, lo + ROPE_HALF:lo + RET_DK] = ((x1 * sin + x2 * cos) * scale).astype(BF16)

    def vg_head(j):
        lo = 2 * RET_QK + j * RET_DV
        acc = jnp.dot(hb_prev[...], w_s[:, lo:lo + RET_DV], preferred_element_type=F32)
        if j >= RET_HEADS:
            acc = _silu(acc)
        p_ref[:, lo:lo + RET_DV] = acc.astype(BF16)

    vg_head(0)
    vg_head(1)
    if has_prev:
        x_ref, yg0_ref, yg1_ref, gates_ref = head[:4]
        x = _add_moe(x_ref[...], yg0_ref, yg1_ref, gates_ref, _tile_mod(head[4:6], is_prompt))
    else:
        x = jnp.where(is_prompt, head[0][...], head[1][...])
    xo_ref[...] = x
    hb_cur[...] = _norm_mod(x, gain_ref, _tile_mod(mod_refs, is_prompt), 0).astype(BF16)
    for j in range(2, 2 * RET_HEADS):
        vg_head(j)
    for j in range(2 * RET_HEADS):
        qk_head(j)
    hb_prev[...] = hb_cur[...]


def _ret_proj_call(x, prev, mod, gain, w_in, layer, rope):
    rows = ROWS_WIDE
    n = _n_tiles(rows)
    has_prev = prev is not None
    if has_prev:
        in_specs = [_row_spec(rows, D)] + _prev_specs(rows)
        args = [x] + list(prev)
    else:
        in_specs = [_prompt_rows_spec(rows, D), _sample_rows_spec(rows, D)]
        args = list(x)
    in_specs = [_stage_a_spec(s, n) for s in in_specs + _mod_specs(rows)] + [
        _resident((1, D)), pl.BlockSpec(memory_space=pl.ANY),
        _stage_b_spec(_row_spec(rows, ROPE_TABLE_LANES))]
    args += [*mod, gain, w_in, rope]
    return pl.pallas_call(
        functools.partial(_ret_proj_kernel, has_prev, n, _n_prompt_tiles(rows), layer),
        grid=(n + 1,),
        in_specs=in_specs,
        out_specs=[_stage_a_spec(_row_spec(rows, D), n), _stage_b_spec(_row_spec(rows, RET_IN))],
        out_shape=[jax.ShapeDtypeStruct((T_ALL, D), F32), jax.ShapeDtypeStruct((T_ALL, RET_IN), BF16)],
        scratch_shapes=[pltpu.VMEM((rows, D), BF16), pltpu.VMEM((rows, D), BF16),
                        *_weight_scratch(D, RET_IN)],
        compiler_params=_params("arbitrary"),
        name="ret_proj",
    )(*args)


RET_SEQS_PER_STEP = 2


def _ret_core_kernel(has_s0, n_chunks, layer, *refs):
    refs = list(refs)
    p_refs = [refs.pop(0) for _ in range(RET_SEQS_PER_STEP)]
    s0_ref = refs.pop(0) if has_s0 else None
    causal_ref, cd_ref = refs[:2]
    y_ref, so_ref, s_ref = refs[-3:]
    c = pl.program_id(1)

    @pl.when(c == 0)
    def _():
        if has_s0:
            s_ref[...] = s0_ref[0]
        else:
            s_ref[...] = jnp.zeros_like(s_ref)

    for h in range(RET_HEADS):
        for q, p_ref in enumerate(p_refs):
            qb = p_ref[:, h * RET_DK:(h + 1) * RET_DK]
            kb = p_ref[:, RET_QK + h * RET_DK:RET_QK + (h + 1) * RET_DK]
            vb = p_ref[:, 2 * RET_QK + h * RET_DV:2 * RET_QK + (h + 1) * RET_DV]
            gb = p_ref[:, 2 * RET_QK + RET_V + h * RET_DV:2 * RET_QK + RET_V + (h + 1) * RET_DV]
            scores = lax.dot_general(qb, kb, (((1,), (1,)), ((), ())), preferred_element_type=F32)
            scores = scores * causal_ref[...]
            s_old = s_ref[q, h]
            o = (jnp.dot(scores.astype(BF16), vb, preferred_element_type=F32)
                 + jnp.dot(qb, s_old.astype(BF16), preferred_element_type=F32))
            s_ref[q, h] = cd_ref[h][:, 0:1] * (s_old + lax.dot_general(
                kb, vb, (((0,), (0,)), ((), ())), preferred_element_type=F32))
            y_ref[q, :, h * RET_DV:(h + 1) * RET_DV] = (gb.astype(F32) * _rms(o)).astype(BF16)

    @pl.when(c == n_chunks - 1)
    def _():
        so_ref[0] = s_ref[...]
        if layer == 0:
            for later in range(1, N_RET):
                so_ref[later] = jnp.zeros_like(s_ref)


def _ret_log_gamma():
    return np.log1p(-np.exp2(-5.0 - np.arange(RET_HEADS, dtype=np.float64)))


def _ret_chunk_tables(cl):
    idx = np.arange(cl)
    causal = (idx[:, None] >= idx[None, :]).astype(np.float32)
    cd = np.broadcast_to(np.exp(_ret_log_gamma() * cl)[:, None, None], (RET_HEADS, 1, 128))
    return jnp.asarray(causal, F32), jnp.asarray(cd, F32)


def _ret_row_scales():
    c = np.concatenate([np.arange(T_PROMPT) % RET_CHUNK_PROMPT,
                        np.arange(T_SAMPLE) % RET_CHUNK_SAMPLE]).astype(np.float64)
    e = (c[:, None] + 1.0) * _ret_log_gamma()[None, :]
    return np.concatenate([np.exp(e), np.exp(-e) * RET_DK ** -0.5], axis=1)


def _ret_core_call(p, s0, states, layer, n_seq, seq_len, cl, row0):
    has_s0 = s0 is not None
    n_chunks = seq_len // cl
    rb0 = row0 // cl
    per = RET_SEQS_PER_STEP
    assert n_seq % per == 0
    states_blk = (per, RET_HEADS, RET_DK, RET_DV)
    in_specs = [pl.BlockSpec((cl, RET_IN), lambda b, c, q=q: (rb0 + (per * b + q) * n_chunks + c, 0))
                for q in range(per)]
    args = [p] * per
    if has_s0:
        in_specs.append(pl.BlockSpec((1,) + states_blk, lambda b, c: (layer, b, 0, 0, 0)))
        args.append(s0)
    in_specs += [_resident((cl, cl)), _resident((RET_HEADS, 1, 128))]
    args += list(_ret_chunk_tables(cl))
    if layer == 0:
        assert states is None
        state_spec = pl.BlockSpec((N_RET,) + states_blk, lambda b, c: (0, b, 0, 0, 0))
        aliases = {}
    else:
        in_specs.append(pl.BlockSpec(memory_space=pl.ANY))
        args.append(states)
        state_spec = pl.BlockSpec((1,) + states_blk, lambda b, c: (layer, b, 0, 0, 0))
        aliases = {len(args) - 1: 1}
    y, states = pl.pallas_call(
        functools.partial(_ret_core_kernel, has_s0, n_chunks, layer),
        grid=(n_seq // per, n_chunks),
        in_specs=in_specs,
        out_specs=[pl.BlockSpec((per, cl, RET_V), lambda b, c: (b, c, 0)), state_spec],
        out_shape=[jax.ShapeDtypeStruct((n_seq, seq_len, RET_V), BF16),
                   jax.ShapeDtypeStruct((N_RET, n_seq) + states_blk[1:], F32)],
        scratch_shapes=[pltpu.VMEM(states_blk, F32)],
        input_output_aliases=aliases,
        compiler_params=_params("parallel", "arbitrary"),
        name="ret_core",
    )(*args)
    return y.reshape(n_seq * seq_len, RET_V), states


def _residual_router(acc, x_ref, mod, gain_ref, wr_ref, br_ref, xo_ref, h2_ref, lg_ref):
    gate1 = mod[:, 2:3, :]
    xn = x_ref[...] + _per_group(acc, lambda a3: a3 * gate1)
    xo_ref[...] = xn
    h2 = _norm_mod(xn, gain_ref, mod, 3)
    h2_ref[...] = _pack_bf16_pairs(h2)
    hh = jnp.dot(h2.astype(BF16), wr_ref[...], preferred_element_type=F32)
    lg_ref[...] = hh[:, :ROUTER_LANES] + hh[:, ROUTER_LANES:] + br_ref[...]


def _mix_out_specs(rows):
    return [_row_spec(rows, D), _row_spec(rows, DP), _row_spec(rows, ROUTER_LANES)]


_MIX_OUT_SHAPE = [
    jax.ShapeDtypeStruct((T_ALL, D), F32),
    jax.ShapeDtypeStruct((T_ALL, DP), U32),
    jax.ShapeDtypeStruct((T_ALL, ROUTER_LANES), F32),
]


def _router_specs():
    return [_resident((D, 2 * ROUTER_LANES)), _resident((1, ROUTER_LANES))]


def _ret_out_kernel(n_prompt_tiles, layer, yp_ref, ys_ref, w_hbm, x_ref, modp_ref, mods_ref, gain_ref,
                    wr_ref, br_ref, xo_ref, h2_ref, lg_ref, w_s, w_stage, w_sems):
    _load_weight_bf16(w_hbm, layer, w_s, w_stage, w_sems)
    is_prompt = pl.program_id(0) < n_prompt_tiles
    yin = jnp.where(is_prompt, yp_ref[...], ys_ref[...])
    acc = jnp.dot(yin, w_s[...], preferred_element_type=F32)
    mod = _tile_mod((modp_ref, mods_ref), is_prompt)
    _residual_router(acc, x_ref, mod, gain_ref, wr_ref, br_ref, xo_ref, h2_ref, lg_ref)


def _ret_out_call(y_prompt, y_sample, w_out, layer, x, mod, gain, w_r, b_r):
    rows = ROWS_RET_OUT
    return pl.pallas_call(
        functools.partial(_ret_out_kernel, _n_prompt_tiles(rows), layer),
        grid=(_n_tiles(rows),),
        in_specs=[_prompt_rows_spec(rows, RET_V), _sample_rows_spec(rows, RET_V),
                  pl.BlockSpec(memory_space=pl.ANY), _row_spec(rows, D), *_mod_specs(rows),
                  _resident((1, D))] + _router_specs(),
        out_specs=_mix_out_specs(rows),
        out_shape=_MIX_OUT_SHAPE,
        scratch_shapes=_weight_scratch(RET_V, D),
        compiler_params=_params("arbitrary"),
        name="ret_out",
    )(y_prompt, y_sample, w_out, x, *mod, gain, w_r, b_r)


GM_TN = 1024


_GELU_C = float(np.sqrt(2.0 / np.pi))


def _gelu_tanh(x):
    hx = 0.5 * x
    return hx * jnp.tanh(x * (_GELU_C + (_GELU_C * 0.044715) * (x * x))) + hx


def _gm_proj_kernel(n_tiles, n_prompt_tiles, layer, *refs):
    x_ref, yg0_ref, yg1_ref, gates_ref = refs[:4]
    gain_ref, w_hbm, b_ref, lg_ref, lb_ref = refs[8:13]
    (xo_ref, uv_ref, vs_ref, hb_cur, hb_prev, vraw_cur, vraw_prev, stat_prev, sum_s,
     w_s, w_stage, w_sems) = refs[-12:]
    _load_weight_bf16(w_hbm, layer, w_s, w_stage, w_sems)
    i = pl.program_id(0)
    rows = x_ref.shape[0]

    @pl.when(i == 0)
    def _():
        hb_prev[...] = jnp.zeros_like(hb_prev)
        vraw_prev[...] = jnp.zeros_like(vraw_prev)
        stat_prev[...] = jnp.zeros_like(stat_prev)

    def add_row(x, row_ref, lo, width):
        x3 = x.reshape(rows // 8, 8, width) + row_ref[:, lo:lo + width]
        return x3.reshape(rows, width)

    def mul_row(x, row_ref, lo, width):
        x3 = x.reshape(rows // 8, 8, width) * row_ref[:, lo:lo + width]
        return x3.reshape(rows, width)

    def proj_chunk(hb_ref, lo):
        z = jnp.dot(hb_ref[...], w_s[:, lo:lo + GM_TN], preferred_element_type=F32)
        return add_row(z, b_ref, lo, GM_TN).astype(BF16)

    def stage_b_chunk(lo):
        uv_ref[:, lo:lo + GM_TN] = _gelu_tanh(proj_chunk(hb_prev, lo))
        for k in range(lo, lo + GM_TN, 128):
            vk = vraw_prev[:, k:k + 128].astype(F32) * stat_prev[:, 0:128] + stat_prev[:, 128:256]
            vn = add_row(mul_row(vk, lg_ref, k, 128), lb_ref, k, 128)
            uv_ref[:, GM_HALF + k:GM_HALF + k + 128] = vn.astype(BF16)
            vs_ref[0, :, k:k + 128] = vn

    def stage_a_chunk(n, lo):
        gz = _gelu_tanh(proj_chunk(hb_cur, GM_HALF + lo))
        vraw_cur[:, lo:lo + GM_TN] = gz
        gf = gz.astype(F32)
        pieces = [gf[:, k:k + 128] for k in range(0, GM_TN, 128)]
        t1 = functools.reduce(lambda p, q: p + q, pieces)
        t2 = functools.reduce(lambda p, q: p + q, [p * p for p in pieces])
        if n == 0:
            sum_s[:, 0:128] = t1
            sum_s[:, 128:256] = t2
        else:
            sum_s[:, 0:128] += t1
            sum_s[:, 128:256] += t2

    chunks = list(range(0, GM_HALF, GM_TN))
    stage_b_chunk(chunks[0])
    a_is_prompt = jnp.minimum(i, n_tiles - 1) < n_prompt_tiles
    x = _add_moe(x_ref[...], yg0_ref, yg1_ref, gates_ref, _tile_mod(refs[4:6], a_is_prompt))
    xo_ref[...] = x
    hb_cur[...] = _norm_mod(x, gain_ref, _tile_mod(refs[6:8], a_is_prompt), 0).astype(BF16)
    for n, lo in enumerate(chunks):
        stage_a_chunk(n, lo)
        if 0 < n < len(chunks) - 1:
            stage_b_chunk(lo)
    mu = jnp.sum(sum_s[:, 0:128], axis=-1, keepdims=True) * (1.0 / GM_HALF)
    var = jnp.sum(sum_s[:, 128:256], axis=-1, keepdims=True) * (1.0 / GM_HALF) - mu * mu
    rstd = lax.rsqrt(var + EPS)
    stage_b_chunk(chunks[-1])
    stat_prev[:, 0:128] = jnp.broadcast_to(rstd, (rows, 128))
    stat_prev[:, 128:256] = jnp.broadcast_to(-mu * rstd, (rows, 128))
    hb_prev[...] = hb_cur[...]
    vraw_prev[...] = vraw_cur[...]

    if layer == 0:
        @pl.when(i - 1 >= n_prompt_tiles)
        def _():
            for later in range(1, N_GM):
                vs_ref[later] = jnp.zeros((rows, GM_HALF), F32)


def _gm_proj_call(x, prev, mod, gain, w_in, layer, b_in, ln_g, ln_b, vs_all):
    rows = ROWS_GM_PROJ
    n, npt = _n_tiles(rows), _n_prompt_tiles(rows)

    def stage_a(spec):
        return _stage_a_spec(spec, n)

    stage_b = _stage_b_spec
    in_specs = [stage_a(s) for s in [_row_spec(rows, D)] + _prev_specs(rows) + _mod_specs(rows)] + [
        _resident((1, D)), pl.BlockSpec(memory_space=pl.ANY), _resident((8, GM_FFN)),
        _resident((8, GM_HALF)), _resident((8, GM_HALF))]
    rows8 = [jnp.broadcast_to(r.reshape(1, -1), (8, r.size)) for r in (b_in, ln_g, ln_b)]
    args = [x, *prev, *mod, gain, w_in, *rows8]
    if layer == 0:
        assert vs_all is None
        vs_spec = pl.BlockSpec((N_GM, rows, GM_HALF), lambda i: (0, jnp.maximum(i - 1 - npt, 0), 0))
        aliases = {}
    else:
        in_specs.append(pl.BlockSpec(memory_space=pl.ANY))
        args.append(vs_all)
        vs_spec = pl.BlockSpec((1, rows, GM_HALF), lambda i: (layer, jnp.maximum(i - 1 - npt, 0), 0))
        aliases = {len(args) - 1: 2}
    return pl.pallas_call(
        functools.partial(_gm_proj_kernel, n, npt, layer),
        grid=(n + 1,),
        in_specs=in_specs,
        out_specs=[stage_a(_row_spec(rows, D)), stage_b(_row_spec(rows, GM_FFN)), vs_spec],
        out_shape=[jax.ShapeDtypeStruct((T_ALL, D), F32),
                   jax.ShapeDtypeStruct((T_ALL, GM_FFN), BF16),
                   jax.ShapeDtypeStruct((N_GM, T_SAMPLE, GM_HALF), F32)],
        scratch_shapes=[pltpu.VMEM((rows, D), BF16), pltpu.VMEM((rows, D), BF16),
                        pltpu.VMEM((rows, GM_HALF), BF16), pltpu.VMEM((rows, GM_HALF), BF16),
                        pltpu.VMEM((rows, 256), F32), pltpu.VMEM((rows, 256), F32),
                        *_weight_scratch(D, GM_FFN)],
        input_output_aliases=aliases,
        compiler_params=_params("arbitrary"),
        name="gm_proj",
    )(*args)


def _gm_out_kernel(n_prompt_tiles, layer, uv_ref, ws_ref, bs_ref, w_hbm, bo_ref, x_ref, modp_ref,
                   mods_ref, gain_ref, wr_ref, br_ref, xo_ref, h2_ref, lg_ref, w_s, w_stage, w_sems):
    _load_weight_bf16(w_hbm, layer, w_s, w_stage, w_sems)
    rows = uv_ref.shape[0]
    mod = _tile_mod((modp_ref, mods_ref), pl.program_id(0) < n_prompt_tiles)
    pieces = []
    for r0 in range(0, rows, GM_MIX):
        acc = jnp.zeros((GM_MIX, D), F32)
        for g in range(GM_GROUPS):
            lo = g * GM_GDIM
            sp = jnp.dot(ws_ref[0, g], uv_ref[r0:r0 + GM_MIX, GM_HALF + lo:GM_HALF + lo + GM_GDIM],
                         preferred_element_type=F32) + bs_ref[0, g]
            gated = (uv_ref[r0:r0 + GM_MIX, lo:lo + GM_GDIM].astype(F32) * sp).astype(BF16)
            acc = acc + jnp.dot(gated, w_s[lo:lo + GM_GDIM, :], preferred_element_type=F32)
        pieces.append(acc)
    acc = jnp.concatenate(pieces, axis=0) + bo_ref[...]
    _residual_router(acc, x_ref, mod, gain_ref, wr_ref, br_ref, xo_ref, h2_ref, lg_ref)


def _gm_block_diag(w_s, b_s):
    mats, biases = [], []
    for cl in (GM_CHUNK, DEC_SEQ):
        tri = jnp.tril(jnp.ones((cl, cl), bool))
        blk = jnp.where(tri[None], w_s[:, :cl, :cl], 0.0)
        reps = GM_MIX // cl
        eye = jnp.eye(reps, dtype=w_s.dtype)
        bd = jnp.einsum("ab,gts->gatbs", eye, blk).reshape(GM_GROUPS, GM_MIX, GM_MIX)
        mats.append(bd)
        biases.append(jnp.tile(b_s[:, :cl], (1, reps))[:, :, None])
    return jnp.stack(mats).astype(BF16), jnp.stack(biases).astype(F32)


def _gm_out_call(uv, ws_bd, bs_bd, w_out, layer, b_out, x, mod, gain, w_r, b_r):
    rows = ROWS_WIDE
    npt = _n_prompt_tiles(rows)

    def variant(i):
        return jnp.where(i >= npt, 1, 0)

    return pl.pallas_call(
        functools.partial(_gm_out_kernel, npt, layer),
        grid=(_n_tiles(rows),),
        in_specs=[_row_spec(rows, GM_FFN),
                  pl.BlockSpec((1, GM_GROUPS, GM_MIX, GM_MIX), lambda i: (variant(i), 0, 0, 0)),
                  pl.BlockSpec((1, GM_GROUPS, GM_MIX, 1), lambda i: (variant(i), 0, 0, 0)),
                  pl.BlockSpec(memory_space=pl.ANY), _resident((1, D)), _row_spec(rows, D),
                  *_mod_specs(rows), _resident((1, D))] + _router_specs(),
        out_specs=_mix_out_specs(rows),
        out_shape=_MIX_OUT_SHAPE,
        scratch_shapes=_weight_scratch(GM_HALF, D),
        compiler_params=_params("arbitrary"),
        name="gm_out",
    )(uv, ws_bd, bs_bd, w_out, b_out, x, *mod, gain, w_r, b_r)


def _expert_kernel(layer, be_ref, bv_ref, nb_ref, nx_ref, xb_ref, w1_hbm, w3_hbm, w2_hbm, yb_ref,
                   st1, st3, st2, w1s, w3s, w2s, sems, slot_ref):
    b = pl.program_id(0)

    def weight_copies(e, slot):
        return (pltpu.make_async_copy(w1_hbm.at[layer, e], st1.at[slot], sems.at[slot, 0]),
                pltpu.make_async_copy(w3_hbm.at[layer, e], st3.at[slot], sems.at[slot, 1]),
                pltpu.make_async_copy(w2_hbm.at[layer, e], st2.at[slot], sems.at[slot, 2]))

    @pl.when(b == 0)
    def _():
        slot_ref[0] = 0
        for cp in weight_copies(be_ref[0], 0):
            cp.start()

    @pl.when(b < nb_ref[0])
    def _():
        prev_e = be_ref[jnp.maximum(b - 1, 0)]

        @pl.when((b == 0) | (be_ref[b] != prev_e))
        def _():
            slot = slot_ref[0]
            for cp in weight_copies(be_ref[b], slot):
                cp.wait()
            w1s[...] = st1[slot].astype(BF16)
            w3s[...] = st3[slot].astype(BF16)
            w2s[...] = st2[slot].astype(BF16)

            @pl.when(nx_ref[b] >= 0)
            def _():
                for cp in weight_copies(nx_ref[b], 1 - slot):
                    cp.start(priority=1)

            slot_ref[0] = 1 - slot

        valid = bv_ref[b]

        def run_rows(n):
            row = lax.broadcasted_iota(jnp.int32, (n, 1), 0)
            xw = jnp.where(row < valid, xb_ref[0:n], U32(0))
            x = _unpack_bf16_pairs(xw).astype(BF16)
            a = jnp.dot(x, w1s[...], preferred_element_type=F32)
            c = jnp.dot(x, w3s[...], preferred_element_type=F32)
            h = (_silu(a) * c).astype(BF16)
            yb_ref[0:n] = _pack_bf16_pairs(jnp.dot(h, w2s[...], preferred_element_type=F32))
            if n < EXP_BLOCK:
                yb_ref[n:EXP_BLOCK] = jnp.zeros((EXP_BLOCK - n, DP), U32)

        for n in range(EXP_ROW_STEP, EXP_BLOCK + 1, EXP_ROW_STEP):
            pl.when((valid > n - EXP_ROW_STEP) & (valid <= n))(functools.partial(run_rows, n))


def _expert_call(blk_e, blk_valid, n_blk, blk_next, xb, w1, w3, w2, layer):
    def blk(b, be, bv, nb, nx):
        return (jnp.minimum(b, nb[0] - 1), 0)

    up, down = (D, MOE_HIDDEN), (MOE_HIDDEN, D)
    grid_spec = pltpu.PrefetchScalarGridSpec(
        num_scalar_prefetch=4,
        grid=(N_EXP_BLOCKS,),
        in_specs=[pl.BlockSpec((EXP_BLOCK, DP), blk)] + [pl.BlockSpec(memory_space=pl.ANY)] * 3,
        out_specs=pl.BlockSpec((EXP_BLOCK, DP), blk),
        scratch_shapes=[pltpu.VMEM((2,) + up, F32), pltpu.VMEM((2,) + up, F32), pltpu.VMEM((2,) + down, F32),
                        pltpu.VMEM(up, BF16), pltpu.VMEM(up, BF16), pltpu.VMEM(down, BF16),
                        pltpu.SemaphoreType.DMA((2, 3)), pltpu.SMEM((1,), jnp.int32)],
    )
    return pl.pallas_call(
        functools.partial(_expert_kernel, layer),
        grid_spec=grid_spec,
        out_shape=jax.ShapeDtypeStruct((P_ROWS, DP), U32),
        compiler_params=_params("arbitrary"),
        name="experts",
    )(blk_e, blk_valid, n_blk, blk_next, xb, w1, w3, w2)


def _route_kernel(lg_ref, dest_ref, gates_ref, meta_ref, cnt_ref):
    t = pl.program_id(0)

    @pl.when(t == 0)
    def _():
        cnt_ref[...] = jnp.zeros_like(cnt_ref)

    pl.when(t < N_ROUTE_TILES)(functools.partial(_route_tile, t, lg_ref, dest_ref, gates_ref, cnt_ref))
    pl.when(t == N_ROUTE_TILES)(functools.partial(_route_finish, dest_ref, meta_ref, cnt_ref))


ROUTE_RANK_BITS = 16
assert N_ASSIGN < 2 ** ROUTE_RANK_BITS


def _route_tile(t, lg_ref, dest_ref, gates_ref, cnt_ref):
    tm = ROUTE_TM
    lt = lg_ref[...].T
    el = lt[0:MOE_EXPERTS]
    gl = lt[MOE_EXPERTS:MOE_EXPERTS + 8]
    gidx = lax.broadcasted_iota(jnp.int32, (8, tm), 0)
    neg = jnp.float32(-jnp.inf)
    gl = jnp.where(gidx < MOE_GROUPS, gl, neg)
    gmax = jnp.max(gl, axis=0, keepdims=True)
    grp = jnp.min(jnp.where(gl == gmax, gidx, MOE_GROUPS), axis=0, keepdims=True)
    eidx = lax.broadcasted_iota(jnp.int32, (MOE_EXPERTS, tm), 0)
    els = jnp.where((eidx >> 3) == grp, el, neg)
    m1 = jnp.max(els, axis=0, keepdims=True)
    i1 = jnp.min(jnp.where(els == m1, eidx, MOE_EXPERTS), axis=0, keepdims=True)
    els2 = jnp.where(eidx == i1, neg, els)
    m2 = jnp.max(els2, axis=0, keepdims=True)
    i2 = jnp.min(jnp.where(els2 == m2, eidx, MOE_EXPERTS), axis=0, keepdims=True)
    sel1 = eidx == i1
    sel2 = eidx == i2
    cnt = jnp.where(sel1 | sel2, 1.0, 0.0)

    lane = ROUTER_LANES
    before = (lax.broadcasted_iota(jnp.int32, (lane, lane), 0)
              < lax.broadcasted_iota(jnp.int32, (lane, lane), 1))
    tri = jnp.where(before, 1.0, 0.0).astype(BF16)
    run = cnt_ref[...]
    r1, r2 = [], []
    for k in range(tm // lane):
        piece = slice(k * lane, (k + 1) * lane)
        ck = cnt[:, piece]
        pos = run + jnp.dot(ck.astype(BF16), tri, preferred_element_type=F32)
        r1.append(jnp.sum(jnp.where(sel1[:, piece], pos, 0.0), axis=0, keepdims=True))
        r2.append(jnp.sum(jnp.where(sel2[:, piece], pos, 0.0), axis=0, keepdims=True))
        run = run + jnp.sum(ck, axis=1, keepdims=True)
    cnt_ref[...] = run
    rank = jnp.concatenate([jnp.concatenate(r1, axis=1), jnp.concatenate(r2, axis=1)], axis=0)
    eid = jnp.concatenate([i1, i2], axis=0)
    dest_ref[:, pl.ds(pl.multiple_of(t * tm, tm), tm)] = (eid << ROUTE_RANK_BITS) + rank.astype(jnp.int32)

    g_w = 1.0 / jnp.sum(jnp.exp(gl - gmax), axis=0, keepdims=True)
    e21 = jnp.exp(m2 - m1)
    p1 = 1.0 / (1.0 + e21)
    rid = lax.broadcasted_iota(jnp.int32, (ROUTER_LANES, tm), 0)
    gt = jnp.where(rid == 0, g_w * p1, jnp.where(rid == 1, g_w * (e21 * p1), 0.0))
    gates_ref[...] = gt.T


def _route_finish(dest_ref, meta_ref, cnt_ref):
    counts = cnt_ref[...]
    nblk = jnp.floor((counts + (EXP_BLOCK - 1.0)) * (1.0 / EXP_BLOCK))
    r = lax.broadcasted_iota(jnp.int32, (MOE_EXPERTS, MOE_EXPERTS), 0)
    c = lax.broadcasted_iota(jnp.int32, (MOE_EXPERTS, MOE_EXPERTS), 1)
    nblk_row = jnp.sum(jnp.where(r == c, nblk, 0.0), axis=0, keepdims=True)
    bstart = jnp.sum(jnp.where(c < r, nblk_row, 0.0), axis=1, keepdims=True)
    bend = bstart + nblk
    bidx = lax.broadcasted_iota(jnp.int32, (1, META_LANES), 1).astype(F32)
    blk_e = jnp.minimum(jnp.sum(jnp.where(bidx >= bend, 1.0, 0.0), axis=0, keepdims=True),
                        MOE_EXPERTS - 1.0)
    erow = lax.broadcasted_iota(jnp.int32, (MOE_EXPERTS, META_LANES), 0).astype(F32)
    mine = erow == blk_e
    cnt_b = jnp.sum(jnp.where(mine, counts, 0.0), axis=0, keepdims=True)
    start_b = jnp.sum(jnp.where(mine, bstart, 0.0), axis=0, keepdims=True)
    valid = jnp.clip(cnt_b - (bidx - start_b) * EXP_BLOCK, 0.0, float(EXP_BLOCK))
    n_blk = jnp.sum(nblk, axis=0, keepdims=True)
    end_b = jnp.sum(jnp.where(mine, bend, 0.0), axis=0, keepdims=True)
    nxt = jnp.minimum(jnp.sum(jnp.where(end_b >= bend, 1.0, 0.0), axis=0, keepdims=True),
                      MOE_EXPERTS - 1.0)
    nxt = jnp.where(end_b < n_blk, nxt, -1.0)
    mrow = lax.broadcasted_iota(jnp.int32, (8, META_LANES), 0)
    meta = jnp.where(mrow == 0, blk_e, jnp.where(mrow == 1, valid, jnp.where(
        mrow == 2, n_blk, jnp.where(mrow == 3, nxt, 0.0))))
    meta_ref[...] = meta.astype(jnp.int32)

    base = (bstart * EXP_BLOCK).astype(jnp.int32)
    packed = dest_ref[...]
    eid = packed >> ROUTE_RANK_BITS
    row = packed & (2 ** ROUTE_RANK_BITS - 1)
    for e in range(MOE_EXPERTS):
        row = row + jnp.where(eid == e, base[e:e + 1, :], 0)
    dest_ref[...] = row


def _route_call(logits):
    last = N_ROUTE_TILES - 1
    return pl.pallas_call(
        _route_kernel,
        grid=(N_ROUTE_TILES + 1,),
        in_specs=[pl.BlockSpec((ROUTE_TM, ROUTER_LANES), lambda t: (jnp.minimum(t, last), 0))],
        out_specs=[pl.BlockSpec((MOE_TOPK, T_ALL), lambda t: (0, 0)),
                   pl.BlockSpec((ROUTE_TM, ROUTER_LANES), lambda t: (jnp.minimum(t, last), 0)),
                   pl.BlockSpec((8, META_LANES), lambda t: (0, 0))],
        out_shape=[jax.ShapeDtypeStruct((MOE_TOPK, T_ALL), jnp.int32),
                   jax.ShapeDtypeStruct((T_ALL, ROUTER_LANES), F32),
                   jax.ShapeDtypeStruct((8, META_LANES), jnp.int32)],
        scratch_shapes=[pltpu.VMEM((MOE_EXPERTS, 1), F32)],
        compiler_params=_params("arbitrary"),
        name="route",
    )(logits)


def _sc_mesh():
    return plsc.VectorSubcoreMesh(core_axis_name="c", subcore_axis_name="s")


def _sc_token_offset(j):
    wid = lax.axis_index("s") * SC_CORES + lax.axis_index("c")
    return pl.multiple_of(wid * SC_ROWS_PER_WORKER + j * SC_CHUNK, 8)


SC_N_CHUNKS = SC_ROWS_PER_WORKER // SC_CHUNK
assert SC_N_CHUNKS % 2 == 1


def _start(copies):
    for cp in copies:
        cp.start()


def _wait(copies):
    for cp in copies:
        cp.wait()


def _dispatch_body(h_hbm, d0_hbm, d1_hbm, out_hbm, i0a, i1a, rows_a, i0b, i1b, rows_b, la, lb, sa, sb):
    sets = {"a": (i0a, i1a, rows_a, la, sa), "b": (i0b, i1b, rows_b, lb, sb)}

    def loads(j, s):
        i0, i1, rows, lsem, _ = sets[s]
        src = pl.ds(_sc_token_offset(j), SC_CHUNK)
        return (pltpu.make_async_copy(d0_hbm.at[src], i0, lsem),
                pltpu.make_async_copy(d1_hbm.at[src], i1, lsem),
                pltpu.make_async_copy(h_hbm.at[src], rows, lsem))

    def scatters(s):
        i0, i1, rows, _, ssem = sets[s]
        return (pltpu.make_async_copy(rows, out_hbm.at[i0], ssem),
                pltpu.make_async_copy(rows, out_hbm.at[i1], ssem))

    _start(loads(0, "a"))

    @pl.loop(0, SC_N_CHUNKS // 2)
    def _(p):
        a = 2 * p
        _start(loads(a + 1, "b"))
        _wait(loads(a, "a"))
        _start(scatters("a"))
        _wait(loads(a + 1, "b"))
        _start(scatters("b"))
        _wait(scatters("a"))
        _start(loads(a + 2, "a"))
        _wait(scatters("b"))

    _wait(loads(SC_N_CHUNKS - 1, "a"))
    _start(scatters("a"))
    _wait(scatters("a"))


def _dispatch_call(h2, dest0, dest1):
    buffers = [pltpu.VMEM((SC_CHUNK,), jnp.int32), pltpu.VMEM((SC_CHUNK,), jnp.int32),
               pltpu.VMEM((SC_CHUNK, DP), U32)]
    return pl.kernel(
        _dispatch_body,
        out_type=jax.ShapeDtypeStruct((P_ROWS, DP), U32),
        mesh=_sc_mesh(),
        scratch_types=buffers + buffers + [pltpu.SemaphoreType.DMA] * 4,
        name="moe_dispatch",
    )(h2, dest0, dest1)


def _combine_body(yb_hbm, d0_hbm, d1_hbm, o0_hbm, o1_hbm, i0a, i1a, r0a, r1a, i0b, i1b, r0b, r1b,
                  ia, ib, ga, gb, wa, wb):
    sets = {"a": (i0a, i1a, r0a, r1a, ia, ga, wa), "b": (i0b, i1b, r0b, r1b, ib, gb, wb)}

    def index_loads(j, s):
        i0, i1, _, _, isem, _, _ = sets[s]
        src = pl.ds(_sc_token_offset(j), SC_CHUNK)
        return (pltpu.make_async_copy(d0_hbm.at[src], i0, isem), pltpu.make_async_copy(d1_hbm.at[src], i1, isem))

    def gathers(s):
        i0, i1, r0, r1, _, gsem, _ = sets[s]
        return (pltpu.make_async_copy(yb_hbm.at[i0], r0, gsem), pltpu.make_async_copy(yb_hbm.at[i1], r1, gsem))

    def writes(j, s):
        _, _, r0, r1, _, _, wsem = sets[s]
        dst = pl.ds(_sc_token_offset(j), SC_CHUNK)
        return (pltpu.make_async_copy(r0, o0_hbm.at[dst], wsem), pltpu.make_async_copy(r1, o1_hbm.at[dst], wsem))

    _start(index_loads(0, "a"))
    _wait(index_loads(0, "a"))
    _start(gathers("a"))

    @pl.loop(0, SC_N_CHUNKS // 2)
    def _(p):
        a = 2 * p
        _start(index_loads(a + 1, "b"))
        _wait(index_loads(a + 1, "b"))
        _wait(gathers("a"))
        _start(writes(a, "a"))
        _start(gathers("b"))
        _start(index_loads(a + 2, "a"))
        _wait(index_loads(a + 2, "a"))
        _wait(writes(a, "a"))
        _wait(gathers("b"))
        _start(writes(a + 1, "b"))
        _start(gathers("a"))
        _wait(writes(a + 1, "b"))

    _wait(gathers("a"))
    _start(writes(SC_N_CHUNKS - 1, "a"))
    _wait(writes(SC_N_CHUNKS - 1, "a"))


def _combine_call(yb, dest0, dest1):
    out = jax.ShapeDtypeStruct((T_ALL, DP), U32)
    buffers = [pltpu.VMEM((SC_CHUNK,), jnp.int32), pltpu.VMEM((SC_CHUNK,), jnp.int32),
               pltpu.VMEM((SC_CHUNK, DP), U32), pltpu.VMEM((SC_CHUNK, DP), U32)]
    return pl.kernel(
        _combine_body,
        out_type=(out, out),
        mesh=_sc_mesh(),
        scratch_types=buffers + buffers + [pltpu.SemaphoreType.DMA] * 6,
        name="moe_combine",
    )(yb, dest0, dest1)


def _moe_rows(h2, dest, meta, w1, w3, w2, layer):
    dest0, dest1 = dest[0], dest[1]
    xb = _dispatch_call(h2, dest0, dest1)
    yb = _expert_call(meta[0, :N_EXP_BLOCKS], meta[1, :N_EXP_BLOCKS], meta[2, :1], meta[3, :N_EXP_BLOCKS],
                      xb, w1, w3, w2, layer)
    return _combine_call(yb, dest0, dest1)


def _final_kernel(is_prompt, x_ref, yg0_ref, yg1_ref, gates_ref, modp_ref, mods_ref, gain_ref, o_ref):
    x = _add_moe(x_ref[...], yg0_ref, yg1_ref, gates_ref, _tile_mod((modp_ref, mods_ref), is_prompt))
    o_ref[...] = _rms(x) * gain_ref[...]


def _final_call(x, prev, gain, row0, n_rows):
    rows = ROWS_FINAL
    tile0 = row0 // rows
    is_prompt = row0 < T_PROMPT
    assert row0 + n_rows <= T_PROMPT or not is_prompt

    def tile(width):
        return pl.BlockSpec((rows, width), lambda i: (tile0 + i, 0))

    mod_specs = [pl.BlockSpec(s.block_shape, lambda i, m=s.index_map: m(tile0 + i)) for s in _mod_specs(rows)]
    return pl.pallas_call(
        functools.partial(_final_kernel, is_prompt),
        grid=(n_rows // rows,),
        in_specs=[tile(D), tile(DP), tile(DP), tile(ROUTER_LANES), *mod_specs, _resident((1, D))],
        out_specs=pl.BlockSpec((rows, D), lambda i: (i, 0)),
        out_shape=jax.ShapeDtypeStruct((n_rows, D), F32),
        compiler_params=_params("parallel"),
        name="final_norm",
    )(x, *prev, gain)


def _rope_table():
    pos = np.concatenate([np.tile(np.arange(SEQ), BATCH),
                          np.tile(PAST_LEN + np.arange(DEC_SEQ), DEC_BATCH)]).astype(np.float32)
    inv = (ROPE_BASE ** (-np.arange(ROPE_HALF, dtype=np.float32) / ROPE_HALF)).astype(np.float32)
    ang = (pos[:, None] * inv[None, :]).astype(np.float32).astype(np.float64)
    pad = np.zeros((T_ALL, ROPE_TABLE_LANES - 2 * ROPE_HALF - 2 * RET_HEADS))
    return jnp.asarray(np.concatenate([np.cos(ang), np.sin(ang), _ret_row_scales(), pad], axis=1), F32)


def kernel(x_prompt, x_sample, c_prompt, c_sample, state_ret, ada_w, ada_b, norm1_g, norm2_g, ret_w_in,
           ret_w_out, gm_w_in, gm_b_in, gm_ln_g, gm_ln_b, gm_w_s, gm_b_s, gm_w_out, gm_b_out, moe_w_rg,
           moe_b_rg, moe_w_re, moe_b_re, moe_w1, moe_w3, moe_w2, final_g):
    x = (x_prompt.reshape(T_PROMPT, D), x_sample.reshape(T_SAMPLE, D))
    c_all = jnp.concatenate([c_prompt, c_sample], axis=0)
    rope = _rope_table()

    mod_all = _ada_call(c_all, ada_w, ada_b).reshape(DEPTH, N_SEQ, 6, D)

    def layer_params(i):
        mod = (mod_all[i, :BATCH], mod_all[i, BATCH:])
        w_r = jnp.pad(jnp.concatenate([moe_w_re[i], moe_w_rg[i]], axis=1),
                      ((0, 0), (0, ROUTER_LANES - MOE_GROUPS - MOE_EXPERTS)))
        w_r_hi = w_r.astype(BF16)
        w_r_lo = (w_r - w_r_hi.astype(F32)).astype(BF16)
        w_r = jnp.concatenate([w_r_hi, w_r_lo], axis=1)
        b_r = jnp.pad(jnp.concatenate([moe_b_re[i].reshape(-1), moe_b_rg[i]]),
                      (0, ROUTER_LANES - MOE_GROUPS - MOE_EXPERTS)).reshape(1, ROUTER_LANES)
        return mod, w_r, b_r

    ret_prompt = ret_sample = gm_sample = None
    prev = None
    for i in range(DEPTH):
        j = i // 2
        mod, w_r, b_r = layer_params(i)
        g1 = norm1_g[i].reshape(1, D)
        g2 = norm2_g[i].reshape(1, D)
        if i % 2 == 0:
            x, p = _ret_proj_call(x, prev, mod, g1, ret_w_in, j, rope)
            y_p, ret_prompt = _ret_core_call(p, None, ret_prompt, j, BATCH, SEQ, RET_CHUNK_PROMPT, 0)
            y_s, ret_sample = _ret_core_call(p, state_ret, ret_sample, j, DEC_BATCH, DEC_SEQ,
                                             RET_CHUNK_SAMPLE, T_PROMPT)
            x, h2, logits = _ret_out_call(y_p, y_s, ret_w_out, j, x, mod, g2, w_r, b_r)
        else:
            x, uv, gm_sample = _gm_proj_call(x, prev, mod, g1, gm_w_in, j, gm_b_in[j], gm_ln_g[j],
                                             gm_ln_b[j], gm_sample)
            ws_bd, bs_bd = _gm_block_diag(gm_w_s[j], gm_b_s[j])
            x, h2, logits = _gm_out_call(uv, ws_bd, bs_bd, gm_w_out, j,
                                         gm_b_out[j].reshape(1, D), x, mod, g2, w_r, b_r)
        dest, gates, meta = _route_call(logits)
        yg0, yg1 = _moe_rows(h2, dest, meta, moe_w1, moe_w3, moe_w2, i)
        prev = (yg0, yg1, gates, *mod)

    fg = final_g.reshape(1, D)
    y_prompt = _final_call(x, prev, fg, 0, T_PROMPT).reshape(BATCH, SEQ, D)
    y_sample = _final_call(x, prev, fg, T_PROMPT, T_SAMPLE).reshape(DEC_BATCH, DEC_SEQ, D)
    return (y_prompt, y_sample, ret_prompt, ret_sample,
            gm_sample.reshape(N_GM, DEC_BATCH, DEC_SEQ, GM_HALF))
```
